```python
import math
import jax
import jax.numpy as jnp
from jax import lax
import numpy as np

D_MODEL = 1024
BATCH = 8
SEQ = 2048
DEPTH = 2

CHUNK = 64
Q_BLOCK = 128
RMS_EPS = 1e-6
MAX_POS_OFFSET = 4096

MIX_WIDTH = D_MODEL
GROUP_WIDTH = MIX_WIDTH // 4

CONV_A_WIDTH = 3
A_COLS = 3 * GROUP_WIDTH

FOX_HEAD_DIM = 64
FOX_HEADS = GROUP_WIDTH // FOX_HEAD_DIM
B_COLS = 3 * GROUP_WIDTH + FOX_HEADS

SSM_INNER = GROUP_WIDTH
SSM_HEAD_DIM = 64
SSM_HEADS = SSM_INNER // SSM_HEAD_DIM
SSM_GROUPS = 2
SSM_HEADS_PER_GROUP = SSM_HEADS // SSM_GROUPS
SSM_STATE = 64
SSM_CONV = 4
SSM_CONV_DIM = SSM_INNER + 2 * SSM_GROUPS * SSM_STATE
C_COLS = SSM_INNER + SSM_CONV_DIM + SSM_HEADS

MLA_HEADS = 4
MLA_NOPE = 64
MLA_ROPE = 32
MLA_V = GROUP_WIDTH // MLA_HEADS
MLA_Q_LORA = 256
MLA_KV_LORA = 128
ROPE_BASE = 10000.0
D_COLS = MLA_Q_LORA + MLA_KV_LORA + MLA_ROPE

IN_COLS = A_COLS + B_COLS + C_COLS + D_COLS
MIX_SPLITS = [A_COLS, A_COLS + B_COLS, A_COLS + B_COLS + C_COLS]

N_EXPERT_GROUPS = 4
EXPERTS_PER_GROUP = 8
N_EXPERTS = N_EXPERT_GROUPS * EXPERTS_PER_GROUP
TOP_K = 2
EXPERT_FF = 256
MOE_BLOCK = 128

kernel_name = "hybrid_chunk_causal_block"


def rms_norm(x, g):
    xf = x.astype(jnp.float32)
    y = xf * lax.rsqrt(jnp.mean(xf * xf, axis=-1, keepdims=True) + RMS_EPS)
    return (y * g.astype(jnp.float32)).astype(x.dtype)


def causal_depthwise_conv(x, w):
    k, c = w.shape
    return lax.conv_general_dilated(
        x, w.astype(x.dtype)[:, None, :], window_strides=(1,), padding=[(k - 1, 0)],
        dimension_numbers=('NWC', 'WIO', 'NWC'), feature_group_count=c)


def split_heads(t, n):
    b, s, _ = t.shape
    return t.reshape(b, s, n, -1).transpose(0, 2, 1, 3)


def merge_heads(t):
    b, h, s, d = t.shape
    return t.transpose(0, 2, 1, 3).reshape(b, s, h * d)


def frame_causal(tq, tk):
    return tk <= tq


def chunk_causal(tq, tk):
    return (tk // CHUNK) <= (tq // CHUNK)


def block_attention(q, k, v, allowed, log_decay=None):
    seq = q.shape[2]
    scale = q.shape[-1] ** -0.5
    outs = []
    for i in range(seq // Q_BLOCK):
        q0, q1 = i * Q_BLOCK, (i + 1) * Q_BLOCK
        s = jnp.einsum('bhqd,bhkd->bhqk', q[:, :, q0:q1], k[:, :, :q1]).astype(jnp.float32) * scale
        if log_decay is not None:
            s = s + (log_decay[:, :, q0:q1, None] - log_decay[:, :, None, :q1])
        tq = jnp.arange(q0, q1)[:, None]
        tk = jnp.arange(q1)[None, :]
        s = jnp.where(allowed(tq, tk), s, -jnp.inf)
        p = jax.nn.softmax(s, axis=-1).astype(v.dtype)
        outs.append(jnp.einsum('bhqk,bhkd->bhqd', p, v[:, :, :q1]))
    return jnp.concatenate(outs, axis=2)


def short_conv_mixer(p, conv_w):
    b_gate, c_gate, val = jnp.split(p, 3, axis=-1)
    return b_gate * causal_depthwise_conv(c_gate * val, conv_w)


def forgetting_attention(p, forget_bias):
    q, k, v, f_logit = jnp.split(p, [GROUP_WIDTH, 2 * GROUP_WIDTH, 3 * GROUP_WIDTH], axis=-1)
    log_f = jax.nn.log_sigmoid((f_logit + forget_bias).astype(jnp.float32))
    cum = jnp.cumsum(log_f, axis=1).transpose(0, 2, 1)
    out = block_attention(split_heads(q, FOX_HEADS), split_heads(k, FOX_HEADS),
                          split_heads(v, FOX_HEADS), frame_causal, cum)
    return merge_heads(out)


def segsum(a):
    t = a.shape[-1]
    ae = jnp.broadcast_to(a[..., None], a.shape + (t,))
    ae = jnp.where(jnp.tril(jnp.ones((t, t), bool), -1), ae, 0.0)
    ss = jnp.cumsum(ae, axis=-2)
    return jnp.where(jnp.tril(jnp.ones((t, t), bool), 0), ss, -jnp.inf)


def ssd_chunked(x, a, b, c):
    bsz, seq, h, pdim = x.shape
    n = b.shape[-1]
    nc = seq // CHUNK
    x = x.reshape(bsz, nc, CHUNK, h, pdim).astype(jnp.float32)
    b = b.reshape(bsz, nc, CHUNK, h, n).astype(jnp.float32)
    c = c.reshape(bsz, nc, CHUNK, h, n).astype(jnp.float32)
    a = a.reshape(bsz, nc, CHUNK, h).astype(jnp.float32).transpose(0, 3, 1, 2)
    a_cum = jnp.cumsum(a, axis=-1)
    decay_in = jnp.exp(segsum(a))
    y_diag = jnp.einsum('bclhn,bcshn,bhcls,bcshp->bclhp', c, b, decay_in, x)
    decay_states = jnp.exp(a_cum[..., -1:] - a_cum)
    states = jnp.einsum('bclhn,bhcl,bclhp->bchpn', b, decay_states, x)
    chunk_decay = jnp.exp(a_cum[..., -1])

    def step(carry, inp):
        st, dec = inp
        return carry * dec[..., None, None] + st, carry

    init = jnp.zeros((bsz, h, pdim, n), jnp.float32)
    _, prev = lax.scan(step, init, (states.transpose(1, 0, 2, 3, 4), chunk_decay.transpose(2, 0, 1)))
    prev = prev.transpose(1, 0, 2, 3, 4)
    y_off = jnp.einsum('bclhn,bchpn,bhcl->bclhp', c, prev, jnp.exp(a_cum))
    return (y_diag + y_off).reshape(bsz, seq, h, pdim)


def ssd_mixer(p, conv_w, conv_b, dt_bias, a_log, d_skip, norm_g):
    z, xbc, dt_raw = jnp.split(p, [SSM_INNER, SSM_INNER + SSM_CONV_DIM], axis=-1)
    xbc = jax.nn.silu(causal_depthwise_conv(xbc, conv_w) + conv_b)
    xs, bs, cs = jnp.split(xbc, [SSM_INNER, SSM_INNER + SSM_GROUPS * SSM_STATE], axis=-1)
    bsz, seq, _ = p.shape
    xs = xs.reshape(bsz, seq, SSM_HEADS, SSM_HEAD_DIM)
    bs = jnp.repeat(bs.reshape(bsz, seq, SSM_GROUPS, SSM_STATE), SSM_HEADS_PER_GROUP, axis=2)
    cs = jnp.repeat(cs.reshape(bsz, seq, SSM_GROUPS, SSM_STATE), SSM_HEADS_PER_GROUP, axis=2)
    dt = jax.nn.softplus((dt_raw + dt_bias).astype(jnp.float32))
    a = -jnp.exp(a_log.astype(jnp.float32))
    y = ssd_chunked(xs * dt[..., None], dt * a, bs, cs)
    y = y + d_skip.astype(jnp.float32)[:, None] * xs.astype(jnp.float32)
    y = y.reshape(bsz, seq, SSM_INNER).astype(p.dtype) * jax.nn.silu(z)
    return rms_norm(y, norm_g)


def rotary_cos_sin(positions, dim):
    inv_freq = ROPE_BASE ** (-jnp.arange(0, dim, 2, dtype=jnp.float32) / dim)
    ang = positions.astype(jnp.float32)[..., None] * inv_freq
    return jnp.cos(ang), jnp.sin(ang)


def apply_rotary(x, cos, sin):
    half = x.shape[-1] // 2
    xf = x.astype(jnp.float32)
    x1, x2 = xf[..., :half], xf[..., half:]
    return jnp.concatenate([x1 * cos - x2 * sin, x2 * cos + x1 * sin], axis=-1).astype(x.dtype)


def latent_attention(p, positions, q_norm, kv_norm, w_uq, w_ukv):
    bsz, seq, _ = p.shape
    c_q, c_kv, k_rope = jnp.split(p, [MLA_Q_LORA, MLA_Q_LORA + MLA_KV_LORA], axis=-1)
    q = (rms_norm(c_q, q_norm) @ w_uq).reshape(bsz, seq, MLA_HEADS, MLA_NOPE + MLA_ROPE)
    kv = (rms_norm(c_kv, kv_norm) @ w_ukv).reshape(bsz, seq, MLA_HEADS, MLA_NOPE + MLA_V)
    q_nope, q_rope = jnp.split(q, [MLA_NOPE], axis=-1)
    k_nope, v = jnp.split(kv, [MLA_NOPE], axis=-1)
    cos, sin = rotary_cos_sin(positions, MLA_ROPE)
    q_rope = apply_rotary(q_rope, cos[:, :, None, :], sin[:, :, None, :])
    k_rope = apply_rotary(k_rope, cos, sin)
    q = jnp.concatenate([q_nope, q_rope], axis=-1).transpose(0, 2, 1, 3)
    k = jnp.concatenate([k_nope, jnp.broadcast_to(k_rope[:, :, None, :], (bsz, seq, MLA_HEADS, MLA_ROPE))],
                        axis=-1).transpose(0, 2, 1, 3)
    out = block_attention(q, k, v.transpose(0, 2, 1, 3), chunk_causal)
    return merge_heads(out)


def grouped_expert_ffn(xf, expert_idx, weights, w_gate, w_up, w_down):
    t, d = xf.shape
    n_exp = w_gate.shape[0]
    n_slots = expert_idx.size
    n_blocks = -(-(n_slots + n_exp * (MOE_BLOCK - 1)) // MOE_BLOCK)
    n_rows = n_blocks * MOE_BLOCK
    flat_e = expert_idx.reshape(-1)
    flat_w = weights.reshape(-1)
    order = jnp.argsort(flat_e)
    sorted_e = flat_e[order]
    counts = jnp.zeros((n_exp,), jnp.int32).at[flat_e].add(1)
    padded = (counts + MOE_BLOCK - 1) // MOE_BLOCK * MOE_BLOCK
    start = jnp.cumsum(counts) - counts
    padded_end = jnp.cumsum(padded)
    padded_start = padded_end - padded
    dest = padded_start[sorted_e] + jnp.arange(n_slots, dtype=jnp.int32) - start[sorted_e]
    row_token = jnp.full((n_rows,), t, jnp.int32).at[dest].set((order // TOP_K).astype(jnp.int32))
    row_w = jnp.zeros((n_rows,), xf.dtype).at[dest].set(flat_w[order])
    block_expert = jnp.minimum(
        jnp.searchsorted(padded_end, jnp.arange(n_blocks, dtype=jnp.int32) * MOE_BLOCK, side='right'),
        n_exp - 1)
    x_rows = jnp.concatenate([xf, jnp.zeros((1, d), xf.dtype)], axis=0)[row_token]
    x_rows = x_rows.reshape(n_blocks, MOE_BLOCK, d)

    def expert_block(args):
        xb, e = args
        return (jax.nn.silu(xb @ w_gate[e]) * (xb @ w_up[e])) @ w_down[e]

    y_rows = lax.map(expert_block, (x_rows, block_expert)).reshape(n_rows, d)
    return jax.ops.segment_sum(y_rows * row_w[:, None], row_token, num_segments=t + 1)[:t]


def hierarchical_moe(h, rg_w, rg_b, re_w, re_b, w_gate, w_up, w_down):
    bsz, seq, d = h.shape
    t = bsz * seq
    xf = h.reshape(t, d)
    g_prob = jax.nn.softmax((xf @ rg_w + rg_b).astype(jnp.float32), axis=-1)
    g_w, g_idx = lax.top_k(g_prob, 1)
    e_logits = (xf @ re_w + re_b).astype(jnp.float32).reshape(t, N_EXPERT_GROUPS, EXPERTS_PER_GROUP)
    e_logits = jnp.take_along_axis(e_logits, g_idx[:, :, None], axis=1)[:, 0]
    e_prob = jax.nn.softmax(e_logits, axis=-1)
    e_w, e_local = lax.top_k(e_prob, TOP_K)
    e_w = e_w / jnp.sum(e_w, axis=-1, keepdims=True)
    weights = (g_w * e_w).astype(h.dtype)
    expert_idx = g_idx * EXPERTS_PER_GROUP + e_local
    y = grouped_expert_ffn(xf, expert_idx, weights, w_gate, w_up, w_down)
    return y.reshape(bsz, seq, d)


def setup_inputs(seed: int = 0) -> dict:
    key = jax.random.key(seed)
    ks = jax.random.split(key, 28)
    f32 = jnp.float32

    def nrm(k, shape, fan_in):
        return jax.random.normal(k, shape, f32) * fan_in ** -0.5

    def gain(k, shape):
        return 1.0 + 0.02 * jax.random.normal(k, shape, f32)

    x = jax.random.normal(ks[0], (BATCH, SEQ, D_MODEL), f32)
    offsets = jax.random.randint(ks[1], (BATCH, 1), 0, MAX_POS_OFFSET, jnp.int32)
    positions = offsets + jnp.arange(SEQ, dtype=jnp.int32)[None, :]
    dt_init = jnp.exp(jax.random.uniform(ks[9], (DEPTH, SSM_HEADS), f32, math.log(1e-3), math.log(1e-1)))
    return {
        'x': x,
        'positions': positions,
        'norm_mix': gain(ks[2], (DEPTH, D_MODEL)),
        'w_in': nrm(ks[3], (DEPTH, D_MODEL, IN_COLS), D_MODEL),
        'conv_a': nrm(ks[4], (DEPTH, CONV_A_WIDTH, GROUP_WIDTH), CONV_A_WIDTH),
        'fox_forget_bias': jax.random.uniform(ks[5], (DEPTH, FOX_HEADS), f32, 1.0, 4.0),
        'ssm_conv_w': nrm(ks[6], (DEPTH, SSM_CONV, SSM_CONV_DIM), SSM_CONV),
        'ssm_conv_b': 0.02 * jax.random.normal(ks[7], (DEPTH, SSM_CONV_DIM), f32),
        'ssm_dt_bias': dt_init + jnp.log(-jnp.expm1(-dt_init)),
        'ssm_a_log': jnp.log(jax.random.uniform(ks[10], (DEPTH, SSM_HEADS), f32, 1.0, 16.0)),
        'ssm_d': 1.0 + 0.1 * jax.random.normal(ks[11], (DEPTH, SSM_HEADS), f32),
        'ssm_norm': gain(ks[12], (DEPTH, SSM_INNER)),
        'mla_q_norm': gain(ks[13], (DEPTH, MLA_Q_LORA)),
        'mla_kv_norm': gain(ks[14], (DEPTH, MLA_KV_LORA)),
        'mla_w_uq': nrm(ks[15], (DEPTH, MLA_Q_LORA, MLA_HEADS * (MLA_NOPE + MLA_ROPE)), MLA_Q_LORA),
        'mla_w_ukv': nrm(ks[16], (DEPTH, MLA_KV_LORA, MLA_HEADS * (MLA_NOPE + MLA_V)), MLA_KV_LORA),
        'w_out': nrm(ks[17], (DEPTH, MIX_WIDTH, D_MODEL), MIX_WIDTH),
        'norm_ffn': gain(ks[18], (DEPTH, D_MODEL)),
        'router_group_w': nrm(ks[19], (DEPTH, D_MODEL, N_EXPERT_GROUPS), D_MODEL),
        'router_group_b': 0.01 * jax.random.normal(ks[20], (DEPTH, N_EXPERT_GROUPS), f32),
        'router_expert_w': nrm(ks[21], (DEPTH, D_MODEL, N_EXPERTS), D_MODEL),
        'router_expert_b': 0.01 * jax.random.normal(ks[22], (DEPTH, N_EXPERTS), f32),
        'expert_w_gate': nrm(ks[23], (DEPTH, N_EXPERTS, D_MODEL, EXPERT_FF), D_MODEL),
        'expert_w_up': nrm(ks[24], (DEPTH, N_EXPERTS, D_MODEL, EXPERT_FF), D_MODEL),
        'expert_w_down': nrm(ks[25], (DEPTH, N_EXPERTS, EXPERT_FF, D_MODEL), EXPERT_FF),
        'norm_final': gain(ks[26], (D_MODEL,)),
    }


def reference(x, positions, norm_mix, w_in, conv_a, fox_forget_bias, ssm_conv_w, ssm_conv_b,
              ssm_dt_bias, ssm_a_log, ssm_d, ssm_norm, mla_q_norm, mla_kv_norm, mla_w_uq, mla_w_ukv,
              w_out, norm_ffn, router_group_w, router_group_b, router_expert_w, router_expert_b,
              expert_w_gate, expert_w_up, expert_w_down, norm_final):
    for l in range(DEPTH):
        h = rms_norm(x, norm_mix[l])
        proj = h @ w_in[l]
        p_a, p_b, p_c, p_d = jnp.split(proj, MIX_SPLITS, axis=-1)
        y = jnp.concatenate([
            short_conv_mixer(p_a, conv_a[l]),
            forgetting_attention(p_b, fox_forget_bias[l]),
            ssd_mixer(p_c, ssm_conv_w[l], ssm_conv_b[l], ssm_dt_bias[l], ssm_a_log[l], ssm_d[l], ssm_norm[l]),
            latent_attention(p_d, positions, mla_q_norm[l], mla_kv_norm[l], mla_w_uq[l], mla_w_ukv[l]),
        ], axis=-1)
        x = x + y @ w_out[l]
        h = rms_norm(x, norm_ffn[l])
        x = x + hierarchical_moe(h, router_group_w[l], router_group_b[l], router_expert_w[l],
                                 router_expert_b[l], expert_w_gate[l], expert_w_up[l], expert_w_down[l])
    return rms_norm(x, norm_final)
```

```python
import functools
import math

import jax
import jax.numpy as jnp
import numpy as np
from jax import lax
from jax.experimental import pallas as pl
from jax.experimental.pallas import tpu as pltpu

F32 = jnp.float32
BF16 = jnp.bfloat16
I32 = jnp.int32

LANES = 128
VMEM_LIMIT_BYTES = 56 * 1024 * 1024

D_MODEL = 1024
RMS_EPS = 1e-6
GROUP_WIDTH = 256
HEAD_DIM = 64
N_HEADS = 4

CONV_A_WIDTH = 3
SSM_CONV = 4
SSM_STATE = 64
SSM_CHUNK = 256

MLA_NOPE = 64
MLA_ROPE = 32
MLA_Q_LORA = 256
MLA_KV_LORA = 128
ROPE_BASE = 10000.0
MLA_CHUNK = 64

N_EXPERT_GROUPS = 4
EXPERTS_PER_GROUP = 8
N_EXPERTS = 32
EXPERT_FF = 256
MOE_ROWS = 256

SEG_A = (0, 768)
SEG_B = (768, 1536)
SEG_C = (1536, 2304)
SEG_D = (2304, 2688)
SEG_M = (2688, 2816)
SEG_M2 = (2816, 2944)
IN_COLS_PADDED = 2944
MISC_F = 0
MISC_DT = 4
MISC_ROPE = 64
COL_CUMF = 0
COL_DT = 4
COL_ACUM = 8
N_SCALAR_ROWS = 16


def _cparams(*sem):
    return pltpu.CompilerParams(dimension_semantics=sem, vmem_limit_bytes=VMEM_LIMIT_BYTES)


def _lane_iota(shape):
    return lax.broadcasted_iota(I32, shape, len(shape) - 1)


def _row_iota(shape):
    return lax.broadcasted_iota(I32, shape, 0)


def _rms(x, g):
    ms = jnp.mean(x * x, axis=-1, keepdims=True)
    return x * lax.rsqrt(ms + RMS_EPS) * g


def _silu(x):
    return x / (1.0 + jnp.exp(-x))


def _softplus(x):
    return jnp.maximum(x, 0.0) + jnp.log(1.0 + jnp.exp(-jnp.abs(x)))


def _shift_rows(x, k):
    rolled = pltpu.roll(x, k, 0)
    return jnp.where(_row_iota(x.shape) >= k, rolled, 0.0)


def _rope_kernel(pos_ref, freq_ref, cos_ref, sin_ref):
    ang = pos_ref[...] * freq_ref[...]
    lane = _lane_iota(ang.shape)
    rope = (lane >= MISC_ROPE) & (lane < MISC_ROPE + MLA_ROPE)
    cos_ref[...] = jnp.where(rope, jnp.cos(ang), jnp.where(lane < MISC_ROPE, 1.0, 0.0))
    sin_ref[...] = jnp.where(rope, jnp.sin(ang), 0.0)


def _rope_tables(pos_col, tm):
    t = pos_col.shape[0]
    half = MLA_ROPE // 2
    inv = ROPE_BASE ** (-np.arange(0, MLA_ROPE, 2, dtype=np.float32) / MLA_ROPE)
    freq = np.zeros((1, LANES), np.float32)
    freq[0, MISC_ROPE:MISC_ROPE + half] = inv
    freq[0, MISC_ROPE + half:MISC_ROPE + MLA_ROPE] = inv
    return pl.pallas_call(
        _rope_kernel,
        grid=(t // tm,),
        in_specs=[pl.BlockSpec((tm, 1), lambda i: (i, 0)),
                  pl.BlockSpec((1, LANES), lambda i: (0, 0))],
        out_specs=[pl.BlockSpec((tm, LANES), lambda i: (i, 0))] * 2,
        out_shape=[jax.ShapeDtypeStruct((t, LANES), F32)] * 2,
        compiler_params=_cparams("parallel"),
        name="rope_tables",
    )(pos_col, jnp.asarray(freq))


def _inproj_kernel(x_ref, g_ref, w_ref, oa, ob, oc, od, om, om2):
    h = _rms(x_ref[...], g_ref[...]).astype(BF16)
    for o, (lo, hi) in ((oa, SEG_A), (ob, SEG_B), (oc, SEG_C), (od, SEG_D), (om, SEG_M), (om2, SEG_M2)):
        o[...] = jnp.dot(h, w_ref[:, lo:hi], preferred_element_type=F32).astype(o.dtype)


def _inproj(x, g, w, tm):
    t = x.shape[0]
    widths = [(s[1] - s[0]) for s in (SEG_A, SEG_B, SEG_C, SEG_D, SEG_M, SEG_M2)]
    dtypes = [BF16, BF16, BF16, BF16, F32, F32]
    return pl.pallas_call(
        _inproj_kernel,
        grid=(t // tm,),
        in_specs=[pl.BlockSpec((tm, D_MODEL), lambda i: (i, 0)),
                  pl.BlockSpec((1, D_MODEL), lambda i: (0, 0)),
                  pl.BlockSpec((D_MODEL, IN_COLS_PADDED), lambda i: (0, 0))],
        out_specs=[pl.BlockSpec((tm, wd), lambda i: (i, 0)) for wd in widths],
        out_shape=[jax.ShapeDtypeStruct((t, wd), dt) for wd, dt in zip(widths, dtypes)],
        compiler_params=_cparams("parallel"),
        name="inproj",
    )(x, g, w)


def _scalar_prep_kernel(m_ref, p_ref, col_ref, row_ref):
    s = m_ref.shape[0]
    m = m_ref[...]
    bias = p_ref[0:1, :]
    a_log = p_ref[1:2, :]
    lane = _lane_iota(m.shape)
    z = m + bias
    logf = jnp.minimum(z, 0.0) - jnp.log(1.0 + jnp.exp(-jnp.abs(z)))
    dt = _softplus(z)
    a = dt * (-jnp.exp(a_log))
    is_f = lane < MISC_DT
    is_dt = (lane >= MISC_DT) & (lane < MISC_DT + N_HEADS)
    v = jnp.where(is_f, logf, jnp.where(is_dt, a, 0.0))
    r = _row_iota((SSM_CHUNK, SSM_CHUNK))
    c = _lane_iota((SSM_CHUNK, SSM_CHUNK))
    tril = jnp.where(r >= c, 1.0, 0.0).astype(F32)
    carry = jnp.zeros((1, LANES), F32)
    lane_1 = _lane_iota((1, LANES))
    lane_b = _lane_iota((SSM_CHUNK, LANES))
    for ci in range(s // SSM_CHUNK):
        blk = v[ci * SSM_CHUNK:(ci + 1) * SSM_CHUNK]
        cs = jnp.dot(tril, blk, preferred_element_type=F32, precision=lax.Precision.HIGHEST)
        cs = cs + jnp.where(lane_1 < MISC_DT, carry, 0.0)
        carry = cs[SSM_CHUNK - 1:SSM_CHUNK]
        acum = pltpu.roll(cs, COL_ACUM - MISC_DT, 1)
        out = jnp.where(lane_b < MISC_DT, cs,
                        jnp.where(lane_b < COL_ACUM, dt[ci * SSM_CHUNK:(ci + 1) * SSM_CHUNK],
                                  jnp.where(lane_b < COL_ACUM + N_HEADS, acum, 0.0)))
        col_ref[ci * SSM_CHUNK:(ci + 1) * SSM_CHUNK, :] = out
        row_ref[0, :, ci * SSM_CHUNK:(ci + 1) * SSM_CHUNK] = out.T[:N_SCALAR_ROWS]


def _scalar_prep(misc, params, batch, seq):
    return pl.pallas_call(
        _scalar_prep_kernel,
        grid=(batch,),
        in_specs=[pl.BlockSpec((seq, LANES), lambda b: (b, 0)),
                  pl.BlockSpec((8, LANES), lambda b: (0, 0))],
        out_specs=[pl.BlockSpec((seq, LANES), lambda b: (b, 0)),
                   pl.BlockSpec((1, N_SCALAR_ROWS, seq), lambda b: (b, 0, 0))],
        out_shape=[jax.ShapeDtypeStruct((batch * seq, LANES), F32),
                   jax.ShapeDtypeStruct((batch, N_SCALAR_ROWS, seq), F32)],
        compiler_params=_cparams("parallel"),
        name="scalar_prep",
    )(misc, params)


def _conv_mixer_kernel(p_ref, w_ref, o_ref):
    gw = GROUP_WIDTH
    b_gate = p_ref[:, 0:gw].astype(F32)
    cv = p_ref[:, gw:2 * gw].astype(F32) * p_ref[:, 2 * gw:3 * gw].astype(F32)
    acc = cv * w_ref[CONV_A_WIDTH - 1:CONV_A_WIDTH, :]
    for k in range(1, CONV_A_WIDTH):
        acc = acc + _shift_rows(cv, k) * w_ref[CONV_A_WIDTH - 1 - k:CONV_A_WIDTH - k, :]
    o_ref[...] = (b_gate * acc).astype(o_ref.dtype)


def _conv_mixer(pa, w, batch, seq):
    return pl.pallas_call(
        _conv_mixer_kernel,
        grid=(batch,),
        in_specs=[pl.BlockSpec((seq, 3 * GROUP_WIDTH), lambda b: (b, 0)),
                  pl.BlockSpec((8, GROUP_WIDTH), lambda b: (0, 0))],
        out_specs=pl.BlockSpec((seq, GROUP_WIDTH), lambda b: (b, 0)),
        out_shape=jax.ShapeDtypeStruct((batch * seq, GROUP_WIDTH), BF16),
        compiler_params=_cparams("parallel"),
        name="conv_mixer",
    )(pa, w)


def _pair_lanes(col, base, shape):
    lane = _lane_iota(shape)
    return jnp.where(lane < HEAD_DIM, col[:, base:base + 1], col[:, base + 1:base + 2])


def _ssd_kernel(p_ref, col_ref, row_ref, cw_ref, par_ref, o_ref, u_ref):
    s = p_ref.shape[0]
    q = SSM_CHUNK
    gw = GROUP_WIDTH
    xbc = p_ref[:, gw:3 * gw].astype(F32)
    acc = xbc * cw_ref[SSM_CONV - 1:SSM_CONV, :]
    for k in range(1, SSM_CONV):
        acc = acc + _shift_rows(xbc, k) * cw_ref[SSM_CONV - 1 - k:SSM_CONV - k, :]
    u_ref[...] = _silu(acc + cw_ref[SSM_CONV:SSM_CONV + 1, :])

    d_skip = par_ref[0:1, :]
    norm_g = par_ref[1:2, :]
    lane_q = _lane_iota((q, LANES))
    low = lane_q < HEAD_DIM
    tri = _row_iota((q, q)) >= _lane_iota((q, q))

    def chunk(ci, states):
        r0 = pl.multiple_of(ci * q, q)
        rows = pl.ds(r0, q)
        u = u_ref[rows, :]
        col = col_ref[rows, :]
        bm = u[:, gw:gw + LANES]
        cm = u[:, gw + LANES:gw + 2 * LANES]
        z = p_ref[rows, 0:gw].astype(F32)
        new_states = []
        ys = []
        for g in range(2):
            sel = low if g == 0 else jnp.logical_not(low)
            cg = jnp.where(sel, cm, 0.0).astype(BF16)
            bg = jnp.where(sel, bm, 0.0)
            gmat = lax.dot_general(cg, bm.astype(BF16), (((1,), (1,)), ((), ())),
                                   preferred_element_type=F32)
            xs = u[:, g * LANES:(g + 1) * LANES]
            dt2 = _pair_lanes(col, COL_DT + 2 * g, (q, LANES))
            ac2 = _pair_lanes(col, COL_ACUM + 2 * g, (q, LANES))
            xdt = xs * dt2
            xdt_b = xdt.astype(BF16)
            st = states[g]
            y_off = jnp.dot(cg, st.astype(BF16), preferred_element_type=F32) * jnp.exp(ac2)
            halves = []
            for hh in range(2):
                h = 2 * g + hh
                ac_col = col[:, COL_ACUM + h:COL_ACUM + h + 1]
                ac_row = row_ref[0, COL_ACUM + h:COL_ACUM + h + 1, rows]
                decay = jnp.exp(jnp.where(tri, ac_col - ac_row, -1e30))
                mm = (gmat * decay).astype(BF16)
                halves.append(jnp.dot(mm, xdt_b, preferred_element_type=F32))
            y = jnp.where(low, halves[0], halves[1]) + y_off + d_skip[:, g * LANES:(g + 1) * LANES] * xs
            ys.append(y)
            ac_last = ac2[q - 1:q, :]
            w_end = jnp.exp(ac_last - ac2)
            xw = (xdt * w_end).astype(BF16)
            upd = jnp.dot(bg.T.astype(BF16), xw, preferred_element_type=F32)
            new_states.append(st * jnp.exp(ac_last) + upd)
        yfull = jnp.concatenate(ys, axis=1) * _silu(z)
        o_ref[rows, :] = _rms(yfull, norm_g).astype(o_ref.dtype)
        return tuple(new_states)

    init = (jnp.zeros((LANES, LANES), F32), jnp.zeros((LANES, LANES), F32))
    lax.fori_loop(0, s // q, chunk, init)


def _ssd_mixer(pc, col, rows, conv_wb, par, batch, seq):
    gw = GROUP_WIDTH
    return pl.pallas_call(
        _ssd_kernel,
        grid=(batch,),
        in_specs=[pl.BlockSpec((seq, 3 * gw), lambda b: (b, 0)),
                  pl.BlockSpec((seq, LANES), lambda b: (b, 0)),
                  pl.BlockSpec((1, N_SCALAR_ROWS, seq), lambda b: (b, 0, 0)),
                  pl.BlockSpec((8, 2 * gw), lambda b: (0, 0)),
                  pl.BlockSpec((8, gw), lambda b: (0, 0))],
        out_specs=pl.BlockSpec((seq, gw), lambda b: (b, 0)),
        out_shape=jax.ShapeDtypeStruct((batch * seq, gw), BF16),
        scratch_shapes=[pltpu.VMEM((seq, 2 * gw), F32)],
        compiler_params=_cparams("parallel"),
        name="ssd_mixer",
    )(pc, col, rows, conv_wb, par)


def _mla_prep_kernel(pd_ref, m_ref, m2_ref, cos_ref, sin_ref, nq_ref, nkv_ref,
                     wq_ref, wqs_ref, wk_ref, wv_ref, q_ref, k_ref, v_ref):
    cq = _rms(pd_ref[:, 0:MLA_Q_LORA].astype(F32), nq_ref[...]).astype(BF16)
    ckv = _rms(pd_ref[:, MLA_Q_LORA:MLA_Q_LORA + MLA_KV_LORA].astype(F32), nkv_ref[...]).astype(BF16)
    cos = cos_ref[...]
    sin = sin_ref[...]
    cos4 = jnp.concatenate([cos] * N_HEADS, axis=1)
    sin4 = jnp.concatenate([sin] * N_HEADS, axis=1)
    scale = (MLA_NOPE + MLA_ROPE) ** -0.5
    q = jnp.dot(cq, wq_ref[...], preferred_element_type=F32)
    qs = jnp.dot(cq, wqs_ref[...], preferred_element_type=F32)
    q_ref[...] = ((q * cos4 + qs * sin4) * scale).astype(q_ref.dtype)
    lane = _lane_iota(cos.shape)
    rope = (lane >= MISC_ROPE) & (lane < MISC_ROPE + MLA_ROPE)
    kr = jnp.where(rope, m_ref[...] * cos + m2_ref[...] * sin, 0.0)
    k = jnp.dot(ckv, wk_ref[...], preferred_element_type=F32)
    k_ref[...] = (k + jnp.concatenate([kr] * N_HEADS, axis=1)).astype(k_ref.dtype)
    v_ref[...] = jnp.dot(ckv, wv_ref[...], preferred_element_type=F32).astype(v_ref.dtype)


def _mla_prep(pd, misc, misc2, cos, sin, nq, nkv, wq, wqs, wk, wv, tm):
    t = pd.shape[0]
    hp = N_HEADS * LANES
    full = lambda a: pl.BlockSpec(a.shape, lambda i: (0, 0))
    tile = lambda w: pl.BlockSpec((tm, w), lambda i: (i, 0))
    return pl.pallas_call(
        _mla_prep_kernel,
        grid=(t // tm,),
        in_specs=[tile(MLA_Q_LORA + MLA_KV_LORA), tile(LANES), tile(LANES), tile(LANES), tile(LANES),
                  full(nq), full(nkv), full(wq), full(wqs), full(wk), full(wv)],
        out_specs=[tile(hp), tile(hp), tile(GROUP_WIDTH)],
        out_shape=[jax.ShapeDtypeStruct((t, hp), BF16), jax.ShapeDtypeStruct((t, hp), BF16),
                   jax.ShapeDtypeStruct((t, GROUP_WIDTH), BF16)],
        compiler_params=_cparams("parallel"),
        name="mla_prep",
    )(pd, misc, misc2, cos, sin, nq, nkv, wq, wqs, wk, wv)


def _attn_kernel(*refs, fox, tq, q_off, k_off, head_stride):
    if fox:
        q_ref, kv_ref, col_ref, row_ref, o_ref = refs
        k_ref = v_ref = kv_ref
        v_off = 2 * GROUP_WIDTH
    else:
        q_ref, k_ref, v_ref, o_ref = refs
        col_ref = row_ref = None
        v_off = 0
    i = pl.program_id(1)
    lane = _lane_iota((tq, LANES))
    low = lane < HEAD_DIM
    r = _row_iota((tq, tq))
    c = _lane_iota((tq, tq))
    if fox:
        allowed = c <= r
    else:
        shift = int(math.log2(MLA_CHUNK))
        allowed = (c >> shift) <= (r >> shift)
    scale = HEAD_DIM ** -0.5

    outs = []
    for h in range(N_HEADS):
        pair, odd = h // 2, h % 2
        if fox:
            ql = q_off + pair * LANES
            qh = q_ref[:, ql:ql + LANES]
            qh = jnp.where(low if odd == 0 else jnp.logical_not(low), qh * scale, 0.0).astype(BF16)
            kl = k_off + pair * LANES
            cq = col_ref[:, COL_CUMF + h:COL_CUMF + h + 1]
        else:
            qh = q_ref[:, h * head_stride:h * head_stride + LANES]
            kl = h * head_stride
        vl = v_off + pair * LANES

        def scores(rows, qh=qh, kl=kl, h=h):
            kt = k_ref[rows, kl:kl + LANES]
            s = lax.dot_general(qh, kt, (((1,), (1,)), ((), ())), preferred_element_type=F32)
            if fox:
                s = s + (cq - row_ref[0, COL_CUMF + h:COL_CUMF + h + 1, rows])
            return s

        def update(s, rows, carry, vl=vl):
            m, l, acc = carry
            m_new = jnp.maximum(m, jnp.max(s, axis=-1, keepdims=True))
            alpha = jnp.exp(m - m_new)
            p = jnp.exp(s - m_new)
            l = alpha * l + jnp.sum(p, axis=-1, keepdims=True)
            acc = alpha * acc + jnp.dot(p.astype(BF16), v_ref[rows, vl:vl + LANES],
                                        preferred_element_type=F32)
            return m_new, l, acc

        def body(j, carry):
            rows = pl.ds(pl.multiple_of(j * tq, tq), tq)
            return update(scores(rows), rows, carry)

        init = (jnp.full((tq, 1), -1e30, F32), jnp.zeros((tq, 1), F32), jnp.zeros((tq, LANES), F32))
        carry = lax.fori_loop(0, i, body, init)
        rows = pl.ds(pl.multiple_of(i * tq, tq), tq)
        s = jnp.where(allowed, scores(rows), -1e30)
        m, l, acc = update(s, rows, carry)
        outs.append(acc / l)
    o01 = jnp.where(low, outs[0], outs[1])
    o23 = jnp.where(low, outs[2], outs[3])
    o_ref[...] = jnp.concatenate([o01, o23], axis=1).astype(o_ref.dtype)


def _fox_attention(pb, col, rows, batch, seq, tq):
    nq = seq // tq
    kern = functools.partial(_attn_kernel, fox=True, tq=tq, q_off=0, k_off=GROUP_WIDTH, head_stride=0)
    return pl.pallas_call(
        kern,
        grid=(batch, nq),
        in_specs=[pl.BlockSpec((tq, 3 * GROUP_WIDTH), lambda b, i: (b * nq + i, 0)),
                  pl.BlockSpec((seq, 3 * GROUP_WIDTH), lambda b, i: (b, 0)),
                  pl.BlockSpec((tq, LANES), lambda b, i: (b * nq + i, 0)),
                  pl.BlockSpec((1, N_SCALAR_ROWS, seq), lambda b, i: (b, 0, 0))],
        out_specs=pl.BlockSpec((tq, GROUP_WIDTH), lambda b, i: (b * nq + i, 0)),
        out_shape=jax.ShapeDtypeStruct((batch * seq, GROUP_WIDTH), BF16),
        compiler_params=_cparams("parallel", "arbitrary"),
        name="fox_attention",
    )(pb, pb, col, rows)


def _mla_attention(q, k, v, batch, seq, tq):
    nq = seq // tq
    hp = N_HEADS * LANES
    kern = functools.partial(_attn_kernel, fox=False, tq=tq, q_off=0, k_off=0, head_stride=LANES)
    return pl.pallas_call(
        kern,
        grid=(batch, nq),
        in_specs=[pl.BlockSpec((tq, hp), lambda b, i: (b * nq + i, 0)),
                  pl.BlockSpec((seq, hp), lambda b, i: (b, 0)),
                  pl.BlockSpec((seq, GROUP_WIDTH), lambda b, i: (b, 0))],
        out_specs=pl.BlockSpec((tq, GROUP_WIDTH), lambda b, i: (b * nq + i, 0)),
        out_shape=jax.ShapeDtypeStruct((batch * seq, GROUP_WIDTH), BF16),
        compiler_params=_cparams("parallel", "arbitrary"),
        name="mla_attention",
    )(q, k, v)


def _outproj_kernel(x_ref, ya, yb, yc, yd, w_ref, g_ref, wr_ref, br_ref,
                    x2_ref, h2_ref, rcol_ref, rrow_ref):
    y = jnp.concatenate([ya[...], yb[...], yc[...], yd[...]], axis=1)
    x2 = x_ref[...] + jnp.dot(y, w_ref[...], preferred_element_type=F32)
    x2_ref[...] = x2
    h2 = _rms(x2, g_ref[...])
    h2_ref[...] = h2
    logits = jnp.dot(h2, wr_ref[...], preferred_element_type=F32,
                     precision=lax.Precision.HIGHEST) + br_ref[...]
    lane = _lane_iota(logits.shape)
    neg = -1e30
    big = 1 << 20
    gmask = lane < N_EXPERT_GROUPS
    gl = jnp.where(gmask, logits, neg)
    gmax = jnp.max(gl, axis=-1, keepdims=True)
    gsum = jnp.sum(jnp.where(gmask, jnp.exp(gl - gmax), 0.0), axis=-1, keepdims=True)
    g_w = 1.0 / gsum
    g_idx = jnp.min(jnp.where(gmask & (gl == gmax), lane, big), axis=-1, keepdims=True)
    e_local = lane - N_EXPERT_GROUPS
    emask = (e_local >= 0) & (e_local < N_EXPERTS) & ((e_local >> int(math.log2(EXPERTS_PER_GROUP))) == g_idx)
    el = jnp.where(emask, logits, neg)
    e1v = jnp.max(el, axis=-1, keepdims=True)
    esum = jnp.sum(jnp.where(emask, jnp.exp(el - e1v), 0.0), axis=-1, keepdims=True)
    i1 = jnp.min(jnp.where(emask & (el == e1v), lane, big), axis=-1, keepdims=True)
    el2 = jnp.where(lane == i1, neg, el)
    e2v = jnp.max(el2, axis=-1, keepdims=True)
    i2 = jnp.min(jnp.where(emask & (lane != i1) & (el2 == e2v), lane, big), axis=-1, keepdims=True)
    p1 = 1.0 / esum
    p2 = jnp.exp(e2v - e1v) / esum
    w1 = g_w * (p1 / (p1 + p2))
    w2 = g_w * (p2 / (p1 + p2))
    rt = jnp.where(lane == 0, (i1 - N_EXPERT_GROUPS).astype(F32),
                   jnp.where(lane == 1, (i2 - N_EXPERT_GROUPS).astype(F32),
                             jnp.where(lane == 2, w1, jnp.where(lane == 3, w2, 0.0))))
    rcol_ref[...] = rt
    rrow_ref[...] = rt.T[:8]


def _outproj(x, ya, yb, yc, yd, w, g, wr, br, tm):
    t = x.shape[0]
    full = lambda a: pl.BlockSpec(a.shape, lambda i: (0, 0))
    tile = lambda wd: pl.BlockSpec((tm, wd), lambda i: (i, 0))
    return pl.pallas_call(
        _outproj_kernel,
        grid=(t // tm,),
        in_specs=[tile(D_MODEL)] + [tile(GROUP_WIDTH)] * 4 + [full(w), full(g), full(wr), full(br)],
        out_specs=[tile(D_MODEL), tile(D_MODEL), tile(LANES), pl.BlockSpec((8, tm), lambda i: (0, i))],
        out_shape=[jax.ShapeDtypeStruct((t, D_MODEL), F32), jax.ShapeDtypeStruct((t, D_MODEL), F32),
                   jax.ShapeDtypeStruct((t, LANES), F32), jax.ShapeDtypeStruct((8, t), F32)],
        compiler_params=_cparams("parallel"),
        name="outproj_router",
    )(x, ya, yb, yc, yd, w, g, wr, br)


def _positions_kernel(r_ref, dest_ref, meta_ref, cnt_ref, carry_ref, start_ref, *, tm, nbp):
    phase = pl.program_id(0)
    i = pl.program_id(1)
    e_iota = _row_iota((N_EXPERTS, tm))
    e0 = r_ref[0:1, :].astype(I32)
    e1 = r_ref[1:2, :].astype(I32)
    oh0 = e_iota == e0
    oh1 = e_iota == e1
    oh = jnp.where(oh0 | oh1, 1.0, 0.0)

    @pl.when((phase == 0) & (i == 0))
    def _():
        cnt_ref[...] = jnp.zeros_like(cnt_ref)

    @pl.when(phase == 0)
    def _():
        cnt_ref[...] += jnp.sum(oh, axis=-1, keepdims=True)

    @pl.when((phase == 1) & (i == 0))
    def _():
        cnt = cnt_ref[...]
        padded = jnp.floor((cnt + (MOE_ROWS - 1)) * (1.0 / MOE_ROWS)) * MOE_ROWS
        tril = jnp.where(_row_iota((N_EXPERTS, N_EXPERTS)) >= _lane_iota((N_EXPERTS, N_EXPERTS)), 1.0, 0.0)
        pend = jnp.dot(tril, padded, preferred_element_type=F32, precision=lax.Precision.HIGHEST)
        pstart = pend - padded
        start_ref[...] = pstart
        carry_ref[...] = jnp.zeros_like(carry_ref)
        pend_b = jnp.concatenate([pend] * (nbp // LANES), axis=1)
        vend_b = jnp.concatenate([pstart + cnt] * (nbp // LANES), axis=1)
        b0 = (_lane_iota((N_EXPERTS, nbp)) * MOE_ROWS).astype(F32)
        bexp = jnp.sum(jnp.where(pend_b <= b0, 1.0, 0.0), axis=0, keepdims=True)
        bexp = jnp.minimum(bexp, N_EXPERTS - 1.0)
        is_e = _row_iota((N_EXPERTS, nbp)).astype(F32) == bexp
        vend = jnp.sum(jnp.where(is_e, vend_b, 0.0), axis=0, keepdims=True)
        nvalid = jnp.clip(vend - b0[0:1], 0.0, float(MOE_ROWS))
        total = jnp.max(pend_b, axis=0, keepdims=True) * (1.0 / MOE_ROWS)
        row = _row_iota((8, nbp))
        meta = jnp.where(row == 0, bexp, jnp.where(row == 1, nvalid, jnp.where(row == 2, total, 0.0)))
        meta_ref[...] = meta.astype(I32)

    @pl.when(phase == 1)
    def _():
        su = jnp.where(_row_iota((tm, tm)) < _lane_iota((tm, tm)), 1.0, 0.0).astype(BF16)
        before = jnp.dot(oh.astype(BF16), su, preferred_element_type=F32)
        base = start_ref[:, 0:1] + carry_ref[:, 0:1] + before
        d0 = jnp.sum(jnp.where(oh0, base, 0.0), axis=0, keepdims=True)
        d1 = jnp.sum(jnp.where(oh1, base, 0.0), axis=0, keepdims=True)
        dest_ref[0, 0:1, :] = d0.astype(I32)
        dest_ref[0, 1:2, :] = d1.astype(I32)
        carry_ref[...] += jnp.sum(oh, axis=-1, keepdims=True)


def _positions(rrow, tm, nbp):
    t = rrow.shape[1]
    nt = t // tm
    kern = functools.partial(_positions_kernel, tm=tm, nbp=nbp)
    return pl.pallas_call(
        kern,
        grid=(2, nt),
        in_specs=[pl.BlockSpec((8, tm), lambda p, i: (0, i))],
        out_specs=[pl.BlockSpec((1, 2, tm), lambda p, i: (i * p, 0, 0)),
                   pl.BlockSpec((8, nbp), lambda p, i: (0, 0))],
        out_shape=[jax.ShapeDtypeStruct((nt, 2, tm), I32), jax.ShapeDtypeStruct((8, nbp), I32)],
        scratch_shapes=[pltpu.VMEM((N_EXPERTS, LANES), F32)] * 3,
        compiler_params=_cparams("arbitrary", "arbitrary"),
        name="moe_positions",
    )(rrow)


def _dispatch_kernel(meta_ref, dest_ref, h_ref, xs_ref, zero_ref, sem, zsem, *, tm, nb):
    i = pl.program_id(0)

    @pl.when(i == 0)
    def _():
        zero_ref[...] = jnp.zeros_like(zero_ref)
        n_used = meta_ref[2, 0]

        def zcopy(b):
            return pltpu.make_async_copy(zero_ref, xs_ref.at[pl.ds(b * MOE_ROWS, MOE_ROWS)], zsem)

        def needs(b):
            return (b < n_used) & (meta_ref[1, b] < MOE_ROWS)

        def start(b, c):
            @pl.when(needs(b))
            def _():
                zcopy(b).start()
            return c

        def wait(b, c):
            @pl.when(needs(b))
            def _():
                zcopy(b).wait()
            return c

        lax.fori_loop(0, nb, start, 0)
        lax.fori_loop(0, nb, wait, 0)

    def copy(t, k):
        return pltpu.make_async_copy(h_ref.at[pl.ds(t, 1)], xs_ref.at[pl.ds(dest_ref[0, k, t], 1)], sem)

    def start(t, c):
        copy(t, 0).start()
        copy(t, 1).start()
        return c

    def wait(t, c):
        copy(t, 0).wait()
        copy(t, 1).wait()
        return c

    lax.fori_loop(0, tm, start, 0)
    lax.fori_loop(0, tm, wait, 0)


def _dispatch(meta, dest, h2, tm, nb):
    t = h2.shape[0]
    kern = functools.partial(_dispatch_kernel, tm=tm, nb=nb)
    grid_spec = pltpu.PrefetchScalarGridSpec(
        num_scalar_prefetch=1,
        grid=(t // tm,),
        in_specs=[pl.BlockSpec((1, 2, tm), lambda i, m: (i, 0, 0), memory_space=pltpu.SMEM),
                  pl.BlockSpec((tm, D_MODEL), lambda i, m: (i, 0))],
        out_specs=pl.BlockSpec(memory_space=pl.ANY),
        scratch_shapes=[pltpu.VMEM((MOE_ROWS, D_MODEL), F32),
                        pltpu.SemaphoreType.DMA, pltpu.SemaphoreType.DMA],
    )
    return pl.pallas_call(
        kern,
        grid_spec=grid_spec,
        out_shape=jax.ShapeDtypeStruct((nb * MOE_ROWS, D_MODEL), F32),
        compiler_params=_cparams("arbitrary"),
        name="moe_dispatch",
    )(meta, dest, h2)


def _expert_kernel(meta_ref, x_ref, wg_ref, wu_ref, wd_ref, o_ref):
    b = pl.program_id(0)

    @pl.when(b < meta_ref[2, 0])
    def _():
        x = x_ref[...].astype(BF16)
        gate = jnp.dot(x, wg_ref[0].astype(BF16), preferred_element_type=F32)
        up = jnp.dot(x, wu_ref[0].astype(BF16), preferred_element_type=F32)
        act = (_silu(gate) * up).astype(BF16)
        o_ref[...] = jnp.dot(act, wd_ref[0].astype(BF16), preferred_element_type=F32)


def _experts(meta, xs, wg, wu, wd, nb):
    def blk(b, m):
        return (jnp.minimum(b, m[2, 0] - 1), 0)

    def wblk(b, m):
        return (m[0, jnp.minimum(b, m[2, 0] - 1)], 0, 0)

    grid_spec = pltpu.PrefetchScalarGridSpec(
        num_scalar_prefetch=1,
        grid=(nb,),
        in_specs=[pl.BlockSpec((MOE_ROWS, D_MODEL), blk),
                  pl.BlockSpec((1, D_MODEL, EXPERT_FF), wblk),
                  pl.BlockSpec((1, D_MODEL, EXPERT_FF), wblk),
                  pl.BlockSpec((1, EXPERT_FF, D_MODEL), wblk)],
        out_specs=pl.BlockSpec((MOE_ROWS, D_MODEL), blk),
    )
    return pl.pallas_call(
        _expert_kernel,
        grid_spec=grid_spec,
        out_shape=jax.ShapeDtypeStruct((nb * MOE_ROWS, D_MODEL), F32),
        compiler_params=_cparams("arbitrary"),
        name="moe_experts",
    )(meta, xs, wg, wu, wd)


def _combine_kernel(dest_ref, x_ref, r_ref, g_ref, ys_ref, o_ref, buf_ref, sem, *, tm, final):
    def copy(t, k):
        return pltpu.make_async_copy(ys_ref.at[pl.ds(dest_ref[0, k, t], 1)],
                                     buf_ref.at[k, pl.ds(t, 1)], sem)

    def start(t, c):
        copy(t, 0).start()
        copy(t, 1).start()
        return c

    def wait(t, c):
        copy(t, 0).wait()
        copy(t, 1).wait()
        return c

    lax.fori_loop(0, tm, start, 0)
    lax.fori_loop(0, tm, wait, 0)
    w0 = r_ref[:, 2:3]
    w1 = r_ref[:, 3:4]
    x = x_ref[...] + (buf_ref[0] * w0 + buf_ref[1] * w1)
    o_ref[...] = _rms(x, g_ref[...]) if final else x


def _combine(dest, x2, rcol, g, ys, tm, final):
    t = x2.shape[0]
    kern = functools.partial(_combine_kernel, tm=tm, final=final)
    return pl.pallas_call(
        kern,
        grid=(t // tm,),
        in_specs=[pl.BlockSpec((1, 2, tm), lambda i: (i, 0, 0), memory_space=pltpu.SMEM),
                  pl.BlockSpec((tm, D_MODEL), lambda i: (i, 0)),
                  pl.BlockSpec((tm, LANES), lambda i: (i, 0)),
                  pl.BlockSpec((1, D_MODEL), lambda i: (0, 0)),
                  pl.BlockSpec(memory_space=pl.ANY)],
        out_specs=pl.BlockSpec((tm, D_MODEL), lambda i: (i, 0)),
        out_shape=jax.ShapeDtypeStruct((t, D_MODEL), F32),
        scratch_shapes=[pltpu.VMEM((2, tm, D_MODEL), F32), pltpu.SemaphoreType.DMA],
        compiler_params=_cparams("arbitrary"),
        name="moe_combine",
    )(dest, x2, rcol, g, ys)


def _pad_rows(a, rows=8):
    return jnp.zeros((rows, a.shape[-1]), F32).at[:a.shape[0]].set(a.astype(F32))


def _arrange_w_in(w):
    gw = GROUP_WIDTH
    a0 = 0
    b0 = 3 * gw
    c0 = b0 + 3 * gw + N_HEADS
    d0 = c0 + gw + (gw + 4 * SSM_STATE) + N_HEADS
    half = MLA_ROPE // 2
    f_logit = w[:, b0 + 3 * gw:b0 + 3 * gw + N_HEADS]
    dt_raw = w[:, c0 + 3 * gw:c0 + 3 * gw + N_HEADS]
    kr0 = d0 + MLA_Q_LORA + MLA_KV_LORA
    kr = w[:, kr0:kr0 + MLA_ROPE]
    kr_sw = jnp.concatenate([-kr[:, half:], kr[:, :half]], axis=1)
    zeros = lambda n: jnp.zeros((w.shape[0], n), w.dtype)
    misc = jnp.concatenate([f_logit, dt_raw, zeros(MISC_ROPE - 2 * N_HEADS), kr,
                            zeros(LANES - MISC_ROPE - MLA_ROPE)], axis=1)
    misc2 = jnp.concatenate([zeros(MISC_ROPE), kr_sw, zeros(LANES - MISC_ROPE - MLA_ROPE)], axis=1)
    out = jnp.concatenate([w[:, a0:a0 + 3 * gw], w[:, b0:b0 + 3 * gw], w[:, c0:c0 + 3 * gw],
                           w[:, d0:d0 + MLA_Q_LORA + MLA_KV_LORA], misc, misc2], axis=1)
    return out.astype(BF16)


def _arrange_mla(w_uq, w_ukv):
    half = MLA_ROPE // 2
    qd = MLA_NOPE + MLA_ROPE
    wq, wqs, wk, wv = [], [], [], []
    zq = jnp.zeros((MLA_Q_LORA, LANES - qd), w_uq.dtype)
    zk = jnp.zeros((MLA_KV_LORA, LANES - MLA_NOPE), w_ukv.dtype)
    for h in range(N_HEADS):
        q = w_uq[:, h * qd:(h + 1) * qd]
        nope, rope = q[:, :MLA_NOPE], q[:, MLA_NOPE:]
        wq.append(jnp.concatenate([nope, rope, zq], axis=1))
        wqs.append(jnp.concatenate([jnp.zeros_like(nope), -rope[:, half:], rope[:, :half], zq], axis=1))
        kv = w_ukv[:, h * 2 * MLA_NOPE:(h + 1) * 2 * MLA_NOPE]
        wk.append(jnp.concatenate([kv[:, :MLA_NOPE], zk], axis=1))
        wv.append(kv[:, MLA_NOPE:])
    cat = lambda xs: jnp.concatenate(xs, axis=1).astype(BF16)
    return cat(wq), cat(wqs), cat(wk), cat(wv)


def kernel(x, positions, norm_mix, w_in, conv_a, fox_forget_bias, ssm_conv_w, ssm_conv_b, ssm_dt_bias,
           ssm_a_log, ssm_d, ssm_norm, mla_q_norm, mla_kv_norm, mla_w_uq, mla_w_ukv, w_out, norm_ffn,
           router_group_w, router_group_b, router_expert_w, router_expert_b, expert_w_gate, expert_w_up,
           expert_w_down, norm_final):
    batch, seq, d = x.shape
    t = batch * seq
    depth = w_in.shape[0]
    tm = min(512, t)
    tq = min(256, seq)
    tmd = min(256, t)
    n_slots = 2 * t
    nb = -(-(n_slots + N_EXPERTS * (MOE_ROWS - 1)) // MOE_ROWS)
    nbp = -(-nb // LANES) * LANES

    xf = x.reshape(t, d)
    pos_col = positions.astype(F32).reshape(t, 1)
    cos, sin = _rope_tables(pos_col, tm)

    for l in range(depth):
        w_in_p = _arrange_w_in(w_in[l])
        pa, pb, pc, pd, misc, misc2 = _inproj(xf, norm_mix[l][None, :], w_in_p, tm)

        sp = jnp.zeros((8, LANES), F32)
        sp = sp.at[0, MISC_F:MISC_F + N_HEADS].set(fox_forget_bias[l])
        sp = sp.at[0, MISC_DT:MISC_DT + N_HEADS].set(ssm_dt_bias[l])
        sp = sp.at[1, MISC_DT:MISC_DT + N_HEADS].set(ssm_a_log[l])
        col, rows = _scalar_prep(misc, sp, batch, seq)

        ya = _conv_mixer(pa, _pad_rows(conv_a[l]), batch, seq)
        yb = _fox_attention(pb, col, rows, batch, seq, tq)
        conv_wb = _pad_rows(jnp.concatenate([ssm_conv_w[l], ssm_conv_b[l][None, :]], axis=0))
        ssd_par = _pad_rows(jnp.stack([jnp.repeat(ssm_d[l], HEAD_DIM), ssm_norm[l]]))
        yc = _ssd_mixer(pc, col, rows, conv_wb, ssd_par, batch, seq)
        wq, wqs, wk, wv = _arrange_mla(mla_w_uq[l], mla_w_ukv[l])
        q, k, v = _mla_prep(pd, misc, misc2, cos, sin, mla_q_norm[l][None, :], mla_kv_norm[l][None, :],
                            wq, wqs, wk, wv, tm)
        yd = _mla_attention(q, k, v, batch, seq, tq)

        wr = jnp.zeros((d, LANES), F32)
        wr = wr.at[:, :N_EXPERT_GROUPS].set(router_group_w[l])
        wr = wr.at[:, N_EXPERT_GROUPS:N_EXPERT_GROUPS + N_EXPERTS].set(router_expert_w[l])
        br = jnp.zeros((1, LANES), F32)
        br = br.at[0, :N_EXPERT_GROUPS].set(router_group_b[l])
        br = br.at[0, N_EXPERT_GROUPS:N_EXPERT_GROUPS + N_EXPERTS].set(router_expert_b[l])
        x2, h2, rcol, rrow = _outproj(xf, ya, yb, yc, yd, w_out[l].astype(BF16), norm_ffn[l][None, :],
                                      wr, br, tm)

        dest, meta = _positions(rrow, tm, nbp)
        dest = _retile(dest, tm, tmd)
        xs = _dispatch(meta, dest, h2, tmd, nb)
        ys = _experts(meta, xs, expert_w_gate[l], expert_w_up[l], expert_w_down[l], nb)
        final = l == depth - 1
        xf = _combine(dest, x2, rcol, norm_final[None, :], ys, tmd, final)

    return xf.reshape(batch, seq, d)


def _retile(dest, tm, tmd):
    if tm == tmd:
        return dest
    nt = dest.shape[0]
    return dest.reshape(nt, 2, tm // tmd, tmd).transpose(0, 2, 1, 3).reshape(nt * (tm // tmd), 2, tmd)
```

```python
import functools
import math

import jax
import jax.numpy as jnp
import numpy as np
from jax import lax
from jax.experimental import pallas as pl
from jax.experimental.pallas import tpu as pltpu

F32 = jnp.float32
BF16 = jnp.bfloat16
I32 = jnp.int32

LANES = 128
VMEM_LIMIT_BYTES = 56 * 1024 * 1024

D_MODEL = 1024
RMS_EPS = 1e-6
LOG2E = math.log2(math.e)
GROUP_WIDTH = 256
HEAD_DIM = 64
N_HEADS = 4

CONV_A_WIDTH = 3
SSM_CONV = 4
SSM_STATE = 64
SSM_CHUNK = 256

MLA_NOPE = 64
MLA_ROPE = 32
MLA_Q_LORA = 256
MLA_KV_LORA = 128
ROPE_BASE = 10000.0
MLA_CHUNK = 64
ATTN_TQ = 512
ATTN_RB = 128

N_EXPERT_GROUPS = 4
EXPERTS_PER_GROUP = 8
N_EXPERTS = 32
EXPERT_FF = 256
MOE_ROWS = 256

SEG_A = (0, 768)
SEG_B = (768, 1280)
SEG_C = (1280, 2048)
SEG_D = (2048, 2432)
SEG_M = (2432, 2560)
SEG_M2 = (2560, 2688)
IN_COLS_PADDED = 2688
HEAD_PAD = N_HEADS * LANES
AUG_LANE = HEAD_DIM
MISC_F = 0
MISC_DT = 4
MISC_ROPE = 64
COL_CUMF = 0
COL_DT = 4
COL_ACUM = 8
N_SCALAR_ROWS = 16


def _cparams(*sem):
    return pltpu.CompilerParams(dimension_semantics=sem, vmem_limit_bytes=VMEM_LIMIT_BYTES)


def _lane_iota(shape):
    return lax.broadcasted_iota(I32, shape, len(shape) - 1)


def _row_iota(shape):
    return lax.broadcasted_iota(I32, shape, 0)


def _rms(x, g):
    ms = jnp.mean(x * x, axis=-1, keepdims=True)
    return x * lax.rsqrt(ms + RMS_EPS) * g


def _silu(x):
    return x / (1.0 + jnp.exp(-x))


def _softplus(x):
    return jnp.maximum(x, 0.0) + jnp.log(1.0 + jnp.exp(-jnp.abs(x)))


def _shift_rows(x, k):
    rolled = pltpu.roll(x, k, 0)
    return jnp.where(_row_iota(x.shape) >= k, rolled, 0.0)


def _rope_kernel(pos_ref, freq_ref, cos_ref, sin_ref):
    ang = pos_ref[...] * freq_ref[...]
    lane = _lane_iota(ang.shape)
    rope = (lane >= MISC_ROPE) & (lane < MISC_ROPE + MLA_ROPE)
    cos_ref[...] = jnp.where(rope, jnp.cos(ang), jnp.where(lane < MISC_ROPE, 1.0, 0.0))
    sin_ref[...] = jnp.where(rope, jnp.sin(ang), 0.0)


def _rope_tables(pos_col, tm):
    t = pos_col.shape[0]
    half = MLA_ROPE // 2
    inv = ROPE_BASE ** (-np.arange(0, MLA_ROPE, 2, dtype=np.float32) / MLA_ROPE)
    freq = np.zeros((1, LANES), np.float32)
    freq[0, MISC_ROPE:MISC_ROPE + half] = inv
    freq[0, MISC_ROPE + half:MISC_ROPE + MLA_ROPE] = inv
    return pl.pallas_call(
        _rope_kernel,
        grid=(t // tm,),
        in_specs=[pl.BlockSpec((tm, 1), lambda i: (i, 0)),
                  pl.BlockSpec((1, LANES), lambda i: (0, 0))],
        out_specs=[pl.BlockSpec((tm, LANES), lambda i: (i, 0))] * 2,
        out_shape=[jax.ShapeDtypeStruct((t, LANES), F32)] * 2,
        compiler_params=_cparams("parallel"),
        name="rope_tables",
    )(pos_col, jnp.asarray(freq))


def _inproj_kernel(x_ref, g_ref, w_ref, wvt_ref, oa, ob, oc, od, om, om2, ovt):
    h = _rms(x_ref[...], g_ref[...]).astype(BF16)
    for o, (lo, hi) in ((oa, SEG_A), (ob, SEG_B), (oc, SEG_C), (od, SEG_D), (om, SEG_M), (om2, SEG_M2)):
        o[...] = jnp.dot(h, w_ref[:, lo:hi], preferred_element_type=F32).astype(o.dtype)
    ovt[...] = lax.dot_general(wvt_ref[...], h, (((1,), (1,)), ((), ())),
                               preferred_element_type=F32).astype(ovt.dtype)


def _inproj(x, g, w, wvt, tm):
    t = x.shape[0]
    widths = [(s[1] - s[0]) for s in (SEG_A, SEG_B, SEG_C, SEG_D, SEG_M, SEG_M2)]
    dtypes = [BF16, BF16, BF16, BF16, F32, F32]
    return pl.pallas_call(
        _inproj_kernel,
        grid=(t // tm,),
        in_specs=[pl.BlockSpec((tm, D_MODEL), lambda i: (i, 0)),
                  pl.BlockSpec((1, D_MODEL), lambda i: (0, 0)),
                  pl.BlockSpec((D_MODEL, IN_COLS_PADDED), lambda i: (0, 0)),
                  pl.BlockSpec((GROUP_WIDTH, D_MODEL), lambda i: (0, 0))],
        out_specs=[pl.BlockSpec((tm, wd), lambda i: (i, 0)) for wd in widths]
        + [pl.BlockSpec((GROUP_WIDTH, tm), lambda i: (0, i))],
        out_shape=[jax.ShapeDtypeStruct((t, wd), dt) for wd, dt in zip(widths, dtypes)]
        + [jax.ShapeDtypeStruct((GROUP_WIDTH, t), BF16)],
        compiler_params=_cparams("parallel"),
        name="inproj",
    )(x, g, w, wvt)


def _scalar_prep_kernel(m_ref, p_ref, qk_ref, sel_ref, place_ref, const_ref,
                        col_ref, row_ref, qa_ref, ka_ref, tref_ref, *, tq):
    s = m_ref.shape[0]
    tref_ref[...] = jnp.zeros_like(tref_ref)
    tile_ref = jnp.zeros((1, LANES), F32)
    m = m_ref[...]
    bias = p_ref[0:1, :]
    a_log = p_ref[1:2, :]
    lane = _lane_iota(m.shape)
    z = m + bias
    logf = jnp.minimum(z, 0.0) - jnp.log(1.0 + jnp.exp(-jnp.abs(z)))
    dt = _softplus(z)
    a = dt * (-jnp.exp(a_log))
    is_f = lane < MISC_DT
    is_dt = (lane >= MISC_DT) & (lane < MISC_DT + N_HEADS)
    v = jnp.where(is_f, logf, jnp.where(is_dt, a, 0.0))
    r = _row_iota((SSM_CHUNK, SSM_CHUNK))
    c = _lane_iota((SSM_CHUNK, SSM_CHUNK))
    tril = jnp.where(r >= c, 1.0, 0.0).astype(F32)
    carry = jnp.zeros((1, LANES), F32)
    lane_1 = _lane_iota((1, LANES))
    lane_b = _lane_iota((SSM_CHUNK, LANES))
    for ci in range(s // SSM_CHUNK):
        blk = v[ci * SSM_CHUNK:(ci + 1) * SSM_CHUNK]
        cs = jnp.dot(tril, blk, preferred_element_type=F32, precision=lax.Precision.HIGHEST)
        cs = cs + jnp.where(lane_1 < MISC_DT, carry, 0.0)
        carry = cs[SSM_CHUNK - 1:SSM_CHUNK]
        acum = pltpu.roll(cs, COL_ACUM - MISC_DT, 1)
        out = jnp.where(lane_b < MISC_DT, cs * LOG2E,
                        jnp.where(lane_b < COL_ACUM, dt[ci * SSM_CHUNK:(ci + 1) * SSM_CHUNK],
                                  jnp.where(lane_b < COL_ACUM + N_HEADS, acum, 0.0)))
        rows = slice(ci * SSM_CHUNK, (ci + 1) * SSM_CHUNK)
        col_ref[rows, :] = out
        row_ref[0, :, rows] = out.T[:N_SCALAR_ROWS]
        if (ci * SSM_CHUNK) % tq == 0:
            tile_ref = out[0:1, :]
            ti = (ci * SSM_CHUNK) // tq
            tref_ref[0, ti:ti + 1, :] = tile_ref
        c = out - tile_ref
        c_hi = c.astype(BF16)
        r1 = c - c_hi.astype(F32)
        c_mid = r1.astype(BF16)
        c_lo = (r1 - c_mid.astype(F32)).astype(BF16)
        for o_ref, base, qk_lo in ((qa_ref, 0, 0), (ka_ref, 3, GROUP_WIDTH)):
            aug = jnp.dot(qk_ref[rows, qk_lo:qk_lo + GROUP_WIDTH], sel_ref[...], preferred_element_type=F32)
            for term, cc in enumerate((c_hi, c_mid, c_lo)):
                aug = aug + jnp.dot(cc, place_ref[base + term], preferred_element_type=F32)
            o_ref[rows, :] = (aug + const_ref[base // 3:base // 3 + 1, :]).astype(o_ref.dtype)


def _fox_placement():
    sel = np.zeros((GROUP_WIDTH, HEAD_PAD), np.float32)
    place = np.zeros((6, LANES, HEAD_PAD), np.float32)
    const = np.zeros((8, HEAD_PAD), np.float32)
    for h in range(N_HEADS):
        for d in range(HEAD_DIM):
            sel[h * HEAD_DIM + d, h * LANES + d] = 1.0
        a0 = h * LANES + AUG_LANE
        for term in range(3):
            place[term, COL_CUMF + h, a0 + term] = 1.0
            place[3 + term, COL_CUMF + h, a0 + 3 + term] = -1.0
            const[0, a0 + 3 + term] = 1.0
            const[1, a0 + term] = 1.0
    return jnp.asarray(sel, BF16), jnp.asarray(place, BF16), jnp.asarray(const, F32)


def _scalar_prep(misc, params, qk, batch, seq, tq):
    sel, place, const = _fox_placement()
    full = lambda a: pl.BlockSpec(a.shape, lambda b: (0,) * a.ndim)
    return pl.pallas_call(
        functools.partial(_scalar_prep_kernel, tq=tq),
        grid=(batch,),
        in_specs=[pl.BlockSpec((seq, LANES), lambda b: (b, 0)),
                  pl.BlockSpec((8, LANES), lambda b: (0, 0)),
                  pl.BlockSpec((seq, 2 * GROUP_WIDTH), lambda b: (b, 0)),
                  full(sel), full(place), full(const)],
        out_specs=[pl.BlockSpec((seq, LANES), lambda b: (b, 0)),
                   pl.BlockSpec((1, N_SCALAR_ROWS, seq), lambda b: (b, 0, 0)),
                   pl.BlockSpec((seq, HEAD_PAD), lambda b: (b, 0)),
                   pl.BlockSpec((seq, HEAD_PAD), lambda b: (b, 0)),
                   pl.BlockSpec((1, 8, LANES), lambda b: (b, 0, 0))],
        out_shape=[jax.ShapeDtypeStruct((batch * seq, LANES), F32),
                   jax.ShapeDtypeStruct((batch, N_SCALAR_ROWS, seq), F32),
                   jax.ShapeDtypeStruct((batch * seq, HEAD_PAD), BF16),
                   jax.ShapeDtypeStruct((batch * seq, HEAD_PAD), BF16),
                   jax.ShapeDtypeStruct((batch, 8, LANES), F32)],
        compiler_params=_cparams("parallel"),
        name="scalar_prep",
    )(misc, params, qk, sel, place, const)


def _conv_mixer_kernel(p_ref, w_ref, o_ref):
    gw = GROUP_WIDTH
    b_gate = p_ref[:, 0:gw].astype(F32)
    cv = p_ref[:, gw:2 * gw].astype(F32) * p_ref[:, 2 * gw:3 * gw].astype(F32)
    acc = cv * w_ref[CONV_A_WIDTH - 1:CONV_A_WIDTH, :]
    for k in range(1, CONV_A_WIDTH):
        acc = acc + _shift_rows(cv, k) * w_ref[CONV_A_WIDTH - 1 - k:CONV_A_WIDTH - k, :]
    o_ref[...] = (b_gate * acc).astype(o_ref.dtype)


def _conv_mixer(pa, w, batch, seq):
    return pl.pallas_call(
        _conv_mixer_kernel,
        grid=(batch,),
        in_specs=[pl.BlockSpec((seq, 3 * GROUP_WIDTH), lambda b: (b, 0)),
                  pl.BlockSpec((8, GROUP_WIDTH), lambda b: (0, 0))],
        out_specs=pl.BlockSpec((seq, GROUP_WIDTH), lambda b: (b, 0)),
        out_shape=jax.ShapeDtypeStruct((batch * seq, GROUP_WIDTH), BF16),
        compiler_params=_cparams("parallel"),
        name="conv_mixer",
    )(pa, w)


def _pair_lanes(col, base, shape):
    lane = _lane_iota(shape)
    return jnp.where(lane < HEAD_DIM, col[:, base:base + 1], col[:, base + 1:base + 2])


def _ssd_kernel(p_ref, col_ref, row_ref, cw_ref, par_ref, o_ref, u_ref):
    s = p_ref.shape[0]
    q = SSM_CHUNK
    gw = GROUP_WIDTH
    xbc = p_ref[:, gw:3 * gw].astype(F32)
    acc = xbc * cw_ref[SSM_CONV - 1:SSM_CONV, :]
    for k in range(1, SSM_CONV):
        acc = acc + _shift_rows(xbc, k) * cw_ref[SSM_CONV - 1 - k:SSM_CONV - k, :]
    u_ref[...] = _silu(acc + cw_ref[SSM_CONV:SSM_CONV + 1, :])

    d_skip = par_ref[0:1, :]
    norm_g = par_ref[1:2, :]
    lane_q = _lane_iota((q, LANES))
    low = lane_q < HEAD_DIM
    tri = _row_iota((q, q)) >= _lane_iota((q, q))

    def chunk(ci, states):
        r0 = pl.multiple_of(ci * q, q)
        rows = pl.ds(r0, q)
        u = u_ref[rows, :]
        col = col_ref[rows, :]
        bm = u[:, gw:gw + LANES]
        cm = u[:, gw + LANES:gw + 2 * LANES]
        z = p_ref[rows, 0:gw].astype(F32)
        new_states = []
        ys = []
        for g in range(2):
            sel = low if g == 0 else jnp.logical_not(low)
            cg = jnp.where(sel, cm, 0.0).astype(BF16)
            bg = jnp.where(sel, bm, 0.0)
            gmat = lax.dot_general(cg, bm.astype(BF16), (((1,), (1,)), ((), ())),
                                   preferred_element_type=F32)
            xs = u[:, g * LANES:(g + 1) * LANES]
            dt2 = _pair_lanes(col, COL_DT + 2 * g, (q, LANES))
            ac2 = _pair_lanes(col, COL_ACUM + 2 * g, (q, LANES))
            xdt = xs * dt2
            xdt_b = xdt.astype(BF16)
            st = states[g]
            y_off = jnp.dot(cg, st.astype(BF16), preferred_element_type=F32) * jnp.exp(ac2)
            halves = []
            for hh in range(2):
                h = 2 * g + hh
                ac_col = col[:, COL_ACUM + h:COL_ACUM + h + 1]
                ac_row = row_ref[0, COL_ACUM + h:COL_ACUM + h + 1, rows]
                decay = jnp.exp(jnp.where(tri, ac_col - ac_row, -1e30))
                mm = (gmat * decay).astype(BF16)
                halves.append(jnp.dot(mm, xdt_b, preferred_element_type=F32))
            y = jnp.where(low, halves[0], halves[1]) + y_off + d_skip[:, g * LANES:(g + 1) * LANES] * xs
            ys.append(y)
            ac_last = ac2[q - 1:q, :]
            w_end = jnp.exp(ac_last - ac2)
            xw = (xdt * w_end).astype(BF16)
            upd = jnp.dot(bg.T.astype(BF16), xw, preferred_element_type=F32)
            new_states.append(st * jnp.exp(ac_last) + upd)
        yfull = jnp.concatenate(ys, axis=1) * _silu(z)
        o_ref[rows, :] = _rms(yfull, norm_g).astype(o_ref.dtype)
        return tuple(new_states)

    init = (jnp.zeros((LANES, LANES), F32), jnp.zeros((LANES, LANES), F32))
    lax.fori_loop(0, s // q, chunk, init)


def _ssd_mixer(pc, col, rows, conv_wb, par, batch, seq):
    gw = GROUP_WIDTH
    return pl.pallas_call(
        _ssd_kernel,
        grid=(batch,),
        in_specs=[pl.BlockSpec((seq, 3 * gw), lambda b: (b, 0)),
                  pl.BlockSpec((seq, LANES), lambda b: (b, 0)),
                  pl.BlockSpec((1, N_SCALAR_ROWS, seq), lambda b: (b, 0, 0)),
                  pl.BlockSpec((8, 2 * gw), lambda b: (0, 0)),
                  pl.BlockSpec((8, gw), lambda b: (0, 0))],
        out_specs=pl.BlockSpec((seq, gw), lambda b: (b, 0)),
        out_shape=jax.ShapeDtypeStruct((batch * seq, gw), BF16),
        scratch_shapes=[pltpu.VMEM((seq, 2 * gw), F32)],
        compiler_params=_cparams("parallel"),
        name="ssd_mixer",
    )(pc, col, rows, conv_wb, par)


def _mla_prep_kernel(pd_ref, m_ref, m2_ref, cos_ref, sin_ref, nq_ref, nkv_ref,
                     wq_ref, wqs_ref, wk_ref, wvt_ref, q_ref, k_ref, vt_ref):
    cq = _rms(pd_ref[:, 0:MLA_Q_LORA].astype(F32), nq_ref[...]).astype(BF16)
    ckv = _rms(pd_ref[:, MLA_Q_LORA:MLA_Q_LORA + MLA_KV_LORA].astype(F32), nkv_ref[...]).astype(BF16)
    cos = cos_ref[...]
    sin = sin_ref[...]
    cos4 = jnp.concatenate([cos] * N_HEADS, axis=1)
    sin4 = jnp.concatenate([sin] * N_HEADS, axis=1)
    scale = (MLA_NOPE + MLA_ROPE) ** -0.5 * LOG2E
    q = jnp.dot(cq, wq_ref[...], preferred_element_type=F32)
    qs = jnp.dot(cq, wqs_ref[...], preferred_element_type=F32)
    q_ref[...] = ((q * cos4 + qs * sin4) * scale).astype(q_ref.dtype)
    lane = _lane_iota(cos.shape)
    rope = (lane >= MISC_ROPE) & (lane < MISC_ROPE + MLA_ROPE)
    kr = jnp.where(rope, m_ref[...] * cos + m2_ref[...] * sin, 0.0)
    k = jnp.dot(ckv, wk_ref[...], preferred_element_type=F32)
    k_ref[...] = (k + jnp.concatenate([kr] * N_HEADS, axis=1)).astype(k_ref.dtype)
    vt_ref[...] = lax.dot_general(wvt_ref[...], ckv, (((1,), (1,)), ((), ())),
                                  preferred_element_type=F32).astype(vt_ref.dtype)


def _mla_prep(pd, misc, misc2, cos, sin, nq, nkv, wq, wqs, wk, wv, tm):
    t = pd.shape[0]
    hp = N_HEADS * LANES
    full = lambda a: pl.BlockSpec(a.shape, lambda i: (0, 0))
    tile = lambda w: pl.BlockSpec((tm, w), lambda i: (i, 0))
    return pl.pallas_call(
        _mla_prep_kernel,
        grid=(t // tm,),
        in_specs=[tile(MLA_Q_LORA + MLA_KV_LORA), tile(LANES), tile(LANES), tile(LANES), tile(LANES),
                  full(nq), full(nkv), full(wq), full(wqs), full(wk), full(wv)],
        out_specs=[tile(hp), tile(hp), pl.BlockSpec((GROUP_WIDTH, tm), lambda i: (0, i))],
        out_shape=[jax.ShapeDtypeStruct((t, hp), BF16), jax.ShapeDtypeStruct((t, hp), BF16),
                   jax.ShapeDtypeStruct((GROUP_WIDTH, t), BF16)],
        compiler_params=_cparams("parallel"),
        name="mla_prep",
    )(pd, misc, misc2, cos, sin, nq, nkv, wq, wqs, wk, wv)


def _attn_kernel(*refs, fox, tq):
    if fox:
        tref_ref, q_ref, k_ref, vt_ref, o_ref = refs
    else:
        q_ref, k_ref, vt_ref, o_ref = refs
        tref_ref = None
    b = pl.program_id(0)
    i = pl.program_id(1)
    key = _row_iota((tq, tq))
    qry = _lane_iota((tq, tq))
    if fox:
        allowed = key <= qry
    else:
        shift = int(math.log2(MLA_CHUNK))
        allowed = (key >> shift) <= (qry >> shift)
    qs = [q_ref[:, h * LANES:(h + 1) * LANES] for h in range(N_HEADS)]

    def step(j, masked, carry):
        rk = pl.ds(pl.multiple_of(j * tq, tq), tq)
        scores = [lax.dot_general(k_ref[rk, h * LANES:(h + 1) * LANES], qs[h], (((1,), (1,)), ((), ())),
                                  preferred_element_type=F32) for h in range(N_HEADS)]
        probs = []
        for h in range(N_HEADS):
            m, l, _ = carry[h]
            s = scores[h]
            if masked:
                s = jnp.where(allowed, s, -1e30)
            delta = (tref_ref[b, i, h] - tref_ref[b, j, h]) if fox else 0.0
            m_new = jnp.maximum(m, jnp.max(s, axis=0, keepdims=True) + delta)
            alpha = jnp.exp2(m - m_new)
            p = jnp.exp2(s - (m_new - delta))
            l_new = alpha * l + jnp.sum(p, axis=0, keepdims=True)
            probs.append((m_new, l_new, alpha, p.astype(BF16)))
        new = []
        for h in range(N_HEADS):
            pair = h // 2
            m_new, l_new, alpha, p = probs[h]
            pv = jnp.dot(vt_ref[pair * LANES:(pair + 1) * LANES, rk], p, preferred_element_type=F32)
            new.append((m_new, l_new, alpha * carry[h][2] + pv))
        return tuple(new)

    init = tuple((jnp.full((1, tq), -1e30, F32), jnp.zeros((1, tq), F32), jnp.zeros((LANES, tq), F32))
                 for _ in range(N_HEADS))
    carry = lax.fori_loop(0, i, lambda j, c: step(j, False, c), init)
    carry = step(i, True, carry)
    outs = [acc / l for (_, l, acc) in carry]
    top = _row_iota((LANES, tq)) < HEAD_DIM
    o_t = jnp.concatenate([jnp.where(top, outs[0], outs[1]), jnp.where(top, outs[2], outs[3])], axis=0)
    o_ref[...] = o_t.T.astype(o_ref.dtype)


def _attention(q, k, vt, tref, batch, seq, tq, name):
    nq = seq // tq
    fox = tref is not None
    kern = functools.partial(_attn_kernel, fox=fox, tq=tq)
    grid_spec = pltpu.PrefetchScalarGridSpec(
        num_scalar_prefetch=1 if fox else 0,
        grid=(batch, nq),
        in_specs=[pl.BlockSpec((tq, HEAD_PAD), lambda b, i, *_: (b * nq + i, 0)),
                  pl.BlockSpec((seq, HEAD_PAD), lambda b, i, *_: (b, 0)),
                  pl.BlockSpec((GROUP_WIDTH, seq), lambda b, i, *_: (0, b))],
        out_specs=pl.BlockSpec((tq, GROUP_WIDTH), lambda b, i, *_: (b * nq + i, 0)),
    )
    args = ((tref,) if fox else ()) + (q, k, vt)
    return pl.pallas_call(
        kern,
        grid_spec=grid_spec,
        out_shape=jax.ShapeDtypeStruct((batch * seq, GROUP_WIDTH), BF16),
        compiler_params=_cparams("parallel", "arbitrary"),
        name=name,
    )(*args)


def _outproj_kernel(x_ref, ya, yb, yc, yd, w_ref, g_ref, wr_ref, br_ref,
                    x2_ref, h2_ref, rcol_ref, rrow_ref):
    y = jnp.concatenate([ya[...], yb[...], yc[...], yd[...]], axis=1)
    x2 = x_ref[...] + jnp.dot(y, w_ref[...], preferred_element_type=F32)
    x2_ref[...] = x2
    h2 = _rms(x2, g_ref[...])
    h2_ref[...] = h2
    logits = jnp.dot(h2, wr_ref[...], preferred_element_type=F32,
                     precision=lax.Precision.HIGHEST) + br_ref[...]
    lane = _lane_iota(logits.shape)
    neg = -1e30
    big = 1 << 20
    gmask = lane < N_EXPERT_GROUPS
    gl = jnp.where(gmask, logits, neg)
    gmax = jnp.max(gl, axis=-1, keepdims=True)
    gsum = jnp.sum(jnp.where(gmask, jnp.exp(gl - gmax), 0.0), axis=-1, keepdims=True)
    g_w = 1.0 / gsum
    g_idx = jnp.min(jnp.where(gmask & (gl == gmax), lane, big), axis=-1, keepdims=True)
    e_local = lane - N_EXPERT_GROUPS
    emask = (e_local >= 0) & (e_local < N_EXPERTS) & ((e_local >> int(math.log2(EXPERTS_PER_GROUP))) == g_idx)
    el = jnp.where(emask, logits, neg)
    e1v = jnp.max(el, axis=-1, keepdims=True)
    esum = jnp.sum(jnp.where(emask, jnp.exp(el - e1v), 0.0), axis=-1, keepdims=True)
    i1 = jnp.min(jnp.where(emask & (el == e1v), lane, big), axis=-1, keepdims=True)
    el2 = jnp.where(lane == i1, neg, el)
    e2v = jnp.max(el2, axis=-1, keepdims=True)
    i2 = jnp.min(jnp.where(emask & (lane != i1) & (el2 == e2v), lane, big), axis=-1, keepdims=True)
    p1 = 1.0 / esum
    p2 = jnp.exp(e2v - e1v) / esum
    w1 = g_w * (p1 / (p1 + p2))
    w2 = g_w * (p2 / (p1 + p2))
    rt = jnp.where(lane == 0, (i1 - N_EXPERT_GROUPS).astype(F32),
                   jnp.where(lane == 1, (i2 - N_EXPERT_GROUPS).astype(F32),
                             jnp.where(lane == 2, w1, jnp.where(lane == 3, w2, 0.0))))
    rcol_ref[...] = rt
    rrow_ref[...] = rt.T[:8]


def _outproj(x, ya, yb, yc, yd, w, g, wr, br, tm):
    t = x.shape[0]
    full = lambda a: pl.BlockSpec(a.shape, lambda i: (0, 0))
    tile = lambda wd: pl.BlockSpec((tm, wd), lambda i: (i, 0))
    return pl.pallas_call(
        _outproj_kernel,
        grid=(t // tm,),
        in_specs=[tile(D_MODEL)] + [tile(GROUP_WIDTH)] * 4 + [full(w), full(g), full(wr), full(br)],
        out_specs=[tile(D_MODEL), tile(D_MODEL), tile(LANES), pl.BlockSpec((8, tm), lambda i: (0, i))],
        out_shape=[jax.ShapeDtypeStruct((t, D_MODEL), F32), jax.ShapeDtypeStruct((t, D_MODEL), F32),
                   jax.ShapeDtypeStruct((t, LANES), F32), jax.ShapeDtypeStruct((8, t), F32)],
        compiler_params=_cparams("parallel"),
        name="outproj_router",
    )(x, ya, yb, yc, yd, w, g, wr, br)


def _positions_kernel(r_ref, dest_ref, meta_ref, cnt_ref, carry_ref, start_ref, *, tm, nbp):
    phase = pl.program_id(0)
    i = pl.program_id(1)
    e_iota = _row_iota((N_EXPERTS, tm))
    e0 = r_ref[0:1, :].astype(I32)
    e1 = r_ref[1:2, :].astype(I32)
    oh0 = e_iota == e0
    oh1 = e_iota == e1
    oh = jnp.where(oh0 | oh1, 1.0, 0.0)

    @pl.when((phase == 0) & (i == 0))
    def _():
        cnt_ref[...] = jnp.zeros_like(cnt_ref)

    @pl.when(phase == 0)
    def _():
        cnt_ref[...] += jnp.sum(oh, axis=-1, keepdims=True)

    @pl.when((phase == 1) & (i == 0))
    def _():
        cnt = cnt_ref[...]
        padded = jnp.floor((cnt + (MOE_ROWS - 1)) * (1.0 / MOE_ROWS)) * MOE_ROWS
        tril = jnp.where(_row_iota((N_EXPERTS, N_EXPERTS)) >= _lane_iota((N_EXPERTS, N_EXPERTS)), 1.0, 0.0)
        pend = jnp.dot(tril, padded, preferred_element_type=F32, precision=lax.Precision.HIGHEST)
        pstart = pend - padded
        start_ref[...] = pstart
        carry_ref[...] = jnp.zeros_like(carry_ref)
        pend_b = jnp.concatenate([pend] * (nbp // LANES), axis=1)
        vend_b = jnp.concatenate([pstart + cnt] * (nbp // LANES), axis=1)
        b0 = (_lane_iota((N_EXPERTS, nbp)) * MOE_ROWS).astype(F32)
        bexp = jnp.sum(jnp.where(pend_b <= b0, 1.0, 0.0), axis=0, keepdims=True)
        bexp = jnp.minimum(bexp, N_EXPERTS - 1.0)
        is_e = _row_iota((N_EXPERTS, nbp)).astype(F32) == bexp
        vend = jnp.sum(jnp.where(is_e, vend_b, 0.0), axis=0, keepdims=True)
        nvalid = jnp.clip(vend - b0[0:1], 0.0, float(MOE_ROWS))
        total = jnp.max(pend_b, axis=0, keepdims=True) * (1.0 / MOE_ROWS)
        row = _row_iota((8, nbp))
        meta = jnp.where(row == 0, bexp, jnp.where(row == 1, nvalid, jnp.where(row == 2, total, 0.0)))
        meta_ref[...] = meta.astype(I32)

    @pl.when(phase == 1)
    def _():
        su = jnp.where(_row_iota((tm, tm)) < _lane_iota((tm, tm)), 1.0, 0.0).astype(BF16)
        before = jnp.dot(oh.astype(BF16), su, preferred_element_type=F32)
        base = start_ref[:, 0:1] + carry_ref[:, 0:1] + before
        d0 = jnp.sum(jnp.where(oh0, base, 0.0), axis=0, keepdims=True)
        d1 = jnp.sum(jnp.where(oh1, base, 0.0), axis=0, keepdims=True)
        dest_ref[0, 0:1, :] = d0.astype(I32)
        dest_ref[0, 1:2, :] = d1.astype(I32)
        carry_ref[...] += jnp.sum(oh, axis=-1, keepdims=True)


def _positions(rrow, tm, nbp):
    t = rrow.shape[1]
    nt = t // tm
    kern = functools.partial(_positions_kernel, tm=tm, nbp=nbp)
    return pl.pallas_call(
        kern,
        grid=(2, nt),
        in_specs=[pl.BlockSpec((8, tm), lambda p, i: (0, i))],
        out_specs=[pl.BlockSpec((1, 2, tm), lambda p, i: (i * p, 0, 0)),
                   pl.BlockSpec((8, nbp), lambda p, i: (0, 0))],
        out_shape=[jax.ShapeDtypeStruct((nt, 2, tm), I32), jax.ShapeDtypeStruct((8, nbp), I32)],
        scratch_shapes=[pltpu.VMEM((N_EXPERTS, LANES), F32)] * 3,
        compiler_params=_cparams("arbitrary", "arbitrary"),
        name="moe_positions",
    )(rrow)


def _dispatch_kernel(meta_ref, dest_ref, h_ref, xs_ref, zero_ref, sem, zsem, *, tm, nb):
    i = pl.program_id(0)

    @pl.when(i == 0)
    def _():
        zero_ref[...] = jnp.zeros_like(zero_ref)
        n_used = meta_ref[2, 0]

        def zcopy(b):
            return pltpu.make_async_copy(zero_ref, xs_ref.at[pl.ds(b * MOE_ROWS, MOE_ROWS)], zsem)

        def needs(b):
            return (b < n_used) & (meta_ref[1, b] < MOE_ROWS)

        def start(b, c):
            @pl.when(needs(b))
            def _():
                zcopy(b).start()
            return c

        def wait(b, c):
            @pl.when(needs(b))
            def _():
                zcopy(b).wait()
            return c

        lax.fori_loop(0, nb, start, 0)
        lax.fori_loop(0, nb, wait, 0)

    def copy(t, k):
        return pltpu.make_async_copy(h_ref.at[pl.ds(t, 1)], xs_ref.at[pl.ds(dest_ref[0, k, t], 1)], sem)

    def start(t, c):
        copy(t, 0).start()
        copy(t, 1).start()
        return c

    def wait(t, c):
        copy(t, 0).wait()
        copy(t, 1).wait()
        return c

    lax.fori_loop(0, tm, start, 0)
    lax.fori_loop(0, tm, wait, 0)


def _dispatch(meta, dest, h2, tm, nb):
    t = h2.shape[0]
    kern = functools.partial(_dispatch_kernel, tm=tm, nb=nb)
    grid_spec = pltpu.PrefetchScalarGridSpec(
        num_scalar_prefetch=1,
        grid=(t // tm,),
        in_specs=[pl.BlockSpec((1, 2, tm), lambda i, m: (i, 0, 0), memory_space=pltpu.SMEM),
                  pl.BlockSpec((tm, D_MODEL), lambda i, m: (i, 0))],
        out_specs=pl.BlockSpec(memory_space=pl.ANY),
        scratch_shapes=[pltpu.VMEM((MOE_ROWS, D_MODEL), F32),
                        pltpu.SemaphoreType.DMA, pltpu.SemaphoreType.DMA],
    )
    return pl.pallas_call(
        kern,
        grid_spec=grid_spec,
        out_shape=jax.ShapeDtypeStruct((nb * MOE_ROWS, D_MODEL), F32),
        compiler_params=_cparams("arbitrary"),
        name="moe_dispatch",
    )(meta, dest, h2)


def _expert_kernel(meta_ref, x_ref, wg_ref, wu_ref, wd_ref, o_ref):
    b = pl.program_id(0)

    @pl.when(b < meta_ref[2, 0])
    def _():
        x = x_ref[...].astype(BF16)
        gate = jnp.dot(x, wg_ref[0, 0].astype(BF16), preferred_element_type=F32)
        up = jnp.dot(x, wu_ref[0, 0].astype(BF16), preferred_element_type=F32)
        act = (_silu(gate) * up).astype(BF16)
        o_ref[...] = jnp.dot(act, wd_ref[0, 0].astype(BF16), preferred_element_type=F32)


def _experts(meta, xs, wg, wu, wd, layer, nb):
    def blk(b, m):
        return (jnp.minimum(b, m[2, 0] - 1), 0)

    def wblk(b, m):
        return (layer, m[0, jnp.minimum(b, m[2, 0] - 1)], 0, 0)

    grid_spec = pltpu.PrefetchScalarGridSpec(
        num_scalar_prefetch=1,
        grid=(nb,),
        in_specs=[pl.BlockSpec((MOE_ROWS, D_MODEL), blk),
                  pl.BlockSpec((1, 1, D_MODEL, EXPERT_FF), wblk),
                  pl.BlockSpec((1, 1, D_MODEL, EXPERT_FF), wblk),
                  pl.BlockSpec((1, 1, EXPERT_FF, D_MODEL), wblk)],
        out_specs=pl.BlockSpec((MOE_ROWS, D_MODEL), blk),
    )
    return pl.pallas_call(
        _expert_kernel,
        grid_spec=grid_spec,
        out_shape=jax.ShapeDtypeStruct((nb * MOE_ROWS, D_MODEL), F32),
        compiler_params=_cparams("arbitrary"),
        name="moe_experts",
    )(meta, xs, wg, wu, wd)


def _combine_kernel(dest_ref, x_ref, r_ref, g_ref, ys_ref, o_ref, buf_ref, sem, *, tm, final):
    def copy(t, k):
        return pltpu.make_async_copy(ys_ref.at[pl.ds(dest_ref[0, k, t], 1)],
                                     buf_ref.at[k, pl.ds(t, 1)], sem)

    def start(t, c):
        copy(t, 0).start()
        copy(t, 1).start()
        return c

    def wait(t, c):
        copy(t, 0).wait()
        copy(t, 1).wait()
        return c

    lax.fori_loop(0, tm, start, 0)
    lax.fori_loop(0, tm, wait, 0)
    w0 = r_ref[:, 2:3]
    w1 = r_ref[:, 3:4]
    x = x_ref[...] + (buf_ref[0] * w0 + buf_ref[1] * w1)
    o_ref[...] = _rms(x, g_ref[...]) if final else x


def _combine(dest, x2, rcol, g, ys, tm, final):
    t = x2.shape[0]
    kern = functools.partial(_combine_kernel, tm=tm, final=final)
    return pl.pallas_call(
        kern,
        grid=(t // tm,),
        in_specs=[pl.BlockSpec((1, 2, tm), lambda i: (i, 0, 0), memory_space=pltpu.SMEM),
                  pl.BlockSpec((tm, D_MODEL), lambda i: (i, 0)),
                  pl.BlockSpec((tm, LANES), lambda i: (i, 0)),
                  pl.BlockSpec((1, D_MODEL), lambda i: (0, 0)),
                  pl.BlockSpec(memory_space=pl.ANY)],
        out_specs=pl.BlockSpec((tm, D_MODEL), lambda i: (i, 0)),
        out_shape=jax.ShapeDtypeStruct((t, D_MODEL), F32),
        scratch_shapes=[pltpu.VMEM((2, tm, D_MODEL), F32), pltpu.SemaphoreType.DMA],
        compiler_params=_cparams("arbitrary"),
        name="moe_combine",
    )(dest, x2, rcol, g, ys)


def _pad_rows(a, rows=8):
    return jnp.zeros((rows, a.shape[-1]), F32).at[:a.shape[0]].set(a.astype(F32))


def _arrange_w_in(w):
    gw = GROUP_WIDTH
    a0 = 0
    b0 = 3 * gw
    c0 = b0 + 3 * gw + N_HEADS
    d0 = c0 + gw + (gw + 4 * SSM_STATE) + N_HEADS
    half = MLA_ROPE // 2
    f_logit = w[:, b0 + 3 * gw:b0 + 3 * gw + N_HEADS]
    dt_raw = w[:, c0 + 3 * gw:c0 + 3 * gw + N_HEADS]
    kr0 = d0 + MLA_Q_LORA + MLA_KV_LORA
    kr = w[:, kr0:kr0 + MLA_ROPE]
    kr_sw = jnp.concatenate([-kr[:, half:], kr[:, :half]], axis=1)
    zeros = lambda n: jnp.zeros((w.shape[0], n), w.dtype)
    misc = jnp.concatenate([f_logit, dt_raw, zeros(MISC_ROPE - 2 * N_HEADS), kr,
                            zeros(LANES - MISC_ROPE - MLA_ROPE)], axis=1)
    misc2 = jnp.concatenate([zeros(MISC_ROPE), kr_sw, zeros(LANES - MISC_ROPE - MLA_ROPE)], axis=1)
    fox_q = w[:, b0:b0 + gw] * (HEAD_DIM ** -0.5 * LOG2E)
    out = jnp.concatenate([w[:, a0:a0 + 3 * gw], fox_q, w[:, b0 + gw:b0 + 2 * gw], w[:, c0:c0 + 3 * gw],
                           w[:, d0:d0 + MLA_Q_LORA + MLA_KV_LORA], misc, misc2], axis=1)
    fox_vt = w[:, b0 + 2 * gw:b0 + 3 * gw].T
    return out.astype(BF16), fox_vt.astype(BF16)


def _arrange_mla(w_uq, w_ukv):
    half = MLA_ROPE // 2
    qd = MLA_NOPE + MLA_ROPE
    wq, wqs, wk, wv = [], [], [], []
    zq = jnp.zeros((MLA_Q_LORA, LANES - qd), w_uq.dtype)
    zk = jnp.zeros((MLA_KV_LORA, LANES - MLA_NOPE), w_ukv.dtype)
    for h in range(N_HEADS):
        q = w_uq[:, h * qd:(h + 1) * qd]
        nope, rope = q[:, :MLA_NOPE], q[:, MLA_NOPE:]
        wq.append(jnp.concatenate([nope, rope, zq], axis=1))
        wqs.append(jnp.concatenate([jnp.zeros_like(nope), -rope[:, half:], rope[:, :half], zq], axis=1))
        kv = w_ukv[:, h * 2 * MLA_NOPE:(h + 1) * 2 * MLA_NOPE]
        wk.append(jnp.concatenate([kv[:, :MLA_NOPE], zk], axis=1))
        wv.append(kv[:, MLA_NOPE:])
    cat = lambda xs: jnp.concatenate(xs, axis=1).astype(BF16)
    return cat(wq), cat(wqs), cat(wk), cat(wv).T


def kernel(x, positions, norm_mix, w_in, conv_a, fox_forget_bias, ssm_conv_w, ssm_conv_b, ssm_dt_bias,
           ssm_a_log, ssm_d, ssm_norm, mla_q_norm, mla_kv_norm, mla_w_uq, mla_w_ukv, w_out, norm_ffn,
           router_group_w, router_group_b, router_expert_w, router_expert_b, expert_w_gate, expert_w_up,
           expert_w_down, norm_final):
    batch, seq, d = x.shape
    t = batch * seq
    depth = w_in.shape[0]
    tm = min(512, t)
    tq = min(ATTN_TQ, seq)
    tmd = min(256, t)
    n_slots = 2 * t
    nb = -(-(n_slots + N_EXPERTS * (MOE_ROWS - 1)) // MOE_ROWS)
    nbp = -(-nb // LANES) * LANES

    xf = x.reshape(t, d)
    pos_col = positions.astype(F32).reshape(t, 1)
    cos, sin = _rope_tables(pos_col, tm)

    for l in range(depth):
        w_in_p, w_vt = _arrange_w_in(w_in[l])
        pa, pb, pc, pd, misc, misc2, fox_vt = _inproj(xf, norm_mix[l][None, :], w_in_p, w_vt, tm)

        sp = jnp.zeros((8, LANES), F32)
        sp = sp.at[0, MISC_F:MISC_F + N_HEADS].set(fox_forget_bias[l])
        sp = sp.at[0, MISC_DT:MISC_DT + N_HEADS].set(ssm_dt_bias[l])
        sp = sp.at[1, MISC_DT:MISC_DT + N_HEADS].set(ssm_a_log[l])
        col, rows, fox_q, fox_k, tref = _scalar_prep(misc, sp, pb, batch, seq, tq)

        ya = _conv_mixer(pa, _pad_rows(conv_a[l]), batch, seq)
        yb = _attention(fox_q, fox_k, fox_vt, tref, batch, seq, tq, "fox_attention")
        conv_wb = _pad_rows(jnp.concatenate([ssm_conv_w[l], ssm_conv_b[l][None, :]], axis=0))
        ssd_par = _pad_rows(jnp.stack([jnp.repeat(ssm_d[l], HEAD_DIM), ssm_norm[l]]))
        yc = _ssd_mixer(pc, col, rows, conv_wb, ssd_par, batch, seq)
        wq, wqs, wk, wv = _arrange_mla(mla_w_uq[l], mla_w_ukv[l])
        q, k, v = _mla_prep(pd, misc, misc2, cos, sin, mla_q_norm[l][None, :], mla_kv_norm[l][None, :],
                            wq, wqs, wk, wv, tm)
        yd = _attention(q, k, v, None, batch, seq, tq, "mla_attention")

        wr = jnp.zeros((d, LANES), F32)
        wr = wr.at[:, :N_EXPERT_GROUPS].set(router_group_w[l])
        wr = wr.at[:, N_EXPERT_GROUPS:N_EXPERT_GROUPS + N_EXPERTS].set(router_expert_w[l])
        br = jnp.zeros((1, LANES), F32)
        br = br.at[0, :N_EXPERT_GROUPS].set(router_group_b[l])
        br = br.at[0, N_EXPERT_GROUPS:N_EXPERT_GROUPS + N_EXPERTS].set(router_expert_b[l])
        x2, h2, rcol, rrow = _outproj(xf, ya, yb, yc, yd, w_out[l].astype(BF16), norm_ffn[l][None, :],
                                      wr, br, tm)

        dest, meta = _positions(rrow, tm, nbp)
        dest = _retile(dest, tm, tmd)
        xs = _dispatch(meta, dest, h2, tmd, nb)
        ys = _experts(meta, xs, expert_w_gate, expert_w_up, expert_w_down, l, nb)
        final = l == depth - 1
        xf = _combine(dest, x2, rcol, norm_final[None, :], ys, tmd, final)

    return xf.reshape(batch, seq, d)


def _retile(dest, tm, tmd):
    if tm == tmd:
        return dest
    nt = dest.shape[0]
    return dest.reshape(nt, 2, tm // tmd, tmd).transpose(0, 2, 1, 3).reshape(nt * (tm // tmd), 2, tmd)
```

```python
import functools
import math

import jax
import jax.numpy as jnp
import numpy as np
from jax import lax
from jax.experimental import pallas as pl
from jax.experimental.pallas import tpu as pltpu

F32 = jnp.float32
BF16 = jnp.bfloat16
I32 = jnp.int32

LANES = 128
VMEM_LIMIT_BYTES = 56 * 1024 * 1024

D_MODEL = 1024
RMS_EPS = 1e-6
LOG2E = math.log2(math.e)
GROUP_WIDTH = 256
HEAD_DIM = 64
N_HEADS = 4

CONV_A_WIDTH = 3
SSM_CONV = 4
SSM_STATE = 64
SSM_CHUNK = 256

MLA_NOPE = 64
MLA_ROPE = 32
MLA_Q_LORA = 256
MLA_KV_LORA = 128
ROPE_BASE = 10000.0
MLA_CHUNK = 64
ATTN_TQ = 512
ATTN_RB = 128

N_EXPERT_GROUPS = 4
EXPERTS_PER_GROUP = 8
N_EXPERTS = 32
EXPERT_FF = 256
MOE_ROWS = 256

SEG_A = (0, 768)
SEG_B = (768, 1280)
SEG_C = (1280, 2048)
SEG_D = (2048, 2432)
SEG_M = (2432, 2560)
SEG_M2 = (2560, 2688)
IN_COLS_PADDED = 2688
HEAD_PAD = N_HEADS * LANES
AUG_LANE = HEAD_DIM
MISC_F = 0
MISC_DT = 4
MISC_ROPE = 64
COL_CUMF = 0
COL_DT = 4
COL_ACUM = 8
N_SCALAR_ROWS = 16


def _cparams(*sem):
    return pltpu.CompilerParams(dimension_semantics=sem, vmem_limit_bytes=VMEM_LIMIT_BYTES)


def _lane_iota(shape):
    return lax.broadcasted_iota(I32, shape, len(shape) - 1)


def _row_iota(shape):
    return lax.broadcasted_iota(I32, shape, 0)


def _rms(x, g):
    ms = jnp.mean(x * x, axis=-1, keepdims=True)
    return x * lax.rsqrt(ms + RMS_EPS) * g


def _silu(x):
    return x / (1.0 + jnp.exp(-x))


def _softplus(x):
    return jnp.maximum(x, 0.0) + jnp.log(1.0 + jnp.exp(-jnp.abs(x)))


def _shift_rows(x, k):
    rolled = pltpu.roll(x, k, 0)
    return jnp.where(_row_iota(x.shape) >= k, rolled, 0.0)


def _rope_kernel(pos_ref, freq_ref, cos_ref, sin_ref):
    ang = pos_ref[...] * freq_ref[...]
    lane = _lane_iota(ang.shape)
    rope = (lane >= MISC_ROPE) & (lane < MISC_ROPE + MLA_ROPE)
    cos_ref[...] = jnp.where(rope, jnp.cos(ang), jnp.where(lane < MISC_ROPE, 1.0, 0.0))
    sin_ref[...] = jnp.where(rope, jnp.sin(ang), 0.0)


def _rope_tables(pos_col, tm):
    t = pos_col.shape[0]
    half = MLA_ROPE // 2
    inv = ROPE_BASE ** (-np.arange(0, MLA_ROPE, 2, dtype=np.float32) / MLA_ROPE)
    freq = np.zeros((1, LANES), np.float32)
    freq[0, MISC_ROPE:MISC_ROPE + half] = inv
    freq[0, MISC_ROPE + half:MISC_ROPE + MLA_ROPE] = inv
    return pl.pallas_call(
        _rope_kernel,
        grid=(t // tm,),
        in_specs=[pl.BlockSpec((tm, 1), lambda i: (i, 0)),
                  pl.BlockSpec((1, LANES), lambda i: (0, 0))],
        out_specs=[pl.BlockSpec((tm, LANES), lambda i: (i, 0))] * 2,
        out_shape=[jax.ShapeDtypeStruct((t, LANES), F32)] * 2,
        compiler_params=_cparams("parallel"),
        name="rope_tables",
    )(pos_col, jnp.asarray(freq))


def _inproj_kernel(x_ref, g_ref, w_ref, wvt_ref, oa, ob, oc, od, om, om2, ovt):
    h = _rms(x_ref[...], g_ref[...]).astype(BF16)
    for o, (lo, hi) in ((oa, SEG_A), (ob, SEG_B), (oc, SEG_C), (od, SEG_D), (om, SEG_M), (om2, SEG_M2)):
        o[...] = jnp.dot(h, w_ref[:, lo:hi], preferred_element_type=F32).astype(o.dtype)
    ovt[...] = lax.dot_general(wvt_ref[...], h, (((1,), (1,)), ((), ())),
                               preferred_element_type=F32).astype(ovt.dtype)


def _inproj(x, g, w, wvt, tm):
    t = x.shape[0]
    widths = [(s[1] - s[0]) for s in (SEG_A, SEG_B, SEG_C, SEG_D, SEG_M, SEG_M2)]
    dtypes = [BF16, BF16, BF16, BF16, F32, F32]
    return pl.pallas_call(
        _inproj_kernel,
        grid=(t // tm,),
        in_specs=[pl.BlockSpec((tm, D_MODEL), lambda i: (i, 0)),
                  pl.BlockSpec((1, D_MODEL), lambda i: (0, 0)),
                  pl.BlockSpec((D_MODEL, IN_COLS_PADDED), lambda i: (0, 0)),
                  pl.BlockSpec((GROUP_WIDTH, D_MODEL), lambda i: (0, 0))],
        out_specs=[pl.BlockSpec((tm, wd), lambda i: (i, 0)) for wd in widths]
        + [pl.BlockSpec((GROUP_WIDTH, tm), lambda i: (0, i))],
        out_shape=[jax.ShapeDtypeStruct((t, wd), dt) for wd, dt in zip(widths, dtypes)]
        + [jax.ShapeDtypeStruct((GROUP_WIDTH, t), BF16)],
        compiler_params=_cparams("parallel"),
        name="inproj",
    )(x, g, w, wvt)


def _scalar_prep_kernel(m_ref, p_ref, qk_ref, sel_ref, place_ref, const_ref,
                        col_ref, row_ref, qa_ref, ka_ref, tref_ref, *, tq):
    s = m_ref.shape[0]
    tref_ref[...] = jnp.zeros_like(tref_ref)
    tile_ref = jnp.zeros((1, LANES), F32)
    m = m_ref[...]
    bias = p_ref[0:1, :]
    a_log = p_ref[1:2, :]
    lane = _lane_iota(m.shape)
    z = m + bias
    logf = jnp.minimum(z, 0.0) - jnp.log(1.0 + jnp.exp(-jnp.abs(z)))
    dt = _softplus(z)
    a = dt * (-jnp.exp(a_log))
    is_f = lane < MISC_DT
    is_dt = (lane >= MISC_DT) & (lane < MISC_DT + N_HEADS)
    v = jnp.where(is_f, logf, jnp.where(is_dt, a, 0.0))
    r = _row_iota((SSM_CHUNK, SSM_CHUNK))
    c = _lane_iota((SSM_CHUNK, SSM_CHUNK))
    tril = jnp.where(r >= c, 1.0, 0.0).astype(F32)
    carry = jnp.zeros((1, LANES), F32)
    lane_1 = _lane_iota((1, LANES))
    lane_b = _lane_iota((SSM_CHUNK, LANES))
    for ci in range(s // SSM_CHUNK):
        blk = v[ci * SSM_CHUNK:(ci + 1) * SSM_CHUNK]
        cs = jnp.dot(tril, blk, preferred_element_type=F32, precision=lax.Precision.HIGHEST)
        cs = cs + jnp.where(lane_1 < MISC_DT, carry, 0.0)
        carry = cs[SSM_CHUNK - 1:SSM_CHUNK]
        acum = pltpu.roll(cs, COL_ACUM - MISC_DT, 1)
        out = jnp.where(lane_b < MISC_DT, cs * LOG2E,
                        jnp.where(lane_b < COL_ACUM, dt[ci * SSM_CHUNK:(ci + 1) * SSM_CHUNK],
                                  jnp.where(lane_b < COL_ACUM + N_HEADS, acum, 0.0)))
        rows = slice(ci * SSM_CHUNK, (ci + 1) * SSM_CHUNK)
        col_ref[rows, :] = out
        row_ref[0, :, rows] = out.T[:N_SCALAR_ROWS]
        if (ci * SSM_CHUNK) % tq == 0:
            tile_ref = out[0:1, :]
            ti = (ci * SSM_CHUNK) // tq
            tref_ref[0, ti:ti + 1, :] = tile_ref
        c = out - tile_ref
        c_hi = c.astype(BF16)
        r1 = c - c_hi.astype(F32)
        c_mid = r1.astype(BF16)
        c_lo = (r1 - c_mid.astype(F32)).astype(BF16)
        for o_ref, base, qk_lo in ((qa_ref, 0, 0), (ka_ref, 3, GROUP_WIDTH)):
            aug = jnp.dot(qk_ref[rows, qk_lo:qk_lo + GROUP_WIDTH], sel_ref[...], preferred_element_type=F32)
            for term, cc in enumerate((c_hi, c_mid, c_lo)):
                aug = aug + jnp.dot(cc, place_ref[base + term], preferred_element_type=F32)
            o_ref[rows, :] = (aug + const_ref[base // 3:base // 3 + 1, :]).astype(o_ref.dtype)


def _fox_placement():
    sel = np.zeros((GROUP_WIDTH, HEAD_PAD), np.float32)
    place = np.zeros((6, LANES, HEAD_PAD), np.float32)
    const = np.zeros((8, HEAD_PAD), np.float32)
    for h in range(N_HEADS):
        for d in range(HEAD_DIM):
            sel[h * HEAD_DIM + d, h * LANES + d] = 1.0
        a0 = h * LANES + AUG_LANE
        for term in range(3):
            place[term, COL_CUMF + h, a0 + term] = 1.0
            place[3 + term, COL_CUMF + h, a0 + 3 + term] = -1.0
            const[0, a0 + 3 + term] = 1.0
            const[1, a0 + term] = 1.0
    return jnp.asarray(sel, BF16), jnp.asarray(place, BF16), jnp.asarray(const, F32)


def _scalar_prep(misc, params, qk, batch, seq, tq):
    sel, place, const = _fox_placement()
    full = lambda a: pl.BlockSpec(a.shape, lambda b: (0,) * a.ndim)
    return pl.pallas_call(
        functools.partial(_scalar_prep_kernel, tq=tq),
        grid=(batch,),
        in_specs=[pl.BlockSpec((seq, LANES), lambda b: (b, 0)),
                  pl.BlockSpec((8, LANES), lambda b: (0, 0)),
                  pl.BlockSpec((seq, 2 * GROUP_WIDTH), lambda b: (b, 0)),
                  full(sel), full(place), full(const)],
        out_specs=[pl.BlockSpec((seq, LANES), lambda b: (b, 0)),
                   pl.BlockSpec((1, N_SCALAR_ROWS, seq), lambda b: (b, 0, 0)),
                   pl.BlockSpec((seq, HEAD_PAD), lambda b: (b, 0)),
                   pl.BlockSpec((seq, HEAD_PAD), lambda b: (b, 0)),
                   pl.BlockSpec((1, 8, LANES), lambda b: (b, 0, 0))],
        out_shape=[jax.ShapeDtypeStruct((batch * seq, LANES), F32),
                   jax.ShapeDtypeStruct((batch, N_SCALAR_ROWS, seq), F32),
                   jax.ShapeDtypeStruct((batch * seq, HEAD_PAD), BF16),
                   jax.ShapeDtypeStruct((batch * seq, HEAD_PAD), BF16),
                   jax.ShapeDtypeStruct((batch, 8, LANES), F32)],
        compiler_params=_cparams("parallel"),
        name="scalar_prep",
    )(misc, params, qk, sel, place, const)


def _conv_mixer_kernel(p_ref, w_ref, o_ref):
    gw = GROUP_WIDTH
    b_gate = p_ref[:, 0:gw].astype(F32)
    cv = p_ref[:, gw:2 * gw].astype(F32) * p_ref[:, 2 * gw:3 * gw].astype(F32)
    acc = cv * w_ref[CONV_A_WIDTH - 1:CONV_A_WIDTH, :]
    for k in range(1, CONV_A_WIDTH):
        acc = acc + _shift_rows(cv, k) * w_ref[CONV_A_WIDTH - 1 - k:CONV_A_WIDTH - k, :]
    o_ref[...] = (b_gate * acc).astype(o_ref.dtype)


def _conv_mixer(pa, w, batch, seq):
    return pl.pallas_call(
        _conv_mixer_kernel,
        grid=(batch,),
        in_specs=[pl.BlockSpec((seq, 3 * GROUP_WIDTH), lambda b: (b, 0)),
                  pl.BlockSpec((8, GROUP_WIDTH), lambda b: (0, 0))],
        out_specs=pl.BlockSpec((seq, GROUP_WIDTH), lambda b: (b, 0)),
        out_shape=jax.ShapeDtypeStruct((batch * seq, GROUP_WIDTH), BF16),
        compiler_params=_cparams("parallel"),
        name="conv_mixer",
    )(pa, w)


def _pair_lanes(col, base, shape):
    lane = _lane_iota(shape)
    return jnp.where(lane < HEAD_DIM, col[:, base:base + 1], col[:, base + 1:base + 2])


def _ssd_kernel(p_ref, col_ref, row_ref, cw_ref, par_ref, o_ref, u_ref):
    s = p_ref.shape[0]
    q = SSM_CHUNK
    gw = GROUP_WIDTH
    xbc = p_ref[:, gw:3 * gw].astype(F32)
    acc = xbc * cw_ref[SSM_CONV - 1:SSM_CONV, :]
    for k in range(1, SSM_CONV):
        acc = acc + _shift_rows(xbc, k) * cw_ref[SSM_CONV - 1 - k:SSM_CONV - k, :]
    u_ref[...] = _silu(acc + cw_ref[SSM_CONV:SSM_CONV + 1, :])

    d_skip = par_ref[0:1, :]
    norm_g = par_ref[1:2, :]
    lane_q = _lane_iota((q, LANES))
    low = lane_q < HEAD_DIM
    tri = _row_iota((q, q)) >= _lane_iota((q, q))

    def chunk(ci, states):
        r0 = pl.multiple_of(ci * q, q)
        rows = pl.ds(r0, q)
        u = u_ref[rows, :]
        col = col_ref[rows, :]
        bm = u[:, gw:gw + LANES]
        cm = u[:, gw + LANES:gw + 2 * LANES]
        z = p_ref[rows, 0:gw].astype(F32)
        new_states = []
        ys = []
        for g in range(2):
            sel = low if g == 0 else jnp.logical_not(low)
            cg = jnp.where(sel, cm, 0.0).astype(BF16)
            bg = jnp.where(sel, bm, 0.0)
            gmat = lax.dot_general(cg, bm.astype(BF16), (((1,), (1,)), ((), ())),
                                   preferred_element_type=F32)
            xs = u[:, g * LANES:(g + 1) * LANES]
            dt2 = _pair_lanes(col, COL_DT + 2 * g, (q, LANES))
            ac2 = _pair_lanes(col, COL_ACUM + 2 * g, (q, LANES))
            xdt = xs * dt2
            xdt_b = xdt.astype(BF16)
            st = states[g]
            y_off = jnp.dot(cg, st.astype(BF16), preferred_element_type=F32) * jnp.exp(ac2)
            halves = []
            for hh in range(2):
                h = 2 * g + hh
                ac_col = col[:, COL_ACUM + h:COL_ACUM + h + 1]
                ac_row = row_ref[0, COL_ACUM + h:COL_ACUM + h + 1, rows]
                decay = jnp.exp(jnp.where(tri, ac_col - ac_row, -1e30))
                mm = (gmat * decay).astype(BF16)
                halves.append(jnp.dot(mm, xdt_b, preferred_element_type=F32))
            y = jnp.where(low, halves[0], halves[1]) + y_off + d_skip[:, g * LANES:(g + 1) * LANES] * xs
            ys.append(y)
            ac_last = ac2[q - 1:q, :]
            w_end = jnp.exp(ac_last - ac2)
            xw = (xdt * w_end).astype(BF16)
            upd = jnp.dot(bg.T.astype(BF16), xw, preferred_element_type=F32)
            new_states.append(st * jnp.exp(ac_last) + upd)
        yfull = jnp.concatenate(ys, axis=1) * _silu(z)
        o_ref[rows, :] = _rms(yfull, norm_g).astype(o_ref.dtype)
        return tuple(new_states)

    init = (jnp.zeros((LANES, LANES), F32), jnp.zeros((LANES, LANES), F32))
    lax.fori_loop(0, s // q, chunk, init)


def _ssd_mixer(pc, col, rows, conv_wb, par, batch, seq):
    gw = GROUP_WIDTH
    return pl.pallas_call(
        _ssd_kernel,
        grid=(batch,),
        in_specs=[pl.BlockSpec((seq, 3 * gw), lambda b: (b, 0)),
                  pl.BlockSpec((seq, LANES), lambda b: (b, 0)),
                  pl.BlockSpec((1, N_SCALAR_ROWS, seq), lambda b: (b, 0, 0)),
                  pl.BlockSpec((8, 2 * gw), lambda b: (0, 0)),
                  pl.BlockSpec((8, gw), lambda b: (0, 0))],
        out_specs=pl.BlockSpec((seq, gw), lambda b: (b, 0)),
        out_shape=jax.ShapeDtypeStruct((batch * seq, gw), BF16),
        scratch_shapes=[pltpu.VMEM((seq, 2 * gw), F32)],
        compiler_params=_cparams("parallel"),
        name="ssd_mixer",
    )(pc, col, rows, conv_wb, par)


def _mla_prep_kernel(pd_ref, m_ref, m2_ref, cos_ref, sin_ref, nq_ref, nkv_ref,
                     wq_ref, wqs_ref, wk_ref, wvt_ref, q_ref, k_ref, vt_ref):
    cq = _rms(pd_ref[:, 0:MLA_Q_LORA].astype(F32), nq_ref[...]).astype(BF16)
    ckv = _rms(pd_ref[:, MLA_Q_LORA:MLA_Q_LORA + MLA_KV_LORA].astype(F32), nkv_ref[...]).astype(BF16)
    cos = cos_ref[...]
    sin = sin_ref[...]
    cos4 = jnp.concatenate([cos] * N_HEADS, axis=1)
    sin4 = jnp.concatenate([sin] * N_HEADS, axis=1)
    scale = (MLA_NOPE + MLA_ROPE) ** -0.5 * LOG2E
    q = jnp.dot(cq, wq_ref[...], preferred_element_type=F32)
    qs = jnp.dot(cq, wqs_ref[...], preferred_element_type=F32)
    q_ref[...] = ((q * cos4 + qs * sin4) * scale).astype(q_ref.dtype)
    lane = _lane_iota(cos.shape)
    rope = (lane >= MISC_ROPE) & (lane < MISC_ROPE + MLA_ROPE)
    kr = jnp.where(rope, m_ref[...] * cos + m2_ref[...] * sin, 0.0)
    k = jnp.dot(ckv, wk_ref[...], preferred_element_type=F32)
    k_ref[...] = (k + jnp.concatenate([kr] * N_HEADS, axis=1)).astype(k_ref.dtype)
    vt_ref[...] = lax.dot_general(wvt_ref[...], ckv, (((1,), (1,)), ((), ())),
                                  preferred_element_type=F32).astype(vt_ref.dtype)


def _mla_prep(pd, misc, misc2, cos, sin, nq, nkv, wq, wqs, wk, wv, tm):
    t = pd.shape[0]
    hp = N_HEADS * LANES
    full = lambda a: pl.BlockSpec(a.shape, lambda i: (0, 0))
    tile = lambda w: pl.BlockSpec((tm, w), lambda i: (i, 0))
    return pl.pallas_call(
        _mla_prep_kernel,
        grid=(t // tm,),
        in_specs=[tile(MLA_Q_LORA + MLA_KV_LORA), tile(LANES), tile(LANES), tile(LANES), tile(LANES),
                  full(nq), full(nkv), full(wq), full(wqs), full(wk), full(wv)],
        out_specs=[tile(hp), tile(hp), pl.BlockSpec((GROUP_WIDTH, tm), lambda i: (0, i))],
        out_shape=[jax.ShapeDtypeStruct((t, hp), BF16), jax.ShapeDtypeStruct((t, hp), BF16),
                   jax.ShapeDtypeStruct((GROUP_WIDTH, t), BF16)],
        compiler_params=_cparams("parallel"),
        name="mla_prep",
    )(pd, misc, misc2, cos, sin, nq, nkv, wq, wqs, wk, wv)


def _attn_kernel(*refs, fox, tq):
    if fox:
        tref_ref, q_ref, k_ref, vt_ref, o_ref = refs
    else:
        q_ref, k_ref, vt_ref, o_ref = refs
        tref_ref = None
    b = pl.program_id(0)
    i = pl.program_id(1)
    key = _row_iota((tq, tq))
    qry = _lane_iota((tq, tq))
    if fox:
        allowed = key <= qry
    else:
        shift = int(math.log2(MLA_CHUNK))
        allowed = (key >> shift) <= (qry >> shift)
    qs = [q_ref[:, h * LANES:(h + 1) * LANES] for h in range(N_HEADS)]

    def step(j, masked, carry):
        rk = pl.ds(pl.multiple_of(j * tq, tq), tq)
        scores = [lax.dot_general(k_ref[rk, h * LANES:(h + 1) * LANES], qs[h], (((1,), (1,)), ((), ())),
                                  preferred_element_type=F32) for h in range(N_HEADS)]
        probs = []
        for h in range(N_HEADS):
            m, l, _ = carry[h]
            s = scores[h]
            if masked:
                s = jnp.where(allowed, s, -1e30)
            delta = (tref_ref[b, i, h] - tref_ref[b, j, h]) if fox else 0.0
            m_new = jnp.maximum(m, jnp.max(s, axis=0, keepdims=True) + delta)
            alpha = jnp.exp2(m - m_new)
            p = jnp.exp2(s - (m_new - delta))
            l_new = alpha * l + jnp.sum(p, axis=0, keepdims=True)
            probs.append((m_new, l_new, alpha, p.astype(BF16)))
        new = []
        for h in range(N_HEADS):
            pair = h // 2
            m_new, l_new, alpha, p = probs[h]
            pv = jnp.dot(vt_ref[pair * LANES:(pair + 1) * LANES, rk], p, preferred_element_type=F32)
            new.append((m_new, l_new, alpha * carry[h][2] + pv))
        return tuple(new)

    init = tuple((jnp.full((1, tq), -1e30, F32), jnp.zeros((1, tq), F32), jnp.zeros((LANES, tq), F32))
                 for _ in range(N_HEADS))
    carry = lax.fori_loop(0, i, lambda j, c: step(j, False, c), init)
    carry = step(i, True, carry)
    outs = [acc / l for (_, l, acc) in carry]
    top = _row_iota((LANES, tq)) < HEAD_DIM
    o_t = jnp.concatenate([jnp.where(top, outs[0], outs[1]), jnp.where(top, outs[2], outs[3])], axis=0)
    o_ref[...] = o_t.T.astype(o_ref.dtype)


def _attention(q, k, vt, tref, batch, seq, tq, name):
    nq = seq // tq
    fox = tref is not None
    kern = functools.partial(_attn_kernel, fox=fox, tq=tq)
    grid_spec = pltpu.PrefetchScalarGridSpec(
        num_scalar_prefetch=1 if fox else 0,
        grid=(batch, nq),
        in_specs=[pl.BlockSpec((tq, HEAD_PAD), lambda b, i, *_: (b * nq + i, 0)),
                  pl.BlockSpec((seq, HEAD_PAD), lambda b, i, *_: (b, 0)),
                  pl.BlockSpec((GROUP_WIDTH, seq), lambda b, i, *_: (0, b))],
        out_specs=pl.BlockSpec((tq, GROUP_WIDTH), lambda b, i, *_: (b * nq + i, 0)),
    )
    args = ((tref,) if fox else ()) + (q, k, vt)
    return pl.pallas_call(
        kern,
        grid_spec=grid_spec,
        out_shape=jax.ShapeDtypeStruct((batch * seq, GROUP_WIDTH), BF16),
        compiler_params=_cparams("parallel", "arbitrary"),
        name=name,
    )(*args)


def _outproj_kernel(x_ref, ya, yb, yc, yd, w_ref, g_ref, wr_ref, br_ref,
                    x2_ref, h2_ref, rrow_ref):
    y = jnp.concatenate([ya[...], yb[...], yc[...], yd[...]], axis=1)
    x2 = x_ref[...] + jnp.dot(y, w_ref[...], preferred_element_type=F32)
    x2_ref[...] = x2
    h2 = _rms(x2, g_ref[...])
    h2_ref[...] = h2.astype(h2_ref.dtype)
    logits = jnp.dot(h2, wr_ref[...], preferred_element_type=F32,
                     precision=lax.Precision.HIGHEST) + br_ref[...]
    lane = _lane_iota(logits.shape)
    neg = -1e30
    big = 1 << 20
    gmask = lane < N_EXPERT_GROUPS
    gl = jnp.where(gmask, logits, neg)
    gmax = jnp.max(gl, axis=-1, keepdims=True)
    gsum = jnp.sum(jnp.where(gmask, jnp.exp(gl - gmax), 0.0), axis=-1, keepdims=True)
    g_w = 1.0 / gsum
    g_idx = jnp.min(jnp.where(gmask & (gl == gmax), lane, big), axis=-1, keepdims=True)
    e_local = lane - N_EXPERT_GROUPS
    emask = (e_local >= 0) & (e_local < N_EXPERTS) & ((e_local >> int(math.log2(EXPERTS_PER_GROUP))) == g_idx)
    el = jnp.where(emask, logits, neg)
    e1v = jnp.max(el, axis=-1, keepdims=True)
    esum = jnp.sum(jnp.where(emask, jnp.exp(el - e1v), 0.0), axis=-1, keepdims=True)
    i1 = jnp.min(jnp.where(emask & (el == e1v), lane, big), axis=-1, keepdims=True)
    el2 = jnp.where(lane == i1, neg, el)
    e2v = jnp.max(el2, axis=-1, keepdims=True)
    i2 = jnp.min(jnp.where(emask & (lane != i1) & (el2 == e2v), lane, big), axis=-1, keepdims=True)
    p1 = 1.0 / esum
    p2 = jnp.exp(e2v - e1v) / esum
    w1 = g_w * (p1 / (p1 + p2))
    w2 = g_w * (p2 / (p1 + p2))
    rt = jnp.where(lane == 0, (i1 - N_EXPERT_GROUPS).astype(F32),
                   jnp.where(lane == 1, (i2 - N_EXPERT_GROUPS).astype(F32),
                             jnp.where(lane == 2, w1, jnp.where(lane == 3, w2, 0.0))))
    rrow_ref[...] = rt.T[:8]


def _outproj(x, ya, yb, yc, yd, w, g, wr, br, tm):
    t = x.shape[0]
    full = lambda a: pl.BlockSpec(a.shape, lambda i: (0, 0))
    tile = lambda wd: pl.BlockSpec((tm, wd), lambda i: (i, 0))
    return pl.pallas_call(
        _outproj_kernel,
        grid=(t // tm,),
        in_specs=[tile(D_MODEL)] + [tile(GROUP_WIDTH)] * 4 + [full(w), full(g), full(wr), full(br)],
        out_specs=[tile(D_MODEL), tile(D_MODEL), pl.BlockSpec((8, tm), lambda i: (0, i))],
        out_shape=[jax.ShapeDtypeStruct((t, D_MODEL), F32), jax.ShapeDtypeStruct((t, D_MODEL), BF16),
                   jax.ShapeDtypeStruct((8, t), F32)],
        compiler_params=_cparams("parallel"),
        name="outproj_router",
    )(x, ya, yb, yc, yd, w, g, wr, br)


def _positions_kernel(r_ref, dest_ref, meta_ref, cnt_ref, carry_ref, start_ref, *, tm, nbp):
    phase = pl.program_id(0)
    i = pl.program_id(1)
    e_iota = _row_iota((N_EXPERTS, tm))
    e0 = r_ref[0:1, :].astype(I32)
    e1 = r_ref[1:2, :].astype(I32)
    oh0 = e_iota == e0
    oh1 = e_iota == e1
    oh = jnp.where(oh0 | oh1, 1.0, 0.0)

    @pl.when((phase == 0) & (i == 0))
    def _():
        cnt_ref[...] = jnp.zeros_like(cnt_ref)

    @pl.when(phase == 0)
    def _():
        cnt_ref[...] += jnp.sum(oh, axis=-1, keepdims=True)

    @pl.when((phase == 1) & (i == 0))
    def _():
        cnt = cnt_ref[...]
        padded = jnp.floor((cnt + (MOE_ROWS - 1)) * (1.0 / MOE_ROWS)) * MOE_ROWS
        tril = jnp.where(_row_iota((N_EXPERTS, N_EXPERTS)) >= _lane_iota((N_EXPERTS, N_EXPERTS)), 1.0, 0.0)
        pend = jnp.dot(tril, padded, preferred_element_type=F32, precision=lax.Precision.HIGHEST)
        pstart = pend - padded
        start_ref[...] = pstart
        carry_ref[...] = jnp.zeros_like(carry_ref)
        pend_b = jnp.concatenate([pend] * (nbp // LANES), axis=1)
        vend_b = jnp.concatenate([pstart + cnt] * (nbp // LANES), axis=1)
        b0 = (_lane_iota((N_EXPERTS, nbp)) * MOE_ROWS).astype(F32)
        bexp = jnp.sum(jnp.where(pend_b <= b0, 1.0, 0.0), axis=0, keepdims=True)
        bexp = jnp.minimum(bexp, N_EXPERTS - 1.0)
        is_e = _row_iota((N_EXPERTS, nbp)).astype(F32) == bexp
        vend = jnp.sum(jnp.where(is_e, vend_b, 0.0), axis=0, keepdims=True)
        nvalid = jnp.clip(vend - b0[0:1], 0.0, float(MOE_ROWS))
        total = jnp.max(pend_b, axis=0, keepdims=True) * (1.0 / MOE_ROWS)
        row = _row_iota((8, nbp))
        meta = jnp.where(row == 0, bexp, jnp.where(row == 1, nvalid, jnp.where(row == 2, total, 0.0)))
        meta_ref[...] = meta.astype(I32)

    @pl.when(phase == 1)
    def _():
        su = jnp.where(_row_iota((tm, tm)) < _lane_iota((tm, tm)), 1.0, 0.0).astype(BF16)
        before = jnp.dot(oh.astype(BF16), su, preferred_element_type=F32)
        base = start_ref[:, 0:1] + carry_ref[:, 0:1] + before
        d0 = jnp.sum(jnp.where(oh0, base, 0.0), axis=0, keepdims=True)
        d1 = jnp.sum(jnp.where(oh1, base, 0.0), axis=0, keepdims=True)
        dest_ref[0, 0:1, :] = d0.astype(I32)
        dest_ref[0, 1:2, :] = d1.astype(I32)
        carry_ref[...] += jnp.sum(oh, axis=-1, keepdims=True)


def _positions(rrow, tm, nbp):
    t = rrow.shape[1]
    nt = t // tm
    kern = functools.partial(_positions_kernel, tm=tm, nbp=nbp)
    return pl.pallas_call(
        kern,
        grid=(2, nt),
        in_specs=[pl.BlockSpec((8, tm), lambda p, i: (0, i))],
        out_specs=[pl.BlockSpec((1, 2, tm), lambda p, i: (i * p, 0, 0)),
                   pl.BlockSpec((8, nbp), lambda p, i: (0, 0))],
        out_shape=[jax.ShapeDtypeStruct((nt, 2, tm), I32), jax.ShapeDtypeStruct((8, nbp), I32)],
        scratch_shapes=[pltpu.VMEM((N_EXPERTS, LANES), F32)] * 3,
        compiler_params=_cparams("arbitrary", "arbitrary"),
        name="moe_positions",
    )(rrow)


def _dispatch_kernel(meta_ref, dest_ref, h_ref, xs_ref, zero_ref, sem, zsem, *, tm, nb):
    i = pl.program_id(0)

    @pl.when(i == 0)
    def _():
        zero_ref[...] = jnp.zeros_like(zero_ref)
        n_used = meta_ref[2, 0]

        def zcopy(b):
            return pltpu.make_async_copy(zero_ref, xs_ref.at[pl.ds(b * MOE_ROWS, MOE_ROWS)], zsem)

        def needs(b):
            return (b < n_used) & (meta_ref[1, b] < MOE_ROWS)

        def start(b, c):
            @pl.when(needs(b))
            def _():
                zcopy(b).start()
            return c

        def wait(b, c):
            @pl.when(needs(b))
            def _():
                zcopy(b).wait()
            return c

        lax.fori_loop(0, nb, start, 0)
        lax.fori_loop(0, nb, wait, 0)

    def copy(t, k):
        return pltpu.make_async_copy(h_ref.at[pl.ds(t, 1)], xs_ref.at[pl.ds(dest_ref[0, k, t], 1)], sem)

    def start(t, c):
        copy(t, 0).start()
        copy(t, 1).start()
        return c

    def wait(t, c):
        copy(t, 0).wait()
        copy(t, 1).wait()
        return c

    lax.fori_loop(0, tm, start, 0)
    lax.fori_loop(0, tm, wait, 0)


def _dispatch(meta, dest, h2, tm, nb):
    t = h2.shape[0]
    kern = functools.partial(_dispatch_kernel, tm=tm, nb=nb)
    grid_spec = pltpu.PrefetchScalarGridSpec(
        num_scalar_prefetch=1,
        grid=(t // tm,),
        in_specs=[pl.BlockSpec((1, 2, tm), lambda i, m: (i, 0, 0), memory_space=pltpu.SMEM),
                  pl.BlockSpec((tm, D_MODEL), lambda i, m: (i, 0))],
        out_specs=pl.BlockSpec(memory_space=pl.ANY),
        scratch_shapes=[pltpu.VMEM((MOE_ROWS, D_MODEL), F32),
                        pltpu.SemaphoreType.DMA, pltpu.SemaphoreType.DMA],
    )
    return pl.pallas_call(
        kern,
        grid_spec=grid_spec,
        out_shape=jax.ShapeDtypeStruct((nb * MOE_ROWS, D_MODEL), F32),
        compiler_params=_cparams("arbitrary"),
        name="moe_dispatch",
    )(meta, dest, h2)


def _expert_kernel(meta_ref, x_ref, wg_ref, wu_ref, wd_ref, o_ref):
    b = pl.program_id(0)

    @pl.when(b < meta_ref[2, 0])
    def _():
        x = x_ref[...].astype(BF16)
        gate = jnp.dot(x, wg_ref[0, 0].astype(BF16), preferred_element_type=F32)
        up = jnp.dot(x, wu_ref[0, 0].astype(BF16), preferred_element_type=F32)
        act = (_silu(gate) * up).astype(BF16)
        o_ref[...] = jnp.dot(act, wd_ref[0, 0].astype(BF16), preferred_element_type=F32)


def _experts(meta, xs, wg, wu, wd, layer, nb):
    def blk(b, m):
        return (jnp.minimum(b, m[2, 0] - 1), 0)

    def wblk(b, m):
        return (layer, m[0, jnp.minimum(b, m[2, 0] - 1)], 0, 0)

    grid_spec = pltpu.PrefetchScalarGridSpec(
        num_scalar_prefetch=1,
        grid=(nb,),
        in_specs=[pl.BlockSpec((MOE_ROWS, D_MODEL), blk),
                  pl.BlockSpec((1, 1, D_MODEL, EXPERT_FF), wblk),
                  pl.BlockSpec((1, 1, D_MODEL, EXPERT_FF), wblk),
                  pl.BlockSpec((1, 1, EXPERT_FF, D_MODEL), wblk)],
        out_specs=pl.BlockSpec((MOE_ROWS, D_MODEL), blk),
    )
    return pl.pallas_call(
        _expert_kernel,
        grid_spec=grid_spec,
        out_shape=jax.ShapeDtypeStruct((nb * MOE_ROWS, D_MODEL), F32),
        compiler_params=_cparams("arbitrary"),
        name="moe_experts",
    )(meta, xs, wg, wu, wd)


def _combine_kernel(dest_ref, x_ref, r_ref, g_ref, ys_ref, o_ref, buf_ref, sem, *, tm, final):
    def copy(t, k):
        return pltpu.make_async_copy(ys_ref.at[pl.ds(dest_ref[0, k, t], 1)],
                                     buf_ref.at[k, pl.ds(t, 1)], sem)

    def start(t, c):
        copy(t, 0).start()
        copy(t, 1).start()
        return c

    def wait(t, c):
        copy(t, 0).wait()
        copy(t, 1).wait()
        return c

    lax.fori_loop(0, tm, start, 0)
    lax.fori_loop(0, tm, wait, 0)
    w0 = r_ref[:, 2:3]
    w1 = r_ref[:, 3:4]
    x = x_ref[...] + (buf_ref[0] * w0 + buf_ref[1] * w1)
    o_ref[...] = _rms(x, g_ref[...]) if final else x


def _combine(dest, x2, rcol, g, ys, tm, final):
    t = x2.shape[0]
    kern = functools.partial(_combine_kernel, tm=tm, final=final)
    return pl.pallas_call(
        kern,
        grid=(t // tm,),
        in_specs=[pl.BlockSpec((1, 2, tm), lambda i: (i, 0, 0), memory_space=pltpu.SMEM),
                  pl.BlockSpec((tm, D_MODEL), lambda i: (i, 0)),
                  pl.BlockSpec((tm, LANES), lambda i: (i, 0)),
                  pl.BlockSpec((1, D_MODEL), lambda i: (0, 0)),
                  pl.BlockSpec(memory_space=pl.ANY)],
        out_specs=pl.BlockSpec((tm, D_MODEL), lambda i: (i, 0)),
        out_shape=jax.ShapeDtypeStruct((t, D_MODEL), F32),
        scratch_shapes=[pltpu.VMEM((2, tm, D_MODEL), F32), pltpu.SemaphoreType.DMA],
        compiler_params=_cparams("arbitrary"),
        name="moe_combine",
    )(dest, x2, rcol, g, ys)


MOE_TILE = 256
CHUNK = 8
LOCAL_ROWS = 2 * MOE_TILE + 256
XS_WIDTH = D_MODEL + LANES
TAB_CHUNKS, TAB_LOCAL, TAB_GLOBAL = 0, 1, 2


def _route_kernel(r_ref, lrow_ref, lcol_ref, tab_ref, meta_ref, cnt_ref, loff_ref, *, tm, nbp):
    phase = pl.program_id(0)
    i = pl.program_id(1)
    e_iota = _row_iota((N_EXPERTS, tm))
    oh0 = e_iota == r_ref[0:1, :].astype(I32)
    oh1 = e_iota == r_ref[1:2, :].astype(I32)
    oh = jnp.where(oh0 | oh1, 1.0, 0.0)
    tile_lane = _lane_iota((N_EXPERTS, LANES)) == i
    hi = lax.Precision.HIGHEST

    @pl.when((phase == 0) & (i == 0))
    def _():
        cnt_ref[...] = jnp.zeros_like(cnt_ref)

    @pl.when(phase == 0)
    def _():
        cnt_ref[...] = jnp.where(tile_lane, jnp.sum(oh, axis=-1, keepdims=True), cnt_ref[...])

    @pl.when((phase == 1) & (i == 0))
    def _():
        cnt = cnt_ref[...]
        n8 = jnp.floor((cnt + (CHUNK - 1)) * (1.0 / CHUNK)) * CHUNK
        er = _row_iota((N_EXPERTS, N_EXPERTS))
        ec = _lane_iota((N_EXPERTS, N_EXPERTS))
        below = jnp.where(er > ec, 1.0, 0.0)
        loff = jnp.dot(below, n8, preferred_element_type=F32, precision=hi)
        rows_e = jnp.sum(n8, axis=-1, keepdims=True) + jnp.zeros_like(n8)
        padded = jnp.floor((rows_e + (MOE_ROWS - 1)) * (1.0 / MOE_ROWS)) * MOE_ROWS
        e_start = jnp.dot(below, padded, preferred_element_type=F32, precision=hi)
        tr = _row_iota((LANES, LANES))
        tc = _lane_iota((LANES, LANES))
        earlier = jnp.where(tr < tc, 1.0, 0.0)
        goff = e_start + jnp.dot(n8, earlier, preferred_element_type=F32, precision=hi)
        loff_ref[...] = loff
        tab_ref[TAB_CHUNKS] = (n8 * (1.0 / CHUNK)).astype(I32)
        tab_ref[TAB_LOCAL] = loff.astype(I32)
        tab_ref[TAB_GLOBAL] = goff.astype(I32)
        reps = nbp // LANES
        pend_b = jnp.concatenate([e_start + padded] * reps, axis=1)
        vend_b = jnp.concatenate([e_start + rows_e] * reps, axis=1)
        b0 = (_lane_iota((N_EXPERTS, nbp)) * MOE_ROWS).astype(F32)
        bexp = jnp.sum(jnp.where(pend_b <= b0, 1.0, 0.0), axis=0, keepdims=True)
        bexp = jnp.minimum(bexp, N_EXPERTS - 1.0)
        is_e = _row_iota((N_EXPERTS, nbp)).astype(F32) == bexp
        vend = jnp.sum(jnp.where(is_e, vend_b, 0.0), axis=0, keepdims=True)
        nvalid = jnp.clip(vend - b0[0:1], 0.0, float(MOE_ROWS))
        total = jnp.max(pend_b, axis=0, keepdims=True) * (1.0 / MOE_ROWS)
        row = _row_iota((8, nbp))
        meta = jnp.where(row == 0, bexp, jnp.where(row == 1, nvalid, jnp.where(row == 2, total, 0.0)))
        meta_ref[...] = meta.astype(I32)

    @pl.when(phase == 1)
    def _():
        su = jnp.where(_row_iota((tm, tm)) < _lane_iota((tm, tm)), 1.0, 0.0).astype(BF16)
        before = jnp.dot(oh.astype(BF16), su, preferred_element_type=F32)
        base = jnp.sum(jnp.where(tile_lane, loff_ref[...], 0.0), axis=-1, keepdims=True) + before
        d0 = jnp.sum(jnp.where(oh0, base, 0.0), axis=0, keepdims=True)
        d1 = jnp.sum(jnp.where(oh1, base, 0.0), axis=0, keepdims=True)
        lrow_ref[0, 0:1, :] = d0.astype(I32)
        lrow_ref[0, 1:2, :] = d1.astype(I32)
        row = _row_iota((LANES, tm))
        lcol_ref[...] = jnp.where(row == 0, d0, jnp.where(row == 1, d1, 0.0)).T


def _route(rrow, tm, nbp):
    t = rrow.shape[1]
    nt = t // tm
    kern = functools.partial(_route_kernel, tm=tm, nbp=nbp)
    return pl.pallas_call(
        kern,
        grid=(2, nt),
        in_specs=[pl.BlockSpec((8, tm), lambda p, i: (0, i))],
        out_specs=[pl.BlockSpec((1, 2, tm), lambda p, i: (i * p, 0, 0)),
                   pl.BlockSpec((tm, LANES), lambda p, i: (i * p, 0)),
                   pl.BlockSpec((3, N_EXPERTS, LANES), lambda p, i: (0, 0, 0)),
                   pl.BlockSpec((8, nbp), lambda p, i: (0, 0))],
        out_shape=[jax.ShapeDtypeStruct((nt, 2, tm), I32), jax.ShapeDtypeStruct((t, LANES), F32),
                   jax.ShapeDtypeStruct((3, N_EXPERTS, LANES), I32), jax.ShapeDtypeStruct((8, nbp), I32)],
        scratch_shapes=[pltpu.VMEM((N_EXPERTS, LANES), F32)] * 2,
        compiler_params=_cparams("arbitrary", "arbitrary"),
        name="moe_route",
    )(rrow)


def _chunk_copies(tab_ref, i, local_ref, global_ref, sem, to_global, action):
    def run(e, c0):
        n = tab_ref[TAB_CHUNKS, e, i]
        lo = tab_ref[TAB_LOCAL, e, i]
        go = tab_ref[TAB_GLOBAL, e, i]

        def chunk(c, c1):
            lsl = local_ref.at[pl.ds(pl.multiple_of(lo + c * CHUNK, CHUNK), CHUNK)]
            gsl = global_ref.at[pl.ds(pl.multiple_of(go + c * CHUNK, CHUNK), CHUNK)]
            cp = pltpu.make_async_copy(lsl, gsl, sem) if to_global else pltpu.make_async_copy(gsl, lsl, sem)
            getattr(cp, action)()
            return c1

        return lax.fori_loop(0, n, chunk, c0)

    lax.fori_loop(0, N_EXPERTS, run, 0)


def _scatter_kernel(tab_ref, meta_ref, lrow_ref, r_ref, h_ref, xs_ref, buf_ref, zero_ref, sem, zsem, *, tm, nb):
    i = pl.program_id(0)

    @pl.when(i == 0)
    def _():
        zero_ref[...] = jnp.zeros_like(zero_ref)
        n_used = meta_ref[2, 0]

        def zcopy(b):
            return pltpu.make_async_copy(zero_ref, xs_ref.at[pl.ds(b * MOE_ROWS, MOE_ROWS)], zsem)

        def needs(b):
            return (b < n_used) & (meta_ref[1, b] < MOE_ROWS)

        def start(b, c):
            @pl.when(needs(b))
            def _():
                zcopy(b).start()
            return c

        def wait(b, c):
            @pl.when(needs(b))
            def _():
                zcopy(b).wait()
            return c

        lax.fori_loop(0, nb, start, 0)
        lax.fori_loop(0, nb, wait, 0)

    rows = _row_iota((LOCAL_ROWS, tm))
    p0 = rows == lrow_ref[0, 0:1, :]
    p1 = rows == lrow_ref[0, 1:2, :]
    perm = jnp.where(p0 | p1, 1.0, 0.0).astype(BF16)
    buf_ref[:, 0:D_MODEL] = jnp.dot(perm, h_ref[...], preferred_element_type=F32)
    w = jnp.sum(jnp.where(p0, r_ref[2:3, :], 0.0) + jnp.where(p1, r_ref[3:4, :], 0.0), axis=-1, keepdims=True)
    buf_ref[:, D_MODEL:XS_WIDTH] = w + jnp.zeros((LOCAL_ROWS, LANES), F32)
    _chunk_copies(tab_ref, i, buf_ref, xs_ref, sem, True, "start")
    _chunk_copies(tab_ref, i, buf_ref, xs_ref, sem, True, "wait")


def _scatter(tab, meta, lrow, rrow, h2, tm, nb):
    t = h2.shape[0]
    kern = functools.partial(_scatter_kernel, tm=tm, nb=nb)
    grid_spec = pltpu.PrefetchScalarGridSpec(
        num_scalar_prefetch=2,
        grid=(t // tm,),
        in_specs=[pl.BlockSpec((1, 2, tm), lambda i, *_: (i, 0, 0)),
                  pl.BlockSpec((8, tm), lambda i, *_: (0, i)),
                  pl.BlockSpec((tm, D_MODEL), lambda i, *_: (i, 0))],
        out_specs=pl.BlockSpec(memory_space=pl.ANY),
        scratch_shapes=[pltpu.VMEM((LOCAL_ROWS, XS_WIDTH), F32), pltpu.VMEM((MOE_ROWS, XS_WIDTH), F32),
                        pltpu.SemaphoreType.DMA, pltpu.SemaphoreType.DMA],
    )
    return pl.pallas_call(
        kern,
        grid_spec=grid_spec,
        out_shape=jax.ShapeDtypeStruct((nb * MOE_ROWS, XS_WIDTH), F32),
        compiler_params=_cparams("arbitrary"),
        name="moe_scatter",
    )(tab, meta, lrow, rrow, h2)


def _ffn_kernel(meta_ref, x_ref, wg_ref, wu_ref, wd_ref, o_ref):
    b = pl.program_id(0)

    @pl.when(b < meta_ref[2, 0])
    def _():
        x = x_ref[:, 0:D_MODEL].astype(BF16)
        gate = jnp.dot(x, wg_ref[0, 0].astype(BF16), preferred_element_type=F32)
        up = jnp.dot(x, wu_ref[0, 0].astype(BF16), preferred_element_type=F32)
        act = (_silu(gate) * up).astype(BF16)
        y = jnp.dot(act, wd_ref[0, 0].astype(BF16), preferred_element_type=F32)
        o_ref[...] = y * x_ref[:, D_MODEL:D_MODEL + 1]


def _ffn(meta, xs, wg, wu, wd, layer, nb):
    def blk(b, m):
        return (jnp.maximum(jnp.minimum(b, m[2, 0] - 1), 0), 0)

    def wblk(b, m):
        return (layer, m[0, jnp.maximum(jnp.minimum(b, m[2, 0] - 1), 0)], 0, 0)

    grid_spec = pltpu.PrefetchScalarGridSpec(
        num_scalar_prefetch=1,
        grid=(nb,),
        in_specs=[pl.BlockSpec((MOE_ROWS, XS_WIDTH), blk),
                  pl.BlockSpec((1, 1, D_MODEL, EXPERT_FF), wblk),
                  pl.BlockSpec((1, 1, D_MODEL, EXPERT_FF), wblk),
                  pl.BlockSpec((1, 1, EXPERT_FF, D_MODEL), wblk)],
        out_specs=pl.BlockSpec((MOE_ROWS, D_MODEL), blk),
    )
    return pl.pallas_call(
        _ffn_kernel,
        grid_spec=grid_spec,
        out_shape=jax.ShapeDtypeStruct((nb * MOE_ROWS, D_MODEL), F32),
        compiler_params=_cparams("arbitrary"),
        name="moe_experts",
    )(meta, xs, wg, wu, wd)


def _gather_kernel(tab_ref, lcol_ref, x_ref, g_ref, ys_ref, o_ref, buf_ref, sem, *, tm, final):
    i = pl.program_id(0)

    @pl.when(i == 0)
    def _():
        buf_ref[...] = jnp.zeros_like(buf_ref)

    _chunk_copies(tab_ref, i, buf_ref, ys_ref, sem, False, "start")
    _chunk_copies(tab_ref, i, buf_ref, ys_ref, sem, False, "wait")
    col = _lane_iota((tm, LOCAL_ROWS)).astype(F32)
    pick = jnp.where((col == lcol_ref[:, 0:1]) | (col == lcol_ref[:, 1:2]), 1.0, 0.0).astype(BF16)
    x = x_ref[...] + jnp.dot(pick, buf_ref[...].astype(BF16), preferred_element_type=F32)
    o_ref[...] = _rms(x, g_ref[...]) if final else x


def _gather(tab, lcol, x2, g, ys, tm, final):
    t = x2.shape[0]
    kern = functools.partial(_gather_kernel, tm=tm, final=final)
    grid_spec = pltpu.PrefetchScalarGridSpec(
        num_scalar_prefetch=1,
        grid=(t // tm,),
        in_specs=[pl.BlockSpec((tm, LANES), lambda i, *_: (i, 0)),
                  pl.BlockSpec((tm, D_MODEL), lambda i, *_: (i, 0)),
                  pl.BlockSpec((1, D_MODEL), lambda i, *_: (0, 0)),
                  pl.BlockSpec(memory_space=pl.ANY)],
        out_specs=pl.BlockSpec((tm, D_MODEL), lambda i, *_: (i, 0)),
        scratch_shapes=[pltpu.VMEM((LOCAL_ROWS, D_MODEL), F32), pltpu.SemaphoreType.DMA],
    )
    return pl.pallas_call(
        kern,
        grid_spec=grid_spec,
        out_shape=jax.ShapeDtypeStruct((t, D_MODEL), F32),
        compiler_params=_cparams("arbitrary"),
        name="moe_combine",
    )(tab, lcol, x2, g, ys)


def _pad_rows(a, rows=8):
    return jnp.zeros((rows, a.shape[-1]), F32).at[:a.shape[0]].set(a.astype(F32))


def _arrange_w_in(w):
    gw = GROUP_WIDTH
    a0 = 0
    b0 = 3 * gw
    c0 = b0 + 3 * gw + N_HEADS
    d0 = c0 + gw + (gw + 4 * SSM_STATE) + N_HEADS
    half = MLA_ROPE // 2
    f_logit = w[:, b0 + 3 * gw:b0 + 3 * gw + N_HEADS]
    dt_raw = w[:, c0 + 3 * gw:c0 + 3 * gw + N_HEADS]
    kr0 = d0 + MLA_Q_LORA + MLA_KV_LORA
    kr = w[:, kr0:kr0 + MLA_ROPE]
    kr_sw = jnp.concatenate([-kr[:, half:], kr[:, :half]], axis=1)
    zeros = lambda n: jnp.zeros((w.shape[0], n), w.dtype)
    misc = jnp.concatenate([f_logit, dt_raw, zeros(MISC_ROPE - 2 * N_HEADS), kr,
                            zeros(LANES - MISC_ROPE - MLA_ROPE)], axis=1)
    misc2 = jnp.concatenate([zeros(MISC_ROPE), kr_sw, zeros(LANES - MISC_ROPE - MLA_ROPE)], axis=1)
    fox_q = w[:, b0:b0 + gw] * (HEAD_DIM ** -0.5 * LOG2E)
    out = jnp.concatenate([w[:, a0:a0 + 3 * gw], fox_q, w[:, b0 + gw:b0 + 2 * gw], w[:, c0:c0 + 3 * gw],
                           w[:, d0:d0 + MLA_Q_LORA + MLA_KV_LORA], misc, misc2], axis=1)
    fox_vt = w[:, b0 + 2 * gw:b0 + 3 * gw].T
    return out.astype(BF16), fox_vt.astype(BF16)


def _arrange_mla(w_uq, w_ukv):
    half = MLA_ROPE // 2
    qd = MLA_NOPE + MLA_ROPE
    wq, wqs, wk, wv = [], [], [], []
    zq = jnp.zeros((MLA_Q_LORA, LANES - qd), w_uq.dtype)
    zk = jnp.zeros((MLA_KV_LORA, LANES - MLA_NOPE), w_ukv.dtype)
    for h in range(N_HEADS):
        q = w_uq[:, h * qd:(h + 1) * qd]
        nope, rope = q[:, :MLA_NOPE], q[:, MLA_NOPE:]
        wq.append(jnp.concatenate([nope, rope, zq], axis=1))
        wqs.append(jnp.concatenate([jnp.zeros_like(nope), -rope[:, half:], rope[:, :half], zq], axis=1))
        kv = w_ukv[:, h * 2 * MLA_NOPE:(h + 1) * 2 * MLA_NOPE]
        wk.append(jnp.concatenate([kv[:, :MLA_NOPE], zk], axis=1))
        wv.append(kv[:, MLA_NOPE:])
    cat = lambda xs: jnp.concatenate(xs, axis=1).astype(BF16)
    return cat(wq), cat(wqs), cat(wk), cat(wv).T


def kernel(x, positions, norm_mix, w_in, conv_a, fox_forget_bias, ssm_conv_w, ssm_conv_b, ssm_dt_bias,
           ssm_a_log, ssm_d, ssm_norm, mla_q_norm, mla_kv_norm, mla_w_uq, mla_w_ukv, w_out, norm_ffn,
           router_group_w, router_group_b, router_expert_w, router_expert_b, expert_w_gate, expert_w_up,
           expert_w_down, norm_final):
    batch, seq, d = x.shape
    t = batch * seq
    depth = w_in.shape[0]
    tm = min(512, t)
    tq = min(ATTN_TQ, seq)
    tmd = min(MOE_TILE, t)
    max_rows = 2 * t + (CHUNK - 1) * N_EXPERTS * (t // tmd) + N_EXPERTS * (MOE_ROWS - 1)
    nb = -(-max_rows // MOE_ROWS)
    nbp = -(-nb // LANES) * LANES

    xf = x.reshape(t, d)
    pos_col = positions.astype(F32).reshape(t, 1)
    cos, sin = _rope_tables(pos_col, tm)

    for l in range(depth):
        w_in_p, w_vt = _arrange_w_in(w_in[l])
        pa, pb, pc, pd, misc, misc2, fox_vt = _inproj(xf, norm_mix[l][None, :], w_in_p, w_vt, tm)

        sp = jnp.zeros((8, LANES), F32)
        sp = sp.at[0, MISC_F:MISC_F + N_HEADS].set(fox_forget_bias[l])
        sp = sp.at[0, MISC_DT:MISC_DT + N_HEADS].set(ssm_dt_bias[l])
        sp = sp.at[1, MISC_DT:MISC_DT + N_HEADS].set(ssm_a_log[l])
        col, rows, fox_q, fox_k, tref = _scalar_prep(misc, sp, pb, batch, seq, tq)

        ya = _conv_mixer(pa, _pad_rows(conv_a[l]), batch, seq)
        yb = _attention(fox_q, fox_k, fox_vt, tref, batch, seq, tq, "fox_attention")
        conv_wb = _pad_rows(jnp.concatenate([ssm_conv_w[l], ssm_conv_b[l][None, :]], axis=0))
        ssd_par = _pad_rows(jnp.stack([jnp.repeat(ssm_d[l], HEAD_DIM), ssm_norm[l]]))
        yc = _ssd_mixer(pc, col, rows, conv_wb, ssd_par, batch, seq)
        wq, wqs, wk, wv = _arrange_mla(mla_w_uq[l], mla_w_ukv[l])
        q, k, v = _mla_prep(pd, misc, misc2, cos, sin, mla_q_norm[l][None, :], mla_kv_norm[l][None, :],
                            wq, wqs, wk, wv, tm)
        yd = _attention(q, k, v, None, batch, seq, tq, "mla_attention")

        wr = jnp.zeros((d, LANES), F32)
        wr = wr.at[:, :N_EXPERT_GROUPS].set(router_group_w[l])
        wr = wr.at[:, N_EXPERT_GROUPS:N_EXPERT_GROUPS + N_EXPERTS].set(router_expert_w[l])
        br = jnp.zeros((1, LANES), F32)
        br = br.at[0, :N_EXPERT_GROUPS].set(router_group_b[l])
        br = br.at[0, N_EXPERT_GROUPS:N_EXPERT_GROUPS + N_EXPERTS].set(router_expert_b[l])
        x2, h2, rrow = _outproj(xf, ya, yb, yc, yd, w_out[l].astype(BF16), norm_ffn[l][None, :], wr, br, tm)

        lrow, lcol, tab, meta = _route(rrow, tmd, nbp)
        xs = _scatter(tab, meta, lrow, rrow, h2, tmd, nb)
        ys = _ffn(meta, xs, expert_w_gate, expert_w_up, expert_w_down, l, nb)
        final = l == depth - 1
        xf = _gather(tab, lcol, x2, norm_final[None, :], ys, tmd, final)

    return xf.reshape(batch, seq, d)


def _retile(dest, tm, tmd):
    if tm == tmd:
        return dest
    nt = dest.shape[0]
    return dest.reshape(nt, 2, tm // tmd, tmd).transpose(0, 2, 1, 3).reshape(nt * (tm // tmd), 2, tmd)
```

```python
import functools
import math

import jax
import jax.numpy as jnp
import numpy as np
from jax import lax
from jax.experimental import pallas as pl
from jax.experimental.pallas import tpu as pltpu

F32 = jnp.float32
BF16 = jnp.bfloat16
I32 = jnp.int32

LANES = 128
VMEM_LIMIT_BYTES = 56 * 1024 * 1024

D_MODEL = 1024
RMS_EPS = 1e-6
LOG2E = math.log2(math.e)
GROUP_WIDTH = 256
HEAD_DIM = 64
N_HEADS = 4

CONV_A_WIDTH = 3
SSM_CONV = 4
SSM_STATE = 64
SSM_CHUNK = 256

MLA_NOPE = 64
MLA_ROPE = 32
MLA_Q_LORA = 256
MLA_KV_LORA = 128
ROPE_BASE = 10000.0
MLA_CHUNK = 64
ATTN_TQ = 512
ATTN_RB = 128

N_EXPERT_GROUPS = 4
EXPERTS_PER_GROUP = 8
N_EXPERTS = 32
EXPERT_FF = 256
MOE_ROWS = 256

SEG_A = (0, 768)
SEG_B = (768, 1280)
SEG_C = (1280, 2048)
SEG_D = (2048, 2432)
SEG_M = (2432, 2560)
SEG_M2 = (2560, 2688)
IN_COLS_PADDED = 2688
HEAD_PAD = N_HEADS * LANES
AUG_LANE = HEAD_DIM
MISC_F = 0
MISC_DT = 4
MISC_ROPE = 64
COL_CUMF = 0
COL_DT = 4
COL_ACUM = 8
N_SCALAR_ROWS = 16


def _cparams(*sem):
    return pltpu.CompilerParams(dimension_semantics=sem, vmem_limit_bytes=VMEM_LIMIT_BYTES)


def _lane_iota(shape):
    return lax.broadcasted_iota(I32, shape, len(shape) - 1)


def _row_iota(shape):
    return lax.broadcasted_iota(I32, shape, 0)


def _rms(x, g):
    ms = jnp.mean(x * x, axis=-1, keepdims=True)
    return x * lax.rsqrt(ms + RMS_EPS) * g


def _silu(x):
    return x / (1.0 + jnp.exp(-x))


def _softplus(x):
    return jnp.maximum(x, 0.0) + jnp.log(1.0 + jnp.exp(-jnp.abs(x)))


def _shift_rows(x, k):
    rolled = pltpu.roll(x, k, 0)
    return jnp.where(_row_iota(x.shape) >= k, rolled, 0.0)


def _rope_kernel(pos_ref, freq_ref, cos_ref, sin_ref):
    ang = pos_ref[...] * freq_ref[...]
    lane = _lane_iota(ang.shape)
    rope = (lane >= MISC_ROPE) & (lane < MISC_ROPE + MLA_ROPE)
    cos_ref[...] = jnp.where(rope, jnp.cos(ang), jnp.where(lane < MISC_ROPE, 1.0, 0.0))
    sin_ref[...] = jnp.where(rope, jnp.sin(ang), 0.0)


def _rope_tables(pos_col, tm):
    t = pos_col.shape[0]
    half = MLA_ROPE // 2
    inv = ROPE_BASE ** (-np.arange(0, MLA_ROPE, 2, dtype=np.float32) / MLA_ROPE)
    freq = np.zeros((1, LANES), np.float32)
    freq[0, MISC_ROPE:MISC_ROPE + half] = inv
    freq[0, MISC_ROPE + half:MISC_ROPE + MLA_ROPE] = inv
    return pl.pallas_call(
        _rope_kernel,
        grid=(t // tm,),
        in_specs=[pl.BlockSpec((tm, 1), lambda i: (i, 0)),
                  pl.BlockSpec((1, LANES), lambda i: (0, 0))],
        out_specs=[pl.BlockSpec((tm, LANES), lambda i: (i, 0))] * 2,
        out_shape=[jax.ShapeDtypeStruct((t, LANES), F32)] * 2,
        compiler_params=_cparams("parallel"),
        name="rope_tables",
    )(pos_col, jnp.asarray(freq))


def _inproj_kernel(x_ref, g_ref, w_ref, wvt_ref, oa, ob, oc, od, om, om2, ovt):
    h = _rms(x_ref[...], g_ref[...]).astype(BF16)
    for o, (lo, hi) in ((oa, SEG_A), (ob, SEG_B), (oc, SEG_C), (od, SEG_D), (om, SEG_M), (om2, SEG_M2)):
        o[...] = jnp.dot(h, w_ref[:, lo:hi], preferred_element_type=F32).astype(o.dtype)
    ovt[...] = lax.dot_general(wvt_ref[...], h, (((1,), (1,)), ((), ())),
                               preferred_element_type=F32).astype(ovt.dtype)


def _inproj(x, g, w, wvt, tm):
    t = x.shape[0]
    widths = [(s[1] - s[0]) for s in (SEG_A, SEG_B, SEG_C, SEG_D, SEG_M, SEG_M2)]
    dtypes = [BF16, BF16, BF16, BF16, F32, F32]
    return pl.pallas_call(
        _inproj_kernel,
        grid=(t // tm,),
        in_specs=[pl.BlockSpec((tm, D_MODEL), lambda i: (i, 0)),
                  pl.BlockSpec((1, D_MODEL), lambda i: (0, 0)),
                  pl.BlockSpec((D_MODEL, IN_COLS_PADDED), lambda i: (0, 0)),
                  pl.BlockSpec((GROUP_WIDTH, D_MODEL), lambda i: (0, 0))],
        out_specs=[pl.BlockSpec((tm, wd), lambda i: (i, 0)) for wd in widths]
        + [pl.BlockSpec((GROUP_WIDTH, tm), lambda i: (0, i))],
        out_shape=[jax.ShapeDtypeStruct((t, wd), dt) for wd, dt in zip(widths, dtypes)]
        + [jax.ShapeDtypeStruct((GROUP_WIDTH, t), BF16)],
        compiler_params=_cparams("parallel"),
        name="inproj",
    )(x, g, w, wvt)


def _scalar_prep_kernel(m_ref, p_ref, qk_ref, sel_ref, place_ref, const_ref,
                        col_ref, row_ref, qa_ref, ka_ref, tref_ref, *, tq):
    s = m_ref.shape[0]
    tref_ref[...] = jnp.zeros_like(tref_ref)
    tile_ref = jnp.zeros((1, LANES), F32)
    m = m_ref[...]
    bias = p_ref[0:1, :]
    a_log = p_ref[1:2, :]
    lane = _lane_iota(m.shape)
    z = m + bias
    logf = jnp.minimum(z, 0.0) - jnp.log(1.0 + jnp.exp(-jnp.abs(z)))
    dt = _softplus(z)
    a = dt * (-jnp.exp(a_log))
    is_f = lane < MISC_DT
    is_dt = (lane >= MISC_DT) & (lane < MISC_DT + N_HEADS)
    v = jnp.where(is_f, logf, jnp.where(is_dt, a, 0.0))
    r = _row_iota((SSM_CHUNK, SSM_CHUNK))
    c = _lane_iota((SSM_CHUNK, SSM_CHUNK))
    tril = jnp.where(r >= c, 1.0, 0.0).astype(F32)
    carry = jnp.zeros((1, LANES), F32)
    lane_1 = _lane_iota((1, LANES))
    lane_b = _lane_iota((SSM_CHUNK, LANES))
    for ci in range(s // SSM_CHUNK):
        blk = v[ci * SSM_CHUNK:(ci + 1) * SSM_CHUNK]
        cs = jnp.dot(tril, blk, preferred_element_type=F32, precision=lax.Precision.HIGHEST)
        cs = cs + jnp.where(lane_1 < MISC_DT, carry, 0.0)
        carry = cs[SSM_CHUNK - 1:SSM_CHUNK]
        acum = pltpu.roll(cs, COL_ACUM - MISC_DT, 1)
        out = jnp.where(lane_b < MISC_DT, cs * LOG2E,
                        jnp.where(lane_b < COL_ACUM, dt[ci * SSM_CHUNK:(ci + 1) * SSM_CHUNK],
                                  jnp.where(lane_b < COL_ACUM + N_HEADS, acum, 0.0)))
        rows = slice(ci * SSM_CHUNK, (ci + 1) * SSM_CHUNK)
        col_ref[rows, :] = out
        row_ref[0, :, rows] = out.T[:N_SCALAR_ROWS]
        if (ci * SSM_CHUNK) % tq == 0:
            tile_ref = out[0:1, :]
            ti = (ci * SSM_CHUNK) // tq
            tref_ref[0, ti:ti + 1, :] = tile_ref
        c = out - tile_ref
        c_hi = c.astype(BF16)
        r1 = c - c_hi.astype(F32)
        c_mid = r1.astype(BF16)
        c_lo = (r1 - c_mid.astype(F32)).astype(BF16)
        for o_ref, base, qk_lo in ((qa_ref, 0, 0), (ka_ref, 3, GROUP_WIDTH)):
            aug = jnp.dot(qk_ref[rows, qk_lo:qk_lo + GROUP_WIDTH], sel_ref[...], preferred_element_type=F32)
            for term, cc in enumerate((c_hi, c_mid, c_lo)):
                aug = aug + jnp.dot(cc, place_ref[base + term], preferred_element_type=F32)
            o_ref[rows, :] = (aug + const_ref[base // 3:base // 3 + 1, :]).astype(o_ref.dtype)


def _fox_placement():
    sel = np.zeros((GROUP_WIDTH, HEAD_PAD), np.float32)
    place = np.zeros((6, LANES, HEAD_PAD), np.float32)
    const = np.zeros((8, HEAD_PAD), np.float32)
    for h in range(N_HEADS):
        for d in range(HEAD_DIM):
            sel[h * HEAD_DIM + d, h * LANES + d] = 1.0
        a0 = h * LANES + AUG_LANE
        for term in range(3):
            place[term, COL_CUMF + h, a0 + term] = 1.0
            place[3 + term, COL_CUMF + h, a0 + 3 + term] = -1.0
            const[0, a0 + 3 + term] = 1.0
            const[1, a0 + term] = 1.0
    return jnp.asarray(sel, BF16), jnp.asarray(place, BF16), jnp.asarray(const, F32)


def _scalar_prep(misc, params, qk, batch, seq, tq):
    sel, place, const = _fox_placement()
    full = lambda a: pl.BlockSpec(a.shape, lambda b: (0,) * a.ndim)
    return pl.pallas_call(
        functools.partial(_scalar_prep_kernel, tq=tq),
        grid=(batch,),
        in_specs=[pl.BlockSpec((seq, LANES), lambda b: (b, 0)),
                  pl.BlockSpec((8, LANES), lambda b: (0, 0)),
                  pl.BlockSpec((seq, 2 * GROUP_WIDTH), lambda b: (b, 0)),
                  full(sel), full(place), full(const)],
        out_specs=[pl.BlockSpec((seq, LANES), lambda b: (b, 0)),
                   pl.BlockSpec((1, N_SCALAR_ROWS, seq), lambda b: (b, 0, 0)),
                   pl.BlockSpec((seq, HEAD_PAD), lambda b: (b, 0)),
                   pl.BlockSpec((seq, HEAD_PAD), lambda b: (b, 0)),
                   pl.BlockSpec((1, 8, LANES), lambda b: (b, 0, 0))],
        out_shape=[jax.ShapeDtypeStruct((batch * seq, LANES), F32),
                   jax.ShapeDtypeStruct((batch, N_SCALAR_ROWS, seq), F32),
                   jax.ShapeDtypeStruct((batch * seq, HEAD_PAD), BF16),
                   jax.ShapeDtypeStruct((batch * seq, HEAD_PAD), BF16),
                   jax.ShapeDtypeStruct((batch, 8, LANES), F32)],
        compiler_params=_cparams("parallel"),
        name="scalar_prep",
    )(misc, params, qk, sel, place, const)


def _conv_mixer_kernel(p_ref, w_ref, o_ref):
    gw = GROUP_WIDTH
    b_gate = p_ref[:, 0:gw].astype(F32)
    cv = p_ref[:, gw:2 * gw].astype(F32) * p_ref[:, 2 * gw:3 * gw].astype(F32)
    acc = cv * w_ref[CONV_A_WIDTH - 1:CONV_A_WIDTH, :]
    for k in range(1, CONV_A_WIDTH):
        acc = acc + _shift_rows(cv, k) * w_ref[CONV_A_WIDTH - 1 - k:CONV_A_WIDTH - k, :]
    o_ref[...] = (b_gate * acc).astype(o_ref.dtype)


def _conv_mixer(pa, w, batch, seq):
    return pl.pallas_call(
        _conv_mixer_kernel,
        grid=(batch,),
        in_specs=[pl.BlockSpec((seq, 3 * GROUP_WIDTH), lambda b: (b, 0)),
                  pl.BlockSpec((8, GROUP_WIDTH), lambda b: (0, 0))],
        out_specs=pl.BlockSpec((seq, GROUP_WIDTH), lambda b: (b, 0)),
        out_shape=jax.ShapeDtypeStruct((batch * seq, GROUP_WIDTH), BF16),
        compiler_params=_cparams("parallel"),
        name="conv_mixer",
    )(pa, w)


def _pair_lanes(col, base, shape):
    lane = _lane_iota(shape)
    return jnp.where(lane < HEAD_DIM, col[:, base:base + 1], col[:, base + 1:base + 2])


def _ssd_kernel(p_ref, col_ref, row_ref, cw_ref, par_ref, o_ref, u_ref):
    s = p_ref.shape[0]
    q = SSM_CHUNK
    gw = GROUP_WIDTH
    xbc = p_ref[:, gw:3 * gw].astype(F32)
    acc = xbc * cw_ref[SSM_CONV - 1:SSM_CONV, :]
    for k in range(1, SSM_CONV):
        acc = acc + _shift_rows(xbc, k) * cw_ref[SSM_CONV - 1 - k:SSM_CONV - k, :]
    u_ref[...] = _silu(acc + cw_ref[SSM_CONV:SSM_CONV + 1, :])

    d_skip = par_ref[0:1, :]
    norm_g = par_ref[1:2, :]
    lane_q = _lane_iota((q, LANES))
    low = lane_q < HEAD_DIM
    tri = _row_iota((q, q)) >= _lane_iota((q, q))

    def chunk(ci, states):
        r0 = pl.multiple_of(ci * q, q)
        rows = pl.ds(r0, q)
        u = u_ref[rows, :]
        col = col_ref[rows, :]
        bm = u[:, gw:gw + LANES]
        cm = u[:, gw + LANES:gw + 2 * LANES]
        z = p_ref[rows, 0:gw].astype(F32)
        new_states = []
        ys = []
        for g in range(2):
            sel = low if g == 0 else jnp.logical_not(low)
            cg = jnp.where(sel, cm, 0.0).astype(BF16)
            bg = jnp.where(sel, bm, 0.0)
            gmat = lax.dot_general(cg, bm.astype(BF16), (((1,), (1,)), ((), ())),
                                   preferred_element_type=F32)
            xs = u[:, g * LANES:(g + 1) * LANES]
            dt2 = _pair_lanes(col, COL_DT + 2 * g, (q, LANES))
            ac2 = _pair_lanes(col, COL_ACUM + 2 * g, (q, LANES))
            xdt = xs * dt2
            xdt_b = xdt.astype(BF16)
            st = states[g]
            y_off = jnp.dot(cg, st.astype(BF16), preferred_element_type=F32) * jnp.exp(ac2)
            halves = []
            for hh in range(2):
                h = 2 * g + hh
                ac_col = col[:, COL_ACUM + h:COL_ACUM + h + 1]
                ac_row = row_ref[0, COL_ACUM + h:COL_ACUM + h + 1, rows]
                decay = jnp.exp(jnp.where(tri, ac_col - ac_row, -1e30))
                mm = (gmat * decay).astype(BF16)
                halves.append(jnp.dot(mm, xdt_b, preferred_element_type=F32))
            y = jnp.where(low, halves[0], halves[1]) + y_off + d_skip[:, g * LANES:(g + 1) * LANES] * xs
            ys.append(y)
            ac_last = ac2[q - 1:q, :]
            w_end = jnp.exp(ac_last - ac2)
            xw = (xdt * w_end).astype(BF16)
            upd = jnp.dot(bg.T.astype(BF16), xw, preferred_element_type=F32)
            new_states.append(st * jnp.exp(ac_last) + upd)
        yfull = jnp.concatenate(ys, axis=1) * _silu(z)
        o_ref[rows, :] = _rms(yfull, norm_g).astype(o_ref.dtype)
        return tuple(new_states)

    init = (jnp.zeros((LANES, LANES), F32), jnp.zeros((LANES, LANES), F32))
    lax.fori_loop(0, s // q, chunk, init)


def _ssd_mixer(pc, col, rows, conv_wb, par, batch, seq):
    gw = GROUP_WIDTH
    return pl.pallas_call(
        _ssd_kernel,
        grid=(batch,),
        in_specs=[pl.BlockSpec((seq, 3 * gw), lambda b: (b, 0)),
                  pl.BlockSpec((seq, LANES), lambda b: (b, 0)),
                  pl.BlockSpec((1, N_SCALAR_ROWS, seq), lambda b: (b, 0, 0)),
                  pl.BlockSpec((8, 2 * gw), lambda b: (0, 0)),
                  pl.BlockSpec((8, gw), lambda b: (0, 0))],
        out_specs=pl.BlockSpec((seq, gw), lambda b: (b, 0)),
        out_shape=jax.ShapeDtypeStruct((batch * seq, gw), BF16),
        scratch_shapes=[pltpu.VMEM((seq, 2 * gw), F32)],
        compiler_params=_cparams("parallel"),
        name="ssd_mixer",
    )(pc, col, rows, conv_wb, par)


def _mla_prep_kernel(pd_ref, m_ref, m2_ref, cos_ref, sin_ref, nq_ref, nkv_ref,
                     wq_ref, wqs_ref, wk_ref, wvt_ref, q_ref, k_ref, vt_ref):
    cq = _rms(pd_ref[:, 0:MLA_Q_LORA].astype(F32), nq_ref[...]).astype(BF16)
    ckv = _rms(pd_ref[:, MLA_Q_LORA:MLA_Q_LORA + MLA_KV_LORA].astype(F32), nkv_ref[...]).astype(BF16)
    cos = cos_ref[...]
    sin = sin_ref[...]
    cos4 = jnp.concatenate([cos] * N_HEADS, axis=1)
    sin4 = jnp.concatenate([sin] * N_HEADS, axis=1)
    scale = (MLA_NOPE + MLA_ROPE) ** -0.5 * LOG2E
    q = jnp.dot(cq, wq_ref[...], preferred_element_type=F32)
    qs = jnp.dot(cq, wqs_ref[...], preferred_element_type=F32)
    q_ref[...] = ((q * cos4 + qs * sin4) * scale).astype(q_ref.dtype)
    lane = _lane_iota(cos.shape)
    rope = (lane >= MISC_ROPE) & (lane < MISC_ROPE + MLA_ROPE)
    kr = jnp.where(rope, m_ref[...] * cos + m2_ref[...] * sin, 0.0)
    k = jnp.dot(ckv, wk_ref[...], preferred_element_type=F32)
    k_ref[...] = (k + jnp.concatenate([kr] * N_HEADS, axis=1)).astype(k_ref.dtype)
    vt_ref[...] = lax.dot_general(wvt_ref[...], ckv, (((1,), (1,)), ((), ())),
                                  preferred_element_type=F32).astype(vt_ref.dtype)


def _mla_prep(pd, misc, misc2, cos, sin, nq, nkv, wq, wqs, wk, wv, tm):
    t = pd.shape[0]
    hp = N_HEADS * LANES
    full = lambda a: pl.BlockSpec(a.shape, lambda i: (0, 0))
    tile = lambda w: pl.BlockSpec((tm, w), lambda i: (i, 0))
    return pl.pallas_call(
        _mla_prep_kernel,
        grid=(t // tm,),
        in_specs=[tile(MLA_Q_LORA + MLA_KV_LORA), tile(LANES), tile(LANES), tile(LANES), tile(LANES),
                  full(nq), full(nkv), full(wq), full(wqs), full(wk), full(wv)],
        out_specs=[tile(hp), tile(hp), pl.BlockSpec((GROUP_WIDTH, tm), lambda i: (0, i))],
        out_shape=[jax.ShapeDtypeStruct((t, hp), BF16), jax.ShapeDtypeStruct((t, hp), BF16),
                   jax.ShapeDtypeStruct((GROUP_WIDTH, t), BF16)],
        compiler_params=_cparams("parallel"),
        name="mla_prep",
    )(pd, misc, misc2, cos, sin, nq, nkv, wq, wqs, wk, wv)


def _attn_kernel(*refs, fox, tq):
    if fox:
        tref_ref, q_ref, k_ref, vt_ref, o_ref = refs
    else:
        q_ref, k_ref, vt_ref, o_ref = refs
        tref_ref = None
    b = pl.program_id(0)
    i = pl.program_id(1)
    key = _row_iota((tq, tq))
    qry = _lane_iota((tq, tq))
    if fox:
        allowed = key <= qry
    else:
        shift = int(math.log2(MLA_CHUNK))
        allowed = (key >> shift) <= (qry >> shift)
    qs = [q_ref[:, h * LANES:(h + 1) * LANES] for h in range(N_HEADS)]

    def step(j, masked, carry):
        rk = pl.ds(pl.multiple_of(j * tq, tq), tq)
        scores = [lax.dot_general(k_ref[rk, h * LANES:(h + 1) * LANES], qs[h], (((1,), (1,)), ((), ())),
                                  preferred_element_type=F32) for h in range(N_HEADS)]
        probs = []
        for h in range(N_HEADS):
            m, l, _ = carry[h]
            s = scores[h]
            if masked:
                s = jnp.where(allowed, s, -1e30)
            delta = (tref_ref[b, i, h] - tref_ref[b, j, h]) if fox else 0.0
            m_new = jnp.maximum(m, jnp.max(s, axis=0, keepdims=True) + delta)
            alpha = jnp.exp2(m - m_new)
            p = jnp.exp2(s - (m_new - delta))
            l_new = alpha * l + jnp.sum(p, axis=0, keepdims=True)
            probs.append((m_new, l_new, alpha, p.astype(BF16)))
        new = []
        for h in range(N_HEADS):
            pair = h // 2
            m_new, l_new, alpha, p = probs[h]
            pv = jnp.dot(vt_ref[pair * LANES:(pair + 1) * LANES, rk], p, preferred_element_type=F32)
            new.append((m_new, l_new, alpha * carry[h][2] + pv))
        return tuple(new)

    init = tuple((jnp.full((1, tq), -1e30, F32), jnp.zeros((1, tq), F32), jnp.zeros((LANES, tq), F32))
                 for _ in range(N_HEADS))
    carry = lax.fori_loop(0, i, lambda j, c: step(j, False, c), init)
    carry = step(i, True, carry)
    outs = [acc / l for (_, l, acc) in carry]
    top = _row_iota((LANES, tq)) < HEAD_DIM
    o_t = jnp.concatenate([jnp.where(top, outs[0], outs[1]), jnp.where(top, outs[2], outs[3])], axis=0)
    o_ref[...] = o_t.T.astype(o_ref.dtype)


def _attention(q, k, vt, tref, batch, seq, tq, name):
    nq = seq // tq
    fox = tref is not None
    kern = functools.partial(_attn_kernel, fox=fox, tq=tq)
    grid_spec = pltpu.PrefetchScalarGridSpec(
        num_scalar_prefetch=1 if fox else 0,
        grid=(batch, nq),
        in_specs=[pl.BlockSpec((tq, HEAD_PAD), lambda b, i, *_: (b * nq + i, 0)),
                  pl.BlockSpec((seq, HEAD_PAD), lambda b, i, *_: (b, 0)),
                  pl.BlockSpec((GROUP_WIDTH, seq), lambda b, i, *_: (0, b))],
        out_specs=pl.BlockSpec((tq, GROUP_WIDTH), lambda b, i, *_: (b * nq + i, 0)),
    )
    args = ((tref,) if fox else ()) + (q, k, vt)
    return pl.pallas_call(
        kern,
        grid_spec=grid_spec,
        out_shape=jax.ShapeDtypeStruct((batch * seq, GROUP_WIDTH), BF16),
        compiler_params=_cparams("parallel", "arbitrary"),
        name=name,
    )(*args)


def _outproj_kernel(x_ref, ya, yb, yc, yd, w_ref, g_ref, wr_ref, br_ref,
                    x2_ref, h2_ref, rrow_ref):
    y = jnp.concatenate([ya[...], yb[...], yc[...], yd[...]], axis=1)
    x2 = x_ref[...] + jnp.dot(y, w_ref[...], preferred_element_type=F32)
    x2_ref[...] = x2
    h2 = _rms(x2, g_ref[...])
    h2_ref[...] = h2.astype(h2_ref.dtype)
    h_hi = h2.astype(BF16)
    h_lo = (h2 - h_hi.astype(F32)).astype(BF16)
    part = jnp.dot(h_hi, wr_ref[...], preferred_element_type=F32)
    logits = (part[:, 0:LANES] + part[:, LANES:2 * LANES]
              + jnp.dot(h_lo, wr_ref[:, 0:LANES], preferred_element_type=F32) + br_ref[...])
    lt = logits.T
    row = _row_iota(lt.shape)
    neg = -1e30
    big = 1 << 20
    gmask = row < N_EXPERT_GROUPS
    gl = jnp.where(gmask, lt, neg)
    gmax = jnp.max(gl, axis=0, keepdims=True)
    gsum = jnp.sum(jnp.where(gmask, jnp.exp(gl - gmax), 0.0), axis=0, keepdims=True)
    g_w = 1.0 / gsum
    g_idx = jnp.min(jnp.where(gmask & (gl == gmax), row, big), axis=0, keepdims=True)
    e_local = row - N_EXPERT_GROUPS
    emask = (e_local >= 0) & (e_local < N_EXPERTS) & ((e_local >> int(math.log2(EXPERTS_PER_GROUP))) == g_idx)
    el = jnp.where(emask, lt, neg)
    e1v = jnp.max(el, axis=0, keepdims=True)
    esum = jnp.sum(jnp.where(emask, jnp.exp(el - e1v), 0.0), axis=0, keepdims=True)
    i1 = jnp.min(jnp.where(emask & (el == e1v), row, big), axis=0, keepdims=True)
    el2 = jnp.where(row == i1, neg, el)
    e2v = jnp.max(el2, axis=0, keepdims=True)
    i2 = jnp.min(jnp.where(emask & (row != i1) & (el2 == e2v), row, big), axis=0, keepdims=True)
    p1 = 1.0 / esum
    p2 = jnp.exp(e2v - e1v) / esum
    w1 = g_w * (p1 / (p1 + p2))
    w2 = g_w * (p2 / (p1 + p2))
    out_row = _row_iota(rrow_ref.shape)
    rrow_ref[...] = jnp.where(out_row == 0, (i1 - N_EXPERT_GROUPS).astype(F32),
                              jnp.where(out_row == 1, (i2 - N_EXPERT_GROUPS).astype(F32),
                                        jnp.where(out_row == 2, w1, jnp.where(out_row == 3, w2, 0.0))))


def _outproj(x, ya, yb, yc, yd, w, g, wr, br, tm):
    t = x.shape[0]
    full = lambda a: pl.BlockSpec(a.shape, lambda i: (0, 0))
    tile = lambda wd: pl.BlockSpec((tm, wd), lambda i: (i, 0))
    return pl.pallas_call(
        _outproj_kernel,
        grid=(t // tm,),
        in_specs=[tile(D_MODEL)] + [tile(GROUP_WIDTH)] * 4 + [full(w), full(g), full(wr), full(br)],
        out_specs=[tile(D_MODEL), tile(D_MODEL), pl.BlockSpec((8, tm), lambda i: (0, i))],
        out_shape=[jax.ShapeDtypeStruct((t, D_MODEL), F32), jax.ShapeDtypeStruct((t, D_MODEL), BF16),
                   jax.ShapeDtypeStruct((8, t), F32)],
        compiler_params=_cparams("parallel"),
        name="outproj_router",
    )(x, ya, yb, yc, yd, w, g, wr, br)


def _positions_kernel(r_ref, dest_ref, meta_ref, cnt_ref, carry_ref, start_ref, *, tm, nbp):
    phase = pl.program_id(0)
    i = pl.program_id(1)
    e_iota = _row_iota((N_EXPERTS, tm))
    e0 = r_ref[0:1, :].astype(I32)
    e1 = r_ref[1:2, :].astype(I32)
    oh0 = e_iota == e0
    oh1 = e_iota == e1
    oh = jnp.where(oh0 | oh1, 1.0, 0.0)

    @pl.when((phase == 0) & (i == 0))
    def _():
        cnt_ref[...] = jnp.zeros_like(cnt_ref)

    @pl.when(phase == 0)
    def _():
        cnt_ref[...] += jnp.sum(oh, axis=-1, keepdims=True)

    @pl.when((phase == 1) & (i == 0))
    def _():
        cnt = cnt_ref[...]
        padded = jnp.floor((cnt + (MOE_ROWS - 1)) * (1.0 / MOE_ROWS)) * MOE_ROWS
        tril = jnp.where(_row_iota((N_EXPERTS, N_EXPERTS)) >= _lane_iota((N_EXPERTS, N_EXPERTS)), 1.0, 0.0)
        pend = jnp.dot(tril, padded, preferred_element_type=F32, precision=lax.Precision.HIGHEST)
        pstart = pend - padded
        start_ref[...] = pstart
        carry_ref[...] = jnp.zeros_like(carry_ref)
        pend_b = jnp.concatenate([pend] * (nbp // LANES), axis=1)
        vend_b = jnp.concatenate([pstart + cnt] * (nbp // LANES), axis=1)
        b0 = (_lane_iota((N_EXPERTS, nbp)) * MOE_ROWS).astype(F32)
        bexp = jnp.sum(jnp.where(pend_b <= b0, 1.0, 0.0), axis=0, keepdims=True)
        bexp = jnp.minimum(bexp, N_EXPERTS - 1.0)
        is_e = _row_iota((N_EXPERTS, nbp)).astype(F32) == bexp
        vend = jnp.sum(jnp.where(is_e, vend_b, 0.0), axis=0, keepdims=True)
        nvalid = jnp.clip(vend - b0[0:1], 0.0, float(MOE_ROWS))
        total = jnp.max(pend_b, axis=0, keepdims=True) * (1.0 / MOE_ROWS)
        row = _row_iota((8, nbp))
        meta = jnp.where(row == 0, bexp, jnp.where(row == 1, nvalid, jnp.where(row == 2, total, 0.0)))
        meta_ref[...] = meta.astype(I32)

    @pl.when(phase == 1)
    def _():
        su = jnp.where(_row_iota((tm, tm)) < _lane_iota((tm, tm)), 1.0, 0.0).astype(BF16)
        before = jnp.dot(oh.astype(BF16), su, preferred_element_type=F32)
        base = start_ref[:, 0:1] + carry_ref[:, 0:1] + before
        d0 = jnp.sum(jnp.where(oh0, base, 0.0), axis=0, keepdims=True)
        d1 = jnp.sum(jnp.where(oh1, base, 0.0), axis=0, keepdims=True)
        dest_ref[0, 0:1, :] = d0.astype(I32)
        dest_ref[0, 1:2, :] = d1.astype(I32)
        carry_ref[...] += jnp.sum(oh, axis=-1, keepdims=True)


def _positions(rrow, tm, nbp):
    t = rrow.shape[1]
    nt = t // tm
    kern = functools.partial(_positions_kernel, tm=tm, nbp=nbp)
    return pl.pallas_call(
        kern,
        grid=(2, nt),
        in_specs=[pl.BlockSpec((8, tm), lambda p, i: (0, i))],
        out_specs=[pl.BlockSpec((1, 2, tm), lambda p, i: (i * p, 0, 0)),
                   pl.BlockSpec((8, nbp), lambda p, i: (0, 0))],
        out_shape=[jax.ShapeDtypeStruct((nt, 2, tm), I32), jax.ShapeDtypeStruct((8, nbp), I32)],
        scratch_shapes=[pltpu.VMEM((N_EXPERTS, LANES), F32)] * 3,
        compiler_params=_cparams("arbitrary", "arbitrary"),
        name="moe_positions",
    )(rrow)


def _dispatch_kernel(meta_ref, dest_ref, h_ref, xs_ref, zero_ref, sem, zsem, *, tm, nb):
    i = pl.program_id(0)

    @pl.when(i == 0)
    def _():
        zero_ref[...] = jnp.zeros_like(zero_ref)
        n_used = meta_ref[2, 0]

        def zcopy(b):
            return pltpu.make_async_copy(zero_ref, xs_ref.at[pl.ds(b * MOE_ROWS, MOE_ROWS)], zsem)

        def needs(b):
            return (b < n_used) & (meta_ref[1, b] < MOE_ROWS)

        def start(b, c):
            @pl.when(needs(b))
            def _():
                zcopy(b).start()
            return c

        def wait(b, c):
            @pl.when(needs(b))
            def _():
                zcopy(b).wait()
            return c

        lax.fori_loop(0, nb, start, 0)
        lax.fori_loop(0, nb, wait, 0)

    def copy(t, k):
        return pltpu.make_async_copy(h_ref.at[pl.ds(t, 1)], xs_ref.at[pl.ds(dest_ref[0, k, t], 1)], sem)

    def start(t, c):
        copy(t, 0).start()
        copy(t, 1).start()
        return c

    def wait(t, c):
        copy(t, 0).wait()
        copy(t, 1).wait()
        return c

    lax.fori_loop(0, tm, start, 0)
    lax.fori_loop(0, tm, wait, 0)


def _dispatch(meta, dest, h2, tm, nb):
    t = h2.shape[0]
    kern = functools.partial(_dispatch_kernel, tm=tm, nb=nb)
    grid_spec = pltpu.PrefetchScalarGridSpec(
        num_scalar_prefetch=1,
        grid=(t // tm,),
        in_specs=[pl.BlockSpec((1, 2, tm), lambda i, m: (i, 0, 0), memory_space=pltpu.SMEM),
                  pl.BlockSpec((tm, D_MODEL), lambda i, m: (i, 0))],
        out_specs=pl.BlockSpec(memory_space=pl.ANY),
        scratch_shapes=[pltpu.VMEM((MOE_ROWS, D_MODEL), F32),
                        pltpu.SemaphoreType.DMA, pltpu.SemaphoreType.DMA],
    )
    return pl.pallas_call(
        kern,
        grid_spec=grid_spec,
        out_shape=jax.ShapeDtypeStruct((nb * MOE_ROWS, D_MODEL), F32),
        compiler_params=_cparams("arbitrary"),
        name="moe_dispatch",
    )(meta, dest, h2)


def _expert_kernel(meta_ref, x_ref, wg_ref, wu_ref, wd_ref, o_ref):
    b = pl.program_id(0)

    @pl.when(b < meta_ref[2, 0])
    def _():
        x = x_ref[...].astype(BF16)
        gate = jnp.dot(x, wg_ref[0, 0].astype(BF16), preferred_element_type=F32)
        up = jnp.dot(x, wu_ref[0, 0].astype(BF16), preferred_element_type=F32)
        act = (_silu(gate) * up).astype(BF16)
        o_ref[...] = jnp.dot(act, wd_ref[0, 0].astype(BF16), preferred_element_type=F32)


def _experts(meta, xs, wg, wu, wd, layer, nb):
    def blk(b, m):
        return (jnp.minimum(b, m[2, 0] - 1), 0)

    def wblk(b, m):
        return (layer, m[0, jnp.minimum(b, m[2, 0] - 1)], 0, 0)

    grid_spec = pltpu.PrefetchScalarGridSpec(
        num_scalar_prefetch=1,
        grid=(nb,),
        in_specs=[pl.BlockSpec((MOE_ROWS, D_MODEL), blk),
                  pl.BlockSpec((1, 1, D_MODEL, EXPERT_FF), wblk),
                  pl.BlockSpec((1, 1, D_MODEL, EXPERT_FF), wblk),
                  pl.BlockSpec((1, 1, EXPERT_FF, D_MODEL), wblk)],
        out_specs=pl.BlockSpec((MOE_ROWS, D_MODEL), blk),
    )
    return pl.pallas_call(
        _expert_kernel,
        grid_spec=grid_spec,
        out_shape=jax.ShapeDtypeStruct((nb * MOE_ROWS, D_MODEL), F32),
        compiler_params=_cparams("arbitrary"),
        name="moe_experts",
    )(meta, xs, wg, wu, wd)


def _combine_kernel(dest_ref, x_ref, r_ref, g_ref, ys_ref, o_ref, buf_ref, sem, *, tm, final):
    def copy(t, k):
        return pltpu.make_async_copy(ys_ref.at[pl.ds(dest_ref[0, k, t], 1)],
                                     buf_ref.at[k, pl.ds(t, 1)], sem)

    def start(t, c):
        copy(t, 0).start()
        copy(t, 1).start()
        return c

    def wait(t, c):
        copy(t, 0).wait()
        copy(t, 1).wait()
        return c

    lax.fori_loop(0, tm, start, 0)
    lax.fori_loop(0, tm, wait, 0)
    w0 = r_ref[:, 2:3]
    w1 = r_ref[:, 3:4]
    x = x_ref[...] + (buf_ref[0] * w0 + buf_ref[1] * w1)
    o_ref[...] = _rms(x, g_ref[...]) if final else x


def _combine(dest, x2, rcol, g, ys, tm, final):
    t = x2.shape[0]
    kern = functools.partial(_combine_kernel, tm=tm, final=final)
    return pl.pallas_call(
        kern,
        grid=(t // tm,),
        in_specs=[pl.BlockSpec((1, 2, tm), lambda i: (i, 0, 0), memory_space=pltpu.SMEM),
                  pl.BlockSpec((tm, D_MODEL), lambda i: (i, 0)),
                  pl.BlockSpec((tm, LANES), lambda i: (i, 0)),
                  pl.BlockSpec((1, D_MODEL), lambda i: (0, 0)),
                  pl.BlockSpec(memory_space=pl.ANY)],
        out_specs=pl.BlockSpec((tm, D_MODEL), lambda i: (i, 0)),
        out_shape=jax.ShapeDtypeStruct((t, D_MODEL), F32),
        scratch_shapes=[pltpu.VMEM((2, tm, D_MODEL), F32), pltpu.SemaphoreType.DMA],
        compiler_params=_cparams("arbitrary"),
        name="moe_combine",
    )(dest, x2, rcol, g, ys)


MOE_TILE = 256
CHUNK = 8
LOCAL_ROWS = 2 * MOE_TILE + 256
PACKED = D_MODEL // 2
XS_WIDTH = PACKED + LANES
U32 = jnp.uint32


def _pack_bf16_pairs(x):
    x = x.astype(BF16).astype(F32)
    half = x.shape[1] // 2
    lo = lax.bitcast_convert_type(x[:, :half], U32)
    hi = lax.bitcast_convert_type(x[:, half:], U32)
    return hi | (lo >> 16)


def _unpack_bf16_pairs(words):
    lo = lax.bitcast_convert_type(words << 16, F32)
    hi = lax.bitcast_convert_type(words & U32(0xFFFF0000), F32)
    return jnp.concatenate([lo, hi], axis=1).astype(BF16)
TAB_CHUNKS, TAB_LOCAL, TAB_GLOBAL = 0, 1, 2


def _route_kernel(r_ref, lrow_ref, lcol_ref, tab_ref, meta_ref, cnt_ref, loff_ref, *, tm, nbp):
    phase = pl.program_id(0)
    i = pl.program_id(1)
    e_iota = _row_iota((N_EXPERTS, tm))
    oh0 = e_iota == r_ref[0:1, :].astype(I32)
    oh1 = e_iota == r_ref[1:2, :].astype(I32)
    oh = jnp.where(oh0 | oh1, 1.0, 0.0)
    tile_lane = _lane_iota((N_EXPERTS, LANES)) == i
    hi = lax.Precision.HIGHEST

    @pl.when((phase == 0) & (i == 0))
    def _():
        cnt_ref[...] = jnp.zeros_like(cnt_ref)

    @pl.when(phase == 0)
    def _():
        cnt_ref[...] = jnp.where(tile_lane, jnp.sum(oh, axis=-1, keepdims=True), cnt_ref[...])

    @pl.when((phase == 1) & (i == 0))
    def _():
        cnt = cnt_ref[...]
        n8 = jnp.floor((cnt + (CHUNK - 1)) * (1.0 / CHUNK)) * CHUNK
        er = _row_iota((N_EXPERTS, N_EXPERTS))
        ec = _lane_iota((N_EXPERTS, N_EXPERTS))
        below = jnp.where(er > ec, 1.0, 0.0)
        loff = jnp.dot(below, n8, preferred_element_type=F32, precision=hi)
        rows_e = jnp.sum(n8, axis=-1, keepdims=True) + jnp.zeros_like(n8)
        padded = jnp.floor((rows_e + (MOE_ROWS - 1)) * (1.0 / MOE_ROWS)) * MOE_ROWS
        e_start = jnp.dot(below, padded, preferred_element_type=F32, precision=hi)
        tr = _row_iota((LANES, LANES))
        tc = _lane_iota((LANES, LANES))
        earlier = jnp.where(tr < tc, 1.0, 0.0)
        goff = e_start + jnp.dot(n8, earlier, preferred_element_type=F32, precision=hi)
        loff_ref[...] = loff
        tab_ref[TAB_CHUNKS] = (n8 * (1.0 / CHUNK)).astype(I32)
        tab_ref[TAB_LOCAL] = loff.astype(I32)
        tab_ref[TAB_GLOBAL] = goff.astype(I32)
        reps = nbp // LANES
        pend_b = jnp.concatenate([e_start + padded] * reps, axis=1)
        vend_b = jnp.concatenate([e_start + rows_e] * reps, axis=1)
        b0 = (_lane_iota((N_EXPERTS, nbp)) * MOE_ROWS).astype(F32)
        bexp = jnp.sum(jnp.where(pend_b <= b0, 1.0, 0.0), axis=0, keepdims=True)
        bexp = jnp.minimum(bexp, N_EXPERTS - 1.0)
        is_e = _row_iota((N_EXPERTS, nbp)).astype(F32) == bexp
        vend = jnp.sum(jnp.where(is_e, vend_b, 0.0), axis=0, keepdims=True)
        nvalid = jnp.clip(vend - b0[0:1], 0.0, float(MOE_ROWS))
        total = jnp.max(pend_b, axis=0, keepdims=True) * (1.0 / MOE_ROWS)
        row = _row_iota((8, nbp))
        meta = jnp.where(row == 0, bexp, jnp.where(row == 1, nvalid, jnp.where(row == 2, total, 0.0)))
        meta_ref[...] = meta.astype(I32)

    @pl.when(phase == 1)
    def _():
        su = jnp.where(_row_iota((tm, tm)) < _lane_iota((tm, tm)), 1.0, 0.0).astype(BF16)
        before = jnp.dot(oh.astype(BF16), su, preferred_element_type=F32)
        base = jnp.sum(jnp.where(tile_lane, loff_ref[...], 0.0), axis=-1, keepdims=True) + before
        d0 = jnp.sum(jnp.where(oh0, base, 0.0), axis=0, keepdims=True)
        d1 = jnp.sum(jnp.where(oh1, base, 0.0), axis=0, keepdims=True)
        lrow_ref[0, 0:1, :] = d0.astype(I32)
        lrow_ref[0, 1:2, :] = d1.astype(I32)
        row = _row_iota((LANES, tm))
        lcol_ref[...] = jnp.where(row == 0, d0, jnp.where(row == 1, d1, 0.0)).T


def _route(rrow, tm, nbp):
    t = rrow.shape[1]
    nt = t // tm
    kern = functools.partial(_route_kernel, tm=tm, nbp=nbp)
    return pl.pallas_call(
        kern,
        grid=(2, nt),
        in_specs=[pl.BlockSpec((8, tm), lambda p, i: (0, i))],
        out_specs=[pl.BlockSpec((1, 2, tm), lambda p, i: (i * p, 0, 0)),
                   pl.BlockSpec((tm, LANES), lambda p, i: (i * p, 0)),
                   pl.BlockSpec((3, N_EXPERTS, LANES), lambda p, i: (0, 0, 0)),
                   pl.BlockSpec((8, nbp), lambda p, i: (0, 0))],
        out_shape=[jax.ShapeDtypeStruct((nt, 2, tm), I32), jax.ShapeDtypeStruct((t, LANES), F32),
                   jax.ShapeDtypeStruct((3, N_EXPERTS, LANES), I32), jax.ShapeDtypeStruct((8, nbp), I32)],
        scratch_shapes=[pltpu.VMEM((N_EXPERTS, LANES), F32)] * 2,
        compiler_params=_cparams("arbitrary", "arbitrary"),
        name="moe_route",
    )(rrow)


def _chunk_copies(tab_ref, i, local_ref, global_ref, sem, to_global, action):
    def run(e, c0):
        n = tab_ref[TAB_CHUNKS, e, i]
        lo = tab_ref[TAB_LOCAL, e, i]
        go = tab_ref[TAB_GLOBAL, e, i]

        def chunk(c, c1):
            lsl = local_ref.at[pl.ds(pl.multiple_of(lo + c * CHUNK, CHUNK), CHUNK)]
            gsl = global_ref.at[pl.ds(pl.multiple_of(go + c * CHUNK, CHUNK), CHUNK)]
            cp = pltpu.make_async_copy(lsl, gsl, sem) if to_global else pltpu.make_async_copy(gsl, lsl, sem)
            getattr(cp, action)()
            return c1

        return lax.fori_loop(0, n, chunk, c0)

    lax.fori_loop(0, N_EXPERTS, run, 0)


def _scatter_kernel(tab_ref, meta_ref, lrow_ref, r_ref, h_ref, xs_ref, buf_ref, zero_ref, sem, zsem, *, tm, nb):
    i = pl.program_id(0)

    @pl.when(i == 0)
    def _():
        zero_ref[...] = jnp.zeros_like(zero_ref)
        n_used = meta_ref[2, 0]

        def zcopy(b):
            return pltpu.make_async_copy(zero_ref, xs_ref.at[pl.ds(b * MOE_ROWS, MOE_ROWS)], zsem)

        def needs(b):
            return (b < n_used) & (meta_ref[1, b] < MOE_ROWS)

        def start(b, c):
            @pl.when(needs(b))
            def _():
                zcopy(b).start()
            return c

        def wait(b, c):
            @pl.when(needs(b))
            def _():
                zcopy(b).wait()
            return c

        lax.fori_loop(0, nb, start, 0)
        lax.fori_loop(0, nb, wait, 0)

    rows = _row_iota((LOCAL_ROWS, tm))
    p0 = rows == lrow_ref[0, 0:1, :]
    p1 = rows == lrow_ref[0, 1:2, :]
    perm = jnp.where(p0 | p1, 1.0, 0.0).astype(BF16)
    w = jnp.sum(jnp.where(p0, r_ref[2:3, :], 0.0) + jnp.where(p1, r_ref[3:4, :], 0.0), axis=-1, keepdims=True)
    sorted_rows = jnp.dot(perm, h_ref[...], preferred_element_type=F32)

    def fill(slot):
        buf = buf_ref.at[slot]
        buf[:, 0:PACKED] = _pack_bf16_pairs(sorted_rows)
        buf[:, PACKED:XS_WIDTH] = lax.bitcast_convert_type(w + jnp.zeros((LOCAL_ROWS, LANES), F32), U32)
        _chunk_copies(tab_ref, i, buf, xs_ref, sem.at[slot], True, "start")

    def drain(tile, slot):
        _chunk_copies(tab_ref, tile, buf_ref.at[slot], xs_ref, sem.at[slot], True, "wait")

    even = (i & 1) == 0

    @pl.when(even)
    def _():
        fill(0)

    @pl.when(jnp.logical_not(even))
    def _():
        fill(1)

    @pl.when((i > 0) & even)
    def _():
        drain(i - 1, 1)

    @pl.when((i > 0) & jnp.logical_not(even))
    def _():
        drain(i - 1, 0)

    @pl.when((i == pl.num_programs(0) - 1) & even)
    def _():
        drain(i, 0)

    @pl.when((i == pl.num_programs(0) - 1) & jnp.logical_not(even))
    def _():
        drain(i, 1)


def _scatter(tab, meta, lrow, rrow, h2, tm, nb):
    t = h2.shape[0]
    kern = functools.partial(_scatter_kernel, tm=tm, nb=nb)
    grid_spec = pltpu.PrefetchScalarGridSpec(
        num_scalar_prefetch=2,
        grid=(t // tm,),
        in_specs=[pl.BlockSpec((1, 2, tm), lambda i, *_: (i, 0, 0)),
                  pl.BlockSpec((8, tm), lambda i, *_: (0, i)),
                  pl.BlockSpec((tm, D_MODEL), lambda i, *_: (i, 0))],
        out_specs=pl.BlockSpec(memory_space=pl.ANY),
        scratch_shapes=[pltpu.VMEM((2, LOCAL_ROWS, XS_WIDTH), U32), pltpu.VMEM((MOE_ROWS, XS_WIDTH), U32),
                        pltpu.SemaphoreType.DMA((2,)), pltpu.SemaphoreType.DMA],
    )
    return pl.pallas_call(
        kern,
        grid_spec=grid_spec,
        out_shape=jax.ShapeDtypeStruct((nb * MOE_ROWS, XS_WIDTH), U32),
        compiler_params=_cparams("arbitrary"),
        name="moe_scatter",
    )(tab, meta, lrow, rrow, h2)


def _ffn_kernel(meta_ref, x_ref, wg_ref, wu_ref, wd_ref, o_ref, wgu_b, wd_b):
    b = pl.program_id(0)
    live = b < meta_ref[2, 0]
    prev = meta_ref[0, jnp.maximum(b - 1, 0)]

    @pl.when(live & ((b == 0) | (meta_ref[0, b] != prev)))
    def _():
        wgu_b[:, 0:EXPERT_FF] = wg_ref[0, 0].astype(BF16)
        wgu_b[:, EXPERT_FF:2 * EXPERT_FF] = wu_ref[0, 0].astype(BF16)
        wd_b[...] = wd_ref[0, 0].astype(BF16)

    @pl.when(live)
    def _():
        x = _unpack_bf16_pairs(x_ref[:, 0:PACKED])
        gu = jnp.dot(x, wgu_b[...], preferred_element_type=F32)
        act = (_silu(gu[:, 0:EXPERT_FF]) * gu[:, EXPERT_FF:2 * EXPERT_FF]).astype(BF16)
        y = jnp.dot(act, wd_b[...], preferred_element_type=F32)
        w = lax.bitcast_convert_type(x_ref[:, PACKED:PACKED + 1], F32)
        o_ref[...] = _pack_bf16_pairs(y * w)


def _ffn(meta, xs, wg, wu, wd, layer, nb):
    def blk(b, m):
        return (jnp.maximum(jnp.minimum(b, m[2, 0] - 1), 0), 0)

    def wblk(b, m):
        return (layer, m[0, jnp.maximum(jnp.minimum(b, m[2, 0] - 1), 0)], 0, 0)

    grid_spec = pltpu.PrefetchScalarGridSpec(
        num_scalar_prefetch=1,
        grid=(nb,),
        in_specs=[pl.BlockSpec((MOE_ROWS, XS_WIDTH), blk),
                  pl.BlockSpec((1, 1, D_MODEL, EXPERT_FF), wblk),
                  pl.BlockSpec((1, 1, D_MODEL, EXPERT_FF), wblk),
                  pl.BlockSpec((1, 1, EXPERT_FF, D_MODEL), wblk)],
        out_specs=pl.BlockSpec((MOE_ROWS, PACKED), blk),
        scratch_shapes=[pltpu.VMEM((D_MODEL, 2 * EXPERT_FF), BF16), pltpu.VMEM((EXPERT_FF, D_MODEL), BF16)],
    )
    return pl.pallas_call(
        _ffn_kernel,
        grid_spec=grid_spec,
        out_shape=jax.ShapeDtypeStruct((nb * MOE_ROWS, PACKED), U32),
        compiler_params=_cparams("arbitrary"),
        name="moe_experts",
    )(meta, xs, wg, wu, wd)


def _gather_kernel(tab_ref, lcol_ref, x_ref, g_ref, ys_ref, o_ref, buf_ref, sem, *, tm, final):
    i = pl.program_id(0)

    last = pl.num_programs(0) - 1

    def fetch(tile, slot, action):
        _chunk_copies(tab_ref, tile, buf_ref.at[slot], ys_ref, sem.at[slot], False, action)

    @pl.when(i == 0)
    def _():
        buf_ref[...] = jnp.zeros_like(buf_ref)
        fetch(0, 0, "start")

    even = (i & 1) == 0

    @pl.when((i < last) & even)
    def _():
        fetch(i + 1, 1, "start")

    @pl.when((i < last) & jnp.logical_not(even))
    def _():
        fetch(i + 1, 0, "start")

    col = _lane_iota((tm, LOCAL_ROWS)).astype(F32)
    pick = jnp.where((col == lcol_ref[:, 0:1]) | (col == lcol_ref[:, 1:2]), 1.0, 0.0).astype(BF16)

    def finish(slot):
        fetch(i, slot, "wait")
        x = x_ref[...] + jnp.dot(pick, _unpack_bf16_pairs(buf_ref[slot]), preferred_element_type=F32)
        o_ref[...] = _rms(x, g_ref[...]) if final else x

    @pl.when(even)
    def _():
        finish(0)

    @pl.when(jnp.logical_not(even))
    def _():
        finish(1)


def _gather(tab, lcol, x2, g, ys, tm, final):
    t = x2.shape[0]
    kern = functools.partial(_gather_kernel, tm=tm, final=final)
    grid_spec = pltpu.PrefetchScalarGridSpec(
        num_scalar_prefetch=1,
        grid=(t // tm,),
        in_specs=[pl.BlockSpec((tm, LANES), lambda i, *_: (i, 0)),
                  pl.BlockSpec((tm, D_MODEL), lambda i, *_: (i, 0)),
                  pl.BlockSpec((1, D_MODEL), lambda i, *_: (0, 0)),
                  pl.BlockSpec(memory_space=pl.ANY)],
        out_specs=pl.BlockSpec((tm, D_MODEL), lambda i, *_: (i, 0)),
        scratch_shapes=[pltpu.VMEM((2, LOCAL_ROWS, PACKED), U32), pltpu.SemaphoreType.DMA((2,))],
    )
    return pl.pallas_call(
        kern,
        grid_spec=grid_spec,
        out_shape=jax.ShapeDtypeStruct((t, D_MODEL), F32),
        compiler_params=_cparams("arbitrary"),
        name="moe_combine",
    )(tab, lcol, x2, g, ys)


def _pad_rows(a, rows=8):
    return jnp.zeros((rows, a.shape[-1]), F32).at[:a.shape[0]].set(a.astype(F32))


def _arrange_w_in(w):
    gw = GROUP_WIDTH
    a0 = 0
    b0 = 3 * gw
    c0 = b0 + 3 * gw + N_HEADS
    d0 = c0 + gw + (gw + 4 * SSM_STATE) + N_HEADS
    half = MLA_ROPE // 2
    f_logit = w[:, b0 + 3 * gw:b0 + 3 * gw + N_HEADS]
    dt_raw = w[:, c0 + 3 * gw:c0 + 3 * gw + N_HEADS]
    kr0 = d0 + MLA_Q_LORA + MLA_KV_LORA
    kr = w[:, kr0:kr0 + MLA_ROPE]
    kr_sw = jnp.concatenate([-kr[:, half:], kr[:, :half]], axis=1)
    zeros = lambda n: jnp.zeros((w.shape[0], n), w.dtype)
    misc = jnp.concatenate([f_logit, dt_raw, zeros(MISC_ROPE - 2 * N_HEADS), kr,
                            zeros(LANES - MISC_ROPE - MLA_ROPE)], axis=1)
    misc2 = jnp.concatenate([zeros(MISC_ROPE), kr_sw, zeros(LANES - MISC_ROPE - MLA_ROPE)], axis=1)
    fox_q = w[:, b0:b0 + gw] * (HEAD_DIM ** -0.5 * LOG2E)
    out = jnp.concatenate([w[:, a0:a0 + 3 * gw], fox_q, w[:, b0 + gw:b0 + 2 * gw], w[:, c0:c0 + 3 * gw],
                           w[:, d0:d0 + MLA_Q_LORA + MLA_KV_LORA], misc, misc2], axis=1)
    fox_vt = w[:, b0 + 2 * gw:b0 + 3 * gw].T
    return out.astype(BF16), fox_vt.astype(BF16)


def _arrange_mla(w_uq, w_ukv):
    half = MLA_ROPE // 2
    qd = MLA_NOPE + MLA_ROPE
    wq, wqs, wk, wv = [], [], [], []
    zq = jnp.zeros((MLA_Q_LORA, LANES - qd), w_uq.dtype)
    zk = jnp.zeros((MLA_KV_LORA, LANES - MLA_NOPE), w_ukv.dtype)
    for h in range(N_HEADS):
        q = w_uq[:, h * qd:(h + 1) * qd]
        nope, rope = q[:, :MLA_NOPE], q[:, MLA_NOPE:]
        wq.append(jnp.concatenate([nope, rope, zq], axis=1))
        wqs.append(jnp.concatenate([jnp.zeros_like(nope), -rope[:, half:], rope[:, :half], zq], axis=1))
        kv = w_ukv[:, h * 2 * MLA_NOPE:(h + 1) * 2 * MLA_NOPE]
        wk.append(jnp.concatenate([kv[:, :MLA_NOPE], zk], axis=1))
        wv.append(kv[:, MLA_NOPE:])
    cat = lambda xs: jnp.concatenate(xs, axis=1).astype(BF16)
    return cat(wq), cat(wqs), cat(wk), cat(wv).T


def kernel(x, positions, norm_mix, w_in, conv_a, fox_forget_bias, ssm_conv_w, ssm_conv_b, ssm_dt_bias,
           ssm_a_log, ssm_d, ssm_norm, mla_q_norm, mla_kv_norm, mla_w_uq, mla_w_ukv, w_out, norm_ffn,
           router_group_w, router_group_b, router_expert_w, router_expert_b, expert_w_gate, expert_w_up,
           expert_w_down, norm_final):
    batch, seq, d = x.shape
    t = batch * seq
    depth = w_in.shape[0]
    tm = min(512, t)
    tq = min(ATTN_TQ, seq)
    tmd = min(MOE_TILE, t)
    max_rows = 2 * t + (CHUNK - 1) * N_EXPERTS * (t // tmd) + N_EXPERTS * (MOE_ROWS - 1)
    nb = -(-max_rows // MOE_ROWS)
    nbp = -(-nb // LANES) * LANES

    xf = x.reshape(t, d)
    pos_col = positions.astype(F32).reshape(t, 1)
    cos, sin = _rope_tables(pos_col, tm)

    for l in range(depth):
        w_in_p, w_vt = _arrange_w_in(w_in[l])
        pa, pb, pc, pd, misc, misc2, fox_vt = _inproj(xf, norm_mix[l][None, :], w_in_p, w_vt, tm)

        sp = jnp.zeros((8, LANES), F32)
        sp = sp.at[0, MISC_F:MISC_F + N_HEADS].set(fox_forget_bias[l])
        sp = sp.at[0, MISC_DT:MISC_DT + N_HEADS].set(ssm_dt_bias[l])
        sp = sp.at[1, MISC_DT:MISC_DT + N_HEADS].set(ssm_a_log[l])
        col, rows, fox_q, fox_k, tref = _scalar_prep(misc, sp, pb, batch, seq, tq)

        ya = _conv_mixer(pa, _pad_rows(conv_a[l]), batch, seq)
        yb = _attention(fox_q, fox_k, fox_vt, tref, batch, seq, tq, "fox_attention")
        conv_wb = _pad_rows(jnp.concatenate([ssm_conv_w[l], ssm_conv_b[l][None, :]], axis=0))
        ssd_par = _pad_rows(jnp.stack([jnp.repeat(ssm_d[l], HEAD_DIM), ssm_norm[l]]))
        yc = _ssd_mixer(pc, col, rows, conv_wb, ssd_par, batch, seq)
        wq, wqs, wk, wv = _arrange_mla(mla_w_uq[l], mla_w_ukv[l])
        q, k, v = _mla_prep(pd, misc, misc2, cos, sin, mla_q_norm[l][None, :], mla_kv_norm[l][None, :],
                            wq, wqs, wk, wv, tm)
        yd = _attention(q, k, v, None, batch, seq, tq, "mla_attention")

        wr = jnp.zeros((d, LANES), F32)
        wr = wr.at[:, :N_EXPERT_GROUPS].set(router_group_w[l])
        wr = wr.at[:, N_EXPERT_GROUPS:N_EXPERT_GROUPS + N_EXPERTS].set(router_expert_w[l])
        wr_hi = wr.astype(BF16)
        wr = jnp.concatenate([wr_hi, (wr - wr_hi.astype(F32)).astype(BF16)], axis=1)
        br = jnp.zeros((1, LANES), F32)
        br = br.at[0, :N_EXPERT_GROUPS].set(router_group_b[l])
        br = br.at[0, N_EXPERT_GROUPS:N_EXPERT_GROUPS + N_EXPERTS].set(router_expert_b[l])
        x2, h2, rrow = _outproj(xf, ya, yb, yc, yd, w_out[l].astype(BF16), norm_ffn[l][None, :], wr, br, tm)

        lrow, lcol, tab, meta = _route(rrow, tmd, nbp)
        xs = _scatter(tab, meta, lrow, rrow, h2, tmd, nb)
        ys = _ffn(meta, xs, expert_w_gate, expert_w_up, expert_w_down, l, nb)
        final = l == depth - 1
        xf = _gather(tab, lcol, x2, norm_final[None, :], ys, tmd, final)

    return xf.reshape(batch, seq, d)


def _retile(dest, tm, tmd):
    if tm == tmd:
        return dest
    nt = dest.shape[0]
    return dest.reshape(nt, 2, tm // tmd, tmd).transpose(0, 2, 1, 3).reshape(nt * (tm // tmd), 2, tmd)
```

```python
import functools
import math

import jax
import jax.numpy as jnp
import numpy as np
from jax import lax
from jax.experimental import pallas as pl
from jax.experimental.pallas import tpu as pltpu

F32 = jnp.float32
BF16 = jnp.bfloat16
I32 = jnp.int32

LANES = 128
VMEM_LIMIT_BYTES = 56 * 1024 * 1024

D_MODEL = 1024
RMS_EPS = 1e-6
LOG2E = math.log2(math.e)
GROUP_WIDTH = 256
HEAD_DIM = 64
N_HEADS = 4

CONV_A_WIDTH = 3
SSM_CONV = 4
SSM_STATE = 64
SSM_CHUNK = 256

MLA_NOPE = 64
MLA_ROPE = 32
MLA_Q_LORA = 256
MLA_KV_LORA = 128
ROPE_BASE = 10000.0
MLA_CHUNK = 64
ATTN_TQ = 512
ATTN_RB = 128

N_EXPERT_GROUPS = 4
EXPERTS_PER_GROUP = 8
N_EXPERTS = 32
EXPERT_FF = 256
MOE_ROWS = 512
FFN_SUB = 256

SEG_A = (0, 768)
SEG_B = (768, 1280)
SEG_C = (1280, 2048)
SEG_D = (2048, 2432)
SEG_M = (2432, 2560)
SEG_M2 = (2560, 2688)
IN_COLS_PADDED = 2688
HEAD_PAD = N_HEADS * LANES
AUG_LANE = HEAD_DIM
MISC_F = 0
MISC_DT = 4
MISC_ROPE = 64
COL_CUMF = 0
COL_DT = 4
COL_ACUM = 8
N_SCALAR_ROWS = 16


def _cparams(*sem):
    return pltpu.CompilerParams(dimension_semantics=sem, vmem_limit_bytes=VMEM_LIMIT_BYTES)


def _lane_iota(shape):
    return lax.broadcasted_iota(I32, shape, len(shape) - 1)


def _row_iota(shape):
    return lax.broadcasted_iota(I32, shape, 0)


def _rms(x, g):
    ms = jnp.mean(x * x, axis=-1, keepdims=True)
    return x * lax.rsqrt(ms + RMS_EPS) * g


def _silu(x):
    return x / (1.0 + jnp.exp(-x))


def _softplus(x):
    return jnp.maximum(x, 0.0) + jnp.log(1.0 + jnp.exp(-jnp.abs(x)))


def _shift_rows(x, k):
    rolled = pltpu.roll(x, k, 0)
    return jnp.where(_row_iota(x.shape) >= k, rolled, 0.0)


def _rope_kernel(pos_ref, freq_ref, cos_ref, sin_ref):
    ang = pos_ref[...] * freq_ref[...]
    lane = _lane_iota(ang.shape)
    rope = (lane >= MISC_ROPE) & (lane < MISC_ROPE + MLA_ROPE)
    cos_ref[...] = jnp.where(rope, jnp.cos(ang), jnp.where(lane < MISC_ROPE, 1.0, 0.0))
    sin_ref[...] = jnp.where(rope, jnp.sin(ang), 0.0)


def _rope_tables(pos_col, tm):
    t = pos_col.shape[0]
    half = MLA_ROPE // 2
    inv = ROPE_BASE ** (-np.arange(0, MLA_ROPE, 2, dtype=np.float32) / MLA_ROPE)
    freq = np.zeros((1, LANES), np.float32)
    freq[0, MISC_ROPE:MISC_ROPE + half] = inv
    freq[0, MISC_ROPE + half:MISC_ROPE + MLA_ROPE] = inv
    return pl.pallas_call(
        _rope_kernel,
        grid=(t // tm,),
        in_specs=[pl.BlockSpec((tm, 1), lambda i: (i, 0)),
                  pl.BlockSpec((1, LANES), lambda i: (0, 0))],
        out_specs=[pl.BlockSpec((tm, LANES), lambda i: (i, 0))] * 2,
        out_shape=[jax.ShapeDtypeStruct((t, LANES), F32)] * 2,
        compiler_params=_cparams("parallel"),
        name="rope_tables",
    )(pos_col, jnp.asarray(freq))


def _inproj_kernel(x_ref, g_ref, w_ref, wvt_ref, oa, ob, oc, od, om, om2, ovt):
    h = _rms(x_ref[...], g_ref[...]).astype(BF16)
    for o, (lo, hi) in ((oa, SEG_A), (ob, SEG_B), (oc, SEG_C), (od, SEG_D), (om, SEG_M), (om2, SEG_M2)):
        o[...] = jnp.dot(h, w_ref[:, lo:hi], preferred_element_type=F32).astype(o.dtype)
    ovt[...] = lax.dot_general(wvt_ref[...], h, (((1,), (1,)), ((), ())),
                               preferred_element_type=F32).astype(ovt.dtype)


def _inproj(x, g, w, wvt, tm):
    t = x.shape[0]
    widths = [(s[1] - s[0]) for s in (SEG_A, SEG_B, SEG_C, SEG_D, SEG_M, SEG_M2)]
    dtypes = [BF16, BF16, BF16, BF16, F32, F32]
    return pl.pallas_call(
        _inproj_kernel,
        grid=(t // tm,),
        in_specs=[pl.BlockSpec((tm, D_MODEL), lambda i: (i, 0)),
                  pl.BlockSpec((1, D_MODEL), lambda i: (0, 0)),
                  pl.BlockSpec((D_MODEL, IN_COLS_PADDED), lambda i: (0, 0)),
                  pl.BlockSpec((GROUP_WIDTH, D_MODEL), lambda i: (0, 0))],
        out_specs=[pl.BlockSpec((tm, wd), lambda i: (i, 0)) for wd in widths]
        + [pl.BlockSpec((GROUP_WIDTH, tm), lambda i: (0, i))],
        out_shape=[jax.ShapeDtypeStruct((t, wd), dt) for wd, dt in zip(widths, dtypes)]
        + [jax.ShapeDtypeStruct((GROUP_WIDTH, t), BF16)],
        compiler_params=_cparams("parallel"),
        name="inproj",
    )(x, g, w, wvt)


def _scalar_prep_kernel(m_ref, p_ref, qk_ref, sel_ref, place_ref, const_ref,
                        col_ref, row_ref, qa_ref, ka_ref, tref_ref, *, tq):
    s = m_ref.shape[0]
    tref_ref[...] = jnp.zeros_like(tref_ref)
    tile_ref = jnp.zeros((1, LANES), F32)
    m = m_ref[...]
    bias = p_ref[0:1, :]
    a_log = p_ref[1:2, :]
    lane = _lane_iota(m.shape)
    z = m + bias
    logf = jnp.minimum(z, 0.0) - jnp.log(1.0 + jnp.exp(-jnp.abs(z)))
    dt = _softplus(z)
    a = dt * (-jnp.exp(a_log))
    is_f = lane < MISC_DT
    is_dt = (lane >= MISC_DT) & (lane < MISC_DT + N_HEADS)
    v = jnp.where(is_f, logf, jnp.where(is_dt, a, 0.0))
    r = _row_iota((SSM_CHUNK, SSM_CHUNK))
    c = _lane_iota((SSM_CHUNK, SSM_CHUNK))
    tril = jnp.where(r >= c, 1.0, 0.0).astype(F32)
    carry = jnp.zeros((1, LANES), F32)
    lane_1 = _lane_iota((1, LANES))
    lane_b = _lane_iota((SSM_CHUNK, LANES))
    for ci in range(s // SSM_CHUNK):
        blk = v[ci * SSM_CHUNK:(ci + 1) * SSM_CHUNK]
        cs = jnp.dot(tril, blk, preferred_element_type=F32, precision=lax.Precision.HIGHEST)
        cs = cs + jnp.where(lane_1 < MISC_DT, carry, 0.0)
        carry = cs[SSM_CHUNK - 1:SSM_CHUNK]
        acum = pltpu.roll(cs, COL_ACUM - MISC_DT, 1)
        out = jnp.where(lane_b < MISC_DT, cs * LOG2E,
                        jnp.where(lane_b < COL_ACUM, dt[ci * SSM_CHUNK:(ci + 1) * SSM_CHUNK],
                                  jnp.where(lane_b < COL_ACUM + N_HEADS, acum, 0.0)))
        rows = slice(ci * SSM_CHUNK, (ci + 1) * SSM_CHUNK)
        col_ref[rows, :] = out
        row_ref[0, :, rows] = out.T[:N_SCALAR_ROWS]
        if (ci * SSM_CHUNK) % tq == 0:
            tile_ref = out[0:1, :]
            ti = (ci * SSM_CHUNK) // tq
            tref_ref[0, ti:ti + 1, :] = tile_ref
        c = out - tile_ref
        c_hi = c.astype(BF16)
        r1 = c - c_hi.astype(F32)
        c_mid = r1.astype(BF16)
        c_lo = (r1 - c_mid.astype(F32)).astype(BF16)
        for o_ref, base, qk_lo in ((qa_ref, 0, 0), (ka_ref, 3, GROUP_WIDTH)):
            aug = jnp.dot(qk_ref[rows, qk_lo:qk_lo + GROUP_WIDTH], sel_ref[...], preferred_element_type=F32)
            for term, cc in enumerate((c_hi, c_mid, c_lo)):
                aug = aug + jnp.dot(cc, place_ref[base + term], preferred_element_type=F32)
            o_ref[rows, :] = (aug + const_ref[base // 3:base // 3 + 1, :]).astype(o_ref.dtype)


def _fox_placement():
    sel = np.zeros((GROUP_WIDTH, HEAD_PAD), np.float32)
    place = np.zeros((6, LANES, HEAD_PAD), np.float32)
    const = np.zeros((8, HEAD_PAD), np.float32)
    for h in range(N_HEADS):
        for d in range(HEAD_DIM):
            sel[h * HEAD_DIM + d, h * LANES + d] = 1.0
        a0 = h * LANES + AUG_LANE
        for term in range(3):
            place[term, COL_CUMF + h, a0 + term] = 1.0
            place[3 + term, COL_CUMF + h, a0 + 3 + term] = -1.0
            const[0, a0 + 3 + term] = 1.0
            const[1, a0 + term] = 1.0
    return jnp.asarray(sel, BF16), jnp.asarray(place, BF16), jnp.asarray(const, F32)


def _scalar_prep(misc, params, qk, batch, seq, tq):
    sel, place, const = _fox_placement()
    full = lambda a: pl.BlockSpec(a.shape, lambda b: (0,) * a.ndim)
    return pl.pallas_call(
        functools.partial(_scalar_prep_kernel, tq=tq),
        grid=(batch,),
        in_specs=[pl.BlockSpec((seq, LANES), lambda b: (b, 0)),
                  pl.BlockSpec((8, LANES), lambda b: (0, 0)),
                  pl.BlockSpec((seq, 2 * GROUP_WIDTH), lambda b: (b, 0)),
                  full(sel), full(place), full(const)],
        out_specs=[pl.BlockSpec((seq, LANES), lambda b: (b, 0)),
                   pl.BlockSpec((1, N_SCALAR_ROWS, seq), lambda b: (b, 0, 0)),
                   pl.BlockSpec((seq, HEAD_PAD), lambda b: (b, 0)),
                   pl.BlockSpec((seq, HEAD_PAD), lambda b: (b, 0)),
                   pl.BlockSpec((1, 8, LANES), lambda b: (b, 0, 0))],
        out_shape=[jax.ShapeDtypeStruct((batch * seq, LANES), F32),
                   jax.ShapeDtypeStruct((batch, N_SCALAR_ROWS, seq), F32),
                   jax.ShapeDtypeStruct((batch * seq, HEAD_PAD), BF16),
                   jax.ShapeDtypeStruct((batch * seq, HEAD_PAD), BF16),
                   jax.ShapeDtypeStruct((batch, 8, LANES), F32)],
        compiler_params=_cparams("parallel"),
        name="scalar_prep",
    )(misc, params, qk, sel, place, const)


def _conv_mixer_kernel(p_ref, w_ref, o_ref):
    gw = GROUP_WIDTH
    b_gate = p_ref[:, 0:gw].astype(F32)
    cv = p_ref[:, gw:2 * gw].astype(F32) * p_ref[:, 2 * gw:3 * gw].astype(F32)
    acc = cv * w_ref[CONV_A_WIDTH - 1:CONV_A_WIDTH, :]
    for k in range(1, CONV_A_WIDTH):
        acc = acc + _shift_rows(cv, k) * w_ref[CONV_A_WIDTH - 1 - k:CONV_A_WIDTH - k, :]
    o_ref[...] = (b_gate * acc).astype(o_ref.dtype)


def _conv_mixer(pa, w, batch, seq):
    return pl.pallas_call(
        _conv_mixer_kernel,
        grid=(batch,),
        in_specs=[pl.BlockSpec((seq, 3 * GROUP_WIDTH), lambda b: (b, 0)),
                  pl.BlockSpec((8, GROUP_WIDTH), lambda b: (0, 0))],
        out_specs=pl.BlockSpec((seq, GROUP_WIDTH), lambda b: (b, 0)),
        out_shape=jax.ShapeDtypeStruct((batch * seq, GROUP_WIDTH), BF16),
        compiler_params=_cparams("parallel"),
        name="conv_mixer",
    )(pa, w)


def _pair_lanes(col, base, shape):
    lane = _lane_iota(shape)
    return jnp.where(lane < HEAD_DIM, col[:, base:base + 1], col[:, base + 1:base + 2])


def _ssd_kernel(p_ref, col_ref, row_ref, cw_ref, par_ref, o_ref, u_ref):
    s = p_ref.shape[0]
    q = SSM_CHUNK
    gw = GROUP_WIDTH
    xbc = p_ref[:, gw:3 * gw].astype(F32)
    acc = xbc * cw_ref[SSM_CONV - 1:SSM_CONV, :]
    for k in range(1, SSM_CONV):
        acc = acc + _shift_rows(xbc, k) * cw_ref[SSM_CONV - 1 - k:SSM_CONV - k, :]
    u_ref[...] = _silu(acc + cw_ref[SSM_CONV:SSM_CONV + 1, :])

    d_skip = par_ref[0:1, :]
    norm_g = par_ref[1:2, :]
    lane_q = _lane_iota((q, LANES))
    low = lane_q < HEAD_DIM
    tri = _row_iota((q, q)) >= _lane_iota((q, q))

    def chunk(ci, states):
        r0 = pl.multiple_of(ci * q, q)
        rows = pl.ds(r0, q)
        u = u_ref[rows, :]
        col = col_ref[rows, :]
        bm = u[:, gw:gw + LANES]
        cm = u[:, gw + LANES:gw + 2 * LANES]
        z = p_ref[rows, 0:gw].astype(F32)
        new_states = []
        ys = []
        for g in range(2):
            sel = low if g == 0 else jnp.logical_not(low)
            cg = jnp.where(sel, cm, 0.0).astype(BF16)
            bg = jnp.where(sel, bm, 0.0)
            gmat = lax.dot_general(cg, bm.astype(BF16), (((1,), (1,)), ((), ())),
                                   preferred_element_type=F32)
            xs = u[:, g * LANES:(g + 1) * LANES]
            dt2 = _pair_lanes(col, COL_DT + 2 * g, (q, LANES))
            ac2 = _pair_lanes(col, COL_ACUM + 2 * g, (q, LANES))
            xdt = xs * dt2
            xdt_b = xdt.astype(BF16)
            st = states[g]
            y_off = jnp.dot(cg, st.astype(BF16), preferred_element_type=F32) * jnp.exp(ac2)
            halves = []
            for hh in range(2):
                h = 2 * g + hh
                ac_col = col[:, COL_ACUM + h:COL_ACUM + h + 1]
                ac_row = row_ref[0, COL_ACUM + h:COL_ACUM + h + 1, rows]
                decay = jnp.exp(jnp.where(tri, ac_col - ac_row, -1e30))
                mm = (gmat * decay).astype(BF16)
                halves.append(jnp.dot(mm, xdt_b, preferred_element_type=F32))
            y = jnp.where(low, halves[0], halves[1]) + y_off + d_skip[:, g * LANES:(g + 1) * LANES] * xs
            ys.append(y)
            ac_last = ac2[q - 1:q, :]
            w_end = jnp.exp(ac_last - ac2)
            xw = (xdt * w_end).astype(BF16)
            upd = jnp.dot(bg.T.astype(BF16), xw, preferred_element_type=F32)
            new_states.append(st * jnp.exp(ac_last) + upd)
        yfull = jnp.concatenate(ys, axis=1) * _silu(z)
        o_ref[rows, :] = _rms(yfull, norm_g).astype(o_ref.dtype)
        return tuple(new_states)

    init = (jnp.zeros((LANES, LANES), F32), jnp.zeros((LANES, LANES), F32))
    lax.fori_loop(0, s // q, chunk, init)


def _ssd_mixer(pc, col, rows, conv_wb, par, batch, seq):
    gw = GROUP_WIDTH
    return pl.pallas_call(
        _ssd_kernel,
        grid=(batch,),
        in_specs=[pl.BlockSpec((seq, 3 * gw), lambda b: (b, 0)),
                  pl.BlockSpec((seq, LANES), lambda b: (b, 0)),
                  pl.BlockSpec((1, N_SCALAR_ROWS, seq), lambda b: (b, 0, 0)),
                  pl.BlockSpec((8, 2 * gw), lambda b: (0, 0)),
                  pl.BlockSpec((8, gw), lambda b: (0, 0))],
        out_specs=pl.BlockSpec((seq, gw), lambda b: (b, 0)),
        out_shape=jax.ShapeDtypeStruct((batch * seq, gw), BF16),
        scratch_shapes=[pltpu.VMEM((seq, 2 * gw), F32)],
        compiler_params=_cparams("parallel"),
        name="ssd_mixer",
    )(pc, col, rows, conv_wb, par)


def _mla_prep_kernel(pd_ref, m_ref, m2_ref, cos_ref, sin_ref, nq_ref, nkv_ref,
                     wq_ref, wqs_ref, wk_ref, wvt_ref, q_ref, k_ref, vt_ref):
    cq = _rms(pd_ref[:, 0:MLA_Q_LORA].astype(F32), nq_ref[...]).astype(BF16)
    ckv = _rms(pd_ref[:, MLA_Q_LORA:MLA_Q_LORA + MLA_KV_LORA].astype(F32), nkv_ref[...]).astype(BF16)
    cos = cos_ref[...]
    sin = sin_ref[...]
    cos4 = jnp.concatenate([cos] * N_HEADS, axis=1)
    sin4 = jnp.concatenate([sin] * N_HEADS, axis=1)
    scale = (MLA_NOPE + MLA_ROPE) ** -0.5 * LOG2E
    q = jnp.dot(cq, wq_ref[...], preferred_element_type=F32)
    qs = jnp.dot(cq, wqs_ref[...], preferred_element_type=F32)
    q_ref[...] = ((q * cos4 + qs * sin4) * scale).astype(q_ref.dtype)
    lane = _lane_iota(cos.shape)
    rope = (lane >= MISC_ROPE) & (lane < MISC_ROPE + MLA_ROPE)
    kr = jnp.where(rope, m_ref[...] * cos + m2_ref[...] * sin, 0.0)
    k = jnp.dot(ckv, wk_ref[...], preferred_element_type=F32)
    k_ref[...] = (k + jnp.concatenate([kr] * N_HEADS, axis=1)).astype(k_ref.dtype)
    vt_ref[...] = lax.dot_general(wvt_ref[...], ckv, (((1,), (1,)), ((), ())),
                                  preferred_element_type=F32).astype(vt_ref.dtype)


def _mla_prep(pd, misc, misc2, cos, sin, nq, nkv, wq, wqs, wk, wv, tm):
    t = pd.shape[0]
    hp = N_HEADS * LANES
    full = lambda a: pl.BlockSpec(a.shape, lambda i: (0, 0))
    tile = lambda w: pl.BlockSpec((tm, w), lambda i: (i, 0))
    return pl.pallas_call(
        _mla_prep_kernel,
        grid=(t // tm,),
        in_specs=[tile(MLA_Q_LORA + MLA_KV_LORA), tile(LANES), tile(LANES), tile(LANES), tile(LANES),
                  full(nq), full(nkv), full(wq), full(wqs), full(wk), full(wv)],
        out_specs=[tile(hp), tile(hp), pl.BlockSpec((GROUP_WIDTH, tm), lambda i: (0, i))],
        out_shape=[jax.ShapeDtypeStruct((t, hp), BF16), jax.ShapeDtypeStruct((t, hp), BF16),
                   jax.ShapeDtypeStruct((GROUP_WIDTH, t), BF16)],
        compiler_params=_cparams("parallel"),
        name="mla_prep",
    )(pd, misc, misc2, cos, sin, nq, nkv, wq, wqs, wk, wv)


def _attn_kernel(*refs, fox, tq):
    if fox:
        tref_ref, q_ref, k_ref, vt_ref, o_ref = refs
    else:
        q_ref, k_ref, vt_ref, o_ref = refs
        tref_ref = None
    b = pl.program_id(0)
    i = pl.program_id(1)
    key = _row_iota((tq, tq))
    qry = _lane_iota((tq, tq))
    if fox:
        allowed = key <= qry
    else:
        shift = int(math.log2(MLA_CHUNK))
        allowed = (key >> shift) <= (qry >> shift)
    qs = [q_ref[:, h * LANES:(h + 1) * LANES] for h in range(N_HEADS)]

    def step(j, masked, carry):
        rk = pl.ds(pl.multiple_of(j * tq, tq), tq)
        scores = [lax.dot_general(k_ref[rk, h * LANES:(h + 1) * LANES], qs[h], (((1,), (1,)), ((), ())),
                                  preferred_element_type=F32) for h in range(N_HEADS)]
        probs = []
        for h in range(N_HEADS):
            m, l, _ = carry[h]
            s = scores[h]
            if masked:
                s = jnp.where(allowed, s, -1e30)
            delta = (tref_ref[b, i, h] - tref_ref[b, j, h]) if fox else 0.0
            m_new = jnp.maximum(m, jnp.max(s, axis=0, keepdims=True) + delta)
            alpha = jnp.exp2(m - m_new)
            p = jnp.exp2(s - (m_new - delta))
            l_new = alpha * l + jnp.sum(p, axis=0, keepdims=True)
            probs.append((m_new, l_new, alpha, p.astype(BF16)))
        new = []
        for h in range(N_HEADS):
            pair = h // 2
            m_new, l_new, alpha, p = probs[h]
            pv = jnp.dot(vt_ref[pair * LANES:(pair + 1) * LANES, rk], p, preferred_element_type=F32)
            new.append((m_new, l_new, alpha * carry[h][2] + pv))
        return tuple(new)

    init = tuple((jnp.full((1, tq), -1e30, F32), jnp.zeros((1, tq), F32), jnp.zeros((LANES, tq), F32))
                 for _ in range(N_HEADS))
    carry = lax.fori_loop(0, i, lambda j, c: step(j, False, c), init)
    carry = step(i, True, carry)
    outs = [acc / l for (_, l, acc) in carry]
    top = _row_iota((LANES, tq)) < HEAD_DIM
    o_t = jnp.concatenate([jnp.where(top, outs[0], outs[1]), jnp.where(top, outs[2], outs[3])], axis=0)
    o_ref[...] = o_t.T.astype(o_ref.dtype)


def _attention(q, k, vt, tref, batch, seq, tq, name):
    nq = seq // tq
    fox = tref is not None
    kern = functools.partial(_attn_kernel, fox=fox, tq=tq)
    grid_spec = pltpu.PrefetchScalarGridSpec(
        num_scalar_prefetch=1 if fox else 0,
        grid=(batch, nq),
        in_specs=[pl.BlockSpec((tq, HEAD_PAD), lambda b, i, *_: (b * nq + i, 0)),
                  pl.BlockSpec((seq, HEAD_PAD), lambda b, i, *_: (b, 0)),
                  pl.BlockSpec((GROUP_WIDTH, seq), lambda b, i, *_: (0, b))],
        out_specs=pl.BlockSpec((tq, GROUP_WIDTH), lambda b, i, *_: (b * nq + i, 0)),
    )
    args = ((tref,) if fox else ()) + (q, k, vt)
    return pl.pallas_call(
        kern,
        grid_spec=grid_spec,
        out_shape=jax.ShapeDtypeStruct((batch * seq, GROUP_WIDTH), BF16),
        compiler_params=_cparams("parallel", "arbitrary"),
        name=name,
    )(*args)


def _outproj_kernel(x_ref, ya, yb, yc, yd, w_ref, g_ref, wr_ref, br_ref,
                    x2_ref, h2_ref, rrow_ref):
    y = jnp.concatenate([ya[...], yb[...], yc[...], yd[...]], axis=1)
    x2 = x_ref[...] + jnp.dot(y, w_ref[...], preferred_element_type=F32)
    x2_ref[...] = x2
    h2 = _rms(x2, g_ref[...])
    h2_ref[...] = h2.astype(h2_ref.dtype)
    h_hi = h2.astype(BF16)
    h_lo = (h2 - h_hi.astype(F32)).astype(BF16)
    part = jnp.dot(h_hi, wr_ref[...], preferred_element_type=F32)
    logits = (part[:, 0:LANES] + part[:, LANES:2 * LANES]
              + jnp.dot(h_lo, wr_ref[:, 0:LANES], preferred_element_type=F32) + br_ref[...])
    lt = logits.T
    row = _row_iota(lt.shape)
    neg = -1e30
    big = 1 << 20
    gmask = row < N_EXPERT_GROUPS
    gl = jnp.where(gmask, lt, neg)
    gmax = jnp.max(gl, axis=0, keepdims=True)
    gsum = jnp.sum(jnp.where(gmask, jnp.exp(gl - gmax), 0.0), axis=0, keepdims=True)
    g_w = 1.0 / gsum
    g_idx = jnp.min(jnp.where(gmask & (gl == gmax), row, big), axis=0, keepdims=True)
    e_local = row - N_EXPERT_GROUPS
    emask = (e_local >= 0) & (e_local < N_EXPERTS) & ((e_local >> int(math.log2(EXPERTS_PER_GROUP))) == g_idx)
    el = jnp.where(emask, lt, neg)
    e1v = jnp.max(el, axis=0, keepdims=True)
    esum = jnp.sum(jnp.where(emask, jnp.exp(el - e1v), 0.0), axis=0, keepdims=True)
    i1 = jnp.min(jnp.where(emask & (el == e1v), row, big), axis=0, keepdims=True)
    el2 = jnp.where(row == i1, neg, el)
    e2v = jnp.max(el2, axis=0, keepdims=True)
    i2 = jnp.min(jnp.where(emask & (row != i1) & (el2 == e2v), row, big), axis=0, keepdims=True)
    p1 = 1.0 / esum
    p2 = jnp.exp(e2v - e1v) / esum
    w1 = g_w * (p1 / (p1 + p2))
    w2 = g_w * (p2 / (p1 + p2))
    out_row = _row_iota(rrow_ref.shape)
    rrow_ref[...] = jnp.where(out_row == 0, (i1 - N_EXPERT_GROUPS).astype(F32),
                              jnp.where(out_row == 1, (i2 - N_EXPERT_GROUPS).astype(F32),
                                        jnp.where(out_row == 2, w1, jnp.where(out_row == 3, w2, 0.0))))


def _outproj(x, ya, yb, yc, yd, w, g, wr, br, tm):
    t = x.shape[0]
    full = lambda a: pl.BlockSpec(a.shape, lambda i: (0, 0))
    tile = lambda wd: pl.BlockSpec((tm, wd), lambda i: (i, 0))
    return pl.pallas_call(
        _outproj_kernel,
        grid=(t // tm,),
        in_specs=[tile(D_MODEL)] + [tile(GROUP_WIDTH)] * 4 + [full(w), full(g), full(wr), full(br)],
        out_specs=[tile(D_MODEL), tile(D_MODEL), pl.BlockSpec((8, tm), lambda i: (0, i))],
        out_shape=[jax.ShapeDtypeStruct((t, D_MODEL), F32), jax.ShapeDtypeStruct((t, D_MODEL), BF16),
                   jax.ShapeDtypeStruct((8, t), F32)],
        compiler_params=_cparams("parallel"),
        name="outproj_router",
    )(x, ya, yb, yc, yd, w, g, wr, br)


def _positions_kernel(r_ref, dest_ref, meta_ref, cnt_ref, carry_ref, start_ref, *, tm, nbp):
    phase = pl.program_id(0)
    i = pl.program_id(1)
    e_iota = _row_iota((N_EXPERTS, tm))
    e0 = r_ref[0:1, :].astype(I32)
    e1 = r_ref[1:2, :].astype(I32)
    oh0 = e_iota == e0
    oh1 = e_iota == e1
    oh = jnp.where(oh0 | oh1, 1.0, 0.0)

    @pl.when((phase == 0) & (i == 0))
    def _():
        cnt_ref[...] = jnp.zeros_like(cnt_ref)

    @pl.when(phase == 0)
    def _():
        cnt_ref[...] += jnp.sum(oh, axis=-1, keepdims=True)

    @pl.when((phase == 1) & (i == 0))
    def _():
        cnt = cnt_ref[...]
        padded = jnp.floor((cnt + (MOE_ROWS - 1)) * (1.0 / MOE_ROWS)) * MOE_ROWS
        tril = jnp.where(_row_iota((N_EXPERTS, N_EXPERTS)) >= _lane_iota((N_EXPERTS, N_EXPERTS)), 1.0, 0.0)
        pend = jnp.dot(tril, padded, preferred_element_type=F32, precision=lax.Precision.HIGHEST)
        pstart = pend - padded
        start_ref[...] = pstart
        carry_ref[...] = jnp.zeros_like(carry_ref)
        pend_b = jnp.concatenate([pend] * (nbp // LANES), axis=1)
        vend_b = jnp.concatenate([pstart + cnt] * (nbp // LANES), axis=1)
        b0 = (_lane_iota((N_EXPERTS, nbp)) * MOE_ROWS).astype(F32)
        bexp = jnp.sum(jnp.where(pend_b <= b0, 1.0, 0.0), axis=0, keepdims=True)
        bexp = jnp.minimum(bexp, N_EXPERTS - 1.0)
        is_e = _row_iota((N_EXPERTS, nbp)).astype(F32) == bexp
        vend = jnp.sum(jnp.where(is_e, vend_b, 0.0), axis=0, keepdims=True)
        nvalid = jnp.clip(vend - b0[0:1], 0.0, float(MOE_ROWS))
        total = jnp.max(pend_b, axis=0, keepdims=True) * (1.0 / MOE_ROWS)
        row = _row_iota((8, nbp))
        meta = jnp.where(row == 0, bexp, jnp.where(row == 1, nvalid, jnp.where(row == 2, total, 0.0)))
        meta_ref[...] = meta.astype(I32)

    @pl.when(phase == 1)
    def _():
        su = jnp.where(_row_iota((tm, tm)) < _lane_iota((tm, tm)), 1.0, 0.0).astype(BF16)
        before = jnp.dot(oh.astype(BF16), su, preferred_element_type=F32)
        base = start_ref[:, 0:1] + carry_ref[:, 0:1] + before
        d0 = jnp.sum(jnp.where(oh0, base, 0.0), axis=0, keepdims=True)
        d1 = jnp.sum(jnp.where(oh1, base, 0.0), axis=0, keepdims=True)
        dest_ref[0, 0:1, :] = d0.astype(I32)
        dest_ref[0, 1:2, :] = d1.astype(I32)
        carry_ref[...] += jnp.sum(oh, axis=-1, keepdims=True)


def _positions(rrow, tm, nbp):
    t = rrow.shape[1]
    nt = t // tm
    kern = functools.partial(_positions_kernel, tm=tm, nbp=nbp)
    return pl.pallas_call(
        kern,
        grid=(2, nt),
        in_specs=[pl.BlockSpec((8, tm), lambda p, i: (0, i))],
        out_specs=[pl.BlockSpec((1, 2, tm), lambda p, i: (i * p, 0, 0)),
                   pl.BlockSpec((8, nbp), lambda p, i: (0, 0))],
        out_shape=[jax.ShapeDtypeStruct((nt, 2, tm), I32), jax.ShapeDtypeStruct((8, nbp), I32)],
        scratch_shapes=[pltpu.VMEM((N_EXPERTS, LANES), F32)] * 3,
        compiler_params=_cparams("arbitrary", "arbitrary"),
        name="moe_positions",
    )(rrow)


def _dispatch_kernel(meta_ref, dest_ref, h_ref, xs_ref, zero_ref, sem, zsem, *, tm, nb):
    i = pl.program_id(0)

    @pl.when(i == 0)
    def _():
        zero_ref[...] = jnp.zeros_like(zero_ref)
        n_used = meta_ref[2, 0]

        def zcopy(b):
            return pltpu.make_async_copy(zero_ref, xs_ref.at[pl.ds(b * MOE_ROWS, MOE_ROWS)], zsem)

        def needs(b):
            return (b < n_used) & (meta_ref[1, b] < MOE_ROWS)

        def start(b, c):
            @pl.when(needs(b))
            def _():
                zcopy(b).start()
            return c

        def wait(b, c):
            @pl.when(needs(b))
            def _():
                zcopy(b).wait()
            return c

        lax.fori_loop(0, nb, start, 0)
        lax.fori_loop(0, nb, wait, 0)

    def copy(t, k):
        return pltpu.make_async_copy(h_ref.at[pl.ds(t, 1)], xs_ref.at[pl.ds(dest_ref[0, k, t], 1)], sem)

    def start(t, c):
        copy(t, 0).start()
        copy(t, 1).start()
        return c

    def wait(t, c):
        copy(t, 0).wait()
        copy(t, 1).wait()
        return c

    lax.fori_loop(0, tm, start, 0)
    lax.fori_loop(0, tm, wait, 0)


def _dispatch(meta, dest, h2, tm, nb):
    t = h2.shape[0]
    kern = functools.partial(_dispatch_kernel, tm=tm, nb=nb)
    grid_spec = pltpu.PrefetchScalarGridSpec(
        num_scalar_prefetch=1,
        grid=(t // tm,),
        in_specs=[pl.BlockSpec((1, 2, tm), lambda i, m: (i, 0, 0), memory_space=pltpu.SMEM),
                  pl.BlockSpec((tm, D_MODEL), lambda i, m: (i, 0))],
        out_specs=pl.BlockSpec(memory_space=pl.ANY),
        scratch_shapes=[pltpu.VMEM((MOE_ROWS, D_MODEL), F32),
                        pltpu.SemaphoreType.DMA, pltpu.SemaphoreType.DMA],
    )
    return pl.pallas_call(
        kern,
        grid_spec=grid_spec,
        out_shape=jax.ShapeDtypeStruct((nb * MOE_ROWS, D_MODEL), F32),
        compiler_params=_cparams("arbitrary"),
        name="moe_dispatch",
    )(meta, dest, h2)


def _expert_kernel(meta_ref, x_ref, wg_ref, wu_ref, wd_ref, o_ref):
    b = pl.program_id(0)

    @pl.when(b < meta_ref[2, 0])
    def _():
        x = x_ref[...].astype(BF16)
        gate = jnp.dot(x, wg_ref[0, 0].astype(BF16), preferred_element_type=F32)
        up = jnp.dot(x, wu_ref[0, 0].astype(BF16), preferred_element_type=F32)
        act = (_silu(gate) * up).astype(BF16)
        o_ref[...] = jnp.dot(act, wd_ref[0, 0].astype(BF16), preferred_element_type=F32)


def _experts(meta, xs, wg, wu, wd, layer, nb):
    def blk(b, m):
        return (jnp.minimum(b, m[2, 0] - 1), 0)

    def wblk(b, m):
        return (layer, m[0, jnp.minimum(b, m[2, 0] - 1)], 0, 0)

    grid_spec = pltpu.PrefetchScalarGridSpec(
        num_scalar_prefetch=1,
        grid=(nb,),
        in_specs=[pl.BlockSpec((MOE_ROWS, D_MODEL), blk),
                  pl.BlockSpec((1, 1, D_MODEL, EXPERT_FF), wblk),
                  pl.BlockSpec((1, 1, D_MODEL, EXPERT_FF), wblk),
                  pl.BlockSpec((1, 1, EXPERT_FF, D_MODEL), wblk)],
        out_specs=pl.BlockSpec((MOE_ROWS, D_MODEL), blk),
    )
    return pl.pallas_call(
        _expert_kernel,
        grid_spec=grid_spec,
        out_shape=jax.ShapeDtypeStruct((nb * MOE_ROWS, D_MODEL), F32),
        compiler_params=_cparams("arbitrary"),
        name="moe_experts",
    )(meta, xs, wg, wu, wd)


def _combine_kernel(dest_ref, x_ref, r_ref, g_ref, ys_ref, o_ref, buf_ref, sem, *, tm, final):
    def copy(t, k):
        return pltpu.make_async_copy(ys_ref.at[pl.ds(dest_ref[0, k, t], 1)],
                                     buf_ref.at[k, pl.ds(t, 1)], sem)

    def start(t, c):
        copy(t, 0).start()
        copy(t, 1).start()
        return c

    def wait(t, c):
        copy(t, 0).wait()
        copy(t, 1).wait()
        return c

    lax.fori_loop(0, tm, start, 0)
    lax.fori_loop(0, tm, wait, 0)
    w0 = r_ref[:, 2:3]
    w1 = r_ref[:, 3:4]
    x = x_ref[...] + (buf_ref[0] * w0 + buf_ref[1] * w1)
    o_ref[...] = _rms(x, g_ref[...]) if final else x


def _combine(dest, x2, rcol, g, ys, tm, final):
    t = x2.shape[0]
    kern = functools.partial(_combine_kernel, tm=tm, final=final)
    return pl.pallas_call(
        kern,
        grid=(t // tm,),
        in_specs=[pl.BlockSpec((1, 2, tm), lambda i: (i, 0, 0), memory_space=pltpu.SMEM),
                  pl.BlockSpec((tm, D_MODEL), lambda i: (i, 0)),
                  pl.BlockSpec((tm, LANES), lambda i: (i, 0)),
                  pl.BlockSpec((1, D_MODEL), lambda i: (0, 0)),
                  pl.BlockSpec(memory_space=pl.ANY)],
        out_specs=pl.BlockSpec((tm, D_MODEL), lambda i: (i, 0)),
        out_shape=jax.ShapeDtypeStruct((t, D_MODEL), F32),
        scratch_shapes=[pltpu.VMEM((2, tm, D_MODEL), F32), pltpu.SemaphoreType.DMA],
        compiler_params=_cparams("arbitrary"),
        name="moe_combine",
    )(dest, x2, rcol, g, ys)


MOE_TILE = 256
CHUNK = 8
LOCAL_ROWS = 2 * MOE_TILE + 256
PACKED = D_MODEL // 2
XS_WIDTH = PACKED + LANES
U32 = jnp.uint32


def _pack_bf16_pairs(x, exact=False):
    if not exact:
        x = x.astype(BF16).astype(F32)
    half = x.shape[1] // 2
    lo = lax.bitcast_convert_type(x[:, :half], U32)
    hi = lax.bitcast_convert_type(x[:, half:], U32)
    return hi | (lo >> 16)


def _unpack_bf16_pairs(words):
    lo = lax.bitcast_convert_type(words << 16, F32)
    hi = lax.bitcast_convert_type(words & U32(0xFFFF0000), F32)
    return jnp.concatenate([lo, hi], axis=1).astype(BF16)
TAB_CHUNKS, TAB_LOCAL, TAB_GLOBAL, TAB_TOTAL = 0, 1, 2, 3


def _route_kernel(r_ref, lrow_ref, lcol_ref, tab_ref, meta_ref, cnt_ref, loff_ref, *, tm, nbp):
    phase = pl.program_id(0)
    i = pl.program_id(1)
    e_iota = _row_iota((N_EXPERTS, tm))
    oh0 = e_iota == r_ref[0:1, :].astype(I32)
    oh1 = e_iota == r_ref[1:2, :].astype(I32)
    oh = jnp.where(oh0 | oh1, 1.0, 0.0)
    tile_lane = _lane_iota((N_EXPERTS, LANES)) == i
    hi = lax.Precision.HIGHEST

    @pl.when((phase == 0) & (i == 0))
    def _():
        cnt_ref[...] = jnp.zeros_like(cnt_ref)

    @pl.when(phase == 0)
    def _():
        cnt_ref[...] = jnp.where(tile_lane, jnp.sum(oh, axis=-1, keepdims=True), cnt_ref[...])

    @pl.when((phase == 1) & (i == 0))
    def _():
        cnt = cnt_ref[...]
        n8 = jnp.floor((cnt + (CHUNK - 1)) * (1.0 / CHUNK)) * CHUNK
        er = _row_iota((N_EXPERTS, N_EXPERTS))
        ec = _lane_iota((N_EXPERTS, N_EXPERTS))
        below = jnp.where(er > ec, 1.0, 0.0)
        loff = jnp.dot(below, n8, preferred_element_type=F32, precision=hi)
        rows_e = jnp.sum(n8, axis=-1, keepdims=True) + jnp.zeros_like(n8)
        padded = jnp.floor((rows_e + (MOE_ROWS - 1)) * (1.0 / MOE_ROWS)) * MOE_ROWS
        e_start = jnp.dot(below, padded, preferred_element_type=F32, precision=hi)
        tr = _row_iota((LANES, LANES))
        tc = _lane_iota((LANES, LANES))
        earlier = jnp.where(tr < tc, 1.0, 0.0)
        goff = e_start + jnp.dot(n8, earlier, preferred_element_type=F32, precision=hi)
        loff_ref[...] = loff
        tab_ref[TAB_CHUNKS] = (n8 * (1.0 / CHUNK)).astype(I32)
        tab_ref[TAB_LOCAL] = loff.astype(I32)
        tab_ref[TAB_GLOBAL] = goff.astype(I32)
        tab_ref[TAB_TOTAL] = (jnp.sum(n8, axis=0, keepdims=True) * (1.0 / CHUNK)
                              + jnp.zeros_like(n8)).astype(I32)
        reps = nbp // LANES
        pend_b = jnp.concatenate([e_start + padded] * reps, axis=1)
        vend_b = jnp.concatenate([e_start + rows_e] * reps, axis=1)
        b0 = (_lane_iota((N_EXPERTS, nbp)) * MOE_ROWS).astype(F32)
        bexp = jnp.sum(jnp.where(pend_b <= b0, 1.0, 0.0), axis=0, keepdims=True)
        bexp = jnp.minimum(bexp, N_EXPERTS - 1.0)
        is_e = _row_iota((N_EXPERTS, nbp)).astype(F32) == bexp
        vend = jnp.sum(jnp.where(is_e, vend_b, 0.0), axis=0, keepdims=True)
        nvalid = jnp.clip(vend - b0[0:1], 0.0, float(MOE_ROWS))
        total = jnp.max(pend_b, axis=0, keepdims=True) * (1.0 / MOE_ROWS)
        row = _row_iota((8, nbp))
        meta = jnp.where(row == 0, bexp, jnp.where(row == 1, nvalid, jnp.where(row == 2, total, 0.0)))
        meta_ref[...] = meta.astype(I32)

    @pl.when(phase == 1)
    def _():
        su = jnp.where(_row_iota((tm, tm)) < _lane_iota((tm, tm)), 1.0, 0.0).astype(BF16)
        before = jnp.dot(oh.astype(BF16), su, preferred_element_type=F32)
        base = jnp.sum(jnp.where(tile_lane, loff_ref[...], 0.0), axis=-1, keepdims=True) + before
        d0 = jnp.sum(jnp.where(oh0, base, 0.0), axis=0, keepdims=True)
        d1 = jnp.sum(jnp.where(oh1, base, 0.0), axis=0, keepdims=True)
        lrow_ref[0, 0:1, :] = d0.astype(I32)
        lrow_ref[0, 1:2, :] = d1.astype(I32)
        row = _row_iota((LANES, tm))
        lcol_ref[...] = jnp.where(row == 0, d0, jnp.where(row == 1, d1,
                                  jnp.where(row == 2, r_ref[2:3, :], jnp.where(row == 3, r_ref[3:4, :], 0.0)))).T


def _route(rrow, tm, nbp):
    t = rrow.shape[1]
    nt = t // tm
    kern = functools.partial(_route_kernel, tm=tm, nbp=nbp)
    return pl.pallas_call(
        kern,
        grid=(2, nt),
        in_specs=[pl.BlockSpec((8, tm), lambda p, i: (0, i))],
        out_specs=[pl.BlockSpec((1, 2, tm), lambda p, i: (i * p, 0, 0)),
                   pl.BlockSpec((tm, LANES), lambda p, i: (i * p, 0)),
                   pl.BlockSpec((4, N_EXPERTS, LANES), lambda p, i: (0, 0, 0)),
                   pl.BlockSpec((8, nbp), lambda p, i: (0, 0))],
        out_shape=[jax.ShapeDtypeStruct((nt, 2, tm), I32), jax.ShapeDtypeStruct((t, LANES), F32),
                   jax.ShapeDtypeStruct((4, N_EXPERTS, LANES), I32), jax.ShapeDtypeStruct((8, nbp), I32)],
        scratch_shapes=[pltpu.VMEM((N_EXPERTS, LANES), F32)] * 2,
        compiler_params=_cparams("arbitrary", "arbitrary"),
        name="moe_route",
    )(rrow)


def _chunk_copies(tab_ref, i, local_ref, global_ref, sem, to_global, action):
    if action == "wait":
        lsl = local_ref.at[pl.ds(0, CHUNK)]
        gsl = global_ref.at[pl.ds(0, CHUNK)]
        cp = pltpu.make_async_copy(lsl, gsl, sem) if to_global else pltpu.make_async_copy(gsl, lsl, sem)

        def one(c, c1):
            cp.wait()
            return c1

        lax.fori_loop(0, tab_ref[TAB_TOTAL, 0, i], one, 0)
        return

    def run(e, c0):
        n = tab_ref[TAB_CHUNKS, e, i]
        lo = tab_ref[TAB_LOCAL, e, i]
        go = tab_ref[TAB_GLOBAL, e, i]

        def chunk(c, c1):
            lsl = local_ref.at[pl.ds(pl.multiple_of(lo + c * CHUNK, CHUNK), CHUNK)]
            gsl = global_ref.at[pl.ds(pl.multiple_of(go + c * CHUNK, CHUNK), CHUNK)]
            cp = pltpu.make_async_copy(lsl, gsl, sem) if to_global else pltpu.make_async_copy(gsl, lsl, sem)
            getattr(cp, action)()
            return c1

        return lax.fori_loop(0, n, chunk, c0)

    lax.fori_loop(0, N_EXPERTS, run, 0)


def _scatter_kernel(tab_ref, meta_ref, lrow_ref, lcol_ref, h_ref, xs_ref, buf_ref, zero_ref, sem, zsem, *, tm, nb):
    i = pl.program_id(0)

    @pl.when(i == 0)
    def _():
        zero_ref[...] = jnp.zeros_like(zero_ref)
        n_used = meta_ref[2, 0]

        def zcopy(b):
            sub = lax.shift_right_logical(meta_ref[1, b], int(math.log2(FFN_SUB)))
            start = pl.multiple_of(b * MOE_ROWS + sub * FFN_SUB, FFN_SUB)
            return pltpu.make_async_copy(zero_ref, xs_ref.at[pl.ds(start, FFN_SUB)], zsem)

        def needs(b):
            return (b < n_used) & ((meta_ref[1, b] & (FFN_SUB - 1)) != 0)

        def start(b, c):
            @pl.when(needs(b))
            def _():
                zcopy(b).start()
            return c

        def wait(b, c):
            @pl.when(needs(b))
            def _():
                zcopy(b).wait()
            return c

        lax.fori_loop(0, nb, start, 0)
        lax.fori_loop(0, nb, wait, 0)

    rows = _row_iota((LOCAL_ROWS, tm))
    p0 = rows == lrow_ref[0, 0:1, :]
    p1 = rows == lrow_ref[0, 1:2, :]
    perm0 = jnp.where(p0, 1.0, 0.0).astype(BF16)
    perm1 = jnp.where(p1, 1.0, 0.0).astype(BF16)
    sorted_rows = jnp.dot(perm0 + perm1, h_ref[...], preferred_element_type=F32)
    wc = lcol_ref[...]
    lane = _lane_iota(wc.shape)
    w_sorted = jnp.zeros((LOCAL_ROWS, LANES), F32)
    for perm, src_lane in ((perm0, 2), (perm1, 3)):
        rest = jnp.where(lane == src_lane, wc, 0.0)
        for _ in range(3):
            term = rest.astype(BF16)
            w_sorted = w_sorted + jnp.dot(perm, term, preferred_element_type=F32)
            rest = rest - term.astype(F32)

    def fill(slot):
        buf = buf_ref.at[slot]
        buf[:, 0:PACKED] = _pack_bf16_pairs(sorted_rows, exact=True)
        buf[:, PACKED:XS_WIDTH] = lax.bitcast_convert_type(w_sorted, U32)
        _chunk_copies(tab_ref, i, buf, xs_ref, sem.at[slot], True, "start")

    def drain(tile, slot):
        _chunk_copies(tab_ref, tile, buf_ref.at[slot], xs_ref, sem.at[slot], True, "wait")

    even = (i & 1) == 0

    @pl.when(even)
    def _():
        fill(0)

    @pl.when(jnp.logical_not(even))
    def _():
        fill(1)

    @pl.when((i > 0) & even)
    def _():
        drain(i - 1, 1)

    @pl.when((i > 0) & jnp.logical_not(even))
    def _():
        drain(i - 1, 0)

    @pl.when((i == pl.num_programs(0) - 1) & even)
    def _():
        drain(i, 0)

    @pl.when((i == pl.num_programs(0) - 1) & jnp.logical_not(even))
    def _():
        drain(i, 1)


def _scatter(tab, meta, lrow, lcol, h2, tm, nb):
    t = h2.shape[0]
    kern = functools.partial(_scatter_kernel, tm=tm, nb=nb)
    grid_spec = pltpu.PrefetchScalarGridSpec(
        num_scalar_prefetch=2,
        grid=(t // tm,),
        in_specs=[pl.BlockSpec((1, 2, tm), lambda i, *_: (i, 0, 0)),
                  pl.BlockSpec((tm, LANES), lambda i, *_: (i, 0)),
                  pl.BlockSpec((tm, D_MODEL), lambda i, *_: (i, 0))],
        out_specs=pl.BlockSpec(memory_space=pl.ANY),
        scratch_shapes=[pltpu.VMEM((2, LOCAL_ROWS, XS_WIDTH), U32), pltpu.VMEM((FFN_SUB, XS_WIDTH), U32),
                        pltpu.SemaphoreType.DMA((2,)), pltpu.SemaphoreType.DMA],
    )
    return pl.pallas_call(
        kern,
        grid_spec=grid_spec,
        out_shape=jax.ShapeDtypeStruct((nb * MOE_ROWS, XS_WIDTH), U32),
        compiler_params=_cparams("arbitrary"),
        name="moe_scatter",
    )(tab, meta, lrow, lcol, h2)


def _ffn_kernel(meta_ref, x_ref, wg_ref, wu_ref, wd_ref, o_ref, wgu_b, wd_b):
    b = pl.program_id(0)
    live = b < meta_ref[2, 0]
    prev = meta_ref[0, jnp.maximum(b - 1, 0)]

    @pl.when(live & ((b == 0) | (meta_ref[0, b] != prev)))
    def _():
        wgu_b[:, 0:EXPERT_FF] = wg_ref[0, 0].astype(BF16)
        wgu_b[:, EXPERT_FF:2 * EXPERT_FF] = wu_ref[0, 0].astype(BF16)
        wd_b[...] = wd_ref[0, 0].astype(BF16)

    nvalid = meta_ref[1, jnp.maximum(jnp.minimum(b, meta_ref[2, 0] - 1), 0)]
    for sub in range(MOE_ROWS // FFN_SUB):
        rows = slice(sub * FFN_SUB, (sub + 1) * FFN_SUB)
        used = live & (nvalid > sub * FFN_SUB)

        @pl.when(used)
        def _():
            x = _unpack_bf16_pairs(x_ref[rows, 0:PACKED])
            gu = jnp.dot(x, wgu_b[...], preferred_element_type=F32)
            act = (_silu(gu[:, 0:EXPERT_FF]) * gu[:, EXPERT_FF:2 * EXPERT_FF]).astype(BF16)
            y = jnp.dot(act, wd_b[...], preferred_element_type=F32)
            wl = lax.bitcast_convert_type(x_ref[rows, PACKED:XS_WIDTH], F32)
            w = wl[:, 2:3] + wl[:, 3:4]
            o_ref[rows, :] = _pack_bf16_pairs(y * w)

        @pl.when(live & jnp.logical_not(used))
        def _():
            o_ref[rows, :] = jnp.zeros((FFN_SUB, PACKED), U32)


def _ffn(meta, xs, wg, wu, wd, layer, nb):
    def blk(b, m):
        return (jnp.maximum(jnp.minimum(b, m[2, 0] - 1), 0), 0)

    def wblk(b, m):
        return (layer, m[0, jnp.maximum(jnp.minimum(b, m[2, 0] - 1), 0)], 0, 0)

    grid_spec = pltpu.PrefetchScalarGridSpec(
        num_scalar_prefetch=1,
        grid=(nb,),
        in_specs=[pl.BlockSpec((MOE_ROWS, XS_WIDTH), blk),
                  pl.BlockSpec((1, 1, D_MODEL, EXPERT_FF), wblk),
                  pl.BlockSpec((1, 1, D_MODEL, EXPERT_FF), wblk),
                  pl.BlockSpec((1, 1, EXPERT_FF, D_MODEL), wblk)],
        out_specs=pl.BlockSpec((MOE_ROWS, PACKED), blk),
        scratch_shapes=[pltpu.VMEM((D_MODEL, 2 * EXPERT_FF), BF16), pltpu.VMEM((EXPERT_FF, D_MODEL), BF16)],
    )
    return pl.pallas_call(
        _ffn_kernel,
        grid_spec=grid_spec,
        out_shape=jax.ShapeDtypeStruct((nb * MOE_ROWS, PACKED), U32),
        compiler_params=_cparams("arbitrary"),
        name="moe_experts",
    )(meta, xs, wg, wu, wd)


def _gather_kernel(tab_ref, lcol_ref, x_ref, g_ref, ys_ref, o_ref, buf_ref, sem, *, tm, final):
    i = pl.program_id(0)

    last = pl.num_programs(0) - 1

    def fetch(tile, slot, action):
        _chunk_copies(tab_ref, tile, buf_ref.at[slot], ys_ref, sem.at[slot], False, action)

    @pl.when(i == 0)
    def _():
        buf_ref[...] = jnp.zeros_like(buf_ref)
        fetch(0, 0, "start")

    even = (i & 1) == 0

    @pl.when((i < last) & even)
    def _():
        fetch(i + 1, 1, "start")

    @pl.when((i < last) & jnp.logical_not(even))
    def _():
        fetch(i + 1, 0, "start")

    col = _lane_iota((tm, LOCAL_ROWS)).astype(F32)
    pick = jnp.where((col == lcol_ref[:, 0:1]) | (col == lcol_ref[:, 1:2]), 1.0, 0.0).astype(BF16)

    def finish(slot):
        fetch(i, slot, "wait")
        x = x_ref[...] + jnp.dot(pick, _unpack_bf16_pairs(buf_ref[slot]), preferred_element_type=F32)
        o_ref[...] = _rms(x, g_ref[...]) if final else x

    @pl.when(even)
    def _():
        finish(0)

    @pl.when(jnp.logical_not(even))
    def _():
        finish(1)


def _gather(tab, lcol, x2, g, ys, tm, final):
    t = x2.shape[0]
    kern = functools.partial(_gather_kernel, tm=tm, final=final)
    grid_spec = pltpu.PrefetchScalarGridSpec(
        num_scalar_prefetch=1,
        grid=(t // tm,),
        in_specs=[pl.BlockSpec((tm, LANES), lambda i, *_: (i, 0)),
                  pl.BlockSpec((tm, D_MODEL), lambda i, *_: (i, 0)),
                  pl.BlockSpec((1, D_MODEL), lambda i, *_: (0, 0)),
                  pl.BlockSpec(memory_space=pl.ANY)],
        out_specs=pl.BlockSpec((tm, D_MODEL), lambda i, *_: (i, 0)),
        scratch_shapes=[pltpu.VMEM((2, LOCAL_ROWS, PACKED), U32), pltpu.SemaphoreType.DMA((2,))],
    )
    return pl.pallas_call(
        kern,
        grid_spec=grid_spec,
        out_shape=jax.ShapeDtypeStruct((t, D_MODEL), F32),
        compiler_params=_cparams("arbitrary"),
        name="moe_combine",
    )(tab, lcol, x2, g, ys)


def _pad_rows(a, rows=8):
    return jnp.zeros((rows, a.shape[-1]), F32).at[:a.shape[0]].set(a.astype(F32))


def _arrange_w_in(w):
    gw = GROUP_WIDTH
    a0 = 0
    b0 = 3 * gw
    c0 = b0 + 3 * gw + N_HEADS
    d0 = c0 + gw + (gw + 4 * SSM_STATE) + N_HEADS
    half = MLA_ROPE // 2
    f_logit = w[:, b0 + 3 * gw:b0 + 3 * gw + N_HEADS]
    dt_raw = w[:, c0 + 3 * gw:c0 + 3 * gw + N_HEADS]
    kr0 = d0 + MLA_Q_LORA + MLA_KV_LORA
    kr = w[:, kr0:kr0 + MLA_ROPE]
    kr_sw = jnp.concatenate([-kr[:, half:], kr[:, :half]], axis=1)
    zeros = lambda n: jnp.zeros((w.shape[0], n), w.dtype)
    misc = jnp.concatenate([f_logit, dt_raw, zeros(MISC_ROPE - 2 * N_HEADS), kr,
                            zeros(LANES - MISC_ROPE - MLA_ROPE)], axis=1)
    misc2 = jnp.concatenate([zeros(MISC_ROPE), kr_sw, zeros(LANES - MISC_ROPE - MLA_ROPE)], axis=1)
    fox_q = w[:, b0:b0 + gw] * (HEAD_DIM ** -0.5 * LOG2E)
    out = jnp.concatenate([w[:, a0:a0 + 3 * gw], fox_q, w[:, b0 + gw:b0 + 2 * gw], w[:, c0:c0 + 3 * gw],
                           w[:, d0:d0 + MLA_Q_LORA + MLA_KV_LORA], misc, misc2], axis=1)
    fox_vt = w[:, b0 + 2 * gw:b0 + 3 * gw].T
    return out.astype(BF16), fox_vt.astype(BF16)


def _arrange_mla(w_uq, w_ukv):
    half = MLA_ROPE // 2
    qd = MLA_NOPE + MLA_ROPE
    wq, wqs, wk, wv = [], [], [], []
    zq = jnp.zeros((MLA_Q_LORA, LANES - qd), w_uq.dtype)
    zk = jnp.zeros((MLA_KV_LORA, LANES - MLA_NOPE), w_ukv.dtype)
    for h in range(N_HEADS):
        q = w_uq[:, h * qd:(h + 1) * qd]
        nope, rope = q[:, :MLA_NOPE], q[:, MLA_NOPE:]
        wq.append(jnp.concatenate([nope, rope, zq], axis=1))
        wqs.append(jnp.concatenate([jnp.zeros_like(nope), -rope[:, half:], rope[:, :half], zq], axis=1))
        kv = w_ukv[:, h * 2 * MLA_NOPE:(h + 1) * 2 * MLA_NOPE]
        wk.append(jnp.concatenate([kv[:, :MLA_NOPE], zk], axis=1))
        wv.append(kv[:, MLA_NOPE:])
    cat = lambda xs: jnp.concatenate(xs, axis=1).astype(BF16)
    return cat(wq), cat(wqs), cat(wk), cat(wv).T


def kernel(x, positions, norm_mix, w_in, conv_a, fox_forget_bias, ssm_conv_w, ssm_conv_b, ssm_dt_bias,
           ssm_a_log, ssm_d, ssm_norm, mla_q_norm, mla_kv_norm, mla_w_uq, mla_w_ukv, w_out, norm_ffn,
           router_group_w, router_group_b, router_expert_w, router_expert_b, expert_w_gate, expert_w_up,
           expert_w_down, norm_final):
    batch, seq, d = x.shape
    t = batch * seq
    depth = w_in.shape[0]
    tm = min(512, t)
    tq = min(ATTN_TQ, seq)
    tmd = min(MOE_TILE, t)
    max_rows = 2 * t + (CHUNK - 1) * N_EXPERTS * (t // tmd) + N_EXPERTS * (MOE_ROWS - 1)
    nb = -(-max_rows // MOE_ROWS)
    nbp = -(-nb // LANES) * LANES

    xf = x.reshape(t, d)
    pos_col = positions.astype(F32).reshape(t, 1)
    cos, sin = _rope_tables(pos_col, tm)

    for l in range(depth):
        w_in_p, w_vt = _arrange_w_in(w_in[l])
        pa, pb, pc, pd, misc, misc2, fox_vt = _inproj(xf, norm_mix[l][None, :], w_in_p, w_vt, tm)

        sp = jnp.zeros((8, LANES), F32)
        sp = sp.at[0, MISC_F:MISC_F + N_HEADS].set(fox_forget_bias[l])
        sp = sp.at[0, MISC_DT:MISC_DT + N_HEADS].set(ssm_dt_bias[l])
        sp = sp.at[1, MISC_DT:MISC_DT + N_HEADS].set(ssm_a_log[l])
        col, rows, fox_q, fox_k, tref = _scalar_prep(misc, sp, pb, batch, seq, tq)

        ya = _conv_mixer(pa, _pad_rows(conv_a[l]), batch, seq)
        yb = _attention(fox_q, fox_k, fox_vt, tref, batch, seq, tq, "fox_attention")
        conv_wb = _pad_rows(jnp.concatenate([ssm_conv_w[l], ssm_conv_b[l][None, :]], axis=0))
        ssd_par = _pad_rows(jnp.stack([jnp.repeat(ssm_d[l], HEAD_DIM), ssm_norm[l]]))
        yc = _ssd_mixer(pc, col, rows, conv_wb, ssd_par, batch, seq)
        wq, wqs, wk, wv = _arrange_mla(mla_w_uq[l], mla_w_ukv[l])
        q, k, v = _mla_prep(pd, misc, misc2, cos, sin, mla_q_norm[l][None, :], mla_kv_norm[l][None, :],
                            wq, wqs, wk, wv, tm)
        yd = _attention(q, k, v, None, batch, seq, tq, "mla_attention")

        wr = jnp.zeros((d, LANES), F32)
        wr = wr.at[:, :N_EXPERT_GROUPS].set(router_group_w[l])
        wr = wr.at[:, N_EXPERT_GROUPS:N_EXPERT_GROUPS + N_EXPERTS].set(router_expert_w[l])
        wr_hi = wr.astype(BF16)
        wr = jnp.concatenate([wr_hi, (wr - wr_hi.astype(F32)).astype(BF16)], axis=1)
        br = jnp.zeros((1, LANES), F32)
        br = br.at[0, :N_EXPERT_GROUPS].set(router_group_b[l])
        br = br.at[0, N_EXPERT_GROUPS:N_EXPERT_GROUPS + N_EXPERTS].set(router_expert_b[l])
        x2, h2, rrow = _outproj(xf, ya, yb, yc, yd, w_out[l].astype(BF16), norm_ffn[l][None, :], wr, br, tm)

        lrow, lcol, tab, meta = _route(rrow, tmd, nbp)
        xs = _scatter(tab, meta, lrow, lcol, h2, tmd, nb)
        ys = _ffn(meta, xs, expert_w_gate, expert_w_up, expert_w_down, l, nb)
        final = l == depth - 1
        xf = _gather(tab, lcol, x2, norm_final[None, :], ys, tmd, final)

    return xf.reshape(batch, seq, d)


def _retile(dest, tm, tmd):
    if tm == tmd:
        return dest
    nt = dest.shape[0]
    return dest.reshape(nt, 2, tm // tmd, tmd).transpose(0, 2, 1, 3).reshape(nt * (tm // tmd), 2, tmd)
```

```python
import functools
import math

import jax
import jax.numpy as jnp
import numpy as np
from jax import lax
from jax.experimental import pallas as pl
from jax.experimental.pallas import tpu as pltpu

F32 = jnp.float32
BF16 = jnp.bfloat16
I32 = jnp.int32

LANES = 128
VMEM_LIMIT_BYTES = 56 * 1024 * 1024

D_MODEL = 1024
RMS_EPS = 1e-6
LOG2E = math.log2(math.e)
GROUP_WIDTH = 256
HEAD_DIM = 64
N_HEADS = 4

CONV_A_WIDTH = 3
SSM_CONV = 4
SSM_STATE = 64
SSM_CHUNK = 256

MLA_NOPE = 64
MLA_ROPE = 32
MLA_Q_LORA = 256
MLA_KV_LORA = 128
ROPE_BASE = 10000.0
MLA_CHUNK = 64
ATTN_TQ = 512
ATTN_RB = 128

N_EXPERT_GROUPS = 4
EXPERTS_PER_GROUP = 8
N_EXPERTS = 32
EXPERT_FF = 256
MOE_ROWS = 512
FFN_SUB = 256

SEG_A = (0, 768)
SEG_B = (768, 1280)
SEG_C = (1280, 2048)
SEG_D = (2048, 2432)
SEG_M = (2432, 2560)
SEG_M2 = (2560, 2688)
IN_COLS_PADDED = 2688
HEAD_PAD = N_HEADS * LANES
AUG_LANE = HEAD_DIM
MISC_F = 0
MISC_DT = 4
MISC_ROPE = 64
COL_CUMF = 0
COL_DT = 4
COL_ACUM = 8
N_SCALAR_ROWS = 16


def _cparams(*sem):
    return pltpu.CompilerParams(dimension_semantics=sem, vmem_limit_bytes=VMEM_LIMIT_BYTES)


def _lane_iota(shape):
    return lax.broadcasted_iota(I32, shape, len(shape) - 1)


def _row_iota(shape):
    return lax.broadcasted_iota(I32, shape, 0)


def _rms(x, g):
    ms = jnp.mean(x * x, axis=-1, keepdims=True)
    return x * lax.rsqrt(ms + RMS_EPS) * g


def _silu(x):
    return x / (1.0 + jnp.exp(-x))


def _softplus(x):
    return jnp.maximum(x, 0.0) + jnp.log(1.0 + jnp.exp(-jnp.abs(x)))


def _shift_rows(x, k):
    rolled = pltpu.roll(x, k, 0)
    return jnp.where(_row_iota(x.shape) >= k, rolled, 0.0)


def _rope_kernel(pos_ref, freq_ref, cos_ref, sin_ref):
    ang = pos_ref[...] * freq_ref[...]
    lane = _lane_iota(ang.shape)
    rope = (lane >= MISC_ROPE) & (lane < MISC_ROPE + MLA_ROPE)
    cos_ref[...] = jnp.where(rope, jnp.cos(ang), jnp.where(lane < MISC_ROPE, 1.0, 0.0))
    sin_ref[...] = jnp.where(rope, jnp.sin(ang), 0.0)


def _rope_tables(pos_col, tm):
    t = pos_col.shape[0]
    half = MLA_ROPE // 2
    inv = ROPE_BASE ** (-np.arange(0, MLA_ROPE, 2, dtype=np.float32) / MLA_ROPE)
    freq = np.zeros((1, LANES), np.float32)
    freq[0, MISC_ROPE:MISC_ROPE + half] = inv
    freq[0, MISC_ROPE + half:MISC_ROPE + MLA_ROPE] = inv
    return pl.pallas_call(
        _rope_kernel,
        grid=(t // tm,),
        in_specs=[pl.BlockSpec((tm, 1), lambda i: (i, 0)),
                  pl.BlockSpec((1, LANES), lambda i: (0, 0))],
        out_specs=[pl.BlockSpec((tm, LANES), lambda i: (i, 0))] * 2,
        out_shape=[jax.ShapeDtypeStruct((t, LANES), F32)] * 2,
        compiler_params=_cparams("parallel"),
        name="rope_tables",
    )(pos_col, jnp.asarray(freq))


def _inproj_kernel(x_ref, g_ref, w_ref, wvt_ref, oa, ob, oc, od, om, om2, ovt):
    h = _rms(x_ref[...], g_ref[...]).astype(BF16)
    for o, (lo, hi) in ((oa, SEG_A), (ob, SEG_B), (oc, SEG_C), (od, SEG_D), (om, SEG_M), (om2, SEG_M2)):
        o[...] = jnp.dot(h, w_ref[:, lo:hi], preferred_element_type=F32).astype(o.dtype)
    ovt[...] = lax.dot_general(wvt_ref[...], h, (((1,), (1,)), ((), ())),
                               preferred_element_type=F32).astype(ovt.dtype)


def _inproj(x, g, w, wvt, tm):
    t = x.shape[0]
    widths = [(s[1] - s[0]) for s in (SEG_A, SEG_B, SEG_C, SEG_D, SEG_M, SEG_M2)]
    dtypes = [BF16, BF16, BF16, BF16, F32, F32]
    return pl.pallas_call(
        _inproj_kernel,
        grid=(t // tm,),
        in_specs=[pl.BlockSpec((tm, D_MODEL), lambda i: (i, 0)),
                  pl.BlockSpec((1, D_MODEL), lambda i: (0, 0)),
                  pl.BlockSpec((D_MODEL, IN_COLS_PADDED), lambda i: (0, 0)),
                  pl.BlockSpec((GROUP_WIDTH, D_MODEL), lambda i: (0, 0))],
        out_specs=[pl.BlockSpec((tm, wd), lambda i: (i, 0)) for wd in widths]
        + [pl.BlockSpec((GROUP_WIDTH, tm), lambda i: (0, i))],
        out_shape=[jax.ShapeDtypeStruct((t, wd), dt) for wd, dt in zip(widths, dtypes)]
        + [jax.ShapeDtypeStruct((GROUP_WIDTH, t), BF16)],
        compiler_params=_cparams("parallel"),
        name="inproj",
    )(x, g, w, wvt)


def _scalar_prep_kernel(m_ref, p_ref, qk_ref, sel_ref, place_ref, const_ref,
                        col_ref, row_ref, qa_ref, ka_ref, tref_ref, *, tq):
    s = m_ref.shape[0]
    tref_ref[...] = jnp.zeros_like(tref_ref)
    tile_ref = jnp.zeros((1, LANES), F32)
    m = m_ref[...]
    bias = p_ref[0:1, :]
    a_log = p_ref[1:2, :]
    lane = _lane_iota(m.shape)
    z = m + bias
    logf = jnp.minimum(z, 0.0) - jnp.log(1.0 + jnp.exp(-jnp.abs(z)))
    dt = _softplus(z)
    a = dt * (-jnp.exp(a_log))
    is_f = lane < MISC_DT
    is_dt = (lane >= MISC_DT) & (lane < MISC_DT + N_HEADS)
    v = jnp.where(is_f, logf, jnp.where(is_dt, a, 0.0))
    r = _row_iota((SSM_CHUNK, SSM_CHUNK))
    c = _lane_iota((SSM_CHUNK, SSM_CHUNK))
    tril = jnp.where(r >= c, 1.0, 0.0).astype(F32)
    carry = jnp.zeros((1, LANES), F32)
    lane_1 = _lane_iota((1, LANES))
    lane_b = _lane_iota((SSM_CHUNK, LANES))
    for ci in range(s // SSM_CHUNK):
        blk = v[ci * SSM_CHUNK:(ci + 1) * SSM_CHUNK]
        cs = jnp.dot(tril, blk, preferred_element_type=F32, precision=lax.Precision.HIGHEST)
        cs = cs + jnp.where(lane_1 < MISC_DT, carry, 0.0)
        carry = cs[SSM_CHUNK - 1:SSM_CHUNK]
        acum = pltpu.roll(cs, COL_ACUM - MISC_DT, 1)
        out = jnp.where(lane_b < MISC_DT, cs * LOG2E,
                        jnp.where(lane_b < COL_ACUM, dt[ci * SSM_CHUNK:(ci + 1) * SSM_CHUNK],
                                  jnp.where(lane_b < COL_ACUM + N_HEADS, acum, 0.0)))
        rows = slice(ci * SSM_CHUNK, (ci + 1) * SSM_CHUNK)
        col_ref[rows, :] = out
        row_ref[0, :, rows] = out.T[:N_SCALAR_ROWS]
        if (ci * SSM_CHUNK) % tq == 0:
            tile_ref = out[0:1, :]
            ti = (ci * SSM_CHUNK) // tq
            tref_ref[0, ti:ti + 1, :] = tile_ref
        c = out - tile_ref
        c_hi = c.astype(BF16)
        r1 = c - c_hi.astype(F32)
        c_mid = r1.astype(BF16)
        c_lo = (r1 - c_mid.astype(F32)).astype(BF16)
        for o_ref, base, qk_lo in ((qa_ref, 0, 0), (ka_ref, 3, GROUP_WIDTH)):
            aug = jnp.dot(qk_ref[rows, qk_lo:qk_lo + GROUP_WIDTH], sel_ref[...], preferred_element_type=F32)
            for term, cc in enumerate((c_hi, c_mid, c_lo)):
                aug = aug + jnp.dot(cc, place_ref[base + term], preferred_element_type=F32)
            o_ref[rows, :] = (aug + const_ref[base // 3:base // 3 + 1, :]).astype(o_ref.dtype)


def _fox_placement():
    sel = np.zeros((GROUP_WIDTH, HEAD_PAD), np.float32)
    place = np.zeros((6, LANES, HEAD_PAD), np.float32)
    const = np.zeros((8, HEAD_PAD), np.float32)
    for h in range(N_HEADS):
        for d in range(HEAD_DIM):
            sel[h * HEAD_DIM + d, h * LANES + d] = 1.0
        a0 = h * LANES + AUG_LANE
        for term in range(3):
            place[term, COL_CUMF + h, a0 + term] = 1.0
            place[3 + term, COL_CUMF + h, a0 + 3 + term] = -1.0
            const[0, a0 + 3 + term] = 1.0
            const[1, a0 + term] = 1.0
    return jnp.asarray(sel, BF16), jnp.asarray(place, BF16), jnp.asarray(const, F32)


def _scalar_prep(misc, params, qk, batch, seq, tq):
    sel, place, const = _fox_placement()
    full = lambda a: pl.BlockSpec(a.shape, lambda b: (0,) * a.ndim)
    return pl.pallas_call(
        functools.partial(_scalar_prep_kernel, tq=tq),
        grid=(batch,),
        in_specs=[pl.BlockSpec((seq, LANES), lambda b: (b, 0)),
                  pl.BlockSpec((8, LANES), lambda b: (0, 0)),
                  pl.BlockSpec((seq, 2 * GROUP_WIDTH), lambda b: (b, 0)),
                  full(sel), full(place), full(const)],
        out_specs=[pl.BlockSpec((seq, LANES), lambda b: (b, 0)),
                   pl.BlockSpec((1, N_SCALAR_ROWS, seq), lambda b: (b, 0, 0)),
                   pl.BlockSpec((seq, HEAD_PAD), lambda b: (b, 0)),
                   pl.BlockSpec((seq, HEAD_PAD), lambda b: (b, 0)),
                   pl.BlockSpec((1, 8, LANES), lambda b: (b, 0, 0))],
        out_shape=[jax.ShapeDtypeStruct((batch * seq, LANES), F32),
                   jax.ShapeDtypeStruct((batch, N_SCALAR_ROWS, seq), F32),
                   jax.ShapeDtypeStruct((batch * seq, HEAD_PAD), BF16),
                   jax.ShapeDtypeStruct((batch * seq, HEAD_PAD), BF16),
                   jax.ShapeDtypeStruct((batch, 8, LANES), F32)],
        compiler_params=_cparams("parallel"),
        name="scalar_prep",
    )(misc, params, qk, sel, place, const)


def _conv_mixer_kernel(p_ref, w_ref, o_ref):
    gw = GROUP_WIDTH
    b_gate = p_ref[:, 0:gw].astype(F32)
    cv = p_ref[:, gw:2 * gw].astype(F32) * p_ref[:, 2 * gw:3 * gw].astype(F32)
    acc = cv * w_ref[CONV_A_WIDTH - 1:CONV_A_WIDTH, :]
    for k in range(1, CONV_A_WIDTH):
        acc = acc + _shift_rows(cv, k) * w_ref[CONV_A_WIDTH - 1 - k:CONV_A_WIDTH - k, :]
    o_ref[...] = (b_gate * acc).astype(o_ref.dtype)


def _conv_mixer(pa, w, batch, seq):
    return pl.pallas_call(
        _conv_mixer_kernel,
        grid=(batch,),
        in_specs=[pl.BlockSpec((seq, 3 * GROUP_WIDTH), lambda b: (b, 0)),
                  pl.BlockSpec((8, GROUP_WIDTH), lambda b: (0, 0))],
        out_specs=pl.BlockSpec((seq, GROUP_WIDTH), lambda b: (b, 0)),
        out_shape=jax.ShapeDtypeStruct((batch * seq, GROUP_WIDTH), BF16),
        compiler_params=_cparams("parallel"),
        name="conv_mixer",
    )(pa, w)


def _pair_lanes(col, base, shape):
    lane = _lane_iota(shape)
    return jnp.where(lane < HEAD_DIM, col[:, base:base + 1], col[:, base + 1:base + 2])


def _ssd_kernel(p_ref, col_ref, row_ref, cw_ref, par_ref, o_ref, u_ref):
    s = p_ref.shape[0]
    q = SSM_CHUNK
    gw = GROUP_WIDTH
    xbc = p_ref[:, gw:3 * gw].astype(F32)
    acc = xbc * cw_ref[SSM_CONV - 1:SSM_CONV, :]
    for k in range(1, SSM_CONV):
        acc = acc + _shift_rows(xbc, k) * cw_ref[SSM_CONV - 1 - k:SSM_CONV - k, :]
    u_ref[...] = _silu(acc + cw_ref[SSM_CONV:SSM_CONV + 1, :])

    d_skip = par_ref[0:1, :]
    norm_g = par_ref[1:2, :]
    lane_q = _lane_iota((q, LANES))
    low = lane_q < HEAD_DIM
    tri = _row_iota((q, q)) >= _lane_iota((q, q))

    def chunk(ci, states):
        rows = pl.ds(ci * q, q)
        u = u_ref[rows, :]
        col = col_ref[rows, :]
        bm = u[:, gw:gw + LANES]
        cm = u[:, gw + LANES:gw + 2 * LANES]
        z = p_ref[rows, 0:gw].astype(F32)
        new_states = []
        ys = []
        for g in range(2):
            sel = low if g == 0 else jnp.logical_not(low)
            cg = jnp.where(sel, cm, 0.0).astype(BF16)
            bg = jnp.where(sel, bm, 0.0)
            gmat = lax.dot_general(cg, bm.astype(BF16), (((1,), (1,)), ((), ())),
                                   preferred_element_type=F32)
            xs = u[:, g * LANES:(g + 1) * LANES]
            dt2 = _pair_lanes(col, COL_DT + 2 * g, (q, LANES))
            ac2 = _pair_lanes(col, COL_ACUM + 2 * g, (q, LANES))
            xdt = xs * dt2
            xdt_b = xdt.astype(BF16)
            st = states[g]
            y_off = jnp.dot(cg, st.astype(BF16), preferred_element_type=F32) * jnp.exp(ac2)
            halves = []
            for hh in range(2):
                h = 2 * g + hh
                ac_col = col[:, COL_ACUM + h:COL_ACUM + h + 1]
                ac_row = row_ref[0, COL_ACUM + h:COL_ACUM + h + 1, rows]
                decay = jnp.exp(jnp.where(tri, ac_col - ac_row, -1e30))
                mm = (gmat * decay).astype(BF16)
                halves.append(jnp.dot(mm, xdt_b, preferred_element_type=F32))
            y = jnp.where(low, halves[0], halves[1]) + y_off + d_skip[:, g * LANES:(g + 1) * LANES] * xs
            ys.append(y)
            ac_last = ac2[q - 1:q, :]
            w_end = jnp.exp(ac_last - ac2)
            xw = (xdt * w_end).astype(BF16)
            upd = jnp.dot(bg.T.astype(BF16), xw, preferred_element_type=F32)
            new_states.append(st * jnp.exp(ac_last) + upd)
        yfull = jnp.concatenate(ys, axis=1) * _silu(z)
        o_ref[rows, :] = _rms(yfull, norm_g).astype(o_ref.dtype)
        return tuple(new_states)

    init = (jnp.zeros((LANES, LANES), F32), jnp.zeros((LANES, LANES), F32))
    states = init
    for ci in range(s // q):
        states = chunk(ci, states)


def _ssd_mixer(pc, col, rows, conv_wb, par, batch, seq):
    gw = GROUP_WIDTH
    return pl.pallas_call(
        _ssd_kernel,
        grid=(batch,),
        in_specs=[pl.BlockSpec((seq, 3 * gw), lambda b: (b, 0)),
                  pl.BlockSpec((seq, LANES), lambda b: (b, 0)),
                  pl.BlockSpec((1, N_SCALAR_ROWS, seq), lambda b: (b, 0, 0)),
                  pl.BlockSpec((8, 2 * gw), lambda b: (0, 0)),
                  pl.BlockSpec((8, gw), lambda b: (0, 0))],
        out_specs=pl.BlockSpec((seq, gw), lambda b: (b, 0)),
        out_shape=jax.ShapeDtypeStruct((batch * seq, gw), BF16),
        scratch_shapes=[pltpu.VMEM((seq, 2 * gw), F32)],
        compiler_params=_cparams("parallel"),
        name="ssd_mixer",
    )(pc, col, rows, conv_wb, par)


def _mla_prep_kernel(pd_ref, m_ref, m2_ref, cos_ref, sin_ref, nq_ref, nkv_ref,
                     wq_ref, wqs_ref, wk_ref, wvt_ref, q_ref, k_ref, vt_ref):
    cq = _rms(pd_ref[:, 0:MLA_Q_LORA].astype(F32), nq_ref[...]).astype(BF16)
    ckv = _rms(pd_ref[:, MLA_Q_LORA:MLA_Q_LORA + MLA_KV_LORA].astype(F32), nkv_ref[...]).astype(BF16)
    cos = cos_ref[...]
    sin = sin_ref[...]
    cos4 = jnp.concatenate([cos] * N_HEADS, axis=1)
    sin4 = jnp.concatenate([sin] * N_HEADS, axis=1)
    scale = (MLA_NOPE + MLA_ROPE) ** -0.5 * LOG2E
    q = jnp.dot(cq, wq_ref[...], preferred_element_type=F32)
    qs = jnp.dot(cq, wqs_ref[...], preferred_element_type=F32)
    q_ref[...] = ((q * cos4 + qs * sin4) * scale).astype(q_ref.dtype)
    lane = _lane_iota(cos.shape)
    rope = (lane >= MISC_ROPE) & (lane < MISC_ROPE + MLA_ROPE)
    kr = jnp.where(rope, m_ref[...] * cos + m2_ref[...] * sin, 0.0)
    k = jnp.dot(ckv, wk_ref[...], preferred_element_type=F32)
    k_ref[...] = (k + jnp.concatenate([kr] * N_HEADS, axis=1)).astype(k_ref.dtype)
    vt_ref[...] = lax.dot_general(wvt_ref[...], ckv, (((1,), (1,)), ((), ())),
                                  preferred_element_type=F32).astype(vt_ref.dtype)


def _mla_prep(pd, misc, misc2, cos, sin, nq, nkv, wq, wqs, wk, wv, tm):
    t = pd.shape[0]
    hp = N_HEADS * LANES
    full = lambda a: pl.BlockSpec(a.shape, lambda i: (0, 0))
    tile = lambda w: pl.BlockSpec((tm, w), lambda i: (i, 0))
    return pl.pallas_call(
        _mla_prep_kernel,
        grid=(t // tm,),
        in_specs=[tile(MLA_Q_LORA + MLA_KV_LORA), tile(LANES), tile(LANES), tile(LANES), tile(LANES),
                  full(nq), full(nkv), full(wq), full(wqs), full(wk), full(wv)],
        out_specs=[tile(hp), tile(hp), pl.BlockSpec((GROUP_WIDTH, tm), lambda i: (0, i))],
        out_shape=[jax.ShapeDtypeStruct((t, hp), BF16), jax.ShapeDtypeStruct((t, hp), BF16),
                   jax.ShapeDtypeStruct((GROUP_WIDTH, t), BF16)],
        compiler_params=_cparams("parallel"),
        name="mla_prep",
    )(pd, misc, misc2, cos, sin, nq, nkv, wq, wqs, wk, wv)


def _attn_kernel(*refs, fox, tq):
    if fox:
        tref_ref, q_ref, k_ref, vt_ref, o_ref = refs
    else:
        q_ref, k_ref, vt_ref, o_ref = refs
        tref_ref = None
    b = pl.program_id(0)
    i = pl.program_id(1)
    key = _row_iota((tq, tq))
    qry = _lane_iota((tq, tq))
    if fox:
        allowed = key <= qry
    else:
        shift = int(math.log2(MLA_CHUNK))
        allowed = (key >> shift) <= (qry >> shift)
    qs = [q_ref[:, h * LANES:(h + 1) * LANES] for h in range(N_HEADS)]

    def step(j, masked, carry):
        rk = pl.ds(pl.multiple_of(j * tq, tq), tq)
        scores = [lax.dot_general(k_ref[rk, h * LANES:(h + 1) * LANES], qs[h], (((1,), (1,)), ((), ())),
                                  preferred_element_type=F32) for h in range(N_HEADS)]
        probs = []
        for h in range(N_HEADS):
            m, l, _ = carry[h]
            s = scores[h]
            if masked:
                s = jnp.where(allowed, s, -1e30)
            delta = (tref_ref[b, i, h] - tref_ref[b, j, h]) if fox else 0.0
            m_new = jnp.maximum(m, jnp.max(s, axis=0, keepdims=True) + delta)
            alpha = jnp.exp2(m - m_new)
            p = jnp.exp2(s - (m_new - delta))
            l_new = alpha * l + jnp.sum(p, axis=0, keepdims=True)
            probs.append((m_new, l_new, alpha, p.astype(BF16)))
        new = []
        for h in range(N_HEADS):
            pair = h // 2
            m_new, l_new, alpha, p = probs[h]
            pv = jnp.dot(vt_ref[pair * LANES:(pair + 1) * LANES, rk], p, preferred_element_type=F32)
            new.append((m_new, l_new, alpha * carry[h][2] + pv))
        return tuple(new)

    init = tuple((jnp.full((1, tq), -1e30, F32), jnp.zeros((1, tq), F32), jnp.zeros((LANES, tq), F32))
                 for _ in range(N_HEADS))
    carry = lax.fori_loop(0, i, lambda j, c: step(j, False, c), init)
    carry = step(i, True, carry)
    outs = [acc / l for (_, l, acc) in carry]
    top = _row_iota((LANES, tq)) < HEAD_DIM
    o_t = jnp.concatenate([jnp.where(top, outs[0], outs[1]), jnp.where(top, outs[2], outs[3])], axis=0)
    o_ref[...] = o_t.T.astype(o_ref.dtype)


def _attention(q, k, vt, tref, batch, seq, tq, name):
    nq = seq // tq
    fox = tref is not None
    kern = functools.partial(_attn_kernel, fox=fox, tq=tq)
    grid_spec = pltpu.PrefetchScalarGridSpec(
        num_scalar_prefetch=1 if fox else 0,
        grid=(batch, nq),
        in_specs=[pl.BlockSpec((tq, HEAD_PAD), lambda b, i, *_: (b * nq + i, 0)),
                  pl.BlockSpec((seq, HEAD_PAD), lambda b, i, *_: (b, 0)),
                  pl.BlockSpec((GROUP_WIDTH, seq), lambda b, i, *_: (0, b))],
        out_specs=pl.BlockSpec((tq, GROUP_WIDTH), lambda b, i, *_: (b * nq + i, 0)),
    )
    args = ((tref,) if fox else ()) + (q, k, vt)
    return pl.pallas_call(
        kern,
        grid_spec=grid_spec,
        out_shape=jax.ShapeDtypeStruct((batch * seq, GROUP_WIDTH), BF16),
        compiler_params=_cparams("parallel", "arbitrary"),
        name=name,
    )(*args)


def _outproj_kernel(x_ref, ya, yb, yc, yd, w_ref, g_ref, wr_ref, br_ref,
                    x2_ref, h2_ref, rrow_ref, cnt_ref, *, tm, moe_tile):
    y = jnp.concatenate([ya[...], yb[...], yc[...], yd[...]], axis=1)
    x2 = x_ref[...] + jnp.dot(y, w_ref[...], preferred_element_type=F32)
    x2_ref[...] = x2
    h2 = _rms(x2, g_ref[...])
    h2_ref[...] = h2.astype(h2_ref.dtype)
    h_hi = h2.astype(BF16)
    h_lo = (h2 - h_hi.astype(F32)).astype(BF16)
    part = jnp.dot(h_hi, wr_ref[...], preferred_element_type=F32)
    logits = (part[:, 0:LANES] + part[:, LANES:2 * LANES]
              + jnp.dot(h_lo, wr_ref[:, 0:LANES], preferred_element_type=F32) + br_ref[...])
    lt = logits.T
    row = _row_iota(lt.shape)
    neg = -1e30
    big = 1 << 20
    gmask = (row >= N_EXPERTS) & (row < N_EXPERTS + N_EXPERT_GROUPS)
    gl = jnp.where(gmask, lt, neg)
    gmax = jnp.max(gl, axis=0, keepdims=True)
    gsum = jnp.sum(jnp.where(gmask, jnp.exp(gl - gmax), 0.0), axis=0, keepdims=True)
    g_w = 1.0 / gsum
    g_idx = jnp.min(jnp.where(gmask & (gl == gmax), row, big), axis=0, keepdims=True) - N_EXPERTS
    emask = (row < N_EXPERTS) & ((row >> int(math.log2(EXPERTS_PER_GROUP))) == g_idx)
    el = jnp.where(emask, lt, neg)
    e1v = jnp.max(el, axis=0, keepdims=True)
    esum = jnp.sum(jnp.where(emask, jnp.exp(el - e1v), 0.0), axis=0, keepdims=True)
    i1 = jnp.min(jnp.where(emask & (el == e1v), row, big), axis=0, keepdims=True)
    el2 = jnp.where(row == i1, neg, el)
    e2v = jnp.max(el2, axis=0, keepdims=True)
    i2 = jnp.min(jnp.where(emask & (row != i1) & (el2 == e2v), row, big), axis=0, keepdims=True)
    p1 = 1.0 / esum
    p2 = jnp.exp(e2v - e1v) / esum
    w1 = g_w * (p1 / (p1 + p2))
    w2 = g_w * (p2 / (p1 + p2))
    out_row = _row_iota(rrow_ref.shape)
    rrow_ref[...] = jnp.where(out_row == 0, i1.astype(F32),
                              jnp.where(out_row == 1, i2.astype(F32),
                                        jnp.where(out_row == 2, w1, jnp.where(out_row == 3, w2, 0.0))))
    step = pl.program_id(0)

    @pl.when(step == 0)
    def _():
        cnt_ref[...] = jnp.zeros_like(cnt_ref)

    chosen = jnp.where((row == i1) | (row == i2), 1.0, 0.0)
    cnt = cnt_ref[...]
    lane = _lane_iota(cnt.shape)
    tiles = tm // moe_tile
    for k in range(tiles):
        n = jnp.sum(chosen[0:N_EXPERTS, k * moe_tile:(k + 1) * moe_tile], axis=-1, keepdims=True)
        cnt = jnp.where(lane == step * tiles + k, n, cnt)
    cnt_ref[...] = cnt


def _outproj(x, ya, yb, yc, yd, w, g, wr, br, tm, moe_tile):
    t = x.shape[0]
    full = lambda a: pl.BlockSpec(a.shape, lambda i: (0, 0))
    tile = lambda wd: pl.BlockSpec((tm, wd), lambda i: (i, 0))
    return pl.pallas_call(
        functools.partial(_outproj_kernel, tm=tm, moe_tile=moe_tile),
        grid=(t // tm,),
        in_specs=[tile(D_MODEL)] + [tile(GROUP_WIDTH)] * 4 + [full(w), full(g), full(wr), full(br)],
        out_specs=[tile(D_MODEL), tile(D_MODEL), pl.BlockSpec((8, tm), lambda i: (0, i)),
                   pl.BlockSpec((N_EXPERTS, LANES), lambda i: (0, 0))],
        out_shape=[jax.ShapeDtypeStruct((t, D_MODEL), F32), jax.ShapeDtypeStruct((t, D_MODEL), BF16),
                   jax.ShapeDtypeStruct((8, t), F32), jax.ShapeDtypeStruct((N_EXPERTS, LANES), F32)],
        compiler_params=_cparams("arbitrary"),
        name="outproj_router",
    )(x, ya, yb, yc, yd, w, g, wr, br)


def _positions_kernel(r_ref, dest_ref, meta_ref, cnt_ref, carry_ref, start_ref, *, tm, nbp):
    phase = pl.program_id(0)
    i = pl.program_id(1)
    e_iota = _row_iota((N_EXPERTS, tm))
    e0 = r_ref[0:1, :].astype(I32)
    e1 = r_ref[1:2, :].astype(I32)
    oh0 = e_iota == e0
    oh1 = e_iota == e1
    oh = jnp.where(oh0 | oh1, 1.0, 0.0)

    @pl.when((phase == 0) & (i == 0))
    def _():
        cnt_ref[...] = jnp.zeros_like(cnt_ref)

    @pl.when(phase == 0)
    def _():
        cnt_ref[...] += jnp.sum(oh, axis=-1, keepdims=True)

    @pl.when((phase == 1) & (i == 0))
    def _():
        cnt = cnt_ref[...]
        padded = jnp.floor((cnt + (MOE_ROWS - 1)) * (1.0 / MOE_ROWS)) * MOE_ROWS
        tril = jnp.where(_row_iota((N_EXPERTS, N_EXPERTS)) >= _lane_iota((N_EXPERTS, N_EXPERTS)), 1.0, 0.0)
        pend = jnp.dot(tril, padded, preferred_element_type=F32, precision=lax.Precision.HIGHEST)
        pstart = pend - padded
        start_ref[...] = pstart
        carry_ref[...] = jnp.zeros_like(carry_ref)
        pend_b = jnp.concatenate([pend] * (nbp // LANES), axis=1)
        vend_b = jnp.concatenate([pstart + cnt] * (nbp // LANES), axis=1)
        b0 = (_lane_iota((N_EXPERTS, nbp)) * MOE_ROWS).astype(F32)
        bexp = jnp.sum(jnp.where(pend_b <= b0, 1.0, 0.0), axis=0, keepdims=True)
        bexp = jnp.minimum(bexp, N_EXPERTS - 1.0)
        is_e = _row_iota((N_EXPERTS, nbp)).astype(F32) == bexp
        vend = jnp.sum(jnp.where(is_e, vend_b, 0.0), axis=0, keepdims=True)
        nvalid = jnp.clip(vend - b0[0:1], 0.0, float(MOE_ROWS))
        total = jnp.max(pend_b, axis=0, keepdims=True) * (1.0 / MOE_ROWS)
        row = _row_iota((8, nbp))
        meta = jnp.where(row == 0, bexp, jnp.where(row == 1, nvalid, jnp.where(row == 2, total, 0.0)))
        meta_ref[...] = meta.astype(I32)

    @pl.when(phase == 1)
    def _():
        su = jnp.where(_row_iota((tm, tm)) < _lane_iota((tm, tm)), 1.0, 0.0).astype(BF16)
        before = jnp.dot(oh.astype(BF16), su, preferred_element_type=F32)
        base = start_ref[:, 0:1] + carry_ref[:, 0:1] + before
        d0 = jnp.sum(jnp.where(oh0, base, 0.0), axis=0, keepdims=True)
        d1 = jnp.sum(jnp.where(oh1, base, 0.0), axis=0, keepdims=True)
        dest_ref[0, 0:1, :] = d0.astype(I32)
        dest_ref[0, 1:2, :] = d1.astype(I32)
        carry_ref[...] += jnp.sum(oh, axis=-1, keepdims=True)


def _positions(rrow, tm, nbp):
    t = rrow.shape[1]
    nt = t // tm
    kern = functools.partial(_positions_kernel, tm=tm, nbp=nbp)
    return pl.pallas_call(
        kern,
        grid=(2, nt),
        in_specs=[pl.BlockSpec((8, tm), lambda p, i: (0, i))],
        out_specs=[pl.BlockSpec((1, 2, tm), lambda p, i: (i * p, 0, 0)),
                   pl.BlockSpec((8, nbp), lambda p, i: (0, 0))],
        out_shape=[jax.ShapeDtypeStruct((nt, 2, tm), I32), jax.ShapeDtypeStruct((8, nbp), I32)],
        scratch_shapes=[pltpu.VMEM((N_EXPERTS, LANES), F32)] * 3,
        compiler_params=_cparams("arbitrary", "arbitrary"),
        name="moe_positions",
    )(rrow)


def _dispatch_kernel(meta_ref, dest_ref, h_ref, xs_ref, zero_ref, sem, zsem, *, tm, nb):
    i = pl.program_id(0)

    @pl.when(i == 0)
    def _():
        zero_ref[...] = jnp.zeros_like(zero_ref)
        n_used = meta_ref[2, 0]

        def zcopy(b):
            return pltpu.make_async_copy(zero_ref, xs_ref.at[pl.ds(b * MOE_ROWS, MOE_ROWS)], zsem)

        def needs(b):
            return (b < n_used) & (meta_ref[1, b] < MOE_ROWS)

        def start(b, c):
            @pl.when(needs(b))
            def _():
                zcopy(b).start()
            return c

        def wait(b, c):
            @pl.when(needs(b))
            def _():
                zcopy(b).wait()
            return c

        lax.fori_loop(0, nb, start, 0)
        lax.fori_loop(0, nb, wait, 0)

    def copy(t, k):
        return pltpu.make_async_copy(h_ref.at[pl.ds(t, 1)], xs_ref.at[pl.ds(dest_ref[0, k, t], 1)], sem)

    def start(t, c):
        copy(t, 0).start()
        copy(t, 1).start()
        return c

    def wait(t, c):
        copy(t, 0).wait()
        copy(t, 1).wait()
        return c

    lax.fori_loop(0, tm, start, 0)
    lax.fori_loop(0, tm, wait, 0)


def _dispatch(meta, dest, h2, tm, nb):
    t = h2.shape[0]
    kern = functools.partial(_dispatch_kernel, tm=tm, nb=nb)
    grid_spec = pltpu.PrefetchScalarGridSpec(
        num_scalar_prefetch=1,
        grid=(t // tm,),
        in_specs=[pl.BlockSpec((1, 2, tm), lambda i, m: (i, 0, 0), memory_space=pltpu.SMEM),
                  pl.BlockSpec((tm, D_MODEL), lambda i, m: (i, 0))],
        out_specs=pl.BlockSpec(memory_space=pl.ANY),
        scratch_shapes=[pltpu.VMEM((MOE_ROWS, D_MODEL), F32),
                        pltpu.SemaphoreType.DMA, pltpu.SemaphoreType.DMA],
    )
    return pl.pallas_call(
        kern,
        grid_spec=grid_spec,
        out_shape=jax.ShapeDtypeStruct((nb * MOE_ROWS, D_MODEL), F32),
        compiler_params=_cparams("arbitrary"),
        name="moe_dispatch",
    )(meta, dest, h2)


def _expert_kernel(meta_ref, x_ref, wg_ref, wu_ref, wd_ref, o_ref):
    b = pl.program_id(0)

    @pl.when(b < meta_ref[2, 0])
    def _():
        x = x_ref[...].astype(BF16)
        gate = jnp.dot(x, wg_ref[0, 0].astype(BF16), preferred_element_type=F32)
        up = jnp.dot(x, wu_ref[0, 0].astype(BF16), preferred_element_type=F32)
        act = (_silu(gate) * up).astype(BF16)
        o_ref[...] = jnp.dot(act, wd_ref[0, 0].astype(BF16), preferred_element_type=F32)


def _experts(meta, xs, wg, wu, wd, layer, nb):
    def blk(b, m):
        return (jnp.minimum(b, m[2, 0] - 1), 0)

    def wblk(b, m):
        return (layer, m[0, jnp.minimum(b, m[2, 0] - 1)], 0, 0)

    grid_spec = pltpu.PrefetchScalarGridSpec(
        num_scalar_prefetch=1,
        grid=(nb,),
        in_specs=[pl.BlockSpec((MOE_ROWS, D_MODEL), blk),
                  pl.BlockSpec((1, 1, D_MODEL, EXPERT_FF), wblk),
                  pl.BlockSpec((1, 1, D_MODEL, EXPERT_FF), wblk),
                  pl.BlockSpec((1, 1, EXPERT_FF, D_MODEL), wblk)],
        out_specs=pl.BlockSpec((MOE_ROWS, D_MODEL), blk),
    )
    return pl.pallas_call(
        _expert_kernel,
        grid_spec=grid_spec,
        out_shape=jax.ShapeDtypeStruct((nb * MOE_ROWS, D_MODEL), F32),
        compiler_params=_cparams("arbitrary"),
        name="moe_experts",
    )(meta, xs, wg, wu, wd)


def _combine_kernel(dest_ref, x_ref, r_ref, g_ref, ys_ref, o_ref, buf_ref, sem, *, tm, final):
    def copy(t, k):
        return pltpu.make_async_copy(ys_ref.at[pl.ds(dest_ref[0, k, t], 1)],
                                     buf_ref.at[k, pl.ds(t, 1)], sem)

    def start(t, c):
        copy(t, 0).start()
        copy(t, 1).start()
        return c

    def wait(t, c):
        copy(t, 0).wait()
        copy(t, 1).wait()
        return c

    lax.fori_loop(0, tm, start, 0)
    lax.fori_loop(0, tm, wait, 0)
    w0 = r_ref[:, 2:3]
    w1 = r_ref[:, 3:4]
    x = x_ref[...] + (buf_ref[0] * w0 + buf_ref[1] * w1)
    o_ref[...] = _rms(x, g_ref[...]) if final else x


def _combine(dest, x2, rcol, g, ys, tm, final):
    t = x2.shape[0]
    kern = functools.partial(_combine_kernel, tm=tm, final=final)
    return pl.pallas_call(
        kern,
        grid=(t // tm,),
        in_specs=[pl.BlockSpec((1, 2, tm), lambda i: (i, 0, 0), memory_space=pltpu.SMEM),
                  pl.BlockSpec((tm, D_MODEL), lambda i: (i, 0)),
                  pl.BlockSpec((tm, LANES), lambda i: (i, 0)),
                  pl.BlockSpec((1, D_MODEL), lambda i: (0, 0)),
                  pl.BlockSpec(memory_space=pl.ANY)],
        out_specs=pl.BlockSpec((tm, D_MODEL), lambda i: (i, 0)),
        out_shape=jax.ShapeDtypeStruct((t, D_MODEL), F32),
        scratch_shapes=[pltpu.VMEM((2, tm, D_MODEL), F32), pltpu.SemaphoreType.DMA],
        compiler_params=_cparams("arbitrary"),
        name="moe_combine",
    )(dest, x2, rcol, g, ys)


MOE_TILE = 256
CHUNK = 8
LOCAL_ROWS = 2 * MOE_TILE + 256
PACKED = D_MODEL // 2
XS_WIDTH = PACKED
U32 = jnp.uint32


def _pack_bf16_pairs(x, exact=False):
    if not exact:
        x = x.astype(BF16).astype(F32)
    half = x.shape[1] // 2
    lo = lax.bitcast_convert_type(x[:, :half], U32)
    hi = lax.bitcast_convert_type(x[:, half:], U32)
    return hi | (lo >> 16)


def _unpack_bf16_pairs(words):
    lo = lax.bitcast_convert_type(words << 16, F32)
    hi = lax.bitcast_convert_type(words & U32(0xFFFF0000), F32)
    return jnp.concatenate([lo, hi], axis=1).astype(BF16)
TAB_CHUNKS, TAB_LOCAL, TAB_GLOBAL, TAB_TOTAL = 0, 1, 2, 3


def _route_kernel(r_ref, cnt_ref, lrow_ref, lcol_ref, tab_ref, meta_ref, loff_ref, *, tm, nbp):
    i = pl.program_id(0)
    e_iota = _row_iota((N_EXPERTS, tm))
    oh0 = e_iota == r_ref[0:1, :].astype(I32)
    oh1 = e_iota == r_ref[1:2, :].astype(I32)
    oh = jnp.where(oh0 | oh1, 1.0, 0.0)
    tile_lane = _lane_iota((N_EXPERTS, LANES)) == i
    hi = lax.Precision.HIGHEST

    @pl.when(i == 0)
    def _():
        cnt = cnt_ref[...]
        n8 = jnp.floor((cnt + (CHUNK - 1)) * (1.0 / CHUNK)) * CHUNK
        er = _row_iota((N_EXPERTS, N_EXPERTS))
        ec = _lane_iota((N_EXPERTS, N_EXPERTS))
        below = jnp.where(er > ec, 1.0, 0.0)
        loff = jnp.dot(below, n8, preferred_element_type=F32, precision=hi)
        rows_e = jnp.sum(n8, axis=-1, keepdims=True) + jnp.zeros_like(n8)
        padded = jnp.floor((rows_e + (MOE_ROWS - 1)) * (1.0 / MOE_ROWS)) * MOE_ROWS
        e_start = jnp.dot(below, padded, preferred_element_type=F32, precision=hi)
        tr = _row_iota((LANES, LANES))
        tc = _lane_iota((LANES, LANES))
        earlier = jnp.where(tr < tc, 1.0, 0.0)
        goff = e_start + jnp.dot(n8, earlier, preferred_element_type=F32, precision=hi)
        loff_ref[...] = loff
        tab_ref[TAB_CHUNKS] = (n8 * (1.0 / CHUNK)).astype(I32)
        tab_ref[TAB_LOCAL] = loff.astype(I32)
        tab_ref[TAB_GLOBAL] = goff.astype(I32)
        tab_ref[TAB_TOTAL] = (jnp.sum(n8, axis=0, keepdims=True) * (1.0 / CHUNK)
                              + jnp.zeros_like(n8)).astype(I32)
        reps = nbp // LANES
        pend_b = jnp.concatenate([e_start + padded] * reps, axis=1)
        vend_b = jnp.concatenate([e_start + rows_e] * reps, axis=1)
        b0 = (_lane_iota((N_EXPERTS, nbp)) * MOE_ROWS).astype(F32)
        bexp = jnp.sum(jnp.where(pend_b <= b0, 1.0, 0.0), axis=0, keepdims=True)
        bexp = jnp.minimum(bexp, N_EXPERTS - 1.0)
        is_e = _row_iota((N_EXPERTS, nbp)).astype(F32) == bexp
        vend = jnp.sum(jnp.where(is_e, vend_b, 0.0), axis=0, keepdims=True)
        nvalid = jnp.clip(vend - b0[0:1], 0.0, float(MOE_ROWS))
        total = jnp.max(pend_b, axis=0, keepdims=True) * (1.0 / MOE_ROWS)
        row = _row_iota((8, nbp))
        meta = jnp.where(row == 0, bexp, jnp.where(row == 1, nvalid, jnp.where(row == 2, total, 0.0)))
        meta_ref[...] = meta.astype(I32)

    su = jnp.where(_row_iota((tm, tm)) < _lane_iota((tm, tm)), 1.0, 0.0).astype(BF16)
    before = jnp.dot(oh.astype(BF16), su, preferred_element_type=F32)
    base = jnp.sum(jnp.where(tile_lane, loff_ref[...], 0.0), axis=-1, keepdims=True) + before
    d0 = jnp.sum(jnp.where(oh0, base, 0.0), axis=0, keepdims=True)
    d1 = jnp.sum(jnp.where(oh1, base, 0.0), axis=0, keepdims=True)
    lrow_ref[0, 0:1, :] = d0.astype(I32)
    lrow_ref[0, 1:2, :] = d1.astype(I32)
    row = _row_iota((LANES, tm))
    lcol_ref[...] = jnp.where(row == 0, d0, jnp.where(row == 1, d1,
                              jnp.where(row == 2, r_ref[2:3, :], jnp.where(row == 3, r_ref[3:4, :], 0.0)))).T


def _route(rrow, cnt, tm, nbp):
    t = rrow.shape[1]
    nt = t // tm
    kern = functools.partial(_route_kernel, tm=tm, nbp=nbp)
    return pl.pallas_call(
        kern,
        grid=(nt,),
        in_specs=[pl.BlockSpec((8, tm), lambda i: (0, i)),
                  pl.BlockSpec((N_EXPERTS, LANES), lambda i: (0, 0))],
        out_specs=[pl.BlockSpec((1, 2, tm), lambda i: (i, 0, 0)),
                   pl.BlockSpec((tm, LANES), lambda i: (i, 0)),
                   pl.BlockSpec((4, N_EXPERTS, LANES), lambda i: (0, 0, 0)),
                   pl.BlockSpec((8, nbp), lambda i: (0, 0))],
        out_shape=[jax.ShapeDtypeStruct((nt, 2, tm), I32), jax.ShapeDtypeStruct((t, LANES), F32),
                   jax.ShapeDtypeStruct((4, N_EXPERTS, LANES), I32), jax.ShapeDtypeStruct((8, nbp), I32)],
        scratch_shapes=[pltpu.VMEM((N_EXPERTS, LANES), F32)],
        compiler_params=_cparams("arbitrary"),
        name="moe_route",
    )(rrow, cnt)


def _chunk_copies(tab_ref, i, local_ref, global_ref, sem, to_global, action):
    if action == "wait":
        lsl = local_ref.at[pl.ds(0, CHUNK)]
        gsl = global_ref.at[pl.ds(0, CHUNK)]
        cp = pltpu.make_async_copy(lsl, gsl, sem) if to_global else pltpu.make_async_copy(gsl, lsl, sem)

        def one(c, c1):
            cp.wait()
            return c1

        lax.fori_loop(0, tab_ref[TAB_TOTAL, 0, i], one, 0)
        return

    def run(e, c0):
        n = tab_ref[TAB_CHUNKS, e, i]
        lo = tab_ref[TAB_LOCAL, e, i]
        go = tab_ref[TAB_GLOBAL, e, i]

        def chunk(c, c1):
            lsl = local_ref.at[pl.ds(pl.multiple_of(lo + c * CHUNK, CHUNK), CHUNK)]
            gsl = global_ref.at[pl.ds(pl.multiple_of(go + c * CHUNK, CHUNK), CHUNK)]
            cp = pltpu.make_async_copy(lsl, gsl, sem) if to_global else pltpu.make_async_copy(gsl, lsl, sem)
            getattr(cp, action)()
            return c1

        return lax.fori_loop(0, n, chunk, c0)

    lax.fori_loop(0, N_EXPERTS, run, 0)


def _scatter_kernel(tab_ref, meta_ref, lrow_ref, h_ref, xs_ref, buf_ref, zero_ref, sem, zsem, *, tm, nb):
    i = pl.program_id(0)

    @pl.when(i == 0)
    def _():
        zero_ref[...] = jnp.zeros_like(zero_ref)
        n_used = meta_ref[2, 0]

        def zcopy(b):
            sub = lax.shift_right_logical(meta_ref[1, b], int(math.log2(FFN_SUB)))
            start = pl.multiple_of(b * MOE_ROWS + sub * FFN_SUB, FFN_SUB)
            return pltpu.make_async_copy(zero_ref, xs_ref.at[pl.ds(start, FFN_SUB)], zsem)

        def needs(b):
            return (b < n_used) & ((meta_ref[1, b] & (FFN_SUB - 1)) != 0)

        def start(b, c):
            @pl.when(needs(b))
            def _():
                zcopy(b).start()
            return c

        def wait(b, c):
            @pl.when(needs(b))
            def _():
                zcopy(b).wait()
            return c

        lax.fori_loop(0, nb, start, 0)
        lax.fori_loop(0, nb, wait, 0)

    rows = _row_iota((LOCAL_ROWS, tm))
    p0 = rows == lrow_ref[0, 0:1, :]
    p1 = rows == lrow_ref[0, 1:2, :]
    perm = jnp.where(p0 | p1, 1.0, 0.0).astype(BF16)
    sorted_rows = jnp.dot(perm, h_ref[...], preferred_element_type=F32)

    def fill(slot):
        buf = buf_ref.at[slot]
        buf[...] = _pack_bf16_pairs(sorted_rows, exact=True)
        _chunk_copies(tab_ref, i, buf, xs_ref, sem.at[slot], True, "start")

    def drain(tile, slot):
        _chunk_copies(tab_ref, tile, buf_ref.at[slot], xs_ref, sem.at[slot], True, "wait")

    even = (i & 1) == 0

    @pl.when(even)
    def _():
        fill(0)

    @pl.when(jnp.logical_not(even))
    def _():
        fill(1)

    @pl.when((i > 0) & even)
    def _():
        drain(i - 1, 1)

    @pl.when((i > 0) & jnp.logical_not(even))
    def _():
        drain(i - 1, 0)

    @pl.when((i == pl.num_programs(0) - 1) & even)
    def _():
        drain(i, 0)

    @pl.when((i == pl.num_programs(0) - 1) & jnp.logical_not(even))
    def _():
        drain(i, 1)


def _scatter(tab, meta, lrow, h2, tm, nb):
    t = h2.shape[0]
    kern = functools.partial(_scatter_kernel, tm=tm, nb=nb)
    grid_spec = pltpu.PrefetchScalarGridSpec(
        num_scalar_prefetch=2,
        grid=(t // tm,),
        in_specs=[pl.BlockSpec((1, 2, tm), lambda i, *_: (i, 0, 0)),
                  pl.BlockSpec((tm, D_MODEL), lambda i, *_: (i, 0))],
        out_specs=pl.BlockSpec(memory_space=pl.ANY),
        scratch_shapes=[pltpu.VMEM((2, LOCAL_ROWS, XS_WIDTH), U32), pltpu.VMEM((FFN_SUB, XS_WIDTH), U32),
                        pltpu.SemaphoreType.DMA((2,)), pltpu.SemaphoreType.DMA],
    )
    return pl.pallas_call(
        kern,
        grid_spec=grid_spec,
        out_shape=jax.ShapeDtypeStruct((nb * MOE_ROWS, XS_WIDTH), U32),
        compiler_params=_cparams("arbitrary"),
        name="moe_scatter",
    )(tab, meta, lrow, h2)


def _ffn_kernel(meta_ref, x_ref, wg_ref, wu_ref, wd_ref, o_ref, wgu_b, wd_b):
    b = pl.program_id(0)
    live = b < meta_ref[2, 0]
    prev = meta_ref[0, jnp.maximum(b - 1, 0)]

    @pl.when(live & ((b == 0) | (meta_ref[0, b] != prev)))
    def _():
        wgu_b[:, 0:EXPERT_FF] = wg_ref[0, 0].astype(BF16)
        wgu_b[:, EXPERT_FF:2 * EXPERT_FF] = wu_ref[0, 0].astype(BF16)
        wd_b[...] = wd_ref[0, 0].astype(BF16)

    nvalid = meta_ref[1, jnp.maximum(jnp.minimum(b, meta_ref[2, 0] - 1), 0)]
    for sub in range(MOE_ROWS // FFN_SUB):
        rows = slice(sub * FFN_SUB, (sub + 1) * FFN_SUB)
        used = live & (nvalid > sub * FFN_SUB)

        @pl.when(used)
        def _():
            x = _unpack_bf16_pairs(x_ref[rows, 0:PACKED])
            gu = jnp.dot(x, wgu_b[...], preferred_element_type=F32)
            act = (_silu(gu[:, 0:EXPERT_FF]) * gu[:, EXPERT_FF:2 * EXPERT_FF]).astype(BF16)
            y = jnp.dot(act, wd_b[...], preferred_element_type=F32)
            o_ref[rows, :] = _pack_bf16_pairs(y)

        @pl.when(live & jnp.logical_not(used))
        def _():
            o_ref[rows, :] = jnp.zeros((FFN_SUB, PACKED), U32)


def _ffn(meta, xs, wg, wu, wd, layer, nb):
    def blk(b, m):
        return (jnp.maximum(jnp.minimum(b, m[2, 0] - 1), 0), 0)

    def wblk(b, m):
        return (layer, m[0, jnp.maximum(jnp.minimum(b, m[2, 0] - 1), 0)], 0, 0)

    grid_spec = pltpu.PrefetchScalarGridSpec(
        num_scalar_prefetch=1,
        grid=(nb,),
        in_specs=[pl.BlockSpec((MOE_ROWS, XS_WIDTH), blk),
                  pl.BlockSpec((1, 1, D_MODEL, EXPERT_FF), wblk),
                  pl.BlockSpec((1, 1, D_MODEL, EXPERT_FF), wblk),
                  pl.BlockSpec((1, 1, EXPERT_FF, D_MODEL), wblk)],
        out_specs=pl.BlockSpec((MOE_ROWS, PACKED), blk),
        scratch_shapes=[pltpu.VMEM((D_MODEL, 2 * EXPERT_FF), BF16), pltpu.VMEM((EXPERT_FF, D_MODEL), BF16)],
    )
    return pl.pallas_call(
        _ffn_kernel,
        grid_spec=grid_spec,
        out_shape=jax.ShapeDtypeStruct((nb * MOE_ROWS, PACKED), U32),
        compiler_params=_cparams("arbitrary"),
        name="moe_experts",
    )(meta, xs, wg, wu, wd)


def _gather_kernel(tab_ref, lcol_ref, x_ref, g_ref, ys_ref, o_ref, buf_ref, sem, *, tm, final):
    i = pl.program_id(0)

    last = pl.num_programs(0) - 1

    def fetch(tile, slot, action):
        _chunk_copies(tab_ref, tile, buf_ref.at[slot], ys_ref, sem.at[slot], False, action)

    @pl.when(i == 0)
    def _():
        buf_ref[...] = jnp.zeros_like(buf_ref)
        fetch(0, 0, "start")

    even = (i & 1) == 0

    @pl.when((i < last) & even)
    def _():
        fetch(i + 1, 1, "start")

    @pl.when((i < last) & jnp.logical_not(even))
    def _():
        fetch(i + 1, 0, "start")

    col = _lane_iota((tm, LOCAL_ROWS)).astype(F32)
    pick0 = jnp.where(col == lcol_ref[:, 0:1], 1.0, 0.0).astype(BF16)
    pick1 = jnp.where(col == lcol_ref[:, 1:2], 1.0, 0.0).astype(BF16)

    def finish(slot):
        fetch(i, slot, "wait")
        y = _unpack_bf16_pairs(buf_ref[slot])
        both = jnp.dot(jnp.concatenate([pick0, pick1], axis=0), y, preferred_element_type=F32)
        x = x_ref[...] + lcol_ref[:, 2:3] * both[0:tm] + lcol_ref[:, 3:4] * both[tm:2 * tm]
        o_ref[...] = _rms(x, g_ref[...]) if final else x

    @pl.when(even)
    def _():
        finish(0)

    @pl.when(jnp.logical_not(even))
    def _():
        finish(1)


def _gather(tab, lcol, x2, g, ys, tm, final):
    t = x2.shape[0]
    kern = functools.partial(_gather_kernel, tm=tm, final=final)
    grid_spec = pltpu.PrefetchScalarGridSpec(
        num_scalar_prefetch=1,
        grid=(t // tm,),
        in_specs=[pl.BlockSpec((tm, LANES), lambda i, *_: (i, 0)),
                  pl.BlockSpec((tm, D_MODEL), lambda i, *_: (i, 0)),
                  pl.BlockSpec((1, D_MODEL), lambda i, *_: (0, 0)),
                  pl.BlockSpec(memory_space=pl.ANY)],
        out_specs=pl.BlockSpec((tm, D_MODEL), lambda i, *_: (i, 0)),
        scratch_shapes=[pltpu.VMEM((2, LOCAL_ROWS, PACKED), U32), pltpu.SemaphoreType.DMA((2,))],
    )
    return pl.pallas_call(
        kern,
        grid_spec=grid_spec,
        out_shape=jax.ShapeDtypeStruct((t, D_MODEL), F32),
        compiler_params=_cparams("arbitrary"),
        name="moe_combine",
    )(tab, lcol, x2, g, ys)


def _pad_rows(a, rows=8):
    return jnp.zeros((rows, a.shape[-1]), F32).at[:a.shape[0]].set(a.astype(F32))


def _arrange_w_in(w):
    gw = GROUP_WIDTH
    a0 = 0
    b0 = 3 * gw
    c0 = b0 + 3 * gw + N_HEADS
    d0 = c0 + gw + (gw + 4 * SSM_STATE) + N_HEADS
    half = MLA_ROPE // 2
    f_logit = w[:, b0 + 3 * gw:b0 + 3 * gw + N_HEADS]
    dt_raw = w[:, c0 + 3 * gw:c0 + 3 * gw + N_HEADS]
    kr0 = d0 + MLA_Q_LORA + MLA_KV_LORA
    kr = w[:, kr0:kr0 + MLA_ROPE]
    kr_sw = jnp.concatenate([-kr[:, half:], kr[:, :half]], axis=1)
    zeros = lambda n: jnp.zeros((w.shape[0], n), w.dtype)
    misc = jnp.concatenate([f_logit, dt_raw, zeros(MISC_ROPE - 2 * N_HEADS), kr,
                            zeros(LANES - MISC_ROPE - MLA_ROPE)], axis=1)
    misc2 = jnp.concatenate([zeros(MISC_ROPE), kr_sw, zeros(LANES - MISC_ROPE - MLA_ROPE)], axis=1)
    fox_q = w[:, b0:b0 + gw] * (HEAD_DIM ** -0.5 * LOG2E)
    out = jnp.concatenate([w[:, a0:a0 + 3 * gw], fox_q, w[:, b0 + gw:b0 + 2 * gw], w[:, c0:c0 + 3 * gw],
                           w[:, d0:d0 + MLA_Q_LORA + MLA_KV_LORA], misc, misc2], axis=1)
    fox_vt = w[:, b0 + 2 * gw:b0 + 3 * gw].T
    return out.astype(BF16), fox_vt.astype(BF16)


def _arrange_mla(w_uq, w_ukv):
    half = MLA_ROPE // 2
    qd = MLA_NOPE + MLA_ROPE
    wq, wqs, wk, wv = [], [], [], []
    zq = jnp.zeros((MLA_Q_LORA, LANES - qd), w_uq.dtype)
    zk = jnp.zeros((MLA_KV_LORA, LANES - MLA_NOPE), w_ukv.dtype)
    for h in range(N_HEADS):
        q = w_uq[:, h * qd:(h + 1) * qd]
        nope, rope = q[:, :MLA_NOPE], q[:, MLA_NOPE:]
        wq.append(jnp.concatenate([nope, rope, zq], axis=1))
        wqs.append(jnp.concatenate([jnp.zeros_like(nope), -rope[:, half:], rope[:, :half], zq], axis=1))
        kv = w_ukv[:, h * 2 * MLA_NOPE:(h + 1) * 2 * MLA_NOPE]
        wk.append(jnp.concatenate([kv[:, :MLA_NOPE], zk], axis=1))
        wv.append(kv[:, MLA_NOPE:])
    cat = lambda xs: jnp.concatenate(xs, axis=1).astype(BF16)
    return cat(wq), cat(wqs), cat(wk), cat(wv).T


def kernel(x, positions, norm_mix, w_in, conv_a, fox_forget_bias, ssm_conv_w, ssm_conv_b, ssm_dt_bias,
           ssm_a_log, ssm_d, ssm_norm, mla_q_norm, mla_kv_norm, mla_w_uq, mla_w_ukv, w_out, norm_ffn,
           router_group_w, router_group_b, router_expert_w, router_expert_b, expert_w_gate, expert_w_up,
           expert_w_down, norm_final):
    batch, seq, d = x.shape
    t = batch * seq
    depth = w_in.shape[0]
    tm = min(512, t)
    tq = min(ATTN_TQ, seq)
    tmd = min(MOE_TILE, t)
    max_rows = 2 * t + (CHUNK - 1) * N_EXPERTS * (t // tmd) + N_EXPERTS * (MOE_ROWS - 1)
    nb = -(-max_rows // MOE_ROWS)
    nbp = -(-nb // LANES) * LANES

    xf = x.reshape(t, d)
    pos_col = positions.astype(F32).reshape(t, 1)
    cos, sin = _rope_tables(pos_col, tm)

    for l in range(depth):
        w_in_p, w_vt = _arrange_w_in(w_in[l])
        pa, pb, pc, pd, misc, misc2, fox_vt = _inproj(xf, norm_mix[l][None, :], w_in_p, w_vt, tm)

        sp = jnp.zeros((8, LANES), F32)
        sp = sp.at[0, MISC_F:MISC_F + N_HEADS].set(fox_forget_bias[l])
        sp = sp.at[0, MISC_DT:MISC_DT + N_HEADS].set(ssm_dt_bias[l])
        sp = sp.at[1, MISC_DT:MISC_DT + N_HEADS].set(ssm_a_log[l])
        col, rows, fox_q, fox_k, tref = _scalar_prep(misc, sp, pb, batch, seq, tq)

        ya = _conv_mixer(pa, _pad_rows(conv_a[l]), batch, seq)
        yb = _attention(fox_q, fox_k, fox_vt, tref, batch, seq, tq, "fox_attention")
        conv_wb = _pad_rows(jnp.concatenate([ssm_conv_w[l], ssm_conv_b[l][None, :]], axis=0))
        ssd_par = _pad_rows(jnp.stack([jnp.repeat(ssm_d[l], HEAD_DIM), ssm_norm[l]]))
        yc = _ssd_mixer(pc, col, rows, conv_wb, ssd_par, batch, seq)
        wq, wqs, wk, wv = _arrange_mla(mla_w_uq[l], mla_w_ukv[l])
        q, k, v = _mla_prep(pd, misc, misc2, cos, sin, mla_q_norm[l][None, :], mla_kv_norm[l][None, :],
                            wq, wqs, wk, wv, tm)
        yd = _attention(q, k, v, None, batch, seq, tq, "mla_attention")

        pad = jnp.zeros((d, LANES - N_EXPERTS - N_EXPERT_GROUPS), F32)
        wr = jnp.concatenate([router_expert_w[l], router_group_w[l], pad], axis=1)
        wr_hi = wr.astype(BF16)
        wr = jnp.concatenate([wr_hi, (wr - wr_hi.astype(F32)).astype(BF16)], axis=1)
        br = jnp.concatenate([router_expert_b[l], router_group_b[l], pad[0]])[None, :]
        x2, h2, rrow, cnt = _outproj(xf, ya, yb, yc, yd, w_out[l].astype(BF16), norm_ffn[l][None, :], wr, br,
                                     tm, tmd)

        lrow, lcol, tab, meta = _route(rrow, cnt, tmd, nbp)
        xs = _scatter(tab, meta, lrow, h2, tmd, nb)
        ys = _ffn(meta, xs, expert_w_gate, expert_w_up, expert_w_down, l, nb)
        final = l == depth - 1
        xf = _gather(tab, lcol, x2, norm_final[None, :], ys, tmd, final)

    return xf.reshape(batch, seq, d)


def _retile(dest, tm, tmd):
    if tm == tmd:
        return dest
    nt = dest.shape[0]
    return dest.reshape(nt, 2, tm // tmd, tmd).transpose(0, 2, 1, 3).reshape(nt * (tm // tmd), 2, tmd)
```

```python
import functools
import math

import jax
import jax.numpy as jnp
import numpy as np
from jax import lax
from jax.experimental import pallas as pl
from jax.experimental.pallas import tpu as pltpu

F32 = jnp.float32
BF16 = jnp.bfloat16
I32 = jnp.int32

LANES = 128
VMEM_LIMIT_BYTES = 56 * 1024 * 1024

D_MODEL = 1024
RMS_EPS = 1e-6
LOG2E = math.log2(math.e)
GROUP_WIDTH = 256
HEAD_DIM = 64
N_HEADS = 4

CONV_A_WIDTH = 3
SSM_CONV = 4
SSM_STATE = 64
SSM_CHUNK = 256

MLA_NOPE = 64
MLA_ROPE = 32
MLA_Q_LORA = 256
MLA_KV_LORA = 128
ROPE_BASE = 10000.0
MLA_CHUNK = 64
ATTN_TQ = 512
ATTN_RB = 128

N_EXPERT_GROUPS = 4
EXPERTS_PER_GROUP = 8
N_EXPERTS = 32
EXPERT_FF = 256
MOE_ROWS = 512
FFN_SUB = 256

SEG_A = (0, 768)
SEG_B = (768, 1280)
SEG_C = (1280, 2048)
SEG_D = (2048, 2432)
SEG_M = (2432, 2560)
SEG_M2 = (2560, 2688)
IN_COLS_PADDED = 2688
HEAD_PAD = N_HEADS * LANES
AUG_LANE = HEAD_DIM
MISC_F = 0
MISC_DT = 4
MISC_ROPE = 64
COL_CUMF = 0
COL_DT = 4
COL_ACUM = 8
N_SCALAR_ROWS = 16


def _cparams(*sem):
    return pltpu.CompilerParams(dimension_semantics=sem, vmem_limit_bytes=VMEM_LIMIT_BYTES)


def _lane_iota(shape):
    return lax.broadcasted_iota(I32, shape, len(shape) - 1)


def _row_iota(shape):
    return lax.broadcasted_iota(I32, shape, 0)


def _rms(x, g):
    ms = jnp.mean(x * x, axis=-1, keepdims=True)
    return x * lax.rsqrt(ms + RMS_EPS) * g


def _silu(x):
    return x / (1.0 + jnp.exp(-x))


def _softplus(x):
    return jnp.maximum(x, 0.0) + jnp.log(1.0 + jnp.exp(-jnp.abs(x)))


def _shift_rows(x, k):
    rolled = pltpu.roll(x, k, 0)
    return jnp.where(_row_iota(x.shape) >= k, rolled, 0.0)


def _rope_kernel(pos_ref, freq_ref, cos_ref, sin_ref):
    ang = pos_ref[...] * freq_ref[...]
    lane = _lane_iota(ang.shape)
    rope = (lane >= MISC_ROPE) & (lane < MISC_ROPE + MLA_ROPE)
    cos_ref[...] = jnp.where(rope, jnp.cos(ang), jnp.where(lane < MISC_ROPE, 1.0, 0.0))
    sin_ref[...] = jnp.where(rope, jnp.sin(ang), 0.0)


def _rope_tables(pos_col, tm):
    t = pos_col.shape[0]
    half = MLA_ROPE // 2
    inv = ROPE_BASE ** (-np.arange(0, MLA_ROPE, 2, dtype=np.float32) / MLA_ROPE)
    freq = np.zeros((1, LANES), np.float32)
    freq[0, MISC_ROPE:MISC_ROPE + half] = inv
    freq[0, MISC_ROPE + half:MISC_ROPE + MLA_ROPE] = inv
    return pl.pallas_call(
        _rope_kernel,
        grid=(t // tm,),
        in_specs=[pl.BlockSpec((tm, 1), lambda i: (i, 0)),
                  pl.BlockSpec((1, LANES), lambda i: (0, 0))],
        out_specs=[pl.BlockSpec((tm, LANES), lambda i: (i, 0))] * 2,
        out_shape=[jax.ShapeDtypeStruct((t, LANES), F32)] * 2,
        compiler_params=_cparams("parallel"),
        name="rope_tables",
    )(pos_col, jnp.asarray(freq))


def _inproj_kernel(x_ref, g_ref, w_ref, wvt_ref, oa, ob, oc, od, om, om2, ovt):
    h = _rms(x_ref[...], g_ref[...]).astype(BF16)
    for o, (lo, hi) in ((oa, SEG_A), (ob, SEG_B), (oc, SEG_C), (od, SEG_D), (om, SEG_M), (om2, SEG_M2)):
        o[...] = jnp.dot(h, w_ref[:, lo:hi], preferred_element_type=F32).astype(o.dtype)
    ovt[...] = lax.dot_general(wvt_ref[...], h, (((1,), (1,)), ((), ())),
                               preferred_element_type=F32).astype(ovt.dtype)


def _inproj(x, g, w, wvt, tm):
    t = x.shape[0]
    widths = [(s[1] - s[0]) for s in (SEG_A, SEG_B, SEG_C, SEG_D, SEG_M, SEG_M2)]
    dtypes = [BF16, BF16, BF16, BF16, F32, F32]
    return pl.pallas_call(
        _inproj_kernel,
        grid=(t // tm,),
        in_specs=[pl.BlockSpec((tm, D_MODEL), lambda i: (i, 0)),
                  pl.BlockSpec((1, D_MODEL), lambda i: (0, 0)),
                  pl.BlockSpec((D_MODEL, IN_COLS_PADDED), lambda i: (0, 0)),
                  pl.BlockSpec((GROUP_WIDTH, D_MODEL), lambda i: (0, 0))],
        out_specs=[pl.BlockSpec((tm, wd), lambda i: (i, 0)) for wd in widths]
        + [pl.BlockSpec((GROUP_WIDTH, tm), lambda i: (0, i))],
        out_shape=[jax.ShapeDtypeStruct((t, wd), dt) for wd, dt in zip(widths, dtypes)]
        + [jax.ShapeDtypeStruct((GROUP_WIDTH, t), BF16)],
        compiler_params=_cparams("parallel"),
        name="inproj",
    )(x, g, w, wvt)


def _scalar_prep_kernel(m_ref, p_ref, qk_ref, sel_ref, place_ref, const_ref,
                        col_ref, row_ref, qa_ref, ka_ref, tref_ref, *, tq):
    s = m_ref.shape[0]
    tref_ref[...] = jnp.zeros_like(tref_ref)
    tile_ref = jnp.zeros((1, LANES), F32)
    m = m_ref[...]
    bias = p_ref[0:1, :]
    a_log = p_ref[1:2, :]
    lane = _lane_iota(m.shape)
    z = m + bias
    logf = jnp.minimum(z, 0.0) - jnp.log(1.0 + jnp.exp(-jnp.abs(z)))
    dt = _softplus(z)
    a = dt * (-jnp.exp(a_log))
    is_f = lane < MISC_DT
    is_dt = (lane >= MISC_DT) & (lane < MISC_DT + N_HEADS)
    v = jnp.where(is_f, logf, jnp.where(is_dt, a, 0.0))
    r = _row_iota((SSM_CHUNK, SSM_CHUNK))
    c = _lane_iota((SSM_CHUNK, SSM_CHUNK))
    tril = jnp.where(r >= c, 1.0, 0.0).astype(F32)
    carry = jnp.zeros((1, LANES), F32)
    lane_1 = _lane_iota((1, LANES))
    lane_b = _lane_iota((SSM_CHUNK, LANES))
    for ci in range(s // SSM_CHUNK):
        blk = v[ci * SSM_CHUNK:(ci + 1) * SSM_CHUNK]
        cs = jnp.dot(tril, blk, preferred_element_type=F32, precision=lax.Precision.HIGHEST)
        cs = cs + jnp.where(lane_1 < MISC_DT, carry, 0.0)
        carry = cs[SSM_CHUNK - 1:SSM_CHUNK]
        acum = pltpu.roll(cs, COL_ACUM - MISC_DT, 1)
        out = jnp.where(lane_b < MISC_DT, cs * LOG2E,
                        jnp.where(lane_b < COL_ACUM, dt[ci * SSM_CHUNK:(ci + 1) * SSM_CHUNK],
                                  jnp.where(lane_b < COL_ACUM + N_HEADS, acum, 0.0)))
        rows = slice(ci * SSM_CHUNK, (ci + 1) * SSM_CHUNK)
        col_ref[rows, :] = out
        row_ref[0, :, rows] = out.T[:N_SCALAR_ROWS]
        if (ci * SSM_CHUNK) % tq == 0:
            tile_ref = out[0:1, :]
            ti = (ci * SSM_CHUNK) // tq
            tref_ref[0, ti:ti + 1, :] = tile_ref
        c = out - tile_ref
        c_hi = c.astype(BF16)
        r1 = c - c_hi.astype(F32)
        c_mid = r1.astype(BF16)
        c_lo = (r1 - c_mid.astype(F32)).astype(BF16)
        for o_ref, base, qk_lo in ((qa_ref, 0, 0), (ka_ref, 3, GROUP_WIDTH)):
            aug = jnp.dot(qk_ref[rows, qk_lo:qk_lo + GROUP_WIDTH], sel_ref[...], preferred_element_type=F32)
            for term, cc in enumerate((c_hi, c_mid, c_lo)):
                aug = aug + jnp.dot(cc, place_ref[base + term], preferred_element_type=F32)
            o_ref[rows, :] = (aug + const_ref[base // 3:base // 3 + 1, :]).astype(o_ref.dtype)


def _fox_placement():
    sel = np.zeros((GROUP_WIDTH, HEAD_PAD), np.float32)
    place = np.zeros((6, LANES, HEAD_PAD), np.float32)
    const = np.zeros((8, HEAD_PAD), np.float32)
    for h in range(N_HEADS):
        for d in range(HEAD_DIM):
            sel[h * HEAD_DIM + d, h * LANES + d] = 1.0
        a0 = h * LANES + AUG_LANE
        for term in range(3):
            place[term, COL_CUMF + h, a0 + term] = 1.0
            place[3 + term, COL_CUMF + h, a0 + 3 + term] = -1.0
            const[0, a0 + 3 + term] = 1.0
            const[1, a0 + term] = 1.0
    return jnp.asarray(sel, BF16), jnp.asarray(place, BF16), jnp.asarray(const, F32)


def _scalar_prep(misc, params, qk, batch, seq, tq):
    sel, place, const = _fox_placement()
    full = lambda a: pl.BlockSpec(a.shape, lambda b: (0,) * a.ndim)
    return pl.pallas_call(
        functools.partial(_scalar_prep_kernel, tq=tq),
        grid=(batch,),
        in_specs=[pl.BlockSpec((seq, LANES), lambda b: (b, 0)),
                  pl.BlockSpec((8, LANES), lambda b: (0, 0)),
                  pl.BlockSpec((seq, 2 * GROUP_WIDTH), lambda b: (b, 0)),
                  full(sel), full(place), full(const)],
        out_specs=[pl.BlockSpec((seq, LANES), lambda b: (b, 0)),
                   pl.BlockSpec((1, N_SCALAR_ROWS, seq), lambda b: (b, 0, 0)),
                   pl.BlockSpec((seq, HEAD_PAD), lambda b: (b, 0)),
                   pl.BlockSpec((seq, HEAD_PAD), lambda b: (b, 0)),
                   pl.BlockSpec((1, 8, LANES), lambda b: (b, 0, 0))],
        out_shape=[jax.ShapeDtypeStruct((batch * seq, LANES), F32),
                   jax.ShapeDtypeStruct((batch, N_SCALAR_ROWS, seq), F32),
                   jax.ShapeDtypeStruct((batch * seq, HEAD_PAD), BF16),
                   jax.ShapeDtypeStruct((batch * seq, HEAD_PAD), BF16),
                   jax.ShapeDtypeStruct((batch, 8, LANES), F32)],
        compiler_params=_cparams("parallel"),
        name="scalar_prep",
    )(misc, params, qk, sel, place, const)


def _conv_mixer_kernel(p_ref, w_ref, o_ref):
    gw = GROUP_WIDTH
    b_gate = p_ref[:, 0:gw].astype(F32)
    cv = p_ref[:, gw:2 * gw].astype(F32) * p_ref[:, 2 * gw:3 * gw].astype(F32)
    acc = cv * w_ref[CONV_A_WIDTH - 1:CONV_A_WIDTH, :]
    for k in range(1, CONV_A_WIDTH):
        acc = acc + _shift_rows(cv, k) * w_ref[CONV_A_WIDTH - 1 - k:CONV_A_WIDTH - k, :]
    o_ref[...] = (b_gate * acc).astype(o_ref.dtype)


def _conv_mixer(pa, w, batch, seq):
    return pl.pallas_call(
        _conv_mixer_kernel,
        grid=(batch,),
        in_specs=[pl.BlockSpec((seq, 3 * GROUP_WIDTH), lambda b: (b, 0)),
                  pl.BlockSpec((8, GROUP_WIDTH), lambda b: (0, 0))],
        out_specs=pl.BlockSpec((seq, GROUP_WIDTH), lambda b: (b, 0)),
        out_shape=jax.ShapeDtypeStruct((batch * seq, GROUP_WIDTH), BF16),
        compiler_params=_cparams("parallel"),
        name="conv_mixer",
    )(pa, w)


def _pair_lanes(col, base, shape):
    lane = _lane_iota(shape)
    return jnp.where(lane < HEAD_DIM, col[:, base:base + 1], col[:, base + 1:base + 2])


def _ssd_kernel(p_ref, col_ref, row_ref, cw_ref, par_ref, o_ref, u_ref):
    s = p_ref.shape[0]
    q = SSM_CHUNK
    gw = GROUP_WIDTH
    xbc = p_ref[:, gw:3 * gw].astype(F32)
    acc = xbc * cw_ref[SSM_CONV - 1:SSM_CONV, :]
    for k in range(1, SSM_CONV):
        acc = acc + _shift_rows(xbc, k) * cw_ref[SSM_CONV - 1 - k:SSM_CONV - k, :]
    u_ref[...] = _silu(acc + cw_ref[SSM_CONV:SSM_CONV + 1, :])

    d_skip = par_ref[0:1, :]
    norm_g = par_ref[1:2, :]
    lane_q = _lane_iota((q, LANES))
    low = lane_q < HEAD_DIM
    tri = _row_iota((q, q)) >= _lane_iota((q, q))

    def chunk(ci, states):
        rows = pl.ds(ci * q, q)
        u = u_ref[rows, :]
        col = col_ref[rows, :]
        bm = u[:, gw:gw + LANES]
        cm = u[:, gw + LANES:gw + 2 * LANES]
        z = p_ref[rows, 0:gw].astype(F32)
        new_states = []
        ys = []
        for g in range(2):
            sel = low if g == 0 else jnp.logical_not(low)
            cg = jnp.where(sel, cm, 0.0).astype(BF16)
            bg = jnp.where(sel, bm, 0.0)
            gmat = lax.dot_general(cg, bm.astype(BF16), (((1,), (1,)), ((), ())),
                                   preferred_element_type=F32)
            xs = u[:, g * LANES:(g + 1) * LANES]
            dt2 = _pair_lanes(col, COL_DT + 2 * g, (q, LANES))
            ac2 = _pair_lanes(col, COL_ACUM + 2 * g, (q, LANES))
            xdt = xs * dt2
            xdt_b = xdt.astype(BF16)
            st = states[g]
            y_off = jnp.dot(cg, st.astype(BF16), preferred_element_type=F32) * jnp.exp(ac2)
            halves = []
            for hh in range(2):
                h = 2 * g + hh
                ac_col = col[:, COL_ACUM + h:COL_ACUM + h + 1]
                ac_row = row_ref[0, COL_ACUM + h:COL_ACUM + h + 1, rows]
                decay = jnp.exp(jnp.where(tri, ac_col - ac_row, -1e30))
                mm = (gmat * decay).astype(BF16)
                halves.append(jnp.dot(mm, xdt_b, preferred_element_type=F32))
            y = jnp.where(low, halves[0], halves[1]) + y_off + d_skip[:, g * LANES:(g + 1) * LANES] * xs
            ys.append(y)
            ac_last = ac2[q - 1:q, :]
            w_end = jnp.exp(ac_last - ac2)
            xw = (xdt * w_end).astype(BF16)
            upd = jnp.dot(bg.T.astype(BF16), xw, preferred_element_type=F32)
            new_states.append(st * jnp.exp(ac_last) + upd)
        yfull = jnp.concatenate(ys, axis=1) * _silu(z)
        o_ref[rows, :] = _rms(yfull, norm_g).astype(o_ref.dtype)
        return tuple(new_states)

    init = (jnp.zeros((LANES, LANES), F32), jnp.zeros((LANES, LANES), F32))
    states = init
    for ci in range(s // q):
        states = chunk(ci, states)


def _ssd_mixer(pc, col, rows, conv_wb, par, batch, seq):
    gw = GROUP_WIDTH
    return pl.pallas_call(
        _ssd_kernel,
        grid=(batch,),
        in_specs=[pl.BlockSpec((seq, 3 * gw), lambda b: (b, 0)),
                  pl.BlockSpec((seq, LANES), lambda b: (b, 0)),
                  pl.BlockSpec((1, N_SCALAR_ROWS, seq), lambda b: (b, 0, 0)),
                  pl.BlockSpec((8, 2 * gw), lambda b: (0, 0)),
                  pl.BlockSpec((8, gw), lambda b: (0, 0))],
        out_specs=pl.BlockSpec((seq, gw), lambda b: (b, 0)),
        out_shape=jax.ShapeDtypeStruct((batch * seq, gw), BF16),
        scratch_shapes=[pltpu.VMEM((seq, 2 * gw), F32)],
        compiler_params=_cparams("parallel"),
        name="ssd_mixer",
    )(pc, col, rows, conv_wb, par)


def _mla_prep_kernel(pd_ref, m_ref, m2_ref, cos_ref, sin_ref, nq_ref, nkv_ref,
                     wq_ref, wqs_ref, wk_ref, wvt_ref, q_ref, k_ref, vt_ref):
    cq = _rms(pd_ref[:, 0:MLA_Q_LORA].astype(F32), nq_ref[...]).astype(BF16)
    ckv = _rms(pd_ref[:, MLA_Q_LORA:MLA_Q_LORA + MLA_KV_LORA].astype(F32), nkv_ref[...]).astype(BF16)
    cos = cos_ref[...]
    sin = sin_ref[...]
    cos4 = jnp.concatenate([cos] * N_HEADS, axis=1)
    sin4 = jnp.concatenate([sin] * N_HEADS, axis=1)
    scale = (MLA_NOPE + MLA_ROPE) ** -0.5 * LOG2E
    q = jnp.dot(cq, wq_ref[...], preferred_element_type=F32)
    qs = jnp.dot(cq, wqs_ref[...], preferred_element_type=F32)
    q_ref[...] = ((q * cos4 + qs * sin4) * scale).astype(q_ref.dtype)
    lane = _lane_iota(cos.shape)
    rope = (lane >= MISC_ROPE) & (lane < MISC_ROPE + MLA_ROPE)
    kr = jnp.where(rope, m_ref[...] * cos + m2_ref[...] * sin, 0.0)
    k = jnp.dot(ckv, wk_ref[...], preferred_element_type=F32)
    k_ref[...] = (k + jnp.concatenate([kr] * N_HEADS, axis=1)).astype(k_ref.dtype)
    vt_ref[...] = lax.dot_general(wvt_ref[...], ckv, (((1,), (1,)), ((), ())),
                                  preferred_element_type=F32).astype(vt_ref.dtype)


def _mla_prep(pd, misc, misc2, cos, sin, nq, nkv, wq, wqs, wk, wv, tm):
    t = pd.shape[0]
    hp = N_HEADS * LANES
    full = lambda a: pl.BlockSpec(a.shape, lambda i: (0, 0))
    tile = lambda w: pl.BlockSpec((tm, w), lambda i: (i, 0))
    return pl.pallas_call(
        _mla_prep_kernel,
        grid=(t // tm,),
        in_specs=[tile(MLA_Q_LORA + MLA_KV_LORA), tile(LANES), tile(LANES), tile(LANES), tile(LANES),
                  full(nq), full(nkv), full(wq), full(wqs), full(wk), full(wv)],
        out_specs=[tile(hp), tile(hp), pl.BlockSpec((GROUP_WIDTH, tm), lambda i: (0, i))],
        out_shape=[jax.ShapeDtypeStruct((t, hp), BF16), jax.ShapeDtypeStruct((t, hp), BF16),
                   jax.ShapeDtypeStruct((GROUP_WIDTH, t), BF16)],
        compiler_params=_cparams("parallel"),
        name="mla_prep",
    )(pd, misc, misc2, cos, sin, nq, nkv, wq, wqs, wk, wv)


def _attn_kernel(*refs, fox, tq):
    if fox:
        tref_ref, q_ref, k_ref, vt_ref, o_ref = refs
    else:
        q_ref, k_ref, vt_ref, o_ref = refs
        tref_ref = None
    b = pl.program_id(0)
    i = pl.program_id(1)
    key = _row_iota((tq, tq))
    qry = _lane_iota((tq, tq))
    if fox:
        allowed = key <= qry
    else:
        shift = int(math.log2(MLA_CHUNK))
        allowed = (key >> shift) <= (qry >> shift)
    qs = [q_ref[:, h * LANES:(h + 1) * LANES] for h in range(N_HEADS)]

    def step(j, masked, carry):
        rk = pl.ds(pl.multiple_of(j * tq, tq), tq)
        scores = [lax.dot_general(k_ref[rk, h * LANES:(h + 1) * LANES], qs[h], (((1,), (1,)), ((), ())),
                                  preferred_element_type=F32) for h in range(N_HEADS)]
        probs = []
        for h in range(N_HEADS):
            m, l, _ = carry[h]
            s = scores[h]
            if masked:
                s = jnp.where(allowed, s, -1e30)
            delta = (tref_ref[b, i, h] - tref_ref[b, j, h]) if fox else 0.0
            m_new = jnp.maximum(m, jnp.max(s, axis=0, keepdims=True) + delta)
            alpha = jnp.exp2(m - m_new)
            p = jnp.exp2(s - (m_new - delta))
            l_new = alpha * l + jnp.sum(p, axis=0, keepdims=True)
            probs.append((m_new, l_new, alpha, p.astype(BF16)))
        new = []
        for h in range(N_HEADS):
            pair = h // 2
            m_new, l_new, alpha, p = probs[h]
            pv = jnp.dot(vt_ref[pair * LANES:(pair + 1) * LANES, rk], p, preferred_element_type=F32)
            new.append((m_new, l_new, alpha * carry[h][2] + pv))
        return tuple(new)

    init = tuple((jnp.full((1, tq), -1e30, F32), jnp.zeros((1, tq), F32), jnp.zeros((LANES, tq), F32))
                 for _ in range(N_HEADS))
    carry = lax.fori_loop(0, i, lambda j, c: step(j, False, c), init)
    carry = step(i, True, carry)
    outs = [acc / l for (_, l, acc) in carry]
    top = _row_iota((LANES, tq)) < HEAD_DIM
    o_t = jnp.concatenate([jnp.where(top, outs[0], outs[1]), jnp.where(top, outs[2], outs[3])], axis=0)
    o_ref[...] = o_t.T.astype(o_ref.dtype)


def _attention(q, k, vt, tref, batch, seq, tq, name):
    nq = seq // tq
    fox = tref is not None
    kern = functools.partial(_attn_kernel, fox=fox, tq=tq)
    grid_spec = pltpu.PrefetchScalarGridSpec(
        num_scalar_prefetch=1 if fox else 0,
        grid=(batch, nq),
        in_specs=[pl.BlockSpec((tq, HEAD_PAD), lambda b, i, *_: (b * nq + i, 0)),
                  pl.BlockSpec((seq, HEAD_PAD), lambda b, i, *_: (b, 0)),
                  pl.BlockSpec((GROUP_WIDTH, seq), lambda b, i, *_: (0, b))],
        out_specs=pl.BlockSpec((tq, GROUP_WIDTH), lambda b, i, *_: (b * nq + i, 0)),
    )
    args = ((tref,) if fox else ()) + (q, k, vt)
    return pl.pallas_call(
        kern,
        grid_spec=grid_spec,
        out_shape=jax.ShapeDtypeStruct((batch * seq, GROUP_WIDTH), BF16),
        compiler_params=_cparams("parallel", "arbitrary"),
        name=name,
    )(*args)


def _outproj_kernel(x_ref, ya, yb, yc, yd, w_ref, g_ref, wr_ref, br_ref,
                    x2_ref, h2_ref, rrow_ref, cnt_ref, *, tm, moe_tile):
    y = jnp.concatenate([ya[...], yb[...], yc[...], yd[...]], axis=1)
    x2 = x_ref[...] + jnp.dot(y, w_ref[...], preferred_element_type=F32)
    x2_ref[...] = x2
    h2 = _rms(x2, g_ref[...])
    h2_ref[...] = h2.astype(h2_ref.dtype)
    h_hi = h2.astype(BF16)
    h_lo = (h2 - h_hi.astype(F32)).astype(BF16)
    part = jnp.dot(h_hi, wr_ref[...], preferred_element_type=F32)
    logits = (part[:, 0:LANES] + part[:, LANES:2 * LANES]
              + jnp.dot(h_lo, wr_ref[:, 0:LANES], preferred_element_type=F32) + br_ref[...])
    lt = logits.T
    row = _row_iota(lt.shape)
    neg = -1e30
    big = 1 << 20
    gmask = (row >= N_EXPERTS) & (row < N_EXPERTS + N_EXPERT_GROUPS)
    gl = jnp.where(gmask, lt, neg)
    gmax = jnp.max(gl, axis=0, keepdims=True)
    gsum = jnp.sum(jnp.where(gmask, jnp.exp(gl - gmax), 0.0), axis=0, keepdims=True)
    g_w = 1.0 / gsum
    g_idx = jnp.min(jnp.where(gmask & (gl == gmax), row, big), axis=0, keepdims=True) - N_EXPERTS
    emask = (row < N_EXPERTS) & ((row >> int(math.log2(EXPERTS_PER_GROUP))) == g_idx)
    el = jnp.where(emask, lt, neg)
    e1v = jnp.max(el, axis=0, keepdims=True)
    esum = jnp.sum(jnp.where(emask, jnp.exp(el - e1v), 0.0), axis=0, keepdims=True)
    i1 = jnp.min(jnp.where(emask & (el == e1v), row, big), axis=0, keepdims=True)
    el2 = jnp.where(row == i1, neg, el)
    e2v = jnp.max(el2, axis=0, keepdims=True)
    i2 = jnp.min(jnp.where(emask & (row != i1) & (el2 == e2v), row, big), axis=0, keepdims=True)
    p1 = 1.0 / esum
    p2 = jnp.exp(e2v - e1v) / esum
    w1 = g_w * (p1 / (p1 + p2))
    w2 = g_w * (p2 / (p1 + p2))
    out_row = _row_iota(rrow_ref.shape)
    rrow_ref[...] = jnp.where(out_row == 0, i1.astype(F32),
                              jnp.where(out_row == 1, i2.astype(F32),
                                        jnp.where(out_row == 2, w1, jnp.where(out_row == 3, w2, 0.0))))
    step = pl.program_id(0)

    @pl.when(step == 0)
    def _():
        cnt_ref[...] = jnp.zeros_like(cnt_ref)

    chosen = jnp.where((row == i1) | (row == i2), 1.0, 0.0)
    cnt = cnt_ref[...]
    lane = _lane_iota(cnt.shape)
    tiles = tm // moe_tile
    for k in range(tiles):
        n = jnp.sum(chosen[0:N_EXPERTS, k * moe_tile:(k + 1) * moe_tile], axis=-1, keepdims=True)
        cnt = jnp.where(lane == step * tiles + k, n, cnt)
    cnt_ref[...] = cnt


def _outproj(x, ya, yb, yc, yd, w, g, wr, br, tm, moe_tile):
    t = x.shape[0]
    full = lambda a: pl.BlockSpec(a.shape, lambda i: (0, 0))
    tile = lambda wd: pl.BlockSpec((tm, wd), lambda i: (i, 0))
    return pl.pallas_call(
        functools.partial(_outproj_kernel, tm=tm, moe_tile=moe_tile),
        grid=(t // tm,),
        in_specs=[tile(D_MODEL)] + [tile(GROUP_WIDTH)] * 4 + [full(w), full(g), full(wr), full(br)],
        out_specs=[tile(D_MODEL), tile(D_MODEL), pl.BlockSpec((8, tm), lambda i: (0, i)),
                   pl.BlockSpec((N_EXPERTS, LANES), lambda i: (0, 0))],
        out_shape=[jax.ShapeDtypeStruct((t, D_MODEL), F32), jax.ShapeDtypeStruct((t, D_MODEL), BF16),
                   jax.ShapeDtypeStruct((8, t), F32), jax.ShapeDtypeStruct((N_EXPERTS, LANES), F32)],
        compiler_params=_cparams("arbitrary"),
        name="outproj_router",
    )(x, ya, yb, yc, yd, w, g, wr, br)


def _positions_kernel(r_ref, dest_ref, meta_ref, cnt_ref, carry_ref, start_ref, *, tm, nbp):
    phase = pl.program_id(0)
    i = pl.program_id(1)
    e_iota = _row_iota((N_EXPERTS, tm))
    e0 = r_ref[0:1, :].astype(I32)
    e1 = r_ref[1:2, :].astype(I32)
    oh0 = e_iota == e0
    oh1 = e_iota == e1
    oh = jnp.where(oh0 | oh1, 1.0, 0.0)

    @pl.when((phase == 0) & (i == 0))
    def _():
        cnt_ref[...] = jnp.zeros_like(cnt_ref)

    @pl.when(phase == 0)
    def _():
        cnt_ref[...] += jnp.sum(oh, axis=-1, keepdims=True)

    @pl.when((phase == 1) & (i == 0))
    def _():
        cnt = cnt_ref[...]
        padded = jnp.floor((cnt + (MOE_ROWS - 1)) * (1.0 / MOE_ROWS)) * MOE_ROWS
        tril = jnp.where(_row_iota((N_EXPERTS, N_EXPERTS)) >= _lane_iota((N_EXPERTS, N_EXPERTS)), 1.0, 0.0)
        pend = jnp.dot(tril, padded, preferred_element_type=F32, precision=lax.Precision.HIGHEST)
        pstart = pend - padded
        start_ref[...] = pstart
        carry_ref[...] = jnp.zeros_like(carry_ref)
        pend_b = jnp.concatenate([pend] * (nbp // LANES), axis=1)
        vend_b = jnp.concatenate([pstart + cnt] * (nbp // LANES), axis=1)
        b0 = (_lane_iota((N_EXPERTS, nbp)) * MOE_ROWS).astype(F32)
        bexp = jnp.sum(jnp.where(pend_b <= b0, 1.0, 0.0), axis=0, keepdims=True)
        bexp = jnp.minimum(bexp, N_EXPERTS - 1.0)
        is_e = _row_iota((N_EXPERTS, nbp)).astype(F32) == bexp
        vend = jnp.sum(jnp.where(is_e, vend_b, 0.0), axis=0, keepdims=True)
        nvalid = jnp.clip(vend - b0[0:1], 0.0, float(MOE_ROWS))
        total = jnp.max(pend_b, axis=0, keepdims=True) * (1.0 / MOE_ROWS)
        row = _row_iota((8, nbp))
        meta = jnp.where(row == 0, bexp, jnp.where(row == 1, nvalid, jnp.where(row == 2, total, 0.0)))
        meta_ref[...] = meta.astype(I32)

    @pl.when(phase == 1)
    def _():
        su = jnp.where(_row_iota((tm, tm)) < _lane_iota((tm, tm)), 1.0, 0.0).astype(BF16)
        before = jnp.dot(oh.astype(BF16), su, preferred_element_type=F32)
        base = start_ref[:, 0:1] + carry_ref[:, 0:1] + before
        d0 = jnp.sum(jnp.where(oh0, base, 0.0), axis=0, keepdims=True)
        d1 = jnp.sum(jnp.where(oh1, base, 0.0), axis=0, keepdims=True)
        dest_ref[0, 0:1, :] = d0.astype(I32)
        dest_ref[0, 1:2, :] = d1.astype(I32)
        carry_ref[...] += jnp.sum(oh, axis=-1, keepdims=True)


def _positions(rrow, tm, nbp):
    t = rrow.shape[1]
    nt = t // tm
    kern = functools.partial(_positions_kernel, tm=tm, nbp=nbp)
    return pl.pallas_call(
        kern,
        grid=(2, nt),
        in_specs=[pl.BlockSpec((8, tm), lambda p, i: (0, i))],
        out_specs=[pl.BlockSpec((1, 2, tm), lambda p, i: (i * p, 0, 0)),
                   pl.BlockSpec((8, nbp), lambda p, i: (0, 0))],
        out_shape=[jax.ShapeDtypeStruct((nt, 2, tm), I32), jax.ShapeDtypeStruct((8, nbp), I32)],
        scratch_shapes=[pltpu.VMEM((N_EXPERTS, LANES), F32)] * 3,
        compiler_params=_cparams("arbitrary", "arbitrary"),
        name="moe_positions",
    )(rrow)


def _dispatch_kernel(meta_ref, dest_ref, h_ref, xs_ref, zero_ref, sem, zsem, *, tm, nb):
    i = pl.program_id(0)

    @pl.when(i == 0)
    def _():
        zero_ref[...] = jnp.zeros_like(zero_ref)
        n_used = meta_ref[2, 0]

        def zcopy(b):
            return pltpu.make_async_copy(zero_ref, xs_ref.at[pl.ds(b * MOE_ROWS, MOE_ROWS)], zsem)

        def needs(b):
            return (b < n_used) & (meta_ref[1, b] < MOE_ROWS)

        def start(b, c):
            @pl.when(needs(b))
            def _():
                zcopy(b).start()
            return c

        def wait(b, c):
            @pl.when(needs(b))
            def _():
                zcopy(b).wait()
            return c

        lax.fori_loop(0, nb, start, 0)
        lax.fori_loop(0, nb, wait, 0)

    def copy(t, k):
        return pltpu.make_async_copy(h_ref.at[pl.ds(t, 1)], xs_ref.at[pl.ds(dest_ref[0, k, t], 1)], sem)

    def start(t, c):
        copy(t, 0).start()
        copy(t, 1).start()
        return c

    def wait(t, c):
        copy(t, 0).wait()
        copy(t, 1).wait()
        return c

    lax.fori_loop(0, tm, start, 0)
    lax.fori_loop(0, tm, wait, 0)


def _dispatch(meta, dest, h2, tm, nb):
    t = h2.shape[0]
    kern = functools.partial(_dispatch_kernel, tm=tm, nb=nb)
    grid_spec = pltpu.PrefetchScalarGridSpec(
        num_scalar_prefetch=1,
        grid=(t // tm,),
        in_specs=[pl.BlockSpec((1, 2, tm), lambda i, m: (i, 0, 0), memory_space=pltpu.SMEM),
                  pl.BlockSpec((tm, D_MODEL), lambda i, m: (i, 0))],
        out_specs=pl.BlockSpec(memory_space=pl.ANY),
        scratch_shapes=[pltpu.VMEM((MOE_ROWS, D_MODEL), F32),
                        pltpu.SemaphoreType.DMA, pltpu.SemaphoreType.DMA],
    )
    return pl.pallas_call(
        kern,
        grid_spec=grid_spec,
        out_shape=jax.ShapeDtypeStruct((nb * MOE_ROWS, D_MODEL), F32),
        compiler_params=_cparams("arbitrary"),
        name="moe_dispatch",
    )(meta, dest, h2)


def _expert_kernel(meta_ref, x_ref, wg_ref, wu_ref, wd_ref, o_ref):
    b = pl.program_id(0)

    @pl.when(b < meta_ref[2, 0])
    def _():
        x = x_ref[...].astype(BF16)
        gate = jnp.dot(x, wg_ref[0, 0].astype(BF16), preferred_element_type=F32)
        up = jnp.dot(x, wu_ref[0, 0].astype(BF16), preferred_element_type=F32)
        act = (_silu(gate) * up).astype(BF16)
        o_ref[...] = jnp.dot(act, wd_ref[0, 0].astype(BF16), preferred_element_type=F32)


def _experts(meta, xs, wg, wu, wd, layer, nb):
    def blk(b, m):
        return (jnp.minimum(b, m[2, 0] - 1), 0)

    def wblk(b, m):
        return (layer, m[0, jnp.minimum(b, m[2, 0] - 1)], 0, 0)

    grid_spec = pltpu.PrefetchScalarGridSpec(
        num_scalar_prefetch=1,
        grid=(nb,),
        in_specs=[pl.BlockSpec((MOE_ROWS, D_MODEL), blk),
                  pl.BlockSpec((1, 1, D_MODEL, EXPERT_FF), wblk),
                  pl.BlockSpec((1, 1, D_MODEL, EXPERT_FF), wblk),
                  pl.BlockSpec((1, 1, EXPERT_FF, D_MODEL), wblk)],
        out_specs=pl.BlockSpec((MOE_ROWS, D_MODEL), blk),
    )
    return pl.pallas_call(
        _expert_kernel,
        grid_spec=grid_spec,
        out_shape=jax.ShapeDtypeStruct((nb * MOE_ROWS, D_MODEL), F32),
        compiler_params=_cparams("arbitrary"),
        name="moe_experts",
    )(meta, xs, wg, wu, wd)


def _combine_kernel(dest_ref, x_ref, r_ref, g_ref, ys_ref, o_ref, buf_ref, sem, *, tm, final):
    def copy(t, k):
        return pltpu.make_async_copy(ys_ref.at[pl.ds(dest_ref[0, k, t], 1)],
                                     buf_ref.at[k, pl.ds(t, 1)], sem)

    def start(t, c):
        copy(t, 0).start()
        copy(t, 1).start()
        return c

    def wait(t, c):
        copy(t, 0).wait()
        copy(t, 1).wait()
        return c

    lax.fori_loop(0, tm, start, 0)
    lax.fori_loop(0, tm, wait, 0)
    w0 = r_ref[:, 2:3]
    w1 = r_ref[:, 3:4]
    x = x_ref[...] + (buf_ref[0] * w0 + buf_ref[1] * w1)
    o_ref[...] = _rms(x, g_ref[...]) if final else x


def _combine(dest, x2, rcol, g, ys, tm, final):
    t = x2.shape[0]
    kern = functools.partial(_combine_kernel, tm=tm, final=final)
    return pl.pallas_call(
        kern,
        grid=(t // tm,),
        in_specs=[pl.BlockSpec((1, 2, tm), lambda i: (i, 0, 0), memory_space=pltpu.SMEM),
                  pl.BlockSpec((tm, D_MODEL), lambda i: (i, 0)),
                  pl.BlockSpec((tm, LANES), lambda i: (i, 0)),
                  pl.BlockSpec((1, D_MODEL), lambda i: (0, 0)),
                  pl.BlockSpec(memory_space=pl.ANY)],
        out_specs=pl.BlockSpec((tm, D_MODEL), lambda i: (i, 0)),
        out_shape=jax.ShapeDtypeStruct((t, D_MODEL), F32),
        scratch_shapes=[pltpu.VMEM((2, tm, D_MODEL), F32), pltpu.SemaphoreType.DMA],
        compiler_params=_cparams("arbitrary"),
        name="moe_combine",
    )(dest, x2, rcol, g, ys)


MOE_TILE = 256
CHUNK = 8
LOCAL_ROWS = 2 * MOE_TILE + 256
PACKED = D_MODEL // 2
XS_WIDTH = PACKED
U32 = jnp.uint32


def _pack_bf16_pairs(x, exact=False):
    if not exact:
        x = x.astype(BF16).astype(F32)
    half = x.shape[1] // 2
    lo = lax.bitcast_convert_type(x[:, :half], U32)
    hi = lax.bitcast_convert_type(x[:, half:], U32)
    return hi | (lo >> 16)


def _unpack_bf16_pairs(words):
    lo = lax.bitcast_convert_type(words << 16, F32)
    hi = lax.bitcast_convert_type(words & U32(0xFFFF0000), F32)
    return jnp.concatenate([lo, hi], axis=1).astype(BF16)
TAB_CHUNKS, TAB_LOCAL, TAB_GLOBAL, TAB_TOTAL = 0, 1, 2, 3


def _route_kernel(r_ref, cnt_ref, lrow_ref, lcol_ref, tab_ref, ctab_ref, meta_ref,
                  loff_ref, goff_ref, n8_ref, *, tm, nbp):
    i = pl.program_id(0)
    e_iota = _row_iota((N_EXPERTS, tm))
    oh0 = e_iota == r_ref[0:1, :].astype(I32)
    oh1 = e_iota == r_ref[1:2, :].astype(I32)
    oh = jnp.where(oh0 | oh1, 1.0, 0.0)
    tile_lane = _lane_iota((N_EXPERTS, LANES)) == i
    hi = lax.Precision.HIGHEST

    @pl.when(i == 0)
    def _():
        cnt = cnt_ref[...]
        n8 = jnp.floor((cnt + (CHUNK - 1)) * (1.0 / CHUNK)) * CHUNK
        er = _row_iota((N_EXPERTS, N_EXPERTS))
        ec = _lane_iota((N_EXPERTS, N_EXPERTS))
        below = jnp.where(er > ec, 1.0, 0.0)
        loff = jnp.dot(below, n8, preferred_element_type=F32, precision=hi)
        rows_e = jnp.sum(n8, axis=-1, keepdims=True) + jnp.zeros_like(n8)
        padded = jnp.floor((rows_e + (MOE_ROWS - 1)) * (1.0 / MOE_ROWS)) * MOE_ROWS
        e_start = jnp.dot(below, padded, preferred_element_type=F32, precision=hi)
        tr = _row_iota((LANES, LANES))
        tc = _lane_iota((LANES, LANES))
        earlier = jnp.where(tr < tc, 1.0, 0.0)
        goff = e_start + jnp.dot(n8, earlier, preferred_element_type=F32, precision=hi)
        loff_ref[...] = loff
        goff_ref[...] = goff
        n8_ref[...] = n8
        tab_ref[TAB_CHUNKS] = (n8 * (1.0 / CHUNK)).astype(I32)
        tab_ref[TAB_LOCAL] = loff.astype(I32)
        tab_ref[TAB_GLOBAL] = goff.astype(I32)
        tab_ref[TAB_TOTAL] = (jnp.sum(n8, axis=0, keepdims=True) * (1.0 / CHUNK)
                              + jnp.zeros_like(n8)).astype(I32)
        reps = nbp // LANES
        pend_b = jnp.concatenate([e_start + padded] * reps, axis=1)
        vend_b = jnp.concatenate([e_start + rows_e] * reps, axis=1)
        b0 = (_lane_iota((N_EXPERTS, nbp)) * MOE_ROWS).astype(F32)
        bexp = jnp.sum(jnp.where(pend_b <= b0, 1.0, 0.0), axis=0, keepdims=True)
        bexp = jnp.minimum(bexp, N_EXPERTS - 1.0)
        is_e = _row_iota((N_EXPERTS, nbp)).astype(F32) == bexp
        vend = jnp.sum(jnp.where(is_e, vend_b, 0.0), axis=0, keepdims=True)
        nvalid = jnp.clip(vend - b0[0:1], 0.0, float(MOE_ROWS))
        total = jnp.max(pend_b, axis=0, keepdims=True) * (1.0 / MOE_ROWS)
        row = _row_iota((8, nbp))
        meta = jnp.where(row == 0, bexp, jnp.where(row == 1, nvalid, jnp.where(row == 2, total, 0.0)))
        meta_ref[...] = meta.astype(I32)

    su = jnp.where(_row_iota((tm, tm)) < _lane_iota((tm, tm)), 1.0, 0.0).astype(BF16)
    before = jnp.dot(oh.astype(BF16), su, preferred_element_type=F32)
    base = jnp.sum(jnp.where(tile_lane, loff_ref[...], 0.0), axis=-1, keepdims=True) + before
    d0 = jnp.sum(jnp.where(oh0, base, 0.0), axis=0, keepdims=True)
    d1 = jnp.sum(jnp.where(oh1, base, 0.0), axis=0, keepdims=True)
    lrow_ref[0, 0:1, :] = d0.astype(I32)
    lrow_ref[0, 1:2, :] = d1.astype(I32)
    pick_tile = lambda ref: jnp.sum(jnp.where(tile_lane, ref[...], 0.0), axis=-1, keepdims=True)
    nch = pick_tile(n8_ref) * (1.0 / CHUNK)
    incl = jnp.where(_row_iota((N_EXPERTS, N_EXPERTS)) >= _lane_iota((N_EXPERTS, N_EXPERTS)), 1.0, 0.0)
    cend = jnp.dot(incl, nch + jnp.zeros((N_EXPERTS, LANES), F32), preferred_element_type=F32, precision=hi)
    cstart = cend - nch
    cidx = _lane_iota((N_EXPERTS, LANES)).astype(F32)
    mine = (cidx >= cstart) & (cidx < cend)
    step_rows = (cidx - cstart) * CHUNK
    lo = jnp.sum(jnp.where(mine, pick_tile(loff_ref) + step_rows, 0.0), axis=0, keepdims=True)
    go = jnp.sum(jnp.where(mine, pick_tile(goff_ref) + step_rows, 0.0), axis=0, keepdims=True)
    ctab_ref[0, 0:1, :] = lo.astype(I32)
    ctab_ref[0, 1:2, :] = go.astype(I32)
    row = _row_iota((LANES, tm))
    lcol_ref[...] = jnp.where(row == 0, d0, jnp.where(row == 1, d1,
                              jnp.where(row == 2, r_ref[2:3, :], jnp.where(row == 3, r_ref[3:4, :], 0.0)))).T


def _route(rrow, cnt, tm, nbp):
    t = rrow.shape[1]
    nt = t // tm
    kern = functools.partial(_route_kernel, tm=tm, nbp=nbp)
    return pl.pallas_call(
        kern,
        grid=(nt,),
        in_specs=[pl.BlockSpec((8, tm), lambda i: (0, i)),
                  pl.BlockSpec((N_EXPERTS, LANES), lambda i: (0, 0))],
        out_specs=[pl.BlockSpec((1, 2, tm), lambda i: (i, 0, 0)),
                   pl.BlockSpec((tm, LANES), lambda i: (i, 0)),
                   pl.BlockSpec((4, N_EXPERTS, LANES), lambda i: (0, 0, 0)),
                   pl.BlockSpec((1, 2, LANES), lambda i: (i, 0, 0)),
                   pl.BlockSpec((8, nbp), lambda i: (0, 0))],
        out_shape=[jax.ShapeDtypeStruct((nt, 2, tm), I32), jax.ShapeDtypeStruct((t, LANES), F32),
                   jax.ShapeDtypeStruct((4, N_EXPERTS, LANES), I32), jax.ShapeDtypeStruct((nt, 2, LANES), I32),
                   jax.ShapeDtypeStruct((8, nbp), I32)],
        scratch_shapes=[pltpu.VMEM((N_EXPERTS, LANES), F32)] * 3,
        compiler_params=_cparams("arbitrary"),
        name="moe_route",
    )(rrow, cnt)


def _chunk_copies(tabs, i, local_ref, global_ref, sem, to_global, action):
    tab_ref, ctab_ref = tabs
    if action == "wait":
        lsl = local_ref.at[pl.ds(0, CHUNK)]
        gsl = global_ref.at[pl.ds(0, CHUNK)]
        cp = pltpu.make_async_copy(lsl, gsl, sem) if to_global else pltpu.make_async_copy(gsl, lsl, sem)

        def one(c, c1):
            cp.wait()
            return c1

        lax.fori_loop(0, tab_ref[TAB_TOTAL, 0, i], one, 0)
        return

    def chunk(c, c1):
        lo = ctab_ref[i, 0, c]
        go = ctab_ref[i, 1, c]
        lsl = local_ref.at[pl.ds(pl.multiple_of(lo, CHUNK), CHUNK)]
        gsl = global_ref.at[pl.ds(pl.multiple_of(go, CHUNK), CHUNK)]
        cp = pltpu.make_async_copy(lsl, gsl, sem) if to_global else pltpu.make_async_copy(gsl, lsl, sem)
        cp.start()
        return c1

    lax.fori_loop(0, tab_ref[TAB_TOTAL, 0, i], chunk, 0)


def _scatter_kernel(tab_ref, ctab_ref, meta_ref, lrow_ref, h_ref, xs_ref, buf_ref, zero_ref, sem, zsem, *, tm, nb):
    i = pl.program_id(0)
    tabs = (tab_ref, ctab_ref)

    @pl.when(i == 0)
    def _():
        zero_ref[...] = jnp.zeros_like(zero_ref)
        n_used = meta_ref[2, 0]

        def zcopy(b):
            sub = lax.shift_right_logical(meta_ref[1, b], int(math.log2(FFN_SUB)))
            start = pl.multiple_of(b * MOE_ROWS + sub * FFN_SUB, FFN_SUB)
            return pltpu.make_async_copy(zero_ref, xs_ref.at[pl.ds(start, FFN_SUB)], zsem)

        def needs(b):
            return (b < n_used) & ((meta_ref[1, b] & (FFN_SUB - 1)) != 0)

        def start(b, c):
            @pl.when(needs(b))
            def _():
                zcopy(b).start()
            return c

        def wait(b, c):
            @pl.when(needs(b))
            def _():
                zcopy(b).wait()
            return c

        lax.fori_loop(0, nb, start, 0)
        lax.fori_loop(0, nb, wait, 0)

    rows = _row_iota((LOCAL_ROWS, tm))
    p0 = rows == lrow_ref[0, 0:1, :]
    p1 = rows == lrow_ref[0, 1:2, :]
    perm = jnp.where(p0 | p1, 1.0, 0.0).astype(BF16)
    sorted_rows = jnp.dot(perm, h_ref[...], preferred_element_type=F32)

    def fill(slot):
        buf = buf_ref.at[slot]
        buf[...] = _pack_bf16_pairs(sorted_rows, exact=True)
        _chunk_copies(tabs, i, buf, xs_ref, sem.at[slot], True, "start")

    def drain(tile, slot):
        _chunk_copies(tabs, tile, buf_ref.at[slot], xs_ref, sem.at[slot], True, "wait")

    even = (i & 1) == 0

    @pl.when(even)
    def _():
        fill(0)

    @pl.when(jnp.logical_not(even))
    def _():
        fill(1)

    @pl.when((i > 0) & even)
    def _():
        drain(i - 1, 1)

    @pl.when((i > 0) & jnp.logical_not(even))
    def _():
        drain(i - 1, 0)

    @pl.when((i == pl.num_programs(0) - 1) & even)
    def _():
        drain(i, 0)

    @pl.when((i == pl.num_programs(0) - 1) & jnp.logical_not(even))
    def _():
        drain(i, 1)


def _scatter(tab, ctab, meta, lrow, h2, tm, nb):
    t = h2.shape[0]
    kern = functools.partial(_scatter_kernel, tm=tm, nb=nb)
    grid_spec = pltpu.PrefetchScalarGridSpec(
        num_scalar_prefetch=3,
        grid=(t // tm,),
        in_specs=[pl.BlockSpec((1, 2, tm), lambda i, *_: (i, 0, 0)),
                  pl.BlockSpec((tm, D_MODEL), lambda i, *_: (i, 0))],
        out_specs=pl.BlockSpec(memory_space=pl.ANY),
        scratch_shapes=[pltpu.VMEM((2, LOCAL_ROWS, XS_WIDTH), U32), pltpu.VMEM((FFN_SUB, XS_WIDTH), U32),
                        pltpu.SemaphoreType.DMA((2,)), pltpu.SemaphoreType.DMA],
    )
    return pl.pallas_call(
        kern,
        grid_spec=grid_spec,
        out_shape=jax.ShapeDtypeStruct((nb * MOE_ROWS, XS_WIDTH), U32),
        compiler_params=_cparams("arbitrary"),
        name="moe_scatter",
    )(tab, ctab, meta, lrow, h2)


def _ffn_kernel(meta_ref, x_ref, wg_ref, wu_ref, wd_ref, o_ref, wgu_b, wd_b):
    b = pl.program_id(0)
    live = b < meta_ref[2, 0]
    prev = meta_ref[0, jnp.maximum(b - 1, 0)]

    @pl.when(live & ((b == 0) | (meta_ref[0, b] != prev)))
    def _():
        wgu_b[:, 0:EXPERT_FF] = wg_ref[0, 0].astype(BF16)
        wgu_b[:, EXPERT_FF:2 * EXPERT_FF] = wu_ref[0, 0].astype(BF16)
        wd_b[...] = wd_ref[0, 0].astype(BF16)

    nvalid = meta_ref[1, jnp.maximum(jnp.minimum(b, meta_ref[2, 0] - 1), 0)]
    for sub in range(MOE_ROWS // FFN_SUB):
        rows = slice(sub * FFN_SUB, (sub + 1) * FFN_SUB)
        used = live & (nvalid > sub * FFN_SUB)

        @pl.when(used)
        def _():
            x = _unpack_bf16_pairs(x_ref[rows, 0:PACKED])
            gu = jnp.dot(x, wgu_b[...], preferred_element_type=F32)
            act = (_silu(gu[:, 0:EXPERT_FF]) * gu[:, EXPERT_FF:2 * EXPERT_FF]).astype(BF16)
            y = jnp.dot(act, wd_b[...], preferred_element_type=F32)
            o_ref[rows, :] = _pack_bf16_pairs(y)

        @pl.when(live & jnp.logical_not(used))
        def _():
            o_ref[rows, :] = jnp.zeros((FFN_SUB, PACKED), U32)


def _ffn(meta, xs, wg, wu, wd, layer, nb):
    def blk(b, m):
        return (jnp.maximum(jnp.minimum(b, m[2, 0] - 1), 0), 0)

    def wblk(b, m):
        return (layer, m[0, jnp.maximum(jnp.minimum(b, m[2, 0] - 1), 0)], 0, 0)

    grid_spec = pltpu.PrefetchScalarGridSpec(
        num_scalar_prefetch=1,
        grid=(nb,),
        in_specs=[pl.BlockSpec((MOE_ROWS, XS_WIDTH), blk),
                  pl.BlockSpec((1, 1, D_MODEL, EXPERT_FF), wblk),
                  pl.BlockSpec((1, 1, D_MODEL, EXPERT_FF), wblk),
                  pl.BlockSpec((1, 1, EXPERT_FF, D_MODEL), wblk)],
        out_specs=pl.BlockSpec((MOE_ROWS, PACKED), blk),
        scratch_shapes=[pltpu.VMEM((D_MODEL, 2 * EXPERT_FF), BF16), pltpu.VMEM((EXPERT_FF, D_MODEL), BF16)],
    )
    return pl.pallas_call(
        _ffn_kernel,
        grid_spec=grid_spec,
        out_shape=jax.ShapeDtypeStruct((nb * MOE_ROWS, PACKED), U32),
        compiler_params=_cparams("arbitrary"),
        name="moe_experts",
    )(meta, xs, wg, wu, wd)


def _gather_kernel(tab_ref, ctab_ref, lcol_ref, x_ref, g_ref, ys_ref, o_ref, buf_ref, sem, *, tm, final):
    i = pl.program_id(0)

    last = pl.num_programs(0) - 1

    def fetch(tile, slot, action):
        _chunk_copies((tab_ref, ctab_ref), tile, buf_ref.at[slot], ys_ref, sem.at[slot], False, action)

    @pl.when(i == 0)
    def _():
        buf_ref[...] = jnp.zeros_like(buf_ref)
        fetch(0, 0, "start")

    even = (i & 1) == 0

    @pl.when((i < last) & even)
    def _():
        fetch(i + 1, 1, "start")

    @pl.when((i < last) & jnp.logical_not(even))
    def _():
        fetch(i + 1, 0, "start")

    col = _lane_iota((tm, LOCAL_ROWS)).astype(F32)
    pick0 = jnp.where(col == lcol_ref[:, 0:1], 1.0, 0.0).astype(BF16)
    pick1 = jnp.where(col == lcol_ref[:, 1:2], 1.0, 0.0).astype(BF16)

    def finish(slot):
        fetch(i, slot, "wait")
        y = _unpack_bf16_pairs(buf_ref[slot])
        both = jnp.dot(jnp.concatenate([pick0, pick1], axis=0), y, preferred_element_type=F32)
        x = x_ref[...] + lcol_ref[:, 2:3] * both[0:tm] + lcol_ref[:, 3:4] * both[tm:2 * tm]
        o_ref[...] = _rms(x, g_ref[...]) if final else x

    @pl.when(even)
    def _():
        finish(0)

    @pl.when(jnp.logical_not(even))
    def _():
        finish(1)


def _gather(tab, ctab, lcol, x2, g, ys, tm, final):
    t = x2.shape[0]
    kern = functools.partial(_gather_kernel, tm=tm, final=final)
    grid_spec = pltpu.PrefetchScalarGridSpec(
        num_scalar_prefetch=2,
        grid=(t // tm,),
        in_specs=[pl.BlockSpec((tm, LANES), lambda i, *_: (i, 0)),
                  pl.BlockSpec((tm, D_MODEL), lambda i, *_: (i, 0)),
                  pl.BlockSpec((1, D_MODEL), lambda i, *_: (0, 0)),
                  pl.BlockSpec(memory_space=pl.ANY)],
        out_specs=pl.BlockSpec((tm, D_MODEL), lambda i, *_: (i, 0)),
        scratch_shapes=[pltpu.VMEM((2, LOCAL_ROWS, PACKED), U32), pltpu.SemaphoreType.DMA((2,))],
    )
    return pl.pallas_call(
        kern,
        grid_spec=grid_spec,
        out_shape=jax.ShapeDtypeStruct((t, D_MODEL), F32),
        compiler_params=_cparams("arbitrary"),
        name="moe_combine",
    )(tab, ctab, lcol, x2, g, ys)


def _pad_rows(a, rows=8):
    return jnp.zeros((rows, a.shape[-1]), F32).at[:a.shape[0]].set(a.astype(F32))


def _arrange_w_in(w):
    gw = GROUP_WIDTH
    a0 = 0
    b0 = 3 * gw
    c0 = b0 + 3 * gw + N_HEADS
    d0 = c0 + gw + (gw + 4 * SSM_STATE) + N_HEADS
    half = MLA_ROPE // 2
    f_logit = w[:, b0 + 3 * gw:b0 + 3 * gw + N_HEADS]
    dt_raw = w[:, c0 + 3 * gw:c0 + 3 * gw + N_HEADS]
    kr0 = d0 + MLA_Q_LORA + MLA_KV_LORA
    kr = w[:, kr0:kr0 + MLA_ROPE]
    kr_sw = jnp.concatenate([-kr[:, half:], kr[:, :half]], axis=1)
    zeros = lambda n: jnp.zeros((w.shape[0], n), w.dtype)
    misc = jnp.concatenate([f_logit, dt_raw, zeros(MISC_ROPE - 2 * N_HEADS), kr,
                            zeros(LANES - MISC_ROPE - MLA_ROPE)], axis=1)
    misc2 = jnp.concatenate([zeros(MISC_ROPE), kr_sw, zeros(LANES - MISC_ROPE - MLA_ROPE)], axis=1)
    fox_q = w[:, b0:b0 + gw] * (HEAD_DIM ** -0.5 * LOG2E)
    out = jnp.concatenate([w[:, a0:a0 + 3 * gw], fox_q, w[:, b0 + gw:b0 + 2 * gw], w[:, c0:c0 + 3 * gw],
                           w[:, d0:d0 + MLA_Q_LORA + MLA_KV_LORA], misc, misc2], axis=1)
    fox_vt = w[:, b0 + 2 * gw:b0 + 3 * gw].T
    return out.astype(BF16), fox_vt.astype(BF16)


def _arrange_mla(w_uq, w_ukv):
    half = MLA_ROPE // 2
    qd = MLA_NOPE + MLA_ROPE
    wq, wqs, wk, wv = [], [], [], []
    zq = jnp.zeros((MLA_Q_LORA, LANES - qd), w_uq.dtype)
    zk = jnp.zeros((MLA_KV_LORA, LANES - MLA_NOPE), w_ukv.dtype)
    for h in range(N_HEADS):
        q = w_uq[:, h * qd:(h + 1) * qd]
        nope, rope = q[:, :MLA_NOPE], q[:, MLA_NOPE:]
        wq.append(jnp.concatenate([nope, rope, zq], axis=1))
        wqs.append(jnp.concatenate([jnp.zeros_like(nope), -rope[:, half:], rope[:, :half], zq], axis=1))
        kv = w_ukv[:, h * 2 * MLA_NOPE:(h + 1) * 2 * MLA_NOPE]
        wk.append(jnp.concatenate([kv[:, :MLA_NOPE], zk], axis=1))
        wv.append(kv[:, MLA_NOPE:])
    cat = lambda xs: jnp.concatenate(xs, axis=1).astype(BF16)
    return cat(wq), cat(wqs), cat(wk), cat(wv).T


def kernel(x, positions, norm_mix, w_in, conv_a, fox_forget_bias, ssm_conv_w, ssm_conv_b, ssm_dt_bias,
           ssm_a_log, ssm_d, ssm_norm, mla_q_norm, mla_kv_norm, mla_w_uq, mla_w_ukv, w_out, norm_ffn,
           router_group_w, router_group_b, router_expert_w, router_expert_b, expert_w_gate, expert_w_up,
           expert_w_down, norm_final):
    batch, seq, d = x.shape
    t = batch * seq
    depth = w_in.shape[0]
    tm = min(512, t)
    tq = min(ATTN_TQ, seq)
    tmd = min(MOE_TILE, t)
    max_rows = 2 * t + (CHUNK - 1) * N_EXPERTS * (t // tmd) + N_EXPERTS * (MOE_ROWS - 1)
    nb = -(-max_rows // MOE_ROWS)
    nbp = -(-nb // LANES) * LANES

    xf = x.reshape(t, d)
    pos_col = positions.astype(F32).reshape(t, 1)
    cos, sin = _rope_tables(pos_col, tm)

    for l in range(depth):
        w_in_p, w_vt = _arrange_w_in(w_in[l])
        pa, pb, pc, pd, misc, misc2, fox_vt = _inproj(xf, norm_mix[l][None, :], w_in_p, w_vt, tm)

        sp = jnp.zeros((8, LANES), F32)
        sp = sp.at[0, MISC_F:MISC_F + N_HEADS].set(fox_forget_bias[l])
        sp = sp.at[0, MISC_DT:MISC_DT + N_HEADS].set(ssm_dt_bias[l])
        sp = sp.at[1, MISC_DT:MISC_DT + N_HEADS].set(ssm_a_log[l])
        col, rows, fox_q, fox_k, tref = _scalar_prep(misc, sp, pb, batch, seq, tq)

        ya = _conv_mixer(pa, _pad_rows(conv_a[l]), batch, seq)
        yb = _attention(fox_q, fox_k, fox_vt, tref, batch, seq, tq, "fox_attention")
        conv_wb = _pad_rows(jnp.concatenate([ssm_conv_w[l], ssm_conv_b[l][None, :]], axis=0))
        ssd_par = _pad_rows(jnp.stack([jnp.repeat(ssm_d[l], HEAD_DIM), ssm_norm[l]]))
        yc = _ssd_mixer(pc, col, rows, conv_wb, ssd_par, batch, seq)
        wq, wqs, wk, wv = _arrange_mla(mla_w_uq[l], mla_w_ukv[l])
        q, k, v = _mla_prep(pd, misc, misc2, cos, sin, mla_q_norm[l][None, :], mla_kv_norm[l][None, :],
                            wq, wqs, wk, wv, tm)
        yd = _attention(q, k, v, None, batch, seq, tq, "mla_attention")

        pad = jnp.zeros((d, LANES - N_EXPERTS - N_EXPERT_GROUPS), F32)
        wr = jnp.concatenate([router_expert_w[l], router_group_w[l], pad], axis=1)
        wr_hi = wr.astype(BF16)
        wr = jnp.concatenate([wr_hi, (wr - wr_hi.astype(F32)).astype(BF16)], axis=1)
        br = jnp.concatenate([router_expert_b[l], router_group_b[l], pad[0]])[None, :]
        x2, h2, rrow, cnt = _outproj(xf, ya, yb, yc, yd, w_out[l].astype(BF16), norm_ffn[l][None, :], wr, br,
                                     tm, tmd)

        lrow, lcol, tab, ctab, meta = _route(rrow, cnt, tmd, nbp)
        xs = _scatter(tab, ctab, meta, lrow, h2, tmd, nb)
        ys = _ffn(meta, xs, expert_w_gate, expert_w_up, expert_w_down, l, nb)
        final = l == depth - 1
        xf = _gather(tab, ctab, lcol, x2, norm_final[None, :], ys, tmd, final)

    return xf.reshape(batch, seq, d)


def _retile(dest, tm, tmd):
    if tm == tmd:
        return dest
    nt = dest.shape[0]
    return dest.reshape(nt, 2, tm // tmd, tmd).transpose(0, 2, 1, 3).reshape(nt * (tm // tmd), 2, tmd)
```

```python
import functools
import math

import jax
import jax.numpy as jnp
import numpy as np
from jax import lax
from jax.experimental import pallas as pl
from jax.experimental.pallas import tpu as pltpu

F32 = jnp.float32
BF16 = jnp.bfloat16
I32 = jnp.int32

LANES = 128
VMEM_LIMIT_BYTES = 56 * 1024 * 1024

D_MODEL = 1024
RMS_EPS = 1e-6
LOG2E = math.log2(math.e)
GROUP_WIDTH = 256
HEAD_DIM = 64
N_HEADS = 4

CONV_A_WIDTH = 3
SSM_CONV = 4
SSM_STATE = 64
SSM_CHUNK = 256

MLA_NOPE = 64
MLA_ROPE = 32
MLA_Q_LORA = 256
MLA_KV_LORA = 128
ROPE_BASE = 10000.0
MLA_CHUNK = 64
ATTN_TQ = 512
ATTN_RB = 128

N_EXPERT_GROUPS = 4
EXPERTS_PER_GROUP = 8
N_EXPERTS = 32
EXPERT_FF = 256
MOE_ROWS = 512
FFN_SUB = 256

SEG_A = (0, 768)
SEG_B = (768, 1280)
SEG_C = (1280, 2048)
SEG_D = (2048, 2432)
SEG_M = (2432, 2560)
SEG_M2 = (2560, 2688)
IN_COLS_PADDED = 2688
HEAD_PAD = N_HEADS * LANES
AUG_LANE = HEAD_DIM
MISC_F = 0
MISC_DT = 4
MISC_ROPE = 64
COL_CUMF = 0
COL_DT = 4
COL_ACUM = 8
N_SCALAR_ROWS = 16


def _cparams(*sem):
    return pltpu.CompilerParams(dimension_semantics=sem, vmem_limit_bytes=VMEM_LIMIT_BYTES)


def _lane_iota(shape):
    return lax.broadcasted_iota(I32, shape, len(shape) - 1)


def _row_iota(shape):
    return lax.broadcasted_iota(I32, shape, 0)


def _rms(x, g):
    ms = jnp.mean(x * x, axis=-1, keepdims=True)
    return x * lax.rsqrt(ms + RMS_EPS) * g


def _silu(x):
    return x / (1.0 + jnp.exp(-x))


def _softplus(x):
    return jnp.maximum(x, 0.0) + jnp.log(1.0 + jnp.exp(-jnp.abs(x)))


def _shift_rows(x, k):
    rolled = pltpu.roll(x, k, 0)
    return jnp.where(_row_iota(x.shape) >= k, rolled, 0.0)


def _rope_kernel(pos_ref, freq_ref, cos_ref, sin_ref):
    ang = pos_ref[...] * freq_ref[...]
    lane = _lane_iota(ang.shape)
    rope = (lane >= MISC_ROPE) & (lane < MISC_ROPE + MLA_ROPE)
    cos_ref[...] = jnp.where(rope, jnp.cos(ang), jnp.where(lane < MISC_ROPE, 1.0, 0.0))
    sin_ref[...] = jnp.where(rope, jnp.sin(ang), 0.0)


def _rope_tables(pos_col, tm):
    t = pos_col.shape[0]
    half = MLA_ROPE // 2
    inv = ROPE_BASE ** (-np.arange(0, MLA_ROPE, 2, dtype=np.float32) / MLA_ROPE)
    freq = np.zeros((1, LANES), np.float32)
    freq[0, MISC_ROPE:MISC_ROPE + half] = inv
    freq[0, MISC_ROPE + half:MISC_ROPE + MLA_ROPE] = inv
    return pl.pallas_call(
        _rope_kernel,
        grid=(t // tm,),
        in_specs=[pl.BlockSpec((tm, 1), lambda i: (i, 0)),
                  pl.BlockSpec((1, LANES), lambda i: (0, 0))],
        out_specs=[pl.BlockSpec((tm, LANES), lambda i: (i, 0))] * 2,
        out_shape=[jax.ShapeDtypeStruct((t, LANES), F32)] * 2,
        compiler_params=_cparams("parallel"),
        name="rope_tables",
    )(pos_col, jnp.asarray(freq))


def _inproj_kernel(x_ref, g_ref, w_ref, wvt_ref, oa, ob, oc, od, om, om2, ovt):
    h = _rms(x_ref[...], g_ref[...]).astype(BF16)
    for o, (lo, hi) in ((oa, SEG_A), (ob, SEG_B), (oc, SEG_C), (od, SEG_D), (om, SEG_M), (om2, SEG_M2)):
        o[...] = jnp.dot(h, w_ref[:, lo:hi], preferred_element_type=F32).astype(o.dtype)
    ovt[...] = lax.dot_general(wvt_ref[...], h, (((1,), (1,)), ((), ())),
                               preferred_element_type=F32).astype(ovt.dtype)


def _inproj(x, g, w, wvt, tm):
    t = x.shape[0]
    widths = [(s[1] - s[0]) for s in (SEG_A, SEG_B, SEG_C, SEG_D, SEG_M, SEG_M2)]
    dtypes = [BF16, BF16, BF16, BF16, F32, F32]
    return pl.pallas_call(
        _inproj_kernel,
        grid=(t // tm,),
        in_specs=[pl.BlockSpec((tm, D_MODEL), lambda i: (i, 0)),
                  pl.BlockSpec((1, D_MODEL), lambda i: (0, 0)),
                  pl.BlockSpec((D_MODEL, IN_COLS_PADDED), lambda i: (0, 0)),
                  pl.BlockSpec((GROUP_WIDTH, D_MODEL), lambda i: (0, 0))],
        out_specs=[pl.BlockSpec((tm, wd), lambda i: (i, 0)) for wd in widths]
        + [pl.BlockSpec((GROUP_WIDTH, tm), lambda i: (0, i))],
        out_shape=[jax.ShapeDtypeStruct((t, wd), dt) for wd, dt in zip(widths, dtypes)]
        + [jax.ShapeDtypeStruct((GROUP_WIDTH, t), BF16)],
        compiler_params=_cparams("parallel"),
        name="inproj",
    )(x, g, w, wvt)


def _scalar_prep_kernel(m_ref, p_ref, qk_ref, sel_ref, place_ref, const_ref,
                        col_ref, row_ref, qa_ref, ka_ref, tref_ref, *, tq):
    s = m_ref.shape[0]
    tref_ref[...] = jnp.zeros_like(tref_ref)
    tile_ref = jnp.zeros((1, LANES), F32)
    m = m_ref[...]
    bias = p_ref[0:1, :]
    a_log = p_ref[1:2, :]
    lane = _lane_iota(m.shape)
    z = m + bias
    logf = jnp.minimum(z, 0.0) - jnp.log(1.0 + jnp.exp(-jnp.abs(z)))
    dt = _softplus(z)
    a = dt * (-jnp.exp(a_log))
    is_f = lane < MISC_DT
    is_dt = (lane >= MISC_DT) & (lane < MISC_DT + N_HEADS)
    v = jnp.where(is_f, logf, jnp.where(is_dt, a, 0.0))
    r = _row_iota((SSM_CHUNK, SSM_CHUNK))
    c = _lane_iota((SSM_CHUNK, SSM_CHUNK))
    tril = jnp.where(r >= c, 1.0, 0.0).astype(F32)
    carry = jnp.zeros((1, LANES), F32)
    lane_1 = _lane_iota((1, LANES))
    lane_b = _lane_iota((SSM_CHUNK, LANES))
    for ci in range(s // SSM_CHUNK):
        blk = v[ci * SSM_CHUNK:(ci + 1) * SSM_CHUNK]
        cs = jnp.dot(tril, blk, preferred_element_type=F32, precision=lax.Precision.HIGHEST)
        cs = cs + jnp.where(lane_1 < MISC_DT, carry, 0.0)
        carry = cs[SSM_CHUNK - 1:SSM_CHUNK]
        acum = pltpu.roll(cs, COL_ACUM - MISC_DT, 1)
        out = jnp.where(lane_b < MISC_DT, cs * LOG2E,
                        jnp.where(lane_b < COL_ACUM, dt[ci * SSM_CHUNK:(ci + 1) * SSM_CHUNK],
                                  jnp.where(lane_b < COL_ACUM + N_HEADS, acum, 0.0)))
        rows = slice(ci * SSM_CHUNK, (ci + 1) * SSM_CHUNK)
        col_ref[rows, :] = out
        row_ref[0, :, rows] = out.T[:N_SCALAR_ROWS]
        if (ci * SSM_CHUNK) % tq == 0:
            tile_ref = out[0:1, :]
            ti = (ci * SSM_CHUNK) // tq
            tref_ref[0, ti:ti + 1, :] = tile_ref
        c = out - tile_ref
        c_hi = c.astype(BF16)
        r1 = c - c_hi.astype(F32)
        c_mid = r1.astype(BF16)
        c_lo = (r1 - c_mid.astype(F32)).astype(BF16)
        for o_ref, base, qk_lo in ((qa_ref, 0, 0), (ka_ref, 3, GROUP_WIDTH)):
            aug = jnp.dot(qk_ref[rows, qk_lo:qk_lo + GROUP_WIDTH], sel_ref[...], preferred_element_type=F32)
            for term, cc in enumerate((c_hi, c_mid, c_lo)):
                aug = aug + jnp.dot(cc, place_ref[base + term], preferred_element_type=F32)
            o_ref[rows, :] = (aug + const_ref[base // 3:base // 3 + 1, :]).astype(o_ref.dtype)


def _fox_placement():
    sel = np.zeros((GROUP_WIDTH, HEAD_PAD), np.float32)
    place = np.zeros((6, LANES, HEAD_PAD), np.float32)
    const = np.zeros((8, HEAD_PAD), np.float32)
    for h in range(N_HEADS):
        for d in range(HEAD_DIM):
            sel[h * HEAD_DIM + d, h * LANES + d] = 1.0
        a0 = h * LANES + AUG_LANE
        for term in range(3):
            place[term, COL_CUMF + h, a0 + term] = 1.0
            place[3 + term, COL_CUMF + h, a0 + 3 + term] = -1.0
            const[0, a0 + 3 + term] = 1.0
            const[1, a0 + term] = 1.0
    return jnp.asarray(sel, BF16), jnp.asarray(place, BF16), jnp.asarray(const, F32)


def _scalar_prep(misc, params, qk, batch, seq, tq):
    sel, place, const = _fox_placement()
    full = lambda a: pl.BlockSpec(a.shape, lambda b: (0,) * a.ndim)
    return pl.pallas_call(
        functools.partial(_scalar_prep_kernel, tq=tq),
        grid=(batch,),
        in_specs=[pl.BlockSpec((seq, LANES), lambda b: (b, 0)),
                  pl.BlockSpec((8, LANES), lambda b: (0, 0)),
                  pl.BlockSpec((seq, 2 * GROUP_WIDTH), lambda b: (b, 0)),
                  full(sel), full(place), full(const)],
        out_specs=[pl.BlockSpec((seq, LANES), lambda b: (b, 0)),
                   pl.BlockSpec((1, N_SCALAR_ROWS, seq), lambda b: (b, 0, 0)),
                   pl.BlockSpec((seq, HEAD_PAD), lambda b: (b, 0)),
                   pl.BlockSpec((seq, HEAD_PAD), lambda b: (b, 0)),
                   pl.BlockSpec((1, 8, LANES), lambda b: (b, 0, 0))],
        out_shape=[jax.ShapeDtypeStruct((batch * seq, LANES), F32),
                   jax.ShapeDtypeStruct((batch, N_SCALAR_ROWS, seq), F32),
                   jax.ShapeDtypeStruct((batch * seq, HEAD_PAD), BF16),
                   jax.ShapeDtypeStruct((batch * seq, HEAD_PAD), BF16),
                   jax.ShapeDtypeStruct((batch, 8, LANES), F32)],
        compiler_params=_cparams("parallel"),
        name="scalar_prep",
    )(misc, params, qk, sel, place, const)


def _conv_mixer_kernel(p_ref, w_ref, o_ref):
    gw = GROUP_WIDTH
    b_gate = p_ref[:, 0:gw].astype(F32)
    cv = p_ref[:, gw:2 * gw].astype(F32) * p_ref[:, 2 * gw:3 * gw].astype(F32)
    acc = cv * w_ref[CONV_A_WIDTH - 1:CONV_A_WIDTH, :]
    for k in range(1, CONV_A_WIDTH):
        acc = acc + _shift_rows(cv, k) * w_ref[CONV_A_WIDTH - 1 - k:CONV_A_WIDTH - k, :]
    o_ref[...] = (b_gate * acc).astype(o_ref.dtype)


def _conv_mixer(pa, w, batch, seq):
    return pl.pallas_call(
        _conv_mixer_kernel,
        grid=(batch,),
        in_specs=[pl.BlockSpec((seq, 3 * GROUP_WIDTH), lambda b: (b, 0)),
                  pl.BlockSpec((8, GROUP_WIDTH), lambda b: (0, 0))],
        out_specs=pl.BlockSpec((seq, GROUP_WIDTH), lambda b: (b, 0)),
        out_shape=jax.ShapeDtypeStruct((batch * seq, GROUP_WIDTH), BF16),
        compiler_params=_cparams("parallel"),
        name="conv_mixer",
    )(pa, w)


def _pair_lanes(col, base, shape):
    lane = _lane_iota(shape)
    return jnp.where(lane < HEAD_DIM, col[:, base:base + 1], col[:, base + 1:base + 2])


def _ssd_kernel(p_ref, col_ref, row_ref, cw_ref, par_ref, o_ref, u_ref):
    s = p_ref.shape[0]
    q = SSM_CHUNK
    gw = GROUP_WIDTH
    xbc = p_ref[:, gw:3 * gw].astype(F32)
    acc = xbc * cw_ref[SSM_CONV - 1:SSM_CONV, :]
    for k in range(1, SSM_CONV):
        acc = acc + _shift_rows(xbc, k) * cw_ref[SSM_CONV - 1 - k:SSM_CONV - k, :]
    u_ref[...] = _silu(acc + cw_ref[SSM_CONV:SSM_CONV + 1, :])

    d_skip = par_ref[0:1, :]
    norm_g = par_ref[1:2, :]
    lane_q = _lane_iota((q, LANES))
    low = lane_q < HEAD_DIM
    tri = _row_iota((q, q)) >= _lane_iota((q, q))

    def chunk(ci, states):
        rows = pl.ds(ci * q, q)
        u = u_ref[rows, :]
        col = col_ref[rows, :]
        bm = u[:, gw:gw + LANES]
        cm = u[:, gw + LANES:gw + 2 * LANES]
        z = p_ref[rows, 0:gw].astype(F32)
        new_states = []
        ys = []
        for g in range(2):
            sel = low if g == 0 else jnp.logical_not(low)
            cg = jnp.where(sel, cm, 0.0).astype(BF16)
            bg = jnp.where(sel, bm, 0.0)
            gmat = lax.dot_general(cg, bm.astype(BF16), (((1,), (1,)), ((), ())),
                                   preferred_element_type=F32)
            xs = u[:, g * LANES:(g + 1) * LANES]
            dt2 = _pair_lanes(col, COL_DT + 2 * g, (q, LANES))
            ac2 = _pair_lanes(col, COL_ACUM + 2 * g, (q, LANES))
            xdt = xs * dt2
            xdt_b = xdt.astype(BF16)
            st = states[g]
            y_off = jnp.dot(cg, st.astype(BF16), preferred_element_type=F32) * jnp.exp(ac2)
            halves = []
            for hh in range(2):
                h = 2 * g + hh
                ac_col = col[:, COL_ACUM + h:COL_ACUM + h + 1]
                ac_row = row_ref[0, COL_ACUM + h:COL_ACUM + h + 1, rows]
                decay = jnp.exp(jnp.where(tri, ac_col - ac_row, -1e30))
                mm = (gmat * decay).astype(BF16)
                halves.append(jnp.dot(mm, xdt_b, preferred_element_type=F32))
            y = jnp.where(low, halves[0], halves[1]) + y_off + d_skip[:, g * LANES:(g + 1) * LANES] * xs
            ys.append(y)
            ac_last = ac2[q - 1:q, :]
            w_end = jnp.exp(ac_last - ac2)
            xw = (xdt * w_end).astype(BF16)
            upd = jnp.dot(bg.T.astype(BF16), xw, preferred_element_type=F32)
            new_states.append(st * jnp.exp(ac_last) + upd)
        yfull = jnp.concatenate(ys, axis=1) * _silu(z)
        o_ref[rows, :] = _rms(yfull, norm_g).astype(o_ref.dtype)
        return tuple(new_states)

    init = (jnp.zeros((LANES, LANES), F32), jnp.zeros((LANES, LANES), F32))
    states = init
    for ci in range(s // q):
        states = chunk(ci, states)


def _ssd_mixer(pc, col, rows, conv_wb, par, batch, seq):
    gw = GROUP_WIDTH
    return pl.pallas_call(
        _ssd_kernel,
        grid=(batch,),
        in_specs=[pl.BlockSpec((seq, 3 * gw), lambda b: (b, 0)),
                  pl.BlockSpec((seq, LANES), lambda b: (b, 0)),
                  pl.BlockSpec((1, N_SCALAR_ROWS, seq), lambda b: (b, 0, 0)),
                  pl.BlockSpec((8, 2 * gw), lambda b: (0, 0)),
                  pl.BlockSpec((8, gw), lambda b: (0, 0))],
        out_specs=pl.BlockSpec((seq, gw), lambda b: (b, 0)),
        out_shape=jax.ShapeDtypeStruct((batch * seq, gw), BF16),
        scratch_shapes=[pltpu.VMEM((seq, 2 * gw), F32)],
        compiler_params=_cparams("parallel"),
        name="ssd_mixer",
    )(pc, col, rows, conv_wb, par)


def _mla_prep_kernel(pd_ref, m_ref, m2_ref, cos_ref, sin_ref, nq_ref, nkv_ref,
                     wq_ref, wqs_ref, wk_ref, wvt_ref, q_ref, k_ref, vt_ref):
    cq = _rms(pd_ref[:, 0:MLA_Q_LORA].astype(F32), nq_ref[...]).astype(BF16)
    ckv = _rms(pd_ref[:, MLA_Q_LORA:MLA_Q_LORA + MLA_KV_LORA].astype(F32), nkv_ref[...]).astype(BF16)
    cos = cos_ref[...]
    sin = sin_ref[...]
    cos4 = jnp.concatenate([cos] * N_HEADS, axis=1)
    sin4 = jnp.concatenate([sin] * N_HEADS, axis=1)
    scale = (MLA_NOPE + MLA_ROPE) ** -0.5 * LOG2E
    q = jnp.dot(cq, wq_ref[...], preferred_element_type=F32)
    qs = jnp.dot(cq, wqs_ref[...], preferred_element_type=F32)
    q_ref[...] = ((q * cos4 + qs * sin4) * scale).astype(q_ref.dtype)
    lane = _lane_iota(cos.shape)
    rope = (lane >= MISC_ROPE) & (lane < MISC_ROPE + MLA_ROPE)
    kr = jnp.where(rope, m_ref[...] * cos + m2_ref[...] * sin, 0.0)
    k = jnp.dot(ckv, wk_ref[...], preferred_element_type=F32)
    k_ref[...] = (k + jnp.concatenate([kr] * N_HEADS, axis=1)).astype(k_ref.dtype)
    vt_ref[...] = lax.dot_general(wvt_ref[...], ckv, (((1,), (1,)), ((), ())),
                                  preferred_element_type=F32).astype(vt_ref.dtype)


def _mla_prep(pd, misc, misc2, cos, sin, nq, nkv, wq, wqs, wk, wv, tm):
    t = pd.shape[0]
    hp = N_HEADS * LANES
    full = lambda a: pl.BlockSpec(a.shape, lambda i: (0, 0))
    tile = lambda w: pl.BlockSpec((tm, w), lambda i: (i, 0))
    return pl.pallas_call(
        _mla_prep_kernel,
        grid=(t // tm,),
        in_specs=[tile(MLA_Q_LORA + MLA_KV_LORA), tile(LANES), tile(LANES), tile(LANES), tile(LANES),
                  full(nq), full(nkv), full(wq), full(wqs), full(wk), full(wv)],
        out_specs=[tile(hp), tile(hp), pl.BlockSpec((GROUP_WIDTH, tm), lambda i: (0, i))],
        out_shape=[jax.ShapeDtypeStruct((t, hp), BF16), jax.ShapeDtypeStruct((t, hp), BF16),
                   jax.ShapeDtypeStruct((GROUP_WIDTH, t), BF16)],
        compiler_params=_cparams("parallel"),
        name="mla_prep",
    )(pd, misc, misc2, cos, sin, nq, nkv, wq, wqs, wk, wv)


def _attn_kernel(*refs, fox, tq):
    if fox:
        tref_ref, q_ref, k_ref, vt_ref, o_ref = refs
    else:
        q_ref, k_ref, vt_ref, o_ref = refs
        tref_ref = None
    b = pl.program_id(0)
    i = pl.program_id(1)
    key = _row_iota((tq, tq))
    qry = _lane_iota((tq, tq))
    if fox:
        allowed = key <= qry
    else:
        shift = int(math.log2(MLA_CHUNK))
        allowed = (key >> shift) <= (qry >> shift)
    qs = [q_ref[:, h * LANES:(h + 1) * LANES] for h in range(N_HEADS)]

    def step(j, masked, carry):
        rk = pl.ds(pl.multiple_of(j * tq, tq), tq)
        scores = [lax.dot_general(k_ref[rk, h * LANES:(h + 1) * LANES], qs[h], (((1,), (1,)), ((), ())),
                                  preferred_element_type=F32) for h in range(N_HEADS)]
        probs = []
        for h in range(N_HEADS):
            m, l, _ = carry[h]
            s = scores[h]
            if masked:
                s = jnp.where(allowed, s, -1e30)
            delta = (tref_ref[b, i, h] - tref_ref[b, j, h]) if fox else 0.0
            m_new = jnp.maximum(m, jnp.max(s, axis=0, keepdims=True) + delta)
            alpha = jnp.exp2(m - m_new)
            p = jnp.exp2(s - (m_new - delta))
            l_new = alpha * l + jnp.sum(p, axis=0, keepdims=True)
            probs.append((m_new, l_new, alpha, p.astype(BF16)))
        new = []
        for h in range(N_HEADS):
            pair = h // 2
            m_new, l_new, alpha, p = probs[h]
            pv = jnp.dot(vt_ref[pair * LANES:(pair + 1) * LANES, rk], p, preferred_element_type=F32)
            new.append((m_new, l_new, alpha * carry[h][2] + pv))
        return tuple(new)

    init = tuple((jnp.full((1, tq), -1e30, F32), jnp.zeros((1, tq), F32), jnp.zeros((LANES, tq), F32))
                 for _ in range(N_HEADS))
    carry = lax.fori_loop(0, i, lambda j, c: step(j, False, c), init)
    carry = step(i, True, carry)
    outs = [acc / l for (_, l, acc) in carry]
    top = _row_iota((LANES, tq)) < HEAD_DIM
    o_t = jnp.concatenate([jnp.where(top, outs[0], outs[1]), jnp.where(top, outs[2], outs[3])], axis=0)
    o_ref[...] = o_t.T.astype(o_ref.dtype)


def _attention(q, k, vt, tref, batch, seq, tq, name):
    nq = seq // tq
    fox = tref is not None
    kern = functools.partial(_attn_kernel, fox=fox, tq=tq)
    grid_spec = pltpu.PrefetchScalarGridSpec(
        num_scalar_prefetch=1 if fox else 0,
        grid=(batch, nq),
        in_specs=[pl.BlockSpec((tq, HEAD_PAD), lambda b, i, *_: (b * nq + i, 0)),
                  pl.BlockSpec((seq, HEAD_PAD), lambda b, i, *_: (b, 0)),
                  pl.BlockSpec((GROUP_WIDTH, seq), lambda b, i, *_: (0, b))],
        out_specs=pl.BlockSpec((tq, GROUP_WIDTH), lambda b, i, *_: (b * nq + i, 0)),
    )
    args = ((tref,) if fox else ()) + (q, k, vt)
    return pl.pallas_call(
        kern,
        grid_spec=grid_spec,
        out_shape=jax.ShapeDtypeStruct((batch * seq, GROUP_WIDTH), BF16),
        compiler_params=_cparams("parallel", "arbitrary"),
        name=name,
    )(*args)


def _outproj_kernel(x_ref, ya, yb, yc, yd, w_ref, g_ref, wr_ref, br_ref,
                    x2_ref, h2_ref, rrow_ref, cnt_ref, *, tm, moe_tile):
    y = jnp.concatenate([ya[...], yb[...], yc[...], yd[...]], axis=1)
    x2 = x_ref[...] + jnp.dot(y, w_ref[...], preferred_element_type=F32)
    x2_ref[...] = x2
    h2 = _rms(x2, g_ref[...])
    h2_ref[...] = h2.astype(h2_ref.dtype)
    h_hi = h2.astype(BF16)
    h_lo = (h2 - h_hi.astype(F32)).astype(BF16)
    part = jnp.dot(h_hi, wr_ref[...], preferred_element_type=F32)
    logits = (part[:, 0:LANES] + part[:, LANES:2 * LANES]
              + jnp.dot(h_lo, wr_ref[:, 0:LANES], preferred_element_type=F32) + br_ref[...])
    lt = logits.T
    row = _row_iota(lt.shape)
    neg = -1e30
    big = 1 << 20
    gmask = (row >= N_EXPERTS) & (row < N_EXPERTS + N_EXPERT_GROUPS)
    gl = jnp.where(gmask, lt, neg)
    gmax = jnp.max(gl, axis=0, keepdims=True)
    gsum = jnp.sum(jnp.where(gmask, jnp.exp(gl - gmax), 0.0), axis=0, keepdims=True)
    g_w = 1.0 / gsum
    g_idx = jnp.min(jnp.where(gmask & (gl == gmax), row, big), axis=0, keepdims=True) - N_EXPERTS
    emask = (row < N_EXPERTS) & ((row >> int(math.log2(EXPERTS_PER_GROUP))) == g_idx)
    el = jnp.where(emask, lt, neg)
    e1v = jnp.max(el, axis=0, keepdims=True)
    esum = jnp.sum(jnp.where(emask, jnp.exp(el - e1v), 0.0), axis=0, keepdims=True)
    i1 = jnp.min(jnp.where(emask & (el == e1v), row, big), axis=0, keepdims=True)
    el2 = jnp.where(row == i1, neg, el)
    e2v = jnp.max(el2, axis=0, keepdims=True)
    i2 = jnp.min(jnp.where(emask & (row != i1) & (el2 == e2v), row, big), axis=0, keepdims=True)
    p1 = 1.0 / esum
    p2 = jnp.exp(e2v - e1v) / esum
    w1 = g_w * (p1 / (p1 + p2))
    w2 = g_w * (p2 / (p1 + p2))
    out_row = _row_iota(rrow_ref.shape)
    rrow_ref[...] = jnp.where(out_row == 0, i1.astype(F32),
                              jnp.where(out_row == 1, i2.astype(F32),
                                        jnp.where(out_row == 2, w1, jnp.where(out_row == 3, w2, 0.0))))
    step = pl.program_id(0)

    @pl.when(step == 0)
    def _():
        cnt_ref[...] = jnp.zeros_like(cnt_ref)

    chosen = jnp.where((row == i1) | (row == i2), 1.0, 0.0)
    cnt = cnt_ref[...]
    lane = _lane_iota(cnt.shape)
    tiles = tm // moe_tile
    for k in range(tiles):
        n = jnp.sum(chosen[0:N_EXPERTS, k * moe_tile:(k + 1) * moe_tile], axis=-1, keepdims=True)
        cnt = jnp.where(lane == step * tiles + k, n, cnt)
    cnt_ref[...] = cnt


def _outproj(x, ya, yb, yc, yd, w, g, wr, br, tm, moe_tile):
    t = x.shape[0]
    full = lambda a: pl.BlockSpec(a.shape, lambda i: (0, 0))
    tile = lambda wd: pl.BlockSpec((tm, wd), lambda i: (i, 0))
    return pl.pallas_call(
        functools.partial(_outproj_kernel, tm=tm, moe_tile=moe_tile),
        grid=(t // tm,),
        in_specs=[tile(D_MODEL)] + [tile(GROUP_WIDTH)] * 4 + [full(w), full(g), full(wr), full(br)],
        out_specs=[tile(D_MODEL), tile(D_MODEL), pl.BlockSpec((8, tm), lambda i: (0, i)),
                   pl.BlockSpec((N_EXPERTS, LANES), lambda i: (0, 0))],
        out_shape=[jax.ShapeDtypeStruct((t, D_MODEL), F32), jax.ShapeDtypeStruct((t, D_MODEL), BF16),
                   jax.ShapeDtypeStruct((8, t), F32), jax.ShapeDtypeStruct((N_EXPERTS, LANES), F32)],
        compiler_params=_cparams("arbitrary"),
        name="outproj_router",
    )(x, ya, yb, yc, yd, w, g, wr, br)


def _positions_kernel(r_ref, dest_ref, meta_ref, cnt_ref, carry_ref, start_ref, *, tm, nbp):
    phase = pl.program_id(0)
    i = pl.program_id(1)
    e_iota = _row_iota((N_EXPERTS, tm))
    e0 = r_ref[0:1, :].astype(I32)
    e1 = r_ref[1:2, :].astype(I32)
    oh0 = e_iota == e0
    oh1 = e_iota == e1
    oh = jnp.where(oh0 | oh1, 1.0, 0.0)

    @pl.when((phase == 0) & (i == 0))
    def _():
        cnt_ref[...] = jnp.zeros_like(cnt_ref)

    @pl.when(phase == 0)
    def _():
        cnt_ref[...] += jnp.sum(oh, axis=-1, keepdims=True)

    @pl.when((phase == 1) & (i == 0))
    def _():
        cnt = cnt_ref[...]
        padded = jnp.floor((cnt + (MOE_ROWS - 1)) * (1.0 / MOE_ROWS)) * MOE_ROWS
        tril = jnp.where(_row_iota((N_EXPERTS, N_EXPERTS)) >= _lane_iota((N_EXPERTS, N_EXPERTS)), 1.0, 0.0)
        pend = jnp.dot(tril, padded, preferred_element_type=F32, precision=lax.Precision.HIGHEST)
        pstart = pend - padded
        start_ref[...] = pstart
        carry_ref[...] = jnp.zeros_like(carry_ref)
        pend_b = jnp.concatenate([pend] * (nbp // LANES), axis=1)
        vend_b = jnp.concatenate([pstart + cnt] * (nbp // LANES), axis=1)
        b0 = (_lane_iota((N_EXPERTS, nbp)) * MOE_ROWS).astype(F32)
        bexp = jnp.sum(jnp.where(pend_b <= b0, 1.0, 0.0), axis=0, keepdims=True)
        bexp = jnp.minimum(bexp, N_EXPERTS - 1.0)
        is_e = _row_iota((N_EXPERTS, nbp)).astype(F32) == bexp
        vend = jnp.sum(jnp.where(is_e, vend_b, 0.0), axis=0, keepdims=True)
        nvalid = jnp.clip(vend - b0[0:1], 0.0, float(MOE_ROWS))
        total = jnp.max(pend_b, axis=0, keepdims=True) * (1.0 / MOE_ROWS)
        row = _row_iota((8, nbp))
        meta = jnp.where(row == 0, bexp, jnp.where(row == 1, nvalid, jnp.where(row == 2, total, 0.0)))
        meta_ref[...] = meta.astype(I32)

    @pl.when(phase == 1)
    def _():
        su = jnp.where(_row_iota((tm, tm)) < _lane_iota((tm, tm)), 1.0, 0.0).astype(BF16)
        before = jnp.dot(oh.astype(BF16), su, preferred_element_type=F32)
        base = start_ref[:, 0:1] + carry_ref[:, 0:1] + before
        d0 = jnp.sum(jnp.where(oh0, base, 0.0), axis=0, keepdims=True)
        d1 = jnp.sum(jnp.where(oh1, base, 0.0), axis=0, keepdims=True)
        dest_ref[0, 0:1, :] = d0.astype(I32)
        dest_ref[0, 1:2, :] = d1.astype(I32)
        carry_ref[...] += jnp.sum(oh, axis=-1, keepdims=True)


def _positions(rrow, tm, nbp):
    t = rrow.shape[1]
    nt = t // tm
    kern = functools.partial(_positions_kernel, tm=tm, nbp=nbp)
    return pl.pallas_call(
        kern,
        grid=(2, nt),
        in_specs=[pl.BlockSpec((8, tm), lambda p, i: (0, i))],
        out_specs=[pl.BlockSpec((1, 2, tm), lambda p, i: (i * p, 0, 0)),
                   pl.BlockSpec((8, nbp), lambda p, i: (0, 0))],
        out_shape=[jax.ShapeDtypeStruct((nt, 2, tm), I32), jax.ShapeDtypeStruct((8, nbp), I32)],
        scratch_shapes=[pltpu.VMEM((N_EXPERTS, LANES), F32)] * 3,
        compiler_params=_cparams("arbitrary", "arbitrary"),
        name="moe_positions",
    )(rrow)


def _dispatch_kernel(meta_ref, dest_ref, h_ref, xs_ref, zero_ref, sem, zsem, *, tm, nb):
    i = pl.program_id(0)

    @pl.when(i == 0)
    def _():
        zero_ref[...] = jnp.zeros_like(zero_ref)
        n_used = meta_ref[2, 0]

        def zcopy(b):
            return pltpu.make_async_copy(zero_ref, xs_ref.at[pl.ds(b * MOE_ROWS, MOE_ROWS)], zsem)

        def needs(b):
            return (b < n_used) & (meta_ref[1, b] < MOE_ROWS)

        def start(b, c):
            @pl.when(needs(b))
            def _():
                zcopy(b).start()
            return c

        def wait(b, c):
            @pl.when(needs(b))
            def _():
                zcopy(b).wait()
            return c

        lax.fori_loop(0, nb, start, 0)
        lax.fori_loop(0, nb, wait, 0)

    def copy(t, k):
        return pltpu.make_async_copy(h_ref.at[pl.ds(t, 1)], xs_ref.at[pl.ds(dest_ref[0, k, t], 1)], sem)

    def start(t, c):
        copy(t, 0).start()
        copy(t, 1).start()
        return c

    def wait(t, c):
        copy(t, 0).wait()
        copy(t, 1).wait()
        return c

    lax.fori_loop(0, tm, start, 0)
    lax.fori_loop(0, tm, wait, 0)


def _dispatch(meta, dest, h2, tm, nb):
    t = h2.shape[0]
    kern = functools.partial(_dispatch_kernel, tm=tm, nb=nb)
    grid_spec = pltpu.PrefetchScalarGridSpec(
        num_scalar_prefetch=1,
        grid=(t // tm,),
        in_specs=[pl.BlockSpec((1, 2, tm), lambda i, m: (i, 0, 0), memory_space=pltpu.SMEM),
                  pl.BlockSpec((tm, D_MODEL), lambda i, m: (i, 0))],
        out_specs=pl.BlockSpec(memory_space=pl.ANY),
        scratch_shapes=[pltpu.VMEM((MOE_ROWS, D_MODEL), F32),
                        pltpu.SemaphoreType.DMA, pltpu.SemaphoreType.DMA],
    )
    return pl.pallas_call(
        kern,
        grid_spec=grid_spec,
        out_shape=jax.ShapeDtypeStruct((nb * MOE_ROWS, D_MODEL), F32),
        compiler_params=_cparams("arbitrary"),
        name="moe_dispatch",
    )(meta, dest, h2)


def _expert_kernel(meta_ref, x_ref, wg_ref, wu_ref, wd_ref, o_ref):
    b = pl.program_id(0)

    @pl.when(b < meta_ref[2, 0])
    def _():
        x = x_ref[...].astype(BF16)
        gate = jnp.dot(x, wg_ref[0, 0].astype(BF16), preferred_element_type=F32)
        up = jnp.dot(x, wu_ref[0, 0].astype(BF16), preferred_element_type=F32)
        act = (_silu(gate) * up).astype(BF16)
        o_ref[...] = jnp.dot(act, wd_ref[0, 0].astype(BF16), preferred_element_type=F32)


def _experts(meta, xs, wg, wu, wd, layer, nb):
    def blk(b, m):
        return (jnp.minimum(b, m[2, 0] - 1), 0)

    def wblk(b, m):
        return (layer, m[0, jnp.minimum(b, m[2, 0] - 1)], 0, 0)

    grid_spec = pltpu.PrefetchScalarGridSpec(
        num_scalar_prefetch=1,
        grid=(nb,),
        in_specs=[pl.BlockSpec((MOE_ROWS, D_MODEL), blk),
                  pl.BlockSpec((1, 1, D_MODEL, EXPERT_FF), wblk),
                  pl.BlockSpec((1, 1, D_MODEL, EXPERT_FF), wblk),
                  pl.BlockSpec((1, 1, EXPERT_FF, D_MODEL), wblk)],
        out_specs=pl.BlockSpec((MOE_ROWS, D_MODEL), blk),
    )
    return pl.pallas_call(
        _expert_kernel,
        grid_spec=grid_spec,
        out_shape=jax.ShapeDtypeStruct((nb * MOE_ROWS, D_MODEL), F32),
        compiler_params=_cparams("arbitrary"),
        name="moe_experts",
    )(meta, xs, wg, wu, wd)


def _combine_kernel(dest_ref, x_ref, r_ref, g_ref, ys_ref, o_ref, buf_ref, sem, *, tm, final):
    def copy(t, k):
        return pltpu.make_async_copy(ys_ref.at[pl.ds(dest_ref[0, k, t], 1)],
                                     buf_ref.at[k, pl.ds(t, 1)], sem)

    def start(t, c):
        copy(t, 0).start()
        copy(t, 1).start()
        return c

    def wait(t, c):
        copy(t, 0).wait()
        copy(t, 1).wait()
        return c

    lax.fori_loop(0, tm, start, 0)
    lax.fori_loop(0, tm, wait, 0)
    w0 = r_ref[:, 2:3]
    w1 = r_ref[:, 3:4]
    x = x_ref[...] + (buf_ref[0] * w0 + buf_ref[1] * w1)
    o_ref[...] = _rms(x, g_ref[...]) if final else x


def _combine(dest, x2, rcol, g, ys, tm, final):
    t = x2.shape[0]
    kern = functools.partial(_combine_kernel, tm=tm, final=final)
    return pl.pallas_call(
        kern,
        grid=(t // tm,),
        in_specs=[pl.BlockSpec((1, 2, tm), lambda i: (i, 0, 0), memory_space=pltpu.SMEM),
                  pl.BlockSpec((tm, D_MODEL), lambda i: (i, 0)),
                  pl.BlockSpec((tm, LANES), lambda i: (i, 0)),
                  pl.BlockSpec((1, D_MODEL), lambda i: (0, 0)),
                  pl.BlockSpec(memory_space=pl.ANY)],
        out_specs=pl.BlockSpec((tm, D_MODEL), lambda i: (i, 0)),
        out_shape=jax.ShapeDtypeStruct((t, D_MODEL), F32),
        scratch_shapes=[pltpu.VMEM((2, tm, D_MODEL), F32), pltpu.SemaphoreType.DMA],
        compiler_params=_cparams("arbitrary"),
        name="moe_combine",
    )(dest, x2, rcol, g, ys)


MOE_TILE = 256
CHUNK = 8
LOCAL_ROWS = 2 * MOE_TILE + 256
PACKED = D_MODEL // 2
XS_WIDTH = PACKED
U32 = jnp.uint32


def _pack_bf16_pairs(x, exact=False):
    if not exact:
        x = x.astype(BF16).astype(F32)
    half = x.shape[1] // 2
    lo = lax.bitcast_convert_type(x[:, :half], U32)
    hi = lax.bitcast_convert_type(x[:, half:], U32)
    return hi | (lo >> 16)


def _unpack_bf16_pairs(words):
    lo = lax.bitcast_convert_type(words << 16, F32)
    hi = lax.bitcast_convert_type(words & U32(0xFFFF0000), F32)
    return jnp.concatenate([lo, hi], axis=1).astype(BF16)
TAB_CHUNKS, TAB_LOCAL, TAB_GLOBAL, TAB_TOTAL, TAB_EXPERT = 0, 1, 2, 3, 4
N_TABS = 5


def _route_kernel(r_ref, cnt_ref, lrow_ref, lcol_ref, tab_ref, ctab_ref, meta_ref,
                  loff_ref, goff_ref, n8_ref, *, tm, nbp):
    i = pl.program_id(0)
    e_iota = _row_iota((N_EXPERTS, tm))
    oh0 = e_iota == r_ref[0:1, :].astype(I32)
    oh1 = e_iota == r_ref[1:2, :].astype(I32)
    oh = jnp.where(oh0 | oh1, 1.0, 0.0)
    tile_lane = _lane_iota((N_EXPERTS, LANES)) == i
    hi = lax.Precision.HIGHEST

    @pl.when(i == 0)
    def _():
        cnt = cnt_ref[...]
        n8 = jnp.floor((cnt + (CHUNK - 1)) * (1.0 / CHUNK)) * CHUNK
        er = _row_iota((N_EXPERTS, N_EXPERTS))
        ec = _lane_iota((N_EXPERTS, N_EXPERTS))
        below = jnp.where(er > ec, 1.0, 0.0)
        loff = jnp.dot(below, n8, preferred_element_type=F32, precision=hi)
        rows_e = jnp.sum(n8, axis=-1, keepdims=True) + jnp.zeros_like(n8)
        padded = jnp.floor((rows_e + (MOE_ROWS - 1)) * (1.0 / MOE_ROWS)) * MOE_ROWS
        e_start = jnp.dot(below, padded, preferred_element_type=F32, precision=hi)
        tr = _row_iota((LANES, LANES))
        tc = _lane_iota((LANES, LANES))
        earlier = jnp.where(tr < tc, 1.0, 0.0)
        goff = e_start + jnp.dot(n8, earlier, preferred_element_type=F32, precision=hi)
        loff_ref[...] = loff
        goff_ref[...] = goff
        n8_ref[...] = n8
        tab_ref[TAB_CHUNKS] = (n8 * (1.0 / CHUNK)).astype(I32)
        tab_ref[TAB_LOCAL] = loff.astype(I32)
        tab_ref[TAB_GLOBAL] = goff.astype(I32)
        tab_ref[TAB_TOTAL] = (jnp.sum(n8, axis=0, keepdims=True) * (1.0 / CHUNK)
                              + jnp.zeros_like(n8)).astype(I32)
        lane_t = _lane_iota(n8.shape)
        tab_ref[TAB_EXPERT] = jnp.where(lane_t == 0, e_start, jnp.where(lane_t == 1, rows_e, 0.0)).astype(I32)
        reps = nbp // LANES
        pend_b = jnp.concatenate([e_start + padded] * reps, axis=1)
        vend_b = jnp.concatenate([e_start + rows_e] * reps, axis=1)
        b0 = (_lane_iota((N_EXPERTS, nbp)) * MOE_ROWS).astype(F32)
        bexp = jnp.sum(jnp.where(pend_b <= b0, 1.0, 0.0), axis=0, keepdims=True)
        bexp = jnp.minimum(bexp, N_EXPERTS - 1.0)
        is_e = _row_iota((N_EXPERTS, nbp)).astype(F32) == bexp
        vend = jnp.sum(jnp.where(is_e, vend_b, 0.0), axis=0, keepdims=True)
        nvalid = jnp.clip(vend - b0[0:1], 0.0, float(MOE_ROWS))
        total = jnp.max(pend_b, axis=0, keepdims=True) * (1.0 / MOE_ROWS)
        row = _row_iota((8, nbp))
        meta = jnp.where(row == 0, bexp, jnp.where(row == 1, nvalid, jnp.where(row == 2, total, 0.0)))
        meta_ref[...] = meta.astype(I32)

    su = jnp.where(_row_iota((tm, tm)) < _lane_iota((tm, tm)), 1.0, 0.0).astype(BF16)
    before = jnp.dot(oh.astype(BF16), su, preferred_element_type=F32)
    base = jnp.sum(jnp.where(tile_lane, loff_ref[...], 0.0), axis=-1, keepdims=True) + before
    d0 = jnp.sum(jnp.where(oh0, base, 0.0), axis=0, keepdims=True)
    d1 = jnp.sum(jnp.where(oh1, base, 0.0), axis=0, keepdims=True)
    lrow_ref[0, 0:1, :] = d0.astype(I32)
    lrow_ref[0, 1:2, :] = d1.astype(I32)
    pick_tile = lambda ref: jnp.sum(jnp.where(tile_lane, ref[...], 0.0), axis=-1, keepdims=True)
    nch = pick_tile(n8_ref) * (1.0 / CHUNK)
    incl = jnp.where(_row_iota((N_EXPERTS, N_EXPERTS)) >= _lane_iota((N_EXPERTS, N_EXPERTS)), 1.0, 0.0)
    cend = jnp.dot(incl, nch + jnp.zeros((N_EXPERTS, LANES), F32), preferred_element_type=F32, precision=hi)
    cstart = cend - nch
    cidx = _lane_iota((N_EXPERTS, LANES)).astype(F32)
    mine = (cidx >= cstart) & (cidx < cend)
    step_rows = (cidx - cstart) * CHUNK
    lo = jnp.sum(jnp.where(mine, pick_tile(loff_ref) + step_rows, 0.0), axis=0, keepdims=True)
    go = jnp.sum(jnp.where(mine, pick_tile(goff_ref) + step_rows, 0.0), axis=0, keepdims=True)
    ctab_ref[0, 0:1, :] = lo.astype(I32)
    ctab_ref[0, 1:2, :] = go.astype(I32)
    row = _row_iota((LANES, tm))
    lcol_ref[...] = jnp.where(row == 0, d0, jnp.where(row == 1, d1,
                              jnp.where(row == 2, r_ref[2:3, :], jnp.where(row == 3, r_ref[3:4, :], 0.0)))).T


def _route(rrow, cnt, tm, nbp):
    t = rrow.shape[1]
    nt = t // tm
    kern = functools.partial(_route_kernel, tm=tm, nbp=nbp)
    return pl.pallas_call(
        kern,
        grid=(nt,),
        in_specs=[pl.BlockSpec((8, tm), lambda i: (0, i)),
                  pl.BlockSpec((N_EXPERTS, LANES), lambda i: (0, 0))],
        out_specs=[pl.BlockSpec((1, 2, tm), lambda i: (i, 0, 0)),
                   pl.BlockSpec((tm, LANES), lambda i: (i, 0)),
                   pl.BlockSpec((N_TABS, N_EXPERTS, LANES), lambda i: (0, 0, 0)),
                   pl.BlockSpec((1, 2, LANES), lambda i: (i, 0, 0)),
                   pl.BlockSpec((8, nbp), lambda i: (0, 0))],
        out_shape=[jax.ShapeDtypeStruct((nt, 2, tm), I32), jax.ShapeDtypeStruct((t, LANES), F32),
                   jax.ShapeDtypeStruct((N_TABS, N_EXPERTS, LANES), I32), jax.ShapeDtypeStruct((nt, 2, LANES), I32),
                   jax.ShapeDtypeStruct((8, nbp), I32)],
        scratch_shapes=[pltpu.VMEM((N_EXPERTS, LANES), F32)] * 3,
        compiler_params=_cparams("arbitrary"),
        name="moe_route",
    )(rrow, cnt)


def _chunk_copies(tabs, i, local_ref, global_ref, sem, to_global, action):
    tab_ref, ctab_ref = tabs
    if action == "wait":
        lsl = local_ref.at[pl.ds(0, CHUNK)]
        gsl = global_ref.at[pl.ds(0, CHUNK)]
        cp = pltpu.make_async_copy(lsl, gsl, sem) if to_global else pltpu.make_async_copy(gsl, lsl, sem)

        def one(c, c1):
            cp.wait()
            return c1

        lax.fori_loop(0, tab_ref[TAB_TOTAL, 0, i], one, 0)
        return

    total = tab_ref[TAB_TOTAL, 0, i]

    def start(c, priority):
        lo = ctab_ref[i, 0, c]
        go = ctab_ref[i, 1, c]
        lsl = local_ref.at[pl.ds(pl.multiple_of(lo, CHUNK), CHUNK)]
        gsl = global_ref.at[pl.ds(pl.multiple_of(go, CHUNK), CHUNK)]
        cp = pltpu.make_async_copy(lsl, gsl, sem) if to_global else pltpu.make_async_copy(gsl, lsl, sem)
        cp.start(priority=priority)

    def pair(j, c1):
        start(2 * j, 0)

        @pl.when(2 * j + 1 < total)
        def _():
            start(2 * j + 1, 1)

        return c1

    lax.fori_loop(0, lax.shift_right_logical(total + 1, 1), pair, 0)


def _scatter_kernel(tab_ref, ctab_ref, meta_ref, lrow_ref, h_ref, xs_ref, buf_ref, zero_ref, sem, zsem, *, tm, nb):
    i = pl.program_id(0)
    tabs = (tab_ref, ctab_ref)

    @pl.when(i == 0)
    def _():
        zero_ref[...] = jnp.zeros_like(zero_ref)
        n_used = meta_ref[2, 0]

        def zcopy(b):
            sub = lax.shift_right_logical(meta_ref[1, b], int(math.log2(FFN_SUB)))
            start = pl.multiple_of(b * MOE_ROWS + sub * FFN_SUB, FFN_SUB)
            return pltpu.make_async_copy(zero_ref, xs_ref.at[pl.ds(start, FFN_SUB)], zsem)

        def needs(b):
            return (b < n_used) & ((meta_ref[1, b] & (FFN_SUB - 1)) != 0)

        def start(b, c):
            @pl.when(needs(b))
            def _():
                zcopy(b).start()
            return c

        def wait(b, c):
            @pl.when(needs(b))
            def _():
                zcopy(b).wait()
            return c

        lax.fori_loop(0, nb, start, 0)
        lax.fori_loop(0, nb, wait, 0)

    rows = _row_iota((LOCAL_ROWS, tm))
    p0 = rows == lrow_ref[0, 0:1, :]
    p1 = rows == lrow_ref[0, 1:2, :]
    perm = jnp.where(p0 | p1, 1.0, 0.0).astype(BF16)
    sorted_rows = jnp.dot(perm, h_ref[...], preferred_element_type=F32)

    def fill(slot):
        buf = buf_ref.at[slot]
        buf[...] = _pack_bf16_pairs(sorted_rows, exact=True)
        _chunk_copies(tabs, i, buf, xs_ref, sem.at[slot], True, "start")

    def drain(tile, slot):
        _chunk_copies(tabs, tile, buf_ref.at[slot], xs_ref, sem.at[slot], True, "wait")

    even = (i & 1) == 0

    @pl.when(even)
    def _():
        fill(0)

    @pl.when(jnp.logical_not(even))
    def _():
        fill(1)

    @pl.when((i > 0) & even)
    def _():
        drain(i - 1, 1)

    @pl.when((i > 0) & jnp.logical_not(even))
    def _():
        drain(i - 1, 0)

    @pl.when((i == pl.num_programs(0) - 1) & even)
    def _():
        drain(i, 0)

    @pl.when((i == pl.num_programs(0) - 1) & jnp.logical_not(even))
    def _():
        drain(i, 1)


def _scatter(tab, ctab, meta, lrow, h2, tm, nb):
    t = h2.shape[0]
    kern = functools.partial(_scatter_kernel, tm=tm, nb=nb)
    grid_spec = pltpu.PrefetchScalarGridSpec(
        num_scalar_prefetch=3,
        grid=(t // tm,),
        in_specs=[pl.BlockSpec((1, 2, tm), lambda i, *_: (i, 0, 0)),
                  pl.BlockSpec((tm, D_MODEL), lambda i, *_: (i, 0))],
        out_specs=pl.BlockSpec(memory_space=pl.ANY),
        scratch_shapes=[pltpu.VMEM((2, LOCAL_ROWS, XS_WIDTH), U32), pltpu.VMEM((FFN_SUB, XS_WIDTH), U32),
                        pltpu.SemaphoreType.DMA((2,)), pltpu.SemaphoreType.DMA],
    )
    return pl.pallas_call(
        kern,
        grid_spec=grid_spec,
        out_shape=jax.ShapeDtypeStruct((nb * MOE_ROWS, XS_WIDTH), U32),
        compiler_params=_cparams("arbitrary"),
        name="moe_scatter",
    )(tab, ctab, meta, lrow, h2)


def _ffn_kernel(tab_ref, xs_ref, wg_ref, wu_ref, wd_ref, ys_ref, xbuf, ybuf, wgu_b, wd_b, sem_in, sem_out):
    e = pl.program_id(0)
    start = tab_ref[TAB_EXPERT, e, 0]
    nsub = lax.shift_right_logical(tab_ref[TAB_EXPERT, e, 1] + (FFN_SUB - 1), int(math.log2(FFN_SUB)))

    def rows_of(j):
        return pl.ds(pl.multiple_of(start + j * FFN_SUB, FFN_SUB), FFN_SUB)

    def copy_in(j, slot):
        return pltpu.make_async_copy(xs_ref.at[rows_of(j)], xbuf.at[slot], sem_in.at[slot])

    def copy_out(j, slot):
        return pltpu.make_async_copy(ybuf.at[slot], ys_ref.at[rows_of(j)], sem_out.at[slot])

    @pl.when(nsub > 0)
    def _():
        copy_in(0, 0).start()
        wgu_b[:, 0:EXPERT_FF] = wg_ref[0, 0].astype(BF16)
        wgu_b[:, EXPERT_FF:2 * EXPERT_FF] = wu_ref[0, 0].astype(BF16)
        wd_b[...] = wd_ref[0, 0].astype(BF16)

        def sub(j, c):
            slot = j & 1
            copy_in(j, slot).wait()

            @pl.when(j + 1 < nsub)
            def _():
                copy_in(j + 1, 1 - slot).start()

            x = _unpack_bf16_pairs(xbuf[slot])
            gu = jnp.dot(x, wgu_b[...], preferred_element_type=F32)
            act = (_silu(gu[:, 0:EXPERT_FF]) * gu[:, EXPERT_FF:2 * EXPERT_FF]).astype(BF16)
            y = jnp.dot(act, wd_b[...], preferred_element_type=F32)

            @pl.when(j >= 2)
            def _():
                copy_out(j - 2, slot).wait()

            ybuf[slot] = _pack_bf16_pairs(y)
            copy_out(j, slot).start()
            return c

        lax.fori_loop(0, nsub, sub, 0)

        @pl.when(nsub >= 2)
        def _():
            copy_out(nsub - 2, nsub & 1).wait()

        copy_out(nsub - 1, (nsub - 1) & 1).wait()


def _ffn(tab, xs, wg, wu, wd, layer):
    wblk = lambda e, *_: (layer, e, 0, 0)
    grid_spec = pltpu.PrefetchScalarGridSpec(
        num_scalar_prefetch=1,
        grid=(N_EXPERTS,),
        in_specs=[pl.BlockSpec(memory_space=pl.ANY),
                  pl.BlockSpec((1, 1, D_MODEL, EXPERT_FF), wblk),
                  pl.BlockSpec((1, 1, D_MODEL, EXPERT_FF), wblk),
                  pl.BlockSpec((1, 1, EXPERT_FF, D_MODEL), wblk)],
        out_specs=pl.BlockSpec(memory_space=pl.ANY),
        scratch_shapes=[pltpu.VMEM((2, FFN_SUB, PACKED), U32), pltpu.VMEM((2, FFN_SUB, PACKED), U32),
                        pltpu.VMEM((D_MODEL, 2 * EXPERT_FF), BF16), pltpu.VMEM((EXPERT_FF, D_MODEL), BF16),
                        pltpu.SemaphoreType.DMA((2,)), pltpu.SemaphoreType.DMA((2,))],
    )
    return pl.pallas_call(
        _ffn_kernel,
        grid_spec=grid_spec,
        out_shape=jax.ShapeDtypeStruct(xs.shape, U32),
        compiler_params=_cparams("arbitrary"),
        name="moe_experts",
    )(tab, xs, wg, wu, wd)


def _gather_kernel(tab_ref, ctab_ref, lcol_ref, x_ref, g_ref, ys_ref, o_ref, buf_ref, sem, *, tm, final):
    i = pl.program_id(0)

    last = pl.num_programs(0) - 1

    def fetch(tile, slot, action):
        _chunk_copies((tab_ref, ctab_ref), tile, buf_ref.at[slot], ys_ref, sem.at[slot], False, action)

    @pl.when(i == 0)
    def _():
        buf_ref[...] = jnp.zeros_like(buf_ref)
        fetch(0, 0, "start")

    even = (i & 1) == 0

    @pl.when((i < last) & even)
    def _():
        fetch(i + 1, 1, "start")

    @pl.when((i < last) & jnp.logical_not(even))
    def _():
        fetch(i + 1, 0, "start")

    col = _lane_iota((tm, LOCAL_ROWS)).astype(F32)
    pick0 = jnp.where(col == lcol_ref[:, 0:1], 1.0, 0.0).astype(BF16)
    pick1 = jnp.where(col == lcol_ref[:, 1:2], 1.0, 0.0).astype(BF16)

    def finish(slot):
        fetch(i, slot, "wait")
        y = _unpack_bf16_pairs(buf_ref[slot])
        both = jnp.dot(jnp.concatenate([pick0, pick1], axis=0), y, preferred_element_type=F32)
        x = x_ref[...] + lcol_ref[:, 2:3] * both[0:tm] + lcol_ref[:, 3:4] * both[tm:2 * tm]
        o_ref[...] = _rms(x, g_ref[...]) if final else x

    @pl.when(even)
    def _():
        finish(0)

    @pl.when(jnp.logical_not(even))
    def _():
        finish(1)


def _gather(tab, ctab, lcol, x2, g, ys, tm, final):
    t = x2.shape[0]
    kern = functools.partial(_gather_kernel, tm=tm, final=final)
    grid_spec = pltpu.PrefetchScalarGridSpec(
        num_scalar_prefetch=2,
        grid=(t // tm,),
        in_specs=[pl.BlockSpec((tm, LANES), lambda i, *_: (i, 0)),
                  pl.BlockSpec((tm, D_MODEL), lambda i, *_: (i, 0)),
                  pl.BlockSpec((1, D_MODEL), lambda i, *_: (0, 0)),
                  pl.BlockSpec(memory_space=pl.ANY)],
        out_specs=pl.BlockSpec((tm, D_MODEL), lambda i, *_: (i, 0)),
        scratch_shapes=[pltpu.VMEM((2, LOCAL_ROWS, PACKED), U32), pltpu.SemaphoreType.DMA((2,))],
    )
    return pl.pallas_call(
        kern,
        grid_spec=grid_spec,
        out_shape=jax.ShapeDtypeStruct((t, D_MODEL), F32),
        compiler_params=_cparams("arbitrary"),
        name="moe_combine",
    )(tab, ctab, lcol, x2, g, ys)


def _pad_rows(a, rows=8):
    return jnp.zeros((rows, a.shape[-1]), F32).at[:a.shape[0]].set(a.astype(F32))


def _arrange_w_in(w):
    gw = GROUP_WIDTH
    a0 = 0
    b0 = 3 * gw
    c0 = b0 + 3 * gw + N_HEADS
    d0 = c0 + gw + (gw + 4 * SSM_STATE) + N_HEADS
    half = MLA_ROPE // 2
    f_logit = w[:, b0 + 3 * gw:b0 + 3 * gw + N_HEADS]
    dt_raw = w[:, c0 + 3 * gw:c0 + 3 * gw + N_HEADS]
    kr0 = d0 + MLA_Q_LORA + MLA_KV_LORA
    kr = w[:, kr0:kr0 + MLA_ROPE]
    kr_sw = jnp.concatenate([-kr[:, half:], kr[:, :half]], axis=1)
    zeros = lambda n: jnp.zeros((w.shape[0], n), w.dtype)
    misc = jnp.concatenate([f_logit, dt_raw, zeros(MISC_ROPE - 2 * N_HEADS), kr,
                            zeros(LANES - MISC_ROPE - MLA_ROPE)], axis=1)
    misc2 = jnp.concatenate([zeros(MISC_ROPE), kr_sw, zeros(LANES - MISC_ROPE - MLA_ROPE)], axis=1)
    fox_q = w[:, b0:b0 + gw] * (HEAD_DIM ** -0.5 * LOG2E)
    out = jnp.concatenate([w[:, a0:a0 + 3 * gw], fox_q, w[:, b0 + gw:b0 + 2 * gw], w[:, c0:c0 + 3 * gw],
                           w[:, d0:d0 + MLA_Q_LORA + MLA_KV_LORA], misc, misc2], axis=1)
    fox_vt = w[:, b0 + 2 * gw:b0 + 3 * gw].T
    return out.astype(BF16), fox_vt.astype(BF16)


def _arrange_mla(w_uq, w_ukv):
    half = MLA_ROPE // 2
    qd = MLA_NOPE + MLA_ROPE
    wq, wqs, wk, wv = [], [], [], []
    zq = jnp.zeros((MLA_Q_LORA, LANES - qd), w_uq.dtype)
    zk = jnp.zeros((MLA_KV_LORA, LANES - MLA_NOPE), w_ukv.dtype)
    for h in range(N_HEADS):
        q = w_uq[:, h * qd:(h + 1) * qd]
        nope, rope = q[:, :MLA_NOPE], q[:, MLA_NOPE:]
        wq.append(jnp.concatenate([nope, rope, zq], axis=1))
        wqs.append(jnp.concatenate([jnp.zeros_like(nope), -rope[:, half:], rope[:, :half], zq], axis=1))
        kv = w_ukv[:, h * 2 * MLA_NOPE:(h + 1) * 2 * MLA_NOPE]
        wk.append(jnp.concatenate([kv[:, :MLA_NOPE], zk], axis=1))
        wv.append(kv[:, MLA_NOPE:])
    cat = lambda xs: jnp.concatenate(xs, axis=1).astype(BF16)
    return cat(wq), cat(wqs), cat(wk), cat(wv).T


def kernel(x, positions, norm_mix, w_in, conv_a, fox_forget_bias, ssm_conv_w, ssm_conv_b, ssm_dt_bias,
           ssm_a_log, ssm_d, ssm_norm, mla_q_norm, mla_kv_norm, mla_w_uq, mla_w_ukv, w_out, norm_ffn,
           router_group_w, router_group_b, router_expert_w, router_expert_b, expert_w_gate, expert_w_up,
           expert_w_down, norm_final):
    batch, seq, d = x.shape
    t = batch * seq
    depth = w_in.shape[0]
    tm = min(512, t)
    tq = min(ATTN_TQ, seq)
    tmd = min(MOE_TILE, t)
    max_rows = 2 * t + (CHUNK - 1) * N_EXPERTS * (t // tmd) + N_EXPERTS * (MOE_ROWS - 1)
    nb = -(-max_rows // MOE_ROWS)
    nbp = -(-nb // LANES) * LANES

    xf = x.reshape(t, d)
    pos_col = positions.astype(F32).reshape(t, 1)
    cos, sin = _rope_tables(pos_col, tm)

    for l in range(depth):
        w_in_p, w_vt = _arrange_w_in(w_in[l])
        pa, pb, pc, pd, misc, misc2, fox_vt = _inproj(xf, norm_mix[l][None, :], w_in_p, w_vt, tm)

        sp = jnp.zeros((8, LANES), F32)
        sp = sp.at[0, MISC_F:MISC_F + N_HEADS].set(fox_forget_bias[l])
        sp = sp.at[0, MISC_DT:MISC_DT + N_HEADS].set(ssm_dt_bias[l])
        sp = sp.at[1, MISC_DT:MISC_DT + N_HEADS].set(ssm_a_log[l])
        col, rows, fox_q, fox_k, tref = _scalar_prep(misc, sp, pb, batch, seq, tq)

        ya = _conv_mixer(pa, _pad_rows(conv_a[l]), batch, seq)
        yb = _attention(fox_q, fox_k, fox_vt, tref, batch, seq, tq, "fox_attention")
        conv_wb = _pad_rows(jnp.concatenate([ssm_conv_w[l], ssm_conv_b[l][None, :]], axis=0))
        ssd_par = _pad_rows(jnp.stack([jnp.repeat(ssm_d[l], HEAD_DIM), ssm_norm[l]]))
        yc = _ssd_mixer(pc, col, rows, conv_wb, ssd_par, batch, seq)
        wq, wqs, wk, wv = _arrange_mla(mla_w_uq[l], mla_w_ukv[l])
        q, k, v = _mla_prep(pd, misc, misc2, cos, sin, mla_q_norm[l][None, :], mla_kv_norm[l][None, :],
                            wq, wqs, wk, wv, tm)
        yd = _attention(q, k, v, None, batch, seq, tq, "mla_attention")

        pad = jnp.zeros((d, LANES - N_EXPERTS - N_EXPERT_GROUPS), F32)
        wr = jnp.concatenate([router_expert_w[l], router_group_w[l], pad], axis=1)
        wr_hi = wr.astype(BF16)
        wr = jnp.concatenate([wr_hi, (wr - wr_hi.astype(F32)).astype(BF16)], axis=1)
        br = jnp.concatenate([router_expert_b[l], router_group_b[l], pad[0]])[None, :]
        x2, h2, rrow, cnt = _outproj(xf, ya, yb, yc, yd, w_out[l].astype(BF16), norm_ffn[l][None, :], wr, br,
                                     tm, tmd)

        lrow, lcol, tab, ctab, meta = _route(rrow, cnt, tmd, nbp)
        xs = _scatter(tab, ctab, meta, lrow, h2, tmd, nb)
        ys = _ffn(tab, xs, expert_w_gate, expert_w_up, expert_w_down, l)
        final = l == depth - 1
        xf = _gather(tab, ctab, lcol, x2, norm_final[None, :], ys, tmd, final)

    return xf.reshape(batch, seq, d)


def _retile(dest, tm, tmd):
    if tm == tmd:
        return dest
    nt = dest.shape[0]
    return dest.reshape(nt, 2, tm // tmd, tmd).transpose(0, 2, 1, 3).reshape(nt * (tm // tmd), 2, tmd)
```

```python
import functools
import math

import jax
import jax.numpy as jnp
import numpy as np
from jax import lax
from jax.experimental import pallas as pl
from jax.experimental.pallas import tpu as pltpu

F32 = jnp.float32
BF16 = jnp.bfloat16
I32 = jnp.int32

LANES = 128
VMEM_LIMIT_BYTES = 56 * 1024 * 1024

D_MODEL = 1024
RMS_EPS = 1e-6
LOG2E = math.log2(math.e)
GROUP_WIDTH = 256
HEAD_DIM = 64
N_HEADS = 4

CONV_A_WIDTH = 3
SSM_CONV = 4
SSM_STATE = 64
SSM_CHUNK = 256

MLA_NOPE = 64
MLA_ROPE = 32
MLA_Q_LORA = 256
MLA_KV_LORA = 128
ROPE_BASE = 10000.0
MLA_CHUNK = 64
ATTN_TQ = 512
ATTN_RB = 128

N_EXPERT_GROUPS = 4
EXPERTS_PER_GROUP = 8
N_EXPERTS = 32
EXPERT_FF = 256
MOE_ROWS = 512
FFN_SUB = 256

SEG_A = (0, 768)
SEG_B = (768, 1280)
SEG_C = (1280, 2048)
SEG_D = (2048, 2432)
SEG_M = (2432, 2560)
SEG_M2 = (2560, 2688)
IN_COLS_PADDED = 2688
HEAD_PAD = N_HEADS * LANES
AUG_LANE = HEAD_DIM
MISC_F = 0
MISC_DT = 4
MISC_ROPE = 64
COL_CUMF = 0
COL_DT = 4
COL_ACUM = 8
N_SCALAR_ROWS = 16


def _cparams(*sem):
    return pltpu.CompilerParams(dimension_semantics=sem, vmem_limit_bytes=VMEM_LIMIT_BYTES)


def _lane_iota(shape):
    return lax.broadcasted_iota(I32, shape, len(shape) - 1)


def _row_iota(shape):
    return lax.broadcasted_iota(I32, shape, 0)


def _rms(x, g):
    ms = jnp.mean(x * x, axis=-1, keepdims=True)
    return x * lax.rsqrt(ms + RMS_EPS) * g


def _silu(x):
    return x / (1.0 + jnp.exp(-x))


def _softplus(x):
    return jnp.maximum(x, 0.0) + jnp.log(1.0 + jnp.exp(-jnp.abs(x)))


def _shift_rows(x, k):
    rolled = pltpu.roll(x, k, 0)
    return jnp.where(_row_iota(x.shape) >= k, rolled, 0.0)


def _rope_kernel(pos_ref, freq_ref, cos_ref, sin_ref):
    ang = pos_ref[...] * freq_ref[...]
    lane = _lane_iota(ang.shape)
    rope = (lane >= MISC_ROPE) & (lane < MISC_ROPE + MLA_ROPE)
    cos_ref[...] = jnp.where(rope, jnp.cos(ang), jnp.where(lane < MISC_ROPE, 1.0, 0.0))
    sin_ref[...] = jnp.where(rope, jnp.sin(ang), 0.0)


def _rope_tables(pos_col, tm):
    t = pos_col.shape[0]
    half = MLA_ROPE // 2
    inv = ROPE_BASE ** (-np.arange(0, MLA_ROPE, 2, dtype=np.float32) / MLA_ROPE)
    freq = np.zeros((1, LANES), np.float32)
    freq[0, MISC_ROPE:MISC_ROPE + half] = inv
    freq[0, MISC_ROPE + half:MISC_ROPE + MLA_ROPE] = inv
    return pl.pallas_call(
        _rope_kernel,
        grid=(t // tm,),
        in_specs=[pl.BlockSpec((tm, 1), lambda i: (i, 0)),
                  pl.BlockSpec((1, LANES), lambda i: (0, 0))],
        out_specs=[pl.BlockSpec((tm, LANES), lambda i: (i, 0))] * 2,
        out_shape=[jax.ShapeDtypeStruct((t, LANES), F32)] * 2,
        compiler_params=_cparams("parallel"),
        name="rope_tables",
    )(pos_col, jnp.asarray(freq))


def _inproj_kernel(x_ref, g_ref, w_ref, wvt_ref, oa, ob, oc, od, om, om2, ovt):
    h = _rms(x_ref[...], g_ref[...]).astype(BF16)
    for o, (lo, hi) in ((oa, SEG_A), (ob, SEG_B), (oc, SEG_C), (od, SEG_D), (om, SEG_M), (om2, SEG_M2)):
        o[...] = jnp.dot(h, w_ref[:, lo:hi], preferred_element_type=F32).astype(o.dtype)
    ovt[...] = lax.dot_general(wvt_ref[...], h, (((1,), (1,)), ((), ())),
                               preferred_element_type=F32).astype(ovt.dtype)


def _inproj(x, g, w, wvt, tm):
    t = x.shape[0]
    widths = [(s[1] - s[0]) for s in (SEG_A, SEG_B, SEG_C, SEG_D, SEG_M, SEG_M2)]
    dtypes = [BF16, BF16, BF16, BF16, F32, F32]
    return pl.pallas_call(
        _inproj_kernel,
        grid=(t // tm,),
        in_specs=[pl.BlockSpec((tm, D_MODEL), lambda i: (i, 0)),
                  pl.BlockSpec((1, D_MODEL), lambda i: (0, 0)),
                  pl.BlockSpec((D_MODEL, IN_COLS_PADDED), lambda i: (0, 0)),
                  pl.BlockSpec((GROUP_WIDTH, D_MODEL), lambda i: (0, 0))],
        out_specs=[pl.BlockSpec((tm, wd), lambda i: (i, 0)) for wd in widths]
        + [pl.BlockSpec((GROUP_WIDTH, tm), lambda i: (0, i))],
        out_shape=[jax.ShapeDtypeStruct((t, wd), dt) for wd, dt in zip(widths, dtypes)]
        + [jax.ShapeDtypeStruct((GROUP_WIDTH, t), BF16)],
        compiler_params=_cparams("parallel"),
        name="inproj",
    )(x, g, w, wvt)


def _scalar_prep_kernel(m_ref, p_ref, qk_ref, place_ref, const_ref,
                        col_ref, row_ref, qa_ref, ka_ref, tref_ref, *, tq):
    s = m_ref.shape[0]
    tref_ref[...] = jnp.zeros_like(tref_ref)
    tile_ref = jnp.zeros((1, LANES), F32)
    m = m_ref[...]
    bias = p_ref[0:1, :]
    a_log = p_ref[1:2, :]
    lane = _lane_iota(m.shape)
    z = m + bias
    logf = jnp.minimum(z, 0.0) - jnp.log(1.0 + jnp.exp(-jnp.abs(z)))
    dt = _softplus(z)
    a = dt * (-jnp.exp(a_log))
    is_f = lane < MISC_DT
    is_dt = (lane >= MISC_DT) & (lane < MISC_DT + N_HEADS)
    v = jnp.where(is_f, logf, jnp.where(is_dt, a, 0.0))
    r = _row_iota((SSM_CHUNK, SSM_CHUNK))
    c = _lane_iota((SSM_CHUNK, SSM_CHUNK))
    tril = jnp.where(r >= c, 1.0, 0.0).astype(BF16)
    carry = jnp.zeros((1, LANES), F32)
    lane_1 = _lane_iota((1, LANES))
    lane_b = _lane_iota((SSM_CHUNK, LANES))
    low = lane_b < HEAD_DIM
    aug_lanes = (lane_b >= AUG_LANE) & (lane_b < AUG_LANE + AUG_TERMS)
    for ci in range(s // SSM_CHUNK):
        rest = v[ci * SSM_CHUNK:(ci + 1) * SSM_CHUNK]
        cs = jnp.zeros((SSM_CHUNK, LANES), F32)
        for _ in range(3):
            term = rest.astype(BF16)
            cs = cs + jnp.dot(tril, term, preferred_element_type=F32)
            rest = rest - term.astype(F32)
        cs = cs + jnp.where(lane_1 < MISC_DT, carry, 0.0)
        carry = cs[SSM_CHUNK - 1:SSM_CHUNK]
        acum = pltpu.roll(cs, COL_ACUM - MISC_DT, 1)
        out = jnp.where(lane_b < MISC_DT, cs * LOG2E,
                        jnp.where(lane_b < COL_ACUM, dt[ci * SSM_CHUNK:(ci + 1) * SSM_CHUNK],
                                  jnp.where(lane_b < COL_ACUM + N_HEADS, acum, 0.0)))
        rows = slice(ci * SSM_CHUNK, (ci + 1) * SSM_CHUNK)
        col_ref[rows, :] = out
        row_ref[0, :, rows] = out.T[:N_SCALAR_ROWS]
        if (ci * SSM_CHUNK) % tq == 0:
            tile_ref = out[0:1, :]
            ti = (ci * SSM_CHUNK) // tq
            tref_ref[0, ti:ti + 1, :] = tile_ref
        c = out - tile_ref
        c_hi = c.astype(BF16)
        r1 = c - c_hi.astype(F32)
        c_mid = r1.astype(BF16)
        c_lo = (r1 - c_mid.astype(F32)).astype(BF16)
        compact = jnp.dot(jnp.concatenate([c_hi, c_mid, c_lo], axis=1), place_ref[...],
                          preferred_element_type=F32) + const_ref[0:1, :]
        for side, o_ref in enumerate((qa_ref, ka_ref)):
            dec = compact[:, side * LANES:(side + 1) * LANES]
            for h in range(N_HEADS):
                pair = qk_ref[rows, side * GROUP_WIDTH + (h // 2) * LANES:
                              side * GROUP_WIDTH + (h // 2 + 1) * LANES].astype(F32)
                feat = pair if h % 2 == 0 else pltpu.roll(pair, HEAD_DIM, 1)
                dec_h = pltpu.roll(dec, AUG_LANE - AUG_TERMS * h, 1)
                group = jnp.where(low, feat, jnp.where(aug_lanes, dec_h, 0.0))
                o_ref[rows, h * LANES:(h + 1) * LANES] = group.astype(o_ref.dtype)


AUG_TERMS = 6


def _fox_placement():
    place = np.zeros((3 * LANES, 2 * LANES), np.float32)
    const = np.zeros((8, 2 * LANES), np.float32)
    for h in range(N_HEADS):
        a0 = AUG_TERMS * h
        for term in range(3):
            place[term * LANES + COL_CUMF + h, a0 + term] = 1.0
            place[term * LANES + COL_CUMF + h, LANES + a0 + 3 + term] = -1.0
            const[0, a0 + 3 + term] = 1.0
            const[0, LANES + a0 + term] = 1.0
    return jnp.asarray(place, BF16), jnp.asarray(const, F32)


def _scalar_prep(misc, params, qk, batch, seq, tq):
    place, const = _fox_placement()
    full = lambda a: pl.BlockSpec(a.shape, lambda b: (0,) * a.ndim)
    return pl.pallas_call(
        functools.partial(_scalar_prep_kernel, tq=tq),
        grid=(batch,),
        in_specs=[pl.BlockSpec((seq, LANES), lambda b: (b, 0)),
                  pl.BlockSpec((8, LANES), lambda b: (0, 0)),
                  pl.BlockSpec((seq, 2 * GROUP_WIDTH), lambda b: (b, 0)),
                  full(place), full(const)],
        out_specs=[pl.BlockSpec((seq, LANES), lambda b: (b, 0)),
                   pl.BlockSpec((1, N_SCALAR_ROWS, seq), lambda b: (b, 0, 0)),
                   pl.BlockSpec((seq, HEAD_PAD), lambda b: (b, 0)),
                   pl.BlockSpec((seq, HEAD_PAD), lambda b: (b, 0)),
                   pl.BlockSpec((1, 8, LANES), lambda b: (b, 0, 0))],
        out_shape=[jax.ShapeDtypeStruct((batch * seq, LANES), F32),
                   jax.ShapeDtypeStruct((batch, N_SCALAR_ROWS, seq), F32),
                   jax.ShapeDtypeStruct((batch * seq, HEAD_PAD), BF16),
                   jax.ShapeDtypeStruct((batch * seq, HEAD_PAD), BF16),
                   jax.ShapeDtypeStruct((batch, 8, LANES), F32)],
        compiler_params=_cparams("parallel"),
        name="scalar_prep",
    )(misc, params, qk, place, const)


def _conv_mixer_kernel(p_ref, w_ref, o_ref):
    gw = GROUP_WIDTH
    b_gate = p_ref[:, 0:gw].astype(F32)
    cv = p_ref[:, gw:2 * gw].astype(F32) * p_ref[:, 2 * gw:3 * gw].astype(F32)
    acc = cv * w_ref[CONV_A_WIDTH - 1:CONV_A_WIDTH, :]
    for k in range(1, CONV_A_WIDTH):
        acc = acc + _shift_rows(cv, k) * w_ref[CONV_A_WIDTH - 1 - k:CONV_A_WIDTH - k, :]
    o_ref[...] = (b_gate * acc).astype(o_ref.dtype)


def _conv_mixer(pa, w, batch, seq):
    return pl.pallas_call(
        _conv_mixer_kernel,
        grid=(batch,),
        in_specs=[pl.BlockSpec((seq, 3 * GROUP_WIDTH), lambda b: (b, 0)),
                  pl.BlockSpec((8, GROUP_WIDTH), lambda b: (0, 0))],
        out_specs=pl.BlockSpec((seq, GROUP_WIDTH), lambda b: (b, 0)),
        out_shape=jax.ShapeDtypeStruct((batch * seq, GROUP_WIDTH), BF16),
        compiler_params=_cparams("parallel"),
        name="conv_mixer",
    )(pa, w)


def _pair_lanes(col, base, shape):
    lane = _lane_iota(shape)
    return jnp.where(lane < HEAD_DIM, col[:, base:base + 1], col[:, base + 1:base + 2])


def _ssd_kernel(p_ref, col_ref, row_ref, cw_ref, par_ref, o_ref, u_ref):
    s = p_ref.shape[0]
    q = SSM_CHUNK
    gw = GROUP_WIDTH
    xbc = p_ref[:, gw:3 * gw].astype(F32)
    acc = xbc * cw_ref[SSM_CONV - 1:SSM_CONV, :]
    for k in range(1, SSM_CONV):
        acc = acc + _shift_rows(xbc, k) * cw_ref[SSM_CONV - 1 - k:SSM_CONV - k, :]
    u_ref[...] = _silu(acc + cw_ref[SSM_CONV:SSM_CONV + 1, :])

    d_skip = par_ref[0:1, :]
    norm_g = par_ref[1:2, :]
    lane_q = _lane_iota((q, LANES))
    low = lane_q < HEAD_DIM
    tri = _row_iota((q, q)) >= _lane_iota((q, q))

    def chunk(ci, states):
        rows = pl.ds(ci * q, q)
        u = u_ref[rows, :]
        col = col_ref[rows, :]
        bm = u[:, gw:gw + LANES]
        cm = u[:, gw + LANES:gw + 2 * LANES]
        z = p_ref[rows, 0:gw].astype(F32)
        new_states = []
        ys = []
        for g in range(2):
            sel = low if g == 0 else jnp.logical_not(low)
            cg = jnp.where(sel, cm, 0.0).astype(BF16)
            bg = jnp.where(sel, bm, 0.0)
            gmat = lax.dot_general(cg, bm.astype(BF16), (((1,), (1,)), ((), ())),
                                   preferred_element_type=F32)
            xs = u[:, g * LANES:(g + 1) * LANES]
            dt2 = _pair_lanes(col, COL_DT + 2 * g, (q, LANES))
            ac2 = _pair_lanes(col, COL_ACUM + 2 * g, (q, LANES))
            xdt = xs * dt2
            xdt_b = xdt.astype(BF16)
            st = states[g]
            y_off = jnp.dot(cg, st.astype(BF16), preferred_element_type=F32) * jnp.exp(ac2)
            halves = []
            for hh in range(2):
                h = 2 * g + hh
                ac_col = col[:, COL_ACUM + h:COL_ACUM + h + 1]
                ac_row = row_ref[0, COL_ACUM + h:COL_ACUM + h + 1, rows]
                decay = jnp.exp(jnp.where(tri, ac_col - ac_row, -1e30))
                mm = (gmat * decay).astype(BF16)
                halves.append(jnp.dot(mm, xdt_b, preferred_element_type=F32))
            y = jnp.where(low, halves[0], halves[1]) + y_off + d_skip[:, g * LANES:(g + 1) * LANES] * xs
            ys.append(y)
            ac_last = ac2[q - 1:q, :]
            w_end = jnp.exp(ac_last - ac2)
            xw = (xdt * w_end).astype(BF16)
            upd = jnp.dot(bg.T.astype(BF16), xw, preferred_element_type=F32)
            new_states.append(st * jnp.exp(ac_last) + upd)
        yfull = jnp.concatenate(ys, axis=1) * _silu(z)
        o_ref[rows, :] = _rms(yfull, norm_g).astype(o_ref.dtype)
        return tuple(new_states)

    init = (jnp.zeros((LANES, LANES), F32), jnp.zeros((LANES, LANES), F32))
    states = init
    for ci in range(s // q):
        states = chunk(ci, states)


def _ssd_mixer(pc, col, rows, conv_wb, par, batch, seq):
    gw = GROUP_WIDTH
    return pl.pallas_call(
        _ssd_kernel,
        grid=(batch,),
        in_specs=[pl.BlockSpec((seq, 3 * gw), lambda b: (b, 0)),
                  pl.BlockSpec((seq, LANES), lambda b: (b, 0)),
                  pl.BlockSpec((1, N_SCALAR_ROWS, seq), lambda b: (b, 0, 0)),
                  pl.BlockSpec((8, 2 * gw), lambda b: (0, 0)),
                  pl.BlockSpec((8, gw), lambda b: (0, 0))],
        out_specs=pl.BlockSpec((seq, gw), lambda b: (b, 0)),
        out_shape=jax.ShapeDtypeStruct((batch * seq, gw), BF16),
        scratch_shapes=[pltpu.VMEM((seq, 2 * gw), F32)],
        compiler_params=_cparams("parallel"),
        name="ssd_mixer",
    )(pc, col, rows, conv_wb, par)


def _mla_prep_kernel(pd_ref, m_ref, m2_ref, cos_ref, sin_ref, nq_ref, nkv_ref,
                     wq_ref, wqs_ref, wk_ref, wvt_ref, q_ref, k_ref, vt_ref):
    cq = _rms(pd_ref[:, 0:MLA_Q_LORA].astype(F32), nq_ref[...]).astype(BF16)
    ckv = _rms(pd_ref[:, MLA_Q_LORA:MLA_Q_LORA + MLA_KV_LORA].astype(F32), nkv_ref[...]).astype(BF16)
    cos = cos_ref[...]
    sin = sin_ref[...]
    cos4 = jnp.concatenate([cos] * N_HEADS, axis=1)
    sin4 = jnp.concatenate([sin] * N_HEADS, axis=1)
    scale = (MLA_NOPE + MLA_ROPE) ** -0.5 * LOG2E
    q = jnp.dot(cq, wq_ref[...], preferred_element_type=F32)
    qs = jnp.dot(cq, wqs_ref[...], preferred_element_type=F32)
    q_ref[...] = ((q * cos4 + qs * sin4) * scale).astype(q_ref.dtype)
    lane = _lane_iota(cos.shape)
    rope = (lane >= MISC_ROPE) & (lane < MISC_ROPE + MLA_ROPE)
    kr = jnp.where(rope, m_ref[...] * cos + m2_ref[...] * sin, 0.0)
    k = jnp.dot(ckv, wk_ref[...], preferred_element_type=F32)
    k_ref[...] = (k + jnp.concatenate([kr] * N_HEADS, axis=1)).astype(k_ref.dtype)
    vt_ref[...] = lax.dot_general(wvt_ref[...], ckv, (((1,), (1,)), ((), ())),
                                  preferred_element_type=F32).astype(vt_ref.dtype)


def _mla_prep(pd, misc, misc2, cos, sin, nq, nkv, wq, wqs, wk, wv, tm):
    t = pd.shape[0]
    hp = N_HEADS * LANES
    full = lambda a: pl.BlockSpec(a.shape, lambda i: (0, 0))
    tile = lambda w: pl.BlockSpec((tm, w), lambda i: (i, 0))
    return pl.pallas_call(
        _mla_prep_kernel,
        grid=(t // tm,),
        in_specs=[tile(MLA_Q_LORA + MLA_KV_LORA), tile(LANES), tile(LANES), tile(LANES), tile(LANES),
                  full(nq), full(nkv), full(wq), full(wqs), full(wk), full(wv)],
        out_specs=[tile(hp), tile(hp), pl.BlockSpec((GROUP_WIDTH, tm), lambda i: (0, i))],
        out_shape=[jax.ShapeDtypeStruct((t, hp), BF16), jax.ShapeDtypeStruct((t, hp), BF16),
                   jax.ShapeDtypeStruct((GROUP_WIDTH, t), BF16)],
        compiler_params=_cparams("parallel"),
        name="mla_prep",
    )(pd, misc, misc2, cos, sin, nq, nkv, wq, wqs, wk, wv)


def _attn_kernel(*refs, fox, tq):
    if fox:
        tref_ref, q_ref, k_ref, vt_ref, o_ref = refs
    else:
        q_ref, k_ref, vt_ref, o_ref = refs
        tref_ref = None
    b = pl.program_id(0)
    i = pl.program_id(1)
    key = _row_iota((tq, tq))
    qry = _lane_iota((tq, tq))
    if fox:
        allowed = key <= qry
    else:
        shift = int(math.log2(MLA_CHUNK))
        allowed = (key >> shift) <= (qry >> shift)
    qs = [q_ref[:, h * LANES:(h + 1) * LANES] for h in range(N_HEADS)]

    def step(j, masked, carry):
        rk = pl.ds(pl.multiple_of(j * tq, tq), tq)
        scores = [lax.dot_general(k_ref[rk, h * LANES:(h + 1) * LANES], qs[h], (((1,), (1,)), ((), ())),
                                  preferred_element_type=F32) for h in range(N_HEADS)]
        probs = []
        for h in range(N_HEADS):
            m, l, _ = carry[h]
            s = scores[h]
            if masked:
                s = jnp.where(allowed, s, -1e30)
            delta = (tref_ref[b, i, h] - tref_ref[b, j, h]) if fox else 0.0
            m_new = jnp.maximum(m, jnp.max(s, axis=0, keepdims=True) + delta)
            alpha = jnp.exp2(m - m_new)
            p = jnp.exp2(s - (m_new - delta))
            l_new = alpha * l + jnp.sum(p, axis=0, keepdims=True)
            probs.append((m_new, l_new, alpha, p.astype(BF16)))
        new = []
        for h in range(N_HEADS):
            pair = h // 2
            m_new, l_new, alpha, p = probs[h]
            pv = jnp.dot(vt_ref[pair * LANES:(pair + 1) * LANES, rk], p, preferred_element_type=F32)
            new.append((m_new, l_new, alpha * carry[h][2] + pv))
        return tuple(new)

    init = tuple((jnp.full((1, tq), -1e30, F32), jnp.zeros((1, tq), F32), jnp.zeros((LANES, tq), F32))
                 for _ in range(N_HEADS))
    carry = lax.fori_loop(0, i, lambda j, c: step(j, False, c), init)
    carry = step(i, True, carry)
    outs = [acc / l for (_, l, acc) in carry]
    top = _row_iota((LANES, tq)) < HEAD_DIM
    o_t = jnp.concatenate([jnp.where(top, outs[0], outs[1]), jnp.where(top, outs[2], outs[3])], axis=0)
    o_ref[...] = o_t.T.astype(o_ref.dtype)


def _attention(q, k, vt, tref, batch, seq, tq, name):
    nq = seq // tq
    fox = tref is not None
    kern = functools.partial(_attn_kernel, fox=fox, tq=tq)
    grid_spec = pltpu.PrefetchScalarGridSpec(
        num_scalar_prefetch=1 if fox else 0,
        grid=(batch, nq),
        in_specs=[pl.BlockSpec((tq, HEAD_PAD), lambda b, i, *_: (b * nq + i, 0)),
                  pl.BlockSpec((seq, HEAD_PAD), lambda b, i, *_: (b, 0)),
                  pl.BlockSpec((GROUP_WIDTH, seq), lambda b, i, *_: (0, b))],
        out_specs=pl.BlockSpec((tq, GROUP_WIDTH), lambda b, i, *_: (b * nq + i, 0)),
    )
    args = ((tref,) if fox else ()) + (q, k, vt)
    return pl.pallas_call(
        kern,
        grid_spec=grid_spec,
        out_shape=jax.ShapeDtypeStruct((batch * seq, GROUP_WIDTH), BF16),
        compiler_params=_cparams("parallel", "arbitrary"),
        name=name,
    )(*args)


def _outproj_kernel(x_ref, ya, yb, yc, yd, w_ref, g_ref, wr_ref, br_ref,
                    x2_ref, h2_ref, rrow_ref, cnt_ref, *, tm, moe_tile):
    y = jnp.concatenate([ya[...], yb[...], yc[...], yd[...]], axis=1)
    x2 = x_ref[...] + jnp.dot(y, w_ref[...], preferred_element_type=F32)
    x2_ref[...] = x2
    h2 = _rms(x2, g_ref[...])
    h2_ref[...] = h2.astype(h2_ref.dtype)
    h_hi = h2.astype(BF16)
    h_lo = (h2 - h_hi.astype(F32)).astype(BF16)
    part = jnp.dot(h_hi, wr_ref[...], preferred_element_type=F32)
    logits = (part[:, 0:LANES] + part[:, LANES:2 * LANES]
              + jnp.dot(h_lo, wr_ref[:, 0:LANES], preferred_element_type=F32) + br_ref[...])
    lt = logits.T
    row = _row_iota(lt.shape)
    neg = -1e30
    big = 1 << 20
    gmask = (row >= N_EXPERTS) & (row < N_EXPERTS + N_EXPERT_GROUPS)
    gl = jnp.where(gmask, lt, neg)
    gmax = jnp.max(gl, axis=0, keepdims=True)
    gsum = jnp.sum(jnp.where(gmask, jnp.exp(gl - gmax), 0.0), axis=0, keepdims=True)
    g_w = 1.0 / gsum
    g_idx = jnp.min(jnp.where(gmask & (gl == gmax), row, big), axis=0, keepdims=True) - N_EXPERTS
    emask = (row < N_EXPERTS) & ((row >> int(math.log2(EXPERTS_PER_GROUP))) == g_idx)
    el = jnp.where(emask, lt, neg)
    e1v = jnp.max(el, axis=0, keepdims=True)
    esum = jnp.sum(jnp.where(emask, jnp.exp(el - e1v), 0.0), axis=0, keepdims=True)
    i1 = jnp.min(jnp.where(emask & (el == e1v), row, big), axis=0, keepdims=True)
    el2 = jnp.where(row == i1, neg, el)
    e2v = jnp.max(el2, axis=0, keepdims=True)
    i2 = jnp.min(jnp.where(emask & (row != i1) & (el2 == e2v), row, big), axis=0, keepdims=True)
    p1 = 1.0 / esum
    p2 = jnp.exp(e2v - e1v) / esum
    w1 = g_w * (p1 / (p1 + p2))
    w2 = g_w * (p2 / (p1 + p2))
    out_row = _row_iota(rrow_ref.shape)
    rrow_ref[...] = jnp.where(out_row == 0, i1.astype(F32),
                              jnp.where(out_row == 1, i2.astype(F32),
                                        jnp.where(out_row == 2, w1, jnp.where(out_row == 3, w2, 0.0))))
    step = pl.program_id(0)

    @pl.when(step == 0)
    def _():
        cnt_ref[...] = jnp.zeros_like(cnt_ref)

    chosen = jnp.where((row == i1) | (row == i2), 1.0, 0.0)
    cnt = cnt_ref[...]
    lane = _lane_iota(cnt.shape)
    tiles = tm // moe_tile
    for k in range(tiles):
        n = jnp.sum(chosen[0:N_EXPERTS, k * moe_tile:(k + 1) * moe_tile], axis=-1, keepdims=True)
        cnt = jnp.where(lane == step * tiles + k, n, cnt)
    cnt_ref[...] = cnt


def _outproj(x, ya, yb, yc, yd, w, g, wr, br, tm, moe_tile):
    t = x.shape[0]
    full = lambda a: pl.BlockSpec(a.shape, lambda i: (0, 0))
    tile = lambda wd: pl.BlockSpec((tm, wd), lambda i: (i, 0))
    return pl.pallas_call(
        functools.partial(_outproj_kernel, tm=tm, moe_tile=moe_tile),
        grid=(t // tm,),
        in_specs=[tile(D_MODEL)] + [tile(GROUP_WIDTH)] * 4 + [full(w), full(g), full(wr), full(br)],
        out_specs=[tile(D_MODEL), tile(D_MODEL), pl.BlockSpec((8, tm), lambda i: (0, i)),
                   pl.BlockSpec((N_EXPERTS, LANES), lambda i: (0, 0))],
        out_shape=[jax.ShapeDtypeStruct((t, D_MODEL), F32), jax.ShapeDtypeStruct((t, D_MODEL), BF16),
                   jax.ShapeDtypeStruct((8, t), F32), jax.ShapeDtypeStruct((N_EXPERTS, LANES), F32)],
        compiler_params=_cparams("arbitrary"),
        name="outproj_router",
    )(x, ya, yb, yc, yd, w, g, wr, br)


def _positions_kernel(r_ref, dest_ref, meta_ref, cnt_ref, carry_ref, start_ref, *, tm, nbp):
    phase = pl.program_id(0)
    i = pl.program_id(1)
    e_iota = _row_iota((N_EXPERTS, tm))
    e0 = r_ref[0:1, :].astype(I32)
    e1 = r_ref[1:2, :].astype(I32)
    oh0 = e_iota == e0
    oh1 = e_iota == e1
    oh = jnp.where(oh0 | oh1, 1.0, 0.0)

    @pl.when((phase == 0) & (i == 0))
    def _():
        cnt_ref[...] = jnp.zeros_like(cnt_ref)

    @pl.when(phase == 0)
    def _():
        cnt_ref[...] += jnp.sum(oh, axis=-1, keepdims=True)

    @pl.when((phase == 1) & (i == 0))
    def _():
        cnt = cnt_ref[...]
        padded = jnp.floor((cnt + (MOE_ROWS - 1)) * (1.0 / MOE_ROWS)) * MOE_ROWS
        tril = jnp.where(_row_iota((N_EXPERTS, N_EXPERTS)) >= _lane_iota((N_EXPERTS, N_EXPERTS)), 1.0, 0.0)
        pend = jnp.dot(tril, padded, preferred_element_type=F32, precision=lax.Precision.HIGHEST)
        pstart = pend - padded
        start_ref[...] = pstart
        carry_ref[...] = jnp.zeros_like(carry_ref)
        pend_b = jnp.concatenate([pend] * (nbp // LANES), axis=1)
        vend_b = jnp.concatenate([pstart + cnt] * (nbp // LANES), axis=1)
        b0 = (_lane_iota((N_EXPERTS, nbp)) * MOE_ROWS).astype(F32)
        bexp = jnp.sum(jnp.where(pend_b <= b0, 1.0, 0.0), axis=0, keepdims=True)
        bexp = jnp.minimum(bexp, N_EXPERTS - 1.0)
        is_e = _row_iota((N_EXPERTS, nbp)).astype(F32) == bexp
        vend = jnp.sum(jnp.where(is_e, vend_b, 0.0), axis=0, keepdims=True)
        nvalid = jnp.clip(vend - b0[0:1], 0.0, float(MOE_ROWS))
        total = jnp.max(pend_b, axis=0, keepdims=True) * (1.0 / MOE_ROWS)
        row = _row_iota((8, nbp))
        meta = jnp.where(row == 0, bexp, jnp.where(row == 1, nvalid, jnp.where(row == 2, total, 0.0)))
        meta_ref[...] = meta.astype(I32)

    @pl.when(phase == 1)
    def _():
        su = jnp.where(_row_iota((tm, tm)) < _lane_iota((tm, tm)), 1.0, 0.0).astype(BF16)
        before = jnp.dot(oh.astype(BF16), su, preferred_element_type=F32)
        base = start_ref[:, 0:1] + carry_ref[:, 0:1] + before
        d0 = jnp.sum(jnp.where(oh0, base, 0.0), axis=0, keepdims=True)
        d1 = jnp.sum(jnp.where(oh1, base, 0.0), axis=0, keepdims=True)
        dest_ref[0, 0:1, :] = d0.astype(I32)
        dest_ref[0, 1:2, :] = d1.astype(I32)
        carry_ref[...] += jnp.sum(oh, axis=-1, keepdims=True)


def _positions(rrow, tm, nbp):
    t = rrow.shape[1]
    nt = t // tm
    kern = functools.partial(_positions_kernel, tm=tm, nbp=nbp)
    return pl.pallas_call(
        kern,
        grid=(2, nt),
        in_specs=[pl.BlockSpec((8, tm), lambda p, i: (0, i))],
        out_specs=[pl.BlockSpec((1, 2, tm), lambda p, i: (i * p, 0, 0)),
                   pl.BlockSpec((8, nbp), lambda p, i: (0, 0))],
        out_shape=[jax.ShapeDtypeStruct((nt, 2, tm), I32), jax.ShapeDtypeStruct((8, nbp), I32)],
        scratch_shapes=[pltpu.VMEM((N_EXPERTS, LANES), F32)] * 3,
        compiler_params=_cparams("arbitrary", "arbitrary"),
        name="moe_positions",
    )(rrow)


def _dispatch_kernel(meta_ref, dest_ref, h_ref, xs_ref, zero_ref, sem, zsem, *, tm, nb):
    i = pl.program_id(0)

    @pl.when(i == 0)
    def _():
        zero_ref[...] = jnp.zeros_like(zero_ref)
        n_used = meta_ref[2, 0]

        def zcopy(b):
            return pltpu.make_async_copy(zero_ref, xs_ref.at[pl.ds(b * MOE_ROWS, MOE_ROWS)], zsem)

        def needs(b):
            return (b < n_used) & (meta_ref[1, b] < MOE_ROWS)

        def start(b, c):
            @pl.when(needs(b))
            def _():
                zcopy(b).start()
            return c

        def wait(b, c):
            @pl.when(needs(b))
            def _():
                zcopy(b).wait()
            return c

        lax.fori_loop(0, nb, start, 0)
        lax.fori_loop(0, nb, wait, 0)

    def copy(t, k):
        return pltpu.make_async_copy(h_ref.at[pl.ds(t, 1)], xs_ref.at[pl.ds(dest_ref[0, k, t], 1)], sem)

    def start(t, c):
        copy(t, 0).start()
        copy(t, 1).start()
        return c

    def wait(t, c):
        copy(t, 0).wait()
        copy(t, 1).wait()
        return c

    lax.fori_loop(0, tm, start, 0)
    lax.fori_loop(0, tm, wait, 0)


def _dispatch(meta, dest, h2, tm, nb):
    t = h2.shape[0]
    kern = functools.partial(_dispatch_kernel, tm=tm, nb=nb)
    grid_spec = pltpu.PrefetchScalarGridSpec(
        num_scalar_prefetch=1,
        grid=(t // tm,),
        in_specs=[pl.BlockSpec((1, 2, tm), lambda i, m: (i, 0, 0), memory_space=pltpu.SMEM),
                  pl.BlockSpec((tm, D_MODEL), lambda i, m: (i, 0))],
        out_specs=pl.BlockSpec(memory_space=pl.ANY),
        scratch_shapes=[pltpu.VMEM((MOE_ROWS, D_MODEL), F32),
                        pltpu.SemaphoreType.DMA, pltpu.SemaphoreType.DMA],
    )
    return pl.pallas_call(
        kern,
        grid_spec=grid_spec,
        out_shape=jax.ShapeDtypeStruct((nb * MOE_ROWS, D_MODEL), F32),
        compiler_params=_cparams("arbitrary"),
        name="moe_dispatch",
    )(meta, dest, h2)


def _expert_kernel(meta_ref, x_ref, wg_ref, wu_ref, wd_ref, o_ref):
    b = pl.program_id(0)

    @pl.when(b < meta_ref[2, 0])
    def _():
        x = x_ref[...].astype(BF16)
        gate = jnp.dot(x, wg_ref[0, 0].astype(BF16), preferred_element_type=F32)
        up = jnp.dot(x, wu_ref[0, 0].astype(BF16), preferred_element_type=F32)
        act = (_silu(gate) * up).astype(BF16)
        o_ref[...] = jnp.dot(act, wd_ref[0, 0].astype(BF16), preferred_element_type=F32)


def _experts(meta, xs, wg, wu, wd, layer, nb):
    def blk(b, m):
        return (jnp.minimum(b, m[2, 0] - 1), 0)

    def wblk(b, m):
        return (layer, m[0, jnp.minimum(b, m[2, 0] - 1)], 0, 0)

    grid_spec = pltpu.PrefetchScalarGridSpec(
        num_scalar_prefetch=1,
        grid=(nb,),
        in_specs=[pl.BlockSpec((MOE_ROWS, D_MODEL), blk),
                  pl.BlockSpec((1, 1, D_MODEL, EXPERT_FF), wblk),
                  pl.BlockSpec((1, 1, D_MODEL, EXPERT_FF), wblk),
                  pl.BlockSpec((1, 1, EXPERT_FF, D_MODEL), wblk)],
        out_specs=pl.BlockSpec((MOE_ROWS, D_MODEL), blk),
    )
    return pl.pallas_call(
        _expert_kernel,
        grid_spec=grid_spec,
        out_shape=jax.ShapeDtypeStruct((nb * MOE_ROWS, D_MODEL), F32),
        compiler_params=_cparams("arbitrary"),
        name="moe_experts",
    )(meta, xs, wg, wu, wd)


def _combine_kernel(dest_ref, x_ref, r_ref, g_ref, ys_ref, o_ref, buf_ref, sem, *, tm, final):
    def copy(t, k):
        return pltpu.make_async_copy(ys_ref.at[pl.ds(dest_ref[0, k, t], 1)],
                                     buf_ref.at[k, pl.ds(t, 1)], sem)

    def start(t, c):
        copy(t, 0).start()
        copy(t, 1).start()
        return c

    def wait(t, c):
        copy(t, 0).wait()
        copy(t, 1).wait()
        return c

    lax.fori_loop(0, tm, start, 0)
    lax.fori_loop(0, tm, wait, 0)
    w0 = r_ref[:, 2:3]
    w1 = r_ref[:, 3:4]
    x = x_ref[...] + (buf_ref[0] * w0 + buf_ref[1] * w1)
    o_ref[...] = _rms(x, g_ref[...]) if final else x


def _combine(dest, x2, rcol, g, ys, tm, final):
    t = x2.shape[0]
    kern = functools.partial(_combine_kernel, tm=tm, final=final)
    return pl.pallas_call(
        kern,
        grid=(t // tm,),
        in_specs=[pl.BlockSpec((1, 2, tm), lambda i: (i, 0, 0), memory_space=pltpu.SMEM),
                  pl.BlockSpec((tm, D_MODEL), lambda i: (i, 0)),
                  pl.BlockSpec((tm, LANES), lambda i: (i, 0)),
                  pl.BlockSpec((1, D_MODEL), lambda i: (0, 0)),
                  pl.BlockSpec(memory_space=pl.ANY)],
        out_specs=pl.BlockSpec((tm, D_MODEL), lambda i: (i, 0)),
        out_shape=jax.ShapeDtypeStruct((t, D_MODEL), F32),
        scratch_shapes=[pltpu.VMEM((2, tm, D_MODEL), F32), pltpu.SemaphoreType.DMA],
        compiler_params=_cparams("arbitrary"),
        name="moe_combine",
    )(dest, x2, rcol, g, ys)


MOE_TILE = 256
CHUNK = 8
LOCAL_ROWS = 2 * MOE_TILE + 256
PACKED = D_MODEL // 2
XS_WIDTH = PACKED
U32 = jnp.uint32


def _pack_bf16_pairs(x, exact=False):
    if not exact:
        x = x.astype(BF16).astype(F32)
    half = x.shape[1] // 2
    lo = lax.bitcast_convert_type(x[:, :half], U32)
    hi = lax.bitcast_convert_type(x[:, half:], U32)
    return hi | (lo >> 16)


def _unpack_bf16_pairs(words):
    lo = lax.bitcast_convert_type(words << 16, F32)
    hi = lax.bitcast_convert_type(words & U32(0xFFFF0000), F32)
    return jnp.concatenate([lo, hi], axis=1).astype(BF16)
TAB_CHUNKS, TAB_LOCAL, TAB_GLOBAL, TAB_TOTAL, TAB_EXPERT = 0, 1, 2, 3, 4
N_TABS = 5
COPY_ROWS = (2 * CHUNK, CHUNK)


def _route_kernel(r_ref, cnt_ref, lrow_ref, lcol_ref, tab_ref, ctab_ref, meta_ref,
                  loff_ref, goff_ref, n8_ref, *, tm, nbp):
    i = pl.program_id(0)
    e_iota = _row_iota((N_EXPERTS, tm))
    oh0 = e_iota == r_ref[0:1, :].astype(I32)
    oh1 = e_iota == r_ref[1:2, :].astype(I32)
    oh = jnp.where(oh0 | oh1, 1.0, 0.0)
    tile_lane = _lane_iota((N_EXPERTS, LANES)) == i
    hi = lax.Precision.HIGHEST

    @pl.when(i == 0)
    def _():
        cnt = cnt_ref[...]
        n8 = jnp.floor((cnt + (CHUNK - 1)) * (1.0 / CHUNK)) * CHUNK
        er = _row_iota((N_EXPERTS, N_EXPERTS))
        ec = _lane_iota((N_EXPERTS, N_EXPERTS))
        below = jnp.where(er > ec, 1.0, 0.0)
        loff = jnp.dot(below, n8, preferred_element_type=F32, precision=hi)
        rows_e = jnp.sum(n8, axis=-1, keepdims=True) + jnp.zeros_like(n8)
        padded = jnp.floor((rows_e + (MOE_ROWS - 1)) * (1.0 / MOE_ROWS)) * MOE_ROWS
        e_start = jnp.dot(below, padded, preferred_element_type=F32, precision=hi)
        tr = _row_iota((LANES, LANES))
        tc = _lane_iota((LANES, LANES))
        earlier = jnp.where(tr < tc, 1.0, 0.0)
        goff = e_start + jnp.dot(n8, earlier, preferred_element_type=F32, precision=hi)
        loff_ref[...] = loff
        goff_ref[...] = goff
        n8_ref[...] = n8
        tab_ref[TAB_CHUNKS] = (n8 * (1.0 / CHUNK)).astype(I32)
        tab_ref[TAB_LOCAL] = loff.astype(I32)
        tab_ref[TAB_GLOBAL] = goff.astype(I32)
        big = jnp.floor(n8 * (0.5 / CHUNK))
        small = n8 * (1.0 / CHUNK) - 2.0 * big
        row_t = _row_iota(n8.shape)
        tab_ref[TAB_TOTAL] = jnp.where(row_t == 0, jnp.sum(big, axis=0, keepdims=True),
                                       jnp.where(row_t == 1, jnp.sum(small, axis=0, keepdims=True),
                                                 0.0)).astype(I32)
        lane_t = _lane_iota(n8.shape)
        tab_ref[TAB_EXPERT] = jnp.where(lane_t == 0, e_start, jnp.where(lane_t == 1, rows_e, 0.0)).astype(I32)
        reps = nbp // LANES
        pend_b = jnp.concatenate([e_start + padded] * reps, axis=1)
        vend_b = jnp.concatenate([e_start + rows_e] * reps, axis=1)
        b0 = (_lane_iota((N_EXPERTS, nbp)) * MOE_ROWS).astype(F32)
        bexp = jnp.sum(jnp.where(pend_b <= b0, 1.0, 0.0), axis=0, keepdims=True)
        bexp = jnp.minimum(bexp, N_EXPERTS - 1.0)
        is_e = _row_iota((N_EXPERTS, nbp)).astype(F32) == bexp
        vend = jnp.sum(jnp.where(is_e, vend_b, 0.0), axis=0, keepdims=True)
        nvalid = jnp.clip(vend - b0[0:1], 0.0, float(MOE_ROWS))
        total = jnp.max(pend_b, axis=0, keepdims=True) * (1.0 / MOE_ROWS)
        row = _row_iota((8, nbp))
        meta = jnp.where(row == 0, bexp, jnp.where(row == 1, nvalid, jnp.where(row == 2, total, 0.0)))
        meta_ref[...] = meta.astype(I32)

    su = jnp.where(_row_iota((tm, tm)) < _lane_iota((tm, tm)), 1.0, 0.0).astype(BF16)
    before = jnp.dot(oh.astype(BF16), su, preferred_element_type=F32)
    base = jnp.sum(jnp.where(tile_lane, loff_ref[...], 0.0), axis=-1, keepdims=True) + before
    d0 = jnp.sum(jnp.where(oh0, base, 0.0), axis=0, keepdims=True)
    d1 = jnp.sum(jnp.where(oh1, base, 0.0), axis=0, keepdims=True)
    lrow_ref[0, 0:1, :] = d0.astype(I32)
    lrow_ref[0, 1:2, :] = d1.astype(I32)
    pick_tile = lambda ref: jnp.sum(jnp.where(tile_lane, ref[...], 0.0), axis=-1, keepdims=True)
    nch = pick_tile(n8_ref) * (1.0 / CHUNK)
    n_big = jnp.floor(nch * 0.5)
    n_small = nch - 2.0 * n_big
    incl = jnp.where(_row_iota((N_EXPERTS, N_EXPERTS)) >= _lane_iota((N_EXPERTS, N_EXPERTS)), 1.0, 0.0)
    cidx = _lane_iota((N_EXPERTS, LANES)).astype(F32)
    loff_t = pick_tile(loff_ref)
    goff_t = pick_tile(goff_ref)
    for k, (n, rows, first) in enumerate(((n_big, COPY_ROWS[0], 0.0), (n_small, COPY_ROWS[1], n_big * COPY_ROWS[0]))):
        cend = jnp.dot(incl, n + jnp.zeros((N_EXPERTS, LANES), F32), preferred_element_type=F32, precision=hi)
        cstart = cend - n
        mine = (cidx >= cstart) & (cidx < cend)
        step_rows = first + (cidx - cstart) * rows
        ctab_ref[0, 2 * k:2 * k + 1, :] = jnp.sum(jnp.where(mine, loff_t + step_rows, 0.0), axis=0,
                                                  keepdims=True).astype(I32)
        ctab_ref[0, 2 * k + 1:2 * k + 2, :] = jnp.sum(jnp.where(mine, goff_t + step_rows, 0.0), axis=0,
                                                      keepdims=True).astype(I32)
    row = _row_iota((LANES, tm))
    lcol_ref[...] = jnp.where(row == 0, d0, jnp.where(row == 1, d1,
                              jnp.where(row == 2, r_ref[2:3, :], jnp.where(row == 3, r_ref[3:4, :], 0.0)))).T


def _route(rrow, cnt, tm, nbp):
    t = rrow.shape[1]
    nt = t // tm
    kern = functools.partial(_route_kernel, tm=tm, nbp=nbp)
    return pl.pallas_call(
        kern,
        grid=(nt,),
        in_specs=[pl.BlockSpec((8, tm), lambda i: (0, i)),
                  pl.BlockSpec((N_EXPERTS, LANES), lambda i: (0, 0))],
        out_specs=[pl.BlockSpec((1, 2, tm), lambda i: (i, 0, 0)),
                   pl.BlockSpec((tm, LANES), lambda i: (i, 0)),
                   pl.BlockSpec((N_TABS, N_EXPERTS, LANES), lambda i: (0, 0, 0)),
                   pl.BlockSpec((1, 2 * len(COPY_ROWS), LANES), lambda i: (i, 0, 0)),
                   pl.BlockSpec((8, nbp), lambda i: (0, 0))],
        out_shape=[jax.ShapeDtypeStruct((nt, 2, tm), I32), jax.ShapeDtypeStruct((t, LANES), F32),
                   jax.ShapeDtypeStruct((N_TABS, N_EXPERTS, LANES), I32), jax.ShapeDtypeStruct((nt, 2 * len(COPY_ROWS), LANES), I32),
                   jax.ShapeDtypeStruct((8, nbp), I32)],
        scratch_shapes=[pltpu.VMEM((N_EXPERTS, LANES), F32)] * 3,
        compiler_params=_cparams("arbitrary"),
        name="moe_route",
    )(rrow, cnt)


def _chunk_copies(tabs, i, local_ref, global_ref, sem, to_global, action):
    tab_ref, ctab_ref = tabs
    for k, rows in enumerate(COPY_ROWS):
        count = tab_ref[TAB_TOTAL, k, i]

        def copy(lo, go, rows=rows):
            lsl = local_ref.at[pl.ds(pl.multiple_of(lo, CHUNK), rows)]
            gsl = global_ref.at[pl.ds(pl.multiple_of(go, CHUNK), rows)]
            return pltpu.make_async_copy(lsl, gsl, sem) if to_global else pltpu.make_async_copy(gsl, lsl, sem)

        if action == "wait":
            def one(c, c1, copy=copy):
                copy(0, 0).wait()
                return c1
        else:
            def one(c, c1, copy=copy, k=k):
                copy(ctab_ref[i, 2 * k, c], ctab_ref[i, 2 * k + 1, c]).start()
                return c1

        lax.fori_loop(0, count, one, 0)


def _scatter_kernel(tab_ref, ctab_ref, meta_ref, lrow_ref, h_ref, xs_ref, buf_ref, zero_ref, sem, zsem, *, tm, nb):
    i = pl.program_id(0)
    tabs = (tab_ref, ctab_ref)

    @pl.when(i == 0)
    def _():
        zero_ref[...] = jnp.zeros_like(zero_ref)
        n_used = meta_ref[2, 0]

        def zcopy(b):
            sub = lax.shift_right_logical(meta_ref[1, b], int(math.log2(FFN_SUB)))
            start = pl.multiple_of(b * MOE_ROWS + sub * FFN_SUB, FFN_SUB)
            return pltpu.make_async_copy(zero_ref, xs_ref.at[pl.ds(start, FFN_SUB)], zsem)

        def needs(b):
            return (b < n_used) & ((meta_ref[1, b] & (FFN_SUB - 1)) != 0)

        def start(b, c):
            @pl.when(needs(b))
            def _():
                zcopy(b).start()
            return c

        def wait(b, c):
            @pl.when(needs(b))
            def _():
                zcopy(b).wait()
            return c

        lax.fori_loop(0, nb, start, 0)
        lax.fori_loop(0, nb, wait, 0)

    rows = _row_iota((LOCAL_ROWS, tm))
    p0 = rows == lrow_ref[0, 0:1, :]
    p1 = rows == lrow_ref[0, 1:2, :]
    perm = jnp.where(p0 | p1, 1.0, 0.0).astype(BF16)
    sorted_rows = jnp.dot(perm, h_ref[...], preferred_element_type=F32)

    def fill(slot):
        buf = buf_ref.at[slot]
        buf[...] = _pack_bf16_pairs(sorted_rows, exact=True)
        _chunk_copies(tabs, i, buf, xs_ref, sem.at[slot], True, "start")

    def drain(tile, slot):
        _chunk_copies(tabs, tile, buf_ref.at[slot], xs_ref, sem.at[slot], True, "wait")

    even = (i & 1) == 0

    @pl.when(even)
    def _():
        fill(0)

    @pl.when(jnp.logical_not(even))
    def _():
        fill(1)

    @pl.when((i > 0) & even)
    def _():
        drain(i - 1, 1)

    @pl.when((i > 0) & jnp.logical_not(even))
    def _():
        drain(i - 1, 0)

    @pl.when((i == pl.num_programs(0) - 1) & even)
    def _():
        drain(i, 0)

    @pl.when((i == pl.num_programs(0) - 1) & jnp.logical_not(even))
    def _():
        drain(i, 1)


def _scatter(tab, ctab, meta, lrow, h2, tm, nb):
    t = h2.shape[0]
    kern = functools.partial(_scatter_kernel, tm=tm, nb=nb)
    grid_spec = pltpu.PrefetchScalarGridSpec(
        num_scalar_prefetch=3,
        grid=(t // tm,),
        in_specs=[pl.BlockSpec((1, 2, tm), lambda i, *_: (i, 0, 0)),
                  pl.BlockSpec((tm, D_MODEL), lambda i, *_: (i, 0))],
        out_specs=pl.BlockSpec(memory_space=pl.ANY),
        scratch_shapes=[pltpu.VMEM((2, LOCAL_ROWS, XS_WIDTH), U32), pltpu.VMEM((FFN_SUB, XS_WIDTH), U32),
                        pltpu.SemaphoreType.DMA((2,)), pltpu.SemaphoreType.DMA],
    )
    return pl.pallas_call(
        kern,
        grid_spec=grid_spec,
        out_shape=jax.ShapeDtypeStruct((nb * MOE_ROWS, XS_WIDTH), U32),
        compiler_params=_cparams("arbitrary"),
        name="moe_scatter",
    )(tab, ctab, meta, lrow, h2)


def _ffn_kernel(meta_ref, x_ref, wg_ref, wu_ref, wd_ref, o_ref, wgu_b, wd_b):
    b = pl.program_id(0)
    live = b < meta_ref[2, 0]
    prev = meta_ref[0, jnp.maximum(b - 1, 0)]

    @pl.when(live & ((b == 0) | (meta_ref[0, b] != prev)))
    def _():
        wgu_b[:, 0:EXPERT_FF] = wg_ref[0, 0].astype(BF16)
        wgu_b[:, EXPERT_FF:2 * EXPERT_FF] = wu_ref[0, 0].astype(BF16)
        wd_b[...] = wd_ref[0, 0].astype(BF16)

    nvalid = meta_ref[1, jnp.maximum(jnp.minimum(b, meta_ref[2, 0] - 1), 0)]
    for sub in range(MOE_ROWS // FFN_SUB):
        rows = slice(sub * FFN_SUB, (sub + 1) * FFN_SUB)
        used = live & (nvalid > sub * FFN_SUB)

        @pl.when(used)
        def _():
            x = _unpack_bf16_pairs(x_ref[rows, 0:PACKED])
            gu = jnp.dot(x, wgu_b[...], preferred_element_type=F32)
            act = (_silu(gu[:, 0:EXPERT_FF]) * gu[:, EXPERT_FF:2 * EXPERT_FF]).astype(BF16)
            y = jnp.dot(act, wd_b[...], preferred_element_type=F32)
            o_ref[rows, :] = _pack_bf16_pairs(y)

        @pl.when(live & jnp.logical_not(used))
        def _():
            o_ref[rows, :] = jnp.zeros((FFN_SUB, PACKED), U32)


def _ffn(meta, xs, wg, wu, wd, layer, nb):
    def blk(b, m):
        return (jnp.maximum(jnp.minimum(b, m[2, 0] - 1), 0), 0)

    def wblk(b, m):
        return (layer, m[0, jnp.maximum(jnp.minimum(b, m[2, 0] - 1), 0)], 0, 0)

    grid_spec = pltpu.PrefetchScalarGridSpec(
        num_scalar_prefetch=1,
        grid=(nb,),
        in_specs=[pl.BlockSpec((MOE_ROWS, XS_WIDTH), blk),
                  pl.BlockSpec((1, 1, D_MODEL, EXPERT_FF), wblk),
                  pl.BlockSpec((1, 1, D_MODEL, EXPERT_FF), wblk),
                  pl.BlockSpec((1, 1, EXPERT_FF, D_MODEL), wblk)],
        out_specs=pl.BlockSpec((MOE_ROWS, PACKED), blk),
        scratch_shapes=[pltpu.VMEM((D_MODEL, 2 * EXPERT_FF), BF16), pltpu.VMEM((EXPERT_FF, D_MODEL), BF16)],
    )
    return pl.pallas_call(
        _ffn_kernel,
        grid_spec=grid_spec,
        out_shape=jax.ShapeDtypeStruct((nb * MOE_ROWS, PACKED), U32),
        compiler_params=_cparams("arbitrary"),
        name="moe_experts",
    )(meta, xs, wg, wu, wd)


def _gather_kernel(tab_ref, ctab_ref, lcol_ref, x_ref, g_ref, ys_ref, o_ref, buf_ref, sem, *, tm, final):
    i = pl.program_id(0)

    last = pl.num_programs(0) - 1

    def fetch(tile, slot, action):
        _chunk_copies((tab_ref, ctab_ref), tile, buf_ref.at[slot], ys_ref, sem.at[slot], False, action)

    @pl.when(i == 0)
    def _():
        buf_ref[...] = jnp.zeros_like(buf_ref)
        fetch(0, 0, "start")

    even = (i & 1) == 0

    @pl.when((i < last) & even)
    def _():
        fetch(i + 1, 1, "start")

    @pl.when((i < last) & jnp.logical_not(even))
    def _():
        fetch(i + 1, 0, "start")

    col = _lane_iota((tm, LOCAL_ROWS)).astype(F32)
    pick0 = jnp.where(col == lcol_ref[:, 0:1], 1.0, 0.0).astype(BF16)
    pick1 = jnp.where(col == lcol_ref[:, 1:2], 1.0, 0.0).astype(BF16)

    def finish(slot):
        fetch(i, slot, "wait")
        y = _unpack_bf16_pairs(buf_ref[slot])
        both = jnp.dot(jnp.concatenate([pick0, pick1], axis=0), y, preferred_element_type=F32)
        x = x_ref[...] + lcol_ref[:, 2:3] * both[0:tm] + lcol_ref[:, 3:4] * both[tm:2 * tm]
        o_ref[...] = _rms(x, g_ref[...]) if final else x

    @pl.when(even)
    def _():
        finish(0)

    @pl.when(jnp.logical_not(even))
    def _():
        finish(1)


def _gather(tab, ctab, lcol, x2, g, ys, tm, final):
    t = x2.shape[0]
    kern = functools.partial(_gather_kernel, tm=tm, final=final)
    grid_spec = pltpu.PrefetchScalarGridSpec(
        num_scalar_prefetch=2,
        grid=(t // tm,),
        in_specs=[pl.BlockSpec((tm, LANES), lambda i, *_: (i, 0)),
                  pl.BlockSpec((tm, D_MODEL), lambda i, *_: (i, 0)),
                  pl.BlockSpec((1, D_MODEL), lambda i, *_: (0, 0)),
                  pl.BlockSpec(memory_space=pl.ANY)],
        out_specs=pl.BlockSpec((tm, D_MODEL), lambda i, *_: (i, 0)),
        scratch_shapes=[pltpu.VMEM((2, LOCAL_ROWS, PACKED), U32), pltpu.SemaphoreType.DMA((2,))],
    )
    return pl.pallas_call(
        kern,
        grid_spec=grid_spec,
        out_shape=jax.ShapeDtypeStruct((t, D_MODEL), F32),
        compiler_params=_cparams("arbitrary"),
        name="moe_combine",
    )(tab, ctab, lcol, x2, g, ys)


def _pad_rows(a, rows=8):
    return jnp.zeros((rows, a.shape[-1]), F32).at[:a.shape[0]].set(a.astype(F32))


def _arrange_w_in(w):
    gw = GROUP_WIDTH
    a0 = 0
    b0 = 3 * gw
    c0 = b0 + 3 * gw + N_HEADS
    d0 = c0 + gw + (gw + 4 * SSM_STATE) + N_HEADS
    half = MLA_ROPE // 2
    f_logit = w[:, b0 + 3 * gw:b0 + 3 * gw + N_HEADS]
    dt_raw = w[:, c0 + 3 * gw:c0 + 3 * gw + N_HEADS]
    kr0 = d0 + MLA_Q_LORA + MLA_KV_LORA
    kr = w[:, kr0:kr0 + MLA_ROPE]
    kr_sw = jnp.concatenate([-kr[:, half:], kr[:, :half]], axis=1)
    zeros = lambda n: jnp.zeros((w.shape[0], n), w.dtype)
    misc = jnp.concatenate([f_logit, dt_raw, zeros(MISC_ROPE - 2 * N_HEADS), kr,
                            zeros(LANES - MISC_ROPE - MLA_ROPE)], axis=1)
    misc2 = jnp.concatenate([zeros(MISC_ROPE), kr_sw, zeros(LANES - MISC_ROPE - MLA_ROPE)], axis=1)
    fox_q = w[:, b0:b0 + gw] * (HEAD_DIM ** -0.5 * LOG2E)
    out = jnp.concatenate([w[:, a0:a0 + 3 * gw], fox_q, w[:, b0 + gw:b0 + 2 * gw], w[:, c0:c0 + 3 * gw],
                           w[:, d0:d0 + MLA_Q_LORA + MLA_KV_LORA], misc, misc2], axis=1)
    fox_vt = w[:, b0 + 2 * gw:b0 + 3 * gw].T
    return out.astype(BF16), fox_vt.astype(BF16)


def _arrange_mla(w_uq, w_ukv):
    half = MLA_ROPE // 2
    qd = MLA_NOPE + MLA_ROPE
    wq, wqs, wk, wv = [], [], [], []
    zq = jnp.zeros((MLA_Q_LORA, LANES - qd), w_uq.dtype)
    zk = jnp.zeros((MLA_KV_LORA, LANES - MLA_NOPE), w_ukv.dtype)
    for h in range(N_HEADS):
        q = w_uq[:, h * qd:(h + 1) * qd]
        nope, rope = q[:, :MLA_NOPE], q[:, MLA_NOPE:]
        wq.append(jnp.concatenate([nope, rope, zq], axis=1))
        wqs.append(jnp.concatenate([jnp.zeros_like(nope), -rope[:, half:], rope[:, :half], zq], axis=1))
        kv = w_ukv[:, h * 2 * MLA_NOPE:(h + 1) * 2 * MLA_NOPE]
        wk.append(jnp.concatenate([kv[:, :MLA_NOPE], zk], axis=1))
        wv.append(kv[:, MLA_NOPE:])
    cat = lambda xs: jnp.concatenate(xs, axis=1).astype(BF16)
    return cat(wq), cat(wqs), cat(wk), cat(wv).T


def kernel(x, positions, norm_mix, w_in, conv_a, fox_forget_bias, ssm_conv_w, ssm_conv_b, ssm_dt_bias,
           ssm_a_log, ssm_d, ssm_norm, mla_q_norm, mla_kv_norm, mla_w_uq, mla_w_ukv, w_out, norm_ffn,
           router_group_w, router_group_b, router_expert_w, router_expert_b, expert_w_gate, expert_w_up,
           expert_w_down, norm_final):
    batch, seq, d = x.shape
    t = batch * seq
    depth = w_in.shape[0]
    tm = min(512, t)
    tq = min(ATTN_TQ, seq)
    tmd = min(MOE_TILE, t)
    max_rows = 2 * t + (CHUNK - 1) * N_EXPERTS * (t // tmd) + N_EXPERTS * (MOE_ROWS - 1)
    nb = -(-max_rows // MOE_ROWS)
    nbp = -(-nb // LANES) * LANES

    xf = x.reshape(t, d)
    pos_col = positions.astype(F32).reshape(t, 1)
    cos, sin = _rope_tables(pos_col, tm)

    for l in range(depth):
        w_in_p, w_vt = _arrange_w_in(w_in[l])
        pa, pb, pc, pd, misc, misc2, fox_vt = _inproj(xf, norm_mix[l][None, :], w_in_p, w_vt, tm)

        sp = jnp.zeros((8, LANES), F32)
        sp = sp.at[0, MISC_F:MISC_F + N_HEADS].set(fox_forget_bias[l])
        sp = sp.at[0, MISC_DT:MISC_DT + N_HEADS].set(ssm_dt_bias[l])
        sp = sp.at[1, MISC_DT:MISC_DT + N_HEADS].set(ssm_a_log[l])
        col, rows, fox_q, fox_k, tref = _scalar_prep(misc, sp, pb, batch, seq, tq)

        ya = _conv_mixer(pa, _pad_rows(conv_a[l]), batch, seq)
        yb = _attention(fox_q, fox_k, fox_vt, tref, batch, seq, tq, "fox_attention")
        conv_wb = _pad_rows(jnp.concatenate([ssm_conv_w[l], ssm_conv_b[l][None, :]], axis=0))
        ssd_par = _pad_rows(jnp.stack([jnp.repeat(ssm_d[l], HEAD_DIM), ssm_norm[l]]))
        yc = _ssd_mixer(pc, col, rows, conv_wb, ssd_par, batch, seq)
        wq, wqs, wk, wv = _arrange_mla(mla_w_uq[l], mla_w_ukv[l])
        q, k, v = _mla_prep(pd, misc, misc2, cos, sin, mla_q_norm[l][None, :], mla_kv_norm[l][None, :],
                            wq, wqs, wk, wv, tm)
        yd = _attention(q, k, v, None, batch, seq, tq, "mla_attention")

        pad = jnp.zeros((d, LANES - N_EXPERTS - N_EXPERT_GROUPS), F32)
        wr = jnp.concatenate([router_expert_w[l], router_group_w[l], pad], axis=1)
        wr_hi = wr.astype(BF16)
        wr = jnp.concatenate([wr_hi, (wr - wr_hi.astype(F32)).astype(BF16)], axis=1)
        br = jnp.concatenate([router_expert_b[l], router_group_b[l], pad[0]])[None, :]
        x2, h2, rrow, cnt = _outproj(xf, ya, yb, yc, yd, w_out[l].astype(BF16), norm_ffn[l][None, :], wr, br,
                                     tm, tmd)

        lrow, lcol, tab, ctab, meta = _route(rrow, cnt, tmd, nbp)
        xs = _scatter(tab, ctab, meta, lrow, h2, tmd, nb)
        ys = _ffn(meta, xs, expert_w_gate, expert_w_up, expert_w_down, l, nb)
        final = l == depth - 1
        xf = _gather(tab, ctab, lcol, x2, norm_final[None, :], ys, tmd, final)

    return xf.reshape(batch, seq, d)


def _retile(dest, tm, tmd):
    if tm == tmd:
        return dest
    nt = dest.shape[0]
    return dest.reshape(nt, 2, tm // tmd, tmd).transpose(0, 2, 1, 3).reshape(nt * (tm // tmd), 2, tmd)
```

```python
import functools
import math

import jax
import jax.numpy as jnp
import numpy as np
from jax import lax
from jax.experimental import pallas as pl
from jax.experimental.pallas import tpu as pltpu

F32 = jnp.float32
BF16 = jnp.bfloat16
I32 = jnp.int32

LANES = 128
VMEM_LIMIT_BYTES = 56 * 1024 * 1024

D_MODEL = 1024
RMS_EPS = 1e-6
LOG2E = math.log2(math.e)
GROUP_WIDTH = 256
HEAD_DIM = 64
N_HEADS = 4

CONV_A_WIDTH = 3
SSM_CONV = 4
SSM_STATE = 64
SSM_CHUNK = 256

MLA_NOPE = 64
MLA_ROPE = 32
MLA_Q_LORA = 256
MLA_KV_LORA = 128
ROPE_BASE = 10000.0
MLA_CHUNK = 64
ATTN_TQ = 512
ATTN_RB = 128

N_EXPERT_GROUPS = 4
EXPERTS_PER_GROUP = 8
N_EXPERTS = 32
EXPERT_FF = 256
MOE_ROWS = 512
FFN_SUB = 256

SEG_A = (0, 768)
SEG_B = (768, 1280)
SEG_C = (1280, 2048)
SEG_D = (2048, 2432)
SEG_M = (2432, 2560)
SEG_M2 = (2560, 2688)
IN_COLS_PADDED = 2688
HEAD_PAD = N_HEADS * LANES
AUG_LANE = HEAD_DIM
MISC_F = 0
MISC_DT = 4
MISC_ROPE = 64
COL_CUMF = 0
COL_DT = 4
COL_ACUM = 8
N_SCALAR_ROWS = 16


def _cparams(*sem):
    return pltpu.CompilerParams(dimension_semantics=sem, vmem_limit_bytes=VMEM_LIMIT_BYTES)


def _lane_iota(shape):
    return lax.broadcasted_iota(I32, shape, len(shape) - 1)


def _row_iota(shape):
    return lax.broadcasted_iota(I32, shape, 0)


def _rms(x, g):
    ms = jnp.mean(x * x, axis=-1, keepdims=True)
    return x * lax.rsqrt(ms + RMS_EPS) * g


def _silu(x):
    return x / (1.0 + jnp.exp(-x))


def _softplus(x):
    return jnp.maximum(x, 0.0) + jnp.log(1.0 + jnp.exp(-jnp.abs(x)))


def _shift_rows(x, k):
    rolled = pltpu.roll(x, k, 0)
    return jnp.where(_row_iota(x.shape) >= k, rolled, 0.0)


def _rope_kernel(pos_ref, freq_ref, cos_ref, sin_ref):
    ang = pos_ref[...] * freq_ref[...]
    lane = _lane_iota(ang.shape)
    rope = (lane >= MISC_ROPE) & (lane < MISC_ROPE + MLA_ROPE)
    cos_ref[...] = jnp.where(rope, jnp.cos(ang), jnp.where(lane < MISC_ROPE, 1.0, 0.0))
    sin_ref[...] = jnp.where(rope, jnp.sin(ang), 0.0)


def _rope_tables(pos_col, tm):
    t = pos_col.shape[0]
    half = MLA_ROPE // 2
    inv = ROPE_BASE ** (-np.arange(0, MLA_ROPE, 2, dtype=np.float32) / MLA_ROPE)
    freq = np.zeros((1, LANES), np.float32)
    freq[0, MISC_ROPE:MISC_ROPE + half] = inv
    freq[0, MISC_ROPE + half:MISC_ROPE + MLA_ROPE] = inv
    return pl.pallas_call(
        _rope_kernel,
        grid=(t // tm,),
        in_specs=[pl.BlockSpec((tm, 1), lambda i: (i, 0)),
                  pl.BlockSpec((1, LANES), lambda i: (0, 0))],
        out_specs=[pl.BlockSpec((tm, LANES), lambda i: (i, 0))] * 2,
        out_shape=[jax.ShapeDtypeStruct((t, LANES), F32)] * 2,
        compiler_params=_cparams("parallel"),
        name="rope_tables",
    )(pos_col, jnp.asarray(freq))


def _inproj_kernel(x_ref, g_ref, w_ref, wvt_ref, oa, ob, oc, od, om, om2, ovt):
    h = _rms(x_ref[...], g_ref[...]).astype(BF16)
    for o, (lo, hi) in ((oa, SEG_A), (ob, SEG_B), (oc, SEG_C), (od, SEG_D), (om, SEG_M), (om2, SEG_M2)):
        o[...] = jnp.dot(h, w_ref[:, lo:hi], preferred_element_type=F32).astype(o.dtype)
    ovt[...] = lax.dot_general(wvt_ref[...], h, (((1,), (1,)), ((), ())),
                               preferred_element_type=F32).astype(ovt.dtype)


def _inproj(x, g, w, wvt, tm):
    t = x.shape[0]
    widths = [(s[1] - s[0]) for s in (SEG_A, SEG_B, SEG_C, SEG_D, SEG_M, SEG_M2)]
    dtypes = [BF16, BF16, BF16, BF16, F32, F32]
    return pl.pallas_call(
        _inproj_kernel,
        grid=(t // tm,),
        in_specs=[pl.BlockSpec((tm, D_MODEL), lambda i: (i, 0)),
                  pl.BlockSpec((1, D_MODEL), lambda i: (0, 0)),
                  pl.BlockSpec((D_MODEL, IN_COLS_PADDED), lambda i: (0, 0)),
                  pl.BlockSpec((GROUP_WIDTH, D_MODEL), lambda i: (0, 0))],
        out_specs=[pl.BlockSpec((tm, wd), lambda i: (i, 0)) for wd in widths]
        + [pl.BlockSpec((GROUP_WIDTH, tm), lambda i: (0, i))],
        out_shape=[jax.ShapeDtypeStruct((t, wd), dt) for wd, dt in zip(widths, dtypes)]
        + [jax.ShapeDtypeStruct((GROUP_WIDTH, t), BF16)],
        compiler_params=_cparams("parallel"),
        name="inproj",
    )(x, g, w, wvt)


def _scalar_prep_kernel(m_ref, p_ref, qk_ref, place_ref, const_ref,
                        col_ref, row_ref, qa_ref, ka_ref, tref_ref, *, tq):
    s = m_ref.shape[0]
    tref_ref[...] = jnp.zeros_like(tref_ref)
    tile_ref = jnp.zeros((1, LANES), F32)
    m = m_ref[...]
    bias = p_ref[0:1, :]
    a_log = p_ref[1:2, :]
    lane = _lane_iota(m.shape)
    z = m + bias
    logf = jnp.minimum(z, 0.0) - jnp.log(1.0 + jnp.exp(-jnp.abs(z)))
    dt = _softplus(z)
    a = dt * (-jnp.exp(a_log))
    is_f = lane < MISC_DT
    is_dt = (lane >= MISC_DT) & (lane < MISC_DT + N_HEADS)
    v = jnp.where(is_f, logf, jnp.where(is_dt, a, 0.0))
    r = _row_iota((SSM_CHUNK, SSM_CHUNK))
    c = _lane_iota((SSM_CHUNK, SSM_CHUNK))
    tril = jnp.where(r >= c, 1.0, 0.0).astype(BF16)
    carry = jnp.zeros((1, LANES), F32)
    lane_1 = _lane_iota((1, LANES))
    lane_b = _lane_iota((SSM_CHUNK, LANES))
    low = lane_b < HEAD_DIM
    aug_lanes = (lane_b >= AUG_LANE) & (lane_b < AUG_LANE + AUG_TERMS)
    for ci in range(s // SSM_CHUNK):
        rest = v[ci * SSM_CHUNK:(ci + 1) * SSM_CHUNK]
        cs = jnp.zeros((SSM_CHUNK, LANES), F32)
        for _ in range(3):
            term = rest.astype(BF16)
            cs = cs + jnp.dot(tril, term, preferred_element_type=F32)
            rest = rest - term.astype(F32)
        cs = cs + jnp.where(lane_1 < MISC_DT, carry, 0.0)
        carry = cs[SSM_CHUNK - 1:SSM_CHUNK]
        acum = pltpu.roll(cs, COL_ACUM - MISC_DT, 1)
        out = jnp.where(lane_b < MISC_DT, cs * LOG2E,
                        jnp.where(lane_b < COL_ACUM, dt[ci * SSM_CHUNK:(ci + 1) * SSM_CHUNK],
                                  jnp.where(lane_b < COL_ACUM + N_HEADS, acum, 0.0)))
        rows = slice(ci * SSM_CHUNK, (ci + 1) * SSM_CHUNK)
        col_ref[rows, :] = out
        row_ref[0, :, rows] = out.T[:N_SCALAR_ROWS]
        if (ci * SSM_CHUNK) % tq == 0:
            tile_ref = out[0:1, :]
            ti = (ci * SSM_CHUNK) // tq
            tref_ref[0, ti:ti + 1, :] = tile_ref
        c = out - tile_ref
        c_hi = c.astype(BF16)
        r1 = c - c_hi.astype(F32)
        c_mid = r1.astype(BF16)
        c_lo = (r1 - c_mid.astype(F32)).astype(BF16)
        compact = jnp.dot(jnp.concatenate([c_hi, c_mid, c_lo], axis=1), place_ref[...],
                          preferred_element_type=F32) + const_ref[0:1, :]
        for side, o_ref in enumerate((qa_ref, ka_ref)):
            dec = compact[:, side * LANES:(side + 1) * LANES]
            for h in range(N_HEADS):
                pair = qk_ref[rows, side * GROUP_WIDTH + (h // 2) * LANES:
                              side * GROUP_WIDTH + (h // 2 + 1) * LANES].astype(F32)
                feat = pair if h % 2 == 0 else pltpu.roll(pair, HEAD_DIM, 1)
                dec_h = pltpu.roll(dec, AUG_LANE - AUG_TERMS * h, 1)
                group = jnp.where(low, feat, jnp.where(aug_lanes, dec_h, 0.0))
                o_ref[rows, h * LANES:(h + 1) * LANES] = group.astype(o_ref.dtype)


AUG_TERMS = 6


def _fox_placement():
    place = np.zeros((3 * LANES, 2 * LANES), np.float32)
    const = np.zeros((8, 2 * LANES), np.float32)
    for h in range(N_HEADS):
        a0 = AUG_TERMS * h
        for term in range(3):
            place[term * LANES + COL_CUMF + h, a0 + term] = 1.0
            place[term * LANES + COL_CUMF + h, LANES + a0 + 3 + term] = -1.0
            const[0, a0 + 3 + term] = 1.0
            const[0, LANES + a0 + term] = 1.0
    return jnp.asarray(place, BF16), jnp.asarray(const, F32)


def _scalar_prep(misc, params, qk, batch, seq, tq):
    place, const = _fox_placement()
    full = lambda a: pl.BlockSpec(a.shape, lambda b: (0,) * a.ndim)
    return pl.pallas_call(
        functools.partial(_scalar_prep_kernel, tq=tq),
        grid=(batch,),
        in_specs=[pl.BlockSpec((seq, LANES), lambda b: (b, 0)),
                  pl.BlockSpec((8, LANES), lambda b: (0, 0)),
                  pl.BlockSpec((seq, 2 * GROUP_WIDTH), lambda b: (b, 0)),
                  full(place), full(const)],
        out_specs=[pl.BlockSpec((seq, LANES), lambda b: (b, 0)),
                   pl.BlockSpec((1, N_SCALAR_ROWS, seq), lambda b: (b, 0, 0)),
                   pl.BlockSpec((seq, HEAD_PAD), lambda b: (b, 0)),
                   pl.BlockSpec((seq, HEAD_PAD), lambda b: (b, 0)),
                   pl.BlockSpec((1, 8, LANES), lambda b: (b, 0, 0))],
        out_shape=[jax.ShapeDtypeStruct((batch * seq, LANES), F32),
                   jax.ShapeDtypeStruct((batch, N_SCALAR_ROWS, seq), F32),
                   jax.ShapeDtypeStruct((batch * seq, HEAD_PAD), BF16),
                   jax.ShapeDtypeStruct((batch * seq, HEAD_PAD), BF16),
                   jax.ShapeDtypeStruct((batch, 8, LANES), F32)],
        compiler_params=_cparams("parallel"),
        name="scalar_prep",
    )(misc, params, qk, place, const)


def _conv_mixer_kernel(p_ref, w_ref, o_ref):
    gw = GROUP_WIDTH
    b_gate = p_ref[:, 0:gw].astype(F32)
    cv = p_ref[:, gw:2 * gw].astype(F32) * p_ref[:, 2 * gw:3 * gw].astype(F32)
    acc = cv * w_ref[CONV_A_WIDTH - 1:CONV_A_WIDTH, :]
    for k in range(1, CONV_A_WIDTH):
        acc = acc + _shift_rows(cv, k) * w_ref[CONV_A_WIDTH - 1 - k:CONV_A_WIDTH - k, :]
    o_ref[...] = (b_gate * acc).astype(o_ref.dtype)


def _conv_mixer(pa, w, batch, seq):
    return pl.pallas_call(
        _conv_mixer_kernel,
        grid=(batch,),
        in_specs=[pl.BlockSpec((seq, 3 * GROUP_WIDTH), lambda b: (b, 0)),
                  pl.BlockSpec((8, GROUP_WIDTH), lambda b: (0, 0))],
        out_specs=pl.BlockSpec((seq, GROUP_WIDTH), lambda b: (b, 0)),
        out_shape=jax.ShapeDtypeStruct((batch * seq, GROUP_WIDTH), BF16),
        compiler_params=_cparams("parallel"),
        name="conv_mixer",
    )(pa, w)


def _pair_lanes(col, base, shape):
    lane = _lane_iota(shape)
    return jnp.where(lane < HEAD_DIM, col[:, base:base + 1], col[:, base + 1:base + 2])


def _ssd_kernel(p_ref, col_ref, row_ref, cw_ref, par_ref, o_ref, u_ref):
    s = p_ref.shape[0]
    q = SSM_CHUNK
    gw = GROUP_WIDTH
    xbc = p_ref[:, gw:3 * gw].astype(F32)
    acc = xbc * cw_ref[SSM_CONV - 1:SSM_CONV, :]
    for k in range(1, SSM_CONV):
        acc = acc + _shift_rows(xbc, k) * cw_ref[SSM_CONV - 1 - k:SSM_CONV - k, :]
    u_ref[...] = _silu(acc + cw_ref[SSM_CONV:SSM_CONV + 1, :])

    d_skip = par_ref[0:1, :]
    norm_g = par_ref[1:2, :]
    lane_q = _lane_iota((q, LANES))
    low = lane_q < HEAD_DIM
    tri = _row_iota((q, q)) >= _lane_iota((q, q))

    def chunk(ci, states):
        rows = pl.ds(ci * q, q)
        u = u_ref[rows, :]
        col = col_ref[rows, :]
        bm = u[:, gw:gw + LANES]
        cm = u[:, gw + LANES:gw + 2 * LANES]
        z = p_ref[rows, 0:gw].astype(F32)
        new_states = []
        ys = []
        for g in range(2):
            sel = low if g == 0 else jnp.logical_not(low)
            cg = jnp.where(sel, cm, 0.0).astype(BF16)
            bg = jnp.where(sel, bm, 0.0)
            gmat = lax.dot_general(cg, bm.astype(BF16), (((1,), (1,)), ((), ())),
                                   preferred_element_type=F32)
            xs = u[:, g * LANES:(g + 1) * LANES]
            dt2 = _pair_lanes(col, COL_DT + 2 * g, (q, LANES))
            ac2 = _pair_lanes(col, COL_ACUM + 2 * g, (q, LANES))
            xdt = xs * dt2
            xdt_b = xdt.astype(BF16)
            st = states[g]
            y_off = jnp.dot(cg, st.astype(BF16), preferred_element_type=F32) * jnp.exp(ac2)
            halves = []
            for hh in range(2):
                h = 2 * g + hh
                ac_col = col[:, COL_ACUM + h:COL_ACUM + h + 1]
                ac_row = row_ref[0, COL_ACUM + h:COL_ACUM + h + 1, rows]
                decay = jnp.exp(jnp.where(tri, ac_col - ac_row, -1e30))
                mm = (gmat * decay).astype(BF16)
                halves.append(jnp.dot(mm, xdt_b, preferred_element_type=F32))
            y = jnp.where(low, halves[0], halves[1]) + y_off + d_skip[:, g * LANES:(g + 1) * LANES] * xs
            ys.append(y)
            ac_last = ac2[q - 1:q, :]
            w_end = jnp.exp(ac_last - ac2)
            xw = (xdt * w_end).astype(BF16)
            upd = jnp.dot(bg.T.astype(BF16), xw, preferred_element_type=F32)
            new_states.append(st * jnp.exp(ac_last) + upd)
        yfull = jnp.concatenate(ys, axis=1) * _silu(z)
        o_ref[rows, :] = _rms(yfull, norm_g).astype(o_ref.dtype)
        return tuple(new_states)

    init = (jnp.zeros((LANES, LANES), F32), jnp.zeros((LANES, LANES), F32))
    states = init
    for ci in range(s // q):
        states = chunk(ci, states)


def _ssd_mixer(pc, col, rows, conv_wb, par, batch, seq):
    gw = GROUP_WIDTH
    return pl.pallas_call(
        _ssd_kernel,
        grid=(batch,),
        in_specs=[pl.BlockSpec((seq, 3 * gw), lambda b: (b, 0)),
                  pl.BlockSpec((seq, LANES), lambda b: (b, 0)),
                  pl.BlockSpec((1, N_SCALAR_ROWS, seq), lambda b: (b, 0, 0)),
                  pl.BlockSpec((8, 2 * gw), lambda b: (0, 0)),
                  pl.BlockSpec((8, gw), lambda b: (0, 0))],
        out_specs=pl.BlockSpec((seq, gw), lambda b: (b, 0)),
        out_shape=jax.ShapeDtypeStruct((batch * seq, gw), BF16),
        scratch_shapes=[pltpu.VMEM((seq, 2 * gw), F32)],
        compiler_params=_cparams("parallel"),
        name="ssd_mixer",
    )(pc, col, rows, conv_wb, par)


def _mla_prep_kernel(pd_ref, m_ref, m2_ref, cos_ref, sin_ref, nq_ref, nkv_ref,
                     wq_ref, wqs_ref, wk_ref, wvt_ref, q_ref, k_ref, vt_ref):
    cq = _rms(pd_ref[:, 0:MLA_Q_LORA].astype(F32), nq_ref[...]).astype(BF16)
    ckv = _rms(pd_ref[:, MLA_Q_LORA:MLA_Q_LORA + MLA_KV_LORA].astype(F32), nkv_ref[...]).astype(BF16)
    cos = cos_ref[...]
    sin = sin_ref[...]
    cos4 = jnp.concatenate([cos] * N_HEADS, axis=1)
    sin4 = jnp.concatenate([sin] * N_HEADS, axis=1)
    scale = (MLA_NOPE + MLA_ROPE) ** -0.5 * LOG2E
    q = jnp.dot(cq, wq_ref[...], preferred_element_type=F32)
    qs = jnp.dot(cq, wqs_ref[...], preferred_element_type=F32)
    q_ref[...] = ((q * cos4 + qs * sin4) * scale).astype(q_ref.dtype)
    lane = _lane_iota(cos.shape)
    rope = (lane >= MISC_ROPE) & (lane < MISC_ROPE + MLA_ROPE)
    kr = jnp.where(rope, m_ref[...] * cos + m2_ref[...] * sin, 0.0)
    k = jnp.dot(ckv, wk_ref[...], preferred_element_type=F32)
    k_ref[...] = (k + jnp.concatenate([kr] * N_HEADS, axis=1)).astype(k_ref.dtype)
    vt_ref[...] = lax.dot_general(wvt_ref[...], ckv, (((1,), (1,)), ((), ())),
                                  preferred_element_type=F32).astype(vt_ref.dtype)


def _mla_prep(pd, misc, misc2, cos, sin, nq, nkv, wq, wqs, wk, wv, tm):
    t = pd.shape[0]
    hp = N_HEADS * LANES
    full = lambda a: pl.BlockSpec(a.shape, lambda i: (0, 0))
    tile = lambda w: pl.BlockSpec((tm, w), lambda i: (i, 0))
    return pl.pallas_call(
        _mla_prep_kernel,
        grid=(t // tm,),
        in_specs=[tile(MLA_Q_LORA + MLA_KV_LORA), tile(LANES), tile(LANES), tile(LANES), tile(LANES),
                  full(nq), full(nkv), full(wq), full(wqs), full(wk), full(wv)],
        out_specs=[tile(hp), tile(hp), pl.BlockSpec((GROUP_WIDTH, tm), lambda i: (0, i))],
        out_shape=[jax.ShapeDtypeStruct((t, hp), BF16), jax.ShapeDtypeStruct((t, hp), BF16),
                   jax.ShapeDtypeStruct((GROUP_WIDTH, t), BF16)],
        compiler_params=_cparams("parallel"),
        name="mla_prep",
    )(pd, misc, misc2, cos, sin, nq, nkv, wq, wqs, wk, wv)


def _attn_kernel(*refs, fox, tq):
    if fox:
        tref_ref, q_ref, k_ref, vt_ref, o_ref = refs
    else:
        q_ref, k_ref, vt_ref, o_ref = refs
        tref_ref = None
    b = pl.program_id(0)
    i = pl.program_id(1)
    key = _row_iota((tq, tq))
    qry = _lane_iota((tq, tq))
    if fox:
        allowed = key <= qry
    else:
        shift = int(math.log2(MLA_CHUNK))
        allowed = (key >> shift) <= (qry >> shift)
    qs = [q_ref[:, h * LANES:(h + 1) * LANES] for h in range(N_HEADS)]

    def step(j, masked, carry):
        rk = pl.ds(pl.multiple_of(j * tq, tq), tq)
        scores = [lax.dot_general(k_ref[rk, h * LANES:(h + 1) * LANES], qs[h], (((1,), (1,)), ((), ())),
                                  preferred_element_type=F32) for h in range(N_HEADS)]
        probs = []
        for h in range(N_HEADS):
            m, l, _ = carry[h]
            s = scores[h]
            if masked:
                s = jnp.where(allowed, s, -1e30)
            delta = (tref_ref[b, i, h] - tref_ref[b, j, h]) if fox else 0.0
            m_new = jnp.maximum(m, jnp.max(s, axis=0, keepdims=True) + delta)
            alpha = jnp.exp2(m - m_new)
            p = jnp.exp2(s - (m_new - delta))
            l_new = alpha * l + jnp.sum(p, axis=0, keepdims=True)
            probs.append((m_new, l_new, alpha, p.astype(BF16)))
        new = []
        for h in range(N_HEADS):
            pair = h // 2
            m_new, l_new, alpha, p = probs[h]
            pv = jnp.dot(vt_ref[pair * LANES:(pair + 1) * LANES, rk], p, preferred_element_type=F32)
            new.append((m_new, l_new, alpha * carry[h][2] + pv))
        return tuple(new)

    init = tuple((jnp.full((1, tq), -1e30, F32), jnp.zeros((1, tq), F32), jnp.zeros((LANES, tq), F32))
                 for _ in range(N_HEADS))
    carry = lax.fori_loop(0, i, lambda j, c: step(j, False, c), init)
    carry = step(i, True, carry)
    outs = [acc / l for (_, l, acc) in carry]
    top = _row_iota((LANES, tq)) < HEAD_DIM
    o_t = jnp.concatenate([jnp.where(top, outs[0], outs[1]), jnp.where(top, outs[2], outs[3])], axis=0)
    o_ref[...] = o_t.T.astype(o_ref.dtype)


def _attention(q, k, vt, tref, batch, seq, tq, name):
    nq = seq // tq
    fox = tref is not None
    kern = functools.partial(_attn_kernel, fox=fox, tq=tq)
    grid_spec = pltpu.PrefetchScalarGridSpec(
        num_scalar_prefetch=1 if fox else 0,
        grid=(batch, nq),
        in_specs=[pl.BlockSpec((tq, HEAD_PAD), lambda b, i, *_: (b * nq + i, 0)),
                  pl.BlockSpec((seq, HEAD_PAD), lambda b, i, *_: (b, 0)),
                  pl.BlockSpec((GROUP_WIDTH, seq), lambda b, i, *_: (0, b))],
        out_specs=pl.BlockSpec((tq, GROUP_WIDTH), lambda b, i, *_: (b * nq + i, 0)),
    )
    args = ((tref,) if fox else ()) + (q, k, vt)
    return pl.pallas_call(
        kern,
        grid_spec=grid_spec,
        out_shape=jax.ShapeDtypeStruct((batch * seq, GROUP_WIDTH), BF16),
        compiler_params=_cparams("parallel", "arbitrary"),
        name=name,
    )(*args)


def _outproj_kernel(x_ref, ya, yb, yc, yd, w_ref, g_ref, wr_ref, br_ref,
                    x2_ref, h2_ref, rrow_ref, cnt_ref, *, tm, moe_tile):
    y = jnp.concatenate([ya[...], yb[...], yc[...], yd[...]], axis=1)
    x2 = x_ref[...] + jnp.dot(y, w_ref[...], preferred_element_type=F32)
    x2_ref[...] = x2
    h2 = _rms(x2, g_ref[...])
    h2_ref[...] = h2.astype(h2_ref.dtype)
    h_hi = h2.astype(BF16)
    h_lo = (h2 - h_hi.astype(F32)).astype(BF16)
    part = jnp.dot(h_hi, wr_ref[...], preferred_element_type=F32)
    logits = (part[:, 0:LANES] + part[:, LANES:2 * LANES]
              + jnp.dot(h_lo, wr_ref[:, 0:LANES], preferred_element_type=F32) + br_ref[...])
    lt = logits.T
    row = _row_iota(lt.shape)
    neg = -1e30
    big = 1 << 20
    gmask = (row >= N_EXPERTS) & (row < N_EXPERTS + N_EXPERT_GROUPS)
    gl = jnp.where(gmask, lt, neg)
    gmax = jnp.max(gl, axis=0, keepdims=True)
    gsum = jnp.sum(jnp.where(gmask, jnp.exp(gl - gmax), 0.0), axis=0, keepdims=True)
    g_w = 1.0 / gsum
    g_idx = jnp.min(jnp.where(gmask & (gl == gmax), row, big), axis=0, keepdims=True) - N_EXPERTS
    emask = (row < N_EXPERTS) & ((row >> int(math.log2(EXPERTS_PER_GROUP))) == g_idx)
    el = jnp.where(emask, lt, neg)
    e1v = jnp.max(el, axis=0, keepdims=True)
    esum = jnp.sum(jnp.where(emask, jnp.exp(el - e1v), 0.0), axis=0, keepdims=True)
    i1 = jnp.min(jnp.where(emask & (el == e1v), row, big), axis=0, keepdims=True)
    el2 = jnp.where(row == i1, neg, el)
    e2v = jnp.max(el2, axis=0, keepdims=True)
    i2 = jnp.min(jnp.where(emask & (row != i1) & (el2 == e2v), row, big), axis=0, keepdims=True)
    p1 = 1.0 / esum
    p2 = jnp.exp(e2v - e1v) / esum
    w1 = g_w * (p1 / (p1 + p2))
    w2 = g_w * (p2 / (p1 + p2))
    out_row = _row_iota(rrow_ref.shape)
    rrow_ref[...] = jnp.where(out_row == 0, i1.astype(F32),
                              jnp.where(out_row == 1, i2.astype(F32),
                                        jnp.where(out_row == 2, w1, jnp.where(out_row == 3, w2, 0.0))))
    step = pl.program_id(0)

    @pl.when(step == 0)
    def _():
        cnt_ref[...] = jnp.zeros_like(cnt_ref)

    chosen = jnp.where((row == i1) | (row == i2), 1.0, 0.0).astype(BF16)
    tiles = tm // moe_tile
    tile_of = (_row_iota((tm, LANES)) >> int(math.log2(moe_tile))) + step * tiles
    to_tile = jnp.where(_lane_iota((tm, LANES)) == tile_of, 1.0, 0.0).astype(BF16)
    counts = jnp.dot(chosen, to_tile, preferred_element_type=F32)
    cnt_ref[...] += counts[0:N_EXPERTS]


def _outproj(x, ya, yb, yc, yd, w, g, wr, br, tm, moe_tile):
    t = x.shape[0]
    full = lambda a: pl.BlockSpec(a.shape, lambda i: (0, 0))
    tile = lambda wd: pl.BlockSpec((tm, wd), lambda i: (i, 0))
    return pl.pallas_call(
        functools.partial(_outproj_kernel, tm=tm, moe_tile=moe_tile),
        grid=(t // tm,),
        in_specs=[tile(D_MODEL)] + [tile(GROUP_WIDTH)] * 4 + [full(w), full(g), full(wr), full(br)],
        out_specs=[tile(D_MODEL), tile(D_MODEL), pl.BlockSpec((8, tm), lambda i: (0, i)),
                   pl.BlockSpec((N_EXPERTS, LANES), lambda i: (0, 0))],
        out_shape=[jax.ShapeDtypeStruct((t, D_MODEL), F32), jax.ShapeDtypeStruct((t, D_MODEL), BF16),
                   jax.ShapeDtypeStruct((8, t), F32), jax.ShapeDtypeStruct((N_EXPERTS, LANES), F32)],
        compiler_params=_cparams("arbitrary"),
        name="outproj_router",
    )(x, ya, yb, yc, yd, w, g, wr, br)


def _positions_kernel(r_ref, dest_ref, meta_ref, cnt_ref, carry_ref, start_ref, *, tm, nbp):
    phase = pl.program_id(0)
    i = pl.program_id(1)
    e_iota = _row_iota((N_EXPERTS, tm))
    e0 = r_ref[0:1, :].astype(I32)
    e1 = r_ref[1:2, :].astype(I32)
    oh0 = e_iota == e0
    oh1 = e_iota == e1
    oh = jnp.where(oh0 | oh1, 1.0, 0.0)

    @pl.when((phase == 0) & (i == 0))
    def _():
        cnt_ref[...] = jnp.zeros_like(cnt_ref)

    @pl.when(phase == 0)
    def _():
        cnt_ref[...] += jnp.sum(oh, axis=-1, keepdims=True)

    @pl.when((phase == 1) & (i == 0))
    def _():
        cnt = cnt_ref[...]
        padded = jnp.floor((cnt + (MOE_ROWS - 1)) * (1.0 / MOE_ROWS)) * MOE_ROWS
        tril = jnp.where(_row_iota((N_EXPERTS, N_EXPERTS)) >= _lane_iota((N_EXPERTS, N_EXPERTS)), 1.0, 0.0)
        pend = jnp.dot(tril, padded, preferred_element_type=F32, precision=lax.Precision.HIGHEST)
        pstart = pend - padded
        start_ref[...] = pstart
        carry_ref[...] = jnp.zeros_like(carry_ref)
        pend_b = jnp.concatenate([pend] * (nbp // LANES), axis=1)
        vend_b = jnp.concatenate([pstart + cnt] * (nbp // LANES), axis=1)
        b0 = (_lane_iota((N_EXPERTS, nbp)) * MOE_ROWS).astype(F32)
        bexp = jnp.sum(jnp.where(pend_b <= b0, 1.0, 0.0), axis=0, keepdims=True)
        bexp = jnp.minimum(bexp, N_EXPERTS - 1.0)
        is_e = _row_iota((N_EXPERTS, nbp)).astype(F32) == bexp
        vend = jnp.sum(jnp.where(is_e, vend_b, 0.0), axis=0, keepdims=True)
        nvalid = jnp.clip(vend - b0[0:1], 0.0, float(MOE_ROWS))
        total = jnp.max(pend_b, axis=0, keepdims=True) * (1.0 / MOE_ROWS)
        row = _row_iota((8, nbp))
        meta = jnp.where(row == 0, bexp, jnp.where(row == 1, nvalid, jnp.where(row == 2, total, 0.0)))
        meta_ref[...] = meta.astype(I32)

    @pl.when(phase == 1)
    def _():
        su = jnp.where(_row_iota((tm, tm)) < _lane_iota((tm, tm)), 1.0, 0.0).astype(BF16)
        before = jnp.dot(oh.astype(BF16), su, preferred_element_type=F32)
        base = start_ref[:, 0:1] + carry_ref[:, 0:1] + before
        d0 = jnp.sum(jnp.where(oh0, base, 0.0), axis=0, keepdims=True)
        d1 = jnp.sum(jnp.where(oh1, base, 0.0), axis=0, keepdims=True)
        dest_ref[0, 0:1, :] = d0.astype(I32)
        dest_ref[0, 1:2, :] = d1.astype(I32)
        carry_ref[...] += jnp.sum(oh, axis=-1, keepdims=True)


def _positions(rrow, tm, nbp):
    t = rrow.shape[1]
    nt = t // tm
    kern = functools.partial(_positions_kernel, tm=tm, nbp=nbp)
    return pl.pallas_call(
        kern,
        grid=(2, nt),
        in_specs=[pl.BlockSpec((8, tm), lambda p, i: (0, i))],
        out_specs=[pl.BlockSpec((1, 2, tm), lambda p, i: (i * p, 0, 0)),
                   pl.BlockSpec((8, nbp), lambda p, i: (0, 0))],
        out_shape=[jax.ShapeDtypeStruct((nt, 2, tm), I32), jax.ShapeDtypeStruct((8, nbp), I32)],
        scratch_shapes=[pltpu.VMEM((N_EXPERTS, LANES), F32)] * 3,
        compiler_params=_cparams("arbitrary", "arbitrary"),
        name="moe_positions",
    )(rrow)


def _dispatch_kernel(meta_ref, dest_ref, h_ref, xs_ref, zero_ref, sem, zsem, *, tm, nb):
    i = pl.program_id(0)

    @pl.when(i == 0)
    def _():
        zero_ref[...] = jnp.zeros_like(zero_ref)
        n_used = meta_ref[2, 0]

        def zcopy(b):
            return pltpu.make_async_copy(zero_ref, xs_ref.at[pl.ds(b * MOE_ROWS, MOE_ROWS)], zsem)

        def needs(b):
            return (b < n_used) & (meta_ref[1, b] < MOE_ROWS)

        def start(b, c):
            @pl.when(needs(b))
            def _():
                zcopy(b).start()
            return c

        def wait(b, c):
            @pl.when(needs(b))
            def _():
                zcopy(b).wait()
            return c

        lax.fori_loop(0, nb, start, 0)
        lax.fori_loop(0, nb, wait, 0)

    def copy(t, k):
        return pltpu.make_async_copy(h_ref.at[pl.ds(t, 1)], xs_ref.at[pl.ds(dest_ref[0, k, t], 1)], sem)

    def start(t, c):
        copy(t, 0).start()
        copy(t, 1).start()
        return c

    def wait(t, c):
        copy(t, 0).wait()
        copy(t, 1).wait()
        return c

    lax.fori_loop(0, tm, start, 0)
    lax.fori_loop(0, tm, wait, 0)


def _dispatch(meta, dest, h2, tm, nb):
    t = h2.shape[0]
    kern = functools.partial(_dispatch_kernel, tm=tm, nb=nb)
    grid_spec = pltpu.PrefetchScalarGridSpec(
        num_scalar_prefetch=1,
        grid=(t // tm,),
        in_specs=[pl.BlockSpec((1, 2, tm), lambda i, m: (i, 0, 0), memory_space=pltpu.SMEM),
                  pl.BlockSpec((tm, D_MODEL), lambda i, m: (i, 0))],
        out_specs=pl.BlockSpec(memory_space=pl.ANY),
        scratch_shapes=[pltpu.VMEM((MOE_ROWS, D_MODEL), F32),
                        pltpu.SemaphoreType.DMA, pltpu.SemaphoreType.DMA],
    )
    return pl.pallas_call(
        kern,
        grid_spec=grid_spec,
        out_shape=jax.ShapeDtypeStruct((nb * MOE_ROWS, D_MODEL), F32),
        compiler_params=_cparams("arbitrary"),
        name="moe_dispatch",
    )(meta, dest, h2)


def _expert_kernel(meta_ref, x_ref, wg_ref, wu_ref, wd_ref, o_ref):
    b = pl.program_id(0)

    @pl.when(b < meta_ref[2, 0])
    def _():
        x = x_ref[...].astype(BF16)
        gate = jnp.dot(x, wg_ref[0, 0].astype(BF16), preferred_element_type=F32)
        up = jnp.dot(x, wu_ref[0, 0].astype(BF16), preferred_element_type=F32)
        act = (_silu(gate) * up).astype(BF16)
        o_ref[...] = jnp.dot(act, wd_ref[0, 0].astype(BF16), preferred_element_type=F32)


def _experts(meta, xs, wg, wu, wd, layer, nb):
    def blk(b, m):
        return (jnp.minimum(b, m[2, 0] - 1), 0)

    def wblk(b, m):
        return (layer, m[0, jnp.minimum(b, m[2, 0] - 1)], 0, 0)

    grid_spec = pltpu.PrefetchScalarGridSpec(
        num_scalar_prefetch=1,
        grid=(nb,),
        in_specs=[pl.BlockSpec((MOE_ROWS, D_MODEL), blk),
                  pl.BlockSpec((1, 1, D_MODEL, EXPERT_FF), wblk),
                  pl.BlockSpec((1, 1, D_MODEL, EXPERT_FF), wblk),
                  pl.BlockSpec((1, 1, EXPERT_FF, D_MODEL), wblk)],
        out_specs=pl.BlockSpec((MOE_ROWS, D_MODEL), blk),
    )
    return pl.pallas_call(
        _expert_kernel,
        grid_spec=grid_spec,
        out_shape=jax.ShapeDtypeStruct((nb * MOE_ROWS, D_MODEL), F32),
        compiler_params=_cparams("arbitrary"),
        name="moe_experts",
    )(meta, xs, wg, wu, wd)


def _combine_kernel(dest_ref, x_ref, r_ref, g_ref, ys_ref, o_ref, buf_ref, sem, *, tm, final):
    def copy(t, k):
        return pltpu.make_async_copy(ys_ref.at[pl.ds(dest_ref[0, k, t], 1)],
                                     buf_ref.at[k, pl.ds(t, 1)], sem)

    def start(t, c):
        copy(t, 0).start()
        copy(t, 1).start()
        return c

    def wait(t, c):
        copy(t, 0).wait()
        copy(t, 1).wait()
        return c

    lax.fori_loop(0, tm, start, 0)
    lax.fori_loop(0, tm, wait, 0)
    w0 = r_ref[:, 2:3]
    w1 = r_ref[:, 3:4]
    x = x_ref[...] + (buf_ref[0] * w0 + buf_ref[1] * w1)
    o_ref[...] = _rms(x, g_ref[...]) if final else x


def _combine(dest, x2, rcol, g, ys, tm, final):
    t = x2.shape[0]
    kern = functools.partial(_combine_kernel, tm=tm, final=final)
    return pl.pallas_call(
        kern,
        grid=(t // tm,),
        in_specs=[pl.BlockSpec((1, 2, tm), lambda i: (i, 0, 0), memory_space=pltpu.SMEM),
                  pl.BlockSpec((tm, D_MODEL), lambda i: (i, 0)),
                  pl.BlockSpec((tm, LANES), lambda i: (i, 0)),
                  pl.BlockSpec((1, D_MODEL), lambda i: (0, 0)),
                  pl.BlockSpec(memory_space=pl.ANY)],
        out_specs=pl.BlockSpec((tm, D_MODEL), lambda i: (i, 0)),
        out_shape=jax.ShapeDtypeStruct((t, D_MODEL), F32),
        scratch_shapes=[pltpu.VMEM((2, tm, D_MODEL), F32), pltpu.SemaphoreType.DMA],
        compiler_params=_cparams("arbitrary"),
        name="moe_combine",
    )(dest, x2, rcol, g, ys)


MOE_TILE = 256
CHUNK = 8
LOCAL_ROWS = 2 * MOE_TILE + 256
PACKED = D_MODEL // 2
XS_WIDTH = PACKED
U32 = jnp.uint32


def _pack_bf16_pairs(x, exact=False):
    if not exact:
        x = x.astype(BF16).astype(F32)
    half = x.shape[1] // 2
    lo = lax.bitcast_convert_type(x[:, :half], U32)
    hi = lax.bitcast_convert_type(x[:, half:], U32)
    return hi | (lo >> 16)


def _unpack_bf16_pairs(words):
    lo = lax.bitcast_convert_type(words << 16, F32)
    hi = lax.bitcast_convert_type(words & U32(0xFFFF0000), F32)
    return jnp.concatenate([lo, hi], axis=1).astype(BF16)
TAB_CHUNKS, TAB_LOCAL, TAB_GLOBAL, TAB_TOTAL, TAB_EXPERT = 0, 1, 2, 3, 4
N_TABS = 5
COPY_ROWS = (2 * CHUNK, CHUNK)


def _route_kernel(r_ref, cnt_ref, lrow_ref, lcol_ref, tab_ref, ctab_ref, meta_ref,
                  loff_ref, goff_ref, n8_ref, *, tm, nbp):
    i = pl.program_id(0)
    e_iota = _row_iota((N_EXPERTS, tm))
    oh0 = e_iota == r_ref[0:1, :].astype(I32)
    oh1 = e_iota == r_ref[1:2, :].astype(I32)
    oh = jnp.where(oh0 | oh1, 1.0, 0.0)
    tile_lane = _lane_iota((N_EXPERTS, LANES)) == i
    hi = lax.Precision.HIGHEST

    @pl.when(i == 0)
    def _():
        cnt = cnt_ref[...]
        n8 = jnp.floor((cnt + (CHUNK - 1)) * (1.0 / CHUNK)) * CHUNK
        er = _row_iota((N_EXPERTS, N_EXPERTS))
        ec = _lane_iota((N_EXPERTS, N_EXPERTS))
        below = jnp.where(er > ec, 1.0, 0.0)
        loff = jnp.dot(below, n8, preferred_element_type=F32, precision=hi)
        rows_e = jnp.sum(n8, axis=-1, keepdims=True) + jnp.zeros_like(n8)
        padded = jnp.floor((rows_e + (MOE_ROWS - 1)) * (1.0 / MOE_ROWS)) * MOE_ROWS
        e_start = jnp.dot(below, padded, preferred_element_type=F32, precision=hi)
        tr = _row_iota((LANES, LANES))
        tc = _lane_iota((LANES, LANES))
        earlier = jnp.where(tr < tc, 1.0, 0.0)
        goff = e_start + jnp.dot(n8, earlier, preferred_element_type=F32, precision=hi)
        loff_ref[...] = loff
        goff_ref[...] = goff
        n8_ref[...] = n8
        tab_ref[TAB_CHUNKS] = (n8 * (1.0 / CHUNK)).astype(I32)
        tab_ref[TAB_LOCAL] = loff.astype(I32)
        tab_ref[TAB_GLOBAL] = goff.astype(I32)
        big = jnp.floor(n8 * (0.5 / CHUNK))
        small = n8 * (1.0 / CHUNK) - 2.0 * big
        row_t = _row_iota(n8.shape)
        tab_ref[TAB_TOTAL] = jnp.where(row_t == 0, jnp.sum(big, axis=0, keepdims=True),
                                       jnp.where(row_t == 1, jnp.sum(small, axis=0, keepdims=True),
                                                 0.0)).astype(I32)
        lane_t = _lane_iota(n8.shape)
        tab_ref[TAB_EXPERT] = jnp.where(lane_t == 0, e_start, jnp.where(lane_t == 1, rows_e, 0.0)).astype(I32)
        reps = nbp // LANES
        pend_b = jnp.concatenate([e_start + padded] * reps, axis=1)
        vend_b = jnp.concatenate([e_start + rows_e] * reps, axis=1)
        used_b = jnp.concatenate([padded] * reps, axis=1) > 0.0
        b0 = (_lane_iota((N_EXPERTS, nbp)) * MOE_ROWS).astype(F32)
        bexp = jnp.sum(jnp.where(pend_b <= b0, 1.0, 0.0), axis=0, keepdims=True)
        bexp = jnp.minimum(bexp, N_EXPERTS - 1.0)
        e_b = _row_iota((N_EXPERTS, nbp)).astype(F32)
        is_e = e_b == bexp
        vend = jnp.sum(jnp.where(is_e, vend_b, 0.0), axis=0, keepdims=True)
        nvalid = jnp.clip(vend - b0[0:1], 0.0, float(MOE_ROWS))
        total = jnp.max(pend_b, axis=0, keepdims=True) * (1.0 / MOE_ROWS)
        order = jnp.sum(jnp.where(used_b & (e_b < bexp), 1.0, 0.0), axis=0, keepdims=True)
        nxt = jnp.min(jnp.where(used_b & (e_b > bexp), e_b, float(N_EXPERTS)), axis=0, keepdims=True)
        row = _row_iota((8, nbp))
        meta = jnp.where(row == 0, bexp, jnp.where(row == 1, nvalid, jnp.where(row == 2, total,
                         jnp.where(row == 3, order, jnp.where(row == 4, nxt, 0.0)))))
        meta_ref[...] = meta.astype(I32)

    su = jnp.where(_row_iota((tm, tm)) < _lane_iota((tm, tm)), 1.0, 0.0).astype(BF16)
    before = jnp.dot(oh.astype(BF16), su, preferred_element_type=F32)
    base = jnp.sum(jnp.where(tile_lane, loff_ref[...], 0.0), axis=-1, keepdims=True) + before
    d0 = jnp.sum(jnp.where(oh0, base, 0.0), axis=0, keepdims=True)
    d1 = jnp.sum(jnp.where(oh1, base, 0.0), axis=0, keepdims=True)
    lrow_ref[0, 0:1, :] = d0.astype(I32)
    lrow_ref[0, 1:2, :] = d1.astype(I32)
    pick_tile = lambda ref: jnp.sum(jnp.where(tile_lane, ref[...], 0.0), axis=-1, keepdims=True)
    nch = pick_tile(n8_ref) * (1.0 / CHUNK)
    n_big = jnp.floor(nch * 0.5)
    n_small = nch - 2.0 * n_big
    incl = jnp.where(_row_iota((N_EXPERTS, N_EXPERTS)) >= _lane_iota((N_EXPERTS, N_EXPERTS)), 1.0, 0.0)
    cidx = _lane_iota((N_EXPERTS, LANES)).astype(F32)
    loff_t = pick_tile(loff_ref)
    goff_t = pick_tile(goff_ref)
    for k, (n, rows, first) in enumerate(((n_big, COPY_ROWS[0], 0.0), (n_small, COPY_ROWS[1], n_big * COPY_ROWS[0]))):
        cend = jnp.dot(incl, n + jnp.zeros((N_EXPERTS, LANES), F32), preferred_element_type=F32, precision=hi)
        cstart = cend - n
        mine = (cidx >= cstart) & (cidx < cend)
        step_rows = first + (cidx - cstart) * rows
        ctab_ref[0, 2 * k:2 * k + 1, :] = jnp.sum(jnp.where(mine, loff_t + step_rows, 0.0), axis=0,
                                                  keepdims=True).astype(I32)
        ctab_ref[0, 2 * k + 1:2 * k + 2, :] = jnp.sum(jnp.where(mine, goff_t + step_rows, 0.0), axis=0,
                                                      keepdims=True).astype(I32)
    row = _row_iota((LANES, tm))
    lcol_ref[...] = jnp.where(row == 0, d0, jnp.where(row == 1, d1,
                              jnp.where(row == 2, r_ref[2:3, :], jnp.where(row == 3, r_ref[3:4, :], 0.0)))).T


def _route(rrow, cnt, tm, nbp):
    t = rrow.shape[1]
    nt = t // tm
    kern = functools.partial(_route_kernel, tm=tm, nbp=nbp)
    return pl.pallas_call(
        kern,
        grid=(nt,),
        in_specs=[pl.BlockSpec((8, tm), lambda i: (0, i)),
                  pl.BlockSpec((N_EXPERTS, LANES), lambda i: (0, 0))],
        out_specs=[pl.BlockSpec((1, 2, tm), lambda i: (i, 0, 0)),
                   pl.BlockSpec((tm, LANES), lambda i: (i, 0)),
                   pl.BlockSpec((N_TABS, N_EXPERTS, LANES), lambda i: (0, 0, 0)),
                   pl.BlockSpec((1, 2 * len(COPY_ROWS), LANES), lambda i: (i, 0, 0)),
                   pl.BlockSpec((8, nbp), lambda i: (0, 0))],
        out_shape=[jax.ShapeDtypeStruct((nt, 2, tm), I32), jax.ShapeDtypeStruct((t, LANES), F32),
                   jax.ShapeDtypeStruct((N_TABS, N_EXPERTS, LANES), I32), jax.ShapeDtypeStruct((nt, 2 * len(COPY_ROWS), LANES), I32),
                   jax.ShapeDtypeStruct((8, nbp), I32)],
        scratch_shapes=[pltpu.VMEM((N_EXPERTS, LANES), F32)] * 3,
        compiler_params=_cparams("arbitrary"),
        name="moe_route",
    )(rrow, cnt)


def _chunk_copies(tabs, i, local_ref, global_ref, sem, to_global, action):
    tab_ref, ctab_ref = tabs
    for k, rows in enumerate(COPY_ROWS):
        count = tab_ref[TAB_TOTAL, k, i]

        def copy(lo, go, rows=rows):
            lsl = local_ref.at[pl.ds(pl.multiple_of(lo, CHUNK), rows)]
            gsl = global_ref.at[pl.ds(pl.multiple_of(go, CHUNK), rows)]
            return pltpu.make_async_copy(lsl, gsl, sem) if to_global else pltpu.make_async_copy(gsl, lsl, sem)

        if action == "wait":
            def one(c, c1, copy=copy):
                copy(0, 0).wait()
                return c1
        else:
            def one(c, c1, copy=copy, k=k):
                copy(ctab_ref[i, 2 * k, c], ctab_ref[i, 2 * k + 1, c]).start()
                return c1

        lax.fori_loop(0, count, one, 0)


def _scatter_kernel(tab_ref, ctab_ref, meta_ref, lrow_ref, h_ref, xs_ref, buf_ref, zero_ref, sem, zsem, *, tm, nb):
    i = pl.program_id(0)
    tabs = (tab_ref, ctab_ref)

    @pl.when(i == 0)
    def _():
        zero_ref[...] = jnp.zeros_like(zero_ref)
        n_used = meta_ref[2, 0]

        def zcopy(b):
            sub = lax.shift_right_logical(meta_ref[1, b], int(math.log2(FFN_SUB)))
            start = pl.multiple_of(b * MOE_ROWS + sub * FFN_SUB, FFN_SUB)
            return pltpu.make_async_copy(zero_ref, xs_ref.at[pl.ds(start, FFN_SUB)], zsem)

        def needs(b):
            return (b < n_used) & ((meta_ref[1, b] & (FFN_SUB - 1)) != 0)

        def start(b, c):
            @pl.when(needs(b))
            def _():
                zcopy(b).start()
            return c

        def wait(b, c):
            @pl.when(needs(b))
            def _():
                zcopy(b).wait()
            return c

        lax.fori_loop(0, nb, start, 0)
        lax.fori_loop(0, nb, wait, 0)

    rows = _row_iota((LOCAL_ROWS, tm))
    p0 = rows == lrow_ref[0, 0:1, :]
    p1 = rows == lrow_ref[0, 1:2, :]
    perm = jnp.where(p0 | p1, 1.0, 0.0).astype(BF16)
    sorted_rows = jnp.dot(perm, h_ref[...], preferred_element_type=F32)

    def fill(slot):
        buf = buf_ref.at[slot]
        buf[...] = _pack_bf16_pairs(sorted_rows, exact=True)
        _chunk_copies(tabs, i, buf, xs_ref, sem.at[slot], True, "start")

    def drain(tile, slot):
        _chunk_copies(tabs, tile, buf_ref.at[slot], xs_ref, sem.at[slot], True, "wait")

    even = (i & 1) == 0

    @pl.when(even)
    def _():
        fill(0)

    @pl.when(jnp.logical_not(even))
    def _():
        fill(1)

    @pl.when((i > 0) & even)
    def _():
        drain(i - 1, 1)

    @pl.when((i > 0) & jnp.logical_not(even))
    def _():
        drain(i - 1, 0)

    @pl.when((i == pl.num_programs(0) - 1) & even)
    def _():
        drain(i, 0)

    @pl.when((i == pl.num_programs(0) - 1) & jnp.logical_not(even))
    def _():
        drain(i, 1)


def _scatter(tab, ctab, meta, lrow, h2, tm, nb):
    t = h2.shape[0]
    kern = functools.partial(_scatter_kernel, tm=tm, nb=nb)
    grid_spec = pltpu.PrefetchScalarGridSpec(
        num_scalar_prefetch=3,
        grid=(t // tm,),
        in_specs=[pl.BlockSpec((1, 2, tm), lambda i, *_: (i, 0, 0)),
                  pl.BlockSpec((tm, D_MODEL), lambda i, *_: (i, 0))],
        out_specs=pl.BlockSpec(memory_space=pl.ANY),
        scratch_shapes=[pltpu.VMEM((2, LOCAL_ROWS, XS_WIDTH), U32), pltpu.VMEM((FFN_SUB, XS_WIDTH), U32),
                        pltpu.SemaphoreType.DMA((2,)), pltpu.SemaphoreType.DMA],
    )
    return pl.pallas_call(
        kern,
        grid_spec=grid_spec,
        out_shape=jax.ShapeDtypeStruct((nb * MOE_ROWS, XS_WIDTH), U32),
        compiler_params=_cparams("arbitrary"),
        name="moe_scatter",
    )(tab, ctab, meta, lrow, h2)


def _ffn_kernel(meta_ref, x_ref, wg_ref, wu_ref, wd_ref, o_ref, wgu_b, wd_b, wg_f, wu_f, wd_f, wsem, *, layer):
    b = pl.program_id(0)
    live = b < meta_ref[2, 0]
    expert = meta_ref[0, b]
    prev = meta_ref[0, jnp.maximum(b - 1, 0)]

    def fetch(e, slot):
        return [pltpu.make_async_copy(src.at[layer, e], dst.at[slot], wsem.at[slot])
                for src, dst in ((wg_ref, wg_f), (wu_ref, wu_f), (wd_ref, wd_f))]

    def first_block(slot):
        @pl.when(b == 0)
        def _():
            for cp in fetch(expert, slot):
                cp.start()

        for cp in fetch(expert, slot):
            cp.wait()
        wgu_b[:, 0:EXPERT_FF] = wg_f[slot].astype(BF16)
        wgu_b[:, EXPERT_FF:2 * EXPERT_FF] = wu_f[slot].astype(BF16)
        wd_b[...] = wd_f[slot].astype(BF16)
        nxt = meta_ref[4, b]

        @pl.when(nxt < N_EXPERTS)
        def _():
            for cp in fetch(nxt, 1 - slot):
                cp.start()

    changed = live & ((b == 0) | (expert != prev))
    odd = (meta_ref[3, b] & 1) == 1

    @pl.when(changed & jnp.logical_not(odd))
    def _():
        first_block(0)

    @pl.when(changed & odd)
    def _():
        first_block(1)

    nvalid = meta_ref[1, jnp.maximum(jnp.minimum(b, meta_ref[2, 0] - 1), 0)]
    for sub in range(MOE_ROWS // FFN_SUB):
        rows = slice(sub * FFN_SUB, (sub + 1) * FFN_SUB)
        used = live & (nvalid > sub * FFN_SUB)

        @pl.when(used)
        def _():
            x = _unpack_bf16_pairs(x_ref[rows, 0:PACKED])
            gu = jnp.dot(x, wgu_b[...], preferred_element_type=F32)
            act = (_silu(gu[:, 0:EXPERT_FF]) * gu[:, EXPERT_FF:2 * EXPERT_FF]).astype(BF16)
            y = jnp.dot(act, wd_b[...], preferred_element_type=F32)
            o_ref[rows, :] = _pack_bf16_pairs(y)

        @pl.when(live & jnp.logical_not(used))
        def _():
            o_ref[rows, :] = jnp.zeros((FFN_SUB, PACKED), U32)


def _ffn(meta, xs, wg, wu, wd, layer, nb):
    def blk(b, m):
        return (jnp.maximum(jnp.minimum(b, m[2, 0] - 1), 0), 0)

    grid_spec = pltpu.PrefetchScalarGridSpec(
        num_scalar_prefetch=1,
        grid=(nb,),
        in_specs=[pl.BlockSpec((MOE_ROWS, XS_WIDTH), blk)] + [pl.BlockSpec(memory_space=pl.ANY)] * 3,
        out_specs=pl.BlockSpec((MOE_ROWS, PACKED), blk),
        scratch_shapes=[pltpu.VMEM((D_MODEL, 2 * EXPERT_FF), BF16), pltpu.VMEM((EXPERT_FF, D_MODEL), BF16),
                        pltpu.VMEM((2, D_MODEL, EXPERT_FF), F32), pltpu.VMEM((2, D_MODEL, EXPERT_FF), F32),
                        pltpu.VMEM((2, EXPERT_FF, D_MODEL), F32), pltpu.SemaphoreType.DMA((2,))],
    )
    return pl.pallas_call(
        functools.partial(_ffn_kernel, layer=layer),
        grid_spec=grid_spec,
        out_shape=jax.ShapeDtypeStruct((nb * MOE_ROWS, PACKED), U32),
        compiler_params=_cparams("arbitrary"),
        name="moe_experts",
    )(meta, xs, wg, wu, wd)


def _gather_kernel(tab_ref, ctab_ref, lcol_ref, x_ref, g_ref, ys_ref, o_ref, buf_ref, sem, *, tm, final):
    i = pl.program_id(0)

    last = pl.num_programs(0) - 1

    def fetch(tile, slot, action):
        _chunk_copies((tab_ref, ctab_ref), tile, buf_ref.at[slot], ys_ref, sem.at[slot], False, action)

    @pl.when(i == 0)
    def _():
        buf_ref[...] = jnp.zeros_like(buf_ref)
        fetch(0, 0, "start")

    even = (i & 1) == 0

    @pl.when((i < last) & even)
    def _():
        fetch(i + 1, 1, "start")

    @pl.when((i < last) & jnp.logical_not(even))
    def _():
        fetch(i + 1, 0, "start")

    col = _lane_iota((tm, LOCAL_ROWS)).astype(F32)
    pick0 = jnp.where(col == lcol_ref[:, 0:1], 1.0, 0.0).astype(BF16)
    pick1 = jnp.where(col == lcol_ref[:, 1:2], 1.0, 0.0).astype(BF16)

    def finish(slot):
        fetch(i, slot, "wait")
        y = _unpack_bf16_pairs(buf_ref[slot])
        both = jnp.dot(jnp.concatenate([pick0, pick1], axis=0), y, preferred_element_type=F32)
        x = x_ref[...] + lcol_ref[:, 2:3] * both[0:tm] + lcol_ref[:, 3:4] * both[tm:2 * tm]
        o_ref[...] = _rms(x, g_ref[...]) if final else x

    @pl.when(even)
    def _():
        finish(0)

    @pl.when(jnp.logical_not(even))
    def _():
        finish(1)


def _gather(tab, ctab, lcol, x2, g, ys, tm, final):
    t = x2.shape[0]
    kern = functools.partial(_gather_kernel, tm=tm, final=final)
    grid_spec = pltpu.PrefetchScalarGridSpec(
        num_scalar_prefetch=2,
        grid=(t // tm,),
        in_specs=[pl.BlockSpec((tm, LANES), lambda i, *_: (i, 0)),
                  pl.BlockSpec((tm, D_MODEL), lambda i, *_: (i, 0)),
                  pl.BlockSpec((1, D_MODEL), lambda i, *_: (0, 0)),
                  pl.BlockSpec(memory_space=pl.ANY)],
        out_specs=pl.BlockSpec((tm, D_MODEL), lambda i, *_: (i, 0)),
        scratch_shapes=[pltpu.VMEM((2, LOCAL_ROWS, PACKED), U32), pltpu.SemaphoreType.DMA((2,))],
    )
    return pl.pallas_call(
        kern,
        grid_spec=grid_spec,
        out_shape=jax.ShapeDtypeStruct((t, D_MODEL), F32),
        compiler_params=_cparams("arbitrary"),
        name="moe_combine",
    )(tab, ctab, lcol, x2, g, ys)


def _pad_rows(a, rows=8):
    return jnp.zeros((rows, a.shape[-1]), F32).at[:a.shape[0]].set(a.astype(F32))


def _arrange_w_in(w):
    gw = GROUP_WIDTH
    a0 = 0
    b0 = 3 * gw
    c0 = b0 + 3 * gw + N_HEADS
    d0 = c0 + gw + (gw + 4 * SSM_STATE) + N_HEADS
    half = MLA_ROPE // 2
    fox_q = (w[:, b0:b0 + gw] * (HEAD_DIM ** -0.5 * LOG2E)).astype(BF16)
    w = w.astype(BF16)
    f_logit = w[:, b0 + 3 * gw:b0 + 3 * gw + N_HEADS]
    dt_raw = w[:, c0 + 3 * gw:c0 + 3 * gw + N_HEADS]
    kr0 = d0 + MLA_Q_LORA + MLA_KV_LORA
    kr = w[:, kr0:kr0 + MLA_ROPE]
    kr_sw = jnp.concatenate([-kr[:, half:], kr[:, :half]], axis=1)
    zeros = lambda n: jnp.zeros((w.shape[0], n), w.dtype)
    misc = jnp.concatenate([f_logit, dt_raw, zeros(MISC_ROPE - 2 * N_HEADS), kr,
                            zeros(LANES - MISC_ROPE - MLA_ROPE)], axis=1)
    misc2 = jnp.concatenate([zeros(MISC_ROPE), kr_sw, zeros(LANES - MISC_ROPE - MLA_ROPE)], axis=1)
    out = jnp.concatenate([w[:, a0:a0 + 3 * gw], fox_q, w[:, b0 + gw:b0 + 2 * gw], w[:, c0:c0 + 3 * gw],
                           w[:, d0:d0 + MLA_Q_LORA + MLA_KV_LORA], misc, misc2], axis=1)
    fox_vt = w[:, b0 + 2 * gw:b0 + 3 * gw].T
    return out, fox_vt


def _arrange_mla(w_uq, w_ukv):
    half = MLA_ROPE // 2
    qd = MLA_NOPE + MLA_ROPE
    wq, wqs, wk, wv = [], [], [], []
    zq = jnp.zeros((MLA_Q_LORA, LANES - qd), w_uq.dtype)
    zk = jnp.zeros((MLA_KV_LORA, LANES - MLA_NOPE), w_ukv.dtype)
    for h in range(N_HEADS):
        q = w_uq[:, h * qd:(h + 1) * qd]
        nope, rope = q[:, :MLA_NOPE], q[:, MLA_NOPE:]
        wq.append(jnp.concatenate([nope, rope, zq], axis=1))
        wqs.append(jnp.concatenate([jnp.zeros_like(nope), -rope[:, half:], rope[:, :half], zq], axis=1))
        kv = w_ukv[:, h * 2 * MLA_NOPE:(h + 1) * 2 * MLA_NOPE]
        wk.append(jnp.concatenate([kv[:, :MLA_NOPE], zk], axis=1))
        wv.append(kv[:, MLA_NOPE:])
    cat = lambda xs: jnp.concatenate(xs, axis=1).astype(BF16)
    return cat(wq), cat(wqs), cat(wk), cat(wv).T


def kernel(x, positions, norm_mix, w_in, conv_a, fox_forget_bias, ssm_conv_w, ssm_conv_b, ssm_dt_bias,
           ssm_a_log, ssm_d, ssm_norm, mla_q_norm, mla_kv_norm, mla_w_uq, mla_w_ukv, w_out, norm_ffn,
           router_group_w, router_group_b, router_expert_w, router_expert_b, expert_w_gate, expert_w_up,
           expert_w_down, norm_final):
    batch, seq, d = x.shape
    t = batch * seq
    depth = w_in.shape[0]
    tm = min(512, t)
    tq = min(ATTN_TQ, seq)
    tmd = min(MOE_TILE, t)
    max_rows = 2 * t + (CHUNK - 1) * N_EXPERTS * (t // tmd) + N_EXPERTS * (MOE_ROWS - 1)
    nb = -(-max_rows // MOE_ROWS)
    nbp = -(-nb // LANES) * LANES

    xf = x.reshape(t, d)
    pos_col = positions.astype(F32).reshape(t, 1)
    cos, sin = _rope_tables(pos_col, tm)

    for l in range(depth):
        w_in_p, w_vt = _arrange_w_in(w_in[l])
        pa, pb, pc, pd, misc, misc2, fox_vt = _inproj(xf, norm_mix[l][None, :], w_in_p, w_vt, tm)

        sp = jnp.zeros((8, LANES), F32)
        sp = sp.at[0, MISC_F:MISC_F + N_HEADS].set(fox_forget_bias[l])
        sp = sp.at[0, MISC_DT:MISC_DT + N_HEADS].set(ssm_dt_bias[l])
        sp = sp.at[1, MISC_DT:MISC_DT + N_HEADS].set(ssm_a_log[l])
        col, rows, fox_q, fox_k, tref = _scalar_prep(misc, sp, pb, batch, seq, tq)

        ya = _conv_mixer(pa, _pad_rows(conv_a[l]), batch, seq)
        yb = _attention(fox_q, fox_k, fox_vt, tref, batch, seq, tq, "fox_attention")
        conv_wb = _pad_rows(jnp.concatenate([ssm_conv_w[l], ssm_conv_b[l][None, :]], axis=0))
        ssd_par = _pad_rows(jnp.stack([jnp.repeat(ssm_d[l], HEAD_DIM), ssm_norm[l]]))
        yc = _ssd_mixer(pc, col, rows, conv_wb, ssd_par, batch, seq)
        wq, wqs, wk, wv = _arrange_mla(mla_w_uq[l], mla_w_ukv[l])
        q, k, v = _mla_prep(pd, misc, misc2, cos, sin, mla_q_norm[l][None, :], mla_kv_norm[l][None, :],
                            wq, wqs, wk, wv, tm)
        yd = _attention(q, k, v, None, batch, seq, tq, "mla_attention")

        pad = jnp.zeros((d, LANES - N_EXPERTS - N_EXPERT_GROUPS), F32)
        wr = jnp.concatenate([router_expert_w[l], router_group_w[l], pad], axis=1)
        wr_hi = wr.astype(BF16)
        wr = jnp.concatenate([wr_hi, (wr - wr_hi.astype(F32)).astype(BF16)], axis=1)
        br = jnp.concatenate([router_expert_b[l], router_group_b[l], pad[0]])[None, :]
        x2, h2, rrow, cnt = _outproj(xf, ya, yb, yc, yd, w_out[l].astype(BF16), norm_ffn[l][None, :], wr, br,
                                     tm, tmd)

        lrow, lcol, tab, ctab, meta = _route(rrow, cnt, tmd, nbp)
        xs = _scatter(tab, ctab, meta, lrow, h2, tmd, nb)
        ys = _ffn(meta, xs, expert_w_gate, expert_w_up, expert_w_down, l, nb)
        final = l == depth - 1
        xf = _gather(tab, ctab, lcol, x2, norm_final[None, :], ys, tmd, final)

    return xf.reshape(batch, seq, d)


def _retile(dest, tm, tmd):
    if tm == tmd:
        return dest
    nt = dest.shape[0]
    return dest.reshape(nt, 2, tm // tmd, tmd).transpose(0, 2, 1, 3).reshape(nt * (tm // tmd), 2, tmd)
```

```python
import functools
import math

import jax
import jax.numpy as jnp
import numpy as np
from jax import lax
from jax.experimental import pallas as pl
from jax.experimental.pallas import tpu as pltpu

F32 = jnp.float32
BF16 = jnp.bfloat16
I32 = jnp.int32

LANES = 128
VMEM_LIMIT_BYTES = 56 * 1024 * 1024

D_MODEL = 1024
RMS_EPS = 1e-6
LOG2E = math.log2(math.e)
GROUP_WIDTH = 256
HEAD_DIM = 64
N_HEADS = 4

CONV_A_WIDTH = 3
SSM_CONV = 4
SSM_STATE = 64
SSM_CHUNK = 256

MLA_NOPE = 64
MLA_ROPE = 32
MLA_Q_LORA = 256
MLA_KV_LORA = 128
ROPE_BASE = 10000.0
MLA_CHUNK = 64
ATTN_TQ = 512
ATTN_RB = 128

N_EXPERT_GROUPS = 4
EXPERTS_PER_GROUP = 8
N_EXPERTS = 32
EXPERT_FF = 256
MOE_ROWS = 512
FFN_SUB = 256

SEG_A = (0, 768)
SEG_B = (768, 1280)
SEG_C = (1280, 2048)
SEG_D = (2048, 2432)
SEG_M = (2432, 2560)
SEG_M2 = (2560, 2688)
IN_COLS_PADDED = 2688
HEAD_PAD = N_HEADS * LANES
AUG_LANE = HEAD_DIM
MISC_F = 0
MISC_DT = 4
MISC_ROPE = 64
COL_CUMF = 0
COL_DT = 4
COL_ACUM = 8
N_SCALAR_ROWS = 16


def _cparams(*sem):
    return pltpu.CompilerParams(dimension_semantics=sem, vmem_limit_bytes=VMEM_LIMIT_BYTES)


def _lane_iota(shape):
    return lax.broadcasted_iota(I32, shape, len(shape) - 1)


def _row_iota(shape):
    return lax.broadcasted_iota(I32, shape, 0)


def _rms(x, g):
    ms = jnp.mean(x * x, axis=-1, keepdims=True)
    return x * lax.rsqrt(ms + RMS_EPS) * g


def _silu(x):
    return x / (1.0 + jnp.exp(-x))


def _softplus(x):
    return jnp.maximum(x, 0.0) + jnp.log(1.0 + jnp.exp(-jnp.abs(x)))


def _shift_rows(x, k):
    rolled = pltpu.roll(x, k, 0)
    return jnp.where(_row_iota(x.shape) >= k, rolled, 0.0)


def _rope_kernel(pos_ref, freq_ref, cos_ref, sin_ref):
    ang = pos_ref[...] * freq_ref[...]
    lane = _lane_iota(ang.shape)
    rope = (lane >= MISC_ROPE) & (lane < MISC_ROPE + MLA_ROPE)
    cos_ref[...] = jnp.where(rope, jnp.cos(ang), jnp.where(lane < MISC_ROPE, 1.0, 0.0))
    sin_ref[...] = jnp.where(rope, jnp.sin(ang), 0.0)


def _rope_tables(pos_col, tm):
    t = pos_col.shape[0]
    half = MLA_ROPE // 2
    inv = ROPE_BASE ** (-np.arange(0, MLA_ROPE, 2, dtype=np.float32) / MLA_ROPE)
    freq = np.zeros((1, LANES), np.float32)
    freq[0, MISC_ROPE:MISC_ROPE + half] = inv
    freq[0, MISC_ROPE + half:MISC_ROPE + MLA_ROPE] = inv
    return pl.pallas_call(
        _rope_kernel,
        grid=(t // tm,),
        in_specs=[pl.BlockSpec((tm, 1), lambda i: (i, 0)),
                  pl.BlockSpec((1, LANES), lambda i: (0, 0))],
        out_specs=[pl.BlockSpec((tm, LANES), lambda i: (i, 0))] * 2,
        out_shape=[jax.ShapeDtypeStruct((t, LANES), F32)] * 2,
        compiler_params=_cparams("parallel"),
        name="rope_tables",
    )(pos_col, jnp.asarray(freq))


def _inproj_kernel(x_ref, g_ref, w_ref, wvt_ref, cos_ref, sin_ref, nq_ref, nkv_ref, wq2_ref, wk_ref, wvt2_ref,
                   oa, ob, oc, om, ovt, q_ref, k_ref, vt_ref):
    h = _rms(x_ref[...], g_ref[...]).astype(BF16)
    for o, (lo, hi) in ((oa, SEG_A), (ob, SEG_B), (oc, SEG_C)):
        o[...] = jnp.dot(h, w_ref[:, lo:hi], preferred_element_type=F32).astype(o.dtype)
    ovt[...] = lax.dot_general(wvt_ref[...], h, (((1,), (1,)), ((), ())),
                               preferred_element_type=F32).astype(ovt.dtype)
    misc = jnp.dot(h, w_ref[:, SEG_M[0]:SEG_M[1]], preferred_element_type=F32)
    misc2 = jnp.dot(h, w_ref[:, SEG_M2[0]:SEG_M2[1]], preferred_element_type=F32)
    om[...] = misc

    pd = jnp.dot(h, w_ref[:, SEG_D[0]:SEG_D[1]], preferred_element_type=F32)
    cq = _rms(pd[:, 0:MLA_Q_LORA], nq_ref[...]).astype(BF16)
    ckv = _rms(pd[:, MLA_Q_LORA:MLA_Q_LORA + MLA_KV_LORA], nkv_ref[...]).astype(BF16)
    cos = cos_ref[...]
    sin = sin_ref[...]
    cos4 = jnp.concatenate([cos] * N_HEADS, axis=1)
    sin4 = jnp.concatenate([sin] * N_HEADS, axis=1)
    scale = (MLA_NOPE + MLA_ROPE) ** -0.5 * LOG2E
    q2 = jnp.dot(cq, wq2_ref[...], preferred_element_type=F32)
    q_ref[...] = ((q2[:, 0:HEAD_PAD] * cos4 + q2[:, HEAD_PAD:2 * HEAD_PAD] * sin4) * scale).astype(q_ref.dtype)
    lane = _lane_iota(cos.shape)
    rope = (lane >= MISC_ROPE) & (lane < MISC_ROPE + MLA_ROPE)
    kr = jnp.where(rope, misc * cos + misc2 * sin, 0.0)
    k = jnp.dot(ckv, wk_ref[...], preferred_element_type=F32)
    k_ref[...] = (k + jnp.concatenate([kr] * N_HEADS, axis=1)).astype(k_ref.dtype)
    vt_ref[...] = lax.dot_general(wvt2_ref[...], ckv, (((1,), (1,)), ((), ())),
                                  preferred_element_type=F32).astype(vt_ref.dtype)


def _inproj(x, g, w, wvt, cos, sin, nq, nkv, wq2, wk, wvt2, tm):
    t = x.shape[0]
    full = lambda a: pl.BlockSpec(a.shape, lambda i: (0, 0))
    tile = lambda wd: pl.BlockSpec((tm, wd), lambda i: (i, 0))
    cols = lambda: pl.BlockSpec((GROUP_WIDTH, tm), lambda i: (0, i))
    seg = lambda s: s[1] - s[0]
    return pl.pallas_call(
        _inproj_kernel,
        grid=(t // tm,),
        in_specs=[tile(D_MODEL), full(g), full(w), full(wvt), tile(LANES), tile(LANES),
                  full(nq), full(nkv), full(wq2), full(wk), full(wvt2)],
        out_specs=[tile(seg(SEG_A)), tile(seg(SEG_B)), tile(seg(SEG_C)), tile(LANES), cols(),
                   tile(HEAD_PAD), tile(HEAD_PAD), cols()],
        out_shape=[jax.ShapeDtypeStruct((t, seg(SEG_A)), BF16), jax.ShapeDtypeStruct((t, seg(SEG_B)), BF16),
                   jax.ShapeDtypeStruct((t, seg(SEG_C)), BF16), jax.ShapeDtypeStruct((t, LANES), F32),
                   jax.ShapeDtypeStruct((GROUP_WIDTH, t), BF16),
                   jax.ShapeDtypeStruct((t, HEAD_PAD), BF16), jax.ShapeDtypeStruct((t, HEAD_PAD), BF16),
                   jax.ShapeDtypeStruct((GROUP_WIDTH, t), BF16)],
        compiler_params=_cparams("parallel"),
        name="inproj",
    )(x, g, w, wvt, cos, sin, nq, nkv, wq2, wk, wvt2)


def _scalar_prep_kernel(m_ref, p_ref, qk_ref, place_ref, const_ref,
                        col_ref, row_ref, qa_ref, ka_ref, tref_ref, *, tq):
    s = m_ref.shape[0]
    tref_ref[...] = jnp.zeros_like(tref_ref)
    tile_ref = jnp.zeros((1, LANES), F32)
    m = m_ref[...]
    bias = p_ref[0:1, :]
    a_log = p_ref[1:2, :]
    lane = _lane_iota(m.shape)
    z = m + bias
    logf = jnp.minimum(z, 0.0) - jnp.log(1.0 + jnp.exp(-jnp.abs(z)))
    dt = _softplus(z)
    a = dt * (-jnp.exp(a_log))
    is_f = lane < MISC_DT
    is_dt = (lane >= MISC_DT) & (lane < MISC_DT + N_HEADS)
    v = jnp.where(is_f, logf, jnp.where(is_dt, a, 0.0))
    r = _row_iota((SSM_CHUNK, SSM_CHUNK))
    c = _lane_iota((SSM_CHUNK, SSM_CHUNK))
    tril = jnp.where(r >= c, 1.0, 0.0).astype(BF16)
    carry = jnp.zeros((1, LANES), F32)
    lane_1 = _lane_iota((1, LANES))
    lane_b = _lane_iota((SSM_CHUNK, LANES))
    low = lane_b < HEAD_DIM
    aug_lanes = (lane_b >= AUG_LANE) & (lane_b < AUG_LANE + AUG_TERMS)
    for ci in range(s // SSM_CHUNK):
        rest = v[ci * SSM_CHUNK:(ci + 1) * SSM_CHUNK]
        cs = jnp.zeros((SSM_CHUNK, LANES), F32)
        for _ in range(3):
            term = rest.astype(BF16)
            cs = cs + jnp.dot(tril, term, preferred_element_type=F32)
            rest = rest - term.astype(F32)
        cs = cs + jnp.where(lane_1 < MISC_DT, carry, 0.0)
        carry = cs[SSM_CHUNK - 1:SSM_CHUNK]
        acum = pltpu.roll(cs, COL_ACUM - MISC_DT, 1)
        out = jnp.where(lane_b < MISC_DT, cs * LOG2E,
                        jnp.where(lane_b < COL_ACUM, dt[ci * SSM_CHUNK:(ci + 1) * SSM_CHUNK],
                                  jnp.where(lane_b < COL_ACUM + N_HEADS, acum, 0.0)))
        rows = slice(ci * SSM_CHUNK, (ci + 1) * SSM_CHUNK)
        col_ref[rows, :] = out
        row_ref[0, :, rows] = out.T[:N_SCALAR_ROWS]
        if (ci * SSM_CHUNK) % tq == 0:
            tile_ref = out[0:1, :]
            ti = (ci * SSM_CHUNK) // tq
            tref_ref[0, ti:ti + 1, :] = tile_ref
        c = out - tile_ref
        c_hi = c.astype(BF16)
        r1 = c - c_hi.astype(F32)
        c_mid = r1.astype(BF16)
        c_lo = (r1 - c_mid.astype(F32)).astype(BF16)
        compact = jnp.dot(jnp.concatenate([c_hi, c_mid, c_lo], axis=1), place_ref[...],
                          preferred_element_type=F32) + const_ref[0:1, :]
        for side, o_ref in enumerate((qa_ref, ka_ref)):
            dec = compact[:, side * LANES:(side + 1) * LANES]
            for h in range(N_HEADS):
                pair = qk_ref[rows, side * GROUP_WIDTH + (h // 2) * LANES:
                              side * GROUP_WIDTH + (h // 2 + 1) * LANES].astype(F32)
                feat = pair if h % 2 == 0 else pltpu.roll(pair, HEAD_DIM, 1)
                dec_h = pltpu.roll(dec, AUG_LANE - AUG_TERMS * h, 1)
                group = jnp.where(low, feat, jnp.where(aug_lanes, dec_h, 0.0))
                o_ref[rows, h * LANES:(h + 1) * LANES] = group.astype(o_ref.dtype)


AUG_TERMS = 6


def _fox_placement():
    place = np.zeros((3 * LANES, 2 * LANES), np.float32)
    const = np.zeros((8, 2 * LANES), np.float32)
    for h in range(N_HEADS):
        a0 = AUG_TERMS * h
        for term in range(3):
            place[term * LANES + COL_CUMF + h, a0 + term] = 1.0
            place[term * LANES + COL_CUMF + h, LANES + a0 + 3 + term] = -1.0
            const[0, a0 + 3 + term] = 1.0
            const[0, LANES + a0 + term] = 1.0
    return jnp.asarray(place, BF16), jnp.asarray(const, F32)


def _scalar_prep(misc, params, qk, batch, seq, tq):
    place, const = _fox_placement()
    full = lambda a: pl.BlockSpec(a.shape, lambda b: (0,) * a.ndim)
    return pl.pallas_call(
        functools.partial(_scalar_prep_kernel, tq=tq),
        grid=(batch,),
        in_specs=[pl.BlockSpec((seq, LANES), lambda b: (b, 0)),
                  pl.BlockSpec((8, LANES), lambda b: (0, 0)),
                  pl.BlockSpec((seq, 2 * GROUP_WIDTH), lambda b: (b, 0)),
                  full(place), full(const)],
        out_specs=[pl.BlockSpec((seq, LANES), lambda b: (b, 0)),
                   pl.BlockSpec((1, N_SCALAR_ROWS, seq), lambda b: (b, 0, 0)),
                   pl.BlockSpec((seq, HEAD_PAD), lambda b: (b, 0)),
                   pl.BlockSpec((seq, HEAD_PAD), lambda b: (b, 0)),
                   pl.BlockSpec((1, 8, LANES), lambda b: (b, 0, 0))],
        out_shape=[jax.ShapeDtypeStruct((batch * seq, LANES), F32),
                   jax.ShapeDtypeStruct((batch, N_SCALAR_ROWS, seq), F32),
                   jax.ShapeDtypeStruct((batch * seq, HEAD_PAD), BF16),
                   jax.ShapeDtypeStruct((batch * seq, HEAD_PAD), BF16),
                   jax.ShapeDtypeStruct((batch, 8, LANES), F32)],
        compiler_params=_cparams("parallel"),
        name="scalar_prep",
    )(misc, params, qk, place, const)


def _conv_mixer_kernel(p_ref, w_ref, o_ref):
    gw = GROUP_WIDTH
    b_gate = p_ref[:, 0:gw].astype(F32)
    cv = p_ref[:, gw:2 * gw].astype(F32) * p_ref[:, 2 * gw:3 * gw].astype(F32)
    acc = cv * w_ref[CONV_A_WIDTH - 1:CONV_A_WIDTH, :]
    for k in range(1, CONV_A_WIDTH):
        acc = acc + _shift_rows(cv, k) * w_ref[CONV_A_WIDTH - 1 - k:CONV_A_WIDTH - k, :]
    o_ref[...] = (b_gate * acc).astype(o_ref.dtype)


def _conv_mixer(pa, w, batch, seq):
    return pl.pallas_call(
        _conv_mixer_kernel,
        grid=(batch,),
        in_specs=[pl.BlockSpec((seq, 3 * GROUP_WIDTH), lambda b: (b, 0)),
                  pl.BlockSpec((8, GROUP_WIDTH), lambda b: (0, 0))],
        out_specs=pl.BlockSpec((seq, GROUP_WIDTH), lambda b: (b, 0)),
        out_shape=jax.ShapeDtypeStruct((batch * seq, GROUP_WIDTH), BF16),
        compiler_params=_cparams("parallel"),
        name="conv_mixer",
    )(pa, w)


def _pair_lanes(col, base, shape):
    lane = _lane_iota(shape)
    return jnp.where(lane < HEAD_DIM, col[:, base:base + 1], col[:, base + 1:base + 2])


def _ssd_kernel(p_ref, col_ref, row_ref, cw_ref, par_ref, o_ref, u_ref):
    s = p_ref.shape[0]
    q = SSM_CHUNK
    gw = GROUP_WIDTH
    xbc = p_ref[:, gw:3 * gw].astype(F32)
    acc = xbc * cw_ref[SSM_CONV - 1:SSM_CONV, :]
    for k in range(1, SSM_CONV):
        acc = acc + _shift_rows(xbc, k) * cw_ref[SSM_CONV - 1 - k:SSM_CONV - k, :]
    u_ref[...] = _silu(acc + cw_ref[SSM_CONV:SSM_CONV + 1, :])

    d_skip = par_ref[0:1, :]
    norm_g = par_ref[1:2, :]
    lane_q = _lane_iota((q, LANES))
    low = lane_q < HEAD_DIM
    tri = _row_iota((q, q)) >= _lane_iota((q, q))

    def chunk(ci, states):
        rows = pl.ds(ci * q, q)
        u = u_ref[rows, :]
        col = col_ref[rows, :]
        bm = u[:, gw:gw + LANES]
        cm = u[:, gw + LANES:gw + 2 * LANES]
        z = p_ref[rows, 0:gw].astype(F32)
        new_states = []
        ys = []
        for g in range(2):
            sel = low if g == 0 else jnp.logical_not(low)
            cg = jnp.where(sel, cm, 0.0).astype(BF16)
            bg = jnp.where(sel, bm, 0.0)
            gmat = lax.dot_general(cg, bm.astype(BF16), (((1,), (1,)), ((), ())),
                                   preferred_element_type=F32)
            xs = u[:, g * LANES:(g + 1) * LANES]
            dt2 = _pair_lanes(col, COL_DT + 2 * g, (q, LANES))
            ac2 = _pair_lanes(col, COL_ACUM + 2 * g, (q, LANES))
            xdt = xs * dt2
            xdt_b = xdt.astype(BF16)
            st = states[g]
            y_off = jnp.dot(cg, st.astype(BF16), preferred_element_type=F32) * jnp.exp(ac2)
            halves = []
            for hh in range(2):
                h = 2 * g + hh
                ac_col = col[:, COL_ACUM + h:COL_ACUM + h + 1]
                ac_row = row_ref[0, COL_ACUM + h:COL_ACUM + h + 1, rows]
                decay = jnp.exp(jnp.where(tri, ac_col - ac_row, -1e30))
                mm = (gmat * decay).astype(BF16)
                halves.append(jnp.dot(mm, xdt_b, preferred_element_type=F32))
            y = jnp.where(low, halves[0], halves[1]) + y_off + d_skip[:, g * LANES:(g + 1) * LANES] * xs
            ys.append(y)
            ac_last = ac2[q - 1:q, :]
            w_end = jnp.exp(ac_last - ac2)
            xw = (xdt * w_end).astype(BF16)
            upd = jnp.dot(bg.T.astype(BF16), xw, preferred_element_type=F32)
            new_states.append(st * jnp.exp(ac_last) + upd)
        yfull = jnp.concatenate(ys, axis=1) * _silu(z)
        o_ref[rows, :] = _rms(yfull, norm_g).astype(o_ref.dtype)
        return tuple(new_states)

    init = (jnp.zeros((LANES, LANES), F32), jnp.zeros((LANES, LANES), F32))
    states = init
    for ci in range(s // q):
        states = chunk(ci, states)


def _ssd_mixer(pc, col, rows, conv_wb, par, batch, seq):
    gw = GROUP_WIDTH
    return pl.pallas_call(
        _ssd_kernel,
        grid=(batch,),
        in_specs=[pl.BlockSpec((seq, 3 * gw), lambda b: (b, 0)),
                  pl.BlockSpec((seq, LANES), lambda b: (b, 0)),
                  pl.BlockSpec((1, N_SCALAR_ROWS, seq), lambda b: (b, 0, 0)),
                  pl.BlockSpec((8, 2 * gw), lambda b: (0, 0)),
                  pl.BlockSpec((8, gw), lambda b: (0, 0))],
        out_specs=pl.BlockSpec((seq, gw), lambda b: (b, 0)),
        out_shape=jax.ShapeDtypeStruct((batch * seq, gw), BF16),
        scratch_shapes=[pltpu.VMEM((seq, 2 * gw), F32)],
        compiler_params=_cparams("parallel"),
        name="ssd_mixer",
    )(pc, col, rows, conv_wb, par)


def _attn_kernel(*refs, fox, tq):
    if fox:
        tref_ref, q_ref, k_ref, vt_ref, o_ref = refs
    else:
        q_ref, k_ref, vt_ref, o_ref = refs
        tref_ref = None
    b = pl.program_id(0)
    i = pl.program_id(1)
    key = _row_iota((tq, tq))
    qry = _lane_iota((tq, tq))
    if fox:
        allowed = key <= qry
    else:
        shift = int(math.log2(MLA_CHUNK))
        allowed = (key >> shift) <= (qry >> shift)
    qs = [q_ref[:, h * LANES:(h + 1) * LANES] for h in range(N_HEADS)]
    ones_rows = jnp.ones((16, tq), BF16)

    def step(j, masked, carry):
        rk = pl.ds(pl.multiple_of(j * tq, tq), tq)
        scores = [lax.dot_general(k_ref[rk, h * LANES:(h + 1) * LANES], qs[h], (((1,), (1,)), ((), ())),
                                  preferred_element_type=F32) for h in range(N_HEADS)]
        probs = []
        for h in range(N_HEADS):
            m, l, _ = carry[h]
            s = scores[h]
            if masked:
                s = jnp.where(allowed, s, -1e30)
            delta = (tref_ref[b, i, h] - tref_ref[b, j, h]) if fox else 0.0
            m_new = jnp.maximum(m, jnp.max(s, axis=0, keepdims=True) + delta)
            alpha = jnp.exp2(m - m_new)
            p = jnp.exp2(s - (m_new - delta))
            probs.append((m_new, alpha, p.astype(BF16)))
        new = []
        for h in range(N_HEADS):
            pair = h // 2
            m_new, alpha, p = probs[h]
            lhs = jnp.concatenate([vt_ref[pair * LANES:(pair + 1) * LANES, rk], ones_rows], axis=0)
            pv = jnp.dot(lhs, p, preferred_element_type=F32)
            new.append((m_new, alpha * carry[h][1] + pv[LANES:LANES + 1], alpha * carry[h][2] + pv[0:LANES]))
        return tuple(new)

    init = tuple((jnp.full((1, tq), -1e30, F32), jnp.zeros((1, tq), F32), jnp.zeros((LANES, tq), F32))
                 for _ in range(N_HEADS))
    carry = lax.fori_loop(0, i, lambda j, c: step(j, False, c), init)
    carry = step(i, True, carry)
    outs = [acc / l for (_, l, acc) in carry]
    top = _row_iota((LANES, tq)) < HEAD_DIM
    o_t = jnp.concatenate([jnp.where(top, outs[0], outs[1]), jnp.where(top, outs[2], outs[3])], axis=0)
    o_ref[...] = o_t.T.astype(o_ref.dtype)


def _attention(q, k, vt, tref, batch, seq, tq, name):
    nq = seq // tq
    fox = tref is not None
    kern = functools.partial(_attn_kernel, fox=fox, tq=tq)
    grid_spec = pltpu.PrefetchScalarGridSpec(
        num_scalar_prefetch=1 if fox else 0,
        grid=(batch, nq),
        in_specs=[pl.BlockSpec((tq, HEAD_PAD), lambda b, i, *_: (b * nq + i, 0)),
                  pl.BlockSpec((seq, HEAD_PAD), lambda b, i, *_: (b, 0)),
                  pl.BlockSpec((GROUP_WIDTH, seq), lambda b, i, *_: (0, b))],
        out_specs=pl.BlockSpec((tq, GROUP_WIDTH), lambda b, i, *_: (b * nq + i, 0)),
    )
    args = ((tref,) if fox else ()) + (q, k, vt)
    return pl.pallas_call(
        kern,
        grid_spec=grid_spec,
        out_shape=jax.ShapeDtypeStruct((batch * seq, GROUP_WIDTH), BF16),
        compiler_params=_cparams("parallel", "arbitrary"),
        name=name,
    )(*args)


def _outproj_kernel(x_ref, ya, yb, yc, yd, w_ref, g_ref, wr_ref, br_ref,
                    x2_ref, h2_ref, rrow_ref, cnt_ref, *, tm, moe_tile):
    y = jnp.concatenate([ya[...], yb[...], yc[...], yd[...]], axis=1)
    x2 = x_ref[...] + jnp.dot(y, w_ref[...], preferred_element_type=F32)
    x2_ref[...] = x2
    h2 = _rms(x2, g_ref[...])
    h2_ref[...] = h2.astype(h2_ref.dtype)
    h_hi = h2.astype(BF16)
    h_lo = (h2 - h_hi.astype(F32)).astype(BF16)
    part = jnp.dot(h_hi, wr_ref[...], preferred_element_type=F32)
    logits = (part[:, 0:LANES] + part[:, LANES:2 * LANES]
              + jnp.dot(h_lo, wr_ref[:, 0:LANES], preferred_element_type=F32) + br_ref[...])
    lt = logits.T
    row = _row_iota(lt.shape)
    neg = -1e30
    big = 1 << 20
    gmask = (row >= N_EXPERTS) & (row < N_EXPERTS + N_EXPERT_GROUPS)
    gl = jnp.where(gmask, lt, neg)
    gmax = jnp.max(gl, axis=0, keepdims=True)
    gsum = jnp.sum(jnp.where(gmask, jnp.exp(gl - gmax), 0.0), axis=0, keepdims=True)
    g_w = 1.0 / gsum
    g_idx = jnp.min(jnp.where(gmask & (gl == gmax), row, big), axis=0, keepdims=True) - N_EXPERTS
    emask = (row < N_EXPERTS) & ((row >> int(math.log2(EXPERTS_PER_GROUP))) == g_idx)
    el = jnp.where(emask, lt, neg)
    e1v = jnp.max(el, axis=0, keepdims=True)
    esum = jnp.sum(jnp.where(emask, jnp.exp(el - e1v), 0.0), axis=0, keepdims=True)
    i1 = jnp.min(jnp.where(emask & (el == e1v), row, big), axis=0, keepdims=True)
    el2 = jnp.where(row == i1, neg, el)
    e2v = jnp.max(el2, axis=0, keepdims=True)
    i2 = jnp.min(jnp.where(emask & (row != i1) & (el2 == e2v), row, big), axis=0, keepdims=True)
    p1 = 1.0 / esum
    p2 = jnp.exp(e2v - e1v) / esum
    w1 = g_w * (p1 / (p1 + p2))
    w2 = g_w * (p2 / (p1 + p2))
    out_row = _row_iota(rrow_ref.shape)
    rrow_ref[...] = jnp.where(out_row == 0, i1.astype(F32),
                              jnp.where(out_row == 1, i2.astype(F32),
                                        jnp.where(out_row == 2, w1, jnp.where(out_row == 3, w2, 0.0))))
    step = pl.program_id(0)

    @pl.when(step == 0)
    def _():
        cnt_ref[...] = jnp.zeros_like(cnt_ref)

    chosen = jnp.where((row == i1) | (row == i2), 1.0, 0.0).astype(BF16)
    tiles = tm // moe_tile
    tile_of = (_row_iota((tm, LANES)) >> int(math.log2(moe_tile))) + step * tiles
    to_tile = jnp.where(_lane_iota((tm, LANES)) == tile_of, 1.0, 0.0).astype(BF16)
    counts = jnp.dot(chosen, to_tile, preferred_element_type=F32)
    cnt_ref[...] += counts[0:N_EXPERTS]


def _outproj(x, ya, yb, yc, yd, w, g, wr, br, tm, moe_tile):
    t = x.shape[0]
    full = lambda a: pl.BlockSpec(a.shape, lambda i: (0, 0))
    tile = lambda wd: pl.BlockSpec((tm, wd), lambda i: (i, 0))
    return pl.pallas_call(
        functools.partial(_outproj_kernel, tm=tm, moe_tile=moe_tile),
        grid=(t // tm,),
        in_specs=[tile(D_MODEL)] + [tile(GROUP_WIDTH)] * 4 + [full(w), full(g), full(wr), full(br)],
        out_specs=[tile(D_MODEL), tile(D_MODEL), pl.BlockSpec((8, tm), lambda i: (0, i)),
                   pl.BlockSpec((N_EXPERTS, LANES), lambda i: (0, 0))],
        out_shape=[jax.ShapeDtypeStruct((t, D_MODEL), F32), jax.ShapeDtypeStruct((t, D_MODEL), BF16),
                   jax.ShapeDtypeStruct((8, t), F32), jax.ShapeDtypeStruct((N_EXPERTS, LANES), F32)],
        compiler_params=_cparams("arbitrary"),
        name="outproj_router",
    )(x, ya, yb, yc, yd, w, g, wr, br)


def _positions_kernel(r_ref, dest_ref, meta_ref, cnt_ref, carry_ref, start_ref, *, tm, nbp):
    phase = pl.program_id(0)
    i = pl.program_id(1)
    e_iota = _row_iota((N_EXPERTS, tm))
    e0 = r_ref[0:1, :].astype(I32)
    e1 = r_ref[1:2, :].astype(I32)
    oh0 = e_iota == e0
    oh1 = e_iota == e1
    oh = jnp.where(oh0 | oh1, 1.0, 0.0)

    @pl.when((phase == 0) & (i == 0))
    def _():
        cnt_ref[...] = jnp.zeros_like(cnt_ref)

    @pl.when(phase == 0)
    def _():
        cnt_ref[...] += jnp.sum(oh, axis=-1, keepdims=True)

    @pl.when((phase == 1) & (i == 0))
    def _():
        cnt = cnt_ref[...]
        padded = jnp.floor((cnt + (MOE_ROWS - 1)) * (1.0 / MOE_ROWS)) * MOE_ROWS
        tril = jnp.where(_row_iota((N_EXPERTS, N_EXPERTS)) >= _lane_iota((N_EXPERTS, N_EXPERTS)), 1.0, 0.0)
        pend = jnp.dot(tril, padded, preferred_element_type=F32, precision=lax.Precision.HIGHEST)
        pstart = pend - padded
        start_ref[...] = pstart
        carry_ref[...] = jnp.zeros_like(carry_ref)
        pend_b = jnp.concatenate([pend] * (nbp // LANES), axis=1)
        vend_b = jnp.concatenate([pstart + cnt] * (nbp // LANES), axis=1)
        b0 = (_lane_iota((N_EXPERTS, nbp)) * MOE_ROWS).astype(F32)
        bexp = jnp.sum(jnp.where(pend_b <= b0, 1.0, 0.0), axis=0, keepdims=True)
        bexp = jnp.minimum(bexp, N_EXPERTS - 1.0)
        is_e = _row_iota((N_EXPERTS, nbp)).astype(F32) == bexp
        vend = jnp.sum(jnp.where(is_e, vend_b, 0.0), axis=0, keepdims=True)
        nvalid = jnp.clip(vend - b0[0:1], 0.0, float(MOE_ROWS))
        total = jnp.max(pend_b, axis=0, keepdims=True) * (1.0 / MOE_ROWS)
        row = _row_iota((8, nbp))
        meta = jnp.where(row == 0, bexp, jnp.where(row == 1, nvalid, jnp.where(row == 2, total, 0.0)))
        meta_ref[...] = meta.astype(I32)

    @pl.when(phase == 1)
    def _():
        su = jnp.where(_row_iota((tm, tm)) < _lane_iota((tm, tm)), 1.0, 0.0).astype(BF16)
        before = jnp.dot(oh.astype(BF16), su, preferred_element_type=F32)
        base = start_ref[:, 0:1] + carry_ref[:, 0:1] + before
        d0 = jnp.sum(jnp.where(oh0, base, 0.0), axis=0, keepdims=True)
        d1 = jnp.sum(jnp.where(oh1, base, 0.0), axis=0, keepdims=True)
        dest_ref[0, 0:1, :] = d0.astype(I32)
        dest_ref[0, 1:2, :] = d1.astype(I32)
        carry_ref[...] += jnp.sum(oh, axis=-1, keepdims=True)


def _positions(rrow, tm, nbp):
    t = rrow.shape[1]
    nt = t // tm
    kern = functools.partial(_positions_kernel, tm=tm, nbp=nbp)
    return pl.pallas_call(
        kern,
        grid=(2, nt),
        in_specs=[pl.BlockSpec((8, tm), lambda p, i: (0, i))],
        out_specs=[pl.BlockSpec((1, 2, tm), lambda p, i: (i * p, 0, 0)),
                   pl.BlockSpec((8, nbp), lambda p, i: (0, 0))],
        out_shape=[jax.ShapeDtypeStruct((nt, 2, tm), I32), jax.ShapeDtypeStruct((8, nbp), I32)],
        scratch_shapes=[pltpu.VMEM((N_EXPERTS, LANES), F32)] * 3,
        compiler_params=_cparams("arbitrary", "arbitrary"),
        name="moe_positions",
    )(rrow)


def _dispatch_kernel(meta_ref, dest_ref, h_ref, xs_ref, zero_ref, sem, zsem, *, tm, nb):
    i = pl.program_id(0)

    @pl.when(i == 0)
    def _():
        zero_ref[...] = jnp.zeros_like(zero_ref)
        n_used = meta_ref[2, 0]

        def zcopy(b):
            return pltpu.make_async_copy(zero_ref, xs_ref.at[pl.ds(b * MOE_ROWS, MOE_ROWS)], zsem)

        def needs(b):
            return (b < n_used) & (meta_ref[1, b] < MOE_ROWS)

        def start(b, c):
            @pl.when(needs(b))
            def _():
                zcopy(b).start()
            return c

        def wait(b, c):
            @pl.when(needs(b))
            def _():
                zcopy(b).wait()
            return c

        lax.fori_loop(0, nb, start, 0)
        lax.fori_loop(0, nb, wait, 0)

    def copy(t, k):
        return pltpu.make_async_copy(h_ref.at[pl.ds(t, 1)], xs_ref.at[pl.ds(dest_ref[0, k, t], 1)], sem)

    def start(t, c):
        copy(t, 0).start()
        copy(t, 1).start()
        return c

    def wait(t, c):
        copy(t, 0).wait()
        copy(t, 1).wait()
        return c

    lax.fori_loop(0, tm, start, 0)
    lax.fori_loop(0, tm, wait, 0)


def _dispatch(meta, dest, h2, tm, nb):
    t = h2.shape[0]
    kern = functools.partial(_dispatch_kernel, tm=tm, nb=nb)
    grid_spec = pltpu.PrefetchScalarGridSpec(
        num_scalar_prefetch=1,
        grid=(t // tm,),
        in_specs=[pl.BlockSpec((1, 2, tm), lambda i, m: (i, 0, 0), memory_space=pltpu.SMEM),
                  pl.BlockSpec((tm, D_MODEL), lambda i, m: (i, 0))],
        out_specs=pl.BlockSpec(memory_space=pl.ANY),
        scratch_shapes=[pltpu.VMEM((MOE_ROWS, D_MODEL), F32),
                        pltpu.SemaphoreType.DMA, pltpu.SemaphoreType.DMA],
    )
    return pl.pallas_call(
        kern,
        grid_spec=grid_spec,
        out_shape=jax.ShapeDtypeStruct((nb * MOE_ROWS, D_MODEL), F32),
        compiler_params=_cparams("arbitrary"),
        name="moe_dispatch",
    )(meta, dest, h2)


def _expert_kernel(meta_ref, x_ref, wg_ref, wu_ref, wd_ref, o_ref):
    b = pl.program_id(0)

    @pl.when(b < meta_ref[2, 0])
    def _():
        x = x_ref[...].astype(BF16)
        gate = jnp.dot(x, wg_ref[0, 0].astype(BF16), preferred_element_type=F32)
        up = jnp.dot(x, wu_ref[0, 0].astype(BF16), preferred_element_type=F32)
        act = (_silu(gate) * up).astype(BF16)
        o_ref[...] = jnp.dot(act, wd_ref[0, 0].astype(BF16), preferred_element_type=F32)


def _experts(meta, xs, wg, wu, wd, layer, nb):
    def blk(b, m):
        return (jnp.minimum(b, m[2, 0] - 1), 0)

    def wblk(b, m):
        return (layer, m[0, jnp.minimum(b, m[2, 0] - 1)], 0, 0)

    grid_spec = pltpu.PrefetchScalarGridSpec(
        num_scalar_prefetch=1,
        grid=(nb,),
        in_specs=[pl.BlockSpec((MOE_ROWS, D_MODEL), blk),
                  pl.BlockSpec((1, 1, D_MODEL, EXPERT_FF), wblk),
                  pl.BlockSpec((1, 1, D_MODEL, EXPERT_FF), wblk),
                  pl.BlockSpec((1, 1, EXPERT_FF, D_MODEL), wblk)],
        out_specs=pl.BlockSpec((MOE_ROWS, D_MODEL), blk),
    )
    return pl.pallas_call(
        _expert_kernel,
        grid_spec=grid_spec,
        out_shape=jax.ShapeDtypeStruct((nb * MOE_ROWS, D_MODEL), F32),
        compiler_params=_cparams("arbitrary"),
        name="moe_experts",
    )(meta, xs, wg, wu, wd)


def _combine_kernel(dest_ref, x_ref, r_ref, g_ref, ys_ref, o_ref, buf_ref, sem, *, tm, final):
    def copy(t, k):
        return pltpu.make_async_copy(ys_ref.at[pl.ds(dest_ref[0, k, t], 1)],
                                     buf_ref.at[k, pl.ds(t, 1)], sem)

    def start(t, c):
        copy(t, 0).start()
        copy(t, 1).start()
        return c

    def wait(t, c):
        copy(t, 0).wait()
        copy(t, 1).wait()
        return c

    lax.fori_loop(0, tm, start, 0)
    lax.fori_loop(0, tm, wait, 0)
    w0 = r_ref[:, 2:3]
    w1 = r_ref[:, 3:4]
    x = x_ref[...] + (buf_ref[0] * w0 + buf_ref[1] * w1)
    o_ref[...] = _rms(x, g_ref[...]) if final else x


def _combine(dest, x2, rcol, g, ys, tm, final):
    t = x2.shape[0]
    kern = functools.partial(_combine_kernel, tm=tm, final=final)
    return pl.pallas_call(
        kern,
        grid=(t // tm,),
        in_specs=[pl.BlockSpec((1, 2, tm), lambda i: (i, 0, 0), memory_space=pltpu.SMEM),
                  pl.BlockSpec((tm, D_MODEL), lambda i: (i, 0)),
                  pl.BlockSpec((tm, LANES), lambda i: (i, 0)),
                  pl.BlockSpec((1, D_MODEL), lambda i: (0, 0)),
                  pl.BlockSpec(memory_space=pl.ANY)],
        out_specs=pl.BlockSpec((tm, D_MODEL), lambda i: (i, 0)),
        out_shape=jax.ShapeDtypeStruct((t, D_MODEL), F32),
        scratch_shapes=[pltpu.VMEM((2, tm, D_MODEL), F32), pltpu.SemaphoreType.DMA],
        compiler_params=_cparams("arbitrary"),
        name="moe_combine",
    )(dest, x2, rcol, g, ys)


MOE_TILE = 256
CHUNK = 8
LOCAL_ROWS = 2 * MOE_TILE + 256
PACKED = D_MODEL // 2
XS_WIDTH = PACKED
U32 = jnp.uint32


def _pack_bf16_pairs(x, exact=False):
    if not exact:
        x = x.astype(BF16).astype(F32)
    half = x.shape[1] // 2
    lo = lax.bitcast_convert_type(x[:, :half], U32)
    hi = lax.bitcast_convert_type(x[:, half:], U32)
    return hi | (lo >> 16)


def _unpack_bf16_pairs(words):
    lo = lax.bitcast_convert_type(words << 16, F32)
    hi = lax.bitcast_convert_type(words & U32(0xFFFF0000), F32)
    return jnp.concatenate([lo, hi], axis=1).astype(BF16)
TAB_CHUNKS, TAB_LOCAL, TAB_GLOBAL, TAB_TOTAL, TAB_EXPERT = 0, 1, 2, 3, 4
N_TABS = 5
COPY_ROWS = (2 * CHUNK, CHUNK)


def _route_kernel(r_ref, cnt_ref, lrow_ref, lcol_ref, tab_ref, ctab_ref, meta_ref,
                  loff_ref, goff_ref, n8_ref, *, tm, nbp):
    i = pl.program_id(0)
    e_iota = _row_iota((N_EXPERTS, tm))
    oh0 = e_iota == r_ref[0:1, :].astype(I32)
    oh1 = e_iota == r_ref[1:2, :].astype(I32)
    oh = jnp.where(oh0 | oh1, 1.0, 0.0)
    tile_lane = _lane_iota((N_EXPERTS, LANES)) == i
    hi = lax.Precision.HIGHEST

    @pl.when(i == 0)
    def _():
        cnt = cnt_ref[...]
        n8 = jnp.floor((cnt + (CHUNK - 1)) * (1.0 / CHUNK)) * CHUNK
        er = _row_iota((N_EXPERTS, N_EXPERTS))
        ec = _lane_iota((N_EXPERTS, N_EXPERTS))
        below = jnp.where(er > ec, 1.0, 0.0)
        loff = jnp.dot(below, n8, preferred_element_type=F32, precision=hi)
        rows_e = jnp.sum(n8, axis=-1, keepdims=True) + jnp.zeros_like(n8)
        padded = jnp.floor((rows_e + (MOE_ROWS - 1)) * (1.0 / MOE_ROWS)) * MOE_ROWS
        e_start = jnp.dot(below, padded, preferred_element_type=F32, precision=hi)
        tr = _row_iota((LANES, LANES))
        tc = _lane_iota((LANES, LANES))
        earlier = jnp.where(tr < tc, 1.0, 0.0)
        goff = e_start + jnp.dot(n8, earlier, preferred_element_type=F32, precision=hi)
        loff_ref[...] = loff
        goff_ref[...] = goff
        n8_ref[...] = n8
        tab_ref[TAB_CHUNKS] = (n8 * (1.0 / CHUNK)).astype(I32)
        tab_ref[TAB_LOCAL] = loff.astype(I32)
        tab_ref[TAB_GLOBAL] = goff.astype(I32)
        big = jnp.floor(n8 * (0.5 / CHUNK))
        small = n8 * (1.0 / CHUNK) - 2.0 * big
        row_t = _row_iota(n8.shape)
        tab_ref[TAB_TOTAL] = jnp.where(row_t == 0, jnp.sum(big, axis=0, keepdims=True),
                                       jnp.where(row_t == 1, jnp.sum(small, axis=0, keepdims=True),
                                                 0.0)).astype(I32)
        lane_t = _lane_iota(n8.shape)
        tab_ref[TAB_EXPERT] = jnp.where(lane_t == 0, e_start, jnp.where(lane_t == 1, rows_e, 0.0)).astype(I32)
        reps = nbp // LANES
        pend_b = jnp.concatenate([e_start + padded] * reps, axis=1)
        vend_b = jnp.concatenate([e_start + rows_e] * reps, axis=1)
        used_b = jnp.concatenate([padded] * reps, axis=1) > 0.0
        b0 = (_lane_iota((N_EXPERTS, nbp)) * MOE_ROWS).astype(F32)
        bexp = jnp.sum(jnp.where(pend_b <= b0, 1.0, 0.0), axis=0, keepdims=True)
        bexp = jnp.minimum(bexp, N_EXPERTS - 1.0)
        e_b = _row_iota((N_EXPERTS, nbp)).astype(F32)
        is_e = e_b == bexp
        vend = jnp.sum(jnp.where(is_e, vend_b, 0.0), axis=0, keepdims=True)
        nvalid = jnp.clip(vend - b0[0:1], 0.0, float(MOE_ROWS))
        total = jnp.max(pend_b, axis=0, keepdims=True) * (1.0 / MOE_ROWS)
        order = jnp.sum(jnp.where(used_b & (e_b < bexp), 1.0, 0.0), axis=0, keepdims=True)
        nxt = jnp.min(jnp.where(used_b & (e_b > bexp), e_b, float(N_EXPERTS)), axis=0, keepdims=True)
        row = _row_iota((8, nbp))
        meta = jnp.where(row == 0, bexp, jnp.where(row == 1, nvalid, jnp.where(row == 2, total,
                         jnp.where(row == 3, order, jnp.where(row == 4, nxt, 0.0)))))
        meta_ref[...] = meta.astype(I32)

    su = jnp.where(_row_iota((tm, tm)) < _lane_iota((tm, tm)), 1.0, 0.0).astype(BF16)
    before = jnp.dot(oh.astype(BF16), su, preferred_element_type=F32)
    base = jnp.sum(jnp.where(tile_lane, loff_ref[...], 0.0), axis=-1, keepdims=True) + before
    d0 = jnp.sum(jnp.where(oh0, base, 0.0), axis=0, keepdims=True)
    d1 = jnp.sum(jnp.where(oh1, base, 0.0), axis=0, keepdims=True)
    lrow_ref[0, 0:1, :] = d0.astype(I32)
    lrow_ref[0, 1:2, :] = d1.astype(I32)
    pick_tile = lambda ref: jnp.sum(jnp.where(tile_lane, ref[...], 0.0), axis=-1, keepdims=True)
    nch = pick_tile(n8_ref) * (1.0 / CHUNK)
    n_big = jnp.floor(nch * 0.5)
    n_small = nch - 2.0 * n_big
    incl = jnp.where(_row_iota((N_EXPERTS, N_EXPERTS)) >= _lane_iota((N_EXPERTS, N_EXPERTS)), 1.0, 0.0)
    cidx = _lane_iota((N_EXPERTS, LANES)).astype(F32)
    loff_t = pick_tile(loff_ref)
    goff_t = pick_tile(goff_ref)
    for k, (n, rows, first) in enumerate(((n_big, COPY_ROWS[0], 0.0), (n_small, COPY_ROWS[1], n_big * COPY_ROWS[0]))):
        cend = jnp.dot(incl, n + jnp.zeros((N_EXPERTS, LANES), F32), preferred_element_type=F32, precision=hi)
        cstart = cend - n
        mine = (cidx >= cstart) & (cidx < cend)
        step_rows = first + (cidx - cstart) * rows
        ctab_ref[0, 2 * k:2 * k + 1, :] = jnp.sum(jnp.where(mine, loff_t + step_rows, 0.0), axis=0,
                                                  keepdims=True).astype(I32)
        ctab_ref[0, 2 * k + 1:2 * k + 2, :] = jnp.sum(jnp.where(mine, goff_t + step_rows, 0.0), axis=0,
                                                      keepdims=True).astype(I32)
    row = _row_iota((LANES, tm))
    lcol_ref[...] = jnp.where(row == 0, d0, jnp.where(row == 1, d1,
                              jnp.where(row == 2, r_ref[2:3, :], jnp.where(row == 3, r_ref[3:4, :], 0.0)))).T


def _route(rrow, cnt, tm, nbp):
    t = rrow.shape[1]
    nt = t // tm
    kern = functools.partial(_route_kernel, tm=tm, nbp=nbp)
    return pl.pallas_call(
        kern,
        grid=(nt,),
        in_specs=[pl.BlockSpec((8, tm), lambda i: (0, i)),
                  pl.BlockSpec((N_EXPERTS, LANES), lambda i: (0, 0))],
        out_specs=[pl.BlockSpec((1, 2, tm), lambda i: (i, 0, 0)),
                   pl.BlockSpec((tm, LANES), lambda i: (i, 0)),
                   pl.BlockSpec((N_TABS, N_EXPERTS, LANES), lambda i: (0, 0, 0)),
                   pl.BlockSpec((1, 2 * len(COPY_ROWS), LANES), lambda i: (i, 0, 0)),
                   pl.BlockSpec((8, nbp), lambda i: (0, 0))],
        out_shape=[jax.ShapeDtypeStruct((nt, 2, tm), I32), jax.ShapeDtypeStruct((t, LANES), F32),
                   jax.ShapeDtypeStruct((N_TABS, N_EXPERTS, LANES), I32), jax.ShapeDtypeStruct((nt, 2 * len(COPY_ROWS), LANES), I32),
                   jax.ShapeDtypeStruct((8, nbp), I32)],
        scratch_shapes=[pltpu.VMEM((N_EXPERTS, LANES), F32)] * 3,
        compiler_params=_cparams("arbitrary"),
        name="moe_route",
    )(rrow, cnt)


def _chunk_copies(tabs, i, local_ref, global_ref, sem, to_global, action):
    tab_ref, ctab_ref = tabs
    for k, rows in enumerate(COPY_ROWS):
        count = tab_ref[TAB_TOTAL, k, i]

        def copy(lo, go, rows=rows):
            lsl = local_ref.at[pl.ds(pl.multiple_of(lo, CHUNK), rows)]
            gsl = global_ref.at[pl.ds(pl.multiple_of(go, CHUNK), rows)]
            return pltpu.make_async_copy(lsl, gsl, sem) if to_global else pltpu.make_async_copy(gsl, lsl, sem)

        if action == "wait":
            def one(c, c1, copy=copy):
                copy(0, 0).wait()
                return c1
        else:
            def one(c, c1, copy=copy, k=k):
                copy(ctab_ref[i, 2 * k, c], ctab_ref[i, 2 * k + 1, c]).start()
                return c1

        lax.fori_loop(0, count, one, 0)


def _scatter_kernel(tab_ref, ctab_ref, meta_ref, lrow_ref, h_ref, xs_ref, buf_ref, zero_ref, sem, zsem, *, tm, nb):
    i = pl.program_id(0)
    tabs = (tab_ref, ctab_ref)

    @pl.when(i == 0)
    def _():
        zero_ref[...] = jnp.zeros_like(zero_ref)
        n_used = meta_ref[2, 0]

        def zcopy(b):
            sub = lax.shift_right_logical(meta_ref[1, b], int(math.log2(FFN_SUB)))
            start = pl.multiple_of(b * MOE_ROWS + sub * FFN_SUB, FFN_SUB)
            return pltpu.make_async_copy(zero_ref, xs_ref.at[pl.ds(start, FFN_SUB)], zsem)

        def needs(b):
            return (b < n_used) & ((meta_ref[1, b] & (FFN_SUB - 1)) != 0)

        def start(b, c):
            @pl.when(needs(b))
            def _():
                zcopy(b).start()
            return c

        def wait(b, c):
            @pl.when(needs(b))
            def _():
                zcopy(b).wait()
            return c

        lax.fori_loop(0, nb, start, 0)
        lax.fori_loop(0, nb, wait, 0)

    rows = _row_iota((LOCAL_ROWS, tm))
    p0 = rows == lrow_ref[0, 0:1, :]
    p1 = rows == lrow_ref[0, 1:2, :]
    perm = jnp.where(p0 | p1, 1.0, 0.0).astype(BF16)
    sorted_rows = jnp.dot(perm, h_ref[...], preferred_element_type=F32)

    def fill(slot):
        buf = buf_ref.at[slot]
        buf[...] = _pack_bf16_pairs(sorted_rows, exact=True)
        _chunk_copies(tabs, i, buf, xs_ref, sem.at[slot], True, "start")

    def drain(tile, slot):
        _chunk_copies(tabs, tile, buf_ref.at[slot], xs_ref, sem.at[slot], True, "wait")

    even = (i & 1) == 0

    @pl.when(even)
    def _():
        fill(0)

    @pl.when(jnp.logical_not(even))
    def _():
        fill(1)

    @pl.when((i > 0) & even)
    def _():
        drain(i - 1, 1)

    @pl.when((i > 0) & jnp.logical_not(even))
    def _():
        drain(i - 1, 0)

    @pl.when((i == pl.num_programs(0) - 1) & even)
    def _():
        drain(i, 0)

    @pl.when((i == pl.num_programs(0) - 1) & jnp.logical_not(even))
    def _():
        drain(i, 1)


def _scatter(tab, ctab, meta, lrow, h2, tm, nb):
    t = h2.shape[0]
    kern = functools.partial(_scatter_kernel, tm=tm, nb=nb)
    grid_spec = pltpu.PrefetchScalarGridSpec(
        num_scalar_prefetch=3,
        grid=(t // tm,),
        in_specs=[pl.BlockSpec((1, 2, tm), lambda i, *_: (i, 0, 0)),
                  pl.BlockSpec((tm, D_MODEL), lambda i, *_: (i, 0))],
        out_specs=pl.BlockSpec(memory_space=pl.ANY),
        scratch_shapes=[pltpu.VMEM((2, LOCAL_ROWS, XS_WIDTH), U32), pltpu.VMEM((FFN_SUB, XS_WIDTH), U32),
                        pltpu.SemaphoreType.DMA((2,)), pltpu.SemaphoreType.DMA],
    )
    return pl.pallas_call(
        kern,
        grid_spec=grid_spec,
        out_shape=jax.ShapeDtypeStruct((nb * MOE_ROWS, XS_WIDTH), U32),
        compiler_params=_cparams("arbitrary"),
        name="moe_scatter",
    )(tab, ctab, meta, lrow, h2)


def _ffn_kernel(meta_ref, x_ref, wg_ref, wu_ref, wd_ref, o_ref, wgu_b, wd_b, wg_f, wu_f, wd_f, wsem, *, layer):
    b = pl.program_id(0)
    live = b < meta_ref[2, 0]
    expert = meta_ref[0, b]
    prev = meta_ref[0, jnp.maximum(b - 1, 0)]

    def fetch(e, slot):
        return [pltpu.make_async_copy(src.at[layer, e], dst.at[slot], wsem.at[slot])
                for src, dst in ((wg_ref, wg_f), (wu_ref, wu_f), (wd_ref, wd_f))]

    def first_block(slot):
        @pl.when(b == 0)
        def _():
            for cp in fetch(expert, slot):
                cp.start()

        for cp in fetch(expert, slot):
            cp.wait()
        wgu_b[:, 0:EXPERT_FF] = wg_f[slot].astype(BF16)
        wgu_b[:, EXPERT_FF:2 * EXPERT_FF] = wu_f[slot].astype(BF16)
        wd_b[...] = wd_f[slot].astype(BF16)
        nxt = meta_ref[4, b]

        @pl.when(nxt < N_EXPERTS)
        def _():
            for cp in fetch(nxt, 1 - slot):
                cp.start()

    changed = live & ((b == 0) | (expert != prev))
    odd = (meta_ref[3, b] & 1) == 1

    @pl.when(changed & jnp.logical_not(odd))
    def _():
        first_block(0)

    @pl.when(changed & odd)
    def _():
        first_block(1)

    nvalid = meta_ref[1, jnp.maximum(jnp.minimum(b, meta_ref[2, 0] - 1), 0)]
    for sub in range(MOE_ROWS // FFN_SUB):
        rows = slice(sub * FFN_SUB, (sub + 1) * FFN_SUB)
        used = live & (nvalid > sub * FFN_SUB)

        @pl.when(used)
        def _():
            x = _unpack_bf16_pairs(x_ref[rows, 0:PACKED])
            gu = jnp.dot(x, wgu_b[...], preferred_element_type=F32)
            act = (_silu(gu[:, 0:EXPERT_FF]) * gu[:, EXPERT_FF:2 * EXPERT_FF]).astype(BF16)
            y = jnp.dot(act, wd_b[...], preferred_element_type=F32)
            o_ref[rows, :] = _pack_bf16_pairs(y)

        @pl.when(live & jnp.logical_not(used))
        def _():
            o_ref[rows, :] = jnp.zeros((FFN_SUB, PACKED), U32)


def _ffn(meta, xs, wg, wu, wd, layer, nb):
    def blk(b, m):
        return (jnp.maximum(jnp.minimum(b, m[2, 0] - 1), 0), 0)

    grid_spec = pltpu.PrefetchScalarGridSpec(
        num_scalar_prefetch=1,
        grid=(nb,),
        in_specs=[pl.BlockSpec((MOE_ROWS, XS_WIDTH), blk)] + [pl.BlockSpec(memory_space=pl.ANY)] * 3,
        out_specs=pl.BlockSpec((MOE_ROWS, PACKED), blk),
        scratch_shapes=[pltpu.VMEM((D_MODEL, 2 * EXPERT_FF), BF16), pltpu.VMEM((EXPERT_FF, D_MODEL), BF16),
                        pltpu.VMEM((2, D_MODEL, EXPERT_FF), F32), pltpu.VMEM((2, D_MODEL, EXPERT_FF), F32),
                        pltpu.VMEM((2, EXPERT_FF, D_MODEL), F32), pltpu.SemaphoreType.DMA((2,))],
    )
    return pl.pallas_call(
        functools.partial(_ffn_kernel, layer=layer),
        grid_spec=grid_spec,
        out_shape=jax.ShapeDtypeStruct((nb * MOE_ROWS, PACKED), U32),
        compiler_params=_cparams("arbitrary"),
        name="moe_experts",
    )(meta, xs, wg, wu, wd)


def _gather_kernel(tab_ref, ctab_ref, lcol_ref, x_ref, g_ref, ys_ref, o_ref, buf_ref, sem, *, tm, final):
    i = pl.program_id(0)

    last = pl.num_programs(0) - 1

    def fetch(tile, slot, action):
        _chunk_copies((tab_ref, ctab_ref), tile, buf_ref.at[slot], ys_ref, sem.at[slot], False, action)

    @pl.when(i == 0)
    def _():
        buf_ref[...] = jnp.zeros_like(buf_ref)
        fetch(0, 0, "start")

    even = (i & 1) == 0

    @pl.when((i < last) & even)
    def _():
        fetch(i + 1, 1, "start")

    @pl.when((i < last) & jnp.logical_not(even))
    def _():
        fetch(i + 1, 0, "start")

    col = _lane_iota((tm, LOCAL_ROWS)).astype(F32)
    pick0 = jnp.where(col == lcol_ref[:, 0:1], 1.0, 0.0).astype(BF16)
    pick1 = jnp.where(col == lcol_ref[:, 1:2], 1.0, 0.0).astype(BF16)

    def finish(slot):
        fetch(i, slot, "wait")
        y = _unpack_bf16_pairs(buf_ref[slot])
        both = jnp.dot(jnp.concatenate([pick0, pick1], axis=0), y, preferred_element_type=F32)
        x = x_ref[...] + lcol_ref[:, 2:3] * both[0:tm] + lcol_ref[:, 3:4] * both[tm:2 * tm]
        o_ref[...] = _rms(x, g_ref[...]) if final else x

    @pl.when(even)
    def _():
        finish(0)

    @pl.when(jnp.logical_not(even))
    def _():
        finish(1)


def _gather(tab, ctab, lcol, x2, g, ys, tm, final):
    t = x2.shape[0]
    kern = functools.partial(_gather_kernel, tm=tm, final=final)
    grid_spec = pltpu.PrefetchScalarGridSpec(
        num_scalar_prefetch=2,
        grid=(t // tm,),
        in_specs=[pl.BlockSpec((tm, LANES), lambda i, *_: (i, 0)),
                  pl.BlockSpec((tm, D_MODEL), lambda i, *_: (i, 0)),
                  pl.BlockSpec((1, D_MODEL), lambda i, *_: (0, 0)),
                  pl.BlockSpec(memory_space=pl.ANY)],
        out_specs=pl.BlockSpec((tm, D_MODEL), lambda i, *_: (i, 0)),
        scratch_shapes=[pltpu.VMEM((2, LOCAL_ROWS, PACKED), U32), pltpu.SemaphoreType.DMA((2,))],
    )
    return pl.pallas_call(
        kern,
        grid_spec=grid_spec,
        out_shape=jax.ShapeDtypeStruct((t, D_MODEL), F32),
        compiler_params=_cparams("arbitrary"),
        name="moe_combine",
    )(tab, ctab, lcol, x2, g, ys)


def _pad_rows(a, rows=8):
    return jnp.zeros((rows, a.shape[-1]), F32).at[:a.shape[0]].set(a.astype(F32))


def _arrange_w_in(w):
    gw = GROUP_WIDTH
    a0 = 0
    b0 = 3 * gw
    c0 = b0 + 3 * gw + N_HEADS
    d0 = c0 + gw + (gw + 4 * SSM_STATE) + N_HEADS
    half = MLA_ROPE // 2
    fox_q = (w[:, b0:b0 + gw] * (HEAD_DIM ** -0.5 * LOG2E)).astype(BF16)
    w = w.astype(BF16)
    f_logit = w[:, b0 + 3 * gw:b0 + 3 * gw + N_HEADS]
    dt_raw = w[:, c0 + 3 * gw:c0 + 3 * gw + N_HEADS]
    kr0 = d0 + MLA_Q_LORA + MLA_KV_LORA
    kr = w[:, kr0:kr0 + MLA_ROPE]
    kr_sw = jnp.concatenate([-kr[:, half:], kr[:, :half]], axis=1)
    zeros = lambda n: jnp.zeros((w.shape[0], n), w.dtype)
    misc = jnp.concatenate([f_logit, dt_raw, zeros(MISC_ROPE - 2 * N_HEADS), kr,
                            zeros(LANES - MISC_ROPE - MLA_ROPE)], axis=1)
    misc2 = jnp.concatenate([zeros(MISC_ROPE), kr_sw, zeros(LANES - MISC_ROPE - MLA_ROPE)], axis=1)
    out = jnp.concatenate([w[:, a0:a0 + 3 * gw], fox_q, w[:, b0 + gw:b0 + 2 * gw], w[:, c0:c0 + 3 * gw],
                           w[:, d0:d0 + MLA_Q_LORA + MLA_KV_LORA], misc, misc2], axis=1)
    fox_vt = w[:, b0 + 2 * gw:b0 + 3 * gw].T
    return out, fox_vt


def _arrange_mla(w_uq, w_ukv):
    half = MLA_ROPE // 2
    qd = MLA_NOPE + MLA_ROPE
    wq, wqs, wk, wv = [], [], [], []
    zq = jnp.zeros((MLA_Q_LORA, LANES - qd), w_uq.dtype)
    zk = jnp.zeros((MLA_KV_LORA, LANES - MLA_NOPE), w_ukv.dtype)
    for h in range(N_HEADS):
        q = w_uq[:, h * qd:(h + 1) * qd]
        nope, rope = q[:, :MLA_NOPE], q[:, MLA_NOPE:]
        wq.append(jnp.concatenate([nope, rope, zq], axis=1))
        wqs.append(jnp.concatenate([jnp.zeros_like(nope), -rope[:, half:], rope[:, :half], zq], axis=1))
        kv = w_ukv[:, h * 2 * MLA_NOPE:(h + 1) * 2 * MLA_NOPE]
        wk.append(jnp.concatenate([kv[:, :MLA_NOPE], zk], axis=1))
        wv.append(kv[:, MLA_NOPE:])
    cat = lambda xs: jnp.concatenate(xs, axis=1).astype(BF16)
    return cat(wq), cat(wqs), cat(wk), cat(wv).T


def kernel(x, positions, norm_mix, w_in, conv_a, fox_forget_bias, ssm_conv_w, ssm_conv_b, ssm_dt_bias,
           ssm_a_log, ssm_d, ssm_norm, mla_q_norm, mla_kv_norm, mla_w_uq, mla_w_ukv, w_out, norm_ffn,
           router_group_w, router_group_b, router_expert_w, router_expert_b, expert_w_gate, expert_w_up,
           expert_w_down, norm_final):
    batch, seq, d = x.shape
    t = batch * seq
    depth = w_in.shape[0]
    tm = min(512, t)
    tq = min(ATTN_TQ, seq)
    tmd = min(MOE_TILE, t)
    max_rows = 2 * t + (CHUNK - 1) * N_EXPERTS * (t // tmd) + N_EXPERTS * (MOE_ROWS - 1)
    nb = -(-max_rows // MOE_ROWS)
    nbp = -(-nb // LANES) * LANES

    xf = x.reshape(t, d)
    pos_col = positions.astype(F32).reshape(t, 1)
    cos, sin = _rope_tables(pos_col, tm)

    for l in range(depth):
        w_in_p, w_vt = _arrange_w_in(w_in[l])
        wq, wqs, wk, wv = _arrange_mla(mla_w_uq[l], mla_w_ukv[l])
        pa, pb, pc, misc, fox_vt, q, k, v = _inproj(
            xf, norm_mix[l][None, :], w_in_p, w_vt, cos, sin, mla_q_norm[l][None, :], mla_kv_norm[l][None, :],
            jnp.concatenate([wq, wqs], axis=1), wk, wv, tm)

        sp = jnp.zeros((8, LANES), F32)
        sp = sp.at[0, MISC_F:MISC_F + N_HEADS].set(fox_forget_bias[l])
        sp = sp.at[0, MISC_DT:MISC_DT + N_HEADS].set(ssm_dt_bias[l])
        sp = sp.at[1, MISC_DT:MISC_DT + N_HEADS].set(ssm_a_log[l])
        col, rows, fox_q, fox_k, tref = _scalar_prep(misc, sp, pb, batch, seq, tq)

        ya = _conv_mixer(pa, _pad_rows(conv_a[l]), batch, seq)
        yb = _attention(fox_q, fox_k, fox_vt, tref, batch, seq, tq, "fox_attention")
        conv_wb = _pad_rows(jnp.concatenate([ssm_conv_w[l], ssm_conv_b[l][None, :]], axis=0))
        ssd_par = _pad_rows(jnp.stack([jnp.repeat(ssm_d[l], HEAD_DIM), ssm_norm[l]]))
        yc = _ssd_mixer(pc, col, rows, conv_wb, ssd_par, batch, seq)
        yd = _attention(q, k, v, None, batch, seq, tq, "mla_attention")

        pad = jnp.zeros((d, LANES - N_EXPERTS - N_EXPERT_GROUPS), F32)
        wr = jnp.concatenate([router_expert_w[l], router_group_w[l], pad], axis=1)
        wr_hi = wr.astype(BF16)
        wr = jnp.concatenate([wr_hi, (wr - wr_hi.astype(F32)).astype(BF16)], axis=1)
        br = jnp.concatenate([router_expert_b[l], router_group_b[l], pad[0]])[None, :]
        x2, h2, rrow, cnt = _outproj(xf, ya, yb, yc, yd, w_out[l].astype(BF16), norm_ffn[l][None, :], wr, br,
                                     tm, tmd)

        lrow, lcol, tab, ctab, meta = _route(rrow, cnt, tmd, nbp)
        xs = _scatter(tab, ctab, meta, lrow, h2, tmd, nb)
        ys = _ffn(meta, xs, expert_w_gate, expert_w_up, expert_w_down, l, nb)
        final = l == depth - 1
        xf = _gather(tab, ctab, lcol, x2, norm_final[None, :], ys, tmd, final)

    return xf.reshape(batch, seq, d)


def _retile(dest, tm, tmd):
    if tm == tmd:
        return dest
    nt = dest.shape[0]
    return dest.reshape(nt, 2, tm // tmd, tmd).transpose(0, 2, 1, 3).reshape(nt * (tm // tmd), 2, tmd)
```

```python
import functools
import math

import jax
import jax.numpy as jnp
import numpy as np
from jax import lax
from jax.experimental import pallas as pl
from jax.experimental.pallas import tpu as pltpu

F32 = jnp.float32
BF16 = jnp.bfloat16
I32 = jnp.int32

LANES = 128
VMEM_LIMIT_BYTES = 56 * 1024 * 1024

D_MODEL = 1024
RMS_EPS = 1e-6
LOG2E = math.log2(math.e)
GROUP_WIDTH = 256
HEAD_DIM = 64
N_HEADS = 4

CONV_A_WIDTH = 3
SSM_CONV = 4
SSM_STATE = 64
SSM_CHUNK = 256

MLA_NOPE = 64
MLA_ROPE = 32
MLA_Q_LORA = 256
MLA_KV_LORA = 128
ROPE_BASE = 10000.0
MLA_CHUNK = 64
ATTN_TQ = 512
ATTN_RB = 128

N_EXPERT_GROUPS = 4
EXPERTS_PER_GROUP = 8
N_EXPERTS = 32
EXPERT_FF = 256
MOE_ROWS = 512
FFN_SUB = 256

SEG_A = (0, 768)
SEG_B = (768, 1280)
SEG_C = (1280, 2048)
SEG_D = (2048, 2432)
SEG_M = (2432, 2560)
SEG_M2 = (2560, 2688)
IN_COLS_PADDED = 2688
HEAD_PAD = N_HEADS * LANES
AUG_LANE = HEAD_DIM
MISC_F = 0
MISC_DT = 4
MISC_ROPE = 64
COL_CUMF = 0
COL_DT = 4
COL_ACUM = 8
N_SCALAR_ROWS = 16


def _cparams(*sem):
    return pltpu.CompilerParams(dimension_semantics=sem, vmem_limit_bytes=VMEM_LIMIT_BYTES)


def _lane_iota(shape):
    return lax.broadcasted_iota(I32, shape, len(shape) - 1)


def _row_iota(shape):
    return lax.broadcasted_iota(I32, shape, 0)


def _rms(x, g):
    ms = jnp.mean(x * x, axis=-1, keepdims=True)
    return x * lax.rsqrt(ms + RMS_EPS) * g


def _silu(x):
    return x / (1.0 + jnp.exp(-x))


def _softplus(x):
    return jnp.maximum(x, 0.0) + jnp.log(1.0 + jnp.exp(-jnp.abs(x)))


def _shift_rows(x, k):
    rolled = pltpu.roll(x, k, 0)
    return jnp.where(_row_iota(x.shape) >= k, rolled, 0.0)


def _rope_kernel(pos_ref, freq_ref, cos_ref, sin_ref):
    ang = pos_ref[...] * freq_ref[...]
    lane = _lane_iota(ang.shape)
    rope = (lane >= MISC_ROPE) & (lane < MISC_ROPE + MLA_ROPE)
    cos_ref[...] = jnp.where(rope, jnp.cos(ang), jnp.where(lane < MISC_ROPE, 1.0, 0.0))
    sin_ref[...] = jnp.where(rope, jnp.sin(ang), 0.0)


def _rope_tables(pos_col, tm):
    t = pos_col.shape[0]
    half = MLA_ROPE // 2
    inv = ROPE_BASE ** (-np.arange(0, MLA_ROPE, 2, dtype=np.float32) / MLA_ROPE)
    freq = np.zeros((1, LANES), np.float32)
    freq[0, MISC_ROPE:MISC_ROPE + half] = inv
    freq[0, MISC_ROPE + half:MISC_ROPE + MLA_ROPE] = inv
    return pl.pallas_call(
        _rope_kernel,
        grid=(t // tm,),
        in_specs=[pl.BlockSpec((tm, 1), lambda i: (i, 0)),
                  pl.BlockSpec((1, LANES), lambda i: (0, 0))],
        out_specs=[pl.BlockSpec((tm, LANES), lambda i: (i, 0))] * 2,
        out_shape=[jax.ShapeDtypeStruct((t, LANES), F32)] * 2,
        compiler_params=_cparams("parallel"),
        name="rope_tables",
    )(pos_col, jnp.asarray(freq))


def _arrange_w_in_kernel(win_ref, w_ref, wvt_ref):
    gw = GROUP_WIDTH
    col = lambda lo, hi: win_ref[0, :, lo:hi]
    w_ref[:, SEG_A[0]:SEG_A[1]] = col(0, 3 * gw).astype(BF16)
    w_ref[:, SEG_B[0]:SEG_B[0] + gw] = (col(3 * gw, 4 * gw) * (HEAD_DIM ** -0.5 * LOG2E)).astype(BF16)
    w_ref[:, SEG_B[0] + gw:SEG_B[1]] = col(4 * gw, 5 * gw).astype(BF16)
    wvt_ref[...] = col(5 * gw, 6 * gw).T.astype(BF16)
    c0 = 6 * gw
    win = col(c0, c0 + 7 * LANES)
    w_ref[:, SEG_C[0]:SEG_C[1]] = pltpu.roll(win, 7 * LANES - N_HEADS, 1)[:, 0:3 * gw].astype(BF16)
    d0 = c0 + 3 * gw
    win = col(d0, d0 + 4 * LANES)
    n_d = MLA_Q_LORA + MLA_KV_LORA
    w_ref[:, SEG_D[0]:SEG_D[1]] = pltpu.roll(win, 4 * LANES - 2 * N_HEADS, 1)[:, 0:n_d].astype(BF16)
    t_f = col(c0, c0 + LANES)
    t_dt = col(d0, d0 + LANES)
    t_kr = col(d0 + 3 * LANES, d0 + 4 * LANES)
    lane = _lane_iota(t_f.shape)
    half = MLA_ROPE // 2
    kr_lane0 = 2 * N_HEADS
    rope_lo = (lane >= MISC_ROPE) & (lane < MISC_ROPE + half)
    rope_hi = (lane >= MISC_ROPE + half) & (lane < MISC_ROPE + MLA_ROPE)
    kr = pltpu.roll(t_kr, MISC_ROPE - kr_lane0, 1)
    misc = jnp.where(lane < N_HEADS, t_f, jnp.where(lane < 2 * N_HEADS, t_dt,
                     jnp.where(rope_lo | rope_hi, kr, 0.0)))
    x2_first = pltpu.roll(t_kr, MISC_ROPE - kr_lane0 - half, 1)
    x1_second = pltpu.roll(t_kr, MISC_ROPE - kr_lane0 + half, 1)
    misc2 = jnp.where(rope_lo, -x2_first, jnp.where(rope_hi, x1_second, 0.0))
    w_ref[:, SEG_M[0]:SEG_M[1]] = misc.astype(BF16)
    w_ref[:, SEG_M2[0]:SEG_M2[1]] = misc2.astype(BF16)


def _inproj_kernel(x_ref, g_ref, win_ref, cos_ref, sin_ref, nq_ref, nkv_ref, wq2_ref, wk_ref, wvt2_ref,
                   oa, ob, oc, om, ovt, q_ref, k_ref, vt_ref, w_ref, wvt_ref):
    @pl.when(pl.program_id(0) == 0)
    def _():
        _arrange_w_in_kernel(win_ref, w_ref, wvt_ref)

    h = _rms(x_ref[...], g_ref[...]).astype(BF16)
    for o, (lo, hi) in ((oa, SEG_A), (ob, SEG_B), (oc, SEG_C)):
        o[...] = jnp.dot(h, w_ref[:, lo:hi], preferred_element_type=F32).astype(o.dtype)
    ovt[...] = lax.dot_general(wvt_ref[...], h, (((1,), (1,)), ((), ())),
                               preferred_element_type=F32).astype(ovt.dtype)
    misc = jnp.dot(h, w_ref[:, SEG_M[0]:SEG_M[1]], preferred_element_type=F32)
    misc2 = jnp.dot(h, w_ref[:, SEG_M2[0]:SEG_M2[1]], preferred_element_type=F32)
    om[...] = misc

    pd = jnp.dot(h, w_ref[:, SEG_D[0]:SEG_D[1]], preferred_element_type=F32)
    cq = _rms(pd[:, 0:MLA_Q_LORA], nq_ref[...]).astype(BF16)
    ckv = _rms(pd[:, MLA_Q_LORA:MLA_Q_LORA + MLA_KV_LORA], nkv_ref[...]).astype(BF16)
    cos = cos_ref[...]
    sin = sin_ref[...]
    cos4 = jnp.concatenate([cos] * N_HEADS, axis=1)
    sin4 = jnp.concatenate([sin] * N_HEADS, axis=1)
    scale = (MLA_NOPE + MLA_ROPE) ** -0.5 * LOG2E
    q2 = jnp.dot(cq, wq2_ref[...], preferred_element_type=F32)
    q_ref[...] = ((q2[:, 0:HEAD_PAD] * cos4 + q2[:, HEAD_PAD:2 * HEAD_PAD] * sin4) * scale).astype(q_ref.dtype)
    lane = _lane_iota(cos.shape)
    rope = (lane >= MISC_ROPE) & (lane < MISC_ROPE + MLA_ROPE)
    kr = jnp.where(rope, misc * cos + misc2 * sin, 0.0)
    k = jnp.dot(ckv, wk_ref[...], preferred_element_type=F32)
    k_ref[...] = (k + jnp.concatenate([kr] * N_HEADS, axis=1)).astype(k_ref.dtype)
    vt_ref[...] = lax.dot_general(wvt2_ref[...], ckv, (((1,), (1,)), ((), ())),
                                  preferred_element_type=F32).astype(vt_ref.dtype)


def _inproj(x, g, w_in_padded, layer, cos, sin, nq, nkv, wq2, wk, wvt2, tm):
    t = x.shape[0]
    w_block = pl.BlockSpec((1,) + w_in_padded.shape[1:], lambda i: (layer, 0, 0), pipeline_mode=pl.Buffered(1))
    full = lambda a: pl.BlockSpec(a.shape, lambda i: (0, 0))
    tile = lambda wd: pl.BlockSpec((tm, wd), lambda i: (i, 0))
    cols = lambda: pl.BlockSpec((GROUP_WIDTH, tm), lambda i: (0, i))
    seg = lambda s: s[1] - s[0]
    return pl.pallas_call(
        _inproj_kernel,
        grid=(t // tm,),
        in_specs=[tile(D_MODEL), full(g), w_block, tile(LANES), tile(LANES),
                  full(nq), full(nkv), full(wq2), full(wk), full(wvt2)],
        out_specs=[tile(seg(SEG_A)), tile(seg(SEG_B)), tile(seg(SEG_C)), tile(LANES), cols(),
                   tile(HEAD_PAD), tile(HEAD_PAD), cols()],
        out_shape=[jax.ShapeDtypeStruct((t, seg(SEG_A)), BF16), jax.ShapeDtypeStruct((t, seg(SEG_B)), BF16),
                   jax.ShapeDtypeStruct((t, seg(SEG_C)), BF16), jax.ShapeDtypeStruct((t, LANES), F32),
                   jax.ShapeDtypeStruct((GROUP_WIDTH, t), BF16),
                   jax.ShapeDtypeStruct((t, HEAD_PAD), BF16), jax.ShapeDtypeStruct((t, HEAD_PAD), BF16),
                   jax.ShapeDtypeStruct((GROUP_WIDTH, t), BF16)],
        scratch_shapes=[pltpu.VMEM((D_MODEL, IN_COLS_PADDED), BF16), pltpu.VMEM((GROUP_WIDTH, D_MODEL), BF16)],
        compiler_params=_cparams("arbitrary"),
        name="inproj",
    )(x, g, w_in_padded, cos, sin, nq, nkv, wq2, wk, wvt2)


def _scalar_prep_kernel(m_ref, p_ref, qk_ref, place_ref, const_ref,
                        col_ref, row_ref, qa_ref, ka_ref, tref_ref, *, tq):
    s = m_ref.shape[0]
    tref_ref[...] = jnp.zeros_like(tref_ref)
    tile_ref = jnp.zeros((1, LANES), F32)
    m = m_ref[...]
    bias = p_ref[0:1, :]
    a_log = p_ref[1:2, :]
    lane = _lane_iota(m.shape)
    z = m + bias
    logf = jnp.minimum(z, 0.0) - jnp.log(1.0 + jnp.exp(-jnp.abs(z)))
    dt = _softplus(z)
    a = dt * (-jnp.exp(a_log))
    is_f = lane < MISC_DT
    is_dt = (lane >= MISC_DT) & (lane < MISC_DT + N_HEADS)
    v = jnp.where(is_f, logf, jnp.where(is_dt, a, 0.0))
    r = _row_iota((SSM_CHUNK, SSM_CHUNK))
    c = _lane_iota((SSM_CHUNK, SSM_CHUNK))
    tril = jnp.where(r >= c, 1.0, 0.0).astype(BF16)
    carry = jnp.zeros((1, LANES), F32)
    lane_1 = _lane_iota((1, LANES))
    lane_b = _lane_iota((SSM_CHUNK, LANES))
    low = lane_b < HEAD_DIM
    aug_lanes = (lane_b >= AUG_LANE) & (lane_b < AUG_LANE + AUG_TERMS)
    for ci in range(s // SSM_CHUNK):
        rest = v[ci * SSM_CHUNK:(ci + 1) * SSM_CHUNK]
        cs = jnp.zeros((SSM_CHUNK, LANES), F32)
        for _ in range(3):
            term = rest.astype(BF16)
            cs = cs + jnp.dot(tril, term, preferred_element_type=F32)
            rest = rest - term.astype(F32)
        cs = cs + jnp.where(lane_1 < MISC_DT, carry, 0.0)
        carry = cs[SSM_CHUNK - 1:SSM_CHUNK]
        acum = pltpu.roll(cs, COL_ACUM - MISC_DT, 1)
        out = jnp.where(lane_b < MISC_DT, cs * LOG2E,
                        jnp.where(lane_b < COL_ACUM, dt[ci * SSM_CHUNK:(ci + 1) * SSM_CHUNK],
                                  jnp.where(lane_b < COL_ACUM + N_HEADS, acum, 0.0)))
        rows = slice(ci * SSM_CHUNK, (ci + 1) * SSM_CHUNK)
        col_ref[rows, :] = out
        row_ref[0, :, rows] = out.T[:N_SCALAR_ROWS]
        if (ci * SSM_CHUNK) % tq == 0:
            tile_ref = out[0:1, :]
            ti = (ci * SSM_CHUNK) // tq
            tref_ref[0, ti:ti + 1, :] = tile_ref
        c = out - tile_ref
        c_hi = c.astype(BF16)
        r1 = c - c_hi.astype(F32)
        c_mid = r1.astype(BF16)
        c_lo = (r1 - c_mid.astype(F32)).astype(BF16)
        compact = jnp.dot(jnp.concatenate([c_hi, c_mid, c_lo], axis=1), place_ref[...],
                          preferred_element_type=F32) + const_ref[0:1, :]
        for side, o_ref in enumerate((qa_ref, ka_ref)):
            dec = compact[:, side * LANES:(side + 1) * LANES]
            for h in range(N_HEADS):
                pair = qk_ref[rows, side * GROUP_WIDTH + (h // 2) * LANES:
                              side * GROUP_WIDTH + (h // 2 + 1) * LANES].astype(F32)
                feat = pair if h % 2 == 0 else pltpu.roll(pair, HEAD_DIM, 1)
                dec_h = pltpu.roll(dec, AUG_LANE - AUG_TERMS * h, 1)
                group = jnp.where(low, feat, jnp.where(aug_lanes, dec_h, 0.0))
                o_ref[rows, h * LANES:(h + 1) * LANES] = group.astype(o_ref.dtype)


AUG_TERMS = 6


def _fox_placement():
    place = np.zeros((3 * LANES, 2 * LANES), np.float32)
    const = np.zeros((8, 2 * LANES), np.float32)
    for h in range(N_HEADS):
        a0 = AUG_TERMS * h
        for term in range(3):
            place[term * LANES + COL_CUMF + h, a0 + term] = 1.0
            place[term * LANES + COL_CUMF + h, LANES + a0 + 3 + term] = -1.0
            const[0, a0 + 3 + term] = 1.0
            const[0, LANES + a0 + term] = 1.0
    return jnp.asarray(place, BF16), jnp.asarray(const, F32)


def _scalar_prep(misc, params, qk, batch, seq, tq):
    place, const = _fox_placement()
    full = lambda a: pl.BlockSpec(a.shape, lambda b: (0,) * a.ndim)
    return pl.pallas_call(
        functools.partial(_scalar_prep_kernel, tq=tq),
        grid=(batch,),
        in_specs=[pl.BlockSpec((seq, LANES), lambda b: (b, 0)),
                  pl.BlockSpec((8, LANES), lambda b: (0, 0)),
                  pl.BlockSpec((seq, 2 * GROUP_WIDTH), lambda b: (b, 0)),
                  full(place), full(const)],
        out_specs=[pl.BlockSpec((seq, LANES), lambda b: (b, 0)),
                   pl.BlockSpec((1, N_SCALAR_ROWS, seq), lambda b: (b, 0, 0)),
                   pl.BlockSpec((seq, HEAD_PAD), lambda b: (b, 0)),
                   pl.BlockSpec((seq, HEAD_PAD), lambda b: (b, 0)),
                   pl.BlockSpec((1, 8, LANES), lambda b: (b, 0, 0))],
        out_shape=[jax.ShapeDtypeStruct((batch * seq, LANES), F32),
                   jax.ShapeDtypeStruct((batch, N_SCALAR_ROWS, seq), F32),
                   jax.ShapeDtypeStruct((batch * seq, HEAD_PAD), BF16),
                   jax.ShapeDtypeStruct((batch * seq, HEAD_PAD), BF16),
                   jax.ShapeDtypeStruct((batch, 8, LANES), F32)],
        compiler_params=_cparams("parallel"),
        name="scalar_prep",
    )(misc, params, qk, place, const)


def _conv_mixer_kernel(p_ref, w_ref, o_ref):
    gw = GROUP_WIDTH
    b_gate = p_ref[:, 0:gw].astype(F32)
    cv = p_ref[:, gw:2 * gw].astype(F32) * p_ref[:, 2 * gw:3 * gw].astype(F32)
    acc = cv * w_ref[CONV_A_WIDTH - 1:CONV_A_WIDTH, :]
    for k in range(1, CONV_A_WIDTH):
        acc = acc + _shift_rows(cv, k) * w_ref[CONV_A_WIDTH - 1 - k:CONV_A_WIDTH - k, :]
    o_ref[...] = (b_gate * acc).astype(o_ref.dtype)


def _conv_mixer(pa, w, batch, seq):
    return pl.pallas_call(
        _conv_mixer_kernel,
        grid=(batch,),
        in_specs=[pl.BlockSpec((seq, 3 * GROUP_WIDTH), lambda b: (b, 0)),
                  pl.BlockSpec((8, GROUP_WIDTH), lambda b: (0, 0))],
        out_specs=pl.BlockSpec((seq, GROUP_WIDTH), lambda b: (b, 0)),
        out_shape=jax.ShapeDtypeStruct((batch * seq, GROUP_WIDTH), BF16),
        compiler_params=_cparams("parallel"),
        name="conv_mixer",
    )(pa, w)


def _pair_lanes(col, base, shape):
    lane = _lane_iota(shape)
    return jnp.where(lane < HEAD_DIM, col[:, base:base + 1], col[:, base + 1:base + 2])


def _ssd_kernel(p_ref, col_ref, row_ref, cw_ref, par_ref, o_ref, u_ref):
    s = p_ref.shape[0]
    q = SSM_CHUNK
    gw = GROUP_WIDTH
    xbc = p_ref[:, gw:3 * gw].astype(F32)
    acc = xbc * cw_ref[SSM_CONV - 1:SSM_CONV, :]
    for k in range(1, SSM_CONV):
        acc = acc + _shift_rows(xbc, k) * cw_ref[SSM_CONV - 1 - k:SSM_CONV - k, :]
    u_ref[...] = _silu(acc + cw_ref[SSM_CONV:SSM_CONV + 1, :])

    d_skip = par_ref[0:1, :]
    norm_g = par_ref[1:2, :]
    lane_q = _lane_iota((q, LANES))
    low = lane_q < HEAD_DIM
    tri = _row_iota((q, q)) >= _lane_iota((q, q))

    def chunk(ci, states):
        rows = pl.ds(ci * q, q)
        u = u_ref[rows, :]
        col = col_ref[rows, :]
        bm = u[:, gw:gw + LANES]
        cm = u[:, gw + LANES:gw + 2 * LANES]
        z = p_ref[rows, 0:gw].astype(F32)
        new_states = []
        ys = []
        for g in range(2):
            sel = low if g == 0 else jnp.logical_not(low)
            cg = jnp.where(sel, cm, 0.0).astype(BF16)
            bg = jnp.where(sel, bm, 0.0)
            gmat = lax.dot_general(cg, bm.astype(BF16), (((1,), (1,)), ((), ())),
                                   preferred_element_type=F32)
            xs = u[:, g * LANES:(g + 1) * LANES]
            dt2 = _pair_lanes(col, COL_DT + 2 * g, (q, LANES))
            ac2 = _pair_lanes(col, COL_ACUM + 2 * g, (q, LANES))
            xdt = xs * dt2
            xdt_b = xdt.astype(BF16)
            st = states[g]
            y_off = jnp.dot(cg, st.astype(BF16), preferred_element_type=F32) * jnp.exp(ac2)
            halves = []
            for hh in range(2):
                h = 2 * g + hh
                ac_col = col[:, COL_ACUM + h:COL_ACUM + h + 1]
                ac_row = row_ref[0, COL_ACUM + h:COL_ACUM + h + 1, rows]
                decay = jnp.exp(jnp.where(tri, ac_col - ac_row, -1e30))
                mm = (gmat * decay).astype(BF16)
                halves.append(jnp.dot(mm, xdt_b, preferred_element_type=F32))
            y = jnp.where(low, halves[0], halves[1]) + y_off + d_skip[:, g * LANES:(g + 1) * LANES] * xs
            ys.append(y)
            ac_last = ac2[q - 1:q, :]
            w_end = jnp.exp(ac_last - ac2)
            xw = (xdt * w_end).astype(BF16)
            upd = jnp.dot(bg.T.astype(BF16), xw, preferred_element_type=F32)
            new_states.append(st * jnp.exp(ac_last) + upd)
        yfull = jnp.concatenate(ys, axis=1) * _silu(z)
        o_ref[rows, :] = _rms(yfull, norm_g).astype(o_ref.dtype)
        return tuple(new_states)

    init = (jnp.zeros((LANES, LANES), F32), jnp.zeros((LANES, LANES), F32))
    states = init
    for ci in range(s // q):
        states = chunk(ci, states)


def _ssd_mixer(pc, col, rows, conv_wb, par, batch, seq):
    gw = GROUP_WIDTH
    return pl.pallas_call(
        _ssd_kernel,
        grid=(batch,),
        in_specs=[pl.BlockSpec((seq, 3 * gw), lambda b: (b, 0)),
                  pl.BlockSpec((seq, LANES), lambda b: (b, 0)),
                  pl.BlockSpec((1, N_SCALAR_ROWS, seq), lambda b: (b, 0, 0)),
                  pl.BlockSpec((8, 2 * gw), lambda b: (0, 0)),
                  pl.BlockSpec((8, gw), lambda b: (0, 0))],
        out_specs=pl.BlockSpec((seq, gw), lambda b: (b, 0)),
        out_shape=jax.ShapeDtypeStruct((batch * seq, gw), BF16),
        scratch_shapes=[pltpu.VMEM((seq, 2 * gw), F32)],
        compiler_params=_cparams("parallel"),
        name="ssd_mixer",
    )(pc, col, rows, conv_wb, par)


def _attn_kernel(*refs, fox, tq):
    if fox:
        tref_ref, q_ref, k_ref, vt_ref, o_ref = refs
    else:
        q_ref, k_ref, vt_ref, o_ref = refs
        tref_ref = None
    b = pl.program_id(0)
    i = pl.program_id(1)
    key = _row_iota((tq, tq))
    qry = _lane_iota((tq, tq))
    if fox:
        allowed = key <= qry
    else:
        shift = int(math.log2(MLA_CHUNK))
        allowed = (key >> shift) <= (qry >> shift)
    qs = [q_ref[:, h * LANES:(h + 1) * LANES] for h in range(N_HEADS)]
    ones_rows = jnp.ones((16, tq), BF16)

    def step(j, masked, carry):
        rk = pl.ds(pl.multiple_of(j * tq, tq), tq)
        scores = [lax.dot_general(k_ref[rk, h * LANES:(h + 1) * LANES], qs[h], (((1,), (1,)), ((), ())),
                                  preferred_element_type=F32) for h in range(N_HEADS)]
        probs = []
        for h in range(N_HEADS):
            m, l, _ = carry[h]
            s = scores[h]
            if masked:
                s = jnp.where(allowed, s, -1e30)
            delta = (tref_ref[b, i, h] - tref_ref[b, j, h]) if fox else 0.0
            m_new = jnp.maximum(m, jnp.max(s, axis=0, keepdims=True) + delta)
            alpha = jnp.exp2(m - m_new)
            p = jnp.exp2(s - (m_new - delta))
            probs.append((m_new, alpha, p.astype(BF16)))
        new = []
        for h in range(N_HEADS):
            pair = h // 2
            m_new, alpha, p = probs[h]
            lhs = jnp.concatenate([vt_ref[pair * LANES:(pair + 1) * LANES, rk], ones_rows], axis=0)
            pv = jnp.dot(lhs, p, preferred_element_type=F32)
            new.append((m_new, alpha * carry[h][1] + pv[LANES:LANES + 1], alpha * carry[h][2] + pv[0:LANES]))
        return tuple(new)

    init = tuple((jnp.full((1, tq), -1e30, F32), jnp.zeros((1, tq), F32), jnp.zeros((LANES, tq), F32))
                 for _ in range(N_HEADS))
    carry = lax.fori_loop(0, i, lambda j, c: step(j, False, c), init)
    carry = step(i, True, carry)
    outs = [acc / l for (_, l, acc) in carry]
    top = _row_iota((LANES, tq)) < HEAD_DIM
    o_t = jnp.concatenate([jnp.where(top, outs[0], outs[1]), jnp.where(top, outs[2], outs[3])], axis=0)
    o_ref[...] = o_t.T.astype(o_ref.dtype)


def _attention(q, k, vt, tref, batch, seq, tq, name):
    nq = seq // tq
    fox = tref is not None
    kern = functools.partial(_attn_kernel, fox=fox, tq=tq)
    grid_spec = pltpu.PrefetchScalarGridSpec(
        num_scalar_prefetch=1 if fox else 0,
        grid=(batch, nq),
        in_specs=[pl.BlockSpec((tq, HEAD_PAD), lambda b, i, *_: (b * nq + i, 0)),
                  pl.BlockSpec((seq, HEAD_PAD), lambda b, i, *_: (b, 0)),
                  pl.BlockSpec((GROUP_WIDTH, seq), lambda b, i, *_: (0, b))],
        out_specs=pl.BlockSpec((tq, GROUP_WIDTH), lambda b, i, *_: (b * nq + i, 0)),
    )
    args = ((tref,) if fox else ()) + (q, k, vt)
    return pl.pallas_call(
        kern,
        grid_spec=grid_spec,
        out_shape=jax.ShapeDtypeStruct((batch * seq, GROUP_WIDTH), BF16),
        compiler_params=_cparams("parallel", "arbitrary"),
        name=name,
    )(*args)


def _outproj_kernel(x_ref, ya, yb, yc, yd, w_ref, g_ref, wr_ref, br_ref,
                    x2_ref, h2_ref, rrow_ref, cnt_ref, *, tm, moe_tile):
    y = jnp.concatenate([ya[...], yb[...], yc[...], yd[...]], axis=1)
    x2 = x_ref[...] + jnp.dot(y, w_ref[...], preferred_element_type=F32)
    x2_ref[...] = x2
    h2 = _rms(x2, g_ref[...])
    h2_ref[...] = h2.astype(h2_ref.dtype)
    h_hi = h2.astype(BF16)
    h_lo = (h2 - h_hi.astype(F32)).astype(BF16)
    part = jnp.dot(h_hi, wr_ref[...], preferred_element_type=F32)
    logits = (part[:, 0:LANES] + part[:, LANES:2 * LANES]
              + jnp.dot(h_lo, wr_ref[:, 0:LANES], preferred_element_type=F32) + br_ref[...])
    lt = logits.T
    row = _row_iota(lt.shape)
    neg = -1e30
    big = 1 << 20
    gmask = (row >= N_EXPERTS) & (row < N_EXPERTS + N_EXPERT_GROUPS)
    gl = jnp.where(gmask, lt, neg)
    gmax = jnp.max(gl, axis=0, keepdims=True)
    gsum = jnp.sum(jnp.where(gmask, jnp.exp(gl - gmax), 0.0), axis=0, keepdims=True)
    g_w = 1.0 / gsum
    g_idx = jnp.min(jnp.where(gmask & (gl == gmax), row, big), axis=0, keepdims=True) - N_EXPERTS
    emask = (row < N_EXPERTS) & ((row >> int(math.log2(EXPERTS_PER_GROUP))) == g_idx)
    el = jnp.where(emask, lt, neg)
    e1v = jnp.max(el, axis=0, keepdims=True)
    esum = jnp.sum(jnp.where(emask, jnp.exp(el - e1v), 0.0), axis=0, keepdims=True)
    i1 = jnp.min(jnp.where(emask & (el == e1v), row, big), axis=0, keepdims=True)
    el2 = jnp.where(row == i1, neg, el)
    e2v = jnp.max(el2, axis=0, keepdims=True)
    i2 = jnp.min(jnp.where(emask & (row != i1) & (el2 == e2v), row, big), axis=0, keepdims=True)
    p1 = 1.0 / esum
    p2 = jnp.exp(e2v - e1v) / esum
    w1 = g_w * (p1 / (p1 + p2))
    w2 = g_w * (p2 / (p1 + p2))
    out_row = _row_iota(rrow_ref.shape)
    rrow_ref[...] = jnp.where(out_row == 0, i1.astype(F32),
                              jnp.where(out_row == 1, i2.astype(F32),
                                        jnp.where(out_row == 2, w1, jnp.where(out_row == 3, w2, 0.0))))
    step = pl.program_id(0)

    @pl.when(step == 0)
    def _():
        cnt_ref[...] = jnp.zeros_like(cnt_ref)

    chosen = jnp.where((row == i1) | (row == i2), 1.0, 0.0).astype(BF16)
    tiles = tm // moe_tile
    tile_of = (_row_iota((tm, LANES)) >> int(math.log2(moe_tile))) + step * tiles
    to_tile = jnp.where(_lane_iota((tm, LANES)) == tile_of, 1.0, 0.0).astype(BF16)
    counts = jnp.dot(chosen, to_tile, preferred_element_type=F32)
    cnt_ref[...] += counts[0:N_EXPERTS]


def _outproj(x, ya, yb, yc, yd, w, g, wr, br, tm, moe_tile):
    t = x.shape[0]
    full = lambda a: pl.BlockSpec(a.shape, lambda i: (0, 0))
    tile = lambda wd: pl.BlockSpec((tm, wd), lambda i: (i, 0))
    return pl.pallas_call(
        functools.partial(_outproj_kernel, tm=tm, moe_tile=moe_tile),
        grid=(t // tm,),
        in_specs=[tile(D_MODEL)] + [tile(GROUP_WIDTH)] * 4 + [full(w), full(g), full(wr), full(br)],
        out_specs=[tile(D_MODEL), tile(D_MODEL), pl.BlockSpec((8, tm), lambda i: (0, i)),
                   pl.BlockSpec((N_EXPERTS, LANES), lambda i: (0, 0))],
        out_shape=[jax.ShapeDtypeStruct((t, D_MODEL), F32), jax.ShapeDtypeStruct((t, D_MODEL), BF16),
                   jax.ShapeDtypeStruct((8, t), F32), jax.ShapeDtypeStruct((N_EXPERTS, LANES), F32)],
        compiler_params=_cparams("arbitrary"),
        name="outproj_router",
    )(x, ya, yb, yc, yd, w, g, wr, br)


def _positions_kernel(r_ref, dest_ref, meta_ref, cnt_ref, carry_ref, start_ref, *, tm, nbp):
    phase = pl.program_id(0)
    i = pl.program_id(1)
    e_iota = _row_iota((N_EXPERTS, tm))
    e0 = r_ref[0:1, :].astype(I32)
    e1 = r_ref[1:2, :].astype(I32)
    oh0 = e_iota == e0
    oh1 = e_iota == e1
    oh = jnp.where(oh0 | oh1, 1.0, 0.0)

    @pl.when((phase == 0) & (i == 0))
    def _():
        cnt_ref[...] = jnp.zeros_like(cnt_ref)

    @pl.when(phase == 0)
    def _():
        cnt_ref[...] += jnp.sum(oh, axis=-1, keepdims=True)

    @pl.when((phase == 1) & (i == 0))
    def _():
        cnt = cnt_ref[...]
        padded = jnp.floor((cnt + (MOE_ROWS - 1)) * (1.0 / MOE_ROWS)) * MOE_ROWS
        tril = jnp.where(_row_iota((N_EXPERTS, N_EXPERTS)) >= _lane_iota((N_EXPERTS, N_EXPERTS)), 1.0, 0.0)
        pend = jnp.dot(tril, padded, preferred_element_type=F32, precision=lax.Precision.HIGHEST)
        pstart = pend - padded
        start_ref[...] = pstart
        carry_ref[...] = jnp.zeros_like(carry_ref)
        pend_b = jnp.concatenate([pend] * (nbp // LANES), axis=1)
        vend_b = jnp.concatenate([pstart + cnt] * (nbp // LANES), axis=1)
        b0 = (_lane_iota((N_EXPERTS, nbp)) * MOE_ROWS).astype(F32)
        bexp = jnp.sum(jnp.where(pend_b <= b0, 1.0, 0.0), axis=0, keepdims=True)
        bexp = jnp.minimum(bexp, N_EXPERTS - 1.0)
        is_e = _row_iota((N_EXPERTS, nbp)).astype(F32) == bexp
        vend = jnp.sum(jnp.where(is_e, vend_b, 0.0), axis=0, keepdims=True)
        nvalid = jnp.clip(vend - b0[0:1], 0.0, float(MOE_ROWS))
        total = jnp.max(pend_b, axis=0, keepdims=True) * (1.0 / MOE_ROWS)
        row = _row_iota((8, nbp))
        meta = jnp.where(row == 0, bexp, jnp.where(row == 1, nvalid, jnp.where(row == 2, total, 0.0)))
        meta_ref[...] = meta.astype(I32)

    @pl.when(phase == 1)
    def _():
        su = jnp.where(_row_iota((tm, tm)) < _lane_iota((tm, tm)), 1.0, 0.0).astype(BF16)
        before = jnp.dot(oh.astype(BF16), su, preferred_element_type=F32)
        base = start_ref[:, 0:1] + carry_ref[:, 0:1] + before
        d0 = jnp.sum(jnp.where(oh0, base, 0.0), axis=0, keepdims=True)
        d1 = jnp.sum(jnp.where(oh1, base, 0.0), axis=0, keepdims=True)
        dest_ref[0, 0:1, :] = d0.astype(I32)
        dest_ref[0, 1:2, :] = d1.astype(I32)
        carry_ref[...] += jnp.sum(oh, axis=-1, keepdims=True)


def _positions(rrow, tm, nbp):
    t = rrow.shape[1]
    nt = t // tm
    kern = functools.partial(_positions_kernel, tm=tm, nbp=nbp)
    return pl.pallas_call(
        kern,
        grid=(2, nt),
        in_specs=[pl.BlockSpec((8, tm), lambda p, i: (0, i))],
        out_specs=[pl.BlockSpec((1, 2, tm), lambda p, i: (i * p, 0, 0)),
                   pl.BlockSpec((8, nbp), lambda p, i: (0, 0))],
        out_shape=[jax.ShapeDtypeStruct((nt, 2, tm), I32), jax.ShapeDtypeStruct((8, nbp), I32)],
        scratch_shapes=[pltpu.VMEM((N_EXPERTS, LANES), F32)] * 3,
        compiler_params=_cparams("arbitrary", "arbitrary"),
        name="moe_positions",
    )(rrow)


def _dispatch_kernel(meta_ref, dest_ref, h_ref, xs_ref, zero_ref, sem, zsem, *, tm, nb):
    i = pl.program_id(0)

    @pl.when(i == 0)
    def _():
        zero_ref[...] = jnp.zeros_like(zero_ref)
        n_used = meta_ref[2, 0]

        def zcopy(b):
            return pltpu.make_async_copy(zero_ref, xs_ref.at[pl.ds(b * MOE_ROWS, MOE_ROWS)], zsem)

        def needs(b):
            return (b < n_used) & (meta_ref[1, b] < MOE_ROWS)

        def start(b, c):
            @pl.when(needs(b))
            def _():
                zcopy(b).start()
            return c

        def wait(b, c):
            @pl.when(needs(b))
            def _():
                zcopy(b).wait()
            return c

        lax.fori_loop(0, nb, start, 0)
        lax.fori_loop(0, nb, wait, 0)

    def copy(t, k):
        return pltpu.make_async_copy(h_ref.at[pl.ds(t, 1)], xs_ref.at[pl.ds(dest_ref[0, k, t], 1)], sem)

    def start(t, c):
        copy(t, 0).start()
        copy(t, 1).start()
        return c

    def wait(t, c):
        copy(t, 0).wait()
        copy(t, 1).wait()
        return c

    lax.fori_loop(0, tm, start, 0)
    lax.fori_loop(0, tm, wait, 0)


def _dispatch(meta, dest, h2, tm, nb):
    t = h2.shape[0]
    kern = functools.partial(_dispatch_kernel, tm=tm, nb=nb)
    grid_spec = pltpu.PrefetchScalarGridSpec(
        num_scalar_prefetch=1,
        grid=(t // tm,),
        in_specs=[pl.BlockSpec((1, 2, tm), lambda i, m: (i, 0, 0), memory_space=pltpu.SMEM),
                  pl.BlockSpec((tm, D_MODEL), lambda i, m: (i, 0))],
        out_specs=pl.BlockSpec(memory_space=pl.ANY),
        scratch_shapes=[pltpu.VMEM((MOE_ROWS, D_MODEL), F32),
                        pltpu.SemaphoreType.DMA, pltpu.SemaphoreType.DMA],
    )
    return pl.pallas_call(
        kern,
        grid_spec=grid_spec,
        out_shape=jax.ShapeDtypeStruct((nb * MOE_ROWS, D_MODEL), F32),
        compiler_params=_cparams("arbitrary"),
        name="moe_dispatch",
    )(meta, dest, h2)


def _expert_kernel(meta_ref, x_ref, wg_ref, wu_ref, wd_ref, o_ref):
    b = pl.program_id(0)

    @pl.when(b < meta_ref[2, 0])
    def _():
        x = x_ref[...].astype(BF16)
        gate = jnp.dot(x, wg_ref[0, 0].astype(BF16), preferred_element_type=F32)
        up = jnp.dot(x, wu_ref[0, 0].astype(BF16), preferred_element_type=F32)
        act = (_silu(gate) * up).astype(BF16)
        o_ref[...] = jnp.dot(act, wd_ref[0, 0].astype(BF16), preferred_element_type=F32)


def _experts(meta, xs, wg, wu, wd, layer, nb):
    def blk(b, m):
        return (jnp.minimum(b, m[2, 0] - 1), 0)

    def wblk(b, m):
        return (layer, m[0, jnp.minimum(b, m[2, 0] - 1)], 0, 0)

    grid_spec = pltpu.PrefetchScalarGridSpec(
        num_scalar_prefetch=1,
        grid=(nb,),
        in_specs=[pl.BlockSpec((MOE_ROWS, D_MODEL), blk),
                  pl.BlockSpec((1, 1, D_MODEL, EXPERT_FF), wblk),
                  pl.BlockSpec((1, 1, D_MODEL, EXPERT_FF), wblk),
                  pl.BlockSpec((1, 1, EXPERT_FF, D_MODEL), wblk)],
        out_specs=pl.BlockSpec((MOE_ROWS, D_MODEL), blk),
    )
    return pl.pallas_call(
        _expert_kernel,
        grid_spec=grid_spec,
        out_shape=jax.ShapeDtypeStruct((nb * MOE_ROWS, D_MODEL), F32),
        compiler_params=_cparams("arbitrary"),
        name="moe_experts",
    )(meta, xs, wg, wu, wd)


def _combine_kernel(dest_ref, x_ref, r_ref, g_ref, ys_ref, o_ref, buf_ref, sem, *, tm, final):
    def copy(t, k):
        return pltpu.make_async_copy(ys_ref.at[pl.ds(dest_ref[0, k, t], 1)],
                                     buf_ref.at[k, pl.ds(t, 1)], sem)

    def start(t, c):
        copy(t, 0).start()
        copy(t, 1).start()
        return c

    def wait(t, c):
        copy(t, 0).wait()
        copy(t, 1).wait()
        return c

    lax.fori_loop(0, tm, start, 0)
    lax.fori_loop(0, tm, wait, 0)
    w0 = r_ref[:, 2:3]
    w1 = r_ref[:, 3:4]
    x = x_ref[...] + (buf_ref[0] * w0 + buf_ref[1] * w1)
    o_ref[...] = _rms(x, g_ref[...]) if final else x


def _combine(dest, x2, rcol, g, ys, tm, final):
    t = x2.shape[0]
    kern = functools.partial(_combine_kernel, tm=tm, final=final)
    return pl.pallas_call(
        kern,
        grid=(t // tm,),
        in_specs=[pl.BlockSpec((1, 2, tm), lambda i: (i, 0, 0), memory_space=pltpu.SMEM),
                  pl.BlockSpec((tm, D_MODEL), lambda i: (i, 0)),
                  pl.BlockSpec((tm, LANES), lambda i: (i, 0)),
                  pl.BlockSpec((1, D_MODEL), lambda i: (0, 0)),
                  pl.BlockSpec(memory_space=pl.ANY)],
        out_specs=pl.BlockSpec((tm, D_MODEL), lambda i: (i, 0)),
        out_shape=jax.ShapeDtypeStruct((t, D_MODEL), F32),
        scratch_shapes=[pltpu.VMEM((2, tm, D_MODEL), F32), pltpu.SemaphoreType.DMA],
        compiler_params=_cparams("arbitrary"),
        name="moe_combine",
    )(dest, x2, rcol, g, ys)


MOE_TILE = 256
CHUNK = 8
LOCAL_ROWS = 2 * MOE_TILE + 256
PACKED = D_MODEL // 2
XS_WIDTH = PACKED
U32 = jnp.uint32


def _pack_bf16_pairs(x, exact=False):
    if not exact:
        x = x.astype(BF16).astype(F32)
    half = x.shape[1] // 2
    lo = lax.bitcast_convert_type(x[:, :half], U32)
    hi = lax.bitcast_convert_type(x[:, half:], U32)
    return hi | (lo >> 16)


def _unpack_bf16_pairs(words):
    lo = lax.bitcast_convert_type(words << 16, F32)
    hi = lax.bitcast_convert_type(words & U32(0xFFFF0000), F32)
    return jnp.concatenate([lo, hi], axis=1).astype(BF16)
TAB_CHUNKS, TAB_LOCAL, TAB_GLOBAL, TAB_TOTAL, TAB_EXPERT = 0, 1, 2, 3, 4
N_TABS = 5
COPY_ROWS = (2 * CHUNK, CHUNK)


def _route_kernel(r_ref, cnt_ref, lrow_ref, lcol_ref, tab_ref, ctab_ref, meta_ref,
                  loff_ref, goff_ref, n8_ref, *, tm, nbp):
    i = pl.program_id(0)
    e_iota = _row_iota((N_EXPERTS, tm))
    oh0 = e_iota == r_ref[0:1, :].astype(I32)
    oh1 = e_iota == r_ref[1:2, :].astype(I32)
    oh = jnp.where(oh0 | oh1, 1.0, 0.0)
    tile_lane = _lane_iota((N_EXPERTS, LANES)) == i
    hi = lax.Precision.HIGHEST

    @pl.when(i == 0)
    def _():
        cnt = cnt_ref[...]
        n8 = jnp.floor((cnt + (CHUNK - 1)) * (1.0 / CHUNK)) * CHUNK
        er = _row_iota((N_EXPERTS, N_EXPERTS))
        ec = _lane_iota((N_EXPERTS, N_EXPERTS))
        below = jnp.where(er > ec, 1.0, 0.0)
        loff = jnp.dot(below, n8, preferred_element_type=F32, precision=hi)
        rows_e = jnp.sum(n8, axis=-1, keepdims=True) + jnp.zeros_like(n8)
        padded = jnp.floor((rows_e + (MOE_ROWS - 1)) * (1.0 / MOE_ROWS)) * MOE_ROWS
        e_start = jnp.dot(below, padded, preferred_element_type=F32, precision=hi)
        tr = _row_iota((LANES, LANES))
        tc = _lane_iota((LANES, LANES))
        earlier = jnp.where(tr < tc, 1.0, 0.0)
        goff = e_start + jnp.dot(n8, earlier, preferred_element_type=F32, precision=hi)
        loff_ref[...] = loff
        goff_ref[...] = goff
        n8_ref[...] = n8
        tab_ref[TAB_CHUNKS] = (n8 * (1.0 / CHUNK)).astype(I32)
        tab_ref[TAB_LOCAL] = loff.astype(I32)
        tab_ref[TAB_GLOBAL] = goff.astype(I32)
        big = jnp.floor(n8 * (0.5 / CHUNK))
        small = n8 * (1.0 / CHUNK) - 2.0 * big
        row_t = _row_iota(n8.shape)
        tab_ref[TAB_TOTAL] = jnp.where(row_t == 0, jnp.sum(big, axis=0, keepdims=True),
                                       jnp.where(row_t == 1, jnp.sum(small, axis=0, keepdims=True),
                                                 0.0)).astype(I32)
        lane_t = _lane_iota(n8.shape)
        tab_ref[TAB_EXPERT] = jnp.where(lane_t == 0, e_start, jnp.where(lane_t == 1, rows_e, 0.0)).astype(I32)
        reps = nbp // LANES
        pend_b = jnp.concatenate([e_start + padded] * reps, axis=1)
        vend_b = jnp.concatenate([e_start + rows_e] * reps, axis=1)
        used_b = jnp.concatenate([padded] * reps, axis=1) > 0.0
        b0 = (_lane_iota((N_EXPERTS, nbp)) * MOE_ROWS).astype(F32)
        bexp = jnp.sum(jnp.where(pend_b <= b0, 1.0, 0.0), axis=0, keepdims=True)
        bexp = jnp.minimum(bexp, N_EXPERTS - 1.0)
        e_b = _row_iota((N_EXPERTS, nbp)).astype(F32)
        is_e = e_b == bexp
        vend = jnp.sum(jnp.where(is_e, vend_b, 0.0), axis=0, keepdims=True)
        nvalid = jnp.clip(vend - b0[0:1], 0.0, float(MOE_ROWS))
        total = jnp.max(pend_b, axis=0, keepdims=True) * (1.0 / MOE_ROWS)
        order = jnp.sum(jnp.where(used_b & (e_b < bexp), 1.0, 0.0), axis=0, keepdims=True)
        nxt = jnp.min(jnp.where(used_b & (e_b > bexp), e_b, float(N_EXPERTS)), axis=0, keepdims=True)
        row = _row_iota((8, nbp))
        meta = jnp.where(row == 0, bexp, jnp.where(row == 1, nvalid, jnp.where(row == 2, total,
                         jnp.where(row == 3, order, jnp.where(row == 4, nxt, 0.0)))))
        meta_ref[...] = meta.astype(I32)

    su = jnp.where(_row_iota((tm, tm)) < _lane_iota((tm, tm)), 1.0, 0.0).astype(BF16)
    before = jnp.dot(oh.astype(BF16), su, preferred_element_type=F32)
    base = jnp.sum(jnp.where(tile_lane, loff_ref[...], 0.0), axis=-1, keepdims=True) + before
    d0 = jnp.sum(jnp.where(oh0, base, 0.0), axis=0, keepdims=True)
    d1 = jnp.sum(jnp.where(oh1, base, 0.0), axis=0, keepdims=True)
    lrow_ref[0, 0:1, :] = d0.astype(I32)
    lrow_ref[0, 1:2, :] = d1.astype(I32)
    pick_tile = lambda ref: jnp.sum(jnp.where(tile_lane, ref[...], 0.0), axis=-1, keepdims=True)
    nch = pick_tile(n8_ref) * (1.0 / CHUNK)
    n_big = jnp.floor(nch * 0.5)
    n_small = nch - 2.0 * n_big
    incl = jnp.where(_row_iota((N_EXPERTS, N_EXPERTS)) >= _lane_iota((N_EXPERTS, N_EXPERTS)), 1.0, 0.0)
    cidx = _lane_iota((N_EXPERTS, LANES)).astype(F32)
    loff_t = pick_tile(loff_ref)
    goff_t = pick_tile(goff_ref)
    for k, (n, rows, first) in enumerate(((n_big, COPY_ROWS[0], 0.0), (n_small, COPY_ROWS[1], n_big * COPY_ROWS[0]))):
        cend = jnp.dot(incl, n + jnp.zeros((N_EXPERTS, LANES), F32), preferred_element_type=F32, precision=hi)
        cstart = cend - n
        mine = (cidx >= cstart) & (cidx < cend)
        step_rows = first + (cidx - cstart) * rows
        ctab_ref[0, 2 * k:2 * k + 1, :] = jnp.sum(jnp.where(mine, loff_t + step_rows, 0.0), axis=0,
                                                  keepdims=True).astype(I32)
        ctab_ref[0, 2 * k + 1:2 * k + 2, :] = jnp.sum(jnp.where(mine, goff_t + step_rows, 0.0), axis=0,
                                                      keepdims=True).astype(I32)
    row = _row_iota((LANES, tm))
    lcol_ref[...] = jnp.where(row == 0, d0, jnp.where(row == 1, d1,
                              jnp.where(row == 2, r_ref[2:3, :], jnp.where(row == 3, r_ref[3:4, :], 0.0)))).T


def _route(rrow, cnt, tm, nbp):
    t = rrow.shape[1]
    nt = t // tm
    kern = functools.partial(_route_kernel, tm=tm, nbp=nbp)
    return pl.pallas_call(
        kern,
        grid=(nt,),
        in_specs=[pl.BlockSpec((8, tm), lambda i: (0, i)),
                  pl.BlockSpec((N_EXPERTS, LANES), lambda i: (0, 0))],
        out_specs=[pl.BlockSpec((1, 2, tm), lambda i: (i, 0, 0)),
                   pl.BlockSpec((tm, LANES), lambda i: (i, 0)),
                   pl.BlockSpec((N_TABS, N_EXPERTS, LANES), lambda i: (0, 0, 0)),
                   pl.BlockSpec((1, 2 * len(COPY_ROWS), LANES), lambda i: (i, 0, 0)),
                   pl.BlockSpec((8, nbp), lambda i: (0, 0))],
        out_shape=[jax.ShapeDtypeStruct((nt, 2, tm), I32), jax.ShapeDtypeStruct((t, LANES), F32),
                   jax.ShapeDtypeStruct((N_TABS, N_EXPERTS, LANES), I32), jax.ShapeDtypeStruct((nt, 2 * len(COPY_ROWS), LANES), I32),
                   jax.ShapeDtypeStruct((8, nbp), I32)],
        scratch_shapes=[pltpu.VMEM((N_EXPERTS, LANES), F32)] * 3,
        compiler_params=_cparams("arbitrary"),
        name="moe_route",
    )(rrow, cnt)


def _chunk_copies(tabs, i, local_ref, global_ref, sem, to_global, action):
    tab_ref, ctab_ref = tabs
    for k, rows in enumerate(COPY_ROWS):
        count = tab_ref[TAB_TOTAL, k, i]

        def copy(lo, go, rows=rows):
            lsl = local_ref.at[pl.ds(pl.multiple_of(lo, CHUNK), rows)]
            gsl = global_ref.at[pl.ds(pl.multiple_of(go, CHUNK), rows)]
            return pltpu.make_async_copy(lsl, gsl, sem) if to_global else pltpu.make_async_copy(gsl, lsl, sem)

        if action == "wait":
            def one(c, c1, copy=copy):
                copy(0, 0).wait()
                return c1
        else:
            def one(c, c1, copy=copy, k=k):
                copy(ctab_ref[i, 2 * k, c], ctab_ref[i, 2 * k + 1, c]).start()
                return c1

        lax.fori_loop(0, count, one, 0)


def _scatter_kernel(tab_ref, ctab_ref, meta_ref, lrow_ref, h_ref, xs_ref, buf_ref, zero_ref, sem, zsem, *, tm, nb):
    i = pl.program_id(0)
    tabs = (tab_ref, ctab_ref)

    @pl.when(i == 0)
    def _():
        zero_ref[...] = jnp.zeros_like(zero_ref)
        n_used = meta_ref[2, 0]

        def zcopy(b):
            sub = lax.shift_right_logical(meta_ref[1, b], int(math.log2(FFN_SUB)))
            start = pl.multiple_of(b * MOE_ROWS + sub * FFN_SUB, FFN_SUB)
            return pltpu.make_async_copy(zero_ref, xs_ref.at[pl.ds(start, FFN_SUB)], zsem)

        def needs(b):
            return (b < n_used) & ((meta_ref[1, b] & (FFN_SUB - 1)) != 0)

        def start(b, c):
            @pl.when(needs(b))
            def _():
                zcopy(b).start()
            return c

        def wait(b, c):
            @pl.when(needs(b))
            def _():
                zcopy(b).wait()
            return c

        lax.fori_loop(0, nb, start, 0)
        lax.fori_loop(0, nb, wait, 0)

    rows = _row_iota((LOCAL_ROWS, tm))
    p0 = rows == lrow_ref[0, 0:1, :]
    p1 = rows == lrow_ref[0, 1:2, :]
    perm = jnp.where(p0 | p1, 1.0, 0.0).astype(BF16)
    sorted_rows = jnp.dot(perm, h_ref[...], preferred_element_type=F32)

    def fill(slot):
        buf = buf_ref.at[slot]
        buf[...] = _pack_bf16_pairs(sorted_rows, exact=True)
        _chunk_copies(tabs, i, buf, xs_ref, sem.at[slot], True, "start")

    def drain(tile, slot):
        _chunk_copies(tabs, tile, buf_ref.at[slot], xs_ref, sem.at[slot], True, "wait")

    even = (i & 1) == 0

    @pl.when(even)
    def _():
        fill(0)

    @pl.when(jnp.logical_not(even))
    def _():
        fill(1)

    @pl.when((i > 0) & even)
    def _():
        drain(i - 1, 1)

    @pl.when((i > 0) & jnp.logical_not(even))
    def _():
        drain(i - 1, 0)

    @pl.when((i == pl.num_programs(0) - 1) & even)
    def _():
        drain(i, 0)

    @pl.when((i == pl.num_programs(0) - 1) & jnp.logical_not(even))
    def _():
        drain(i, 1)


def _scatter(tab, ctab, meta, lrow, h2, tm, nb):
    t = h2.shape[0]
    kern = functools.partial(_scatter_kernel, tm=tm, nb=nb)
    grid_spec = pltpu.PrefetchScalarGridSpec(
        num_scalar_prefetch=3,
        grid=(t // tm,),
        in_specs=[pl.BlockSpec((1, 2, tm), lambda i, *_: (i, 0, 0)),
                  pl.BlockSpec((tm, D_MODEL), lambda i, *_: (i, 0))],
        out_specs=pl.BlockSpec(memory_space=pl.ANY),
        scratch_shapes=[pltpu.VMEM((2, LOCAL_ROWS, XS_WIDTH), U32), pltpu.VMEM((FFN_SUB, XS_WIDTH), U32),
                        pltpu.SemaphoreType.DMA((2,)), pltpu.SemaphoreType.DMA],
    )
    return pl.pallas_call(
        kern,
        grid_spec=grid_spec,
        out_shape=jax.ShapeDtypeStruct((nb * MOE_ROWS, XS_WIDTH), U32),
        compiler_params=_cparams("arbitrary"),
        name="moe_scatter",
    )(tab, ctab, meta, lrow, h2)


def _ffn_kernel(meta_ref, x_ref, wg_ref, wu_ref, wd_ref, o_ref, wgu_b, wd_b, wg_f, wu_f, wd_f, wsem, *, layer):
    b = pl.program_id(0)
    live = b < meta_ref[2, 0]
    expert = meta_ref[0, b]
    prev = meta_ref[0, jnp.maximum(b - 1, 0)]

    def fetch(e, slot):
        return [pltpu.make_async_copy(src.at[layer, e], dst.at[slot], wsem.at[slot])
                for src, dst in ((wg_ref, wg_f), (wu_ref, wu_f), (wd_ref, wd_f))]

    def first_block(slot):
        @pl.when(b == 0)
        def _():
            for cp in fetch(expert, slot):
                cp.start()

        for cp in fetch(expert, slot):
            cp.wait()
        wgu_b[:, 0:EXPERT_FF] = wg_f[slot].astype(BF16)
        wgu_b[:, EXPERT_FF:2 * EXPERT_FF] = wu_f[slot].astype(BF16)
        wd_b[...] = wd_f[slot].astype(BF16)
        nxt = meta_ref[4, b]

        @pl.when(nxt < N_EXPERTS)
        def _():
            for cp in fetch(nxt, 1 - slot):
                cp.start()

    changed = live & ((b == 0) | (expert != prev))
    odd = (meta_ref[3, b] & 1) == 1

    @pl.when(changed & jnp.logical_not(odd))
    def _():
        first_block(0)

    @pl.when(changed & odd)
    def _():
        first_block(1)

    nvalid = meta_ref[1, jnp.maximum(jnp.minimum(b, meta_ref[2, 0] - 1), 0)]
    for sub in range(MOE_ROWS // FFN_SUB):
        rows = slice(sub * FFN_SUB, (sub + 1) * FFN_SUB)
        used = live & (nvalid > sub * FFN_SUB)

        @pl.when(used)
        def _():
            x = _unpack_bf16_pairs(x_ref[rows, 0:PACKED])
            gu = jnp.dot(x, wgu_b[...], preferred_element_type=F32)
            act = (_silu(gu[:, 0:EXPERT_FF]) * gu[:, EXPERT_FF:2 * EXPERT_FF]).astype(BF16)
            y = jnp.dot(act, wd_b[...], preferred_element_type=F32)
            o_ref[rows, :] = _pack_bf16_pairs(y)

        @pl.when(live & jnp.logical_not(used))
        def _():
            o_ref[rows, :] = jnp.zeros((FFN_SUB, PACKED), U32)


def _ffn(meta, xs, wg, wu, wd, layer, nb):
    def blk(b, m):
        return (jnp.maximum(jnp.minimum(b, m[2, 0] - 1), 0), 0)

    grid_spec = pltpu.PrefetchScalarGridSpec(
        num_scalar_prefetch=1,
        grid=(nb,),
        in_specs=[pl.BlockSpec((MOE_ROWS, XS_WIDTH), blk)] + [pl.BlockSpec(memory_space=pl.ANY)] * 3,
        out_specs=pl.BlockSpec((MOE_ROWS, PACKED), blk),
        scratch_shapes=[pltpu.VMEM((D_MODEL, 2 * EXPERT_FF), BF16), pltpu.VMEM((EXPERT_FF, D_MODEL), BF16),
                        pltpu.VMEM((2, D_MODEL, EXPERT_FF), F32), pltpu.VMEM((2, D_MODEL, EXPERT_FF), F32),
                        pltpu.VMEM((2, EXPERT_FF, D_MODEL), F32), pltpu.SemaphoreType.DMA((2,))],
    )
    return pl.pallas_call(
        functools.partial(_ffn_kernel, layer=layer),
        grid_spec=grid_spec,
        out_shape=jax.ShapeDtypeStruct((nb * MOE_ROWS, PACKED), U32),
        compiler_params=_cparams("arbitrary"),
        name="moe_experts",
    )(meta, xs, wg, wu, wd)


def _gather_kernel(tab_ref, ctab_ref, lcol_ref, x_ref, g_ref, ys_ref, o_ref, buf_ref, sem, *, tm, final):
    i = pl.program_id(0)

    last = pl.num_programs(0) - 1

    def fetch(tile, slot, action):
        _chunk_copies((tab_ref, ctab_ref), tile, buf_ref.at[slot], ys_ref, sem.at[slot], False, action)

    @pl.when(i == 0)
    def _():
        buf_ref[...] = jnp.zeros_like(buf_ref)
        fetch(0, 0, "start")

    even = (i & 1) == 0

    @pl.when((i < last) & even)
    def _():
        fetch(i + 1, 1, "start")

    @pl.when((i < last) & jnp.logical_not(even))
    def _():
        fetch(i + 1, 0, "start")

    col = _lane_iota((tm, LOCAL_ROWS)).astype(F32)
    pick0 = jnp.where(col == lcol_ref[:, 0:1], 1.0, 0.0).astype(BF16)
    pick1 = jnp.where(col == lcol_ref[:, 1:2], 1.0, 0.0).astype(BF16)

    def finish(slot):
        fetch(i, slot, "wait")
        y = _unpack_bf16_pairs(buf_ref[slot])
        both = jnp.dot(jnp.concatenate([pick0, pick1], axis=0), y, preferred_element_type=F32)
        x = x_ref[...] + lcol_ref[:, 2:3] * both[0:tm] + lcol_ref[:, 3:4] * both[tm:2 * tm]
        o_ref[...] = _rms(x, g_ref[...]) if final else x

    @pl.when(even)
    def _():
        finish(0)

    @pl.when(jnp.logical_not(even))
    def _():
        finish(1)


def _gather(tab, ctab, lcol, x2, g, ys, tm, final):
    t = x2.shape[0]
    kern = functools.partial(_gather_kernel, tm=tm, final=final)
    grid_spec = pltpu.PrefetchScalarGridSpec(
        num_scalar_prefetch=2,
        grid=(t // tm,),
        in_specs=[pl.BlockSpec((tm, LANES), lambda i, *_: (i, 0)),
                  pl.BlockSpec((tm, D_MODEL), lambda i, *_: (i, 0)),
                  pl.BlockSpec((1, D_MODEL), lambda i, *_: (0, 0)),
                  pl.BlockSpec(memory_space=pl.ANY)],
        out_specs=pl.BlockSpec((tm, D_MODEL), lambda i, *_: (i, 0)),
        scratch_shapes=[pltpu.VMEM((2, LOCAL_ROWS, PACKED), U32), pltpu.SemaphoreType.DMA((2,))],
    )
    return pl.pallas_call(
        kern,
        grid_spec=grid_spec,
        out_shape=jax.ShapeDtypeStruct((t, D_MODEL), F32),
        compiler_params=_cparams("arbitrary"),
        name="moe_combine",
    )(tab, ctab, lcol, x2, g, ys)


def _pad_rows(a, rows=8):
    return jnp.zeros((rows, a.shape[-1]), F32).at[:a.shape[0]].set(a.astype(F32))


def _arrange_mla(w_uq, w_ukv):
    half = MLA_ROPE // 2
    qd = MLA_NOPE + MLA_ROPE
    wq, wqs, wk, wv = [], [], [], []
    zq = jnp.zeros((MLA_Q_LORA, LANES - qd), w_uq.dtype)
    zk = jnp.zeros((MLA_KV_LORA, LANES - MLA_NOPE), w_ukv.dtype)
    for h in range(N_HEADS):
        q = w_uq[:, h * qd:(h + 1) * qd]
        nope, rope = q[:, :MLA_NOPE], q[:, MLA_NOPE:]
        wq.append(jnp.concatenate([nope, rope, zq], axis=1))
        wqs.append(jnp.concatenate([jnp.zeros_like(nope), -rope[:, half:], rope[:, :half], zq], axis=1))
        kv = w_ukv[:, h * 2 * MLA_NOPE:(h + 1) * 2 * MLA_NOPE]
        wk.append(jnp.concatenate([kv[:, :MLA_NOPE], zk], axis=1))
        wv.append(kv[:, MLA_NOPE:])
    cat = lambda xs: jnp.concatenate(xs, axis=1).astype(BF16)
    return cat(wq), cat(wqs), cat(wk), cat(wv).T


def kernel(x, positions, norm_mix, w_in, conv_a, fox_forget_bias, ssm_conv_w, ssm_conv_b, ssm_dt_bias,
           ssm_a_log, ssm_d, ssm_norm, mla_q_norm, mla_kv_norm, mla_w_uq, mla_w_ukv, w_out, norm_ffn,
           router_group_w, router_group_b, router_expert_w, router_expert_b, expert_w_gate, expert_w_up,
           expert_w_down, norm_final):
    batch, seq, d = x.shape
    t = batch * seq
    depth = w_in.shape[0]
    tm = min(512, t)
    tq = min(ATTN_TQ, seq)
    tmd = min(MOE_TILE, t)
    max_rows = 2 * t + (CHUNK - 1) * N_EXPERTS * (t // tmd) + N_EXPERTS * (MOE_ROWS - 1)
    nb = -(-max_rows // MOE_ROWS)
    nbp = -(-nb // LANES) * LANES

    xf = x.reshape(t, d)
    pos_col = positions.astype(F32).reshape(t, 1)
    cos, sin = _rope_tables(pos_col, tm)
    in_cols = w_in.shape[-1]
    w_in_padded = jnp.pad(w_in, ((0, 0), (0, 0), (0, -(-in_cols // LANES) * LANES - in_cols)))

    for l in range(depth):
        wq, wqs, wk, wv = _arrange_mla(mla_w_uq[l], mla_w_ukv[l])
        pa, pb, pc, misc, fox_vt, q, k, v = _inproj(
            xf, norm_mix[l][None, :], w_in_padded, l, cos, sin, mla_q_norm[l][None, :], mla_kv_norm[l][None, :],
            jnp.concatenate([wq, wqs], axis=1), wk, wv, tm)

        sp = jnp.zeros((8, LANES), F32)
        sp = sp.at[0, MISC_F:MISC_F + N_HEADS].set(fox_forget_bias[l])
        sp = sp.at[0, MISC_DT:MISC_DT + N_HEADS].set(ssm_dt_bias[l])
        sp = sp.at[1, MISC_DT:MISC_DT + N_HEADS].set(ssm_a_log[l])
        col, rows, fox_q, fox_k, tref = _scalar_prep(misc, sp, pb, batch, seq, tq)

        ya = _conv_mixer(pa, _pad_rows(conv_a[l]), batch, seq)
        yb = _attention(fox_q, fox_k, fox_vt, tref, batch, seq, tq, "fox_attention")
        conv_wb = _pad_rows(jnp.concatenate([ssm_conv_w[l], ssm_conv_b[l][None, :]], axis=0))
        ssd_par = _pad_rows(jnp.stack([jnp.repeat(ssm_d[l], HEAD_DIM), ssm_norm[l]]))
        yc = _ssd_mixer(pc, col, rows, conv_wb, ssd_par, batch, seq)
        yd = _attention(q, k, v, None, batch, seq, tq, "mla_attention")

        pad = jnp.zeros((d, LANES - N_EXPERTS - N_EXPERT_GROUPS), F32)
        wr = jnp.concatenate([router_expert_w[l], router_group_w[l], pad], axis=1)
        wr_hi = wr.astype(BF16)
        wr = jnp.concatenate([wr_hi, (wr - wr_hi.astype(F32)).astype(BF16)], axis=1)
        br = jnp.concatenate([router_expert_b[l], router_group_b[l], pad[0]])[None, :]
        x2, h2, rrow, cnt = _outproj(xf, ya, yb, yc, yd, w_out[l].astype(BF16), norm_ffn[l][None, :], wr, br,
                                     tm, tmd)

        lrow, lcol, tab, ctab, meta = _route(rrow, cnt, tmd, nbp)
        xs = _scatter(tab, ctab, meta, lrow, h2, tmd, nb)
        ys = _ffn(meta, xs, expert_w_gate, expert_w_up, expert_w_down, l, nb)
        final = l == depth - 1
        xf = _gather(tab, ctab, lcol, x2, norm_final[None, :], ys, tmd, final)

    return xf.reshape(batch, seq, d)


def _retile(dest, tm, tmd):
    if tm == tmd:
        return dest
    nt = dest.shape[0]
    return dest.reshape(nt, 2, tm // tmd, tmd).transpose(0, 2, 1, 3).reshape(nt * (tm // tmd), 2, tmd)
```

```python
import functools
import math

import jax
import jax.numpy as jnp
import numpy as np
from jax import lax
from jax.experimental import pallas as pl
from jax.experimental.pallas import tpu as pltpu

F32 = jnp.float32
BF16 = jnp.bfloat16
I32 = jnp.int32

LANES = 128
VMEM_LIMIT_BYTES = 56 * 1024 * 1024

D_MODEL = 1024
RMS_EPS = 1e-6
LOG2E = math.log2(math.e)
GROUP_WIDTH = 256
HEAD_DIM = 64
N_HEADS = 4

CONV_A_WIDTH = 3
SSM_CONV = 4
SSM_STATE = 64
SSM_CHUNK = 256

MLA_NOPE = 64
MLA_ROPE = 32
MLA_Q_LORA = 256
MLA_KV_LORA = 128
ROPE_BASE = 10000.0
MLA_CHUNK = 64
ATTN_TQ = 512
ATTN_RB = 128

N_EXPERT_GROUPS = 4
EXPERTS_PER_GROUP = 8
N_EXPERTS = 32
EXPERT_FF = 256
MOE_ROWS = 512
FFN_SUB = 256

SEG_A = (0, 768)
SEG_B = (768, 1280)
SEG_C = (1280, 2048)
SEG_D = (2048, 2432)
SEG_M = (2432, 2560)
SEG_M2 = (2560, 2688)
IN_COLS_PADDED = 2688
HEAD_PAD = N_HEADS * LANES
AUG_LANE = HEAD_DIM
MISC_F = 0
MISC_DT = 4
MISC_ROPE = 64
COL_CUMF = 0
COL_DT = 4
COL_ACUM = 8
N_SCALAR_ROWS = 16


def _cparams(*sem):
    return pltpu.CompilerParams(dimension_semantics=sem, vmem_limit_bytes=VMEM_LIMIT_BYTES)


def _lane_iota(shape):
    return lax.broadcasted_iota(I32, shape, len(shape) - 1)


def _row_iota(shape):
    return lax.broadcasted_iota(I32, shape, 0)


def _rms(x, g):
    ms = jnp.mean(x * x, axis=-1, keepdims=True)
    return x * lax.rsqrt(ms + RMS_EPS) * g


def _silu(x):
    return x / (1.0 + jnp.exp(-x))


def _softplus(x):
    return jnp.maximum(x, 0.0) + jnp.log(1.0 + jnp.exp(-jnp.abs(x)))


def _shift_rows(x, k):
    rolled = pltpu.roll(x, k, 0)
    return jnp.where(_row_iota(x.shape) >= k, rolled, 0.0)


def _rope_kernel(pos_ref, freq_ref, cos_ref, sin_ref):
    ang = pos_ref[...] * freq_ref[...]
    lane = _lane_iota(ang.shape)
    rope = (lane >= MISC_ROPE) & (lane < MISC_ROPE + MLA_ROPE)
    cos_ref[...] = jnp.where(rope, jnp.cos(ang), jnp.where(lane < MISC_ROPE, 1.0, 0.0))
    sin_ref[...] = jnp.where(rope, jnp.sin(ang), 0.0)


def _rope_tables(pos_col, tm):
    t = pos_col.shape[0]
    half = MLA_ROPE // 2
    inv = ROPE_BASE ** (-np.arange(0, MLA_ROPE, 2, dtype=np.float32) / MLA_ROPE)
    freq = np.zeros((1, LANES), np.float32)
    freq[0, MISC_ROPE:MISC_ROPE + half] = inv
    freq[0, MISC_ROPE + half:MISC_ROPE + MLA_ROPE] = inv
    return pl.pallas_call(
        _rope_kernel,
        grid=(t // tm,),
        in_specs=[pl.BlockSpec((tm, 1), lambda i: (i, 0)),
                  pl.BlockSpec((1, LANES), lambda i: (0, 0))],
        out_specs=[pl.BlockSpec((tm, LANES), lambda i: (i, 0))] * 2,
        out_shape=[jax.ShapeDtypeStruct((t, LANES), F32)] * 2,
        compiler_params=_cparams("parallel"),
        name="rope_tables",
    )(pos_col, jnp.asarray(freq))


def _arrange_w_in_kernel(win_ref, tail_ref, w_ref, wvt_ref):
    gw = GROUP_WIDTH
    col = lambda lo, hi: win_ref[0, :, lo:hi]
    w_ref[:, SEG_A[0]:SEG_A[1]] = col(0, 3 * gw).astype(BF16)
    w_ref[:, SEG_B[0]:SEG_B[0] + gw] = (col(3 * gw, 4 * gw) * (HEAD_DIM ** -0.5 * LOG2E)).astype(BF16)
    w_ref[:, SEG_B[0] + gw:SEG_B[1]] = col(4 * gw, 5 * gw).astype(BF16)
    wvt_ref[...] = col(5 * gw, 6 * gw).T.astype(BF16)
    c0 = 6 * gw
    win = col(c0, c0 + 7 * LANES)
    w_ref[:, SEG_C[0]:SEG_C[1]] = pltpu.roll(win, 7 * LANES - N_HEADS, 1)[:, 0:3 * gw].astype(BF16)
    d0 = c0 + 3 * gw
    tail = lambda lo, hi: tail_ref[0, :, lo:hi]
    win = tail(0, 4 * LANES)
    n_d = MLA_Q_LORA + MLA_KV_LORA
    w_ref[:, SEG_D[0]:SEG_D[1]] = pltpu.roll(win, 4 * LANES - 2 * N_HEADS, 1)[:, 0:n_d].astype(BF16)
    t_f = col(c0, c0 + LANES)
    t_dt = tail(0, LANES)
    t_kr = tail(3 * LANES, 4 * LANES)
    lane = _lane_iota(t_f.shape)
    half = MLA_ROPE // 2
    kr_lane0 = 2 * N_HEADS
    rope_lo = (lane >= MISC_ROPE) & (lane < MISC_ROPE + half)
    rope_hi = (lane >= MISC_ROPE + half) & (lane < MISC_ROPE + MLA_ROPE)
    kr = pltpu.roll(t_kr, MISC_ROPE - kr_lane0, 1)
    misc = jnp.where(lane < N_HEADS, t_f, jnp.where(lane < 2 * N_HEADS, t_dt,
                     jnp.where(rope_lo | rope_hi, kr, 0.0)))
    x2_first = pltpu.roll(t_kr, MISC_ROPE - kr_lane0 - half, 1)
    x1_second = pltpu.roll(t_kr, MISC_ROPE - kr_lane0 + half, 1)
    misc2 = jnp.where(rope_lo, -x2_first, jnp.where(rope_hi, x1_second, 0.0))
    w_ref[:, SEG_M[0]:SEG_M[1]] = misc.astype(BF16)
    w_ref[:, SEG_M2[0]:SEG_M2[1]] = misc2.astype(BF16)


def _inproj_kernel(x_ref, g_ref, win_ref, tail_ref, cw_ref, cos_ref, sin_ref, nq_ref, nkv_ref, wq2_ref, wk_ref,
                   wvt2_ref, oa, ob, oc, om, ovt, q_ref, k_ref, vt_ref, w_ref, wvt_ref, halo_ref,
                   *, tm, tiles_per_seq):
    @pl.when(pl.program_id(0) == 0)
    def _():
        _arrange_w_in_kernel(win_ref, tail_ref, w_ref, wvt_ref)

    h = _rms(x_ref[...], g_ref[...]).astype(BF16)
    for o, (lo, hi) in ((ob, SEG_B), (oc, SEG_C)):
        o[...] = jnp.dot(h, w_ref[:, lo:hi], preferred_element_type=F32).astype(o.dtype)

    gw = GROUP_WIDTH
    pa = jnp.dot(h, w_ref[:, SEG_A[0]:SEG_A[1]], preferred_element_type=F32)
    cv = pa[:, gw:2 * gw] * pa[:, 2 * gw:3 * gw]

    @pl.when(pl.program_id(0) % tiles_per_seq == 0)
    def _():
        halo_ref[...] = jnp.zeros_like(halo_ref)

    halo = halo_ref[...]
    row8 = _row_iota(halo.shape)
    acc = cv * cw_ref[CONV_A_WIDTH - 1:CONV_A_WIDTH, :]
    for k in range(1, CONV_A_WIDTH):
        shifted = pltpu.roll(cv, k, 0)
        top = jnp.where(row8 < k, pltpu.roll(halo, k, 0), shifted[0:8])
        shifted = jnp.concatenate([top, shifted[8:]], axis=0)
        acc = acc + shifted * cw_ref[CONV_A_WIDTH - 1 - k:CONV_A_WIDTH - k, :]
    halo_ref[...] = cv[tm - 8:tm]
    oa[...] = (pa[:, 0:gw] * acc).astype(oa.dtype)
    ovt[...] = lax.dot_general(wvt_ref[...], h, (((1,), (1,)), ((), ())),
                               preferred_element_type=F32).astype(ovt.dtype)
    misc = jnp.dot(h, w_ref[:, SEG_M[0]:SEG_M[1]], preferred_element_type=F32)
    misc2 = jnp.dot(h, w_ref[:, SEG_M2[0]:SEG_M2[1]], preferred_element_type=F32)
    om[...] = misc

    pd = jnp.dot(h, w_ref[:, SEG_D[0]:SEG_D[1]], preferred_element_type=F32)
    cq = _rms(pd[:, 0:MLA_Q_LORA], nq_ref[...]).astype(BF16)
    ckv = _rms(pd[:, MLA_Q_LORA:MLA_Q_LORA + MLA_KV_LORA], nkv_ref[...]).astype(BF16)
    cos = cos_ref[...]
    sin = sin_ref[...]
    cos4 = jnp.concatenate([cos] * N_HEADS, axis=1)
    sin4 = jnp.concatenate([sin] * N_HEADS, axis=1)
    scale = (MLA_NOPE + MLA_ROPE) ** -0.5 * LOG2E
    q2 = jnp.dot(cq, wq2_ref[...], preferred_element_type=F32)
    q_ref[...] = ((q2[:, 0:HEAD_PAD] * cos4 + q2[:, HEAD_PAD:2 * HEAD_PAD] * sin4) * scale).astype(q_ref.dtype)
    lane = _lane_iota(cos.shape)
    rope = (lane >= MISC_ROPE) & (lane < MISC_ROPE + MLA_ROPE)
    kr = jnp.where(rope, misc * cos + misc2 * sin, 0.0)
    k = jnp.dot(ckv, wk_ref[...], preferred_element_type=F32)
    k_ref[...] = (k + jnp.concatenate([kr] * N_HEADS, axis=1)).astype(k_ref.dtype)
    vt_ref[...] = lax.dot_general(wvt2_ref[...], ckv, (((1,), (1,)), ((), ())),
                                  preferred_element_type=F32).astype(vt_ref.dtype)


def _inproj(x, g, w_in, w_tail, layer, conv_w, cos, sin, nq, nkv, wq2, wk, wvt2, tm, seq):
    t = x.shape[0]
    once = lambda a: pl.BlockSpec((1,) + a.shape[1:], lambda i: (layer, 0, 0), pipeline_mode=pl.Buffered(1))
    full = lambda a: pl.BlockSpec(a.shape, lambda i: (0, 0))
    tile = lambda wd: pl.BlockSpec((tm, wd), lambda i: (i, 0))
    cols = lambda: pl.BlockSpec((GROUP_WIDTH, tm), lambda i: (0, i))
    seg = lambda s: s[1] - s[0]
    return pl.pallas_call(
        functools.partial(_inproj_kernel, tm=tm, tiles_per_seq=seq // tm),
        grid=(t // tm,),
        in_specs=[tile(D_MODEL), full(g), once(w_in), once(w_tail), full(conv_w), tile(LANES), tile(LANES),
                  full(nq), full(nkv), full(wq2), full(wk), full(wvt2)],
        out_specs=[tile(GROUP_WIDTH), tile(seg(SEG_B)), tile(seg(SEG_C)), tile(LANES), cols(),
                   tile(HEAD_PAD), tile(HEAD_PAD), cols()],
        out_shape=[jax.ShapeDtypeStruct((t, GROUP_WIDTH), BF16), jax.ShapeDtypeStruct((t, seg(SEG_B)), BF16),
                   jax.ShapeDtypeStruct((t, seg(SEG_C)), BF16), jax.ShapeDtypeStruct((t, LANES), F32),
                   jax.ShapeDtypeStruct((GROUP_WIDTH, t), BF16),
                   jax.ShapeDtypeStruct((t, HEAD_PAD), BF16), jax.ShapeDtypeStruct((t, HEAD_PAD), BF16),
                   jax.ShapeDtypeStruct((GROUP_WIDTH, t), BF16)],
        scratch_shapes=[pltpu.VMEM((D_MODEL, IN_COLS_PADDED), BF16), pltpu.VMEM((GROUP_WIDTH, D_MODEL), BF16),
                        pltpu.VMEM((8, GROUP_WIDTH), F32)],
        compiler_params=_cparams("arbitrary"),
        name="inproj",
    )(x, g, w_in, w_tail, conv_w, cos, sin, nq, nkv, wq2, wk, wvt2)


def _scalar_prep_kernel(m_ref, p_ref, qk_ref, place_ref, const_ref,
                        col_ref, row_ref, qa_ref, ka_ref, tref_ref, *, tq):
    s = m_ref.shape[0]
    tref_ref[...] = jnp.zeros_like(tref_ref)
    tile_ref = jnp.zeros((1, LANES), F32)
    m = m_ref[...]
    bias = p_ref[0:1, :]
    a_log = p_ref[1:2, :]
    lane = _lane_iota(m.shape)
    z = m + bias
    logf = jnp.minimum(z, 0.0) - jnp.log(1.0 + jnp.exp(-jnp.abs(z)))
    dt = _softplus(z)
    a = dt * (-jnp.exp(a_log))
    is_f = lane < MISC_DT
    is_dt = (lane >= MISC_DT) & (lane < MISC_DT + N_HEADS)
    v = jnp.where(is_f, logf, jnp.where(is_dt, a, 0.0))
    r = _row_iota((SSM_CHUNK, SSM_CHUNK))
    c = _lane_iota((SSM_CHUNK, SSM_CHUNK))
    tril = jnp.where(r >= c, 1.0, 0.0).astype(BF16)
    carry = jnp.zeros((1, LANES), F32)
    lane_1 = _lane_iota((1, LANES))
    lane_b = _lane_iota((SSM_CHUNK, LANES))
    low = lane_b < HEAD_DIM
    aug_lanes = (lane_b >= AUG_LANE) & (lane_b < AUG_LANE + AUG_TERMS)
    for ci in range(s // SSM_CHUNK):
        rest = v[ci * SSM_CHUNK:(ci + 1) * SSM_CHUNK]
        cs = jnp.zeros((SSM_CHUNK, LANES), F32)
        for _ in range(3):
            term = rest.astype(BF16)
            cs = cs + jnp.dot(tril, term, preferred_element_type=F32)
            rest = rest - term.astype(F32)
        cs = cs + jnp.where(lane_1 < MISC_DT, carry, 0.0)
        carry = cs[SSM_CHUNK - 1:SSM_CHUNK]
        acum = pltpu.roll(cs, COL_ACUM - MISC_DT, 1)
        out = jnp.where(lane_b < MISC_DT, cs * LOG2E,
                        jnp.where(lane_b < COL_ACUM, dt[ci * SSM_CHUNK:(ci + 1) * SSM_CHUNK],
                                  jnp.where(lane_b < COL_ACUM + N_HEADS, acum, 0.0)))
        rows = slice(ci * SSM_CHUNK, (ci + 1) * SSM_CHUNK)
        col_ref[rows, :] = out
        row_ref[0, :, rows] = out.T[:N_SCALAR_ROWS]
        if (ci * SSM_CHUNK) % tq == 0:
            tile_ref = out[0:1, :]
            ti = (ci * SSM_CHUNK) // tq
            tref_ref[0, ti:ti + 1, :] = tile_ref
        c = out - tile_ref
        c_hi = c.astype(BF16)
        r1 = c - c_hi.astype(F32)
        c_mid = r1.astype(BF16)
        c_lo = (r1 - c_mid.astype(F32)).astype(BF16)
        compact = jnp.dot(jnp.concatenate([c_hi, c_mid, c_lo], axis=1), place_ref[...],
                          preferred_element_type=F32) + const_ref[0:1, :]
        for side, o_ref in enumerate((qa_ref, ka_ref)):
            dec = compact[:, side * LANES:(side + 1) * LANES]
            for h in range(N_HEADS):
                pair = qk_ref[rows, side * GROUP_WIDTH + (h // 2) * LANES:
                              side * GROUP_WIDTH + (h // 2 + 1) * LANES].astype(F32)
                feat = pair if h % 2 == 0 else pltpu.roll(pair, HEAD_DIM, 1)
                dec_h = pltpu.roll(dec, AUG_LANE - AUG_TERMS * h, 1)
                group = jnp.where(low, feat, jnp.where(aug_lanes, dec_h, 0.0))
                o_ref[rows, h * LANES:(h + 1) * LANES] = group.astype(o_ref.dtype)


AUG_TERMS = 6


def _fox_placement():
    place = np.zeros((3 * LANES, 2 * LANES), np.float32)
    const = np.zeros((8, 2 * LANES), np.float32)
    for h in range(N_HEADS):
        a0 = AUG_TERMS * h
        for term in range(3):
            place[term * LANES + COL_CUMF + h, a0 + term] = 1.0
            place[term * LANES + COL_CUMF + h, LANES + a0 + 3 + term] = -1.0
            const[0, a0 + 3 + term] = 1.0
            const[0, LANES + a0 + term] = 1.0
    return jnp.asarray(place, BF16), jnp.asarray(const, F32)


def _scalar_prep(misc, params, qk, batch, seq, tq):
    place, const = _fox_placement()
    full = lambda a: pl.BlockSpec(a.shape, lambda b: (0,) * a.ndim)
    return pl.pallas_call(
        functools.partial(_scalar_prep_kernel, tq=tq),
        grid=(batch,),
        in_specs=[pl.BlockSpec((seq, LANES), lambda b: (b, 0)),
                  pl.BlockSpec((8, LANES), lambda b: (0, 0)),
                  pl.BlockSpec((seq, 2 * GROUP_WIDTH), lambda b: (b, 0)),
                  full(place), full(const)],
        out_specs=[pl.BlockSpec((seq, LANES), lambda b: (b, 0)),
                   pl.BlockSpec((1, N_SCALAR_ROWS, seq), lambda b: (b, 0, 0)),
                   pl.BlockSpec((seq, HEAD_PAD), lambda b: (b, 0)),
                   pl.BlockSpec((seq, HEAD_PAD), lambda b: (b, 0)),
                   pl.BlockSpec((1, 8, LANES), lambda b: (b, 0, 0))],
        out_shape=[jax.ShapeDtypeStruct((batch * seq, LANES), F32),
                   jax.ShapeDtypeStruct((batch, N_SCALAR_ROWS, seq), F32),
                   jax.ShapeDtypeStruct((batch * seq, HEAD_PAD), BF16),
                   jax.ShapeDtypeStruct((batch * seq, HEAD_PAD), BF16),
                   jax.ShapeDtypeStruct((batch, 8, LANES), F32)],
        compiler_params=_cparams("parallel"),
        name="scalar_prep",
    )(misc, params, qk, place, const)


def _conv_mixer_kernel(p_ref, w_ref, o_ref):
    gw = GROUP_WIDTH
    b_gate = p_ref[:, 0:gw].astype(F32)
    cv = p_ref[:, gw:2 * gw].astype(F32) * p_ref[:, 2 * gw:3 * gw].astype(F32)
    acc = cv * w_ref[CONV_A_WIDTH - 1:CONV_A_WIDTH, :]
    for k in range(1, CONV_A_WIDTH):
        acc = acc + _shift_rows(cv, k) * w_ref[CONV_A_WIDTH - 1 - k:CONV_A_WIDTH - k, :]
    o_ref[...] = (b_gate * acc).astype(o_ref.dtype)


def _conv_mixer(pa, w, batch, seq):
    return pl.pallas_call(
        _conv_mixer_kernel,
        grid=(batch,),
        in_specs=[pl.BlockSpec((seq, 3 * GROUP_WIDTH), lambda b: (b, 0)),
                  pl.BlockSpec((8, GROUP_WIDTH), lambda b: (0, 0))],
        out_specs=pl.BlockSpec((seq, GROUP_WIDTH), lambda b: (b, 0)),
        out_shape=jax.ShapeDtypeStruct((batch * seq, GROUP_WIDTH), BF16),
        compiler_params=_cparams("parallel"),
        name="conv_mixer",
    )(pa, w)


def _pair_lanes(col, base, shape):
    lane = _lane_iota(shape)
    return jnp.where(lane < HEAD_DIM, col[:, base:base + 1], col[:, base + 1:base + 2])


def _ssd_kernel(p_ref, col_ref, row_ref, cw_ref, par_ref, o_ref, u_ref):
    s = p_ref.shape[0]
    q = SSM_CHUNK
    gw = GROUP_WIDTH
    xbc = p_ref[:, gw:3 * gw].astype(F32)
    acc = xbc * cw_ref[SSM_CONV - 1:SSM_CONV, :]
    for k in range(1, SSM_CONV):
        acc = acc + _shift_rows(xbc, k) * cw_ref[SSM_CONV - 1 - k:SSM_CONV - k, :]
    u_ref[...] = _silu(acc + cw_ref[SSM_CONV:SSM_CONV + 1, :])

    d_skip = par_ref[0:1, :]
    norm_g = par_ref[1:2, :]
    lane_q = _lane_iota((q, LANES))
    low = lane_q < HEAD_DIM
    tri = _row_iota((q, q)) >= _lane_iota((q, q))

    def chunk(ci, states):
        rows = pl.ds(ci * q, q)
        u = u_ref[rows, :]
        col = col_ref[rows, :]
        bm = u[:, gw:gw + LANES]
        cm = u[:, gw + LANES:gw + 2 * LANES]
        z = p_ref[rows, 0:gw].astype(F32)
        new_states = []
        ys = []
        for g in range(2):
            sel = low if g == 0 else jnp.logical_not(low)
            cg = jnp.where(sel, cm, 0.0).astype(BF16)
            bg = jnp.where(sel, bm, 0.0)
            gmat = lax.dot_general(cg, bm.astype(BF16), (((1,), (1,)), ((), ())),
                                   preferred_element_type=F32)
            xs = u[:, g * LANES:(g + 1) * LANES]
            dt2 = _pair_lanes(col, COL_DT + 2 * g, (q, LANES))
            ac2 = _pair_lanes(col, COL_ACUM + 2 * g, (q, LANES))
            xdt = xs * dt2
            xdt_b = xdt.astype(BF16)
            st = states[g]
            y_off = jnp.dot(cg, st.astype(BF16), preferred_element_type=F32) * jnp.exp(ac2)
            halves = []
            for hh in range(2):
                h = 2 * g + hh
                ac_col = col[:, COL_ACUM + h:COL_ACUM + h + 1]
                ac_row = row_ref[0, COL_ACUM + h:COL_ACUM + h + 1, rows]
                decay = jnp.exp(jnp.where(tri, ac_col - ac_row, -1e30))
                mm = (gmat * decay).astype(BF16)
                halves.append(jnp.dot(mm, xdt_b, preferred_element_type=F32))
            y = jnp.where(low, halves[0], halves[1]) + y_off + d_skip[:, g * LANES:(g + 1) * LANES] * xs
            ys.append(y)
            ac_last = ac2[q - 1:q, :]
            w_end = jnp.exp(ac_last - ac2)
            xw = (xdt * w_end).astype(BF16)
            upd = jnp.dot(bg.T.astype(BF16), xw, preferred_element_type=F32)
            new_states.append(st * jnp.exp(ac_last) + upd)
        yfull = jnp.concatenate(ys, axis=1) * _silu(z)
        o_ref[rows, :] = _rms(yfull, norm_g).astype(o_ref.dtype)
        return tuple(new_states)

    init = (jnp.zeros((LANES, LANES), F32), jnp.zeros((LANES, LANES), F32))
    states = init
    for ci in range(s // q):
        states = chunk(ci, states)


def _ssd_mixer(pc, col, rows, conv_wb, par, batch, seq):
    gw = GROUP_WIDTH
    return pl.pallas_call(
        _ssd_kernel,
        grid=(batch,),
        in_specs=[pl.BlockSpec((seq, 3 * gw), lambda b: (b, 0)),
                  pl.BlockSpec((seq, LANES), lambda b: (b, 0)),
                  pl.BlockSpec((1, N_SCALAR_ROWS, seq), lambda b: (b, 0, 0)),
                  pl.BlockSpec((8, 2 * gw), lambda b: (0, 0)),
                  pl.BlockSpec((8, gw), lambda b: (0, 0))],
        out_specs=pl.BlockSpec((seq, gw), lambda b: (b, 0)),
        out_shape=jax.ShapeDtypeStruct((batch * seq, gw), BF16),
        scratch_shapes=[pltpu.VMEM((seq, 2 * gw), F32)],
        compiler_params=_cparams("parallel"),
        name="ssd_mixer",
    )(pc, col, rows, conv_wb, par)


def _attn_kernel(*refs, fox, tq):
    if fox:
        tref_ref, q_ref, k_ref, vt_ref, o_ref = refs
    else:
        q_ref, k_ref, vt_ref, o_ref = refs
        tref_ref = None
    b = pl.program_id(0)
    i = pl.program_id(1)
    key = _row_iota((tq, tq))
    qry = _lane_iota((tq, tq))
    if fox:
        allowed = key <= qry
    else:
        shift = int(math.log2(MLA_CHUNK))
        allowed = (key >> shift) <= (qry >> shift)
    qs = [q_ref[:, h * LANES:(h + 1) * LANES] for h in range(N_HEADS)]
    ones_rows = jnp.ones((16, tq), BF16)

    def step(j, masked, carry):
        rk = pl.ds(pl.multiple_of(j * tq, tq), tq)
        scores = [lax.dot_general(k_ref[rk, h * LANES:(h + 1) * LANES], qs[h], (((1,), (1,)), ((), ())),
                                  preferred_element_type=F32) for h in range(N_HEADS)]
        probs = []
        for h in range(N_HEADS):
            m, l, _ = carry[h]
            s = scores[h]
            if masked:
                s = jnp.where(allowed, s, -1e30)
            delta = (tref_ref[b, i, h] - tref_ref[b, j, h]) if fox else 0.0
            m_new = jnp.maximum(m, jnp.max(s, axis=0, keepdims=True) + delta)
            alpha = jnp.exp2(m - m_new)
            p = jnp.exp2(s - (m_new - delta))
            probs.append((m_new, alpha, p.astype(BF16)))
        new = []
        for h in range(N_HEADS):
            pair = h // 2
            m_new, alpha, p = probs[h]
            lhs = jnp.concatenate([vt_ref[pair * LANES:(pair + 1) * LANES, rk], ones_rows], axis=0)
            pv = jnp.dot(lhs, p, preferred_element_type=F32)
            new.append((m_new, alpha * carry[h][1] + pv[LANES:LANES + 1], alpha * carry[h][2] + pv[0:LANES]))
        return tuple(new)

    init = tuple((jnp.full((1, tq), -1e30, F32), jnp.zeros((1, tq), F32), jnp.zeros((LANES, tq), F32))
                 for _ in range(N_HEADS))
    carry = lax.fori_loop(0, i, lambda j, c: step(j, False, c), init)
    carry = step(i, True, carry)
    outs = [acc / l for (_, l, acc) in carry]
    top = _row_iota((LANES, tq)) < HEAD_DIM
    o_t = jnp.concatenate([jnp.where(top, outs[0], outs[1]), jnp.where(top, outs[2], outs[3])], axis=0)
    o_ref[...] = o_t.T.astype(o_ref.dtype)


def _attention(q, k, vt, tref, batch, seq, tq, name):
    nq = seq // tq
    fox = tref is not None
    kern = functools.partial(_attn_kernel, fox=fox, tq=tq)
    grid_spec = pltpu.PrefetchScalarGridSpec(
        num_scalar_prefetch=1 if fox else 0,
        grid=(batch, nq),
        in_specs=[pl.BlockSpec((tq, HEAD_PAD), lambda b, i, *_: (b * nq + i, 0)),
                  pl.BlockSpec((seq, HEAD_PAD), lambda b, i, *_: (b, 0)),
                  pl.BlockSpec((GROUP_WIDTH, seq), lambda b, i, *_: (0, b))],
        out_specs=pl.BlockSpec((tq, GROUP_WIDTH), lambda b, i, *_: (b * nq + i, 0)),
    )
    args = ((tref,) if fox else ()) + (q, k, vt)
    return pl.pallas_call(
        kern,
        grid_spec=grid_spec,
        out_shape=jax.ShapeDtypeStruct((batch * seq, GROUP_WIDTH), BF16),
        compiler_params=_cparams("parallel", "arbitrary"),
        name=name,
    )(*args)


def _outproj_kernel(x_ref, ya, yb, yc, yd, w_ref, g_ref, wr_ref, br_ref,
                    x2_ref, h2_ref, rrow_ref, cnt_ref, *, tm, moe_tile):
    y = jnp.concatenate([ya[...], yb[...], yc[...], yd[...]], axis=1)
    x2 = x_ref[...] + jnp.dot(y, w_ref[...], preferred_element_type=F32)
    x2_ref[...] = x2
    h2 = _rms(x2, g_ref[...])
    h2_ref[...] = h2.astype(h2_ref.dtype)
    h_hi = h2.astype(BF16)
    h_lo = (h2 - h_hi.astype(F32)).astype(BF16)
    part = jnp.dot(h_hi, wr_ref[...], preferred_element_type=F32)
    logits = (part[:, 0:LANES] + part[:, LANES:2 * LANES]
              + jnp.dot(h_lo, wr_ref[:, 0:LANES], preferred_element_type=F32) + br_ref[...])
    lt = logits.T
    row = _row_iota(lt.shape)
    neg = -1e30
    big = 1 << 20
    gmask = (row >= N_EXPERTS) & (row < N_EXPERTS + N_EXPERT_GROUPS)
    gl = jnp.where(gmask, lt, neg)
    gmax = jnp.max(gl, axis=0, keepdims=True)
    gsum = jnp.sum(jnp.where(gmask, jnp.exp(gl - gmax), 0.0), axis=0, keepdims=True)
    g_w = 1.0 / gsum
    g_idx = jnp.min(jnp.where(gmask & (gl == gmax), row, big), axis=0, keepdims=True) - N_EXPERTS
    emask = (row < N_EXPERTS) & ((row >> int(math.log2(EXPERTS_PER_GROUP))) == g_idx)
    el = jnp.where(emask, lt, neg)
    e1v = jnp.max(el, axis=0, keepdims=True)
    esum = jnp.sum(jnp.where(emask, jnp.exp(el - e1v), 0.0), axis=0, keepdims=True)
    i1 = jnp.min(jnp.where(emask & (el == e1v), row, big), axis=0, keepdims=True)
    el2 = jnp.where(row == i1, neg, el)
    e2v = jnp.max(el2, axis=0, keepdims=True)
    i2 = jnp.min(jnp.where(emask & (row != i1) & (el2 == e2v), row, big), axis=0, keepdims=True)
    p1 = 1.0 / esum
    p2 = jnp.exp(e2v - e1v) / esum
    w1 = g_w * (p1 / (p1 + p2))
    w2 = g_w * (p2 / (p1 + p2))
    out_row = _row_iota(rrow_ref.shape)
    rrow_ref[...] = jnp.where(out_row == 0, i1.astype(F32),
                              jnp.where(out_row == 1, i2.astype(F32),
                                        jnp.where(out_row == 2, w1, jnp.where(out_row == 3, w2, 0.0))))
    step = pl.program_id(0)

    @pl.when(step == 0)
    def _():
        cnt_ref[...] = jnp.zeros_like(cnt_ref)

    chosen = jnp.where((row == i1) | (row == i2), 1.0, 0.0).astype(BF16)
    tiles = tm // moe_tile
    tile_of = (_row_iota((tm, LANES)) >> int(math.log2(moe_tile))) + step * tiles
    to_tile = jnp.where(_lane_iota((tm, LANES)) == tile_of, 1.0, 0.0).astype(BF16)
    counts = jnp.dot(chosen, to_tile, preferred_element_type=F32)
    cnt_ref[...] += counts[0:N_EXPERTS]


def _outproj(x, ya, yb, yc, yd, w, g, wr, br, tm, moe_tile):
    t = x.shape[0]
    full = lambda a: pl.BlockSpec(a.shape, lambda i: (0, 0))
    tile = lambda wd: pl.BlockSpec((tm, wd), lambda i: (i, 0))
    return pl.pallas_call(
        functools.partial(_outproj_kernel, tm=tm, moe_tile=moe_tile),
        grid=(t // tm,),
        in_specs=[tile(D_MODEL)] + [tile(GROUP_WIDTH)] * 4 + [full(w), full(g), full(wr), full(br)],
        out_specs=[tile(D_MODEL), tile(D_MODEL), pl.BlockSpec((8, tm), lambda i: (0, i)),
                   pl.BlockSpec((N_EXPERTS, LANES), lambda i: (0, 0))],
        out_shape=[jax.ShapeDtypeStruct((t, D_MODEL), F32), jax.ShapeDtypeStruct((t, D_MODEL), BF16),
                   jax.ShapeDtypeStruct((8, t), F32), jax.ShapeDtypeStruct((N_EXPERTS, LANES), F32)],
        compiler_params=_cparams("arbitrary"),
        name="outproj_router",
    )(x, ya, yb, yc, yd, w, g, wr, br)


def _positions_kernel(r_ref, dest_ref, meta_ref, cnt_ref, carry_ref, start_ref, *, tm, nbp):
    phase = pl.program_id(0)
    i = pl.program_id(1)
    e_iota = _row_iota((N_EXPERTS, tm))
    e0 = r_ref[0:1, :].astype(I32)
    e1 = r_ref[1:2, :].astype(I32)
    oh0 = e_iota == e0
    oh1 = e_iota == e1
    oh = jnp.where(oh0 | oh1, 1.0, 0.0)

    @pl.when((phase == 0) & (i == 0))
    def _():
        cnt_ref[...] = jnp.zeros_like(cnt_ref)

    @pl.when(phase == 0)
    def _():
        cnt_ref[...] += jnp.sum(oh, axis=-1, keepdims=True)

    @pl.when((phase == 1) & (i == 0))
    def _():
        cnt = cnt_ref[...]
        padded = jnp.floor((cnt + (MOE_ROWS - 1)) * (1.0 / MOE_ROWS)) * MOE_ROWS
        tril = jnp.where(_row_iota((N_EXPERTS, N_EXPERTS)) >= _lane_iota((N_EXPERTS, N_EXPERTS)), 1.0, 0.0)
        pend = jnp.dot(tril, padded, preferred_element_type=F32, precision=lax.Precision.HIGHEST)
        pstart = pend - padded
        start_ref[...] = pstart
        carry_ref[...] = jnp.zeros_like(carry_ref)
        pend_b = jnp.concatenate([pend] * (nbp // LANES), axis=1)
        vend_b = jnp.concatenate([pstart + cnt] * (nbp // LANES), axis=1)
        b0 = (_lane_iota((N_EXPERTS, nbp)) * MOE_ROWS).astype(F32)
        bexp = jnp.sum(jnp.where(pend_b <= b0, 1.0, 0.0), axis=0, keepdims=True)
        bexp = jnp.minimum(bexp, N_EXPERTS - 1.0)
        is_e = _row_iota((N_EXPERTS, nbp)).astype(F32) == bexp
        vend = jnp.sum(jnp.where(is_e, vend_b, 0.0), axis=0, keepdims=True)
        nvalid = jnp.clip(vend - b0[0:1], 0.0, float(MOE_ROWS))
        total = jnp.max(pend_b, axis=0, keepdims=True) * (1.0 / MOE_ROWS)
        row = _row_iota((8, nbp))
        meta = jnp.where(row == 0, bexp, jnp.where(row == 1, nvalid, jnp.where(row == 2, total, 0.0)))
        meta_ref[...] = meta.astype(I32)

    @pl.when(phase == 1)
    def _():
        su = jnp.where(_row_iota((tm, tm)) < _lane_iota((tm, tm)), 1.0, 0.0).astype(BF16)
        before = jnp.dot(oh.astype(BF16), su, preferred_element_type=F32)
        base = start_ref[:, 0:1] + carry_ref[:, 0:1] + before
        d0 = jnp.sum(jnp.where(oh0, base, 0.0), axis=0, keepdims=True)
        d1 = jnp.sum(jnp.where(oh1, base, 0.0), axis=0, keepdims=True)
        dest_ref[0, 0:1, :] = d0.astype(I32)
        dest_ref[0, 1:2, :] = d1.astype(I32)
        carry_ref[...] += jnp.sum(oh, axis=-1, keepdims=True)


def _positions(rrow, tm, nbp):
    t = rrow.shape[1]
    nt = t // tm
    kern = functools.partial(_positions_kernel, tm=tm, nbp=nbp)
    return pl.pallas_call(
        kern,
        grid=(2, nt),
        in_specs=[pl.BlockSpec((8, tm), lambda p, i: (0, i))],
        out_specs=[pl.BlockSpec((1, 2, tm), lambda p, i: (i * p, 0, 0)),
                   pl.BlockSpec((8, nbp), lambda p, i: (0, 0))],
        out_shape=[jax.ShapeDtypeStruct((nt, 2, tm), I32), jax.ShapeDtypeStruct((8, nbp), I32)],
        scratch_shapes=[pltpu.VMEM((N_EXPERTS, LANES), F32)] * 3,
        compiler_params=_cparams("arbitrary", "arbitrary"),
        name="moe_positions",
    )(rrow)


def _dispatch_kernel(meta_ref, dest_ref, h_ref, xs_ref, zero_ref, sem, zsem, *, tm, nb):
    i = pl.program_id(0)

    @pl.when(i == 0)
    def _():
        zero_ref[...] = jnp.zeros_like(zero_ref)
        n_used = meta_ref[2, 0]

        def zcopy(b):
            return pltpu.make_async_copy(zero_ref, xs_ref.at[pl.ds(b * MOE_ROWS, MOE_ROWS)], zsem)

        def needs(b):
            return (b < n_used) & (meta_ref[1, b] < MOE_ROWS)

        def start(b, c):
            @pl.when(needs(b))
            def _():
                zcopy(b).start()
            return c

        def wait(b, c):
            @pl.when(needs(b))
            def _():
                zcopy(b).wait()
            return c

        lax.fori_loop(0, nb, start, 0)
        lax.fori_loop(0, nb, wait, 0)

    def copy(t, k):
        return pltpu.make_async_copy(h_ref.at[pl.ds(t, 1)], xs_ref.at[pl.ds(dest_ref[0, k, t], 1)], sem)

    def start(t, c):
        copy(t, 0).start()
        copy(t, 1).start()
        return c

    def wait(t, c):
        copy(t, 0).wait()
        copy(t, 1).wait()
        return c

    lax.fori_loop(0, tm, start, 0)
    lax.fori_loop(0, tm, wait, 0)


def _dispatch(meta, dest, h2, tm, nb):
    t = h2.shape[0]
    kern = functools.partial(_dispatch_kernel, tm=tm, nb=nb)
    grid_spec = pltpu.PrefetchScalarGridSpec(
        num_scalar_prefetch=1,
        grid=(t // tm,),
        in_specs=[pl.BlockSpec((1, 2, tm), lambda i, m: (i, 0, 0), memory_space=pltpu.SMEM),
                  pl.BlockSpec((tm, D_MODEL), lambda i, m: (i, 0))],
        out_specs=pl.BlockSpec(memory_space=pl.ANY),
        scratch_shapes=[pltpu.VMEM((MOE_ROWS, D_MODEL), F32),
                        pltpu.SemaphoreType.DMA, pltpu.SemaphoreType.DMA],
    )
    return pl.pallas_call(
        kern,
        grid_spec=grid_spec,
        out_shape=jax.ShapeDtypeStruct((nb * MOE_ROWS, D_MODEL), F32),
        compiler_params=_cparams("arbitrary"),
        name="moe_dispatch",
    )(meta, dest, h2)


def _expert_kernel(meta_ref, x_ref, wg_ref, wu_ref, wd_ref, o_ref):
    b = pl.program_id(0)

    @pl.when(b < meta_ref[2, 0])
    def _():
        x = x_ref[...].astype(BF16)
        gate = jnp.dot(x, wg_ref[0, 0].astype(BF16), preferred_element_type=F32)
        up = jnp.dot(x, wu_ref[0, 0].astype(BF16), preferred_element_type=F32)
        act = (_silu(gate) * up).astype(BF16)
        o_ref[...] = jnp.dot(act, wd_ref[0, 0].astype(BF16), preferred_element_type=F32)


def _experts(meta, xs, wg, wu, wd, layer, nb):
    def blk(b, m):
        return (jnp.minimum(b, m[2, 0] - 1), 0)

    def wblk(b, m):
        return (layer, m[0, jnp.minimum(b, m[2, 0] - 1)], 0, 0)

    grid_spec = pltpu.PrefetchScalarGridSpec(
        num_scalar_prefetch=1,
        grid=(nb,),
        in_specs=[pl.BlockSpec((MOE_ROWS, D_MODEL), blk),
                  pl.BlockSpec((1, 1, D_MODEL, EXPERT_FF), wblk),
                  pl.BlockSpec((1, 1, D_MODEL, EXPERT_FF), wblk),
                  pl.BlockSpec((1, 1, EXPERT_FF, D_MODEL), wblk)],
        out_specs=pl.BlockSpec((MOE_ROWS, D_MODEL), blk),
    )
    return pl.pallas_call(
        _expert_kernel,
        grid_spec=grid_spec,
        out_shape=jax.ShapeDtypeStruct((nb * MOE_ROWS, D_MODEL), F32),
        compiler_params=_cparams("arbitrary"),
        name="moe_experts",
    )(meta, xs, wg, wu, wd)


def _combine_kernel(dest_ref, x_ref, r_ref, g_ref, ys_ref, o_ref, buf_ref, sem, *, tm, final):
    def copy(t, k):
        return pltpu.make_async_copy(ys_ref.at[pl.ds(dest_ref[0, k, t], 1)],
                                     buf_ref.at[k, pl.ds(t, 1)], sem)

    def start(t, c):
        copy(t, 0).start()
        copy(t, 1).start()
        return c

    def wait(t, c):
        copy(t, 0).wait()
        copy(t, 1).wait()
        return c

    lax.fori_loop(0, tm, start, 0)
    lax.fori_loop(0, tm, wait, 0)
    w0 = r_ref[:, 2:3]
    w1 = r_ref[:, 3:4]
    x = x_ref[...] + (buf_ref[0] * w0 + buf_ref[1] * w1)
    o_ref[...] = _rms(x, g_ref[...]) if final else x


def _combine(dest, x2, rcol, g, ys, tm, final):
    t = x2.shape[0]
    kern = functools.partial(_combine_kernel, tm=tm, final=final)
    return pl.pallas_call(
        kern,
        grid=(t // tm,),
        in_specs=[pl.BlockSpec((1, 2, tm), lambda i: (i, 0, 0), memory_space=pltpu.SMEM),
                  pl.BlockSpec((tm, D_MODEL), lambda i: (i, 0)),
                  pl.BlockSpec((tm, LANES), lambda i: (i, 0)),
                  pl.BlockSpec((1, D_MODEL), lambda i: (0, 0)),
                  pl.BlockSpec(memory_space=pl.ANY)],
        out_specs=pl.BlockSpec((tm, D_MODEL), lambda i: (i, 0)),
        out_shape=jax.ShapeDtypeStruct((t, D_MODEL), F32),
        scratch_shapes=[pltpu.VMEM((2, tm, D_MODEL), F32), pltpu.SemaphoreType.DMA],
        compiler_params=_cparams("arbitrary"),
        name="moe_combine",
    )(dest, x2, rcol, g, ys)


MOE_TILE = 256
CHUNK = 8
LOCAL_ROWS = 2 * MOE_TILE + 256
PACKED = D_MODEL // 2
XS_WIDTH = PACKED
U32 = jnp.uint32


def _pack_bf16_pairs(x, exact=False):
    if not exact:
        x = x.astype(BF16).astype(F32)
    half = x.shape[1] // 2
    lo = lax.bitcast_convert_type(x[:, :half], U32)
    hi = lax.bitcast_convert_type(x[:, half:], U32)
    return hi | (lo >> 16)


def _unpack_bf16_pairs(words):
    lo = lax.bitcast_convert_type(words << 16, F32)
    hi = lax.bitcast_convert_type(words & U32(0xFFFF0000), F32)
    return jnp.concatenate([lo, hi], axis=1).astype(BF16)
TAB_CHUNKS, TAB_LOCAL, TAB_GLOBAL, TAB_TOTAL, TAB_EXPERT = 0, 1, 2, 3, 4
N_TABS = 5
COPY_ROWS = (2 * CHUNK, CHUNK)


def _route_kernel(r_ref, cnt_ref, lrow_ref, lcol_ref, tab_ref, ctab_ref, meta_ref,
                  loff_ref, goff_ref, n8_ref, *, tm, nbp):
    i = pl.program_id(0)
    e_iota = _row_iota((N_EXPERTS, tm))
    oh0 = e_iota == r_ref[0:1, :].astype(I32)
    oh1 = e_iota == r_ref[1:2, :].astype(I32)
    oh = jnp.where(oh0 | oh1, 1.0, 0.0)
    tile_lane = _lane_iota((N_EXPERTS, LANES)) == i
    hi = lax.Precision.HIGHEST

    @pl.when(i == 0)
    def _():
        cnt = cnt_ref[...]
        n8 = jnp.floor((cnt + (CHUNK - 1)) * (1.0 / CHUNK)) * CHUNK
        er = _row_iota((N_EXPERTS, N_EXPERTS))
        ec = _lane_iota((N_EXPERTS, N_EXPERTS))
        below = jnp.where(er > ec, 1.0, 0.0)
        loff = jnp.dot(below, n8, preferred_element_type=F32, precision=hi)
        rows_e = jnp.sum(n8, axis=-1, keepdims=True) + jnp.zeros_like(n8)
        padded = jnp.floor((rows_e + (MOE_ROWS - 1)) * (1.0 / MOE_ROWS)) * MOE_ROWS
        e_start = jnp.dot(below, padded, preferred_element_type=F32, precision=hi)
        tr = _row_iota((LANES, LANES))
        tc = _lane_iota((LANES, LANES))
        earlier = jnp.where(tr < tc, 1.0, 0.0)
        goff = e_start + jnp.dot(n8, earlier, preferred_element_type=F32, precision=hi)
        loff_ref[...] = loff
        goff_ref[...] = goff
        n8_ref[...] = n8
        tab_ref[TAB_CHUNKS] = (n8 * (1.0 / CHUNK)).astype(I32)
        tab_ref[TAB_LOCAL] = loff.astype(I32)
        tab_ref[TAB_GLOBAL] = goff.astype(I32)
        big = jnp.floor(n8 * (0.5 / CHUNK))
        small = n8 * (1.0 / CHUNK) - 2.0 * big
        row_t = _row_iota(n8.shape)
        tab_ref[TAB_TOTAL] = jnp.where(row_t == 0, jnp.sum(big, axis=0, keepdims=True),
                                       jnp.where(row_t == 1, jnp.sum(small, axis=0, keepdims=True),
                                                 0.0)).astype(I32)
        lane_t = _lane_iota(n8.shape)
        tab_ref[TAB_EXPERT] = jnp.where(lane_t == 0, e_start, jnp.where(lane_t == 1, rows_e, 0.0)).astype(I32)
        reps = nbp // LANES
        pend_b = jnp.concatenate([e_start + padded] * reps, axis=1)
        vend_b = jnp.concatenate([e_start + rows_e] * reps, axis=1)
        used_b = jnp.concatenate([padded] * reps, axis=1) > 0.0
        b0 = (_lane_iota((N_EXPERTS, nbp)) * MOE_ROWS).astype(F32)
        bexp = jnp.sum(jnp.where(pend_b <= b0, 1.0, 0.0), axis=0, keepdims=True)
        bexp = jnp.minimum(bexp, N_EXPERTS - 1.0)
        e_b = _row_iota((N_EXPERTS, nbp)).astype(F32)
        is_e = e_b == bexp
        vend = jnp.sum(jnp.where(is_e, vend_b, 0.0), axis=0, keepdims=True)
        nvalid = jnp.clip(vend - b0[0:1], 0.0, float(MOE_ROWS))
        total = jnp.max(pend_b, axis=0, keepdims=True) * (1.0 / MOE_ROWS)
        order = jnp.sum(jnp.where(used_b & (e_b < bexp), 1.0, 0.0), axis=0, keepdims=True)
        nxt = jnp.min(jnp.where(used_b & (e_b > bexp), e_b, float(N_EXPERTS)), axis=0, keepdims=True)
        row = _row_iota((8, nbp))
        meta = jnp.where(row == 0, bexp, jnp.where(row == 1, nvalid, jnp.where(row == 2, total,
                         jnp.where(row == 3, order, jnp.where(row == 4, nxt, 0.0)))))
        meta_ref[...] = meta.astype(I32)

    su = jnp.where(_row_iota((tm, tm)) < _lane_iota((tm, tm)), 1.0, 0.0).astype(BF16)
    before = jnp.dot(oh.astype(BF16), su, preferred_element_type=F32)
    base = jnp.sum(jnp.where(tile_lane, loff_ref[...], 0.0), axis=-1, keepdims=True) + before
    d0 = jnp.sum(jnp.where(oh0, base, 0.0), axis=0, keepdims=True)
    d1 = jnp.sum(jnp.where(oh1, base, 0.0), axis=0, keepdims=True)
    lrow_ref[0, 0:1, :] = d0.astype(I32)
    lrow_ref[0, 1:2, :] = d1.astype(I32)
    pick_tile = lambda ref: jnp.sum(jnp.where(tile_lane, ref[...], 0.0), axis=-1, keepdims=True)
    nch = pick_tile(n8_ref) * (1.0 / CHUNK)
    n_big = jnp.floor(nch * 0.5)
    n_small = nch - 2.0 * n_big
    incl = jnp.where(_row_iota((N_EXPERTS, N_EXPERTS)) >= _lane_iota((N_EXPERTS, N_EXPERTS)), 1.0, 0.0)
    cidx = _lane_iota((N_EXPERTS, LANES)).astype(F32)
    loff_t = pick_tile(loff_ref)
    goff_t = pick_tile(goff_ref)
    for k, (n, rows, first) in enumerate(((n_big, COPY_ROWS[0], 0.0), (n_small, COPY_ROWS[1], n_big * COPY_ROWS[0]))):
        cend = jnp.dot(incl, n + jnp.zeros((N_EXPERTS, LANES), F32), preferred_element_type=F32, precision=hi)
        cstart = cend - n
        mine = (cidx >= cstart) & (cidx < cend)
        step_rows = first + (cidx - cstart) * rows
        ctab_ref[0, 2 * k:2 * k + 1, :] = jnp.sum(jnp.where(mine, loff_t + step_rows, 0.0), axis=0,
                                                  keepdims=True).astype(I32)
        ctab_ref[0, 2 * k + 1:2 * k + 2, :] = jnp.sum(jnp.where(mine, goff_t + step_rows, 0.0), axis=0,
                                                      keepdims=True).astype(I32)
    row = _row_iota((LANES, tm))
    lcol_ref[...] = jnp.where(row == 0, d0, jnp.where(row == 1, d1,
                              jnp.where(row == 2, r_ref[2:3, :], jnp.where(row == 3, r_ref[3:4, :], 0.0)))).T


def _route(rrow, cnt, tm, nbp):
    t = rrow.shape[1]
    nt = t // tm
    kern = functools.partial(_route_kernel, tm=tm, nbp=nbp)
    return pl.pallas_call(
        kern,
        grid=(nt,),
        in_specs=[pl.BlockSpec((8, tm), lambda i: (0, i)),
                  pl.BlockSpec((N_EXPERTS, LANES), lambda i: (0, 0))],
        out_specs=[pl.BlockSpec((1, 2, tm), lambda i: (i, 0, 0)),
                   pl.BlockSpec((tm, LANES), lambda i: (i, 0)),
                   pl.BlockSpec((N_TABS, N_EXPERTS, LANES), lambda i: (0, 0, 0)),
                   pl.BlockSpec((1, 2 * len(COPY_ROWS), LANES), lambda i: (i, 0, 0)),
                   pl.BlockSpec((8, nbp), lambda i: (0, 0))],
        out_shape=[jax.ShapeDtypeStruct((nt, 2, tm), I32), jax.ShapeDtypeStruct((t, LANES), F32),
                   jax.ShapeDtypeStruct((N_TABS, N_EXPERTS, LANES), I32), jax.ShapeDtypeStruct((nt, 2 * len(COPY_ROWS), LANES), I32),
                   jax.ShapeDtypeStruct((8, nbp), I32)],
        scratch_shapes=[pltpu.VMEM((N_EXPERTS, LANES), F32)] * 3,
        compiler_params=_cparams("arbitrary"),
        name="moe_route",
    )(rrow, cnt)


def _chunk_copies(tabs, i, local_ref, global_ref, sem, to_global, action):
    tab_ref, ctab_ref = tabs
    for k, rows in enumerate(COPY_ROWS):
        count = tab_ref[TAB_TOTAL, k, i]

        def copy(lo, go, rows=rows):
            lsl = local_ref.at[pl.ds(pl.multiple_of(lo, CHUNK), rows)]
            gsl = global_ref.at[pl.ds(pl.multiple_of(go, CHUNK), rows)]
            return pltpu.make_async_copy(lsl, gsl, sem) if to_global else pltpu.make_async_copy(gsl, lsl, sem)

        if action == "wait":
            def one(c, c1, copy=copy):
                copy(0, 0).wait()
                return c1
        else:
            def one(c, c1, copy=copy, k=k):
                copy(ctab_ref[i, 2 * k, c], ctab_ref[i, 2 * k + 1, c]).start()
                return c1

        lax.fori_loop(0, count, one, 0)


def _scatter_kernel(tab_ref, ctab_ref, meta_ref, lrow_ref, h_ref, xs_ref, buf_ref, zero_ref, sem, zsem, *, tm, nb):
    i = pl.program_id(0)
    tabs = (tab_ref, ctab_ref)

    @pl.when(i == 0)
    def _():
        zero_ref[...] = jnp.zeros_like(zero_ref)
        n_used = meta_ref[2, 0]

        def zcopy(b):
            sub = lax.shift_right_logical(meta_ref[1, b], int(math.log2(FFN_SUB)))
            start = pl.multiple_of(b * MOE_ROWS + sub * FFN_SUB, FFN_SUB)
            return pltpu.make_async_copy(zero_ref, xs_ref.at[pl.ds(start, FFN_SUB)], zsem)

        def needs(b):
            return (b < n_used) & ((meta_ref[1, b] & (FFN_SUB - 1)) != 0)

        def start(b, c):
            @pl.when(needs(b))
            def _():
                zcopy(b).start()
            return c

        def wait(b, c):
            @pl.when(needs(b))
            def _():
                zcopy(b).wait()
            return c

        lax.fori_loop(0, nb, start, 0)
        lax.fori_loop(0, nb, wait, 0)

    rows = _row_iota((LOCAL_ROWS, tm))
    p0 = rows == lrow_ref[0, 0:1, :]
    p1 = rows == lrow_ref[0, 1:2, :]
    perm = jnp.where(p0 | p1, 1.0, 0.0).astype(BF16)
    sorted_rows = jnp.dot(perm, h_ref[...], preferred_element_type=F32)

    def fill(slot):
        buf = buf_ref.at[slot]
        buf[...] = _pack_bf16_pairs(sorted_rows, exact=True)
        _chunk_copies(tabs, i, buf, xs_ref, sem.at[slot], True, "start")

    def drain(tile, slot):
        _chunk_copies(tabs, tile, buf_ref.at[slot], xs_ref, sem.at[slot], True, "wait")

    even = (i & 1) == 0

    @pl.when(even)
    def _():
        fill(0)

    @pl.when(jnp.logical_not(even))
    def _():
        fill(1)

    @pl.when((i > 0) & even)
    def _():
        drain(i - 1, 1)

    @pl.when((i > 0) & jnp.logical_not(even))
    def _():
        drain(i - 1, 0)

    @pl.when((i == pl.num_programs(0) - 1) & even)
    def _():
        drain(i, 0)

    @pl.when((i == pl.num_programs(0) - 1) & jnp.logical_not(even))
    def _():
        drain(i, 1)


def _scatter(tab, ctab, meta, lrow, h2, tm, nb):
    t = h2.shape[0]
    kern = functools.partial(_scatter_kernel, tm=tm, nb=nb)
    grid_spec = pltpu.PrefetchScalarGridSpec(
        num_scalar_prefetch=3,
        grid=(t // tm,),
        in_specs=[pl.BlockSpec((1, 2, tm), lambda i, *_: (i, 0, 0)),
                  pl.BlockSpec((tm, D_MODEL), lambda i, *_: (i, 0))],
        out_specs=pl.BlockSpec(memory_space=pl.ANY),
        scratch_shapes=[pltpu.VMEM((2, LOCAL_ROWS, XS_WIDTH), U32), pltpu.VMEM((FFN_SUB, XS_WIDTH), U32),
                        pltpu.SemaphoreType.DMA((2,)), pltpu.SemaphoreType.DMA],
    )
    return pl.pallas_call(
        kern,
        grid_spec=grid_spec,
        out_shape=jax.ShapeDtypeStruct((nb * MOE_ROWS, XS_WIDTH), U32),
        compiler_params=_cparams("arbitrary"),
        name="moe_scatter",
    )(tab, ctab, meta, lrow, h2)


def _ffn_kernel(meta_ref, x_ref, wg_ref, wu_ref, wd_ref, o_ref, wgu_b, wd_b, wg_f, wu_f, wd_f, wsem, *, layer):
    b = pl.program_id(0)
    live = b < meta_ref[2, 0]
    expert = meta_ref[0, b]
    prev = meta_ref[0, jnp.maximum(b - 1, 0)]

    def fetch(e, slot):
        return [pltpu.make_async_copy(src.at[layer, e], dst.at[slot], wsem.at[slot])
                for src, dst in ((wg_ref, wg_f), (wu_ref, wu_f), (wd_ref, wd_f))]

    def first_block(slot):
        @pl.when(b == 0)
        def _():
            for cp in fetch(expert, slot):
                cp.start()

        for cp in fetch(expert, slot):
            cp.wait()
        wgu_b[:, 0:EXPERT_FF] = wg_f[slot].astype(BF16)
        wgu_b[:, EXPERT_FF:2 * EXPERT_FF] = wu_f[slot].astype(BF16)
        wd_b[...] = wd_f[slot].astype(BF16)
        nxt = meta_ref[4, b]

        @pl.when(nxt < N_EXPERTS)
        def _():
            for cp in fetch(nxt, 1 - slot):
                cp.start()

    changed = live & ((b == 0) | (expert != prev))
    odd = (meta_ref[3, b] & 1) == 1

    @pl.when(changed & jnp.logical_not(odd))
    def _():
        first_block(0)

    @pl.when(changed & odd)
    def _():
        first_block(1)

    nvalid = meta_ref[1, jnp.maximum(jnp.minimum(b, meta_ref[2, 0] - 1), 0)]
    for sub in range(MOE_ROWS // FFN_SUB):
        rows = slice(sub * FFN_SUB, (sub + 1) * FFN_SUB)
        used = live & (nvalid > sub * FFN_SUB)

        @pl.when(used)
        def _():
            x = _unpack_bf16_pairs(x_ref[rows, 0:PACKED])
            gu = jnp.dot(x, wgu_b[...], preferred_element_type=F32)
            act = (_silu(gu[:, 0:EXPERT_FF]) * gu[:, EXPERT_FF:2 * EXPERT_FF]).astype(BF16)
            y = jnp.dot(act, wd_b[...], preferred_element_type=F32)
            o_ref[rows, :] = _pack_bf16_pairs(y)

        @pl.when(live & jnp.logical_not(used))
        def _():
            o_ref[rows, :] = jnp.zeros((FFN_SUB, PACKED), U32)


def _ffn(meta, xs, wg, wu, wd, layer, nb):
    def blk(b, m):
        return (jnp.maximum(jnp.minimum(b, m[2, 0] - 1), 0), 0)

    grid_spec = pltpu.PrefetchScalarGridSpec(
        num_scalar_prefetch=1,
        grid=(nb,),
        in_specs=[pl.BlockSpec((MOE_ROWS, XS_WIDTH), blk)] + [pl.BlockSpec(memory_space=pl.ANY)] * 3,
        out_specs=pl.BlockSpec((MOE_ROWS, PACKED), blk),
        scratch_shapes=[pltpu.VMEM((D_MODEL, 2 * EXPERT_FF), BF16), pltpu.VMEM((EXPERT_FF, D_MODEL), BF16),
                        pltpu.VMEM((2, D_MODEL, EXPERT_FF), F32), pltpu.VMEM((2, D_MODEL, EXPERT_FF), F32),
                        pltpu.VMEM((2, EXPERT_FF, D_MODEL), F32), pltpu.SemaphoreType.DMA((2,))],
    )
    return pl.pallas_call(
        functools.partial(_ffn_kernel, layer=layer),
        grid_spec=grid_spec,
        out_shape=jax.ShapeDtypeStruct((nb * MOE_ROWS, PACKED), U32),
        compiler_params=_cparams("arbitrary"),
        name="moe_experts",
    )(meta, xs, wg, wu, wd)


def _gather_kernel(tab_ref, ctab_ref, lcol_ref, x_ref, g_ref, ys_ref, o_ref, buf_ref, sem, *, tm, final):
    i = pl.program_id(0)

    last = pl.num_programs(0) - 1

    def fetch(tile, slot, action):
        _chunk_copies((tab_ref, ctab_ref), tile, buf_ref.at[slot], ys_ref, sem.at[slot], False, action)

    @pl.when(i == 0)
    def _():
        buf_ref[...] = jnp.zeros_like(buf_ref)
        fetch(0, 0, "start")

    even = (i & 1) == 0

    @pl.when((i < last) & even)
    def _():
        fetch(i + 1, 1, "start")

    @pl.when((i < last) & jnp.logical_not(even))
    def _():
        fetch(i + 1, 0, "start")

    col = _lane_iota((tm, LOCAL_ROWS)).astype(F32)
    pick0 = jnp.where(col == lcol_ref[:, 0:1], 1.0, 0.0).astype(BF16)
    pick1 = jnp.where(col == lcol_ref[:, 1:2], 1.0, 0.0).astype(BF16)

    def finish(slot):
        fetch(i, slot, "wait")
        y = _unpack_bf16_pairs(buf_ref[slot])
        both = jnp.dot(jnp.concatenate([pick0, pick1], axis=0), y, preferred_element_type=F32)
        x = x_ref[...] + lcol_ref[:, 2:3] * both[0:tm] + lcol_ref[:, 3:4] * both[tm:2 * tm]
        o_ref[...] = _rms(x, g_ref[...]) if final else x

    @pl.when(even)
    def _():
        finish(0)

    @pl.when(jnp.logical_not(even))
    def _():
        finish(1)


def _gather(tab, ctab, lcol, x2, g, ys, tm, final):
    t = x2.shape[0]
    kern = functools.partial(_gather_kernel, tm=tm, final=final)
    grid_spec = pltpu.PrefetchScalarGridSpec(
        num_scalar_prefetch=2,
        grid=(t // tm,),
        in_specs=[pl.BlockSpec((tm, LANES), lambda i, *_: (i, 0)),
                  pl.BlockSpec((tm, D_MODEL), lambda i, *_: (i, 0)),
                  pl.BlockSpec((1, D_MODEL), lambda i, *_: (0, 0)),
                  pl.BlockSpec(memory_space=pl.ANY)],
        out_specs=pl.BlockSpec((tm, D_MODEL), lambda i, *_: (i, 0)),
        scratch_shapes=[pltpu.VMEM((2, LOCAL_ROWS, PACKED), U32), pltpu.SemaphoreType.DMA((2,))],
    )
    return pl.pallas_call(
        kern,
        grid_spec=grid_spec,
        out_shape=jax.ShapeDtypeStruct((t, D_MODEL), F32),
        compiler_params=_cparams("arbitrary"),
        name="moe_combine",
    )(tab, ctab, lcol, x2, g, ys)


def _pad_rows(a, rows=8):
    return jnp.zeros((rows, a.shape[-1]), F32).at[:a.shape[0]].set(a.astype(F32))


def _arrange_mla(w_uq, w_ukv):
    half = MLA_ROPE // 2
    qd = MLA_NOPE + MLA_ROPE
    wq, wqs, wk, wv = [], [], [], []
    zq = jnp.zeros((MLA_Q_LORA, LANES - qd), w_uq.dtype)
    zk = jnp.zeros((MLA_KV_LORA, LANES - MLA_NOPE), w_ukv.dtype)
    for h in range(N_HEADS):
        q = w_uq[:, h * qd:(h + 1) * qd]
        nope, rope = q[:, :MLA_NOPE], q[:, MLA_NOPE:]
        wq.append(jnp.concatenate([nope, rope, zq], axis=1))
        wqs.append(jnp.concatenate([jnp.zeros_like(nope), -rope[:, half:], rope[:, :half], zq], axis=1))
        kv = w_ukv[:, h * 2 * MLA_NOPE:(h + 1) * 2 * MLA_NOPE]
        wk.append(jnp.concatenate([kv[:, :MLA_NOPE], zk], axis=1))
        wv.append(kv[:, MLA_NOPE:])
    cat = lambda xs: jnp.concatenate(xs, axis=1).astype(BF16)
    return cat(wq), cat(wqs), cat(wk), cat(wv).T


def kernel(x, positions, norm_mix, w_in, conv_a, fox_forget_bias, ssm_conv_w, ssm_conv_b, ssm_dt_bias,
           ssm_a_log, ssm_d, ssm_norm, mla_q_norm, mla_kv_norm, mla_w_uq, mla_w_ukv, w_out, norm_ffn,
           router_group_w, router_group_b, router_expert_w, router_expert_b, expert_w_gate, expert_w_up,
           expert_w_down, norm_final):
    batch, seq, d = x.shape
    t = batch * seq
    depth = w_in.shape[0]
    tm = min(512, t)
    tq = min(ATTN_TQ, seq)
    tmd = min(MOE_TILE, t)
    max_rows = 2 * t + (CHUNK - 1) * N_EXPERTS * (t // tmd) + N_EXPERTS * (MOE_ROWS - 1)
    nb = -(-max_rows // MOE_ROWS)
    nbp = -(-nb // LANES) * LANES

    xf = x.reshape(t, d)
    pos_col = positions.astype(F32).reshape(t, 1)
    cos, sin = _rope_tables(pos_col, tm)
    tail0 = 9 * GROUP_WIDTH
    w_tail = jnp.pad(w_in[:, :, tail0:], ((0, 0), (0, 0), (0, tail0 + 4 * LANES - w_in.shape[-1])))

    for l in range(depth):
        wq, wqs, wk, wv = _arrange_mla(mla_w_uq[l], mla_w_ukv[l])
        ya, pb, pc, misc, fox_vt, q, k, v = _inproj(
            xf, norm_mix[l][None, :], w_in, w_tail, l, _pad_rows(conv_a[l]), cos, sin, mla_q_norm[l][None, :],
            mla_kv_norm[l][None, :], jnp.concatenate([wq, wqs], axis=1), wk, wv, tm, seq)

        sp = jnp.zeros((8, LANES), F32)
        sp = sp.at[0, MISC_F:MISC_F + N_HEADS].set(fox_forget_bias[l])
        sp = sp.at[0, MISC_DT:MISC_DT + N_HEADS].set(ssm_dt_bias[l])
        sp = sp.at[1, MISC_DT:MISC_DT + N_HEADS].set(ssm_a_log[l])
        col, rows, fox_q, fox_k, tref = _scalar_prep(misc, sp, pb, batch, seq, tq)

        yb = _attention(fox_q, fox_k, fox_vt, tref, batch, seq, tq, "fox_attention")
        conv_wb = _pad_rows(jnp.concatenate([ssm_conv_w[l], ssm_conv_b[l][None, :]], axis=0))
        ssd_par = _pad_rows(jnp.stack([jnp.repeat(ssm_d[l], HEAD_DIM), ssm_norm[l]]))
        yc = _ssd_mixer(pc, col, rows, conv_wb, ssd_par, batch, seq)
        yd = _attention(q, k, v, None, batch, seq, tq, "mla_attention")

        pad = jnp.zeros((d, LANES - N_EXPERTS - N_EXPERT_GROUPS), F32)
        wr = jnp.concatenate([router_expert_w[l], router_group_w[l], pad], axis=1)
        wr_hi = wr.astype(BF16)
        wr = jnp.concatenate([wr_hi, (wr - wr_hi.astype(F32)).astype(BF16)], axis=1)
        br = jnp.concatenate([router_expert_b[l], router_group_b[l], pad[0]])[None, :]
        x2, h2, rrow, cnt = _outproj(xf, ya, yb, yc, yd, w_out[l].astype(BF16), norm_ffn[l][None, :], wr, br,
                                     tm, tmd)

        lrow, lcol, tab, ctab, meta = _route(rrow, cnt, tmd, nbp)
        xs = _scatter(tab, ctab, meta, lrow, h2, tmd, nb)
        ys = _ffn(meta, xs, expert_w_gate, expert_w_up, expert_w_down, l, nb)
        final = l == depth - 1
        xf = _gather(tab, ctab, lcol, x2, norm_final[None, :], ys, tmd, final)

    return xf.reshape(batch, seq, d)


def _retile(dest, tm, tmd):
    if tm == tmd:
        return dest
    nt = dest.shape[0]
    return dest.reshape(nt, 2, tm // tmd, tmd).transpose(0, 2, 1, 3).reshape(nt * (tm // tmd), 2, tmd)
```

```python
import functools
import math

import jax
import jax.numpy as jnp
import numpy as np
from jax import lax
from jax.experimental import pallas as pl
from jax.experimental.pallas import tpu as pltpu

F32 = jnp.float32
BF16 = jnp.bfloat16
I32 = jnp.int32

LANES = 128
VMEM_LIMIT_BYTES = 56 * 1024 * 1024

D_MODEL = 1024
RMS_EPS = 1e-6
LOG2E = math.log2(math.e)
GROUP_WIDTH = 256
HEAD_DIM = 64
N_HEADS = 4

CONV_A_WIDTH = 3
SSM_CONV = 4
SSM_STATE = 64
SSM_CHUNK = 256

MLA_NOPE = 64
MLA_ROPE = 32
MLA_Q_LORA = 256
MLA_KV_LORA = 128
ROPE_BASE = 10000.0
MLA_CHUNK = 64
ATTN_TQ = 512

N_EXPERT_GROUPS = 4
EXPERTS_PER_GROUP = 8
N_EXPERTS = 32
EXPERT_FF = 256
ROUTER_ROWS = 48
MOE_ROWS = 512
FFN_SUB = 256

SEG_A = (0, 768)
SEG_B = (768, 1280)
SEG_C = (1280, 2048)
SEG_D = (2048, 2432)
SEG_M = (2432, 2560)
SEG_M2 = (2560, 2688)
IN_COLS_PADDED = 2688
HEAD_PAD = N_HEADS * LANES
AUG_LANE = HEAD_DIM
MISC_F = 0
MISC_DT = 4
MISC_ROPE = 64
COL_CUMF = 0
COL_DT = 4
COL_ACUM = 8
N_SCALAR_ROWS = 16


def _cparams(*sem):
    return pltpu.CompilerParams(dimension_semantics=sem, vmem_limit_bytes=VMEM_LIMIT_BYTES)


def _lane_iota(shape):
    return lax.broadcasted_iota(I32, shape, len(shape) - 1)


def _row_iota(shape):
    return lax.broadcasted_iota(I32, shape, 0)


def _rms(x, g):
    ms = jnp.mean(x * x, axis=-1, keepdims=True)
    return x * lax.rsqrt(ms + RMS_EPS) * g


def _silu(x):
    return x / (1.0 + jnp.exp(-x))


def _softplus(x):
    return jnp.maximum(x, 0.0) + jnp.log(1.0 + jnp.exp(-jnp.abs(x)))


def _shift_rows(x, k):
    rolled = pltpu.roll(x, k, 0)
    return jnp.where(_row_iota(x.shape) >= k, rolled, 0.0)


def _rope_kernel(pos_ref, freq_ref, cos_ref, sin_ref):
    ang = pos_ref[...] * freq_ref[...]
    lane = _lane_iota(ang.shape)
    rope = (lane >= MISC_ROPE) & (lane < MISC_ROPE + MLA_ROPE)
    cos_ref[...] = jnp.where(rope, jnp.cos(ang), jnp.where(lane < MISC_ROPE, 1.0, 0.0))
    sin_ref[...] = jnp.where(rope, jnp.sin(ang), 0.0)


def _rope_tables(pos_col, tm):
    t = pos_col.shape[0]
    half = MLA_ROPE // 2
    inv = ROPE_BASE ** (-np.arange(0, MLA_ROPE, 2, dtype=np.float32) / MLA_ROPE)
    freq = np.zeros((1, LANES), np.float32)
    freq[0, MISC_ROPE:MISC_ROPE + half] = inv
    freq[0, MISC_ROPE + half:MISC_ROPE + MLA_ROPE] = inv
    return pl.pallas_call(
        _rope_kernel,
        grid=(t // tm,),
        in_specs=[pl.BlockSpec((tm, 1), lambda i: (i, 0)),
                  pl.BlockSpec((1, LANES), lambda i: (0, 0))],
        out_specs=[pl.BlockSpec((tm, LANES), lambda i: (i, 0))] * 2,
        out_shape=[jax.ShapeDtypeStruct((t, LANES), F32)] * 2,
        compiler_params=_cparams("parallel"),
        name="rope_tables",
    )(pos_col, jnp.asarray(freq))


def _arrange_w_in_kernel(wt_ref, w_ref, wvt_ref):
    gw = GROUP_WIDTH
    rows = lambda lo, hi: wt_ref[0, lo:hi, :]

    def put(seg0, piece_t):
        w_ref[:, seg0:seg0 + piece_t.shape[0]] = piece_t.T.astype(BF16)

    for j in range(3):
        put(SEG_A[0] + j * gw, rows(j * gw, (j + 1) * gw))
    put(SEG_B[0], rows(3 * gw, 4 * gw) * (HEAD_DIM ** -0.5 * LOG2E))
    put(SEG_B[0] + gw, rows(4 * gw, 5 * gw))
    wvt_ref[...] = rows(5 * gw, 6 * gw).astype(BF16)
    c0 = 6 * gw
    win = pltpu.roll(rows(c0, c0 + 3 * gw + 8), 3 * gw + 8 - N_HEADS, 0)
    for j in range(3):
        put(SEG_C[0] + j * gw, win[j * gw:(j + 1) * gw])
    d0 = c0 + 3 * gw + 8
    put(SEG_D[0], rows(d0, d0 + MLA_Q_LORA))
    put(SEG_D[0] + MLA_Q_LORA, rows(d0 + MLA_Q_LORA, d0 + MLA_Q_LORA + MLA_KV_LORA))
    kr0 = d0 + MLA_Q_LORA + MLA_KV_LORA
    half = MLA_ROPE // 2
    first8 = jnp.where(_row_iota((8, D_MODEL)) < N_HEADS, rows(c0, c0 + 8), rows(d0 - 8, d0))
    zeros = lambda n: jnp.zeros((n, D_MODEL), F32)
    misc_t = jnp.concatenate([first8, zeros(MISC_ROPE - 8), rows(kr0, kr0 + MLA_ROPE),
                              zeros(LANES - MISC_ROPE - MLA_ROPE)], axis=0)
    misc2_t = jnp.concatenate([zeros(MISC_ROPE), -rows(kr0 + half, kr0 + MLA_ROPE), rows(kr0, kr0 + half),
                               zeros(LANES - MISC_ROPE - MLA_ROPE)], axis=0)
    put(SEG_M[0], misc_t)
    put(SEG_M2[0], misc2_t)


def _inproj_kernel(x_ref, g_ref, wt_ref, cw_ref, cos_ref, sin_ref, nq_ref, nkv_ref, wq2_ref, wk_ref,
                   wvt2_ref, oa, ob, oc, om, ovt, q_ref, k_ref, vt_ref, w_ref, wvt_ref, halo_ref,
                   *, tm, tiles_per_seq):
    @pl.when(pl.program_id(0) == 0)
    def _():
        _arrange_w_in_kernel(wt_ref, w_ref, wvt_ref)

    h = _rms(x_ref[...], g_ref[...]).astype(BF16)
    for o, (lo, hi) in ((ob, SEG_B), (oc, SEG_C)):
        o[...] = jnp.dot(h, w_ref[:, lo:hi], preferred_element_type=F32).astype(o.dtype)

    gw = GROUP_WIDTH
    pa = jnp.dot(h, w_ref[:, SEG_A[0]:SEG_A[1]], preferred_element_type=F32)
    cv = pa[:, gw:2 * gw] * pa[:, 2 * gw:3 * gw]

    @pl.when(pl.program_id(0) % tiles_per_seq == 0)
    def _():
        halo_ref[...] = jnp.zeros_like(halo_ref)

    halo = halo_ref[...]
    row8 = _row_iota(halo.shape)
    acc = cv * cw_ref[CONV_A_WIDTH - 1:CONV_A_WIDTH, :]
    for k in range(1, CONV_A_WIDTH):
        shifted = pltpu.roll(cv, k, 0)
        top = jnp.where(row8 < k, pltpu.roll(halo, k, 0), shifted[0:8])
        shifted = jnp.concatenate([top, shifted[8:]], axis=0)
        acc = acc + shifted * cw_ref[CONV_A_WIDTH - 1 - k:CONV_A_WIDTH - k, :]
    halo_ref[...] = cv[tm - 8:tm]
    oa[...] = (pa[:, 0:gw] * acc).astype(oa.dtype)
    ovt[...] = lax.dot_general(wvt_ref[...], h, (((1,), (1,)), ((), ())),
                               preferred_element_type=F32).astype(ovt.dtype)
    misc = jnp.dot(h, w_ref[:, SEG_M[0]:SEG_M[1]], preferred_element_type=F32)
    misc2 = jnp.dot(h, w_ref[:, SEG_M2[0]:SEG_M2[1]], preferred_element_type=F32)
    om[...] = misc

    pd = jnp.dot(h, w_ref[:, SEG_D[0]:SEG_D[1]], preferred_element_type=F32)
    cq = _rms(pd[:, 0:MLA_Q_LORA], nq_ref[...]).astype(BF16)
    ckv = _rms(pd[:, MLA_Q_LORA:MLA_Q_LORA + MLA_KV_LORA], nkv_ref[...]).astype(BF16)
    cos = cos_ref[...]
    sin = sin_ref[...]
    cos4 = jnp.concatenate([cos] * N_HEADS, axis=1)
    sin4 = jnp.concatenate([sin] * N_HEADS, axis=1)
    scale = (MLA_NOPE + MLA_ROPE) ** -0.5 * LOG2E
    q2 = jnp.dot(cq, wq2_ref[...], preferred_element_type=F32)
    q_ref[...] = ((q2[:, 0:HEAD_PAD] * cos4 + q2[:, HEAD_PAD:2 * HEAD_PAD] * sin4) * scale).astype(q_ref.dtype)
    lane = _lane_iota(cos.shape)
    rope = (lane >= MISC_ROPE) & (lane < MISC_ROPE + MLA_ROPE)
    kr = jnp.where(rope, misc * cos + misc2 * sin, 0.0)
    k = jnp.dot(ckv, wk_ref[...], preferred_element_type=F32)
    k_ref[...] = (k + jnp.concatenate([kr] * N_HEADS, axis=1)).astype(k_ref.dtype)
    vt_ref[...] = lax.dot_general(wvt2_ref[...], ckv, (((1,), (1,)), ((), ())),
                                  preferred_element_type=F32).astype(vt_ref.dtype)


def _inproj(x, g, w_in_t, layer, conv_w, cos, sin, nq, nkv, wq2, wk, wvt2, tm, seq):
    t = x.shape[0]
    once = lambda a: pl.BlockSpec((1,) + a.shape[1:], lambda i: (layer, 0, 0), pipeline_mode=pl.Buffered(1))
    full = lambda a: pl.BlockSpec(a.shape, lambda i: (0, 0))
    tile = lambda wd: pl.BlockSpec((tm, wd), lambda i: (i, 0))
    cols = lambda: pl.BlockSpec((GROUP_WIDTH, tm), lambda i: (0, i))
    seg = lambda s: s[1] - s[0]
    return pl.pallas_call(
        functools.partial(_inproj_kernel, tm=tm, tiles_per_seq=seq // tm),
        grid=(t // tm,),
        in_specs=[tile(D_MODEL), full(g), once(w_in_t), full(conv_w), tile(LANES), tile(LANES),
                  full(nq), full(nkv), full(wq2), full(wk), full(wvt2)],
        out_specs=[tile(GROUP_WIDTH), tile(seg(SEG_B)), tile(seg(SEG_C)), tile(LANES), cols(),
                   tile(HEAD_PAD), tile(HEAD_PAD), cols()],
        out_shape=[jax.ShapeDtypeStruct((t, GROUP_WIDTH), BF16), jax.ShapeDtypeStruct((t, seg(SEG_B)), BF16),
                   jax.ShapeDtypeStruct((t, seg(SEG_C)), BF16), jax.ShapeDtypeStruct((t, LANES), F32),
                   jax.ShapeDtypeStruct((GROUP_WIDTH, t), BF16),
                   jax.ShapeDtypeStruct((t, HEAD_PAD), BF16), jax.ShapeDtypeStruct((t, HEAD_PAD), BF16),
                   jax.ShapeDtypeStruct((GROUP_WIDTH, t), BF16)],
        scratch_shapes=[pltpu.VMEM((D_MODEL, IN_COLS_PADDED), BF16), pltpu.VMEM((GROUP_WIDTH, D_MODEL), BF16),
                        pltpu.VMEM((8, GROUP_WIDTH), F32)],
        compiler_params=_cparams("arbitrary"),
        name="inproj",
    )(x, g, w_in_t, conv_w, cos, sin, nq, nkv, wq2, wk, wvt2)


def _scalar_prep_kernel(m_ref, p_ref, qk_ref, place_ref, const_ref,
                        col_ref, row_ref, qa_ref, ka_ref, tref_ref, *, tq):
    s = m_ref.shape[0]
    tref_ref[...] = jnp.zeros_like(tref_ref)
    tile_ref = jnp.zeros((1, LANES), F32)
    m = m_ref[...]
    bias = p_ref[0:1, :]
    a_log = p_ref[1:2, :]
    lane = _lane_iota(m.shape)
    z = m + bias
    logf = jnp.minimum(z, 0.0) - jnp.log(1.0 + jnp.exp(-jnp.abs(z)))
    dt = _softplus(z)
    a = dt * (-jnp.exp(a_log))
    is_f = lane < MISC_DT
    is_dt = (lane >= MISC_DT) & (lane < MISC_DT + N_HEADS)
    v = jnp.where(is_f, logf, jnp.where(is_dt, a, 0.0))
    r = _row_iota((SSM_CHUNK, SSM_CHUNK))
    c = _lane_iota((SSM_CHUNK, SSM_CHUNK))
    tril = jnp.where(r >= c, 1.0, 0.0).astype(BF16)
    carry = jnp.zeros((1, LANES), F32)
    lane_1 = _lane_iota((1, LANES))
    lane_b = _lane_iota((SSM_CHUNK, LANES))
    low = lane_b < HEAD_DIM
    aug_lanes = (lane_b >= AUG_LANE) & (lane_b < AUG_LANE + AUG_TERMS)
    for ci in range(s // SSM_CHUNK):
        rest = v[ci * SSM_CHUNK:(ci + 1) * SSM_CHUNK]
        cs = jnp.zeros((SSM_CHUNK, LANES), F32)
        for _ in range(3):
            term = rest.astype(BF16)
            cs = cs + jnp.dot(tril, term, preferred_element_type=F32)
            rest = rest - term.astype(F32)
        cs = cs + jnp.where(lane_1 < MISC_DT, carry, 0.0)
        carry = cs[SSM_CHUNK - 1:SSM_CHUNK]
        acum = pltpu.roll(cs, COL_ACUM - MISC_DT, 1)
        out = jnp.where(lane_b < MISC_DT, cs * LOG2E,
                        jnp.where(lane_b < COL_ACUM, dt[ci * SSM_CHUNK:(ci + 1) * SSM_CHUNK],
                                  jnp.where(lane_b < COL_ACUM + N_HEADS, acum, 0.0)))
        rows = slice(ci * SSM_CHUNK, (ci + 1) * SSM_CHUNK)
        col_ref[rows, :] = out
        row_ref[0, :, rows] = out.T[:N_SCALAR_ROWS]
        if (ci * SSM_CHUNK) % tq == 0:
            tile_ref = out[0:1, :]
            ti = (ci * SSM_CHUNK) // tq
            tref_ref[0, ti:ti + 1, :] = tile_ref
        c = out - tile_ref
        c_hi = c.astype(BF16)
        r1 = c - c_hi.astype(F32)
        c_mid = r1.astype(BF16)
        c_lo = (r1 - c_mid.astype(F32)).astype(BF16)
        compact = jnp.dot(jnp.concatenate([c_hi, c_mid, c_lo], axis=1), place_ref[...],
                          preferred_element_type=F32) + const_ref[0:1, :]
        for side, o_ref in enumerate((qa_ref, ka_ref)):
            dec = compact[:, side * LANES:(side + 1) * LANES]
            for h in range(N_HEADS):
                pair = qk_ref[rows, side * GROUP_WIDTH + (h // 2) * LANES:
                              side * GROUP_WIDTH + (h // 2 + 1) * LANES].astype(F32)
                feat = pair if h % 2 == 0 else pltpu.roll(pair, HEAD_DIM, 1)
                dec_h = pltpu.roll(dec, AUG_LANE - AUG_TERMS * h, 1)
                group = jnp.where(low, feat, jnp.where(aug_lanes, dec_h, 0.0))
                o_ref[rows, h * LANES:(h + 1) * LANES] = group.astype(o_ref.dtype)


AUG_TERMS = 6


def _fox_placement():
    place = np.zeros((3 * LANES, 2 * LANES), np.float32)
    const = np.zeros((8, 2 * LANES), np.float32)
    for h in range(N_HEADS):
        a0 = AUG_TERMS * h
        for term in range(3):
            place[term * LANES + COL_CUMF + h, a0 + term] = 1.0
            place[term * LANES + COL_CUMF + h, LANES + a0 + 3 + term] = -1.0
            const[0, a0 + 3 + term] = 1.0
            const[0, LANES + a0 + term] = 1.0
    return jnp.asarray(place, BF16), jnp.asarray(const, F32)


def _scalar_prep(misc, params, qk, batch, seq, tq):
    place, const = _fox_placement()
    full = lambda a: pl.BlockSpec(a.shape, lambda b: (0,) * a.ndim)
    return pl.pallas_call(
        functools.partial(_scalar_prep_kernel, tq=tq),
        grid=(batch,),
        in_specs=[pl.BlockSpec((seq, LANES), lambda b: (b, 0)),
                  pl.BlockSpec((8, LANES), lambda b: (0, 0)),
                  pl.BlockSpec((seq, 2 * GROUP_WIDTH), lambda b: (b, 0)),
                  full(place), full(const)],
        out_specs=[pl.BlockSpec((seq, LANES), lambda b: (b, 0)),
                   pl.BlockSpec((1, N_SCALAR_ROWS, seq), lambda b: (b, 0, 0)),
                   pl.BlockSpec((seq, HEAD_PAD), lambda b: (b, 0)),
                   pl.BlockSpec((seq, HEAD_PAD), lambda b: (b, 0)),
                   pl.BlockSpec((1, 8, LANES), lambda b: (b, 0, 0))],
        out_shape=[jax.ShapeDtypeStruct((batch * seq, LANES), F32),
                   jax.ShapeDtypeStruct((batch, N_SCALAR_ROWS, seq), F32),
                   jax.ShapeDtypeStruct((batch * seq, HEAD_PAD), BF16),
                   jax.ShapeDtypeStruct((batch * seq, HEAD_PAD), BF16),
                   jax.ShapeDtypeStruct((batch, 8, LANES), F32)],
        compiler_params=_cparams("parallel"),
        name="scalar_prep",
    )(misc, params, qk, place, const)


def _pair_lanes(col, base, shape):
    lane = _lane_iota(shape)
    return jnp.where(lane < HEAD_DIM, col[:, base:base + 1], col[:, base + 1:base + 2])


def _ssd_kernel(p_ref, col_ref, row_ref, cw_ref, par_ref, o_ref, u_ref):
    s = p_ref.shape[0]
    q = SSM_CHUNK
    gw = GROUP_WIDTH
    xbc = p_ref[:, gw:3 * gw].astype(F32)
    acc = xbc * cw_ref[SSM_CONV - 1:SSM_CONV, :]
    for k in range(1, SSM_CONV):
        acc = acc + _shift_rows(xbc, k) * cw_ref[SSM_CONV - 1 - k:SSM_CONV - k, :]
    u_ref[...] = _silu(acc + cw_ref[SSM_CONV:SSM_CONV + 1, :])

    d_skip = par_ref[0:1, :]
    norm_g = par_ref[1:2, :]
    lane_q = _lane_iota((q, LANES))
    low = lane_q < HEAD_DIM
    tri = _row_iota((q, q)) >= _lane_iota((q, q))

    def chunk(ci, states):
        rows = pl.ds(ci * q, q)
        u = u_ref[rows, :]
        col = col_ref[rows, :]
        bm = u[:, gw:gw + LANES]
        cm = u[:, gw + LANES:gw + 2 * LANES]
        z = p_ref[rows, 0:gw].astype(F32)
        new_states = []
        ys = []
        for g in range(2):
            sel = low if g == 0 else jnp.logical_not(low)
            cg = jnp.where(sel, cm, 0.0).astype(BF16)
            bg = jnp.where(sel, bm, 0.0)
            gmat = lax.dot_general(cg, bm.astype(BF16), (((1,), (1,)), ((), ())),
                                   preferred_element_type=F32)
            xs = u[:, g * LANES:(g + 1) * LANES]
            dt2 = _pair_lanes(col, COL_DT + 2 * g, (q, LANES))
            ac2 = _pair_lanes(col, COL_ACUM + 2 * g, (q, LANES))
            xdt = xs * dt2
            xdt_b = xdt.astype(BF16)
            st = states[g]
            y_off = jnp.dot(cg, st.astype(BF16), preferred_element_type=F32) * jnp.exp(ac2)
            halves = []
            for hh in range(2):
                h = 2 * g + hh
                ac_col = col[:, COL_ACUM + h:COL_ACUM + h + 1]
                ac_row = row_ref[0, COL_ACUM + h:COL_ACUM + h + 1, rows]
                decay = jnp.exp(jnp.where(tri, ac_col - ac_row, -1e30))
                mm = (gmat * decay).astype(BF16)
                halves.append(jnp.dot(mm, xdt_b, preferred_element_type=F32))
            y = jnp.where(low, halves[0], halves[1]) + y_off + d_skip[:, g * LANES:(g + 1) * LANES] * xs
            ys.append(y)
            ac_last = ac2[q - 1:q, :]
            w_end = jnp.exp(ac_last - ac2)
            xw = (xdt * w_end).astype(BF16)
            upd = jnp.dot(bg.T.astype(BF16), xw, preferred_element_type=F32)
            new_states.append(st * jnp.exp(ac_last) + upd)
        yfull = jnp.concatenate(ys, axis=1) * _silu(z)
        o_ref[rows, :] = _rms(yfull, norm_g).astype(o_ref.dtype)
        return tuple(new_states)

    init = (jnp.zeros((LANES, LANES), F32), jnp.zeros((LANES, LANES), F32))
    states = init
    for ci in range(s // q):
        states = chunk(ci, states)


def _ssd_mixer(pc, col, rows, conv_wb, par, batch, seq):
    gw = GROUP_WIDTH
    return pl.pallas_call(
        _ssd_kernel,
        grid=(batch,),
        in_specs=[pl.BlockSpec((seq, 3 * gw), lambda b: (b, 0)),
                  pl.BlockSpec((seq, LANES), lambda b: (b, 0)),
                  pl.BlockSpec((1, N_SCALAR_ROWS, seq), lambda b: (b, 0, 0)),
                  pl.BlockSpec((8, 2 * gw), lambda b: (0, 0)),
                  pl.BlockSpec((8, gw), lambda b: (0, 0))],
        out_specs=pl.BlockSpec((seq, gw), lambda b: (b, 0)),
        out_shape=jax.ShapeDtypeStruct((batch * seq, gw), BF16),
        scratch_shapes=[pltpu.VMEM((seq, 2 * gw), F32)],
        compiler_params=_cparams("parallel"),
        name="ssd_mixer",
    )(pc, col, rows, conv_wb, par)


def _attn_kernel(*refs, fox, tq):
    if fox:
        tref_ref, q_ref, k_ref, vt_ref, o_ref = refs
    else:
        q_ref, k_ref, vt_ref, o_ref = refs
        tref_ref = None
    b = pl.program_id(0)
    i = pl.program_id(1)
    key = _row_iota((tq, tq))
    qry = _lane_iota((tq, tq))
    if fox:
        allowed = key <= qry
    else:
        shift = int(math.log2(MLA_CHUNK))
        allowed = (key >> shift) <= (qry >> shift)
    qs = [q_ref[:, h * LANES:(h + 1) * LANES] for h in range(N_HEADS)]
    ones_rows = jnp.ones((16, tq), BF16)

    def step(j, masked, carry):
        rk = pl.ds(pl.multiple_of(j * tq, tq), tq)
        scores = [lax.dot_general(k_ref[rk, h * LANES:(h + 1) * LANES], qs[h], (((1,), (1,)), ((), ())),
                                  preferred_element_type=F32) for h in range(N_HEADS)]
        probs = []
        for h in range(N_HEADS):
            m, l, _ = carry[h]
            s = scores[h]
            if masked:
                s = jnp.where(allowed, s, -1e30)
            delta = (tref_ref[b, i, h] - tref_ref[b, j, h]) if fox else 0.0
            m_new = jnp.maximum(m, jnp.max(s, axis=0, keepdims=True) + delta)
            alpha = jnp.exp2(m - m_new)
            p = jnp.exp2(s - (m_new - delta))
            probs.append((m_new, alpha, p.astype(BF16)))
        new = []
        for h in range(N_HEADS):
            pair = h // 2
            m_new, alpha, p = probs[h]
            lhs = jnp.concatenate([vt_ref[pair * LANES:(pair + 1) * LANES, rk], ones_rows], axis=0)
            pv = jnp.dot(lhs, p, preferred_element_type=F32)
            new.append((m_new, alpha * carry[h][1] + pv[LANES:LANES + 1], alpha * carry[h][2] + pv[0:LANES]))
        return tuple(new)

    init = tuple((jnp.full((1, tq), -1e30, F32), jnp.zeros((1, tq), F32), jnp.zeros((LANES, tq), F32))
                 for _ in range(N_HEADS))
    carry = lax.fori_loop(0, i, lambda j, c: step(j, False, c), init)
    carry = step(i, True, carry)
    outs = [acc / l for (_, l, acc) in carry]
    top = _row_iota((LANES, tq)) < HEAD_DIM
    o_t = jnp.concatenate([jnp.where(top, outs[0], outs[1]), jnp.where(top, outs[2], outs[3])], axis=0)
    o_ref[...] = o_t.T.astype(o_ref.dtype)


def _attention(q, k, vt, tref, batch, seq, tq, name):
    nq = seq // tq
    fox = tref is not None
    kern = functools.partial(_attn_kernel, fox=fox, tq=tq)
    grid_spec = pltpu.PrefetchScalarGridSpec(
        num_scalar_prefetch=1 if fox else 0,
        grid=(batch, nq),
        in_specs=[pl.BlockSpec((tq, HEAD_PAD), lambda b, i, *_: (b * nq + i, 0)),
                  pl.BlockSpec((seq, HEAD_PAD), lambda b, i, *_: (b, 0)),
                  pl.BlockSpec((GROUP_WIDTH, seq), lambda b, i, *_: (0, b))],
        out_specs=pl.BlockSpec((tq, GROUP_WIDTH), lambda b, i, *_: (b * nq + i, 0)),
    )
    args = ((tref,) if fox else ()) + (q, k, vt)
    return pl.pallas_call(
        kern,
        grid_spec=grid_spec,
        out_shape=jax.ShapeDtypeStruct((batch * seq, GROUP_WIDTH), BF16),
        compiler_params=_cparams("parallel", "arbitrary"),
        name=name,
    )(*args)


def _outproj_kernel(x_ref, ya, yb, yc, yd, w_ref, g_ref, wr_ref, br_ref,
                    x2_ref, h2_ref, rrow_ref, cnt_ref, *, tm, moe_tile):
    y = jnp.concatenate([ya[...], yb[...], yc[...], yd[...]], axis=1)
    x2 = x_ref[...] + jnp.dot(y, w_ref[...], preferred_element_type=F32)
    x2_ref[...] = x2
    h2 = _rms(x2, g_ref[...])
    h2_ref[...] = h2.astype(h2_ref.dtype)
    h_hi = h2.astype(BF16)
    h_lo = (h2 - h_hi.astype(F32)).astype(BF16)
    part = jnp.dot(h_hi, wr_ref[...], preferred_element_type=F32)
    logits = (part[:, 0:LANES] + part[:, LANES:2 * LANES]
              + jnp.dot(h_lo, wr_ref[:, 0:LANES], preferred_element_type=F32) + br_ref[...])
    lt = logits.T[0:ROUTER_ROWS]
    row = _row_iota(lt.shape)
    neg = -1e30
    big = 1 << 20
    gmask = (row >= N_EXPERTS) & (row < N_EXPERTS + N_EXPERT_GROUPS)
    gl = jnp.where(gmask, lt, neg)
    gmax = jnp.max(gl, axis=0, keepdims=True)
    gsum = jnp.sum(jnp.where(gmask, jnp.exp(gl - gmax), 0.0), axis=0, keepdims=True)
    g_w = 1.0 / gsum
    g_idx = jnp.min(jnp.where(gmask & (gl == gmax), row, big), axis=0, keepdims=True) - N_EXPERTS
    emask = (row < N_EXPERTS) & ((row >> int(math.log2(EXPERTS_PER_GROUP))) == g_idx)
    el = jnp.where(emask, lt, neg)
    e1v = jnp.max(el, axis=0, keepdims=True)
    esum = jnp.sum(jnp.where(emask, jnp.exp(el - e1v), 0.0), axis=0, keepdims=True)
    i1 = jnp.min(jnp.where(emask & (el == e1v), row, big), axis=0, keepdims=True)
    el2 = jnp.where(row == i1, neg, el)
    e2v = jnp.max(el2, axis=0, keepdims=True)
    i2 = jnp.min(jnp.where(emask & (row != i1) & (el2 == e2v), row, big), axis=0, keepdims=True)
    p1 = 1.0 / esum
    p2 = jnp.exp(e2v - e1v) / esum
    w1 = g_w * (p1 / (p1 + p2))
    w2 = g_w * (p2 / (p1 + p2))
    out_row = _row_iota(rrow_ref.shape)
    rrow_ref[...] = jnp.where(out_row == 0, i1.astype(F32),
                              jnp.where(out_row == 1, i2.astype(F32),
                                        jnp.where(out_row == 2, w1, jnp.where(out_row == 3, w2, 0.0))))
    step = pl.program_id(0)

    @pl.when(step == 0)
    def _():
        cnt_ref[...] = jnp.zeros_like(cnt_ref)

    chosen = jnp.where((row == i1) | (row == i2), 1.0, 0.0).astype(BF16)
    tiles = tm // moe_tile
    tile_of = (_row_iota((tm, LANES)) >> int(math.log2(moe_tile))) + step * tiles
    to_tile = jnp.where(_lane_iota((tm, LANES)) == tile_of, 1.0, 0.0).astype(BF16)
    counts = jnp.dot(chosen, to_tile, preferred_element_type=F32)
    cnt_ref[...] += counts[0:N_EXPERTS]


def _outproj(x, ya, yb, yc, yd, w, g, wr, br, tm, moe_tile):
    t = x.shape[0]
    full = lambda a: pl.BlockSpec(a.shape, lambda i: (0, 0))
    tile = lambda wd: pl.BlockSpec((tm, wd), lambda i: (i, 0))
    return pl.pallas_call(
        functools.partial(_outproj_kernel, tm=tm, moe_tile=moe_tile),
        grid=(t // tm,),
        in_specs=[tile(D_MODEL)] + [tile(GROUP_WIDTH)] * 4 + [full(w), full(g), full(wr), full(br)],
        out_specs=[tile(D_MODEL), tile(D_MODEL), pl.BlockSpec((8, tm), lambda i: (0, i)),
                   pl.BlockSpec((N_EXPERTS, LANES), lambda i: (0, 0))],
        out_shape=[jax.ShapeDtypeStruct((t, D_MODEL), F32), jax.ShapeDtypeStruct((t, D_MODEL), BF16),
                   jax.ShapeDtypeStruct((8, t), F32), jax.ShapeDtypeStruct((N_EXPERTS, LANES), F32)],
        compiler_params=_cparams("arbitrary"),
        name="outproj_router",
    )(x, ya, yb, yc, yd, w, g, wr, br)


MOE_TILE = 256
CHUNK = 8
LOCAL_ROWS = 2 * MOE_TILE + 256
PACKED = D_MODEL // 2
XS_WIDTH = PACKED
U32 = jnp.uint32


def _pack_bf16_pairs(x, exact=False):
    if not exact:
        x = x.astype(BF16).astype(F32)
    half = x.shape[1] // 2
    lo = lax.bitcast_convert_type(x[:, :half], U32)
    hi = lax.bitcast_convert_type(x[:, half:], U32)
    return hi | (lo >> 16)


def _unpack_bf16_pairs(words):
    lo = lax.bitcast_convert_type(words << 16, F32)
    hi = lax.bitcast_convert_type(words & U32(0xFFFF0000), F32)
    return jnp.concatenate([lo, hi], axis=1).astype(BF16)


COPY_ROWS = (2 * CHUNK, CHUNK)


def _route_kernel(r_ref, cnt_ref, lrow_ref, lcol_ref, tab_ref, ctab_ref, meta_ref,
                  loff_ref, goff_ref, n8_ref, *, tm, nbp, tiles):
    i = pl.program_id(0)
    hi = lax.Precision.HIGHEST

    @pl.when(i == 0)
    def _():
        cnt = cnt_ref[...]
        n8 = jnp.floor((cnt + (CHUNK - 1)) * (1.0 / CHUNK)) * CHUNK
        er = _row_iota((N_EXPERTS, N_EXPERTS))
        ec = _lane_iota((N_EXPERTS, N_EXPERTS))
        below = jnp.where(er > ec, 1.0, 0.0)
        loff = jnp.dot(below, n8, preferred_element_type=F32, precision=hi)
        rows_e = jnp.sum(n8, axis=-1, keepdims=True) + jnp.zeros_like(n8)
        padded = jnp.floor((rows_e + (MOE_ROWS - 1)) * (1.0 / MOE_ROWS)) * MOE_ROWS
        e_start = jnp.dot(below, padded, preferred_element_type=F32, precision=hi)
        tr = _row_iota((LANES, LANES))
        tc = _lane_iota((LANES, LANES))
        earlier = jnp.where(tr < tc, 1.0, 0.0)
        goff = e_start + jnp.dot(n8, earlier, preferred_element_type=F32, precision=hi)
        loff_ref[...] = loff
        goff_ref[...] = goff
        n8_ref[...] = n8
        big = jnp.floor(n8 * (0.5 / CHUNK))
        small = n8 * (1.0 / CHUNK) - 2.0 * big
        row_t = _row_iota(tab_ref.shape)
        tab_ref[...] = jnp.where(row_t == 0, jnp.sum(big, axis=0, keepdims=True),
                                 jnp.where(row_t == 1, jnp.sum(small, axis=0, keepdims=True), 0.0)).astype(I32)
        reps = nbp // LANES
        pend_b = jnp.concatenate([e_start + padded] * reps, axis=1)
        vend_b = jnp.concatenate([e_start + rows_e] * reps, axis=1)
        used_b = jnp.concatenate([padded] * reps, axis=1) > 0.0
        b0 = (_lane_iota((N_EXPERTS, nbp)) * MOE_ROWS).astype(F32)
        bexp = jnp.sum(jnp.where(pend_b <= b0, 1.0, 0.0), axis=0, keepdims=True)
        bexp = jnp.minimum(bexp, N_EXPERTS - 1.0)
        e_b = _row_iota((N_EXPERTS, nbp)).astype(F32)
        is_e = e_b == bexp
        vend = jnp.sum(jnp.where(is_e, vend_b, 0.0), axis=0, keepdims=True)
        nvalid = jnp.clip(vend - b0[0:1], 0.0, float(MOE_ROWS))
        total = jnp.max(pend_b, axis=0, keepdims=True) * (1.0 / MOE_ROWS)
        order = jnp.sum(jnp.where(used_b & (e_b < bexp), 1.0, 0.0), axis=0, keepdims=True)
        nxt = jnp.min(jnp.where(used_b & (e_b > bexp), e_b, float(N_EXPERTS)), axis=0, keepdims=True)
        row = _row_iota((8, nbp))
        meta = jnp.where(row == 0, bexp, jnp.where(row == 1, nvalid, jnp.where(row == 2, total,
                         jnp.where(row == 3, order, jnp.where(row == 4, nxt, 0.0)))))
        meta_ref[...] = meta.astype(I32)

    su = jnp.where(_row_iota((tm, tm)) < _lane_iota((tm, tm)), 1.0, 0.0).astype(BF16)
    incl = jnp.where(_row_iota((N_EXPERTS, N_EXPERTS)) >= _lane_iota((N_EXPERTS, N_EXPERTS)), 1.0, 0.0)
    cidx = _lane_iota((N_EXPERTS, LANES)).astype(F32)
    e_iota = _row_iota((N_EXPERTS, tm))
    out_row = _row_iota((LANES, tm))
    for k in range(tiles):
        tok = slice(k * tm, (k + 1) * tm)
        oh0 = e_iota == r_ref[0:1, tok].astype(I32)
        oh1 = e_iota == r_ref[1:2, tok].astype(I32)
        oh = jnp.where(oh0 | oh1, 1.0, 0.0)
        tile_lane = _lane_iota((N_EXPERTS, LANES)) == i * tiles + k
        before = jnp.dot(oh.astype(BF16), su, preferred_element_type=F32)
        base = jnp.sum(jnp.where(tile_lane, loff_ref[...], 0.0), axis=-1, keepdims=True) + before
        d0 = jnp.sum(jnp.where(oh0, base, 0.0), axis=0, keepdims=True)
        d1 = jnp.sum(jnp.where(oh1, base, 0.0), axis=0, keepdims=True)
        lrow_ref[k, 0:1, :] = d0.astype(I32)
        lrow_ref[k, 1:2, :] = d1.astype(I32)
        pick_tile = lambda ref: jnp.sum(jnp.where(tile_lane, ref[...], 0.0), axis=-1, keepdims=True)
        nch = pick_tile(n8_ref) * (1.0 / CHUNK)
        n_big = jnp.floor(nch * 0.5)
        n_small = nch - 2.0 * n_big
        loff_t = pick_tile(loff_ref)
        goff_t = pick_tile(goff_ref)
        for c, (n, rows, first) in enumerate(((n_big, COPY_ROWS[0], 0.0),
                                              (n_small, COPY_ROWS[1], n_big * COPY_ROWS[0]))):
            cend = jnp.dot(incl, n + jnp.zeros((N_EXPERTS, LANES), F32), preferred_element_type=F32, precision=hi)
            cstart = cend - n
            mine = (cidx >= cstart) & (cidx < cend)
            step_rows = first + (cidx - cstart) * rows
            ctab_ref[k, 2 * c:2 * c + 1, :] = jnp.sum(jnp.where(mine, loff_t + step_rows, 0.0), axis=0,
                                                      keepdims=True).astype(I32)
            ctab_ref[k, 2 * c + 1:2 * c + 2, :] = jnp.sum(jnp.where(mine, goff_t + step_rows, 0.0), axis=0,
                                                          keepdims=True).astype(I32)
        lcol_ref[tok, :] = jnp.where(out_row == 0, d0, jnp.where(out_row == 1, d1,
                                     jnp.where(out_row == 2, r_ref[2:3, tok],
                                               jnp.where(out_row == 3, r_ref[3:4, tok], 0.0)))).T


def _route(rrow, cnt, tm, nbp):
    t = rrow.shape[1]
    nt = t // tm
    tiles = next(r for r in (4, 2, 1) if nt % r == 0)
    kern = functools.partial(_route_kernel, tm=tm, nbp=nbp, tiles=tiles)
    return pl.pallas_call(
        kern,
        grid=(nt // tiles,),
        in_specs=[pl.BlockSpec((8, tiles * tm), lambda i: (0, i)),
                  pl.BlockSpec((N_EXPERTS, LANES), lambda i: (0, 0))],
        out_specs=[pl.BlockSpec((tiles, 2, tm), lambda i: (i, 0, 0)),
                   pl.BlockSpec((tiles * tm, LANES), lambda i: (i, 0)),
                   pl.BlockSpec((8, LANES), lambda i: (0, 0)),
                   pl.BlockSpec((tiles, 2 * len(COPY_ROWS), LANES), lambda i: (i, 0, 0)),
                   pl.BlockSpec((8, nbp), lambda i: (0, 0))],
        out_shape=[jax.ShapeDtypeStruct((nt, 2, tm), I32), jax.ShapeDtypeStruct((t, LANES), F32),
                   jax.ShapeDtypeStruct((8, LANES), I32), jax.ShapeDtypeStruct((nt, 2 * len(COPY_ROWS), LANES), I32),
                   jax.ShapeDtypeStruct((8, nbp), I32)],
        scratch_shapes=[pltpu.VMEM((N_EXPERTS, LANES), F32)] * 3,
        compiler_params=_cparams("arbitrary"),
        name="moe_route",
    )(rrow, cnt)


def _chunk_copies(tabs, i, local_ref, global_ref, sem, to_global, action):
    tab_ref, ctab_ref = tabs
    for k, rows in enumerate(COPY_ROWS):
        count = tab_ref[k, i]

        def copy(lo, go, rows=rows):
            lsl = local_ref.at[pl.ds(pl.multiple_of(lo, CHUNK), rows)]
            gsl = global_ref.at[pl.ds(pl.multiple_of(go, CHUNK), rows)]
            return pltpu.make_async_copy(lsl, gsl, sem) if to_global else pltpu.make_async_copy(gsl, lsl, sem)

        if action == "wait":
            def one(c, c1, copy=copy):
                copy(0, 0).wait()
                return c1
        else:
            def one(c, c1, copy=copy, k=k):
                copy(ctab_ref[i, 2 * k, c], ctab_ref[i, 2 * k + 1, c]).start()
                return c1

        lax.fori_loop(0, count, one, 0)


def _scatter_kernel(tab_ref, ctab_ref, meta_ref, lrow_ref, h_ref, xs_ref, buf_ref, zero_ref, sem, zsem, *, tm, nb):
    i = pl.program_id(0)
    tabs = (tab_ref, ctab_ref)

    @pl.when(i == 0)
    def _():
        zero_ref[...] = jnp.zeros_like(zero_ref)
        n_used = meta_ref[2, 0]

        def zcopy(b):
            sub = lax.shift_right_logical(meta_ref[1, b], int(math.log2(FFN_SUB)))
            start = pl.multiple_of(b * MOE_ROWS + sub * FFN_SUB, FFN_SUB)
            return pltpu.make_async_copy(zero_ref, xs_ref.at[pl.ds(start, FFN_SUB)], zsem)

        def needs(b):
            return (b < n_used) & ((meta_ref[1, b] & (FFN_SUB - 1)) != 0)

        def start(b, c):
            @pl.when(needs(b))
            def _():
                zcopy(b).start()
            return c

        def wait(b, c):
            @pl.when(needs(b))
            def _():
                zcopy(b).wait()
            return c

        lax.fori_loop(0, nb, start, 0)
        lax.fori_loop(0, nb, wait, 0)

    rows = _row_iota((LOCAL_ROWS, tm))
    p0 = rows == lrow_ref[0, 0:1, :]
    p1 = rows == lrow_ref[0, 1:2, :]
    perm = jnp.where(p0 | p1, 1.0, 0.0).astype(BF16)
    sorted_rows = jnp.dot(perm, h_ref[...], preferred_element_type=F32)

    def fill(slot):
        buf = buf_ref.at[slot]
        buf[...] = _pack_bf16_pairs(sorted_rows, exact=True)
        _chunk_copies(tabs, i, buf, xs_ref, sem.at[slot], True, "start")

    def drain(tile, slot):
        _chunk_copies(tabs, tile, buf_ref.at[slot], xs_ref, sem.at[slot], True, "wait")

    even = (i & 1) == 0

    @pl.when(even)
    def _():
        fill(0)

    @pl.when(jnp.logical_not(even))
    def _():
        fill(1)

    @pl.when((i > 0) & even)
    def _():
        drain(i - 1, 1)

    @pl.when((i > 0) & jnp.logical_not(even))
    def _():
        drain(i - 1, 0)

    @pl.when((i == pl.num_programs(0) - 1) & even)
    def _():
        drain(i, 0)

    @pl.when((i == pl.num_programs(0) - 1) & jnp.logical_not(even))
    def _():
        drain(i, 1)


def _scatter(tab, ctab, meta, lrow, h2, tm, nb):
    t = h2.shape[0]
    kern = functools.partial(_scatter_kernel, tm=tm, nb=nb)
    grid_spec = pltpu.PrefetchScalarGridSpec(
        num_scalar_prefetch=3,
        grid=(t // tm,),
        in_specs=[pl.BlockSpec((1, 2, tm), lambda i, *_: (i, 0, 0)),
                  pl.BlockSpec((tm, D_MODEL), lambda i, *_: (i, 0))],
        out_specs=pl.BlockSpec(memory_space=pl.ANY),
        scratch_shapes=[pltpu.VMEM((2, LOCAL_ROWS, XS_WIDTH), U32), pltpu.VMEM((FFN_SUB, XS_WIDTH), U32),
                        pltpu.SemaphoreType.DMA((2,)), pltpu.SemaphoreType.DMA],
    )
    return pl.pallas_call(
        kern,
        grid_spec=grid_spec,
        out_shape=jax.ShapeDtypeStruct((nb * MOE_ROWS, XS_WIDTH), U32),
        compiler_params=_cparams("arbitrary"),
        name="moe_scatter",
    )(tab, ctab, meta, lrow, h2)


def _ffn_kernel(meta_ref, x_ref, wg_ref, wu_ref, wd_ref, o_ref, wgu_b, wd_b, wg_f, wu_f, wd_f, wsem, *, layer):
    b = pl.program_id(0)
    live = b < meta_ref[2, 0]
    expert = meta_ref[0, b]
    prev = meta_ref[0, jnp.maximum(b - 1, 0)]

    def fetch(e, slot):
        return [pltpu.make_async_copy(src.at[layer, e], dst.at[slot], wsem.at[slot])
                for src, dst in ((wg_ref, wg_f), (wu_ref, wu_f), (wd_ref, wd_f))]

    def first_block(slot):
        @pl.when(b == 0)
        def _():
            for cp in fetch(expert, slot):
                cp.start()

        for cp in fetch(expert, slot):
            cp.wait()
        wgu_b[:, 0:EXPERT_FF] = wg_f[slot].astype(BF16)
        wgu_b[:, EXPERT_FF:2 * EXPERT_FF] = wu_f[slot].astype(BF16)
        wd_b[...] = wd_f[slot].astype(BF16)
        nxt = meta_ref[4, b]

        @pl.when(nxt < N_EXPERTS)
        def _():
            for cp in fetch(nxt, 1 - slot):
                cp.start()

    changed = live & ((b == 0) | (expert != prev))
    odd = (meta_ref[3, b] & 1) == 1

    @pl.when(changed & jnp.logical_not(odd))
    def _():
        first_block(0)

    @pl.when(changed & odd)
    def _():
        first_block(1)

    nvalid = meta_ref[1, jnp.maximum(jnp.minimum(b, meta_ref[2, 0] - 1), 0)]
    for sub in range(MOE_ROWS // FFN_SUB):
        rows = slice(sub * FFN_SUB, (sub + 1) * FFN_SUB)
        used = live & (nvalid > sub * FFN_SUB)

        @pl.when(used)
        def _():
            x = _unpack_bf16_pairs(x_ref[rows, 0:PACKED])
            gu = jnp.dot(x, wgu_b[...], preferred_element_type=F32)
            act = (_silu(gu[:, 0:EXPERT_FF]) * gu[:, EXPERT_FF:2 * EXPERT_FF]).astype(BF16)
            y = jnp.dot(act, wd_b[...], preferred_element_type=F32)
            o_ref[rows, :] = _pack_bf16_pairs(y)

        @pl.when(live & jnp.logical_not(used))
        def _():
            o_ref[rows, :] = jnp.zeros((FFN_SUB, PACKED), U32)


def _ffn(meta, xs, wg, wu, wd, layer, nb):
    def blk(b, m):
        return (jnp.maximum(jnp.minimum(b, m[2, 0] - 1), 0), 0)

    grid_spec = pltpu.PrefetchScalarGridSpec(
        num_scalar_prefetch=1,
        grid=(nb,),
        in_specs=[pl.BlockSpec((MOE_ROWS, XS_WIDTH), blk)] + [pl.BlockSpec(memory_space=pl.ANY)] * 3,
        out_specs=pl.BlockSpec((MOE_ROWS, PACKED), blk),
        scratch_shapes=[pltpu.VMEM((D_MODEL, 2 * EXPERT_FF), BF16), pltpu.VMEM((EXPERT_FF, D_MODEL), BF16),
                        pltpu.VMEM((2, D_MODEL, EXPERT_FF), F32), pltpu.VMEM((2, D_MODEL, EXPERT_FF), F32),
                        pltpu.VMEM((2, EXPERT_FF, D_MODEL), F32), pltpu.SemaphoreType.DMA((2,))],
    )
    return pl.pallas_call(
        functools.partial(_ffn_kernel, layer=layer),
        grid_spec=grid_spec,
        out_shape=jax.ShapeDtypeStruct((nb * MOE_ROWS, PACKED), U32),
        compiler_params=_cparams("arbitrary"),
        name="moe_experts",
    )(meta, xs, wg, wu, wd)


def _gather_kernel(tab_ref, ctab_ref, lcol_ref, x_ref, g_ref, ys_ref, o_ref, buf_ref, sem, *, tm, final):
    i = pl.program_id(0)

    last = pl.num_programs(0) - 1

    def fetch(tile, slot, action):
        _chunk_copies((tab_ref, ctab_ref), tile, buf_ref.at[slot], ys_ref, sem.at[slot], False, action)

    @pl.when(i == 0)
    def _():
        buf_ref[...] = jnp.zeros_like(buf_ref)
        fetch(0, 0, "start")

    even = (i & 1) == 0

    @pl.when((i < last) & even)
    def _():
        fetch(i + 1, 1, "start")

    @pl.when((i < last) & jnp.logical_not(even))
    def _():
        fetch(i + 1, 0, "start")

    col = _lane_iota((tm, LOCAL_ROWS)).astype(F32)
    pick0 = jnp.where(col == lcol_ref[:, 0:1], 1.0, 0.0).astype(BF16)
    pick1 = jnp.where(col == lcol_ref[:, 1:2], 1.0, 0.0).astype(BF16)

    def finish(slot):
        fetch(i, slot, "wait")
        y = _unpack_bf16_pairs(buf_ref[slot])
        both = jnp.dot(jnp.concatenate([pick0, pick1], axis=0), y, preferred_element_type=F32)
        x = x_ref[...] + lcol_ref[:, 2:3] * both[0:tm] + lcol_ref[:, 3:4] * both[tm:2 * tm]
        o_ref[...] = _rms(x, g_ref[...]) if final else x

    @pl.when(even)
    def _():
        finish(0)

    @pl.when(jnp.logical_not(even))
    def _():
        finish(1)


def _gather(tab, ctab, lcol, x2, g, ys, tm, final):
    t = x2.shape[0]
    kern = functools.partial(_gather_kernel, tm=tm, final=final)
    grid_spec = pltpu.PrefetchScalarGridSpec(
        num_scalar_prefetch=2,
        grid=(t // tm,),
        in_specs=[pl.BlockSpec((tm, LANES), lambda i, *_: (i, 0)),
                  pl.BlockSpec((tm, D_MODEL), lambda i, *_: (i, 0)),
                  pl.BlockSpec((1, D_MODEL), lambda i, *_: (0, 0)),
                  pl.BlockSpec(memory_space=pl.ANY)],
        out_specs=pl.BlockSpec((tm, D_MODEL), lambda i, *_: (i, 0)),
        scratch_shapes=[pltpu.VMEM((2, LOCAL_ROWS, PACKED), U32), pltpu.SemaphoreType.DMA((2,))],
    )
    return pl.pallas_call(
        kern,
        grid_spec=grid_spec,
        out_shape=jax.ShapeDtypeStruct((t, D_MODEL), F32),
        compiler_params=_cparams("arbitrary"),
        name="moe_combine",
    )(tab, ctab, lcol, x2, g, ys)


def _pad_rows(a, rows=8):
    return jnp.zeros((rows, a.shape[-1]), F32).at[:a.shape[0]].set(a.astype(F32))


def _arrange_mla(w_uq, w_ukv):
    half = MLA_ROPE // 2
    qd = MLA_NOPE + MLA_ROPE
    wq, wqs, wk, wv = [], [], [], []
    zq = jnp.zeros((MLA_Q_LORA, LANES - qd), w_uq.dtype)
    zk = jnp.zeros((MLA_KV_LORA, LANES - MLA_NOPE), w_ukv.dtype)
    for h in range(N_HEADS):
        q = w_uq[:, h * qd:(h + 1) * qd]
        nope, rope = q[:, :MLA_NOPE], q[:, MLA_NOPE:]
        wq.append(jnp.concatenate([nope, rope, zq], axis=1))
        wqs.append(jnp.concatenate([jnp.zeros_like(nope), -rope[:, half:], rope[:, :half], zq], axis=1))
        kv = w_ukv[:, h * 2 * MLA_NOPE:(h + 1) * 2 * MLA_NOPE]
        wk.append(jnp.concatenate([kv[:, :MLA_NOPE], zk], axis=1))
        wv.append(kv[:, MLA_NOPE:])
    cat = lambda xs: jnp.concatenate(xs, axis=1).astype(BF16)
    return cat(wq), cat(wqs), cat(wk), cat(wv).T


def kernel(x, positions, norm_mix, w_in, conv_a, fox_forget_bias, ssm_conv_w, ssm_conv_b, ssm_dt_bias,
           ssm_a_log, ssm_d, ssm_norm, mla_q_norm, mla_kv_norm, mla_w_uq, mla_w_ukv, w_out, norm_ffn,
           router_group_w, router_group_b, router_expert_w, router_expert_b, expert_w_gate, expert_w_up,
           expert_w_down, norm_final):
    batch, seq, d = x.shape
    t = batch * seq
    depth = w_in.shape[0]
    tm = min(512, t)
    tq = min(ATTN_TQ, seq)
    tmd = min(MOE_TILE, t)
    max_rows = 2 * t + (CHUNK - 1) * N_EXPERTS * (t // tmd) + N_EXPERTS * (MOE_ROWS - 1)
    nb = -(-max_rows // MOE_ROWS)
    nbp = -(-nb // LANES) * LANES

    xf = x.reshape(t, d)
    pos_col = positions.astype(F32).reshape(t, 1)
    cos, sin = _rope_tables(pos_col, tm)
    w_in_t = jnp.swapaxes(w_in, 1, 2)

    for l in range(depth):
        wq, wqs, wk, wv = _arrange_mla(mla_w_uq[l], mla_w_ukv[l])
        ya, pb, pc, misc, fox_vt, q, k, v = _inproj(
            xf, norm_mix[l][None, :], w_in_t, l, _pad_rows(conv_a[l]), cos, sin, mla_q_norm[l][None, :],
            mla_kv_norm[l][None, :], jnp.concatenate([wq, wqs], axis=1), wk, wv, tm, seq)

        sp = jnp.zeros((8, LANES), F32)
        sp = sp.at[0, MISC_F:MISC_F + N_HEADS].set(fox_forget_bias[l])
        sp = sp.at[0, MISC_DT:MISC_DT + N_HEADS].set(ssm_dt_bias[l])
        sp = sp.at[1, MISC_DT:MISC_DT + N_HEADS].set(ssm_a_log[l])
        col, rows, fox_q, fox_k, tref = _scalar_prep(misc, sp, pb, batch, seq, tq)

        yb = _attention(fox_q, fox_k, fox_vt, tref, batch, seq, tq, "fox_attention")
        conv_wb = _pad_rows(jnp.concatenate([ssm_conv_w[l], ssm_conv_b[l][None, :]], axis=0))
        ssd_par = _pad_rows(jnp.stack([jnp.repeat(ssm_d[l], HEAD_DIM), ssm_norm[l]]))
        yc = _ssd_mixer(pc, col, rows, conv_wb, ssd_par, batch, seq)
        yd = _attention(q, k, v, None, batch, seq, tq, "mla_attention")

        pad = jnp.zeros((d, LANES - N_EXPERTS - N_EXPERT_GROUPS), F32)
        wr = jnp.concatenate([router_expert_w[l], router_group_w[l], pad], axis=1)
        wr_hi = wr.astype(BF16)
        wr = jnp.concatenate([wr_hi, (wr - wr_hi.astype(F32)).astype(BF16)], axis=1)
        br = jnp.concatenate([router_expert_b[l], router_group_b[l], pad[0]])[None, :]
        x2, h2, rrow, cnt = _outproj(xf, ya, yb, yc, yd, w_out[l].astype(BF16), norm_ffn[l][None, :], wr, br,
                                     tm, tmd)

        lrow, lcol, tab, ctab, meta = _route(rrow, cnt, tmd, nbp)
        xs = _scatter(tab, ctab, meta, lrow, h2, tmd, nb)
        ys = _ffn(meta, xs, expert_w_gate, expert_w_up, expert_w_down, l, nb)
        final = l == depth - 1
        xf = _gather(tab, ctab, lcol, x2, norm_final[None, :], ys, tmd, final)

    return xf.reshape(batch, seq, d)
```

```python
import functools
import math

import jax
import jax.numpy as jnp
import numpy as np
from jax import lax
from jax.experimental import pallas as pl
from jax.experimental.pallas import tpu as pltpu

F32 = jnp.float32
BF16 = jnp.bfloat16
I32 = jnp.int32

LANES = 128
VMEM_LIMIT_BYTES = 56 * 1024 * 1024

D_MODEL = 1024
RMS_EPS = 1e-6
LOG2E = math.log2(math.e)
GROUP_WIDTH = 256
HEAD_DIM = 64
N_HEADS = 4

CONV_A_WIDTH = 3
SSM_CONV = 4
SSM_STATE = 64
SSM_CHUNK = 256

MLA_NOPE = 64
MLA_ROPE = 32
MLA_Q_LORA = 256
MLA_KV_LORA = 128
ROPE_BASE = 10000.0
MLA_CHUNK = 64
ATTN_TQ = 512

N_EXPERT_GROUPS = 4
EXPERTS_PER_GROUP = 8
N_EXPERTS = 32
EXPERT_FF = 256
ROUTER_ROWS = 48
MOE_ROWS = 512
FFN_SUB = 256

SEG_A = (0, 768)
SEG_B = (768, 1280)
SEG_C = (1280, 2048)
SEG_D = (2048, 2432)
SEG_M = (2432, 2560)
SEG_M2 = (2560, 2688)
IN_COLS_PADDED = 2688
HEAD_PAD = N_HEADS * LANES
AUG_LANE = HEAD_DIM
MISC_F = 0
MISC_DT = 4
MISC_ROPE = 64
COL_CUMF = 0
COL_DT = 4
COL_ACUM = 8
N_SCALAR_ROWS = 16


def _cparams(*sem):
    return pltpu.CompilerParams(dimension_semantics=sem, vmem_limit_bytes=VMEM_LIMIT_BYTES)


def _lane_iota(shape):
    return lax.broadcasted_iota(I32, shape, len(shape) - 1)


def _row_iota(shape):
    return lax.broadcasted_iota(I32, shape, 0)


def _rms(x, g):
    ms = jnp.mean(x * x, axis=-1, keepdims=True)
    return x * lax.rsqrt(ms + RMS_EPS) * g


def _silu(x):
    return x / (1.0 + jnp.exp(-x))


def _softplus(x):
    return jnp.maximum(x, 0.0) + jnp.log(1.0 + jnp.exp(-jnp.abs(x)))


def _shift_rows(x, k):
    rolled = pltpu.roll(x, k, 0)
    return jnp.where(_row_iota(x.shape) >= k, rolled, 0.0)


def _rope_kernel(pos_ref, freq_ref, cos_ref, sin_ref):
    tm = pos_ref.shape[0]
    q = tm // ROPE_PACK
    lane = _lane_iota((q, LANES))
    group = lane >> int(math.log2(MLA_ROPE))
    pos = pos_ref[(ROPE_PACK - 1) * q:ROPE_PACK * q, :]
    for k in range(ROPE_PACK - 2, -1, -1):
        pos = jnp.where(group == k, pos_ref[k * q:(k + 1) * q, :], pos)
    ang = pos * freq_ref[...]
    c = jnp.cos(ang)
    s = jnp.sin(ang)
    rope = (lane >= MISC_ROPE) & (lane < MISC_ROPE + MLA_ROPE)
    for k in range(ROPE_PACK):
        shift = (MISC_ROPE - MLA_ROPE * k) % LANES
        ck = pltpu.roll(c, shift, 1) if shift else c
        sk = pltpu.roll(s, shift, 1) if shift else s
        cos_ref[k * q:(k + 1) * q, :] = jnp.where(rope, ck, jnp.where(lane < MISC_ROPE, 1.0, 0.0))
        sin_ref[k * q:(k + 1) * q, :] = jnp.where(rope, sk, 0.0)


ROPE_PACK = LANES // MLA_ROPE


def _rope_tables(pos_col, tm):
    t = pos_col.shape[0]
    inv = ROPE_BASE ** (-np.arange(0, MLA_ROPE, 2, dtype=np.float32) / MLA_ROPE)
    freq = np.tile(np.concatenate([inv, inv]), ROPE_PACK)[None, :].astype(np.float32)
    return pl.pallas_call(
        _rope_kernel,
        grid=(t // tm,),
        in_specs=[pl.BlockSpec((tm, 1), lambda i: (i, 0)),
                  pl.BlockSpec((1, LANES), lambda i: (0, 0))],
        out_specs=[pl.BlockSpec((tm, LANES), lambda i: (i, 0))] * 2,
        out_shape=[jax.ShapeDtypeStruct((t, LANES), F32)] * 2,
        compiler_params=_cparams("parallel"),
        name="rope_tables",
    )(pos_col, jnp.asarray(freq))


def _arrange_w_in_kernel(wt_ref, w_ref, wvt_ref):
    gw = GROUP_WIDTH
    rows = lambda lo, hi: wt_ref[0, lo:hi, :]

    def put(seg0, piece_t):
        w_ref[:, seg0:seg0 + piece_t.shape[0]] = piece_t.T.astype(BF16)

    for j in range(3):
        put(SEG_A[0] + j * gw, rows(j * gw, (j + 1) * gw))
    put(SEG_B[0], rows(3 * gw, 4 * gw) * (HEAD_DIM ** -0.5 * LOG2E))
    put(SEG_B[0] + gw, rows(4 * gw, 5 * gw))
    wvt_ref[...] = rows(5 * gw, 6 * gw).astype(BF16)
    c0 = 6 * gw
    win = pltpu.roll(rows(c0, c0 + 3 * gw + 8), 3 * gw + 8 - N_HEADS, 0)
    for j in range(3):
        put(SEG_C[0] + j * gw, win[j * gw:(j + 1) * gw])
    d0 = c0 + 3 * gw + 8
    put(SEG_D[0], rows(d0, d0 + MLA_Q_LORA))
    put(SEG_D[0] + MLA_Q_LORA, rows(d0 + MLA_Q_LORA, d0 + MLA_Q_LORA + MLA_KV_LORA))
    kr0 = d0 + MLA_Q_LORA + MLA_KV_LORA
    half = MLA_ROPE // 2
    first8 = jnp.where(_row_iota((8, D_MODEL)) < N_HEADS, rows(c0, c0 + 8), rows(d0 - 8, d0))
    zeros = lambda n: jnp.zeros((n, D_MODEL), F32)
    misc_t = jnp.concatenate([first8, zeros(MISC_ROPE - 8), rows(kr0, kr0 + MLA_ROPE),
                              zeros(LANES - MISC_ROPE - MLA_ROPE)], axis=0)
    misc2_t = jnp.concatenate([zeros(MISC_ROPE), -rows(kr0 + half, kr0 + MLA_ROPE), rows(kr0, kr0 + half),
                               zeros(LANES - MISC_ROPE - MLA_ROPE)], axis=0)
    put(SEG_M[0], misc_t)
    put(SEG_M2[0], misc2_t)


def _inproj_kernel(x_ref, g_ref, wt_ref, cw_ref, cos_ref, sin_ref, nq_ref, nkv_ref, wq2_ref, wk_ref,
                   wvt2_ref, oa, ob, oc, om, ovt, q_ref, k_ref, vt_ref, w_ref, wvt_ref, halo_ref,
                   *, tm, tiles_per_seq):
    @pl.when(pl.program_id(0) == 0)
    def _():
        _arrange_w_in_kernel(wt_ref, w_ref, wvt_ref)

    h = _rms(x_ref[...], g_ref[...]).astype(BF16)
    for o, (lo, hi) in ((ob, SEG_B), (oc, SEG_C)):
        o[...] = jnp.dot(h, w_ref[:, lo:hi], preferred_element_type=F32).astype(o.dtype)

    gw = GROUP_WIDTH
    pa = jnp.dot(h, w_ref[:, SEG_A[0]:SEG_A[1]], preferred_element_type=F32)
    cv = pa[:, gw:2 * gw] * pa[:, 2 * gw:3 * gw]

    @pl.when(pl.program_id(0) % tiles_per_seq == 0)
    def _():
        halo_ref[...] = jnp.zeros_like(halo_ref)

    halo = halo_ref[...]
    row8 = _row_iota(halo.shape)
    acc = cv * cw_ref[CONV_A_WIDTH - 1:CONV_A_WIDTH, :]
    for k in range(1, CONV_A_WIDTH):
        shifted = pltpu.roll(cv, k, 0)
        top = jnp.where(row8 < k, pltpu.roll(halo, k, 0), shifted[0:8])
        shifted = jnp.concatenate([top, shifted[8:]], axis=0)
        acc = acc + shifted * cw_ref[CONV_A_WIDTH - 1 - k:CONV_A_WIDTH - k, :]
    halo_ref[...] = cv[tm - 8:tm]
    oa[...] = (pa[:, 0:gw] * acc).astype(oa.dtype)
    ovt[...] = lax.dot_general(wvt_ref[...], h, (((1,), (1,)), ((), ())),
                               preferred_element_type=F32).astype(ovt.dtype)
    misc = jnp.dot(h, w_ref[:, SEG_M[0]:SEG_M[1]], preferred_element_type=F32)
    misc2 = jnp.dot(h, w_ref[:, SEG_M2[0]:SEG_M2[1]], preferred_element_type=F32)
    om[...] = misc

    pd = jnp.dot(h, w_ref[:, SEG_D[0]:SEG_D[1]], preferred_element_type=F32)
    cq = _rms(pd[:, 0:MLA_Q_LORA], nq_ref[...]).astype(BF16)
    ckv = _rms(pd[:, MLA_Q_LORA:MLA_Q_LORA + MLA_KV_LORA], nkv_ref[...]).astype(BF16)
    cos = cos_ref[...]
    sin = sin_ref[...]
    cos4 = jnp.concatenate([cos] * N_HEADS, axis=1)
    sin4 = jnp.concatenate([sin] * N_HEADS, axis=1)
    scale = (MLA_NOPE + MLA_ROPE) ** -0.5 * LOG2E
    q2 = jnp.dot(cq, wq2_ref[...], preferred_element_type=F32)
    q_ref[...] = ((q2[:, 0:HEAD_PAD] * cos4 + q2[:, HEAD_PAD:2 * HEAD_PAD] * sin4) * scale).astype(q_ref.dtype)
    lane = _lane_iota(cos.shape)
    rope = (lane >= MISC_ROPE) & (lane < MISC_ROPE + MLA_ROPE)
    kr = jnp.where(rope, misc * cos + misc2 * sin, 0.0)
    k = jnp.dot(ckv, wk_ref[...], preferred_element_type=F32)
    k_ref[...] = (k + jnp.concatenate([kr] * N_HEADS, axis=1)).astype(k_ref.dtype)
    vt_ref[...] = lax.dot_general(wvt2_ref[...], ckv, (((1,), (1,)), ((), ())),
                                  preferred_element_type=F32).astype(vt_ref.dtype)


def _inproj(x, g, w_in_t, layer, conv_w, cos, sin, nq, nkv, wq2, wk, wvt2, tm, seq):
    t = x.shape[0]
    once = lambda a: pl.BlockSpec((1,) + a.shape[1:], lambda i: (layer, 0, 0), pipeline_mode=pl.Buffered(1))
    full = lambda a: pl.BlockSpec(a.shape, lambda i: (0, 0))
    tile = lambda wd: pl.BlockSpec((tm, wd), lambda i: (i, 0))
    cols = lambda: pl.BlockSpec((GROUP_WIDTH, tm), lambda i: (0, i))
    seg = lambda s: s[1] - s[0]
    return pl.pallas_call(
        functools.partial(_inproj_kernel, tm=tm, tiles_per_seq=seq // tm),
        grid=(t // tm,),
        in_specs=[tile(D_MODEL), full(g), once(w_in_t), full(conv_w), tile(LANES), tile(LANES),
                  full(nq), full(nkv), full(wq2), full(wk), full(wvt2)],
        out_specs=[tile(GROUP_WIDTH), tile(seg(SEG_B)), tile(seg(SEG_C)), tile(LANES), cols(),
                   tile(HEAD_PAD), tile(HEAD_PAD), cols()],
        out_shape=[jax.ShapeDtypeStruct((t, GROUP_WIDTH), BF16), jax.ShapeDtypeStruct((t, seg(SEG_B)), BF16),
                   jax.ShapeDtypeStruct((t, seg(SEG_C)), BF16), jax.ShapeDtypeStruct((t, LANES), F32),
                   jax.ShapeDtypeStruct((GROUP_WIDTH, t), BF16),
                   jax.ShapeDtypeStruct((t, HEAD_PAD), BF16), jax.ShapeDtypeStruct((t, HEAD_PAD), BF16),
                   jax.ShapeDtypeStruct((GROUP_WIDTH, t), BF16)],
        scratch_shapes=[pltpu.VMEM((D_MODEL, IN_COLS_PADDED), BF16), pltpu.VMEM((GROUP_WIDTH, D_MODEL), BF16),
                        pltpu.VMEM((8, GROUP_WIDTH), F32)],
        compiler_params=_cparams("arbitrary"),
        name="inproj",
    )(x, g, w_in_t, conv_w, cos, sin, nq, nkv, wq2, wk, wvt2)


def _scalar_prep_kernel(m_ref, p_ref, qk_ref, place_ref, const_ref,
                        col_ref, row_ref, qa_ref, ka_ref, tref_ref, *, tq):
    s = m_ref.shape[0]
    tref_ref[...] = jnp.zeros_like(tref_ref)
    tile_ref = jnp.zeros((1, LANES), F32)
    m = m_ref[...]
    bias = p_ref[0:1, :]
    a_log = p_ref[1:2, :]
    lane = _lane_iota(m.shape)
    z = m + bias
    logf = jnp.minimum(z, 0.0) - jnp.log(1.0 + jnp.exp(-jnp.abs(z)))
    dt = _softplus(z)
    a = dt * (-jnp.exp(a_log))
    is_f = lane < MISC_DT
    is_dt = (lane >= MISC_DT) & (lane < MISC_DT + N_HEADS)
    v = jnp.where(is_f, logf, jnp.where(is_dt, a, 0.0))
    r = _row_iota((SSM_CHUNK, SSM_CHUNK))
    c = _lane_iota((SSM_CHUNK, SSM_CHUNK))
    tril = jnp.where(r >= c, 1.0, 0.0).astype(BF16)
    carry = jnp.zeros((1, LANES), F32)
    lane_1 = _lane_iota((1, LANES))
    lane_b = _lane_iota((SSM_CHUNK, LANES))
    low = lane_b < HEAD_DIM
    aug_lanes = (lane_b >= AUG_LANE) & (lane_b < AUG_LANE + AUG_TERMS)
    for ci in range(s // SSM_CHUNK):
        rest = v[ci * SSM_CHUNK:(ci + 1) * SSM_CHUNK]
        cs = jnp.zeros((SSM_CHUNK, LANES), F32)
        for _ in range(3):
            term = rest.astype(BF16)
            cs = cs + jnp.dot(tril, term, preferred_element_type=F32)
            rest = rest - term.astype(F32)
        cs = cs + jnp.where(lane_1 < MISC_DT, carry, 0.0)
        carry = cs[SSM_CHUNK - 1:SSM_CHUNK]
        acum = pltpu.roll(cs, COL_ACUM - MISC_DT, 1)
        out = jnp.where(lane_b < MISC_DT, cs * LOG2E,
                        jnp.where(lane_b < COL_ACUM, dt[ci * SSM_CHUNK:(ci + 1) * SSM_CHUNK],
                                  jnp.where(lane_b < COL_ACUM + N_HEADS, acum, 0.0)))
        rows = slice(ci * SSM_CHUNK, (ci + 1) * SSM_CHUNK)
        col_ref[rows, :] = out
        row_ref[0, :, rows] = out.T[:N_SCALAR_ROWS]
        if (ci * SSM_CHUNK) % tq == 0:
            tile_ref = out[0:1, :]
            ti = (ci * SSM_CHUNK) // tq
            tref_ref[0, ti:ti + 1, :] = tile_ref
        c = out - tile_ref
        c_hi = c.astype(BF16)
        r1 = c - c_hi.astype(F32)
        c_mid = r1.astype(BF16)
        c_lo = (r1 - c_mid.astype(F32)).astype(BF16)
        compact = jnp.dot(jnp.concatenate([c_hi, c_mid, c_lo], axis=1), place_ref[...],
                          preferred_element_type=F32) + const_ref[0:1, :]
        for side, o_ref in enumerate((qa_ref, ka_ref)):
            dec = compact[:, side * LANES:(side + 1) * LANES]
            for h in range(N_HEADS):
                pair = qk_ref[rows, side * GROUP_WIDTH + (h // 2) * LANES:
                              side * GROUP_WIDTH + (h // 2 + 1) * LANES].astype(F32)
                feat = pair if h % 2 == 0 else pltpu.roll(pair, HEAD_DIM, 1)
                dec_h = pltpu.roll(dec, AUG_LANE - AUG_TERMS * h, 1)
                group = jnp.where(low, feat, jnp.where(aug_lanes, dec_h, 0.0))
                o_ref[rows, h * LANES:(h + 1) * LANES] = group.astype(o_ref.dtype)


AUG_TERMS = 6


def _fox_placement():
    place = np.zeros((3 * LANES, 2 * LANES), np.float32)
    const = np.zeros((8, 2 * LANES), np.float32)
    for h in range(N_HEADS):
        a0 = AUG_TERMS * h
        for term in range(3):
            place[term * LANES + COL_CUMF + h, a0 + term] = 1.0
            place[term * LANES + COL_CUMF + h, LANES + a0 + 3 + term] = -1.0
            const[0, a0 + 3 + term] = 1.0
            const[0, LANES + a0 + term] = 1.0
    return jnp.asarray(place, BF16), jnp.asarray(const, F32)


def _scalar_prep(misc, params, qk, batch, seq, tq):
    place, const = _fox_placement()
    full = lambda a: pl.BlockSpec(a.shape, lambda b: (0,) * a.ndim)
    return pl.pallas_call(
        functools.partial(_scalar_prep_kernel, tq=tq),
        grid=(batch,),
        in_specs=[pl.BlockSpec((seq, LANES), lambda b: (b, 0)),
                  pl.BlockSpec((8, LANES), lambda b: (0, 0)),
                  pl.BlockSpec((seq, 2 * GROUP_WIDTH), lambda b: (b, 0)),
                  full(place), full(const)],
        out_specs=[pl.BlockSpec((seq, LANES), lambda b: (b, 0)),
                   pl.BlockSpec((1, N_SCALAR_ROWS, seq), lambda b: (b, 0, 0)),
                   pl.BlockSpec((seq, HEAD_PAD), lambda b: (b, 0)),
                   pl.BlockSpec((seq, HEAD_PAD), lambda b: (b, 0)),
                   pl.BlockSpec((1, 8, LANES), lambda b: (b, 0, 0))],
        out_shape=[jax.ShapeDtypeStruct((batch * seq, LANES), F32),
                   jax.ShapeDtypeStruct((batch, N_SCALAR_ROWS, seq), F32),
                   jax.ShapeDtypeStruct((batch * seq, HEAD_PAD), BF16),
                   jax.ShapeDtypeStruct((batch * seq, HEAD_PAD), BF16),
                   jax.ShapeDtypeStruct((batch, 8, LANES), F32)],
        compiler_params=_cparams("parallel"),
        name="scalar_prep",
    )(misc, params, qk, place, const)


def _pair_lanes(col, base, shape):
    lane = _lane_iota(shape)
    return jnp.where(lane < HEAD_DIM, col[:, base:base + 1], col[:, base + 1:base + 2])


def _ssd_kernel(p_ref, col_ref, row_ref, cw_ref, par_ref, o_ref, u_ref):
    s = p_ref.shape[0]
    q = SSM_CHUNK
    gw = GROUP_WIDTH
    xbc = p_ref[:, gw:3 * gw].astype(F32)
    acc = xbc * cw_ref[SSM_CONV - 1:SSM_CONV, :]
    for k in range(1, SSM_CONV):
        acc = acc + _shift_rows(xbc, k) * cw_ref[SSM_CONV - 1 - k:SSM_CONV - k, :]
    u_ref[...] = _silu(acc + cw_ref[SSM_CONV:SSM_CONV + 1, :])

    d_skip = par_ref[0:1, :]
    norm_g = par_ref[1:2, :]
    lane_q = _lane_iota((q, LANES))
    low = lane_q < HEAD_DIM
    tri = _row_iota((q, q)) >= _lane_iota((q, q))

    def chunk(ci, states):
        rows = pl.ds(ci * q, q)
        u = u_ref[rows, :]
        col = col_ref[rows, :]
        bm = u[:, gw:gw + LANES]
        cm = u[:, gw + LANES:gw + 2 * LANES]
        z = p_ref[rows, 0:gw].astype(F32)
        new_states = []
        ys = []
        for g in range(2):
            sel = low if g == 0 else jnp.logical_not(low)
            cg = jnp.where(sel, cm, 0.0).astype(BF16)
            bg = jnp.where(sel, bm, 0.0)
            gmat = lax.dot_general(cg, bm.astype(BF16), (((1,), (1,)), ((), ())),
                                   preferred_element_type=F32)
            xs = u[:, g * LANES:(g + 1) * LANES]
            dt2 = _pair_lanes(col, COL_DT + 2 * g, (q, LANES))
            ac2 = _pair_lanes(col, COL_ACUM + 2 * g, (q, LANES))
            xdt = xs * dt2
            xdt_b = xdt.astype(BF16)
            st = states[g]
            y_off = jnp.dot(cg, st.astype(BF16), preferred_element_type=F32) * jnp.exp(ac2)
            halves = []
            for hh in range(2):
                h = 2 * g + hh
                ac_col = col[:, COL_ACUM + h:COL_ACUM + h + 1]
                ac_row = row_ref[0, COL_ACUM + h:COL_ACUM + h + 1, rows]
                decay = jnp.exp(jnp.where(tri, ac_col - ac_row, -1e30))
                mm = (gmat * decay).astype(BF16)
                halves.append(jnp.dot(mm, xdt_b, preferred_element_type=F32))
            y = jnp.where(low, halves[0], halves[1]) + y_off + d_skip[:, g * LANES:(g + 1) * LANES] * xs
            ys.append(y)
            ac_last = ac2[q - 1:q, :]
            w_end = jnp.exp(ac_last - ac2)
            xw = (xdt * w_end).astype(BF16)
            upd = jnp.dot(bg.T.astype(BF16), xw, preferred_element_type=F32)
            new_states.append(st * jnp.exp(ac_last) + upd)
        yfull = jnp.concatenate(ys, axis=1) * _silu(z)
        o_ref[rows, :] = _rms(yfull, norm_g).astype(o_ref.dtype)
        return tuple(new_states)

    init = (jnp.zeros((LANES, LANES), F32), jnp.zeros((LANES, LANES), F32))
    states = init
    for ci in range(s // q):
        states = chunk(ci, states)


def _ssd_mixer(pc, col, rows, conv_wb, par, batch, seq):
    gw = GROUP_WIDTH
    return pl.pallas_call(
        _ssd_kernel,
        grid=(batch,),
        in_specs=[pl.BlockSpec((seq, 3 * gw), lambda b: (b, 0)),
                  pl.BlockSpec((seq, LANES), lambda b: (b, 0)),
                  pl.BlockSpec((1, N_SCALAR_ROWS, seq), lambda b: (b, 0, 0)),
                  pl.BlockSpec((8, 2 * gw), lambda b: (0, 0)),
                  pl.BlockSpec((8, gw), lambda b: (0, 0))],
        out_specs=pl.BlockSpec((seq, gw), lambda b: (b, 0)),
        out_shape=jax.ShapeDtypeStruct((batch * seq, gw), BF16),
        scratch_shapes=[pltpu.VMEM((seq, 2 * gw), F32)],
        compiler_params=_cparams("parallel"),
        name="ssd_mixer",
    )(pc, col, rows, conv_wb, par)


def _attn_kernel(*refs, fox, tq):
    if fox:
        tref_ref, q_ref, k_ref, vt_ref, o_ref = refs
    else:
        q_ref, k_ref, vt_ref, o_ref = refs
        tref_ref = None
    b = pl.program_id(0)
    i = pl.program_id(1)
    key = _row_iota((tq, tq))
    qry = _lane_iota((tq, tq))
    if fox:
        allowed = key <= qry
    else:
        shift = int(math.log2(MLA_CHUNK))
        allowed = (key >> shift) <= (qry >> shift)
    qs = [q_ref[:, h * LANES:(h + 1) * LANES] for h in range(N_HEADS)]
    ones_rows = jnp.ones((16, tq), BF16)

    def step(j, masked, carry):
        rk = pl.ds(pl.multiple_of(j * tq, tq), tq)
        scores = [lax.dot_general(k_ref[rk, h * LANES:(h + 1) * LANES], qs[h], (((1,), (1,)), ((), ())),
                                  preferred_element_type=F32) for h in range(N_HEADS)]
        probs = []
        for h in range(N_HEADS):
            m, l, _ = carry[h]
            s = scores[h]
            if masked:
                s = jnp.where(allowed, s, -1e30)
            delta = (tref_ref[b, i, h] - tref_ref[b, j, h]) if fox else 0.0
            m_new = jnp.maximum(m, jnp.max(s, axis=0, keepdims=True) + delta)
            alpha = jnp.exp2(m - m_new)
            p = jnp.exp2(s - (m_new - delta))
            probs.append((m_new, alpha, p.astype(BF16)))
        new = []
        for h in range(N_HEADS):
            pair = h // 2
            m_new, alpha, p = probs[h]
            lhs = jnp.concatenate([vt_ref[pair * LANES:(pair + 1) * LANES, rk], ones_rows], axis=0)
            pv = jnp.dot(lhs, p, preferred_element_type=F32)
            new.append((m_new, alpha * carry[h][1] + pv[LANES:LANES + 1], alpha * carry[h][2] + pv[0:LANES]))
        return tuple(new)

    init = tuple((jnp.full((1, tq), -1e30, F32), jnp.zeros((1, tq), F32), jnp.zeros((LANES, tq), F32))
                 for _ in range(N_HEADS))
    carry = lax.fori_loop(0, i, lambda j, c: step(j, False, c), init)
    carry = step(i, True, carry)
    outs = [acc / l for (_, l, acc) in carry]
    top = _row_iota((LANES, tq)) < HEAD_DIM
    o_t = jnp.concatenate([jnp.where(top, outs[0], outs[1]), jnp.where(top, outs[2], outs[3])], axis=0)
    o_ref[...] = o_t.T.astype(o_ref.dtype)


def _attention(q, k, vt, tref, batch, seq, tq, name):
    nq = seq // tq
    fox = tref is not None
    kern = functools.partial(_attn_kernel, fox=fox, tq=tq)
    grid_spec = pltpu.PrefetchScalarGridSpec(
        num_scalar_prefetch=1 if fox else 0,
        grid=(batch, nq),
        in_specs=[pl.BlockSpec((tq, HEAD_PAD), lambda b, i, *_: (b * nq + i, 0)),
                  pl.BlockSpec((seq, HEAD_PAD), lambda b, i, *_: (b, 0)),
                  pl.BlockSpec((GROUP_WIDTH, seq), lambda b, i, *_: (0, b))],
        out_specs=pl.BlockSpec((tq, GROUP_WIDTH), lambda b, i, *_: (b * nq + i, 0)),
    )
    args = ((tref,) if fox else ()) + (q, k, vt)
    return pl.pallas_call(
        kern,
        grid_spec=grid_spec,
        out_shape=jax.ShapeDtypeStruct((batch * seq, GROUP_WIDTH), BF16),
        compiler_params=_cparams("parallel", "arbitrary"),
        name=name,
    )(*args)


def _outproj_kernel(x_ref, ya, yb, yc, yd, w_ref, g_ref, wr_ref, br_ref,
                    x2_ref, h2_ref, rrow_ref, cnt_ref, *, tm, moe_tile):
    y = jnp.concatenate([ya[...], yb[...], yc[...], yd[...]], axis=1)
    x2 = x_ref[...] + jnp.dot(y, w_ref[...], preferred_element_type=F32)
    x2_ref[...] = x2
    h2 = _rms(x2, g_ref[...])
    h2_ref[...] = h2.astype(h2_ref.dtype)
    h_hi = h2.astype(BF16)
    h_lo = (h2 - h_hi.astype(F32)).astype(BF16)
    part = jnp.dot(h_hi, wr_ref[...], preferred_element_type=F32)
    logits = (part[:, 0:LANES] + part[:, LANES:2 * LANES]
              + jnp.dot(h_lo, wr_ref[:, 0:LANES], preferred_element_type=F32) + br_ref[...])
    lt = logits.T[0:ROUTER_ROWS]
    row = _row_iota(lt.shape)
    neg = -1e30
    big = 1 << 20
    gmask = (row >= N_EXPERTS) & (row < N_EXPERTS + N_EXPERT_GROUPS)
    gl = jnp.where(gmask, lt, neg)
    gmax = jnp.max(gl, axis=0, keepdims=True)
    gsum = jnp.sum(jnp.where(gmask, jnp.exp(gl - gmax), 0.0), axis=0, keepdims=True)
    g_w = 1.0 / gsum
    g_idx = jnp.min(jnp.where(gmask & (gl == gmax), row, big), axis=0, keepdims=True) - N_EXPERTS
    emask = (row < N_EXPERTS) & ((row >> int(math.log2(EXPERTS_PER_GROUP))) == g_idx)
    el = jnp.where(emask, lt, neg)
    e1v = jnp.max(el, axis=0, keepdims=True)
    esum = jnp.sum(jnp.where(emask, jnp.exp(el - e1v), 0.0), axis=0, keepdims=True)
    i1 = jnp.min(jnp.where(emask & (el == e1v), row, big), axis=0, keepdims=True)
    el2 = jnp.where(row == i1, neg, el)
    e2v = jnp.max(el2, axis=0, keepdims=True)
    i2 = jnp.min(jnp.where(emask & (row != i1) & (el2 == e2v), row, big), axis=0, keepdims=True)
    p1 = 1.0 / esum
    p2 = jnp.exp(e2v - e1v) / esum
    w1 = g_w * (p1 / (p1 + p2))
    w2 = g_w * (p2 / (p1 + p2))
    out_row = _row_iota(rrow_ref.shape)
    rrow_ref[...] = jnp.where(out_row == 0, i1.astype(F32),
                              jnp.where(out_row == 1, i2.astype(F32),
                                        jnp.where(out_row == 2, w1, jnp.where(out_row == 3, w2, 0.0))))
    step = pl.program_id(0)

    @pl.when(step == 0)
    def _():
        cnt_ref[...] = jnp.zeros_like(cnt_ref)

    chosen = jnp.where((row == i1) | (row == i2), 1.0, 0.0).astype(BF16)
    tiles = tm // moe_tile
    tile_of = (_row_iota((tm, LANES)) >> int(math.log2(moe_tile))) + step * tiles
    to_tile = jnp.where(_lane_iota((tm, LANES)) == tile_of, 1.0, 0.0).astype(BF16)
    counts = jnp.dot(chosen, to_tile, preferred_element_type=F32)
    cnt_ref[...] += counts[0:N_EXPERTS]


def _outproj(x, ya, yb, yc, yd, w, g, wr, br, tm, moe_tile):
    t = x.shape[0]
    full = lambda a: pl.BlockSpec(a.shape, lambda i: (0, 0))
    tile = lambda wd: pl.BlockSpec((tm, wd), lambda i: (i, 0))
    return pl.pallas_call(
        functools.partial(_outproj_kernel, tm=tm, moe_tile=moe_tile),
        grid=(t // tm,),
        in_specs=[tile(D_MODEL)] + [tile(GROUP_WIDTH)] * 4 + [full(w), full(g), full(wr), full(br)],
        out_specs=[tile(D_MODEL), tile(D_MODEL), pl.BlockSpec((8, tm), lambda i: (0, i)),
                   pl.BlockSpec((N_EXPERTS, LANES), lambda i: (0, 0))],
        out_shape=[jax.ShapeDtypeStruct((t, D_MODEL), F32), jax.ShapeDtypeStruct((t, D_MODEL), BF16),
                   jax.ShapeDtypeStruct((8, t), F32), jax.ShapeDtypeStruct((N_EXPERTS, LANES), F32)],
        compiler_params=_cparams("arbitrary"),
        name="outproj_router",
    )(x, ya, yb, yc, yd, w, g, wr, br)


MOE_TILE = 256
CHUNK = 8
LOCAL_ROWS = 2 * MOE_TILE + 256
PACKED = D_MODEL // 2
XS_WIDTH = PACKED
U32 = jnp.uint32


def _pack_bf16_pairs(x, exact=False):
    if not exact:
        x = x.astype(BF16).astype(F32)
    half = x.shape[1] // 2
    lo = lax.bitcast_convert_type(x[:, :half], U32)
    hi = lax.bitcast_convert_type(x[:, half:], U32)
    return hi | (lo >> 16)


def _unpack_bf16_pairs(words):
    lo = lax.bitcast_convert_type(words << 16, F32)
    hi = lax.bitcast_convert_type(words & U32(0xFFFF0000), F32)
    return jnp.concatenate([lo, hi], axis=1).astype(BF16)


COPY_ROWS = (2 * CHUNK, CHUNK)


def _route_kernel(r_ref, cnt_ref, lrow_ref, lcol_ref, tab_ref, ctab_ref, meta_ref,
                  loff_ref, goff_ref, n8_ref, *, tm, nbp, tiles):
    i = pl.program_id(0)
    hi = lax.Precision.HIGHEST

    @pl.when(i == 0)
    def _():
        cnt = cnt_ref[...]
        n8 = jnp.floor((cnt + (CHUNK - 1)) * (1.0 / CHUNK)) * CHUNK
        er = _row_iota((N_EXPERTS, N_EXPERTS))
        ec = _lane_iota((N_EXPERTS, N_EXPERTS))
        below = jnp.where(er > ec, 1.0, 0.0)
        loff = jnp.dot(below, n8, preferred_element_type=F32, precision=hi)
        rows_e = jnp.sum(n8, axis=-1, keepdims=True) + jnp.zeros_like(n8)
        padded = jnp.floor((rows_e + (MOE_ROWS - 1)) * (1.0 / MOE_ROWS)) * MOE_ROWS
        e_start = jnp.dot(below, padded, preferred_element_type=F32, precision=hi)
        tr = _row_iota((LANES, LANES))
        tc = _lane_iota((LANES, LANES))
        earlier = jnp.where(tr < tc, 1.0, 0.0)
        goff = e_start + jnp.dot(n8, earlier, preferred_element_type=F32, precision=hi)
        loff_ref[...] = loff
        goff_ref[...] = goff
        n8_ref[...] = n8
        big = jnp.floor(n8 * (0.5 / CHUNK))
        small = n8 * (1.0 / CHUNK) - 2.0 * big
        row_t = _row_iota(tab_ref.shape)
        tab_ref[...] = jnp.where(row_t == 0, jnp.sum(big, axis=0, keepdims=True),
                                 jnp.where(row_t == 1, jnp.sum(small, axis=0, keepdims=True), 0.0)).astype(I32)
        reps = nbp // LANES
        pend_b = jnp.concatenate([e_start + padded] * reps, axis=1)
        vend_b = jnp.concatenate([e_start + rows_e] * reps, axis=1)
        used_b = jnp.concatenate([padded] * reps, axis=1) > 0.0
        b0 = (_lane_iota((N_EXPERTS, nbp)) * MOE_ROWS).astype(F32)
        bexp = jnp.sum(jnp.where(pend_b <= b0, 1.0, 0.0), axis=0, keepdims=True)
        bexp = jnp.minimum(bexp, N_EXPERTS - 1.0)
        e_b = _row_iota((N_EXPERTS, nbp)).astype(F32)
        is_e = e_b == bexp
        vend = jnp.sum(jnp.where(is_e, vend_b, 0.0), axis=0, keepdims=True)
        nvalid = jnp.clip(vend - b0[0:1], 0.0, float(MOE_ROWS))
        total = jnp.max(pend_b, axis=0, keepdims=True) * (1.0 / MOE_ROWS)
        order = jnp.sum(jnp.where(used_b & (e_b < bexp), 1.0, 0.0), axis=0, keepdims=True)
        nxt = jnp.min(jnp.where(used_b & (e_b > bexp), e_b, float(N_EXPERTS)), axis=0, keepdims=True)
        row = _row_iota((8, nbp))
        meta = jnp.where(row == 0, bexp, jnp.where(row == 1, nvalid, jnp.where(row == 2, total,
                         jnp.where(row == 3, order, jnp.where(row == 4, nxt, 0.0)))))
        meta_ref[...] = meta.astype(I32)

    su = jnp.where(_row_iota((tm, tm)) < _lane_iota((tm, tm)), 1.0, 0.0).astype(BF16)
    incl = jnp.where(_row_iota((N_EXPERTS, N_EXPERTS)) >= _lane_iota((N_EXPERTS, N_EXPERTS)), 1.0, 0.0)
    cidx = _lane_iota((N_EXPERTS, LANES)).astype(F32)
    e_iota = _row_iota((N_EXPERTS, tm))
    out_row = _row_iota((LANES, tm))
    for k in range(tiles):
        tok = slice(k * tm, (k + 1) * tm)
        oh0 = e_iota == r_ref[0:1, tok].astype(I32)
        oh1 = e_iota == r_ref[1:2, tok].astype(I32)
        oh = jnp.where(oh0 | oh1, 1.0, 0.0)
        tile_lane = _lane_iota((N_EXPERTS, LANES)) == i * tiles + k
        before = jnp.dot(oh.astype(BF16), su, preferred_element_type=F32)
        base = jnp.sum(jnp.where(tile_lane, loff_ref[...], 0.0), axis=-1, keepdims=True) + before
        d0 = jnp.sum(jnp.where(oh0, base, 0.0), axis=0, keepdims=True)
        d1 = jnp.sum(jnp.where(oh1, base, 0.0), axis=0, keepdims=True)
        lrow_ref[k, 0:1, :] = d0.astype(I32)
        lrow_ref[k, 1:2, :] = d1.astype(I32)
        pick_tile = lambda ref: jnp.sum(jnp.where(tile_lane, ref[...], 0.0), axis=-1, keepdims=True)
        nch = pick_tile(n8_ref) * (1.0 / CHUNK)
        n_big = jnp.floor(nch * 0.5)
        n_small = nch - 2.0 * n_big
        loff_t = pick_tile(loff_ref)
        goff_t = pick_tile(goff_ref)
        for c, (n, rows, first) in enumerate(((n_big, COPY_ROWS[0], 0.0),
                                              (n_small, COPY_ROWS[1], n_big * COPY_ROWS[0]))):
            cend = jnp.dot(incl, n + jnp.zeros((N_EXPERTS, LANES), F32), preferred_element_type=F32, precision=hi)
            cstart = cend - n
            mine = (cidx >= cstart) & (cidx < cend)
            step_rows = first + (cidx - cstart) * rows
            ctab_ref[k, 2 * c:2 * c + 1, :] = jnp.sum(jnp.where(mine, loff_t + step_rows, 0.0), axis=0,
                                                      keepdims=True).astype(I32)
            ctab_ref[k, 2 * c + 1:2 * c + 2, :] = jnp.sum(jnp.where(mine, goff_t + step_rows, 0.0), axis=0,
                                                          keepdims=True).astype(I32)
        lcol_ref[tok, :] = jnp.where(out_row == 0, d0, jnp.where(out_row == 1, d1,
                                     jnp.where(out_row == 2, r_ref[2:3, tok],
                                               jnp.where(out_row == 3, r_ref[3:4, tok], 0.0)))).T


def _route(rrow, cnt, tm, nbp):
    t = rrow.shape[1]
    nt = t // tm
    tiles = next(r for r in (4, 2, 1) if nt % r == 0)
    kern = functools.partial(_route_kernel, tm=tm, nbp=nbp, tiles=tiles)
    return pl.pallas_call(
        kern,
        grid=(nt // tiles,),
        in_specs=[pl.BlockSpec((8, tiles * tm), lambda i: (0, i)),
                  pl.BlockSpec((N_EXPERTS, LANES), lambda i: (0, 0))],
        out_specs=[pl.BlockSpec((tiles, 2, tm), lambda i: (i, 0, 0)),
                   pl.BlockSpec((tiles * tm, LANES), lambda i: (i, 0)),
                   pl.BlockSpec((8, LANES), lambda i: (0, 0)),
                   pl.BlockSpec((tiles, 2 * len(COPY_ROWS), LANES), lambda i: (i, 0, 0)),
                   pl.BlockSpec((8, nbp), lambda i: (0, 0))],
        out_shape=[jax.ShapeDtypeStruct((nt, 2, tm), I32), jax.ShapeDtypeStruct((t, LANES), F32),
                   jax.ShapeDtypeStruct((8, LANES), I32), jax.ShapeDtypeStruct((nt, 2 * len(COPY_ROWS), LANES), I32),
                   jax.ShapeDtypeStruct((8, nbp), I32)],
        scratch_shapes=[pltpu.VMEM((N_EXPERTS, LANES), F32)] * 3,
        compiler_params=_cparams("arbitrary"),
        name="moe_route",
    )(rrow, cnt)


def _chunk_copies(tabs, i, local_ref, global_ref, sem, to_global, action):
    tab_ref, ctab_ref = tabs
    for k, rows in enumerate(COPY_ROWS):
        count = tab_ref[k, i]

        def copy(lo, go, rows=rows):
            lsl = local_ref.at[pl.ds(pl.multiple_of(lo, CHUNK), rows)]
            gsl = global_ref.at[pl.ds(pl.multiple_of(go, CHUNK), rows)]
            return pltpu.make_async_copy(lsl, gsl, sem) if to_global else pltpu.make_async_copy(gsl, lsl, sem)

        if action == "wait":
            def one(c, c1, copy=copy):
                copy(0, 0).wait()
                return c1
        else:
            def one(c, c1, copy=copy, k=k):
                copy(ctab_ref[i, 2 * k, c], ctab_ref[i, 2 * k + 1, c]).start()
                return c1

        lax.fori_loop(0, count, one, 0)


def _scatter_kernel(tab_ref, ctab_ref, meta_ref, lrow_ref, h_ref, xs_ref, buf_ref, zero_ref, sem, zsem, *, tm, nb):
    i = pl.program_id(0)
    tabs = (tab_ref, ctab_ref)

    @pl.when(i == 0)
    def _():
        zero_ref[...] = jnp.zeros_like(zero_ref)
        n_used = meta_ref[2, 0]

        def zcopy(b):
            sub = lax.shift_right_logical(meta_ref[1, b], int(math.log2(FFN_SUB)))
            start = pl.multiple_of(b * MOE_ROWS + sub * FFN_SUB, FFN_SUB)
            return pltpu.make_async_copy(zero_ref, xs_ref.at[pl.ds(start, FFN_SUB)], zsem)

        def needs(b):
            return (b < n_used) & ((meta_ref[1, b] & (FFN_SUB - 1)) != 0)

        def start(b, c):
            @pl.when(needs(b))
            def _():
                zcopy(b).start()
            return c

        def wait(b, c):
            @pl.when(needs(b))
            def _():
                zcopy(b).wait()
            return c

        lax.fori_loop(0, nb, start, 0)
        lax.fori_loop(0, nb, wait, 0)

    rows = _row_iota((LOCAL_ROWS, tm))
    p0 = rows == lrow_ref[0, 0:1, :]
    p1 = rows == lrow_ref[0, 1:2, :]
    perm = jnp.where(p0 | p1, 1.0, 0.0).astype(BF16)
    sorted_rows = jnp.dot(perm, h_ref[...], preferred_element_type=F32)

    def fill(slot):
        buf = buf_ref.at[slot]
        buf[...] = _pack_bf16_pairs(sorted_rows, exact=True)
        _chunk_copies(tabs, i, buf, xs_ref, sem.at[slot], True, "start")

    def drain(tile, slot):
        _chunk_copies(tabs, tile, buf_ref.at[slot], xs_ref, sem.at[slot], True, "wait")

    even = (i & 1) == 0

    @pl.when(even)
    def _():
        fill(0)

    @pl.when(jnp.logical_not(even))
    def _():
        fill(1)

    @pl.when((i > 0) & even)
    def _():
        drain(i - 1, 1)

    @pl.when((i > 0) & jnp.logical_not(even))
    def _():
        drain(i - 1, 0)

    @pl.when((i == pl.num_programs(0) - 1) & even)
    def _():
        drain(i, 0)

    @pl.when((i == pl.num_programs(0) - 1) & jnp.logical_not(even))
    def _():
        drain(i, 1)


def _scatter(tab, ctab, meta, lrow, h2, tm, nb):
    t = h2.shape[0]
    kern = functools.partial(_scatter_kernel, tm=tm, nb=nb)
    grid_spec = pltpu.PrefetchScalarGridSpec(
        num_scalar_prefetch=3,
        grid=(t // tm,),
        in_specs=[pl.BlockSpec((1, 2, tm), lambda i, *_: (i, 0, 0)),
                  pl.BlockSpec((tm, D_MODEL), lambda i, *_: (i, 0))],
        out_specs=pl.BlockSpec(memory_space=pl.ANY),
        scratch_shapes=[pltpu.VMEM((2, LOCAL_ROWS, XS_WIDTH), U32), pltpu.VMEM((FFN_SUB, XS_WIDTH), U32),
                        pltpu.SemaphoreType.DMA((2,)), pltpu.SemaphoreType.DMA],
    )
    return pl.pallas_call(
        kern,
        grid_spec=grid_spec,
        out_shape=jax.ShapeDtypeStruct((nb * MOE_ROWS, XS_WIDTH), U32),
        compiler_params=_cparams("arbitrary"),
        name="moe_scatter",
    )(tab, ctab, meta, lrow, h2)


def _ffn_kernel(meta_ref, x_ref, wg_ref, wu_ref, wd_ref, o_ref, wgu_b, wd_b, wg_f, wu_f, wd_f, wsem, *, layer):
    b = pl.program_id(0)
    live = b < meta_ref[2, 0]
    expert = meta_ref[0, b]
    prev = meta_ref[0, jnp.maximum(b - 1, 0)]

    def fetch(e, slot):
        return [pltpu.make_async_copy(src.at[layer, e], dst.at[slot], wsem.at[slot])
                for src, dst in ((wg_ref, wg_f), (wu_ref, wu_f), (wd_ref, wd_f))]

    def first_block(slot):
        @pl.when(b == 0)
        def _():
            for cp in fetch(expert, slot):
                cp.start()

        for cp in fetch(expert, slot):
            cp.wait()
        wgu_b[:, 0:EXPERT_FF] = wg_f[slot].astype(BF16)
        wgu_b[:, EXPERT_FF:2 * EXPERT_FF] = wu_f[slot].astype(BF16)
        wd_b[...] = wd_f[slot].astype(BF16)
        nxt = meta_ref[4, b]

        @pl.when(nxt < N_EXPERTS)
        def _():
            for cp in fetch(nxt, 1 - slot):
                cp.start()

    changed = live & ((b == 0) | (expert != prev))
    odd = (meta_ref[3, b] & 1) == 1

    @pl.when(changed & jnp.logical_not(odd))
    def _():
        first_block(0)

    @pl.when(changed & odd)
    def _():
        first_block(1)

    nvalid = meta_ref[1, jnp.maximum(jnp.minimum(b, meta_ref[2, 0] - 1), 0)]
    for sub in range(MOE_ROWS // FFN_SUB):
        rows = slice(sub * FFN_SUB, (sub + 1) * FFN_SUB)
        used = live & (nvalid > sub * FFN_SUB)

        @pl.when(used)
        def _():
            x = _unpack_bf16_pairs(x_ref[rows, 0:PACKED])
            gu = jnp.dot(x, wgu_b[...], preferred_element_type=F32)
            act = (_silu(gu[:, 0:EXPERT_FF]) * gu[:, EXPERT_FF:2 * EXPERT_FF]).astype(BF16)
            y = jnp.dot(act, wd_b[...], preferred_element_type=F32)
            o_ref[rows, :] = _pack_bf16_pairs(y)

        @pl.when(live & jnp.logical_not(used))
        def _():
            o_ref[rows, :] = jnp.zeros((FFN_SUB, PACKED), U32)


def _ffn(meta, xs, wg, wu, wd, layer, nb):
    def blk(b, m):
        return (jnp.maximum(jnp.minimum(b, m[2, 0] - 1), 0), 0)

    grid_spec = pltpu.PrefetchScalarGridSpec(
        num_scalar_prefetch=1,
        grid=(nb,),
        in_specs=[pl.BlockSpec((MOE_ROWS, XS_WIDTH), blk)] + [pl.BlockSpec(memory_space=pl.ANY)] * 3,
        out_specs=pl.BlockSpec((MOE_ROWS, PACKED), blk),
        scratch_shapes=[pltpu.VMEM((D_MODEL, 2 * EXPERT_FF), BF16), pltpu.VMEM((EXPERT_FF, D_MODEL), BF16),
                        pltpu.VMEM((2, D_MODEL, EXPERT_FF), F32), pltpu.VMEM((2, D_MODEL, EXPERT_FF), F32),
                        pltpu.VMEM((2, EXPERT_FF, D_MODEL), F32), pltpu.SemaphoreType.DMA((2,))],
    )
    return pl.pallas_call(
        functools.partial(_ffn_kernel, layer=layer),
        grid_spec=grid_spec,
        out_shape=jax.ShapeDtypeStruct((nb * MOE_ROWS, PACKED), U32),
        compiler_params=_cparams("arbitrary"),
        name="moe_experts",
    )(meta, xs, wg, wu, wd)


def _gather_kernel(tab_ref, ctab_ref, lcol_ref, x_ref, g_ref, ys_ref, o_ref, buf_ref, sem, *, tm, final):
    i = pl.program_id(0)

    last = pl.num_programs(0) - 1

    def fetch(tile, slot, action):
        _chunk_copies((tab_ref, ctab_ref), tile, buf_ref.at[slot], ys_ref, sem.at[slot], False, action)

    @pl.when(i == 0)
    def _():
        buf_ref[...] = jnp.zeros_like(buf_ref)
        fetch(0, 0, "start")

    even = (i & 1) == 0

    @pl.when((i < last) & even)
    def _():
        fetch(i + 1, 1, "start")

    @pl.when((i < last) & jnp.logical_not(even))
    def _():
        fetch(i + 1, 0, "start")

    col = _lane_iota((tm, LOCAL_ROWS)).astype(F32)
    pick0 = jnp.where(col == lcol_ref[:, 0:1], 1.0, 0.0).astype(BF16)
    pick1 = jnp.where(col == lcol_ref[:, 1:2], 1.0, 0.0).astype(BF16)

    def finish(slot):
        fetch(i, slot, "wait")
        y = _unpack_bf16_pairs(buf_ref[slot])
        both = jnp.dot(jnp.concatenate([pick0, pick1], axis=0), y, preferred_element_type=F32)
        x = x_ref[...] + lcol_ref[:, 2:3] * both[0:tm] + lcol_ref[:, 3:4] * both[tm:2 * tm]
        o_ref[...] = _rms(x, g_ref[...]) if final else x

    @pl.when(even)
    def _():
        finish(0)

    @pl.when(jnp.logical_not(even))
    def _():
        finish(1)


def _gather(tab, ctab, lcol, x2, g, ys, tm, final):
    t = x2.shape[0]
    kern = functools.partial(_gather_kernel, tm=tm, final=final)
    grid_spec = pltpu.PrefetchScalarGridSpec(
        num_scalar_prefetch=2,
        grid=(t // tm,),
        in_specs=[pl.BlockSpec((tm, LANES), lambda i, *_: (i, 0)),
                  pl.BlockSpec((tm, D_MODEL), lambda i, *_: (i, 0)),
                  pl.BlockSpec((1, D_MODEL), lambda i, *_: (0, 0)),
                  pl.BlockSpec(memory_space=pl.ANY)],
        out_specs=pl.BlockSpec((tm, D_MODEL), lambda i, *_: (i, 0)),
        scratch_shapes=[pltpu.VMEM((2, LOCAL_ROWS, PACKED), U32), pltpu.SemaphoreType.DMA((2,))],
    )
    return pl.pallas_call(
        kern,
        grid_spec=grid_spec,
        out_shape=jax.ShapeDtypeStruct((t, D_MODEL), F32),
        compiler_params=_cparams("arbitrary"),
        name="moe_combine",
    )(tab, ctab, lcol, x2, g, ys)


def _pad_rows(a, rows=8):
    return jnp.zeros((rows, a.shape[-1]), F32).at[:a.shape[0]].set(a.astype(F32))


def _arrange_mla(w_uq, w_ukv):
    half = MLA_ROPE // 2
    qd = MLA_NOPE + MLA_ROPE
    wq, wqs, wk, wv = [], [], [], []
    zq = jnp.zeros((MLA_Q_LORA, LANES - qd), w_uq.dtype)
    zk = jnp.zeros((MLA_KV_LORA, LANES - MLA_NOPE), w_ukv.dtype)
    for h in range(N_HEADS):
        q = w_uq[:, h * qd:(h + 1) * qd]
        nope, rope = q[:, :MLA_NOPE], q[:, MLA_NOPE:]
        wq.append(jnp.concatenate([nope, rope, zq], axis=1))
        wqs.append(jnp.concatenate([jnp.zeros_like(nope), -rope[:, half:], rope[:, :half], zq], axis=1))
        kv = w_ukv[:, h * 2 * MLA_NOPE:(h + 1) * 2 * MLA_NOPE]
        wk.append(jnp.concatenate([kv[:, :MLA_NOPE], zk], axis=1))
        wv.append(kv[:, MLA_NOPE:])
    cat = lambda xs: jnp.concatenate(xs, axis=1).astype(BF16)
    return cat(wq), cat(wqs), cat(wk), cat(wv).T


def kernel(x, positions, norm_mix, w_in, conv_a, fox_forget_bias, ssm_conv_w, ssm_conv_b, ssm_dt_bias,
           ssm_a_log, ssm_d, ssm_norm, mla_q_norm, mla_kv_norm, mla_w_uq, mla_w_ukv, w_out, norm_ffn,
           router_group_w, router_group_b, router_expert_w, router_expert_b, expert_w_gate, expert_w_up,
           expert_w_down, norm_final):
    batch, seq, d = x.shape
    t = batch * seq
    depth = w_in.shape[0]
    tm = min(512, t)
    tq = min(ATTN_TQ, seq)
    tmd = min(MOE_TILE, t)
    max_rows = 2 * t + (CHUNK - 1) * N_EXPERTS * (t // tmd) + N_EXPERTS * (MOE_ROWS - 1)
    nb = -(-max_rows // MOE_ROWS)
    nbp = -(-nb // LANES) * LANES

    xf = x.reshape(t, d)
    pos_col = positions.astype(F32).reshape(t, 1)
    cos, sin = _rope_tables(pos_col, tm)
    w_in_t = jnp.swapaxes(w_in, 1, 2)

    for l in range(depth):
        wq, wqs, wk, wv = _arrange_mla(mla_w_uq[l], mla_w_ukv[l])
        ya, pb, pc, misc, fox_vt, q, k, v = _inproj(
            xf, norm_mix[l][None, :], w_in_t, l, _pad_rows(conv_a[l]), cos, sin, mla_q_norm[l][None, :],
            mla_kv_norm[l][None, :], jnp.concatenate([wq, wqs], axis=1), wk, wv, tm, seq)

        sp = jnp.zeros((8, LANES), F32)
        sp = sp.at[0, MISC_F:MISC_F + N_HEADS].set(fox_forget_bias[l])
        sp = sp.at[0, MISC_DT:MISC_DT + N_HEADS].set(ssm_dt_bias[l])
        sp = sp.at[1, MISC_DT:MISC_DT + N_HEADS].set(ssm_a_log[l])
        col, rows, fox_q, fox_k, tref = _scalar_prep(misc, sp, pb, batch, seq, tq)

        yb = _attention(fox_q, fox_k, fox_vt, tref, batch, seq, tq, "fox_attention")
        conv_wb = _pad_rows(jnp.concatenate([ssm_conv_w[l], ssm_conv_b[l][None, :]], axis=0))
        ssd_par = _pad_rows(jnp.stack([jnp.repeat(ssm_d[l], HEAD_DIM), ssm_norm[l]]))
        yc = _ssd_mixer(pc, col, rows, conv_wb, ssd_par, batch, seq)
        yd = _attention(q, k, v, None, batch, seq, tq, "mla_attention")

        pad = jnp.zeros((d, LANES - N_EXPERTS - N_EXPERT_GROUPS), F32)
        wr = jnp.concatenate([router_expert_w[l], router_group_w[l], pad], axis=1)
        wr_hi = wr.astype(BF16)
        wr = jnp.concatenate([wr_hi, (wr - wr_hi.astype(F32)).astype(BF16)], axis=1)
        br = jnp.concatenate([router_expert_b[l], router_group_b[l], pad[0]])[None, :]
        x2, h2, rrow, cnt = _outproj(xf, ya, yb, yc, yd, w_out[l].astype(BF16), norm_ffn[l][None, :], wr, br,
                                     tm, tmd)

        lrow, lcol, tab, ctab, meta = _route(rrow, cnt, tmd, nbp)
        xs = _scatter(tab, ctab, meta, lrow, h2, tmd, nb)
        ys = _ffn(meta, xs, expert_w_gate, expert_w_up, expert_w_down, l, nb)
        final = l == depth - 1
        xf = _gather(tab, ctab, lcol, x2, norm_final[None, :], ys, tmd, final)

    return xf.reshape(batch, seq, d)
```

```python
import functools
import math

import jax
import jax.numpy as jnp
import numpy as np
from jax import lax
from jax.experimental import pallas as pl
from jax.experimental.pallas import tpu as pltpu

F32 = jnp.float32
BF16 = jnp.bfloat16
I32 = jnp.int32

LANES = 128
VMEM_LIMIT_BYTES = 56 * 1024 * 1024

D_MODEL = 1024
RMS_EPS = 1e-6
LOG2E = math.log2(math.e)
GROUP_WIDTH = 256
HEAD_DIM = 64
N_HEADS = 4

CONV_A_WIDTH = 3
SSM_CONV = 4
SSM_STATE = 64
SSM_CHUNK = 256

MLA_NOPE = 64
MLA_ROPE = 32
MLA_Q_LORA = 256
MLA_KV_LORA = 128
ROPE_BASE = 10000.0
MLA_CHUNK = 64
ATTN_TQ = 512

N_EXPERT_GROUPS = 4
EXPERTS_PER_GROUP = 8
N_EXPERTS = 32
EXPERT_FF = 256
ROUTER_ROWS = 48
MOE_ROWS = 512
FFN_SUB = 256

SEG_A = (0, 768)
SEG_B = (768, 1280)
SEG_C = (1280, 2048)
SEG_D = (2048, 2432)
SEG_M = (2432, 2560)
SEG_M2 = (2560, 2688)
IN_COLS_PADDED = 2688
HEAD_PAD = N_HEADS * LANES
AUG_LANE = HEAD_DIM
MISC_F = 0
MISC_DT = 4
MISC_ROPE = 64
COL_CUMF = 0
COL_DT = 4
COL_ACUM = 8
N_SCALAR_ROWS = 16


def _cparams(*sem):
    return pltpu.CompilerParams(dimension_semantics=sem, vmem_limit_bytes=VMEM_LIMIT_BYTES)


def _lane_iota(shape):
    return lax.broadcasted_iota(I32, shape, len(shape) - 1)


def _row_iota(shape):
    return lax.broadcasted_iota(I32, shape, 0)


def _rms(x, g):
    ms = jnp.mean(x * x, axis=-1, keepdims=True)
    return x * lax.rsqrt(ms + RMS_EPS) * g


def _silu(x):
    return x / (1.0 + jnp.exp(-x))


def _softplus(x):
    return jnp.maximum(x, 0.0) + jnp.log(1.0 + jnp.exp(-jnp.abs(x)))


def _shift_rows(x, k):
    rolled = pltpu.roll(x, k, 0)
    return jnp.where(_row_iota(x.shape) >= k, rolled, 0.0)


def _rope_kernel(pos_ref, freq_ref, cos_ref, sin_ref):
    q = pos_ref.shape[2]
    lane = _lane_iota((q, LANES))
    group = lane >> int(math.log2(MLA_ROPE))
    rows = jnp.concatenate([pos_ref[0], jnp.zeros((8 - ROPE_PACK, q), F32)], axis=0)
    cols = jnp.concatenate([rows] * (LANES // 8), axis=0).T
    pos = cols[:, ROPE_PACK - 1:ROPE_PACK]
    for k in range(ROPE_PACK - 2, -1, -1):
        pos = jnp.where(group == k, cols[:, k:k + 1], pos)
    ang = pos * freq_ref[...]
    c = jnp.cos(ang)
    s = jnp.sin(ang)
    rope = (lane >= MISC_ROPE) & (lane < MISC_ROPE + MLA_ROPE)
    for k in range(ROPE_PACK):
        shift = (MISC_ROPE - MLA_ROPE * k) % LANES
        ck = pltpu.roll(c, shift, 1) if shift else c
        sk = pltpu.roll(s, shift, 1) if shift else s
        cos_ref[k * q:(k + 1) * q, :] = jnp.where(rope, ck, jnp.where(lane < MISC_ROPE, 1.0, 0.0))
        sin_ref[k * q:(k + 1) * q, :] = jnp.where(rope, sk, 0.0)


ROPE_PACK = LANES // MLA_ROPE


def _rope_tables(positions, tm):
    t = positions.size
    pos = positions.astype(F32).reshape(t // tm, ROPE_PACK, tm // ROPE_PACK)
    inv = ROPE_BASE ** (-np.arange(0, MLA_ROPE, 2, dtype=np.float32) / MLA_ROPE)
    freq = np.tile(np.concatenate([inv, inv]), ROPE_PACK)[None, :].astype(np.float32)
    return pl.pallas_call(
        _rope_kernel,
        grid=(t // tm,),
        in_specs=[pl.BlockSpec((1, ROPE_PACK, tm // ROPE_PACK), lambda i: (i, 0, 0)),
                  pl.BlockSpec((1, LANES), lambda i: (0, 0))],
        out_specs=[pl.BlockSpec((tm, LANES), lambda i: (i, 0))] * 2,
        out_shape=[jax.ShapeDtypeStruct((t, LANES), F32)] * 2,
        compiler_params=_cparams("parallel"),
        name="rope_tables",
    )(pos, jnp.asarray(freq))


def _arrange_w_in_kernel(wt_ref, w_ref, wvt_ref):
    gw = GROUP_WIDTH
    rows = lambda lo, hi: wt_ref[0, lo:hi, :]

    def put(seg0, piece_t):
        w_ref[:, seg0:seg0 + piece_t.shape[0]] = piece_t.T.astype(BF16)

    for j in range(3):
        put(SEG_A[0] + j * gw, rows(j * gw, (j + 1) * gw))
    put(SEG_B[0], rows(3 * gw, 4 * gw) * (HEAD_DIM ** -0.5 * LOG2E))
    put(SEG_B[0] + gw, rows(4 * gw, 5 * gw))
    wvt_ref[...] = rows(5 * gw, 6 * gw).astype(BF16)
    c0 = 6 * gw
    win = pltpu.roll(rows(c0, c0 + 3 * gw + 8), 3 * gw + 8 - N_HEADS, 0)
    for j in range(3):
        put(SEG_C[0] + j * gw, win[j * gw:(j + 1) * gw])
    d0 = c0 + 3 * gw + 8
    put(SEG_D[0], rows(d0, d0 + MLA_Q_LORA))
    put(SEG_D[0] + MLA_Q_LORA, rows(d0 + MLA_Q_LORA, d0 + MLA_Q_LORA + MLA_KV_LORA))
    kr0 = d0 + MLA_Q_LORA + MLA_KV_LORA
    half = MLA_ROPE // 2
    first8 = jnp.where(_row_iota((8, D_MODEL)) < N_HEADS, rows(c0, c0 + 8), rows(d0 - 8, d0))
    zeros = lambda n: jnp.zeros((n, D_MODEL), F32)
    misc_t = jnp.concatenate([first8, zeros(MISC_ROPE - 8), rows(kr0, kr0 + MLA_ROPE),
                              zeros(LANES - MISC_ROPE - MLA_ROPE)], axis=0)
    misc2_t = jnp.concatenate([zeros(MISC_ROPE), -rows(kr0 + half, kr0 + MLA_ROPE), rows(kr0, kr0 + half),
                               zeros(LANES - MISC_ROPE - MLA_ROPE)], axis=0)
    put(SEG_M[0], misc_t)
    put(SEG_M2[0], misc2_t)


def _inproj_kernel(x_ref, g_ref, wt_ref, cw_ref, cos_ref, sin_ref, nq_ref, nkv_ref, wq2_ref, wk_ref,
                   wvt2_ref, oa, ob, oc, om, ovt, q_ref, k_ref, vt_ref, w_ref, wvt_ref, halo_ref,
                   *, tm, tiles_per_seq):
    @pl.when(pl.program_id(0) == 0)
    def _():
        _arrange_w_in_kernel(wt_ref, w_ref, wvt_ref)

    h = _rms(x_ref[...], g_ref[...]).astype(BF16)
    for o, (lo, hi) in ((ob, SEG_B), (oc, SEG_C)):
        o[...] = jnp.dot(h, w_ref[:, lo:hi], preferred_element_type=F32).astype(o.dtype)

    gw = GROUP_WIDTH
    pa = jnp.dot(h, w_ref[:, SEG_A[0]:SEG_A[1]], preferred_element_type=F32)
    cv = pa[:, gw:2 * gw] * pa[:, 2 * gw:3 * gw]

    @pl.when(pl.program_id(0) % tiles_per_seq == 0)
    def _():
        halo_ref[...] = jnp.zeros_like(halo_ref)

    halo = halo_ref[...]
    row8 = _row_iota(halo.shape)
    acc = cv * cw_ref[CONV_A_WIDTH - 1:CONV_A_WIDTH, :]
    for k in range(1, CONV_A_WIDTH):
        shifted = pltpu.roll(cv, k, 0)
        top = jnp.where(row8 < k, pltpu.roll(halo, k, 0), shifted[0:8])
        shifted = jnp.concatenate([top, shifted[8:]], axis=0)
        acc = acc + shifted * cw_ref[CONV_A_WIDTH - 1 - k:CONV_A_WIDTH - k, :]
    halo_ref[...] = cv[tm - 8:tm]
    oa[...] = (pa[:, 0:gw] * acc).astype(oa.dtype)
    ovt[...] = lax.dot_general(wvt_ref[...], h, (((1,), (1,)), ((), ())),
                               preferred_element_type=F32).astype(ovt.dtype)
    misc = jnp.dot(h, w_ref[:, SEG_M[0]:SEG_M[1]], preferred_element_type=F32)
    misc2 = jnp.dot(h, w_ref[:, SEG_M2[0]:SEG_M2[1]], preferred_element_type=F32)
    om[...] = misc

    pd = jnp.dot(h, w_ref[:, SEG_D[0]:SEG_D[1]], preferred_element_type=F32)
    cq = _rms(pd[:, 0:MLA_Q_LORA], nq_ref[...]).astype(BF16)
    ckv = _rms(pd[:, MLA_Q_LORA:MLA_Q_LORA + MLA_KV_LORA], nkv_ref[...]).astype(BF16)
    cos = cos_ref[...]
    sin = sin_ref[...]
    cos4 = jnp.concatenate([cos] * N_HEADS, axis=1)
    sin4 = jnp.concatenate([sin] * N_HEADS, axis=1)
    scale = (MLA_NOPE + MLA_ROPE) ** -0.5 * LOG2E
    q2 = jnp.dot(cq, wq2_ref[...], preferred_element_type=F32)
    q_ref[...] = ((q2[:, 0:HEAD_PAD] * cos4 + q2[:, HEAD_PAD:2 * HEAD_PAD] * sin4) * scale).astype(q_ref.dtype)
    lane = _lane_iota(cos.shape)
    rope = (lane >= MISC_ROPE) & (lane < MISC_ROPE + MLA_ROPE)
    kr = jnp.where(rope, misc * cos + misc2 * sin, 0.0)
    k = jnp.dot(ckv, wk_ref[...], preferred_element_type=F32)
    k_ref[...] = (k + jnp.concatenate([kr] * N_HEADS, axis=1)).astype(k_ref.dtype)
    vt_ref[...] = lax.dot_general(wvt2_ref[...], ckv, (((1,), (1,)), ((), ())),
                                  preferred_element_type=F32).astype(vt_ref.dtype)


def _inproj(x, g, w_in_t, layer, conv_w, cos, sin, nq, nkv, wq2, wk, wvt2, tm, seq):
    t = x.shape[0]
    once = lambda a: pl.BlockSpec((1,) + a.shape[1:], lambda i: (layer, 0, 0), pipeline_mode=pl.Buffered(1))
    full = lambda a: pl.BlockSpec(a.shape, lambda i: (0, 0))
    tile = lambda wd: pl.BlockSpec((tm, wd), lambda i: (i, 0))
    cols = lambda: pl.BlockSpec((GROUP_WIDTH, tm), lambda i: (0, i))
    seg = lambda s: s[1] - s[0]
    return pl.pallas_call(
        functools.partial(_inproj_kernel, tm=tm, tiles_per_seq=seq // tm),
        grid=(t // tm,),
        in_specs=[tile(D_MODEL), full(g), once(w_in_t), full(conv_w), tile(LANES), tile(LANES),
                  full(nq), full(nkv), full(wq2), full(wk), full(wvt2)],
        out_specs=[tile(GROUP_WIDTH), tile(seg(SEG_B)), tile(seg(SEG_C)), tile(LANES), cols(),
                   tile(HEAD_PAD), tile(HEAD_PAD), cols()],
        out_shape=[jax.ShapeDtypeStruct((t, GROUP_WIDTH), BF16), jax.ShapeDtypeStruct((t, seg(SEG_B)), BF16),
                   jax.ShapeDtypeStruct((t, seg(SEG_C)), BF16), jax.ShapeDtypeStruct((t, LANES), F32),
                   jax.ShapeDtypeStruct((GROUP_WIDTH, t), BF16),
                   jax.ShapeDtypeStruct((t, HEAD_PAD), BF16), jax.ShapeDtypeStruct((t, HEAD_PAD), BF16),
                   jax.ShapeDtypeStruct((GROUP_WIDTH, t), BF16)],
        scratch_shapes=[pltpu.VMEM((D_MODEL, IN_COLS_PADDED), BF16), pltpu.VMEM((GROUP_WIDTH, D_MODEL), BF16),
                        pltpu.VMEM((8, GROUP_WIDTH), F32)],
        compiler_params=_cparams("arbitrary"),
        name="inproj",
    )(x, g, w_in_t, conv_w, cos, sin, nq, nkv, wq2, wk, wvt2)


def _scalar_prep_kernel(m_ref, p_ref, qk_ref, place_ref, const_ref,
                        col_ref, row_ref, qa_ref, ka_ref, tref_ref, *, tq):
    s = m_ref.shape[0]
    tref_ref[...] = jnp.zeros_like(tref_ref)
    tile_ref = jnp.zeros((1, LANES), F32)
    m = m_ref[...]
    bias = p_ref[0:1, :]
    a_log = p_ref[1:2, :]
    lane = _lane_iota(m.shape)
    z = m + bias
    logf = jnp.minimum(z, 0.0) - jnp.log(1.0 + jnp.exp(-jnp.abs(z)))
    dt = _softplus(z)
    a = dt * (-jnp.exp(a_log))
    is_f = lane < MISC_DT
    is_dt = (lane >= MISC_DT) & (lane < MISC_DT + N_HEADS)
    v = jnp.where(is_f, logf, jnp.where(is_dt, a, 0.0))
    r = _row_iota((SSM_CHUNK, SSM_CHUNK))
    c = _lane_iota((SSM_CHUNK, SSM_CHUNK))
    tril = jnp.where(r >= c, 1.0, 0.0).astype(BF16)
    carry = jnp.zeros((1, LANES), F32)
    lane_1 = _lane_iota((1, LANES))
    lane_b = _lane_iota((SSM_CHUNK, LANES))
    low = lane_b < HEAD_DIM
    aug_lanes = (lane_b >= AUG_LANE) & (lane_b < AUG_LANE + AUG_TERMS)
    for ci in range(s // SSM_CHUNK):
        rest = v[ci * SSM_CHUNK:(ci + 1) * SSM_CHUNK]
        cs = jnp.zeros((SSM_CHUNK, LANES), F32)
        for _ in range(3):
            term = rest.astype(BF16)
            cs = cs + jnp.dot(tril, term, preferred_element_type=F32)
            rest = rest - term.astype(F32)
        cs = cs + jnp.where(lane_1 < MISC_DT, carry, 0.0)
        carry = cs[SSM_CHUNK - 1:SSM_CHUNK]
        acum = pltpu.roll(cs, COL_ACUM - MISC_DT, 1)
        out = jnp.where(lane_b < MISC_DT, cs * LOG2E,
                        jnp.where(lane_b < COL_ACUM, dt[ci * SSM_CHUNK:(ci + 1) * SSM_CHUNK],
                                  jnp.where(lane_b < COL_ACUM + N_HEADS, acum, 0.0)))
        rows = slice(ci * SSM_CHUNK, (ci + 1) * SSM_CHUNK)
        col_ref[rows, :] = out
        row_ref[0, :, rows] = out.T[:N_SCALAR_ROWS]
        if (ci * SSM_CHUNK) % tq == 0:
            tile_ref = out[0:1, :]
            ti = (ci * SSM_CHUNK) // tq
            tref_ref[0, ti:ti + 1, :] = tile_ref
        c = out - tile_ref
        c_hi = c.astype(BF16)
        r1 = c - c_hi.astype(F32)
        c_mid = r1.astype(BF16)
        c_lo = (r1 - c_mid.astype(F32)).astype(BF16)
        compact = jnp.dot(jnp.concatenate([c_hi, c_mid, c_lo], axis=1), place_ref[...],
                          preferred_element_type=F32) + const_ref[0:1, :]
        for side, o_ref in enumerate((qa_ref, ka_ref)):
            dec = compact[:, side * LANES:(side + 1) * LANES]
            for h in range(N_HEADS):
                pair = qk_ref[rows, side * GROUP_WIDTH + (h // 2) * LANES:
                              side * GROUP_WIDTH + (h // 2 + 1) * LANES].astype(F32)
                feat = pair if h % 2 == 0 else pltpu.roll(pair, HEAD_DIM, 1)
                dec_h = pltpu.roll(dec, AUG_LANE - AUG_TERMS * h, 1)
                group = jnp.where(low, feat, jnp.where(aug_lanes, dec_h, 0.0))
                o_ref[rows, h * LANES:(h + 1) * LANES] = group.astype(o_ref.dtype)


AUG_TERMS = 6


def _fox_placement():
    place = np.zeros((3 * LANES, 2 * LANES), np.float32)
    const = np.zeros((8, 2 * LANES), np.float32)
    for h in range(N_HEADS):
        a0 = AUG_TERMS * h
        for term in range(3):
            place[term * LANES + COL_CUMF + h, a0 + term] = 1.0
            place[term * LANES + COL_CUMF + h, LANES + a0 + 3 + term] = -1.0
            const[0, a0 + 3 + term] = 1.0
            const[0, LANES + a0 + term] = 1.0
    return jnp.asarray(place, BF16), jnp.asarray(const, F32)


def _scalar_prep(misc, params, qk, batch, seq, tq):
    place, const = _fox_placement()
    full = lambda a: pl.BlockSpec(a.shape, lambda b: (0,) * a.ndim)
    return pl.pallas_call(
        functools.partial(_scalar_prep_kernel, tq=tq),
        grid=(batch,),
        in_specs=[pl.BlockSpec((seq, LANES), lambda b: (b, 0)),
                  pl.BlockSpec((8, LANES), lambda b: (0, 0)),
                  pl.BlockSpec((seq, 2 * GROUP_WIDTH), lambda b: (b, 0)),
                  full(place), full(const)],
        out_specs=[pl.BlockSpec((seq, LANES), lambda b: (b, 0)),
                   pl.BlockSpec((1, N_SCALAR_ROWS, seq), lambda b: (b, 0, 0)),
                   pl.BlockSpec((seq, HEAD_PAD), lambda b: (b, 0)),
                   pl.BlockSpec((seq, HEAD_PAD), lambda b: (b, 0)),
                   pl.BlockSpec((1, 8, LANES), lambda b: (b, 0, 0))],
        out_shape=[jax.ShapeDtypeStruct((batch * seq, LANES), F32),
                   jax.ShapeDtypeStruct((batch, N_SCALAR_ROWS, seq), F32),
                   jax.ShapeDtypeStruct((batch * seq, HEAD_PAD), BF16),
                   jax.ShapeDtypeStruct((batch * seq, HEAD_PAD), BF16),
                   jax.ShapeDtypeStruct((batch, 8, LANES), F32)],
        compiler_params=_cparams("parallel"),
        name="scalar_prep",
    )(misc, params, qk, place, const)


def _pair_lanes(col, base, shape):
    lane = _lane_iota(shape)
    return jnp.where(lane < HEAD_DIM, col[:, base:base + 1], col[:, base + 1:base + 2])


def _ssd_kernel(p_ref, col_ref, row_ref, cw_ref, par_ref, o_ref, u_ref):
    s = p_ref.shape[0]
    q = SSM_CHUNK
    gw = GROUP_WIDTH
    xbc = p_ref[:, gw:3 * gw].astype(F32)
    acc = xbc * cw_ref[SSM_CONV - 1:SSM_CONV, :]
    for k in range(1, SSM_CONV):
        acc = acc + _shift_rows(xbc, k) * cw_ref[SSM_CONV - 1 - k:SSM_CONV - k, :]
    u_ref[...] = _silu(acc + cw_ref[SSM_CONV:SSM_CONV + 1, :])

    d_skip = par_ref[0:1, :]
    norm_g = par_ref[1:2, :]
    lane_q = _lane_iota((q, LANES))
    low = lane_q < HEAD_DIM
    tri = _row_iota((q, q)) >= _lane_iota((q, q))

    def chunk(ci, states):
        rows = pl.ds(ci * q, q)
        u = u_ref[rows, :]
        col = col_ref[rows, :]
        bm = u[:, gw:gw + LANES]
        cm = u[:, gw + LANES:gw + 2 * LANES]
        z = p_ref[rows, 0:gw].astype(F32)
        new_states = []
        ys = []
        for g in range(2):
            sel = low if g == 0 else jnp.logical_not(low)
            cg = jnp.where(sel, cm, 0.0).astype(BF16)
            bg = jnp.where(sel, bm, 0.0)
            gmat = lax.dot_general(cg, bm.astype(BF16), (((1,), (1,)), ((), ())),
                                   preferred_element_type=F32)
            xs = u[:, g * LANES:(g + 1) * LANES]
            dt2 = _pair_lanes(col, COL_DT + 2 * g, (q, LANES))
            ac2 = _pair_lanes(col, COL_ACUM + 2 * g, (q, LANES))
            xdt = xs * dt2
            xdt_b = xdt.astype(BF16)
            st = states[g]
            y_off = jnp.dot(cg, st.astype(BF16), preferred_element_type=F32) * jnp.exp(ac2)
            halves = []
            for hh in range(2):
                h = 2 * g + hh
                ac_col = col[:, COL_ACUM + h:COL_ACUM + h + 1]
                ac_row = row_ref[0, COL_ACUM + h:COL_ACUM + h + 1, rows]
                decay = jnp.exp(jnp.where(tri, ac_col - ac_row, -1e30))
                mm = (gmat * decay).astype(BF16)
                halves.append(jnp.dot(mm, xdt_b, preferred_element_type=F32))
            y = jnp.where(low, halves[0], halves[1]) + y_off + d_skip[:, g * LANES:(g + 1) * LANES] * xs
            ys.append(y)
            ac_last = ac2[q - 1:q, :]
            w_end = jnp.exp(ac_last - ac2)
            xw = (xdt * w_end).astype(BF16)
            upd = jnp.dot(bg.T.astype(BF16), xw, preferred_element_type=F32)
            new_states.append(st * jnp.exp(ac_last) + upd)
        yfull = jnp.concatenate(ys, axis=1) * _silu(z)
        o_ref[rows, :] = _rms(yfull, norm_g).astype(o_ref.dtype)
        return tuple(new_states)

    init = (jnp.zeros((LANES, LANES), F32), jnp.zeros((LANES, LANES), F32))
    states = init
    for ci in range(s // q):
        states = chunk(ci, states)


def _ssd_mixer(pc, col, rows, conv_wb, par, batch, seq):
    gw = GROUP_WIDTH
    return pl.pallas_call(
        _ssd_kernel,
        grid=(batch,),
        in_specs=[pl.BlockSpec((seq, 3 * gw), lambda b: (b, 0)),
                  pl.BlockSpec((seq, LANES), lambda b: (b, 0)),
                  pl.BlockSpec((1, N_SCALAR_ROWS, seq), lambda b: (b, 0, 0)),
                  pl.BlockSpec((8, 2 * gw), lambda b: (0, 0)),
                  pl.BlockSpec((8, gw), lambda b: (0, 0))],
        out_specs=pl.BlockSpec((seq, gw), lambda b: (b, 0)),
        out_shape=jax.ShapeDtypeStruct((batch * seq, gw), BF16),
        scratch_shapes=[pltpu.VMEM((seq, 2 * gw), F32)],
        compiler_params=_cparams("parallel"),
        name="ssd_mixer",
    )(pc, col, rows, conv_wb, par)


def _attn_kernel(*refs, fox, tq):
    if fox:
        tref_ref, q_ref, k_ref, vt_ref, o_ref = refs
    else:
        q_ref, k_ref, vt_ref, o_ref = refs
        tref_ref = None
    b = pl.program_id(0)
    i = pl.program_id(1)
    key = _row_iota((tq, tq))
    qry = _lane_iota((tq, tq))
    if fox:
        allowed = key <= qry
    else:
        shift = int(math.log2(MLA_CHUNK))
        allowed = (key >> shift) <= (qry >> shift)
    qs = [q_ref[:, h * LANES:(h + 1) * LANES] for h in range(N_HEADS)]
    ones_rows = jnp.ones((16, tq), BF16)

    def step(j, masked, carry):
        rk = pl.ds(pl.multiple_of(j * tq, tq), tq)
        scores = [lax.dot_general(k_ref[rk, h * LANES:(h + 1) * LANES], qs[h], (((1,), (1,)), ((), ())),
                                  preferred_element_type=F32) for h in range(N_HEADS)]
        probs = []
        for h in range(N_HEADS):
            m, l, _ = carry[h]
            s = scores[h]
            if masked:
                s = jnp.where(allowed, s, -1e30)
            delta = (tref_ref[b, i, h] - tref_ref[b, j, h]) if fox else 0.0
            m_new = jnp.maximum(m, jnp.max(s, axis=0, keepdims=True) + delta)
            alpha = jnp.exp2(m - m_new)
            p = jnp.exp2(s - (m_new - delta))
            probs.append((m_new, alpha, p.astype(BF16)))
        new = []
        for h in range(N_HEADS):
            pair = h // 2
            m_new, alpha, p = probs[h]
            lhs = jnp.concatenate([vt_ref[pair * LANES:(pair + 1) * LANES, rk], ones_rows], axis=0)
            pv = jnp.dot(lhs, p, preferred_element_type=F32)
            new.append((m_new, alpha * carry[h][1] + pv[LANES:LANES + 1], alpha * carry[h][2] + pv[0:LANES]))
        return tuple(new)

    init = tuple((jnp.full((1, tq), -1e30, F32), jnp.zeros((1, tq), F32), jnp.zeros((LANES, tq), F32))
                 for _ in range(N_HEADS))
    carry = lax.fori_loop(0, i, lambda j, c: step(j, False, c), init)
    carry = step(i, True, carry)
    outs = [acc / l for (_, l, acc) in carry]
    top = _row_iota((LANES, tq)) < HEAD_DIM
    o_t = jnp.concatenate([jnp.where(top, outs[0], outs[1]), jnp.where(top, outs[2], outs[3])], axis=0)
    o_ref[...] = o_t.T.astype(o_ref.dtype)


def _attention(q, k, vt, tref, batch, seq, tq, name):
    nq = seq // tq
    fox = tref is not None
    kern = functools.partial(_attn_kernel, fox=fox, tq=tq)
    grid_spec = pltpu.PrefetchScalarGridSpec(
        num_scalar_prefetch=1 if fox else 0,
        grid=(batch, nq),
        in_specs=[pl.BlockSpec((tq, HEAD_PAD), lambda b, i, *_: (b * nq + i, 0)),
                  pl.BlockSpec((seq, HEAD_PAD), lambda b, i, *_: (b, 0)),
                  pl.BlockSpec((GROUP_WIDTH, seq), lambda b, i, *_: (0, b))],
        out_specs=pl.BlockSpec((tq, GROUP_WIDTH), lambda b, i, *_: (b * nq + i, 0)),
    )
    args = ((tref,) if fox else ()) + (q, k, vt)
    return pl.pallas_call(
        kern,
        grid_spec=grid_spec,
        out_shape=jax.ShapeDtypeStruct((batch * seq, GROUP_WIDTH), BF16),
        compiler_params=_cparams("parallel", "arbitrary"),
        name=name,
    )(*args)


def _outproj_kernel(x_ref, ya, yb, yc, yd, w_ref, g_ref, wr_ref, br_ref,
                    x2_ref, h2_ref, rrow_ref, cnt_ref, *, tm, moe_tile):
    y = jnp.concatenate([ya[...], yb[...], yc[...], yd[...]], axis=1)
    x2 = x_ref[...] + jnp.dot(y, w_ref[...], preferred_element_type=F32)
    x2_ref[...] = x2
    h2 = _rms(x2, g_ref[...])
    h2_ref[...] = h2.astype(h2_ref.dtype)
    h_hi = h2.astype(BF16)
    h_lo = (h2 - h_hi.astype(F32)).astype(BF16)
    part = jnp.dot(h_hi, wr_ref[...], preferred_element_type=F32)
    logits = (part[:, 0:LANES] + part[:, LANES:2 * LANES]
              + jnp.dot(h_lo, wr_ref[:, 0:LANES], preferred_element_type=F32) + br_ref[...])
    lt = logits.T[0:ROUTER_ROWS]
    row = _row_iota(lt.shape)
    neg = -1e30
    big = 1 << 20
    gmask = (row >= N_EXPERTS) & (row < N_EXPERTS + N_EXPERT_GROUPS)
    gl = jnp.where(gmask, lt, neg)
    gmax = jnp.max(gl, axis=0, keepdims=True)
    gsum = jnp.sum(jnp.where(gmask, jnp.exp(gl - gmax), 0.0), axis=0, keepdims=True)
    g_w = 1.0 / gsum
    g_idx = jnp.min(jnp.where(gmask & (gl == gmax), row, big), axis=0, keepdims=True) - N_EXPERTS
    emask = (row < N_EXPERTS) & ((row >> int(math.log2(EXPERTS_PER_GROUP))) == g_idx)
    el = jnp.where(emask, lt, neg)
    e1v = jnp.max(el, axis=0, keepdims=True)
    esum = jnp.sum(jnp.where(emask, jnp.exp(el - e1v), 0.0), axis=0, keepdims=True)
    i1 = jnp.min(jnp.where(emask & (el == e1v), row, big), axis=0, keepdims=True)
    el2 = jnp.where(row == i1, neg, el)
    e2v = jnp.max(el2, axis=0, keepdims=True)
    i2 = jnp.min(jnp.where(emask & (row != i1) & (el2 == e2v), row, big), axis=0, keepdims=True)
    p1 = 1.0 / esum
    p2 = jnp.exp(e2v - e1v) / esum
    w1 = g_w * (p1 / (p1 + p2))
    w2 = g_w * (p2 / (p1 + p2))
    out_row = _row_iota(rrow_ref.shape)
    rrow_ref[...] = jnp.where(out_row == 0, i1.astype(F32),
                              jnp.where(out_row == 1, i2.astype(F32),
                                        jnp.where(out_row == 2, w1, jnp.where(out_row == 3, w2, 0.0))))
    step = pl.program_id(0)

    @pl.when(step == 0)
    def _():
        cnt_ref[...] = jnp.zeros_like(cnt_ref)

    chosen = jnp.where((row == i1) | (row == i2), 1.0, 0.0).astype(BF16)
    tiles = tm // moe_tile
    tile_of = (_row_iota((tm, LANES)) >> int(math.log2(moe_tile))) + step * tiles
    to_tile = jnp.where(_lane_iota((tm, LANES)) == tile_of, 1.0, 0.0).astype(BF16)
    counts = jnp.dot(chosen, to_tile, preferred_element_type=F32)
    cnt_ref[...] += counts[0:N_EXPERTS]


def _outproj(x, ya, yb, yc, yd, w, g, wr, br, tm, moe_tile):
    t = x.shape[0]
    full = lambda a: pl.BlockSpec(a.shape, lambda i: (0, 0))
    tile = lambda wd: pl.BlockSpec((tm, wd), lambda i: (i, 0))
    return pl.pallas_call(
        functools.partial(_outproj_kernel, tm=tm, moe_tile=moe_tile),
        grid=(t // tm,),
        in_specs=[tile(D_MODEL)] + [tile(GROUP_WIDTH)] * 4 + [full(w), full(g), full(wr), full(br)],
        out_specs=[tile(D_MODEL), tile(D_MODEL), pl.BlockSpec((8, tm), lambda i: (0, i)),
                   pl.BlockSpec((N_EXPERTS, LANES), lambda i: (0, 0))],
        out_shape=[jax.ShapeDtypeStruct((t, D_MODEL), F32), jax.ShapeDtypeStruct((t, D_MODEL), BF16),
                   jax.ShapeDtypeStruct((8, t), F32), jax.ShapeDtypeStruct((N_EXPERTS, LANES), F32)],
        compiler_params=_cparams("arbitrary"),
        name="outproj_router",
    )(x, ya, yb, yc, yd, w, g, wr, br)


MOE_TILE = 256
CHUNK = 8
LOCAL_ROWS = 2 * MOE_TILE + 256
PACKED = D_MODEL // 2
XS_WIDTH = PACKED
U32 = jnp.uint32


def _pack_bf16_pairs(x, exact=False):
    if not exact:
        x = x.astype(BF16).astype(F32)
    half = x.shape[1] // 2
    lo = lax.bitcast_convert_type(x[:, :half], U32)
    hi = lax.bitcast_convert_type(x[:, half:], U32)
    return hi | (lo >> 16)


def _unpack_bf16_pairs(words):
    lo = lax.bitcast_convert_type(words << 16, F32)
    hi = lax.bitcast_convert_type(words & U32(0xFFFF0000), F32)
    return jnp.concatenate([lo, hi], axis=1).astype(BF16)


COPY_ROWS = (2 * CHUNK, CHUNK)


def _route_kernel(r_ref, cnt_ref, lrow_ref, lcol_ref, tab_ref, ctab_ref, meta_ref,
                  loff_ref, goff_ref, n8_ref, *, tm, nbp, tiles):
    i = pl.program_id(0)
    hi = lax.Precision.HIGHEST

    @pl.when(i == 0)
    def _():
        cnt = cnt_ref[...]
        n8 = jnp.floor((cnt + (CHUNK - 1)) * (1.0 / CHUNK)) * CHUNK
        er = _row_iota((N_EXPERTS, N_EXPERTS))
        ec = _lane_iota((N_EXPERTS, N_EXPERTS))
        below = jnp.where(er > ec, 1.0, 0.0)
        loff = jnp.dot(below, n8, preferred_element_type=F32, precision=hi)
        rows_e = jnp.sum(n8, axis=-1, keepdims=True) + jnp.zeros_like(n8)
        padded = jnp.floor((rows_e + (MOE_ROWS - 1)) * (1.0 / MOE_ROWS)) * MOE_ROWS
        e_start = jnp.dot(below, padded, preferred_element_type=F32, precision=hi)
        tr = _row_iota((LANES, LANES))
        tc = _lane_iota((LANES, LANES))
        earlier = jnp.where(tr < tc, 1.0, 0.0)
        goff = e_start + jnp.dot(n8, earlier, preferred_element_type=F32, precision=hi)
        loff_ref[...] = loff
        goff_ref[...] = goff
        n8_ref[...] = n8
        big = jnp.floor(n8 * (0.5 / CHUNK))
        small = n8 * (1.0 / CHUNK) - 2.0 * big
        row_t = _row_iota(tab_ref.shape)
        tab_ref[...] = jnp.where(row_t == 0, jnp.sum(big, axis=0, keepdims=True),
                                 jnp.where(row_t == 1, jnp.sum(small, axis=0, keepdims=True), 0.0)).astype(I32)
        reps = nbp // LANES
        pend_b = jnp.concatenate([e_start + padded] * reps, axis=1)
        vend_b = jnp.concatenate([e_start + rows_e] * reps, axis=1)
        used_b = jnp.concatenate([padded] * reps, axis=1) > 0.0
        b0 = (_lane_iota((N_EXPERTS, nbp)) * MOE_ROWS).astype(F32)
        bexp = jnp.sum(jnp.where(pend_b <= b0, 1.0, 0.0), axis=0, keepdims=True)
        bexp = jnp.minimum(bexp, N_EXPERTS - 1.0)
        e_b = _row_iota((N_EXPERTS, nbp)).astype(F32)
        is_e = e_b == bexp
        vend = jnp.sum(jnp.where(is_e, vend_b, 0.0), axis=0, keepdims=True)
        nvalid = jnp.clip(vend - b0[0:1], 0.0, float(MOE_ROWS))
        total = jnp.max(pend_b, axis=0, keepdims=True) * (1.0 / MOE_ROWS)
        order = jnp.sum(jnp.where(used_b & (e_b < bexp), 1.0, 0.0), axis=0, keepdims=True)
        nxt = jnp.min(jnp.where(used_b & (e_b > bexp), e_b, float(N_EXPERTS)), axis=0, keepdims=True)
        row = _row_iota((8, nbp))
        meta = jnp.where(row == 0, bexp, jnp.where(row == 1, nvalid, jnp.where(row == 2, total,
                         jnp.where(row == 3, order, jnp.where(row == 4, nxt, 0.0)))))
        meta_ref[...] = meta.astype(I32)

    su = jnp.where(_row_iota((tm, tm)) < _lane_iota((tm, tm)), 1.0, 0.0).astype(BF16)
    incl = jnp.where(_row_iota((N_EXPERTS, N_EXPERTS)) >= _lane_iota((N_EXPERTS, N_EXPERTS)), 1.0, 0.0)
    cidx = _lane_iota((N_EXPERTS, LANES)).astype(F32)
    e_iota = _row_iota((N_EXPERTS, tm))
    out_row = _row_iota((LANES, tm))
    for k in range(tiles):
        tok = slice(k * tm, (k + 1) * tm)
        oh0 = e_iota == r_ref[0:1, tok].astype(I32)
        oh1 = e_iota == r_ref[1:2, tok].astype(I32)
        oh = jnp.where(oh0 | oh1, 1.0, 0.0)
        tile_lane = _lane_iota((N_EXPERTS, LANES)) == i * tiles + k
        before = jnp.dot(oh.astype(BF16), su, preferred_element_type=F32)
        base = jnp.sum(jnp.where(tile_lane, loff_ref[...], 0.0), axis=-1, keepdims=True) + before
        d0 = jnp.sum(jnp.where(oh0, base, 0.0), axis=0, keepdims=True)
        d1 = jnp.sum(jnp.where(oh1, base, 0.0), axis=0, keepdims=True)
        lrow_ref[k, 0:1, :] = d0.astype(I32)
        lrow_ref[k, 1:2, :] = d1.astype(I32)
        pick_tile = lambda ref: jnp.sum(jnp.where(tile_lane, ref[...], 0.0), axis=-1, keepdims=True)
        nch = pick_tile(n8_ref) * (1.0 / CHUNK)
        n_big = jnp.floor(nch * 0.5)
        n_small = nch - 2.0 * n_big
        loff_t = pick_tile(loff_ref)
        goff_t = pick_tile(goff_ref)
        for c, (n, rows, first) in enumerate(((n_big, COPY_ROWS[0], 0.0),
                                              (n_small, COPY_ROWS[1], n_big * COPY_ROWS[0]))):
            cend = jnp.dot(incl, n + jnp.zeros((N_EXPERTS, LANES), F32), preferred_element_type=F32, precision=hi)
            cstart = cend - n
            mine = (cidx >= cstart) & (cidx < cend)
            step_rows = first + (cidx - cstart) * rows
            ctab_ref[k, 2 * c:2 * c + 1, :] = jnp.sum(jnp.where(mine, loff_t + step_rows, 0.0), axis=0,
                                                      keepdims=True).astype(I32)
            ctab_ref[k, 2 * c + 1:2 * c + 2, :] = jnp.sum(jnp.where(mine, goff_t + step_rows, 0.0), axis=0,
                                                          keepdims=True).astype(I32)
        lcol_ref[tok, :] = jnp.where(out_row == 0, d0, jnp.where(out_row == 1, d1,
                                     jnp.where(out_row == 2, r_ref[2:3, tok],
                                               jnp.where(out_row == 3, r_ref[3:4, tok], 0.0)))).T


def _route(rrow, cnt, tm, nbp):
    t = rrow.shape[1]
    nt = t // tm
    tiles = next(r for r in (4, 2, 1) if nt % r == 0)
    kern = functools.partial(_route_kernel, tm=tm, nbp=nbp, tiles=tiles)
    return pl.pallas_call(
        kern,
        grid=(nt // tiles,),
        in_specs=[pl.BlockSpec((8, tiles * tm), lambda i: (0, i)),
                  pl.BlockSpec((N_EXPERTS, LANES), lambda i: (0, 0))],
        out_specs=[pl.BlockSpec((tiles, 2, tm), lambda i: (i, 0, 0)),
                   pl.BlockSpec((tiles * tm, LANES), lambda i: (i, 0)),
                   pl.BlockSpec((8, LANES), lambda i: (0, 0)),
                   pl.BlockSpec((tiles, 2 * len(COPY_ROWS), LANES), lambda i: (i, 0, 0)),
                   pl.BlockSpec((8, nbp), lambda i: (0, 0))],
        out_shape=[jax.ShapeDtypeStruct((nt, 2, tm), I32), jax.ShapeDtypeStruct((t, LANES), F32),
                   jax.ShapeDtypeStruct((8, LANES), I32), jax.ShapeDtypeStruct((nt, 2 * len(COPY_ROWS), LANES), I32),
                   jax.ShapeDtypeStruct((8, nbp), I32)],
        scratch_shapes=[pltpu.VMEM((N_EXPERTS, LANES), F32)] * 3,
        compiler_params=_cparams("arbitrary"),
        name="moe_route",
    )(rrow, cnt)


def _chunk_copies(tabs, i, local_ref, global_ref, sem, to_global, action):
    tab_ref, ctab_ref = tabs
    for k, rows in enumerate(COPY_ROWS):
        count = tab_ref[k, i]

        def copy(lo, go, rows=rows):
            lsl = local_ref.at[pl.ds(pl.multiple_of(lo, CHUNK), rows)]
            gsl = global_ref.at[pl.ds(pl.multiple_of(go, CHUNK), rows)]
            return pltpu.make_async_copy(lsl, gsl, sem) if to_global else pltpu.make_async_copy(gsl, lsl, sem)

        if action == "wait":
            def one(c, c1, copy=copy):
                copy(0, 0).wait()
                return c1
        else:
            def one(c, c1, copy=copy, k=k):
                copy(ctab_ref[i, 2 * k, c], ctab_ref[i, 2 * k + 1, c]).start()
                return c1

        lax.fori_loop(0, count, one, 0)


def _scatter_kernel(tab_ref, ctab_ref, meta_ref, lrow_ref, h_ref, xs_ref, buf_ref, zero_ref, sem, zsem, *, tm, nb):
    i = pl.program_id(0)
    tabs = (tab_ref, ctab_ref)

    @pl.when(i == 0)
    def _():
        zero_ref[...] = jnp.zeros_like(zero_ref)
        n_used = meta_ref[2, 0]

        def zcopy(b):
            sub = lax.shift_right_logical(meta_ref[1, b], int(math.log2(FFN_SUB)))
            start = pl.multiple_of(b * MOE_ROWS + sub * FFN_SUB, FFN_SUB)
            return pltpu.make_async_copy(zero_ref, xs_ref.at[pl.ds(start, FFN_SUB)], zsem)

        def needs(b):
            return (b < n_used) & ((meta_ref[1, b] & (FFN_SUB - 1)) != 0)

        def start(b, c):
            @pl.when(needs(b))
            def _():
                zcopy(b).start()
            return c

        def wait(b, c):
            @pl.when(needs(b))
            def _():
                zcopy(b).wait()
            return c

        lax.fori_loop(0, nb, start, 0)
        lax.fori_loop(0, nb, wait, 0)

    rows = _row_iota((LOCAL_ROWS, tm))

    def drain(tile, slot):
        _chunk_copies(tabs, tile, buf_ref.at[slot], xs_ref, sem.at[slot], True, "wait")

    for slot in range(2):
        tile = 2 * i + slot
        p0 = rows == lrow_ref[slot, 0:1, :]
        p1 = rows == lrow_ref[slot, 1:2, :]
        perm = jnp.where(p0 | p1, 1.0, 0.0).astype(BF16)
        sorted_rows = jnp.dot(perm, h_ref[slot * tm:(slot + 1) * tm, :],
                              preferred_element_type=F32)

        @pl.when(i > 0)
        def _():
            drain(tile - 2, slot)

        buf = buf_ref.at[slot]
        buf[...] = _pack_bf16_pairs(sorted_rows, exact=True)
        _chunk_copies(tabs, tile, buf, xs_ref, sem.at[slot], True, "start")

    @pl.when(i == pl.num_programs(0) - 1)
    def _():
        drain(2 * i, 0)
        drain(2 * i + 1, 1)


def _scatter(tab, ctab, meta, lrow, h2, tm, nb):
    t = h2.shape[0]
    kern = functools.partial(_scatter_kernel, tm=tm, nb=nb)
    grid_spec = pltpu.PrefetchScalarGridSpec(
        num_scalar_prefetch=3,
        grid=(t // (2 * tm),),
        in_specs=[pl.BlockSpec((2, 2, tm), lambda i, *_: (i, 0, 0)),
                  pl.BlockSpec((2 * tm, D_MODEL), lambda i, *_: (i, 0))],
        out_specs=pl.BlockSpec(memory_space=pl.ANY),
        scratch_shapes=[pltpu.VMEM((2, LOCAL_ROWS, XS_WIDTH), U32), pltpu.VMEM((FFN_SUB, XS_WIDTH), U32),
                        pltpu.SemaphoreType.DMA((2,)), pltpu.SemaphoreType.DMA],
    )
    return pl.pallas_call(
        kern,
        grid_spec=grid_spec,
        out_shape=jax.ShapeDtypeStruct((nb * MOE_ROWS, XS_WIDTH), U32),
        compiler_params=_cparams("arbitrary"),
        name="moe_scatter",
    )(tab, ctab, meta, lrow, h2)


def _ffn_kernel(meta_ref, x_ref, wg_ref, wu_ref, wd_ref, o_ref, wgu_b, wd_b, wg_f, wu_f, wd_f, wsem, *, layer):
    b = pl.program_id(0)
    live = b < meta_ref[2, 0]
    expert = meta_ref[0, b]
    prev = meta_ref[0, jnp.maximum(b - 1, 0)]

    def fetch(e, slot):
        return [pltpu.make_async_copy(src.at[layer, e], dst.at[slot], wsem.at[slot])
                for src, dst in ((wg_ref, wg_f), (wu_ref, wu_f), (wd_ref, wd_f))]

    def first_block(slot):
        @pl.when(b == 0)
        def _():
            for cp in fetch(expert, slot):
                cp.start()

        for cp in fetch(expert, slot):
            cp.wait()
        wgu_b[:, 0:EXPERT_FF] = wg_f[slot].astype(BF16)
        wgu_b[:, EXPERT_FF:2 * EXPERT_FF] = wu_f[slot].astype(BF16)
        wd_b[...] = wd_f[slot].astype(BF16)
        nxt = meta_ref[4, b]

        @pl.when(nxt < N_EXPERTS)
        def _():
            for cp in fetch(nxt, 1 - slot):
                cp.start()

    changed = live & ((b == 0) | (expert != prev))
    odd = (meta_ref[3, b] & 1) == 1

    @pl.when(changed & jnp.logical_not(odd))
    def _():
        first_block(0)

    @pl.when(changed & odd)
    def _():
        first_block(1)

    nvalid = meta_ref[1, jnp.maximum(jnp.minimum(b, meta_ref[2, 0] - 1), 0)]
    for sub in range(MOE_ROWS // FFN_SUB):
        rows = slice(sub * FFN_SUB, (sub + 1) * FFN_SUB)
        used = live & (nvalid > sub * FFN_SUB)

        @pl.when(used)
        def _():
            x = _unpack_bf16_pairs(x_ref[rows, 0:PACKED])
            gu = jnp.dot(x, wgu_b[...], preferred_element_type=F32)
            act = (_silu(gu[:, 0:EXPERT_FF]) * gu[:, EXPERT_FF:2 * EXPERT_FF]).astype(BF16)
            y = jnp.dot(act, wd_b[...], preferred_element_type=F32)
            o_ref[rows, :] = _pack_bf16_pairs(y)

        @pl.when(live & jnp.logical_not(used))
        def _():
            o_ref[rows, :] = jnp.zeros((FFN_SUB, PACKED), U32)


def _ffn(meta, xs, wg, wu, wd, layer, nb):
    def blk(b, m):
        return (jnp.maximum(jnp.minimum(b, m[2, 0] - 1), 0), 0)

    grid_spec = pltpu.PrefetchScalarGridSpec(
        num_scalar_prefetch=1,
        grid=(nb,),
        in_specs=[pl.BlockSpec((MOE_ROWS, XS_WIDTH), blk)] + [pl.BlockSpec(memory_space=pl.ANY)] * 3,
        out_specs=pl.BlockSpec((MOE_ROWS, PACKED), blk),
        scratch_shapes=[pltpu.VMEM((D_MODEL, 2 * EXPERT_FF), BF16), pltpu.VMEM((EXPERT_FF, D_MODEL), BF16),
                        pltpu.VMEM((2, D_MODEL, EXPERT_FF), F32), pltpu.VMEM((2, D_MODEL, EXPERT_FF), F32),
                        pltpu.VMEM((2, EXPERT_FF, D_MODEL), F32), pltpu.SemaphoreType.DMA((2,))],
    )
    return pl.pallas_call(
        functools.partial(_ffn_kernel, layer=layer),
        grid_spec=grid_spec,
        out_shape=jax.ShapeDtypeStruct((nb * MOE_ROWS, PACKED), U32),
        compiler_params=_cparams("arbitrary"),
        name="moe_experts",
    )(meta, xs, wg, wu, wd)


def _gather_kernel(tab_ref, ctab_ref, lcol_ref, x_ref, g_ref, ys_ref, o_ref, buf_ref, sem, *, tm, final):
    i = pl.program_id(0)

    last = pl.num_programs(0) - 1

    def fetch(tile, slot, action):
        _chunk_copies((tab_ref, ctab_ref), tile, buf_ref.at[slot], ys_ref, sem.at[slot], False, action)

    @pl.when(i == 0)
    def _():
        buf_ref[...] = jnp.zeros_like(buf_ref)
        fetch(0, 0, "start")

    col = _lane_iota((tm, LOCAL_ROWS)).astype(F32)
    for slot in range(2):
        tile = 2 * i + slot
        tok = slice(slot * tm, (slot + 1) * tm)
        if slot == 0:
            fetch(tile + 1, 1, "start")
        else:
            @pl.when(i < last)
            def _():
                fetch(tile + 1, 0, "start")

        lc = lcol_ref[tok, :]
        pick0 = jnp.where(col == lc[:, 0:1], 1.0, 0.0).astype(BF16)
        pick1 = jnp.where(col == lc[:, 1:2], 1.0, 0.0).astype(BF16)
        fetch(tile, slot, "wait")
        y = _unpack_bf16_pairs(buf_ref[slot])
        both = jnp.dot(jnp.concatenate([pick0, pick1], axis=0), y, preferred_element_type=F32)
        x = x_ref[tok, :] + lc[:, 2:3] * both[0:tm] + lc[:, 3:4] * both[tm:2 * tm]
        o_ref[tok, :] = _rms(x, g_ref[...]) if final else x


def _gather(tab, ctab, lcol, x2, g, ys, tm, final):
    t = x2.shape[0]
    kern = functools.partial(_gather_kernel, tm=tm, final=final)
    grid_spec = pltpu.PrefetchScalarGridSpec(
        num_scalar_prefetch=2,
        grid=(t // (2 * tm),),
        in_specs=[pl.BlockSpec((2 * tm, LANES), lambda i, *_: (i, 0)),
                  pl.BlockSpec((2 * tm, D_MODEL), lambda i, *_: (i, 0)),
                  pl.BlockSpec((1, D_MODEL), lambda i, *_: (0, 0)),
                  pl.BlockSpec(memory_space=pl.ANY)],
        out_specs=pl.BlockSpec((2 * tm, D_MODEL), lambda i, *_: (i, 0)),
        scratch_shapes=[pltpu.VMEM((2, LOCAL_ROWS, PACKED), U32), pltpu.SemaphoreType.DMA((2,))],
    )
    return pl.pallas_call(
        kern,
        grid_spec=grid_spec,
        out_shape=jax.ShapeDtypeStruct((t, D_MODEL), F32),
        compiler_params=_cparams("arbitrary"),
        name="moe_combine",
    )(tab, ctab, lcol, x2, g, ys)


def _pad_rows(a, rows=8):
    return jnp.zeros((rows, a.shape[-1]), F32).at[:a.shape[0]].set(a.astype(F32))


def _arrange_mla(w_uq, w_ukv):
    half = MLA_ROPE // 2
    qd = MLA_NOPE + MLA_ROPE
    wq, wqs, wk, wv = [], [], [], []
    zq = jnp.zeros((MLA_Q_LORA, LANES - qd), w_uq.dtype)
    zk = jnp.zeros((MLA_KV_LORA, LANES - MLA_NOPE), w_ukv.dtype)
    for h in range(N_HEADS):
        q = w_uq[:, h * qd:(h + 1) * qd]
        nope, rope = q[:, :MLA_NOPE], q[:, MLA_NOPE:]
        wq.append(jnp.concatenate([nope, rope, zq], axis=1))
        wqs.append(jnp.concatenate([jnp.zeros_like(nope), -rope[:, half:], rope[:, :half], zq], axis=1))
        kv = w_ukv[:, h * 2 * MLA_NOPE:(h + 1) * 2 * MLA_NOPE]
        wk.append(jnp.concatenate([kv[:, :MLA_NOPE], zk], axis=1))
        wv.append(kv[:, MLA_NOPE:])
    cat = lambda xs: jnp.concatenate(xs, axis=1).astype(BF16)
    return cat(wq), cat(wqs), cat(wk), cat(wv).T


def kernel(x, positions, norm_mix, w_in, conv_a, fox_forget_bias, ssm_conv_w, ssm_conv_b, ssm_dt_bias,
           ssm_a_log, ssm_d, ssm_norm, mla_q_norm, mla_kv_norm, mla_w_uq, mla_w_ukv, w_out, norm_ffn,
           router_group_w, router_group_b, router_expert_w, router_expert_b, expert_w_gate, expert_w_up,
           expert_w_down, norm_final):
    batch, seq, d = x.shape
    t = batch * seq
    depth = w_in.shape[0]
    tm = min(512, t)
    tq = min(ATTN_TQ, seq)
    tmd = min(MOE_TILE, t)
    max_rows = 2 * t + (CHUNK - 1) * N_EXPERTS * (t // tmd) + N_EXPERTS * (MOE_ROWS - 1)
    nb = -(-max_rows // MOE_ROWS)
    nbp = -(-nb // LANES) * LANES

    xf = x.reshape(t, d)
    cos, sin = _rope_tables(positions, tm)
    w_in_t = jnp.swapaxes(w_in, 1, 2)

    for l in range(depth):
        wq, wqs, wk, wv = _arrange_mla(mla_w_uq[l], mla_w_ukv[l])
        ya, pb, pc, misc, fox_vt, q, k, v = _inproj(
            xf, norm_mix[l][None, :], w_in_t, l, _pad_rows(conv_a[l]), cos, sin, mla_q_norm[l][None, :],
            mla_kv_norm[l][None, :], jnp.concatenate([wq, wqs], axis=1), wk, wv, tm, seq)

        sp = jnp.zeros((8, LANES), F32)
        sp = sp.at[0, MISC_F:MISC_F + N_HEADS].set(fox_forget_bias[l])
        sp = sp.at[0, MISC_DT:MISC_DT + N_HEADS].set(ssm_dt_bias[l])
        sp = sp.at[1, MISC_DT:MISC_DT + N_HEADS].set(ssm_a_log[l])
        col, rows, fox_q, fox_k, tref = _scalar_prep(misc, sp, pb, batch, seq, tq)

        yb = _attention(fox_q, fox_k, fox_vt, tref, batch, seq, tq, "fox_attention")
        conv_wb = _pad_rows(jnp.concatenate([ssm_conv_w[l], ssm_conv_b[l][None, :]], axis=0))
        ssd_par = _pad_rows(jnp.stack([jnp.repeat(ssm_d[l], HEAD_DIM), ssm_norm[l]]))
        yc = _ssd_mixer(pc, col, rows, conv_wb, ssd_par, batch, seq)
        yd = _attention(q, k, v, None, batch, seq, tq, "mla_attention")

        pad = jnp.zeros((d, LANES - N_EXPERTS - N_EXPERT_GROUPS), F32)
        wr = jnp.concatenate([router_expert_w[l], router_group_w[l], pad], axis=1)
        wr_hi = wr.astype(BF16)
        wr = jnp.concatenate([wr_hi, (wr - wr_hi.astype(F32)).astype(BF16)], axis=1)
        br = jnp.concatenate([router_expert_b[l], router_group_b[l], pad[0]])[None, :]
        x2, h2, rrow, cnt = _outproj(xf, ya, yb, yc, yd, w_out[l].astype(BF16), norm_ffn[l][None, :], wr, br,
                                     tm, tmd)

        lrow, lcol, tab, ctab, meta = _route(rrow, cnt, tmd, nbp)
        xs = _scatter(tab, ctab, meta, lrow, h2, tmd, nb)
        ys = _ffn(meta, xs, expert_w_gate, expert_w_up, expert_w_down, l, nb)
        final = l == depth - 1
        xf = _gather(tab, ctab, lcol, x2, norm_final[None, :], ys, tmd, final)

    return xf.reshape(batch, seq, d)
```

```python
import functools
import math

import jax
import jax.numpy as jnp
import numpy as np
from jax import lax
from jax.experimental import pallas as pl
from jax.experimental.pallas import tpu as pltpu

F32 = jnp.float32
BF16 = jnp.bfloat16
I32 = jnp.int32

LANES = 128
VMEM_LIMIT_BYTES = 56 * 1024 * 1024

D_MODEL = 1024
RMS_EPS = 1e-6
LOG2E = math.log2(math.e)
GROUP_WIDTH = 256
HEAD_DIM = 64
N_HEADS = 4

CONV_A_WIDTH = 3
SSM_CONV = 4
SSM_STATE = 64
SSM_CHUNK = 256

MLA_NOPE = 64
MLA_ROPE = 32
MLA_Q_LORA = 256
MLA_KV_LORA = 128
ROPE_BASE = 10000.0
MLA_CHUNK = 64
ATTN_TQ = 512
ROW_TILE = 512

N_EXPERT_GROUPS = 4
EXPERTS_PER_GROUP = 8
N_EXPERTS = 32
EXPERT_FF = 256
ROUTER_ROWS = 48
MOE_ROWS = 512
ZERO_ROWS = 256

SEG_A = (0, 768)
SEG_B = (768, 1280)
SEG_C = (1280, 2048)
SEG_D = (2048, 2432)
SEG_M = (2432, 2560)
SEG_M2 = (2560, 2688)
IN_COLS_PADDED = 2688
HEAD_PAD = N_HEADS * LANES
AUG_LANE = HEAD_DIM
MISC_F = 0
MISC_DT = 4
MISC_ROPE = 64
COL_CUMF = 0
COL_DT = 4
COL_ACUM = 8
N_SCALAR_ROWS = 16


def _cparams(*sem):
    return pltpu.CompilerParams(dimension_semantics=sem, vmem_limit_bytes=VMEM_LIMIT_BYTES)


def _lane_iota(shape):
    return lax.broadcasted_iota(I32, shape, len(shape) - 1)


def _row_iota(shape):
    return lax.broadcasted_iota(I32, shape, 0)


def _rms(x, g):
    ms = jnp.mean(x * x, axis=-1, keepdims=True)
    return x * lax.rsqrt(ms + RMS_EPS) * g


def _silu(x):
    return x / (1.0 + jnp.exp(-x))


def _softplus(x):
    return jnp.maximum(x, 0.0) + jnp.log(1.0 + jnp.exp(-jnp.abs(x)))


def _shift_rows(x, k):
    rolled = pltpu.roll(x, k, 0)
    return jnp.where(_row_iota(x.shape) >= k, rolled, 0.0)


def _rope_kernel(pos_ref, freq_ref, cos_ref, sin_ref):
    q = pos_ref.shape[2]
    lane = _lane_iota((q, LANES))
    group = lane >> int(math.log2(MLA_ROPE))
    rows = jnp.concatenate([pos_ref[0], jnp.zeros((8 - ROPE_PACK, q), F32)], axis=0)
    cols = jnp.concatenate([rows] * (LANES // 8), axis=0).T
    pos = cols[:, ROPE_PACK - 1:ROPE_PACK]
    for k in range(ROPE_PACK - 2, -1, -1):
        pos = jnp.where(group == k, cols[:, k:k + 1], pos)
    ang = pos * freq_ref[...]
    c = jnp.cos(ang)
    s = jnp.sin(ang)
    rope = (lane >= MISC_ROPE) & (lane < MISC_ROPE + MLA_ROPE)
    for k in range(ROPE_PACK):
        shift = (MISC_ROPE - MLA_ROPE * k) % LANES
        ck = pltpu.roll(c, shift, 1) if shift else c
        sk = pltpu.roll(s, shift, 1) if shift else s
        cos_ref[k * q:(k + 1) * q, :] = jnp.where(rope, ck, jnp.where(lane < MISC_ROPE, 1.0, 0.0))
        sin_ref[k * q:(k + 1) * q, :] = jnp.where(rope, sk, 0.0)


ROPE_PACK = LANES // MLA_ROPE


def _rope_tables(positions, tm):
    t = positions.size
    pos = positions.astype(F32).reshape(t // tm, ROPE_PACK, tm // ROPE_PACK)
    inv = ROPE_BASE ** (-np.arange(0, MLA_ROPE, 2, dtype=np.float32) / MLA_ROPE)
    freq = np.tile(np.concatenate([inv, inv]), ROPE_PACK)[None, :].astype(np.float32)
    return pl.pallas_call(
        _rope_kernel,
        grid=(t // tm,),
        in_specs=[pl.BlockSpec((1, ROPE_PACK, tm // ROPE_PACK), lambda i: (i, 0, 0)),
                  pl.BlockSpec((1, LANES), lambda i: (0, 0))],
        out_specs=[pl.BlockSpec((tm, LANES), lambda i: (i, 0))] * 2,
        out_shape=[jax.ShapeDtypeStruct((t, LANES), F32)] * 2,
        compiler_params=_cparams("parallel"),
        name="rope_tables",
    )(pos, jnp.asarray(freq))


def _arrange_w_in_kernel(wt_ref, w_ref, wvt_ref):
    gw = GROUP_WIDTH
    rows = lambda lo, hi: wt_ref[0, lo:hi, :]

    def put(seg0, piece_t):
        w_ref[:, seg0:seg0 + piece_t.shape[0]] = piece_t.T.astype(BF16)

    for j in range(3):
        put(SEG_A[0] + j * gw, rows(j * gw, (j + 1) * gw))
    put(SEG_B[0], rows(3 * gw, 4 * gw) * (HEAD_DIM ** -0.5 * LOG2E))
    put(SEG_B[0] + gw, rows(4 * gw, 5 * gw))
    wvt_ref[...] = rows(5 * gw, 6 * gw).astype(BF16)
    c0 = 6 * gw
    win = pltpu.roll(rows(c0, c0 + 3 * gw + 8), 3 * gw + 8 - N_HEADS, 0)
    for j in range(3):
        put(SEG_C[0] + j * gw, win[j * gw:(j + 1) * gw])
    d0 = c0 + 3 * gw + 8
    put(SEG_D[0], rows(d0, d0 + MLA_Q_LORA))
    put(SEG_D[0] + MLA_Q_LORA, rows(d0 + MLA_Q_LORA, d0 + MLA_Q_LORA + MLA_KV_LORA))
    kr0 = d0 + MLA_Q_LORA + MLA_KV_LORA
    half = MLA_ROPE // 2
    first8 = jnp.where(_row_iota((8, D_MODEL)) < N_HEADS, rows(c0, c0 + 8), rows(d0 - 8, d0))
    zeros = lambda n: jnp.zeros((n, D_MODEL), F32)
    misc_t = jnp.concatenate([first8, zeros(MISC_ROPE - 8), rows(kr0, kr0 + MLA_ROPE),
                              zeros(LANES - MISC_ROPE - MLA_ROPE)], axis=0)
    misc2_t = jnp.concatenate([zeros(MISC_ROPE), -rows(kr0 + half, kr0 + MLA_ROPE), rows(kr0, kr0 + half),
                               zeros(LANES - MISC_ROPE - MLA_ROPE)], axis=0)
    put(SEG_M[0], misc_t)
    put(SEG_M2[0], misc2_t)


def _inproj_kernel(x_ref, g_ref, wt_ref, cw_ref, cos_ref, sin_ref, nq_ref, nkv_ref, wq2_ref, wk_ref,
                   wvt2_ref, oa, ob, oc, om, ovt, q_ref, k_ref, vt_ref, w_ref, wvt_ref, halo_ref,
                   *, tm, tiles_per_seq):
    @pl.when(pl.program_id(0) == 0)
    def _():
        _arrange_w_in_kernel(wt_ref, w_ref, wvt_ref)

    h = _rms(x_ref[...], g_ref[...]).astype(BF16)
    for o, (lo, hi) in ((ob, SEG_B), (oc, SEG_C)):
        o[...] = jnp.dot(h, w_ref[:, lo:hi], preferred_element_type=F32).astype(o.dtype)

    gw = GROUP_WIDTH
    pa = jnp.dot(h, w_ref[:, SEG_A[0]:SEG_A[1]], preferred_element_type=F32)
    cv = pa[:, gw:2 * gw] * pa[:, 2 * gw:3 * gw]

    @pl.when(pl.program_id(0) % tiles_per_seq == 0)
    def _():
        halo_ref[...] = jnp.zeros_like(halo_ref)

    halo = halo_ref[...]
    row8 = _row_iota(halo.shape)
    acc = cv * cw_ref[CONV_A_WIDTH - 1:CONV_A_WIDTH, :]
    for k in range(1, CONV_A_WIDTH):
        shifted = pltpu.roll(cv, k, 0)
        top = jnp.where(row8 < k, pltpu.roll(halo, k, 0), shifted[0:8])
        shifted = jnp.concatenate([top, shifted[8:]], axis=0)
        acc = acc + shifted * cw_ref[CONV_A_WIDTH - 1 - k:CONV_A_WIDTH - k, :]
    halo_ref[...] = cv[tm - 8:tm]
    oa[...] = (pa[:, 0:gw] * acc).astype(oa.dtype)
    ovt[...] = lax.dot_general(wvt_ref[...], h, (((1,), (1,)), ((), ())),
                               preferred_element_type=F32).astype(ovt.dtype)
    misc = jnp.dot(h, w_ref[:, SEG_M[0]:SEG_M[1]], preferred_element_type=F32)
    misc2 = jnp.dot(h, w_ref[:, SEG_M2[0]:SEG_M2[1]], preferred_element_type=F32)
    om[...] = misc

    pd = jnp.dot(h, w_ref[:, SEG_D[0]:SEG_D[1]], preferred_element_type=F32)
    cq = _rms(pd[:, 0:MLA_Q_LORA], nq_ref[...]).astype(BF16)
    ckv = _rms(pd[:, MLA_Q_LORA:MLA_Q_LORA + MLA_KV_LORA], nkv_ref[...]).astype(BF16)
    cos = cos_ref[...]
    sin = sin_ref[...]
    cos4 = jnp.concatenate([cos] * N_HEADS, axis=1)
    sin4 = jnp.concatenate([sin] * N_HEADS, axis=1)
    scale = (MLA_NOPE + MLA_ROPE) ** -0.5 * LOG2E
    q2 = jnp.dot(cq, wq2_ref[...], preferred_element_type=F32)
    q_ref[...] = ((q2[:, 0:HEAD_PAD] * cos4 + q2[:, HEAD_PAD:2 * HEAD_PAD] * sin4) * scale).astype(q_ref.dtype)
    lane = _lane_iota(cos.shape)
    rope = (lane >= MISC_ROPE) & (lane < MISC_ROPE + MLA_ROPE)
    kr = jnp.where(rope, misc * cos + misc2 * sin, 0.0)
    k = jnp.dot(ckv, wk_ref[...], preferred_element_type=F32)
    k_ref[...] = (k + jnp.concatenate([kr] * N_HEADS, axis=1)).astype(k_ref.dtype)
    vt_ref[...] = lax.dot_general(wvt2_ref[...], ckv, (((1,), (1,)), ((), ())),
                                  preferred_element_type=F32).astype(vt_ref.dtype)


def _inproj(x, g, w_in_t, layer, conv_w, cos, sin, nq, nkv, wq2, wk, wvt2, tm, seq):
    t = x.shape[0]
    once = lambda a: pl.BlockSpec((1,) + a.shape[1:], lambda i: (layer, 0, 0), pipeline_mode=pl.Buffered(1))
    full = lambda a: pl.BlockSpec(a.shape, lambda i: (0, 0))
    tile = lambda wd: pl.BlockSpec((tm, wd), lambda i: (i, 0))
    cols = lambda: pl.BlockSpec((GROUP_WIDTH, tm), lambda i: (0, i))
    seg = lambda s: s[1] - s[0]
    return pl.pallas_call(
        functools.partial(_inproj_kernel, tm=tm, tiles_per_seq=seq // tm),
        grid=(t // tm,),
        in_specs=[tile(D_MODEL), full(g), once(w_in_t), full(conv_w), tile(LANES), tile(LANES),
                  full(nq), full(nkv), full(wq2), full(wk), full(wvt2)],
        out_specs=[tile(GROUP_WIDTH), tile(seg(SEG_B)), tile(seg(SEG_C)), tile(LANES), cols(),
                   tile(HEAD_PAD), tile(HEAD_PAD), cols()],
        out_shape=[jax.ShapeDtypeStruct((t, GROUP_WIDTH), BF16), jax.ShapeDtypeStruct((t, seg(SEG_B)), BF16),
                   jax.ShapeDtypeStruct((t, seg(SEG_C)), BF16), jax.ShapeDtypeStruct((t, LANES), F32),
                   jax.ShapeDtypeStruct((GROUP_WIDTH, t), BF16),
                   jax.ShapeDtypeStruct((t, HEAD_PAD), BF16), jax.ShapeDtypeStruct((t, HEAD_PAD), BF16),
                   jax.ShapeDtypeStruct((GROUP_WIDTH, t), BF16)],
        scratch_shapes=[pltpu.VMEM((D_MODEL, IN_COLS_PADDED), BF16), pltpu.VMEM((GROUP_WIDTH, D_MODEL), BF16),
                        pltpu.VMEM((8, GROUP_WIDTH), F32)],
        compiler_params=_cparams("arbitrary"),
        name="inproj",
    )(x, g, w_in_t, conv_w, cos, sin, nq, nkv, wq2, wk, wvt2)


def _scalar_prep_kernel(m_ref, p_ref, qk_ref, place_ref, const_ref,
                        col_ref, row_ref, qa_ref, ka_ref, tref_ref, *, tq):
    s = m_ref.shape[0]
    tref_ref[...] = jnp.zeros_like(tref_ref)
    tile_ref = jnp.zeros((1, LANES), F32)
    m = m_ref[...]
    bias = p_ref[0:1, :]
    a_log = p_ref[1:2, :]
    lane = _lane_iota(m.shape)
    z = m + bias
    logf = jnp.minimum(z, 0.0) - jnp.log(1.0 + jnp.exp(-jnp.abs(z)))
    dt = _softplus(z)
    a = dt * (-jnp.exp(a_log))
    is_f = lane < MISC_DT
    is_dt = (lane >= MISC_DT) & (lane < MISC_DT + N_HEADS)
    v = jnp.where(is_f, logf, jnp.where(is_dt, a, 0.0))
    r = _row_iota((SSM_CHUNK, SSM_CHUNK))
    c = _lane_iota((SSM_CHUNK, SSM_CHUNK))
    tril = jnp.where(r >= c, 1.0, 0.0).astype(BF16)
    carry = jnp.zeros((1, LANES), F32)
    lane_1 = _lane_iota((1, LANES))
    lane_b = _lane_iota((SSM_CHUNK, LANES))
    low = lane_b < HEAD_DIM
    aug_lanes = (lane_b >= AUG_LANE) & (lane_b < AUG_LANE + AUG_TERMS)
    for ci in range(s // SSM_CHUNK):
        rest = v[ci * SSM_CHUNK:(ci + 1) * SSM_CHUNK]
        cs = jnp.zeros((SSM_CHUNK, LANES), F32)
        for _ in range(3):
            term = rest.astype(BF16)
            cs = cs + jnp.dot(tril, term, preferred_element_type=F32)
            rest = rest - term.astype(F32)
        cs = cs + jnp.where(lane_1 < MISC_DT, carry, 0.0)
        carry = cs[SSM_CHUNK - 1:SSM_CHUNK]
        acum = pltpu.roll(cs, COL_ACUM - MISC_DT, 1)
        out = jnp.where(lane_b < MISC_DT, cs * LOG2E,
                        jnp.where(lane_b < COL_ACUM, dt[ci * SSM_CHUNK:(ci + 1) * SSM_CHUNK],
                                  jnp.where(lane_b < COL_ACUM + N_HEADS, acum, 0.0)))
        rows = slice(ci * SSM_CHUNK, (ci + 1) * SSM_CHUNK)
        col_ref[rows, :] = out
        row_ref[0, :, rows] = out.T[:N_SCALAR_ROWS]
        if (ci * SSM_CHUNK) % tq == 0:
            tile_ref = out[0:1, :]
            ti = (ci * SSM_CHUNK) // tq
            tref_ref[0, ti:ti + 1, :] = tile_ref
        c = out - tile_ref
        c_hi = c.astype(BF16)
        r1 = c - c_hi.astype(F32)
        c_mid = r1.astype(BF16)
        c_lo = (r1 - c_mid.astype(F32)).astype(BF16)
        compact = jnp.dot(jnp.concatenate([c_hi, c_mid, c_lo], axis=1), place_ref[...],
                          preferred_element_type=F32) + const_ref[0:1, :]
        for side, o_ref in enumerate((qa_ref, ka_ref)):
            dec = compact[:, side * LANES:(side + 1) * LANES]
            for h in range(N_HEADS):
                pair = qk_ref[rows, side * GROUP_WIDTH + (h // 2) * LANES:
                              side * GROUP_WIDTH + (h // 2 + 1) * LANES].astype(F32)
                feat = pair if h % 2 == 0 else pltpu.roll(pair, HEAD_DIM, 1)
                dec_h = pltpu.roll(dec, AUG_LANE - AUG_TERMS * h, 1)
                group = jnp.where(low, feat, jnp.where(aug_lanes, dec_h, 0.0))
                o_ref[rows, h * LANES:(h + 1) * LANES] = group.astype(o_ref.dtype)


AUG_TERMS = 6


def _fox_placement():
    place = np.zeros((3 * LANES, 2 * LANES), np.float32)
    const = np.zeros((8, 2 * LANES), np.float32)
    for h in range(N_HEADS):
        a0 = AUG_TERMS * h
        for term in range(3):
            place[term * LANES + COL_CUMF + h, a0 + term] = 1.0
            place[term * LANES + COL_CUMF + h, LANES + a0 + 3 + term] = -1.0
            const[0, a0 + 3 + term] = 1.0
            const[0, LANES + a0 + term] = 1.0
    return jnp.asarray(place, BF16), jnp.asarray(const, F32)


def _scalar_prep(misc, params, qk, batch, seq, tq):
    place, const = _fox_placement()
    full = lambda a: pl.BlockSpec(a.shape, lambda b: (0,) * a.ndim)
    return pl.pallas_call(
        functools.partial(_scalar_prep_kernel, tq=tq),
        grid=(batch,),
        in_specs=[pl.BlockSpec((seq, LANES), lambda b: (b, 0)),
                  pl.BlockSpec((8, LANES), lambda b: (0, 0)),
                  pl.BlockSpec((seq, 2 * GROUP_WIDTH), lambda b: (b, 0)),
                  full(place), full(const)],
        out_specs=[pl.BlockSpec((seq, LANES), lambda b: (b, 0)),
                   pl.BlockSpec((1, N_SCALAR_ROWS, seq), lambda b: (b, 0, 0)),
                   pl.BlockSpec((seq, HEAD_PAD), lambda b: (b, 0)),
                   pl.BlockSpec((seq, HEAD_PAD), lambda b: (b, 0)),
                   pl.BlockSpec((1, 8, LANES), lambda b: (b, 0, 0))],
        out_shape=[jax.ShapeDtypeStruct((batch * seq, LANES), F32),
                   jax.ShapeDtypeStruct((batch, N_SCALAR_ROWS, seq), F32),
                   jax.ShapeDtypeStruct((batch * seq, HEAD_PAD), BF16),
                   jax.ShapeDtypeStruct((batch * seq, HEAD_PAD), BF16),
                   jax.ShapeDtypeStruct((batch, 8, LANES), F32)],
        compiler_params=_cparams("parallel"),
        name="scalar_prep",
    )(misc, params, qk, place, const)


def _pair_lanes(col, base, shape):
    lane = _lane_iota(shape)
    return jnp.where(lane < HEAD_DIM, col[:, base:base + 1], col[:, base + 1:base + 2])


def _ssd_kernel(p_ref, col_ref, row_ref, cw_ref, par_ref, o_ref, u_ref):
    s = p_ref.shape[0]
    q = SSM_CHUNK
    gw = GROUP_WIDTH
    xbc = p_ref[:, gw:3 * gw].astype(F32)
    acc = xbc * cw_ref[SSM_CONV - 1:SSM_CONV, :]
    for k in range(1, SSM_CONV):
        acc = acc + _shift_rows(xbc, k) * cw_ref[SSM_CONV - 1 - k:SSM_CONV - k, :]
    u_ref[...] = _silu(acc + cw_ref[SSM_CONV:SSM_CONV + 1, :])

    d_skip = par_ref[0:1, :]
    norm_g = par_ref[1:2, :]
    lane_q = _lane_iota((q, LANES))
    low = lane_q < HEAD_DIM
    tri = _row_iota((q, q)) >= _lane_iota((q, q))

    def chunk(ci, states):
        rows = pl.ds(ci * q, q)
        u = u_ref[rows, :]
        col = col_ref[rows, :]
        bm = u[:, gw:gw + LANES]
        cm = u[:, gw + LANES:gw + 2 * LANES]
        z = p_ref[rows, 0:gw].astype(F32)
        new_states = []
        ys = []
        for g in range(2):
            sel = low if g == 0 else jnp.logical_not(low)
            cg = jnp.where(sel, cm, 0.0).astype(BF16)
            bg = jnp.where(sel, bm, 0.0)
            gmat = lax.dot_general(cg, bm.astype(BF16), (((1,), (1,)), ((), ())),
                                   preferred_element_type=F32)
            xs = u[:, g * LANES:(g + 1) * LANES]
            dt2 = _pair_lanes(col, COL_DT + 2 * g, (q, LANES))
            ac2 = _pair_lanes(col, COL_ACUM + 2 * g, (q, LANES))
            xdt = xs * dt2
            xdt_b = xdt.astype(BF16)
            st = states[g]
            y_off = jnp.dot(cg, st.astype(BF16), preferred_element_type=F32) * jnp.exp(ac2)
            halves = []
            for hh in range(2):
                h = 2 * g + hh
                ac_col = col[:, COL_ACUM + h:COL_ACUM + h + 1]
                ac_row = row_ref[0, COL_ACUM + h:COL_ACUM + h + 1, rows]
                decay = jnp.exp(jnp.where(tri, ac_col - ac_row, -1e30))
                mm = (gmat * decay).astype(BF16)
                halves.append(jnp.dot(mm, xdt_b, preferred_element_type=F32))
            y = jnp.where(low, halves[0], halves[1]) + y_off + d_skip[:, g * LANES:(g + 1) * LANES] * xs
            ys.append(y)
            ac_last = ac2[q - 1:q, :]
            w_end = jnp.exp(ac_last - ac2)
            xw = (xdt * w_end).astype(BF16)
            upd = jnp.dot(bg.T.astype(BF16), xw, preferred_element_type=F32)
            new_states.append(st * jnp.exp(ac_last) + upd)
        yfull = jnp.concatenate(ys, axis=1) * _silu(z)
        o_ref[rows, :] = _rms(yfull, norm_g).astype(o_ref.dtype)
        return tuple(new_states)

    init = (jnp.zeros((LANES, LANES), F32), jnp.zeros((LANES, LANES), F32))
    states = init
    for ci in range(s // q):
        states = chunk(ci, states)


def _ssd_mixer(pc, col, rows, conv_wb, par, batch, seq):
    gw = GROUP_WIDTH
    return pl.pallas_call(
        _ssd_kernel,
        grid=(batch,),
        in_specs=[pl.BlockSpec((seq, 3 * gw), lambda b: (b, 0)),
                  pl.BlockSpec((seq, LANES), lambda b: (b, 0)),
                  pl.BlockSpec((1, N_SCALAR_ROWS, seq), lambda b: (b, 0, 0)),
                  pl.BlockSpec((8, 2 * gw), lambda b: (0, 0)),
                  pl.BlockSpec((8, gw), lambda b: (0, 0))],
        out_specs=pl.BlockSpec((seq, gw), lambda b: (b, 0)),
        out_shape=jax.ShapeDtypeStruct((batch * seq, gw), BF16),
        scratch_shapes=[pltpu.VMEM((seq, 2 * gw), F32)],
        compiler_params=_cparams("parallel"),
        name="ssd_mixer",
    )(pc, col, rows, conv_wb, par)


def _attn_kernel(*refs, fox, tq):
    if fox:
        tref_ref, q_ref, k_ref, vt_ref, o_ref = refs
    else:
        q_ref, k_ref, vt_ref, o_ref = refs
        tref_ref = None
    b = pl.program_id(0)
    i = pl.program_id(1)
    key = _row_iota((tq, tq))
    qry = _lane_iota((tq, tq))
    if fox:
        allowed = key <= qry
    else:
        shift = int(math.log2(MLA_CHUNK))
        allowed = (key >> shift) <= (qry >> shift)
    qs = [q_ref[:, h * LANES:(h + 1) * LANES] for h in range(N_HEADS)]
    ones_rows = jnp.ones((16, tq), BF16)

    def step(j, masked, carry):
        rk = pl.ds(pl.multiple_of(j * tq, tq), tq)
        scores = [lax.dot_general(k_ref[rk, h * LANES:(h + 1) * LANES], qs[h], (((1,), (1,)), ((), ())),
                                  preferred_element_type=F32) for h in range(N_HEADS)]
        probs = []
        for h in range(N_HEADS):
            m, l, _ = carry[h]
            s = scores[h]
            if masked:
                s = jnp.where(allowed, s, -1e30)
            delta = (tref_ref[b, i, h] - tref_ref[b, j, h]) if fox else 0.0
            m_new = jnp.maximum(m, jnp.max(s, axis=0, keepdims=True) + delta)
            alpha = jnp.exp2(m - m_new)
            p = jnp.exp2(s - (m_new - delta))
            probs.append((m_new, alpha, p.astype(BF16)))
        new = []
        for h in range(N_HEADS):
            pair = h // 2
            m_new, alpha, p = probs[h]
            lhs = jnp.concatenate([vt_ref[pair * LANES:(pair + 1) * LANES, rk], ones_rows], axis=0)
            pv = jnp.dot(lhs, p, preferred_element_type=F32)
            new.append((m_new, alpha * carry[h][1] + pv[LANES:LANES + 1], alpha * carry[h][2] + pv[0:LANES]))
        return tuple(new)

    init = tuple((jnp.full((1, tq), -1e30, F32), jnp.zeros((1, tq), F32), jnp.zeros((LANES, tq), F32))
                 for _ in range(N_HEADS))
    carry = lax.fori_loop(0, i, lambda j, c: step(j, False, c), init)
    carry = step(i, True, carry)
    outs = [acc / l for (_, l, acc) in carry]
    top = _row_iota((LANES, tq)) < HEAD_DIM
    o_t = jnp.concatenate([jnp.where(top, outs[0], outs[1]), jnp.where(top, outs[2], outs[3])], axis=0)
    o_ref[...] = o_t.T.astype(o_ref.dtype)


def _attention(q, k, vt, tref, batch, seq, tq, name):
    nq = seq // tq
    fox = tref is not None
    kern = functools.partial(_attn_kernel, fox=fox, tq=tq)
    grid_spec = pltpu.PrefetchScalarGridSpec(
        num_scalar_prefetch=1 if fox else 0,
        grid=(batch, nq),
        in_specs=[pl.BlockSpec((tq, HEAD_PAD), lambda b, i, *_: (b * nq + i, 0)),
                  pl.BlockSpec((seq, HEAD_PAD), lambda b, i, *_: (b, 0)),
                  pl.BlockSpec((GROUP_WIDTH, seq), lambda b, i, *_: (0, b))],
        out_specs=pl.BlockSpec((tq, GROUP_WIDTH), lambda b, i, *_: (b * nq + i, 0)),
    )
    args = ((tref,) if fox else ()) + (q, k, vt)
    return pl.pallas_call(
        kern,
        grid_spec=grid_spec,
        out_shape=jax.ShapeDtypeStruct((batch * seq, GROUP_WIDTH), BF16),
        compiler_params=_cparams("parallel", "arbitrary"),
        name=name,
    )(*args)


def _outproj_kernel(x_ref, ya, yb, yc, yd, w_ref, g_ref, wr_ref, br_ref,
                    x2_ref, h2_ref, rrow_ref, cnt_ref, *, tm, moe_tile):
    y = jnp.concatenate([ya[...], yb[...], yc[...], yd[...]], axis=1)
    x2 = x_ref[...] + jnp.dot(y, w_ref[...], preferred_element_type=F32)
    x2_ref[...] = x2
    h2 = _rms(x2, g_ref[...])
    h2_ref[...] = h2.astype(h2_ref.dtype)
    h_hi = h2.astype(BF16)
    h_lo = (h2 - h_hi.astype(F32)).astype(BF16)
    part = jnp.dot(h_hi, wr_ref[...], preferred_element_type=F32)
    logits = (part[:, 0:LANES] + part[:, LANES:2 * LANES]
              + jnp.dot(h_lo, wr_ref[:, 0:LANES], preferred_element_type=F32) + br_ref[...])
    lt = logits.T[0:ROUTER_ROWS]
    row = _row_iota(lt.shape)
    neg = -1e30
    big = 1 << 20
    gmask = (row >= N_EXPERTS) & (row < N_EXPERTS + N_EXPERT_GROUPS)
    gl = jnp.where(gmask, lt, neg)
    gmax = jnp.max(gl, axis=0, keepdims=True)
    gsum = jnp.sum(jnp.where(gmask, jnp.exp(gl - gmax), 0.0), axis=0, keepdims=True)
    g_w = 1.0 / gsum
    g_idx = jnp.min(jnp.where(gmask & (gl == gmax), row, big), axis=0, keepdims=True) - N_EXPERTS
    emask = (row < N_EXPERTS) & ((row >> int(math.log2(EXPERTS_PER_GROUP))) == g_idx)
    el = jnp.where(emask, lt, neg)
    e1v = jnp.max(el, axis=0, keepdims=True)
    esum = jnp.sum(jnp.where(emask, jnp.exp(el - e1v), 0.0), axis=0, keepdims=True)
    i1 = jnp.min(jnp.where(emask & (el == e1v), row, big), axis=0, keepdims=True)
    el2 = jnp.where(row == i1, neg, el)
    e2v = jnp.max(el2, axis=0, keepdims=True)
    i2 = jnp.min(jnp.where(emask & (row != i1) & (el2 == e2v), row, big), axis=0, keepdims=True)
    p1 = 1.0 / esum
    p2 = jnp.exp(e2v - e1v) / esum
    w1 = g_w * (p1 / (p1 + p2))
    w2 = g_w * (p2 / (p1 + p2))
    out_row = _row_iota(rrow_ref.shape)
    rrow_ref[...] = jnp.where(out_row == 0, i1.astype(F32),
                              jnp.where(out_row == 1, i2.astype(F32),
                                        jnp.where(out_row == 2, w1, jnp.where(out_row == 3, w2, 0.0))))
    step = pl.program_id(0)

    @pl.when(step == 0)
    def _():
        cnt_ref[...] = jnp.zeros_like(cnt_ref)

    chosen = jnp.where((row == i1) | (row == i2), 1.0, 0.0).astype(BF16)
    tiles = tm // moe_tile
    tile_of = (_row_iota((tm, LANES)) >> int(math.log2(moe_tile))) + step * tiles
    to_tile = jnp.where(_lane_iota((tm, LANES)) == tile_of, 1.0, 0.0).astype(BF16)
    counts = jnp.dot(chosen, to_tile, preferred_element_type=F32)
    cnt_ref[...] += counts[0:N_EXPERTS]


def _outproj(x, ya, yb, yc, yd, w, g, wr, br, tm, moe_tile):
    t = x.shape[0]
    full = lambda a: pl.BlockSpec(a.shape, lambda i: (0, 0))
    tile = lambda wd: pl.BlockSpec((tm, wd), lambda i: (i, 0))
    return pl.pallas_call(
        functools.partial(_outproj_kernel, tm=tm, moe_tile=moe_tile),
        grid=(t // tm,),
        in_specs=[tile(D_MODEL)] + [tile(GROUP_WIDTH)] * 4 + [full(w), full(g), full(wr), full(br)],
        out_specs=[tile(D_MODEL), tile(D_MODEL), pl.BlockSpec((8, tm), lambda i: (0, i)),
                   pl.BlockSpec((N_EXPERTS, LANES), lambda i: (0, 0))],
        out_shape=[jax.ShapeDtypeStruct((t, D_MODEL), F32), jax.ShapeDtypeStruct((t, D_MODEL), BF16),
                   jax.ShapeDtypeStruct((8, t), F32), jax.ShapeDtypeStruct((N_EXPERTS, LANES), F32)],
        compiler_params=_cparams("arbitrary"),
        name="outproj_router",
    )(x, ya, yb, yc, yd, w, g, wr, br)


MOE_TILE = 256
CHUNK = 8
LOCAL_ROWS = 2 * MOE_TILE + 256
PACKED = D_MODEL // 2
XS_WIDTH = PACKED
U32 = jnp.uint32


def _pack_bf16_pairs(x, exact=False):
    if not exact:
        x = x.astype(BF16).astype(F32)
    half = x.shape[1] // 2
    lo = lax.bitcast_convert_type(x[:, :half], U32)
    hi = lax.bitcast_convert_type(x[:, half:], U32)
    return hi | (lo >> 16)


def _unpack_bf16_pairs(words):
    lo = lax.bitcast_convert_type(words << 16, F32)
    hi = lax.bitcast_convert_type(words & U32(0xFFFF0000), F32)
    return jnp.concatenate([lo, hi], axis=1).astype(BF16)


COPY_ROWS = (2 * CHUNK, CHUNK)


def _route_kernel(r_ref, cnt_ref, lrow_ref, lcol_ref, tab_ref, ctab_ref, meta_ref,
                  loff_ref, goff_ref, n8_ref, *, tm, nbp, tiles):
    i = pl.program_id(0)
    hi = lax.Precision.HIGHEST

    @pl.when(i == 0)
    def _():
        cnt = cnt_ref[...]
        n8 = jnp.floor((cnt + (CHUNK - 1)) * (1.0 / CHUNK)) * CHUNK
        er = _row_iota((N_EXPERTS, N_EXPERTS))
        ec = _lane_iota((N_EXPERTS, N_EXPERTS))
        below = jnp.where(er > ec, 1.0, 0.0)
        loff = jnp.dot(below, n8, preferred_element_type=F32, precision=hi)
        rows_e = jnp.sum(n8, axis=-1, keepdims=True) + jnp.zeros_like(n8)
        padded = jnp.floor((rows_e + (MOE_ROWS - 1)) * (1.0 / MOE_ROWS)) * MOE_ROWS
        e_start = jnp.dot(below, padded, preferred_element_type=F32, precision=hi)
        tr = _row_iota((LANES, LANES))
        tc = _lane_iota((LANES, LANES))
        earlier = jnp.where(tr < tc, 1.0, 0.0)
        goff = e_start + jnp.dot(n8, earlier, preferred_element_type=F32, precision=hi)
        loff_ref[...] = loff
        goff_ref[...] = goff
        n8_ref[...] = n8
        big = jnp.floor(n8 * (0.5 / CHUNK))
        small = n8 * (1.0 / CHUNK) - 2.0 * big
        row_t = _row_iota(tab_ref.shape)
        tab_ref[...] = jnp.where(row_t == 0, jnp.sum(big, axis=0, keepdims=True),
                                 jnp.where(row_t == 1, jnp.sum(small, axis=0, keepdims=True), 0.0)).astype(I32)
        reps = nbp // LANES
        pend_b = jnp.concatenate([e_start + padded] * reps, axis=1)
        vend_b = jnp.concatenate([e_start + rows_e] * reps, axis=1)
        used_b = jnp.concatenate([padded] * reps, axis=1) > 0.0
        b0 = (_lane_iota((N_EXPERTS, nbp)) * MOE_ROWS).astype(F32)
        bexp = jnp.sum(jnp.where(pend_b <= b0, 1.0, 0.0), axis=0, keepdims=True)
        bexp = jnp.minimum(bexp, N_EXPERTS - 1.0)
        e_b = _row_iota((N_EXPERTS, nbp)).astype(F32)
        is_e = e_b == bexp
        vend = jnp.sum(jnp.where(is_e, vend_b, 0.0), axis=0, keepdims=True)
        nvalid = jnp.clip(vend - b0[0:1], 0.0, float(MOE_ROWS))
        total = jnp.max(pend_b, axis=0, keepdims=True) * (1.0 / MOE_ROWS)
        order = jnp.sum(jnp.where(used_b & (e_b < bexp), 1.0, 0.0), axis=0, keepdims=True)
        nxt = jnp.min(jnp.where(used_b & (e_b > bexp), e_b, float(N_EXPERTS)), axis=0, keepdims=True)
        row = _row_iota((8, nbp))
        meta = jnp.where(row == 0, bexp, jnp.where(row == 1, nvalid, jnp.where(row == 2, total,
                         jnp.where(row == 3, order, jnp.where(row == 4, nxt, 0.0)))))
        meta_ref[...] = meta.astype(I32)

    su = jnp.where(_row_iota((tm, tm)) < _lane_iota((tm, tm)), 1.0, 0.0).astype(BF16)
    incl = jnp.where(_row_iota((N_EXPERTS, N_EXPERTS)) >= _lane_iota((N_EXPERTS, N_EXPERTS)), 1.0, 0.0)
    cidx = _lane_iota((N_EXPERTS, LANES)).astype(F32)
    e_iota = _row_iota((N_EXPERTS, tm))
    out_row = _row_iota((LANES, tm))
    for k in range(tiles):
        tok = slice(k * tm, (k + 1) * tm)
        oh0 = e_iota == r_ref[0:1, tok].astype(I32)
        oh1 = e_iota == r_ref[1:2, tok].astype(I32)
        oh = jnp.where(oh0 | oh1, 1.0, 0.0)
        tile_lane = _lane_iota((N_EXPERTS, LANES)) == i * tiles + k
        before = jnp.dot(oh.astype(BF16), su, preferred_element_type=F32)
        base = jnp.sum(jnp.where(tile_lane, loff_ref[...], 0.0), axis=-1, keepdims=True) + before
        d0 = jnp.sum(jnp.where(oh0, base, 0.0), axis=0, keepdims=True)
        d1 = jnp.sum(jnp.where(oh1, base, 0.0), axis=0, keepdims=True)
        lrow_ref[k, 0:1, :] = d0.astype(I32)
        lrow_ref[k, 1:2, :] = d1.astype(I32)
        pick_tile = lambda ref: jnp.sum(jnp.where(tile_lane, ref[...], 0.0), axis=-1, keepdims=True)
        nch = pick_tile(n8_ref) * (1.0 / CHUNK)
        n_big = jnp.floor(nch * 0.5)
        n_small = nch - 2.0 * n_big
        loff_t = pick_tile(loff_ref)
        goff_t = pick_tile(goff_ref)
        for c, (n, rows, first) in enumerate(((n_big, COPY_ROWS[0], 0.0),
                                              (n_small, COPY_ROWS[1], n_big * COPY_ROWS[0]))):
            cend = jnp.dot(incl, n + jnp.zeros((N_EXPERTS, LANES), F32), preferred_element_type=F32, precision=hi)
            cstart = cend - n
            mine = (cidx >= cstart) & (cidx < cend)
            step_rows = first + (cidx - cstart) * rows
            ctab_ref[k, 2 * c:2 * c + 1, :] = jnp.sum(jnp.where(mine, loff_t + step_rows, 0.0), axis=0,
                                                      keepdims=True).astype(I32)
            ctab_ref[k, 2 * c + 1:2 * c + 2, :] = jnp.sum(jnp.where(mine, goff_t + step_rows, 0.0), axis=0,
                                                          keepdims=True).astype(I32)
        lcol_ref[tok, :] = jnp.where(out_row == 0, d0, jnp.where(out_row == 1, d1,
                                     jnp.where(out_row == 2, r_ref[2:3, tok],
                                               jnp.where(out_row == 3, r_ref[3:4, tok], 0.0)))).T


def _route(rrow, cnt, tm, nbp):
    t = rrow.shape[1]
    nt = t // tm
    tiles = next(r for r in (4, 2, 1) if nt % r == 0)
    kern = functools.partial(_route_kernel, tm=tm, nbp=nbp, tiles=tiles)
    return pl.pallas_call(
        kern,
        grid=(nt // tiles,),
        in_specs=[pl.BlockSpec((8, tiles * tm), lambda i: (0, i)),
                  pl.BlockSpec((N_EXPERTS, LANES), lambda i: (0, 0))],
        out_specs=[pl.BlockSpec((tiles, 2, tm), lambda i: (i, 0, 0)),
                   pl.BlockSpec((tiles * tm, LANES), lambda i: (i, 0)),
                   pl.BlockSpec((8, LANES), lambda i: (0, 0)),
                   pl.BlockSpec((tiles, 2 * len(COPY_ROWS), LANES), lambda i: (i, 0, 0)),
                   pl.BlockSpec((8, nbp), lambda i: (0, 0))],
        out_shape=[jax.ShapeDtypeStruct((nt, 2, tm), I32), jax.ShapeDtypeStruct((t, LANES), F32),
                   jax.ShapeDtypeStruct((8, LANES), I32), jax.ShapeDtypeStruct((nt, 2 * len(COPY_ROWS), LANES), I32),
                   jax.ShapeDtypeStruct((8, nbp), I32)],
        scratch_shapes=[pltpu.VMEM((N_EXPERTS, LANES), F32)] * 3,
        compiler_params=_cparams("arbitrary"),
        name="moe_route",
    )(rrow, cnt)


def _chunk_copies(tabs, i, local_ref, global_ref, sem, to_global, action):
    tab_ref, ctab_ref = tabs
    for k, rows in enumerate(COPY_ROWS):
        count = tab_ref[k, i]

        def copy(lo, go, rows=rows):
            lsl = local_ref.at[pl.ds(pl.multiple_of(lo, CHUNK), rows)]
            gsl = global_ref.at[pl.ds(pl.multiple_of(go, CHUNK), rows)]
            return pltpu.make_async_copy(lsl, gsl, sem) if to_global else pltpu.make_async_copy(gsl, lsl, sem)

        if action == "wait":
            def one(c, c1, copy=copy):
                copy(0, 0).wait()
                return c1
        else:
            def one(c, c1, copy=copy, k=k):
                copy(ctab_ref[i, 2 * k, c], ctab_ref[i, 2 * k + 1, c]).start()
                return c1

        lax.fori_loop(0, count, one, 0)


def _scatter_kernel(tab_ref, ctab_ref, meta_ref, lrow_ref, h_ref, xs_ref, buf_ref, zero_ref, sem, zsem, *, tm, nb):
    i = pl.program_id(0)
    tabs = (tab_ref, ctab_ref)

    @pl.when(i == 0)
    def _():
        zero_ref[...] = jnp.zeros_like(zero_ref)
        n_used = meta_ref[2, 0]

        def each_piece(action):
            def body(b, c):
                for piece in range(MOE_ROWS // ZERO_ROWS):
                    @pl.when((b < n_used) & (meta_ref[1, b] < (piece + 1) * ZERO_ROWS))
                    def _():
                        start = pl.multiple_of(b * MOE_ROWS + piece * ZERO_ROWS, ZERO_ROWS)
                        cp = pltpu.make_async_copy(zero_ref, xs_ref.at[pl.ds(start, ZERO_ROWS)], zsem)
                        getattr(cp, action)()
                return c
            lax.fori_loop(0, nb, body, 0)

        each_piece("start")
        each_piece("wait")

    rows = _row_iota((LOCAL_ROWS, tm))

    def drain(tile, slot):
        _chunk_copies(tabs, tile, buf_ref.at[slot], xs_ref, sem.at[slot], True, "wait")

    for slot in range(2):
        tile = 2 * i + slot
        p0 = rows == lrow_ref[slot, 0:1, :]
        p1 = rows == lrow_ref[slot, 1:2, :]
        perm = jnp.where(p0 | p1, 1.0, 0.0).astype(BF16)
        sorted_rows = jnp.dot(perm, h_ref[slot * tm:(slot + 1) * tm, :],
                              preferred_element_type=F32)

        @pl.when(i > 0)
        def _():
            drain(tile - 2, slot)

        buf = buf_ref.at[slot]
        buf[...] = _pack_bf16_pairs(sorted_rows, exact=True)
        _chunk_copies(tabs, tile, buf, xs_ref, sem.at[slot], True, "start")

    @pl.when(i == pl.num_programs(0) - 1)
    def _():
        drain(2 * i, 0)
        drain(2 * i + 1, 1)


def _scatter(tab, ctab, meta, lrow, h2, tm, nb):
    t = h2.shape[0]
    kern = functools.partial(_scatter_kernel, tm=tm, nb=nb)
    grid_spec = pltpu.PrefetchScalarGridSpec(
        num_scalar_prefetch=3,
        grid=(t // (2 * tm),),
        in_specs=[pl.BlockSpec((2, 2, tm), lambda i, *_: (i, 0, 0)),
                  pl.BlockSpec((2 * tm, D_MODEL), lambda i, *_: (i, 0))],
        out_specs=pl.BlockSpec(memory_space=pl.ANY),
        scratch_shapes=[pltpu.VMEM((2, LOCAL_ROWS, XS_WIDTH), U32), pltpu.VMEM((ZERO_ROWS, XS_WIDTH), U32),
                        pltpu.SemaphoreType.DMA((2,)), pltpu.SemaphoreType.DMA],
    )
    return pl.pallas_call(
        kern,
        grid_spec=grid_spec,
        out_shape=jax.ShapeDtypeStruct((nb * MOE_ROWS, XS_WIDTH), U32),
        compiler_params=_cparams("arbitrary"),
        name="moe_scatter",
    )(tab, ctab, meta, lrow, h2)


def _ffn_kernel(meta_ref, x_ref, wg_ref, wu_ref, wd_ref, o_ref, wgu_b, wd_b, wg_f, wu_f, wd_f, wsem, *, layer):
    b = pl.program_id(0)
    live = b < meta_ref[2, 0]
    expert = meta_ref[0, b]
    prev = meta_ref[0, jnp.maximum(b - 1, 0)]

    def fetch(e, slot):
        return [pltpu.make_async_copy(src.at[layer, e], dst.at[slot], wsem.at[slot])
                for src, dst in ((wg_ref, wg_f), (wu_ref, wu_f), (wd_ref, wd_f))]

    def first_block(slot):
        @pl.when(b == 0)
        def _():
            for cp in fetch(expert, slot):
                cp.start()

        for cp in fetch(expert, slot):
            cp.wait()
        wgu_b[:, 0:EXPERT_FF] = wg_f[slot].astype(BF16)
        wgu_b[:, EXPERT_FF:2 * EXPERT_FF] = wu_f[slot].astype(BF16)
        wd_b[...] = wd_f[slot].astype(BF16)
        nxt = meta_ref[4, b]

        @pl.when(nxt < N_EXPERTS)
        def _():
            for cp in fetch(nxt, 1 - slot):
                cp.start()

    changed = live & ((b == 0) | (expert != prev))
    odd = (meta_ref[3, b] & 1) == 1

    @pl.when(changed & jnp.logical_not(odd))
    def _():
        first_block(0)

    @pl.when(changed & odd)
    def _():
        first_block(1)

    @pl.when(live)
    def _():
        x = _unpack_bf16_pairs(x_ref[...])
        gu = jnp.dot(x, wgu_b[...], preferred_element_type=F32)
        act = (_silu(gu[:, 0:EXPERT_FF]) * gu[:, EXPERT_FF:2 * EXPERT_FF]).astype(BF16)
        y = jnp.dot(act, wd_b[...], preferred_element_type=F32)
        o_ref[...] = _pack_bf16_pairs(y)


def _ffn(meta, xs, wg, wu, wd, layer, nb):
    def blk(b, m):
        return (jnp.maximum(jnp.minimum(b, m[2, 0] - 1), 0), 0)

    grid_spec = pltpu.PrefetchScalarGridSpec(
        num_scalar_prefetch=1,
        grid=(nb,),
        in_specs=[pl.BlockSpec((MOE_ROWS, XS_WIDTH), blk)] + [pl.BlockSpec(memory_space=pl.ANY)] * 3,
        out_specs=pl.BlockSpec((MOE_ROWS, PACKED), blk),
        scratch_shapes=[pltpu.VMEM((D_MODEL, 2 * EXPERT_FF), BF16), pltpu.VMEM((EXPERT_FF, D_MODEL), BF16),
                        pltpu.VMEM((2, D_MODEL, EXPERT_FF), F32), pltpu.VMEM((2, D_MODEL, EXPERT_FF), F32),
                        pltpu.VMEM((2, EXPERT_FF, D_MODEL), F32), pltpu.SemaphoreType.DMA((2,))],
    )
    return pl.pallas_call(
        functools.partial(_ffn_kernel, layer=layer),
        grid_spec=grid_spec,
        out_shape=jax.ShapeDtypeStruct((nb * MOE_ROWS, PACKED), U32),
        compiler_params=_cparams("arbitrary"),
        name="moe_experts",
    )(meta, xs, wg, wu, wd)


def _gather_kernel(tab_ref, ctab_ref, lcol_ref, x_ref, g_ref, ys_ref, o_ref, buf_ref, sem, *, tm, final):
    i = pl.program_id(0)

    last = pl.num_programs(0) - 1

    def fetch(tile, slot, action):
        _chunk_copies((tab_ref, ctab_ref), tile, buf_ref.at[slot], ys_ref, sem.at[slot], False, action)

    @pl.when(i == 0)
    def _():
        buf_ref[...] = jnp.zeros_like(buf_ref)
        fetch(0, 0, "start")

    col = _lane_iota((tm, LOCAL_ROWS)).astype(F32)
    for slot in range(2):
        tile = 2 * i + slot
        tok = slice(slot * tm, (slot + 1) * tm)
        if slot == 0:
            fetch(tile + 1, 1, "start")
        else:
            @pl.when(i < last)
            def _():
                fetch(tile + 1, 0, "start")

        lc = lcol_ref[tok, :]
        pick0 = jnp.where(col == lc[:, 0:1], 1.0, 0.0).astype(BF16)
        pick1 = jnp.where(col == lc[:, 1:2], 1.0, 0.0).astype(BF16)
        fetch(tile, slot, "wait")
        y = _unpack_bf16_pairs(buf_ref[slot])
        both = jnp.dot(jnp.concatenate([pick0, pick1], axis=0), y, preferred_element_type=F32)
        x = x_ref[tok, :] + lc[:, 2:3] * both[0:tm] + lc[:, 3:4] * both[tm:2 * tm]
        o_ref[tok, :] = _rms(x, g_ref[...]) if final else x


def _gather(tab, ctab, lcol, x2, g, ys, tm, final):
    t = x2.shape[0]
    kern = functools.partial(_gather_kernel, tm=tm, final=final)
    grid_spec = pltpu.PrefetchScalarGridSpec(
        num_scalar_prefetch=2,
        grid=(t // (2 * tm),),
        in_specs=[pl.BlockSpec((2 * tm, LANES), lambda i, *_: (i, 0)),
                  pl.BlockSpec((2 * tm, D_MODEL), lambda i, *_: (i, 0)),
                  pl.BlockSpec((1, D_MODEL), lambda i, *_: (0, 0)),
                  pl.BlockSpec(memory_space=pl.ANY)],
        out_specs=pl.BlockSpec((2 * tm, D_MODEL), lambda i, *_: (i, 0)),
        scratch_shapes=[pltpu.VMEM((2, LOCAL_ROWS, PACKED), U32), pltpu.SemaphoreType.DMA((2,))],
    )
    return pl.pallas_call(
        kern,
        grid_spec=grid_spec,
        out_shape=jax.ShapeDtypeStruct((t, D_MODEL), F32),
        compiler_params=_cparams("arbitrary"),
        name="moe_combine",
    )(tab, ctab, lcol, x2, g, ys)


def _pad_rows(a, rows=8):
    return jnp.zeros((rows, a.shape[-1]), F32).at[:a.shape[0]].set(a.astype(F32))


def _arrange_mla(w_uq, w_ukv):
    half = MLA_ROPE // 2
    qd = MLA_NOPE + MLA_ROPE
    wq, wqs, wk, wv = [], [], [], []
    zq = jnp.zeros((MLA_Q_LORA, LANES - qd), w_uq.dtype)
    zk = jnp.zeros((MLA_KV_LORA, LANES - MLA_NOPE), w_ukv.dtype)
    for h in range(N_HEADS):
        q = w_uq[:, h * qd:(h + 1) * qd]
        nope, rope = q[:, :MLA_NOPE], q[:, MLA_NOPE:]
        wq.append(jnp.concatenate([nope, rope, zq], axis=1))
        wqs.append(jnp.concatenate([jnp.zeros_like(nope), -rope[:, half:], rope[:, :half], zq], axis=1))
        kv = w_ukv[:, h * 2 * MLA_NOPE:(h + 1) * 2 * MLA_NOPE]
        wk.append(jnp.concatenate([kv[:, :MLA_NOPE], zk], axis=1))
        wv.append(kv[:, MLA_NOPE:])
    cat = lambda xs: jnp.concatenate(xs, axis=1).astype(BF16)
    return cat(wq), cat(wqs), cat(wk), cat(wv).T


def kernel(x, positions, norm_mix, w_in, conv_a, fox_forget_bias, ssm_conv_w, ssm_conv_b, ssm_dt_bias,
           ssm_a_log, ssm_d, ssm_norm, mla_q_norm, mla_kv_norm, mla_w_uq, mla_w_ukv, w_out, norm_ffn,
           router_group_w, router_group_b, router_expert_w, router_expert_b, expert_w_gate, expert_w_up,
           expert_w_down, norm_final):
    batch, seq, d = x.shape
    t = batch * seq
    depth = w_in.shape[0]
    tm = min(ROW_TILE, seq)
    tq = min(ATTN_TQ, seq)
    tmd = MOE_TILE
    assert d == D_MODEL and seq % tm == 0 and seq % tq == 0 and tq % SSM_CHUNK == 0, (x.shape,)
    assert seq // tq <= 8 and tm % (LANES * ROPE_PACK) == 0, (seq, tm)
    assert t % (2 * tmd) == 0 and t // tmd <= LANES and tm % tmd == 0, (t, tmd)
    max_rows = 2 * t + (CHUNK - 1) * N_EXPERTS * (t // tmd) + N_EXPERTS * (MOE_ROWS - 1)
    nb = -(-max_rows // MOE_ROWS)
    nbp = -(-nb // LANES) * LANES

    xf = x.reshape(t, d)
    cos, sin = _rope_tables(positions, tm)
    w_in_t = jnp.swapaxes(w_in, 1, 2)

    for l in range(depth):
        wq, wqs, wk, wv = _arrange_mla(mla_w_uq[l], mla_w_ukv[l])
        ya, pb, pc, misc, fox_vt, q, k, v = _inproj(
            xf, norm_mix[l][None, :], w_in_t, l, _pad_rows(conv_a[l]), cos, sin, mla_q_norm[l][None, :],
            mla_kv_norm[l][None, :], jnp.concatenate([wq, wqs], axis=1), wk, wv, tm, seq)

        sp = jnp.zeros((8, LANES), F32)
        sp = sp.at[0, MISC_F:MISC_F + N_HEADS].set(fox_forget_bias[l])
        sp = sp.at[0, MISC_DT:MISC_DT + N_HEADS].set(ssm_dt_bias[l])
        sp = sp.at[1, MISC_DT:MISC_DT + N_HEADS].set(ssm_a_log[l])
        col, rows, fox_q, fox_k, tref = _scalar_prep(misc, sp, pb, batch, seq, tq)

        yb = _attention(fox_q, fox_k, fox_vt, tref, batch, seq, tq, "fox_attention")
        conv_wb = _pad_rows(jnp.concatenate([ssm_conv_w[l], ssm_conv_b[l][None, :]], axis=0))
        ssd_par = _pad_rows(jnp.stack([jnp.repeat(ssm_d[l], HEAD_DIM), ssm_norm[l]]))
        yc = _ssd_mixer(pc, col, rows, conv_wb, ssd_par, batch, seq)
        yd = _attention(q, k, v, None, batch, seq, tq, "mla_attention")

        pad = jnp.zeros((d, LANES - N_EXPERTS - N_EXPERT_GROUPS), F32)
        wr = jnp.concatenate([router_expert_w[l], router_group_w[l], pad], axis=1)
        wr_hi = wr.astype(BF16)
        wr = jnp.concatenate([wr_hi, (wr - wr_hi.astype(F32)).astype(BF16)], axis=1)
        br = jnp.concatenate([router_expert_b[l], router_group_b[l], pad[0]])[None, :]
        x2, h2, rrow, cnt = _outproj(xf, ya, yb, yc, yd, w_out[l].astype(BF16), norm_ffn[l][None, :], wr, br,
                                     tm, tmd)

        lrow, lcol, tab, ctab, meta = _route(rrow, cnt, tmd, nbp)
        xs = _scatter(tab, ctab, meta, lrow, h2, tmd, nb)
        ys = _ffn(meta, xs, expert_w_gate, expert_w_up, expert_w_down, l, nb)
        final = l == depth - 1
        xf = _gather(tab, ctab, lcol, x2, norm_final[None, :], ys, tmd, final)

    return xf.reshape(batch, seq, d)
```

```python
import functools
import math

import jax
import jax.numpy as jnp
import numpy as np
from jax import lax
from jax.experimental import pallas as pl
from jax.experimental.pallas import tpu as pltpu

F32 = jnp.float32
BF16 = jnp.bfloat16
I32 = jnp.int32

LANES = 128
VMEM_LIMIT_BYTES = 56 * 1024 * 1024

D_MODEL = 1024
RMS_EPS = 1e-6
LOG2E = math.log2(math.e)
GROUP_WIDTH = 256
HEAD_DIM = 64
N_HEADS = 4

CONV_A_WIDTH = 3
SSM_CONV = 4
SSM_STATE = 64
SSM_CHUNK = 256

MLA_NOPE = 64
MLA_ROPE = 32
MLA_Q_LORA = 256
MLA_KV_LORA = 128
ROPE_BASE = 10000.0
MLA_CHUNK = 64
ATTN_TQ = 512
ROW_TILE = 512

N_EXPERT_GROUPS = 4
EXPERTS_PER_GROUP = 8
N_EXPERTS = 32
EXPERT_FF = 256
ROUTER_ROWS = 48
MOE_ROWS = 512
ZERO_ROWS = 256

SEG_A = (0, 768)
SEG_B = (768, 1280)
SEG_C = (1280, 2048)
SEG_D = (2048, 2432)
SEG_M = (2432, 2560)
SEG_M2 = (2560, 2688)
IN_COLS_PADDED = 2688
HEAD_PAD = N_HEADS * LANES
AUG_LANE = HEAD_DIM
MISC_F = 0
MISC_DT = 4
MISC_ROPE = 64
COL_CUMF = 0
COL_DT = 4
COL_ACUM = 8
N_SCALAR_ROWS = 16


def _cparams(*sem):
    return pltpu.CompilerParams(dimension_semantics=sem, vmem_limit_bytes=VMEM_LIMIT_BYTES)


def _lane_iota(shape):
    return lax.broadcasted_iota(I32, shape, len(shape) - 1)


def _row_iota(shape):
    return lax.broadcasted_iota(I32, shape, 0)


def _rms(x, g):
    ms = jnp.mean(x * x, axis=-1, keepdims=True)
    return x * lax.rsqrt(ms + RMS_EPS) * g


def _silu(x):
    return x / (1.0 + jnp.exp(-x))


def _softplus(x):
    return jnp.maximum(x, 0.0) + jnp.log(1.0 + jnp.exp(-jnp.abs(x)))


def _shift_rows(x, k):
    rolled = pltpu.roll(x, k, 0)
    return jnp.where(_row_iota(x.shape) >= k, rolled, 0.0)


def _rope_kernel(pos_ref, freq_ref, cos_ref, sin_ref):
    q = pos_ref.shape[2]
    lane = _lane_iota((q, LANES))
    group = lane >> int(math.log2(MLA_ROPE))
    rows = jnp.concatenate([pos_ref[0], jnp.zeros((8 - ROPE_PACK, q), F32)], axis=0)
    cols = jnp.concatenate([rows] * (LANES // 8), axis=0).T
    pos = cols[:, ROPE_PACK - 1:ROPE_PACK]
    for k in range(ROPE_PACK - 2, -1, -1):
        pos = jnp.where(group == k, cols[:, k:k + 1], pos)
    ang = pos * freq_ref[...]
    c = jnp.cos(ang)
    s = jnp.sin(ang)
    rope = (lane >= MISC_ROPE) & (lane < MISC_ROPE + MLA_ROPE)
    for k in range(ROPE_PACK):
        shift = (MISC_ROPE - MLA_ROPE * k) % LANES
        ck = pltpu.roll(c, shift, 1) if shift else c
        sk = pltpu.roll(s, shift, 1) if shift else s
        cos_ref[k * q:(k + 1) * q, :] = jnp.where(rope, ck, jnp.where(lane < MISC_ROPE, 1.0, 0.0))
        sin_ref[k * q:(k + 1) * q, :] = jnp.where(rope, sk, 0.0)


ROPE_PACK = LANES // MLA_ROPE


def _rope_tables(positions, tm):
    t = positions.size
    pos = positions.astype(F32).reshape(t // tm, ROPE_PACK, tm // ROPE_PACK)
    inv = ROPE_BASE ** (-np.arange(0, MLA_ROPE, 2, dtype=np.float32) / MLA_ROPE)
    freq = np.tile(np.concatenate([inv, inv]), ROPE_PACK)[None, :].astype(np.float32)
    return pl.pallas_call(
        _rope_kernel,
        grid=(t // tm,),
        in_specs=[pl.BlockSpec((1, ROPE_PACK, tm // ROPE_PACK), lambda i: (i, 0, 0)),
                  pl.BlockSpec((1, LANES), lambda i: (0, 0))],
        out_specs=[pl.BlockSpec((tm, LANES), lambda i: (i, 0))] * 2,
        out_shape=[jax.ShapeDtypeStruct((t, LANES), F32)] * 2,
        compiler_params=_cparams("parallel"),
        name="rope_tables",
    )(pos, jnp.asarray(freq))


def _arrange_w_in_kernel(wt_ref, w_ref, wvt_ref):
    gw = GROUP_WIDTH
    rows = lambda lo, hi: wt_ref[0, lo:hi, :]

    def put(seg0, piece_t):
        w_ref[:, seg0:seg0 + piece_t.shape[0]] = piece_t.T.astype(BF16)

    for j in range(3):
        put(SEG_A[0] + j * gw, rows(j * gw, (j + 1) * gw))
    put(SEG_B[0], rows(3 * gw, 4 * gw) * (HEAD_DIM ** -0.5 * LOG2E))
    put(SEG_B[0] + gw, rows(4 * gw, 5 * gw))
    wvt_ref[...] = rows(5 * gw, 6 * gw).astype(BF16)
    c0 = 6 * gw
    win = pltpu.roll(rows(c0, c0 + 3 * gw + 8), 3 * gw + 8 - N_HEADS, 0)
    for j in range(3):
        put(SEG_C[0] + j * gw, win[j * gw:(j + 1) * gw])
    d0 = c0 + 3 * gw + 8
    put(SEG_D[0], rows(d0, d0 + MLA_Q_LORA))
    put(SEG_D[0] + MLA_Q_LORA, rows(d0 + MLA_Q_LORA, d0 + MLA_Q_LORA + MLA_KV_LORA))
    kr0 = d0 + MLA_Q_LORA + MLA_KV_LORA
    half = MLA_ROPE // 2
    first8 = jnp.where(_row_iota((8, D_MODEL)) < N_HEADS, rows(c0, c0 + 8), rows(d0 - 8, d0))
    zeros = lambda n: jnp.zeros((n, D_MODEL), F32)
    misc_t = jnp.concatenate([first8, zeros(MISC_ROPE - 8), rows(kr0, kr0 + MLA_ROPE),
                              zeros(LANES - MISC_ROPE - MLA_ROPE)], axis=0)
    misc2_t = jnp.concatenate([zeros(MISC_ROPE), -rows(kr0 + half, kr0 + MLA_ROPE), rows(kr0, kr0 + half),
                               zeros(LANES - MISC_ROPE - MLA_ROPE)], axis=0)
    put(SEG_M[0], misc_t)
    put(SEG_M2[0], misc2_t)


def _inproj_kernel(x_ref, g_ref, wt_ref, cw_ref, cos_ref, sin_ref, nq_ref, nkv_ref, wq2_ref, wk_ref,
                   wvt2_ref, oa, ob, oc, om, ovt, q_ref, k_ref, vt_ref, w_ref, wvt_ref, halo_ref,
                   *, tm, tiles_per_seq):
    @pl.when(pl.program_id(0) == 0)
    def _():
        _arrange_w_in_kernel(wt_ref, w_ref, wvt_ref)

    h = _rms(x_ref[...], g_ref[...]).astype(BF16)
    for o, (lo, hi) in ((ob, SEG_B), (oc, SEG_C)):
        o[...] = jnp.dot(h, w_ref[:, lo:hi], preferred_element_type=F32).astype(o.dtype)

    gw = GROUP_WIDTH
    pa = jnp.dot(h, w_ref[:, SEG_A[0]:SEG_A[1]], preferred_element_type=F32)
    cv = pa[:, gw:2 * gw] * pa[:, 2 * gw:3 * gw]

    @pl.when(pl.program_id(0) % tiles_per_seq == 0)
    def _():
        halo_ref[...] = jnp.zeros_like(halo_ref)

    halo = halo_ref[...]
    row8 = _row_iota(halo.shape)
    acc = cv * cw_ref[CONV_A_WIDTH - 1:CONV_A_WIDTH, :]
    for k in range(1, CONV_A_WIDTH):
        shifted = pltpu.roll(cv, k, 0)
        top = jnp.where(row8 < k, pltpu.roll(halo, k, 0), shifted[0:8])
        shifted = jnp.concatenate([top, shifted[8:]], axis=0)
        acc = acc + shifted * cw_ref[CONV_A_WIDTH - 1 - k:CONV_A_WIDTH - k, :]
    halo_ref[...] = cv[tm - 8:tm]
    oa[...] = (pa[:, 0:gw] * acc).astype(oa.dtype)
    ovt[...] = lax.dot_general(wvt_ref[...], h, (((1,), (1,)), ((), ())),
                               preferred_element_type=F32).astype(ovt.dtype)
    misc = jnp.dot(h, w_ref[:, SEG_M[0]:SEG_M[1]], preferred_element_type=F32)
    misc2 = jnp.dot(h, w_ref[:, SEG_M2[0]:SEG_M2[1]], preferred_element_type=F32)
    om[...] = misc

    pd = jnp.dot(h, w_ref[:, SEG_D[0]:SEG_D[1]], preferred_element_type=F32)
    cq = _rms(pd[:, 0:MLA_Q_LORA], nq_ref[...]).astype(BF16)
    ckv = _rms(pd[:, MLA_Q_LORA:MLA_Q_LORA + MLA_KV_LORA], nkv_ref[...]).astype(BF16)
    cos = cos_ref[...]
    sin = sin_ref[...]
    cos4 = jnp.concatenate([cos] * N_HEADS, axis=1)
    sin4 = jnp.concatenate([sin] * N_HEADS, axis=1)
    scale = (MLA_NOPE + MLA_ROPE) ** -0.5 * LOG2E
    q2 = jnp.dot(cq, wq2_ref[...], preferred_element_type=F32)
    q_ref[...] = ((q2[:, 0:HEAD_PAD] * cos4 + q2[:, HEAD_PAD:2 * HEAD_PAD] * sin4) * scale).astype(q_ref.dtype)
    lane = _lane_iota(cos.shape)
    rope = (lane >= MISC_ROPE) & (lane < MISC_ROPE + MLA_ROPE)
    kr = jnp.where(rope, misc * cos + misc2 * sin, 0.0)
    k = jnp.dot(ckv, wk_ref[...], preferred_element_type=F32)
    k_ref[...] = (k + jnp.concatenate([kr] * N_HEADS, axis=1)).astype(k_ref.dtype)
    vt_ref[...] = lax.dot_general(wvt2_ref[...], ckv, (((1,), (1,)), ((), ())),
                                  preferred_element_type=F32).astype(vt_ref.dtype)


def _inproj(x, g, w_in_t, layer, conv_w, cos, sin, nq, nkv, wq2, wk, wvt2, tm, seq):
    t = x.shape[0]
    once = lambda a: pl.BlockSpec((1,) + a.shape[1:], lambda i: (layer, 0, 0), pipeline_mode=pl.Buffered(1))
    full = lambda a: pl.BlockSpec(a.shape, lambda i: (0, 0))
    tile = lambda wd: pl.BlockSpec((tm, wd), lambda i: (i, 0))
    cols = lambda: pl.BlockSpec((GROUP_WIDTH, tm), lambda i: (0, i))
    seg = lambda s: s[1] - s[0]
    return pl.pallas_call(
        functools.partial(_inproj_kernel, tm=tm, tiles_per_seq=seq // tm),
        grid=(t // tm,),
        in_specs=[tile(D_MODEL), full(g), once(w_in_t), full(conv_w), tile(LANES), tile(LANES),
                  full(nq), full(nkv), full(wq2), full(wk), full(wvt2)],
        out_specs=[tile(GROUP_WIDTH), tile(seg(SEG_B)), tile(seg(SEG_C)), tile(LANES), cols(),
                   tile(HEAD_PAD), tile(HEAD_PAD), cols()],
        out_shape=[jax.ShapeDtypeStruct((t, GROUP_WIDTH), BF16), jax.ShapeDtypeStruct((t, seg(SEG_B)), BF16),
                   jax.ShapeDtypeStruct((t, seg(SEG_C)), BF16), jax.ShapeDtypeStruct((t, LANES), F32),
                   jax.ShapeDtypeStruct((GROUP_WIDTH, t), BF16),
                   jax.ShapeDtypeStruct((t, HEAD_PAD), BF16), jax.ShapeDtypeStruct((t, HEAD_PAD), BF16),
                   jax.ShapeDtypeStruct((GROUP_WIDTH, t), BF16)],
        scratch_shapes=[pltpu.VMEM((D_MODEL, IN_COLS_PADDED), BF16), pltpu.VMEM((GROUP_WIDTH, D_MODEL), BF16),
                        pltpu.VMEM((8, GROUP_WIDTH), F32)],
        compiler_params=_cparams("arbitrary"),
        name="inproj",
    )(x, g, w_in_t, conv_w, cos, sin, nq, nkv, wq2, wk, wvt2)


def _scalar_prep_kernel(m_ref, p_ref, qk_ref, place_ref, const_ref,
                        col_ref, row_ref, qa_ref, ka_ref, tref_ref, *, tq):
    s = m_ref.shape[0]
    tref_ref[...] = jnp.zeros_like(tref_ref)
    tile_ref = jnp.zeros((1, LANES), F32)
    m = m_ref[...]
    bias = p_ref[0:1, :]
    a_log = p_ref[1:2, :]
    lane = _lane_iota(m.shape)
    z = m + bias
    logf = jnp.minimum(z, 0.0) - jnp.log(1.0 + jnp.exp(-jnp.abs(z)))
    dt = _softplus(z)
    a = dt * (-jnp.exp(a_log))
    is_f = lane < MISC_DT
    is_dt = (lane >= MISC_DT) & (lane < MISC_DT + N_HEADS)
    v = jnp.where(is_f, logf, jnp.where(is_dt, a, 0.0))
    r = _row_iota((SSM_CHUNK, SSM_CHUNK))
    c = _lane_iota((SSM_CHUNK, SSM_CHUNK))
    tril = jnp.where(r >= c, 1.0, 0.0).astype(BF16)
    carry = jnp.zeros((1, LANES), F32)
    lane_1 = _lane_iota((1, LANES))
    lane_b = _lane_iota((SSM_CHUNK, LANES))
    low = lane_b < HEAD_DIM
    aug_lanes = (lane_b >= AUG_LANE) & (lane_b < AUG_LANE + AUG_TERMS)
    for ci in range(s // SSM_CHUNK):
        rest = v[ci * SSM_CHUNK:(ci + 1) * SSM_CHUNK]
        cs = jnp.zeros((SSM_CHUNK, LANES), F32)
        for _ in range(3):
            term = rest.astype(BF16)
            cs = cs + jnp.dot(tril, term, preferred_element_type=F32)
            rest = rest - term.astype(F32)
        cs = cs + jnp.where(lane_1 < MISC_DT, carry, 0.0)
        carry = cs[SSM_CHUNK - 1:SSM_CHUNK]
        acum = pltpu.roll(cs, COL_ACUM - MISC_DT, 1)
        out = jnp.where(lane_b < MISC_DT, cs * LOG2E,
                        jnp.where(lane_b < COL_ACUM, dt[ci * SSM_CHUNK:(ci + 1) * SSM_CHUNK],
                                  jnp.where(lane_b < COL_ACUM + N_HEADS, acum, 0.0)))
        rows = slice(ci * SSM_CHUNK, (ci + 1) * SSM_CHUNK)
        col_ref[rows, :] = out
        row_ref[0, :, rows] = out.T[:N_SCALAR_ROWS]
        if (ci * SSM_CHUNK) % tq == 0:
            tile_ref = out[0:1, :]
            ti = (ci * SSM_CHUNK) // tq
            tref_ref[0, ti:ti + 1, :] = tile_ref
        c = out - tile_ref
        c_hi = c.astype(BF16)
        r1 = c - c_hi.astype(F32)
        c_mid = r1.astype(BF16)
        c_lo = (r1 - c_mid.astype(F32)).astype(BF16)
        compact = jnp.dot(jnp.concatenate([c_hi, c_mid, c_lo], axis=1), place_ref[...],
                          preferred_element_type=F32) + const_ref[0:1, :]
        for side, o_ref in enumerate((qa_ref, ka_ref)):
            dec = compact[:, side * LANES:(side + 1) * LANES]
            for h in range(N_HEADS):
                pair = qk_ref[rows, side * GROUP_WIDTH + (h // 2) * LANES:
                              side * GROUP_WIDTH + (h // 2 + 1) * LANES].astype(F32)
                feat = pair if h % 2 == 0 else pltpu.roll(pair, HEAD_DIM, 1)
                dec_h = pltpu.roll(dec, AUG_LANE - AUG_TERMS * h, 1)
                group = jnp.where(low, feat, jnp.where(aug_lanes, dec_h, 0.0))
                o_ref[rows, h * LANES:(h + 1) * LANES] = group.astype(o_ref.dtype)


AUG_TERMS = 6


def _fox_placement():
    place = np.zeros((3 * LANES, 2 * LANES), np.float32)
    const = np.zeros((8, 2 * LANES), np.float32)
    for h in range(N_HEADS):
        a0 = AUG_TERMS * h
        for term in range(3):
            place[term * LANES + COL_CUMF + h, a0 + term] = 1.0
            place[term * LANES + COL_CUMF + h, LANES + a0 + 3 + term] = -1.0
            const[0, a0 + 3 + term] = 1.0
            const[0, LANES + a0 + term] = 1.0
    return jnp.asarray(place, BF16), jnp.asarray(const, F32)


def _scalar_prep(misc, params, qk, batch, seq, tq):
    place, const = _fox_placement()
    full = lambda a: pl.BlockSpec(a.shape, lambda b: (0,) * a.ndim)
    return pl.pallas_call(
        functools.partial(_scalar_prep_kernel, tq=tq),
        grid=(batch,),
        in_specs=[pl.BlockSpec((seq, LANES), lambda b: (b, 0)),
                  pl.BlockSpec((8, LANES), lambda b: (0, 0)),
                  pl.BlockSpec((seq, 2 * GROUP_WIDTH), lambda b: (b, 0)),
                  full(place), full(const)],
        out_specs=[pl.BlockSpec((seq, LANES), lambda b: (b, 0)),
                   pl.BlockSpec((1, N_SCALAR_ROWS, seq), lambda b: (b, 0, 0)),
                   pl.BlockSpec((seq, HEAD_PAD), lambda b: (b, 0)),
                   pl.BlockSpec((seq, HEAD_PAD), lambda b: (b, 0)),
                   pl.BlockSpec((1, 8, LANES), lambda b: (b, 0, 0))],
        out_shape=[jax.ShapeDtypeStruct((batch * seq, LANES), F32),
                   jax.ShapeDtypeStruct((batch, N_SCALAR_ROWS, seq), F32),
                   jax.ShapeDtypeStruct((batch * seq, HEAD_PAD), BF16),
                   jax.ShapeDtypeStruct((batch * seq, HEAD_PAD), BF16),
                   jax.ShapeDtypeStruct((batch, 8, LANES), F32)],
        compiler_params=_cparams("parallel"),
        name="scalar_prep",
    )(misc, params, qk, place, const)


def _pair_lanes(col, base, shape):
    lane = _lane_iota(shape)
    return jnp.where(lane < HEAD_DIM, col[:, base:base + 1], col[:, base + 1:base + 2])


def _ssd_kernel(p_ref, col_ref, row_ref, cw_ref, par_ref, o_ref, u_ref):
    s = p_ref.shape[0]
    q = SSM_CHUNK
    gw = GROUP_WIDTH
    xbc = p_ref[:, gw:3 * gw].astype(F32)
    acc = xbc * cw_ref[SSM_CONV - 1:SSM_CONV, :]
    for k in range(1, SSM_CONV):
        acc = acc + _shift_rows(xbc, k) * cw_ref[SSM_CONV - 1 - k:SSM_CONV - k, :]
    u_ref[...] = _silu(acc + cw_ref[SSM_CONV:SSM_CONV + 1, :])

    d_skip = par_ref[0:1, :]
    norm_g = par_ref[1:2, :]
    lane_q = _lane_iota((q, LANES))
    low = lane_q < HEAD_DIM
    tri = _row_iota((q, q)) >= _lane_iota((q, q))

    def chunk(ci, states):
        rows = pl.ds(ci * q, q)
        u = u_ref[rows, :]
        col = col_ref[rows, :]
        bm = u[:, gw:gw + LANES]
        cm = u[:, gw + LANES:gw + 2 * LANES]
        z = p_ref[rows, 0:gw].astype(F32)
        new_states = []
        ys = []
        for g in range(2):
            sel = low if g == 0 else jnp.logical_not(low)
            cg = jnp.where(sel, cm, 0.0).astype(BF16)
            bg = jnp.where(sel, bm, 0.0)
            gmat = lax.dot_general(cg, bm.astype(BF16), (((1,), (1,)), ((), ())),
                                   preferred_element_type=F32)
            xs = u[:, g * LANES:(g + 1) * LANES]
            dt2 = _pair_lanes(col, COL_DT + 2 * g, (q, LANES))
            ac2 = _pair_lanes(col, COL_ACUM + 2 * g, (q, LANES))
            xdt = xs * dt2
            xdt_b = xdt.astype(BF16)
            st = states[g]
            y_off = jnp.dot(cg, st.astype(BF16), preferred_element_type=F32) * jnp.exp(ac2)
            halves = []
            for hh in range(2):
                h = 2 * g + hh
                ac_col = col[:, COL_ACUM + h:COL_ACUM + h + 1]
                ac_row = row_ref[0, COL_ACUM + h:COL_ACUM + h + 1, rows]
                decay = jnp.exp(jnp.where(tri, ac_col - ac_row, -1e30))
                mm = (gmat * decay).astype(BF16)
                halves.append(jnp.dot(mm, xdt_b, preferred_element_type=F32))
            y = jnp.where(low, halves[0], halves[1]) + y_off + d_skip[:, g * LANES:(g + 1) * LANES] * xs
            ys.append(y)
            ac_last = ac2[q - 1:q, :]
            w_end = jnp.exp(ac_last - ac2)
            xw = (xdt * w_end).astype(BF16)
            upd = jnp.dot(bg.T.astype(BF16), xw, preferred_element_type=F32)
            new_states.append(st * jnp.exp(ac_last) + upd)
        yfull = jnp.concatenate(ys, axis=1) * _silu(z)
        o_ref[rows, :] = _rms(yfull, norm_g).astype(o_ref.dtype)
        return tuple(new_states)

    init = (jnp.zeros((LANES, LANES), F32), jnp.zeros((LANES, LANES), F32))
    states = init
    for ci in range(s // q):
        states = chunk(ci, states)


def _ssd_mixer(pc, col, rows, conv_wb, par, batch, seq):
    gw = GROUP_WIDTH
    return pl.pallas_call(
        _ssd_kernel,
        grid=(batch,),
        in_specs=[pl.BlockSpec((seq, 3 * gw), lambda b: (b, 0)),
                  pl.BlockSpec((seq, LANES), lambda b: (b, 0)),
                  pl.BlockSpec((1, N_SCALAR_ROWS, seq), lambda b: (b, 0, 0)),
                  pl.BlockSpec((8, 2 * gw), lambda b: (0, 0)),
                  pl.BlockSpec((8, gw), lambda b: (0, 0))],
        out_specs=pl.BlockSpec((seq, gw), lambda b: (b, 0)),
        out_shape=jax.ShapeDtypeStruct((batch * seq, gw), BF16),
        scratch_shapes=[pltpu.VMEM((seq, 2 * gw), F32)],
        compiler_params=_cparams("parallel"),
        name="ssd_mixer",
    )(pc, col, rows, conv_wb, par)


def _attn_kernel(*refs, fox, tq, nq):
    if fox:
        tref_ref, q_ref, k_ref, vt_ref, o_ref = refs
    else:
        q_ref, k_ref, vt_ref, o_ref = refs
        tref_ref = None
    b = pl.program_id(0)
    i = pl.program_id(1)
    key = _row_iota((tq, tq))
    qry = _lane_iota((tq, tq))
    if fox:
        allowed = key <= qry
    else:
        shift = int(math.log2(MLA_CHUNK))
        allowed = (key >> shift) <= (qry >> shift)
    qs = [q_ref[:, h * LANES:(h + 1) * LANES] for h in range(N_HEADS)]
    ones_rows = jnp.ones((16, tq), BF16)

    def step(j, masked, carry):
        rk = pl.ds(j * tq, tq)
        scores = [lax.dot_general(k_ref[rk, h * LANES:(h + 1) * LANES], qs[h], (((1,), (1,)), ((), ())),
                                  preferred_element_type=F32) for h in range(N_HEADS)]
        probs = []
        for h in range(N_HEADS):
            m, l, _ = carry[h]
            s = scores[h]
            if masked:
                s = jnp.where(allowed, s, -1e30)
            delta = (tref_ref[b, i, h] - tref_ref[b, j, h]) if fox else 0.0
            m_new = jnp.maximum(m, jnp.max(s, axis=0, keepdims=True) + delta)
            alpha = jnp.exp2(m - m_new)
            p = jnp.exp2(s - (m_new - delta))
            probs.append((m_new, alpha, p.astype(BF16)))
        new = []
        for h in range(N_HEADS):
            pair = h // 2
            m_new, alpha, p = probs[h]
            lhs = jnp.concatenate([vt_ref[pair * LANES:(pair + 1) * LANES, rk], ones_rows], axis=0)
            pv = jnp.dot(lhs, p, preferred_element_type=F32)
            new.append((m_new, alpha * carry[h][1] + pv[LANES:LANES + 1], alpha * carry[h][2] + pv[0:LANES]))
        return tuple(new)

    init = tuple((jnp.full((1, tq), -1e30, F32), jnp.zeros((1, tq), F32), jnp.zeros((LANES, tq), F32))
                 for _ in range(N_HEADS))
    top = _row_iota((LANES, tq)) < HEAD_DIM

    def sweep(n_before):
        carry = init
        for j in range(n_before):
            carry = step(j, False, carry)
        carry = step(n_before, True, carry)
        outs = [acc / l for (_, l, acc) in carry]
        o_t = jnp.concatenate([jnp.where(top, outs[0], outs[1]), jnp.where(top, outs[2], outs[3])], axis=0)
        o_ref[...] = o_t.T.astype(o_ref.dtype)

    for n_before in range(nq):
        pl.when(i == n_before)(functools.partial(sweep, n_before))


def _attention(q, k, vt, tref, batch, seq, tq, name):
    nq = seq // tq
    fox = tref is not None
    kern = functools.partial(_attn_kernel, fox=fox, tq=tq, nq=nq)
    grid_spec = pltpu.PrefetchScalarGridSpec(
        num_scalar_prefetch=1 if fox else 0,
        grid=(batch, nq),
        in_specs=[pl.BlockSpec((tq, HEAD_PAD), lambda b, i, *_: (b * nq + i, 0)),
                  pl.BlockSpec((seq, HEAD_PAD), lambda b, i, *_: (b, 0)),
                  pl.BlockSpec((GROUP_WIDTH, seq), lambda b, i, *_: (0, b))],
        out_specs=pl.BlockSpec((tq, GROUP_WIDTH), lambda b, i, *_: (b * nq + i, 0)),
    )
    args = ((tref,) if fox else ()) + (q, k, vt)
    return pl.pallas_call(
        kern,
        grid_spec=grid_spec,
        out_shape=jax.ShapeDtypeStruct((batch * seq, GROUP_WIDTH), BF16),
        compiler_params=_cparams("parallel", "arbitrary"),
        name=name,
    )(*args)


def _outproj_kernel(x_ref, ya, yb, yc, yd, w_ref, g_ref, wr_ref, br_ref,
                    x2_ref, h2_ref, rrow_ref, cnt_ref, *, tm, moe_tile):
    y = jnp.concatenate([ya[...], yb[...], yc[...], yd[...]], axis=1)
    x2 = x_ref[...] + jnp.dot(y, w_ref[...], preferred_element_type=F32)
    x2_ref[...] = x2
    h2 = _rms(x2, g_ref[...])
    h2_ref[...] = h2.astype(h2_ref.dtype)
    h_hi = h2.astype(BF16)
    h_lo = (h2 - h_hi.astype(F32)).astype(BF16)
    part = jnp.dot(h_hi, wr_ref[...], preferred_element_type=F32)
    logits = (part[:, 0:LANES] + part[:, LANES:2 * LANES]
              + jnp.dot(h_lo, wr_ref[:, 0:LANES], preferred_element_type=F32) + br_ref[...])
    lt = logits.T[0:ROUTER_ROWS]
    row = _row_iota(lt.shape)
    neg = -1e30
    big = 1 << 20
    gmask = (row >= N_EXPERTS) & (row < N_EXPERTS + N_EXPERT_GROUPS)
    gl = jnp.where(gmask, lt, neg)
    gmax = jnp.max(gl, axis=0, keepdims=True)
    gsum = jnp.sum(jnp.where(gmask, jnp.exp(gl - gmax), 0.0), axis=0, keepdims=True)
    g_w = 1.0 / gsum
    g_idx = jnp.min(jnp.where(gmask & (gl == gmax), row, big), axis=0, keepdims=True) - N_EXPERTS
    emask = (row < N_EXPERTS) & ((row >> int(math.log2(EXPERTS_PER_GROUP))) == g_idx)
    el = jnp.where(emask, lt, neg)
    e1v = jnp.max(el, axis=0, keepdims=True)
    esum = jnp.sum(jnp.where(emask, jnp.exp(el - e1v), 0.0), axis=0, keepdims=True)
    i1 = jnp.min(jnp.where(emask & (el == e1v), row, big), axis=0, keepdims=True)
    el2 = jnp.where(row == i1, neg, el)
    e2v = jnp.max(el2, axis=0, keepdims=True)
    i2 = jnp.min(jnp.where(emask & (row != i1) & (el2 == e2v), row, big), axis=0, keepdims=True)
    p1 = 1.0 / esum
    p2 = jnp.exp(e2v - e1v) / esum
    w1 = g_w * (p1 / (p1 + p2))
    w2 = g_w * (p2 / (p1 + p2))
    out_row = _row_iota(rrow_ref.shape)
    rrow_ref[...] = jnp.where(out_row == 0, i1.astype(F32),
                              jnp.where(out_row == 1, i2.astype(F32),
                                        jnp.where(out_row == 2, w1, jnp.where(out_row == 3, w2, 0.0))))
    step = pl.program_id(0)

    @pl.when(step == 0)
    def _():
        cnt_ref[...] = jnp.zeros_like(cnt_ref)

    chosen = jnp.where((row == i1) | (row == i2), 1.0, 0.0).astype(BF16)
    tiles = tm // moe_tile
    tile_of = (_row_iota((tm, LANES)) >> int(math.log2(moe_tile))) + step * tiles
    to_tile = jnp.where(_lane_iota((tm, LANES)) == tile_of, 1.0, 0.0).astype(BF16)
    counts = jnp.dot(chosen, to_tile, preferred_element_type=F32)
    cnt_ref[...] += counts[0:N_EXPERTS]


def _outproj(x, ya, yb, yc, yd, w, g, wr, br, tm, moe_tile):
    t = x.shape[0]
    full = lambda a: pl.BlockSpec(a.shape, lambda i: (0, 0))
    tile = lambda wd: pl.BlockSpec((tm, wd), lambda i: (i, 0))
    return pl.pallas_call(
        functools.partial(_outproj_kernel, tm=tm, moe_tile=moe_tile),
        grid=(t // tm,),
        in_specs=[tile(D_MODEL)] + [tile(GROUP_WIDTH)] * 4 + [full(w), full(g), full(wr), full(br)],
        out_specs=[tile(D_MODEL), tile(D_MODEL), pl.BlockSpec((8, tm), lambda i: (0, i)),
                   pl.BlockSpec((N_EXPERTS, LANES), lambda i: (0, 0))],
        out_shape=[jax.ShapeDtypeStruct((t, D_MODEL), F32), jax.ShapeDtypeStruct((t, D_MODEL), BF16),
                   jax.ShapeDtypeStruct((8, t), F32), jax.ShapeDtypeStruct((N_EXPERTS, LANES), F32)],
        compiler_params=_cparams("arbitrary"),
        name="outproj_router",
    )(x, ya, yb, yc, yd, w, g, wr, br)


MOE_TILE = 256
CHUNK = 8
LOCAL_ROWS = 2 * MOE_TILE + 256
PACKED = D_MODEL // 2
XS_WIDTH = PACKED
U32 = jnp.uint32


def _pack_bf16_pairs(x, exact=False):
    if not exact:
        x = x.astype(BF16).astype(F32)
    half = x.shape[1] // 2
    lo = lax.bitcast_convert_type(x[:, :half], U32)
    hi = lax.bitcast_convert_type(x[:, half:], U32)
    return hi | (lo >> 16)


def _unpack_bf16_pairs(words):
    lo = lax.bitcast_convert_type(words << 16, F32)
    hi = lax.bitcast_convert_type(words & U32(0xFFFF0000), F32)
    return jnp.concatenate([lo, hi], axis=1).astype(BF16)


COPY_ROWS = (2 * CHUNK, CHUNK)


def _route_kernel(r_ref, cnt_ref, lrow_ref, lcol_ref, tab_ref, ctab_ref, meta_ref,
                  loff_ref, goff_ref, n8_ref, *, tm, nbp, tiles):
    i = pl.program_id(0)
    hi = lax.Precision.HIGHEST

    @pl.when(i == 0)
    def _():
        cnt = cnt_ref[...]
        n8 = jnp.floor((cnt + (CHUNK - 1)) * (1.0 / CHUNK)) * CHUNK
        er = _row_iota((N_EXPERTS, N_EXPERTS))
        ec = _lane_iota((N_EXPERTS, N_EXPERTS))
        below = jnp.where(er > ec, 1.0, 0.0)
        loff = jnp.dot(below, n8, preferred_element_type=F32, precision=hi)
        rows_e = jnp.sum(n8, axis=-1, keepdims=True) + jnp.zeros_like(n8)
        padded = jnp.floor((rows_e + (MOE_ROWS - 1)) * (1.0 / MOE_ROWS)) * MOE_ROWS
        e_start = jnp.dot(below, padded, preferred_element_type=F32, precision=hi)
        tr = _row_iota((LANES, LANES))
        tc = _lane_iota((LANES, LANES))
        earlier = jnp.where(tr < tc, 1.0, 0.0)
        goff = e_start + jnp.dot(n8, earlier, preferred_element_type=F32, precision=hi)
        loff_ref[...] = loff
        goff_ref[...] = goff
        n8_ref[...] = n8
        big = jnp.floor(n8 * (0.5 / CHUNK))
        small = n8 * (1.0 / CHUNK) - 2.0 * big
        row_t = _row_iota(tab_ref.shape)
        tab_ref[...] = jnp.where(row_t == 0, jnp.sum(big, axis=0, keepdims=True),
                                 jnp.where(row_t == 1, jnp.sum(small, axis=0, keepdims=True), 0.0)).astype(I32)
        reps = nbp // LANES
        pend_b = jnp.concatenate([e_start + padded] * reps, axis=1)
        vend_b = jnp.concatenate([e_start + rows_e] * reps, axis=1)
        used_b = jnp.concatenate([padded] * reps, axis=1) > 0.0
        b0 = (_lane_iota((N_EXPERTS, nbp)) * MOE_ROWS).astype(F32)
        bexp = jnp.sum(jnp.where(pend_b <= b0, 1.0, 0.0), axis=0, keepdims=True)
        bexp = jnp.minimum(bexp, N_EXPERTS - 1.0)
        e_b = _row_iota((N_EXPERTS, nbp)).astype(F32)
        is_e = e_b == bexp
        vend = jnp.sum(jnp.where(is_e, vend_b, 0.0), axis=0, keepdims=True)
        nvalid = jnp.clip(vend - b0[0:1], 0.0, float(MOE_ROWS))
        total = jnp.max(pend_b, axis=0, keepdims=True) * (1.0 / MOE_ROWS)
        order = jnp.sum(jnp.where(used_b & (e_b < bexp), 1.0, 0.0), axis=0, keepdims=True)
        nxt = jnp.min(jnp.where(used_b & (e_b > bexp), e_b, float(N_EXPERTS)), axis=0, keepdims=True)
        row = _row_iota((8, nbp))
        meta = jnp.where(row == 0, bexp, jnp.where(row == 1, nvalid, jnp.where(row == 2, total,
                         jnp.where(row == 3, order, jnp.where(row == 4, nxt, 0.0)))))
        meta_ref[...] = meta.astype(I32)

    su = jnp.where(_row_iota((tm, tm)) < _lane_iota((tm, tm)), 1.0, 0.0).astype(BF16)
    incl = jnp.where(_row_iota((N_EXPERTS, N_EXPERTS)) >= _lane_iota((N_EXPERTS, N_EXPERTS)), 1.0, 0.0)
    cidx = _lane_iota((N_EXPERTS, LANES)).astype(F32)
    e_iota = _row_iota((N_EXPERTS, tm))
    out_row = _row_iota((LANES, tm))
    for k in range(tiles):
        tok = slice(k * tm, (k + 1) * tm)
        oh0 = e_iota == r_ref[0:1, tok].astype(I32)
        oh1 = e_iota == r_ref[1:2, tok].astype(I32)
        oh = jnp.where(oh0 | oh1, 1.0, 0.0)
        tile_lane = _lane_iota((N_EXPERTS, LANES)) == i * tiles + k
        before = jnp.dot(oh.astype(BF16), su, preferred_element_type=F32)
        base = jnp.sum(jnp.where(tile_lane, loff_ref[...], 0.0), axis=-1, keepdims=True) + before
        d0 = jnp.sum(jnp.where(oh0, base, 0.0), axis=0, keepdims=True)
        d1 = jnp.sum(jnp.where(oh1, base, 0.0), axis=0, keepdims=True)
        lrow_ref[k, 0:1, :] = d0.astype(I32)
        lrow_ref[k, 1:2, :] = d1.astype(I32)
        pick_tile = lambda ref: jnp.sum(jnp.where(tile_lane, ref[...], 0.0), axis=-1, keepdims=True)
        nch = pick_tile(n8_ref) * (1.0 / CHUNK)
        n_big = jnp.floor(nch * 0.5)
        n_small = nch - 2.0 * n_big
        loff_t = pick_tile(loff_ref)
        goff_t = pick_tile(goff_ref)
        for c, (n, rows, first) in enumerate(((n_big, COPY_ROWS[0], 0.0),
                                              (n_small, COPY_ROWS[1], n_big * COPY_ROWS[0]))):
            cend = jnp.dot(incl, n + jnp.zeros((N_EXPERTS, LANES), F32), preferred_element_type=F32, precision=hi)
            cstart = cend - n
            mine = (cidx >= cstart) & (cidx < cend)
            step_rows = first + (cidx - cstart) * rows
            ctab_ref[k, 2 * c:2 * c + 1, :] = jnp.sum(jnp.where(mine, loff_t + step_rows, 0.0), axis=0,
                                                      keepdims=True).astype(I32)
            ctab_ref[k, 2 * c + 1:2 * c + 2, :] = jnp.sum(jnp.where(mine, goff_t + step_rows, 0.0), axis=0,
                                                          keepdims=True).astype(I32)
        lcol_ref[tok, :] = jnp.where(out_row == 0, d0, jnp.where(out_row == 1, d1,
                                     jnp.where(out_row == 2, r_ref[2:3, tok],
                                               jnp.where(out_row == 3, r_ref[3:4, tok], 0.0)))).T


def _route(rrow, cnt, tm, nbp):
    t = rrow.shape[1]
    nt = t // tm
    tiles = next(r for r in (4, 2, 1) if nt % r == 0)
    kern = functools.partial(_route_kernel, tm=tm, nbp=nbp, tiles=tiles)
    return pl.pallas_call(
        kern,
        grid=(nt // tiles,),
        in_specs=[pl.BlockSpec((8, tiles * tm), lambda i: (0, i)),
                  pl.BlockSpec((N_EXPERTS, LANES), lambda i: (0, 0))],
        out_specs=[pl.BlockSpec((tiles, 2, tm), lambda i: (i, 0, 0)),
                   pl.BlockSpec((tiles * tm, LANES), lambda i: (i, 0)),
                   pl.BlockSpec((8, LANES), lambda i: (0, 0)),
                   pl.BlockSpec((tiles, 2 * len(COPY_ROWS), LANES), lambda i: (i, 0, 0)),
                   pl.BlockSpec((8, nbp), lambda i: (0, 0))],
        out_shape=[jax.ShapeDtypeStruct((nt, 2, tm), I32), jax.ShapeDtypeStruct((t, LANES), F32),
                   jax.ShapeDtypeStruct((8, LANES), I32), jax.ShapeDtypeStruct((nt, 2 * len(COPY_ROWS), LANES), I32),
                   jax.ShapeDtypeStruct((8, nbp), I32)],
        scratch_shapes=[pltpu.VMEM((N_EXPERTS, LANES), F32)] * 3,
        compiler_params=_cparams("arbitrary"),
        name="moe_route",
    )(rrow, cnt)


def _chunk_copies(tabs, i, local_ref, global_ref, sem, to_global, action):
    tab_ref, ctab_ref = tabs
    for k, rows in enumerate(COPY_ROWS):
        count = tab_ref[k, i]

        def copy(lo, go, rows=rows):
            lsl = local_ref.at[pl.ds(pl.multiple_of(lo, CHUNK), rows)]
            gsl = global_ref.at[pl.ds(pl.multiple_of(go, CHUNK), rows)]
            return pltpu.make_async_copy(lsl, gsl, sem) if to_global else pltpu.make_async_copy(gsl, lsl, sem)

        if action == "wait":
            def one(c, c1, copy=copy):
                copy(0, 0).wait()
                return c1
        else:
            def one(c, c1, copy=copy, k=k):
                copy(ctab_ref[i, 2 * k, c], ctab_ref[i, 2 * k + 1, c]).start()
                return c1

        lax.fori_loop(0, count, one, 0)


def _scatter_kernel(tab_ref, ctab_ref, meta_ref, lrow_ref, h_ref, xs_ref, buf_ref, zero_ref, sem, zsem, *, tm, nb):
    i = pl.program_id(0)
    tabs = (tab_ref, ctab_ref)

    @pl.when(i == 0)
    def _():
        zero_ref[...] = jnp.zeros_like(zero_ref)
        n_used = meta_ref[2, 0]

        def each_piece(action):
            def body(b, c):
                for piece in range(MOE_ROWS // ZERO_ROWS):
                    @pl.when((b < n_used) & (meta_ref[1, b] < (piece + 1) * ZERO_ROWS))
                    def _():
                        start = pl.multiple_of(b * MOE_ROWS + piece * ZERO_ROWS, ZERO_ROWS)
                        cp = pltpu.make_async_copy(zero_ref, xs_ref.at[pl.ds(start, ZERO_ROWS)], zsem)
                        getattr(cp, action)()
                return c
            lax.fori_loop(0, nb, body, 0)

        each_piece("start")
        each_piece("wait")

    rows = _row_iota((LOCAL_ROWS, tm))

    def drain(tile, slot):
        _chunk_copies(tabs, tile, buf_ref.at[slot], xs_ref, sem.at[slot], True, "wait")

    for slot in range(2):
        tile = 2 * i + slot
        p0 = rows == lrow_ref[slot, 0:1, :]
        p1 = rows == lrow_ref[slot, 1:2, :]
        perm = jnp.where(p0 | p1, 1.0, 0.0).astype(BF16)
        sorted_rows = jnp.dot(perm, h_ref[slot * tm:(slot + 1) * tm, :],
                              preferred_element_type=F32)

        @pl.when(i > 0)
        def _():
            drain(tile - 2, slot)

        buf = buf_ref.at[slot]
        buf[...] = _pack_bf16_pairs(sorted_rows, exact=True)
        _chunk_copies(tabs, tile, buf, xs_ref, sem.at[slot], True, "start")

    @pl.when(i == pl.num_programs(0) - 1)
    def _():
        drain(2 * i, 0)
        drain(2 * i + 1, 1)


def _scatter(tab, ctab, meta, lrow, h2, tm, nb):
    t = h2.shape[0]
    kern = functools.partial(_scatter_kernel, tm=tm, nb=nb)
    grid_spec = pltpu.PrefetchScalarGridSpec(
        num_scalar_prefetch=3,
        grid=(t // (2 * tm),),
        in_specs=[pl.BlockSpec((2, 2, tm), lambda i, *_: (i, 0, 0)),
                  pl.BlockSpec((2 * tm, D_MODEL), lambda i, *_: (i, 0))],
        out_specs=pl.BlockSpec(memory_space=pl.ANY),
        scratch_shapes=[pltpu.VMEM((2, LOCAL_ROWS, XS_WIDTH), U32), pltpu.VMEM((ZERO_ROWS, XS_WIDTH), U32),
                        pltpu.SemaphoreType.DMA((2,)), pltpu.SemaphoreType.DMA],
    )
    return pl.pallas_call(
        kern,
        grid_spec=grid_spec,
        out_shape=jax.ShapeDtypeStruct((nb * MOE_ROWS, XS_WIDTH), U32),
        compiler_params=_cparams("arbitrary"),
        name="moe_scatter",
    )(tab, ctab, meta, lrow, h2)


def _ffn_kernel(meta_ref, x_ref, wg_ref, wu_ref, wd_ref, o_ref, wgu_b, wd_b, wg_f, wu_f, wd_f, wsem, *, layer):
    b = pl.program_id(0)
    live = b < meta_ref[2, 0]
    expert = meta_ref[0, b]
    prev = meta_ref[0, jnp.maximum(b - 1, 0)]

    def fetch(e, slot):
        return [pltpu.make_async_copy(src.at[layer, e], dst.at[slot], wsem.at[slot])
                for src, dst in ((wg_ref, wg_f), (wu_ref, wu_f), (wd_ref, wd_f))]

    def first_block(slot):
        @pl.when(b == 0)
        def _():
            for cp in fetch(expert, slot):
                cp.start()

        for cp in fetch(expert, slot):
            cp.wait()
        wgu_b[:, 0:EXPERT_FF] = wg_f[slot].astype(BF16)
        wgu_b[:, EXPERT_FF:2 * EXPERT_FF] = wu_f[slot].astype(BF16)
        wd_b[...] = wd_f[slot].astype(BF16)
        nxt = meta_ref[4, b]

        @pl.when(nxt < N_EXPERTS)
        def _():
            for cp in fetch(nxt, 1 - slot):
                cp.start()

    changed = live & ((b == 0) | (expert != prev))
    odd = (meta_ref[3, b] & 1) == 1

    @pl.when(changed & jnp.logical_not(odd))
    def _():
        first_block(0)

    @pl.when(changed & odd)
    def _():
        first_block(1)

    @pl.when(live)
    def _():
        x = _unpack_bf16_pairs(x_ref[...])
        gu = jnp.dot(x, wgu_b[...], preferred_element_type=F32)
        act = (_silu(gu[:, 0:EXPERT_FF]) * gu[:, EXPERT_FF:2 * EXPERT_FF]).astype(BF16)
        y = jnp.dot(act, wd_b[...], preferred_element_type=F32)
        o_ref[...] = _pack_bf16_pairs(y)


def _ffn(meta, xs, wg, wu, wd, layer, nb):
    def blk(b, m):
        return (jnp.maximum(jnp.minimum(b, m[2, 0] - 1), 0), 0)

    grid_spec = pltpu.PrefetchScalarGridSpec(
        num_scalar_prefetch=1,
        grid=(nb,),
        in_specs=[pl.BlockSpec((MOE_ROWS, XS_WIDTH), blk)] + [pl.BlockSpec(memory_space=pl.ANY)] * 3,
        out_specs=pl.BlockSpec((MOE_ROWS, PACKED), blk),
        scratch_shapes=[pltpu.VMEM((D_MODEL, 2 * EXPERT_FF), BF16), pltpu.VMEM((EXPERT_FF, D_MODEL), BF16),
                        pltpu.VMEM((2, D_MODEL, EXPERT_FF), F32), pltpu.VMEM((2, D_MODEL, EXPERT_FF), F32),
                        pltpu.VMEM((2, EXPERT_FF, D_MODEL), F32), pltpu.SemaphoreType.DMA((2,))],
    )
    return pl.pallas_call(
        functools.partial(_ffn_kernel, layer=layer),
        grid_spec=grid_spec,
        out_shape=jax.ShapeDtypeStruct((nb * MOE_ROWS, PACKED), U32),
        compiler_params=_cparams("arbitrary"),
        name="moe_experts",
    )(meta, xs, wg, wu, wd)


def _gather_kernel(tab_ref, ctab_ref, lcol_ref, x_ref, g_ref, ys_ref, o_ref, buf_ref, sem, *, tm, final):
    i = pl.program_id(0)

    last = pl.num_programs(0) - 1

    def fetch(tile, slot, action):
        _chunk_copies((tab_ref, ctab_ref), tile, buf_ref.at[slot], ys_ref, sem.at[slot], False, action)

    @pl.when(i == 0)
    def _():
        buf_ref[...] = jnp.zeros_like(buf_ref)
        fetch(0, 0, "start")

    col = _lane_iota((tm, LOCAL_ROWS)).astype(F32)
    for slot in range(2):
        tile = 2 * i + slot
        tok = slice(slot * tm, (slot + 1) * tm)
        if slot == 0:
            fetch(tile + 1, 1, "start")
        else:
            @pl.when(i < last)
            def _():
                fetch(tile + 1, 0, "start")

        lc = lcol_ref[tok, :]
        pick0 = jnp.where(col == lc[:, 0:1], 1.0, 0.0).astype(BF16)
        pick1 = jnp.where(col == lc[:, 1:2], 1.0, 0.0).astype(BF16)
        fetch(tile, slot, "wait")
        y = _unpack_bf16_pairs(buf_ref[slot])
        both = jnp.dot(jnp.concatenate([pick0, pick1], axis=0), y, preferred_element_type=F32)
        x = x_ref[tok, :] + lc[:, 2:3] * both[0:tm] + lc[:, 3:4] * both[tm:2 * tm]
        o_ref[tok, :] = _rms(x, g_ref[...]) if final else x


def _gather(tab, ctab, lcol, x2, g, ys, tm, final):
    t = x2.shape[0]
    kern = functools.partial(_gather_kernel, tm=tm, final=final)
    grid_spec = pltpu.PrefetchScalarGridSpec(
        num_scalar_prefetch=2,
        grid=(t // (2 * tm),),
        in_specs=[pl.BlockSpec((2 * tm, LANES), lambda i, *_: (i, 0)),
                  pl.BlockSpec((2 * tm, D_MODEL), lambda i, *_: (i, 0)),
                  pl.BlockSpec((1, D_MODEL), lambda i, *_: (0, 0)),
                  pl.BlockSpec(memory_space=pl.ANY)],
        out_specs=pl.BlockSpec((2 * tm, D_MODEL), lambda i, *_: (i, 0)),
        scratch_shapes=[pltpu.VMEM((2, LOCAL_ROWS, PACKED), U32), pltpu.SemaphoreType.DMA((2,))],
    )
    return pl.pallas_call(
        kern,
        grid_spec=grid_spec,
        out_shape=jax.ShapeDtypeStruct((t, D_MODEL), F32),
        compiler_params=_cparams("arbitrary"),
        name="moe_combine",
    )(tab, ctab, lcol, x2, g, ys)


def _pad_rows(a, rows=8):
    return jnp.zeros((rows, a.shape[-1]), F32).at[:a.shape[0]].set(a.astype(F32))


def _arrange_mla(w_uq, w_ukv):
    half = MLA_ROPE // 2
    qd = MLA_NOPE + MLA_ROPE
    wq, wqs, wk, wv = [], [], [], []
    zq = jnp.zeros((MLA_Q_LORA, LANES - qd), w_uq.dtype)
    zk = jnp.zeros((MLA_KV_LORA, LANES - MLA_NOPE), w_ukv.dtype)
    for h in range(N_HEADS):
        q = w_uq[:, h * qd:(h + 1) * qd]
        nope, rope = q[:, :MLA_NOPE], q[:, MLA_NOPE:]
        wq.append(jnp.concatenate([nope, rope, zq], axis=1))
        wqs.append(jnp.concatenate([jnp.zeros_like(nope), -rope[:, half:], rope[:, :half], zq], axis=1))
        kv = w_ukv[:, h * 2 * MLA_NOPE:(h + 1) * 2 * MLA_NOPE]
        wk.append(jnp.concatenate([kv[:, :MLA_NOPE], zk], axis=1))
        wv.append(kv[:, MLA_NOPE:])
    cat = lambda xs: jnp.concatenate(xs, axis=1).astype(BF16)
    return cat(wq), cat(wqs), cat(wk), cat(wv).T


def kernel(x, positions, norm_mix, w_in, conv_a, fox_forget_bias, ssm_conv_w, ssm_conv_b, ssm_dt_bias,
           ssm_a_log, ssm_d, ssm_norm, mla_q_norm, mla_kv_norm, mla_w_uq, mla_w_ukv, w_out, norm_ffn,
           router_group_w, router_group_b, router_expert_w, router_expert_b, expert_w_gate, expert_w_up,
           expert_w_down, norm_final):
    batch, seq, d = x.shape
    t = batch * seq
    depth = w_in.shape[0]
    tm = min(ROW_TILE, seq)
    tq = min(ATTN_TQ, seq)
    tmd = MOE_TILE
    assert d == D_MODEL and seq % tm == 0 and seq % tq == 0 and tq % SSM_CHUNK == 0, (x.shape,)
    assert seq // tq <= 8 and tm % (LANES * ROPE_PACK) == 0, (seq, tm)
    assert t % (2 * tmd) == 0 and t // tmd <= LANES and tm % tmd == 0, (t, tmd)
    max_rows = 2 * t + (CHUNK - 1) * N_EXPERTS * (t // tmd) + N_EXPERTS * (MOE_ROWS - 1)
    nb = -(-max_rows // MOE_ROWS)
    nbp = -(-nb // LANES) * LANES

    xf = x.reshape(t, d)
    cos, sin = _rope_tables(positions, tm)
    w_in_t = jnp.swapaxes(w_in, 1, 2)

    for l in range(depth):
        wq, wqs, wk, wv = _arrange_mla(mla_w_uq[l], mla_w_ukv[l])
        ya, pb, pc, misc, fox_vt, q, k, v = _inproj(
            xf, norm_mix[l][None, :], w_in_t, l, _pad_rows(conv_a[l]), cos, sin, mla_q_norm[l][None, :],
            mla_kv_norm[l][None, :], jnp.concatenate([wq, wqs], axis=1), wk, wv, tm, seq)

        sp = jnp.zeros((8, LANES), F32)
        sp = sp.at[0, MISC_F:MISC_F + N_HEADS].set(fox_forget_bias[l])
        sp = sp.at[0, MISC_DT:MISC_DT + N_HEADS].set(ssm_dt_bias[l])
        sp = sp.at[1, MISC_DT:MISC_DT + N_HEADS].set(ssm_a_log[l])
        col, rows, fox_q, fox_k, tref = _scalar_prep(misc, sp, pb, batch, seq, tq)

        yb = _attention(fox_q, fox_k, fox_vt, tref, batch, seq, tq, "fox_attention")
        conv_wb = _pad_rows(jnp.concatenate([ssm_conv_w[l], ssm_conv_b[l][None, :]], axis=0))
        ssd_par = _pad_rows(jnp.stack([jnp.repeat(ssm_d[l], HEAD_DIM), ssm_norm[l]]))
        yc = _ssd_mixer(pc, col, rows, conv_wb, ssd_par, batch, seq)
        yd = _attention(q, k, v, None, batch, seq, tq, "mla_attention")

        pad = jnp.zeros((d, LANES - N_EXPERTS - N_EXPERT_GROUPS), F32)
        wr = jnp.concatenate([router_expert_w[l], router_group_w[l], pad], axis=1)
        wr_hi = wr.astype(BF16)
        wr = jnp.concatenate([wr_hi, (wr - wr_hi.astype(F32)).astype(BF16)], axis=1)
        br = jnp.concatenate([router_expert_b[l], router_group_b[l], pad[0]])[None, :]
        x2, h2, rrow, cnt = _outproj(xf, ya, yb, yc, yd, w_out[l].astype(BF16), norm_ffn[l][None, :], wr, br,
                                     tm, tmd)

        lrow, lcol, tab, ctab, meta = _route(rrow, cnt, tmd, nbp)
        xs = _scatter(tab, ctab, meta, lrow, h2, tmd, nb)
        ys = _ffn(meta, xs, expert_w_gate, expert_w_up, expert_w_down, l, nb)
        final = l == depth - 1
        xf = _gather(tab, ctab, lcol, x2, norm_final[None, :], ys, tmd, final)

    return xf.reshape(batch, seq, d)
```

```python
import functools
import math

import jax
import jax.numpy as jnp
import numpy as np
from jax import lax
from jax.experimental import pallas as pl
from jax.experimental.pallas import tpu as pltpu

F32 = jnp.float32
BF16 = jnp.bfloat16
I32 = jnp.int32

LANES = 128
VMEM_LIMIT_BYTES = 56 * 1024 * 1024

D_MODEL = 1024
RMS_EPS = 1e-6
LOG2E = math.log2(math.e)
GROUP_WIDTH = 256
HEAD_DIM = 64
N_HEADS = 4

CONV_A_WIDTH = 3
SSM_CONV = 4
SSM_STATE = 64
SSM_CHUNK = 256

MLA_NOPE = 64
MLA_ROPE = 32
MLA_Q_LORA = 256
MLA_KV_LORA = 128
ROPE_BASE = 10000.0
MLA_CHUNK = 64
ATTN_TQ = 512
ROW_TILE = 1024

N_EXPERT_GROUPS = 4
EXPERTS_PER_GROUP = 8
N_EXPERTS = 32
EXPERT_FF = 256
ROUTER_ROWS = 48
MOE_ROWS = 512
ZERO_ROWS = 256

SEG_A = (0, 768)
SEG_B = (768, 1280)
SEG_C = (1280, 2048)
SEG_D = (2048, 2432)
SEG_M = (2432, 2560)
SEG_M2 = (2560, 2688)
IN_COLS_PADDED = 2688
HEAD_PAD = N_HEADS * LANES
AUG_LANE = HEAD_DIM
MISC_F = 0
MISC_DT = 4
MISC_ROPE = 64
COL_CUMF = 0
COL_DT = 4
COL_ACUM = 8
N_SCALAR_ROWS = 16


def _cparams(*sem):
    return pltpu.CompilerParams(dimension_semantics=sem, vmem_limit_bytes=VMEM_LIMIT_BYTES)


def _lane_iota(shape):
    return lax.broadcasted_iota(I32, shape, len(shape) - 1)


def _row_iota(shape):
    return lax.broadcasted_iota(I32, shape, 0)


def _rms(x, g):
    ms = jnp.mean(x * x, axis=-1, keepdims=True)
    return x * lax.rsqrt(ms + RMS_EPS) * g


def _silu(x):
    return x / (1.0 + jnp.exp(-x))


def _softplus(x):
    return jnp.maximum(x, 0.0) + jnp.log(1.0 + jnp.exp(-jnp.abs(x)))


def _shift_rows(x, k):
    rolled = pltpu.roll(x, k, 0)
    return jnp.where(_row_iota(x.shape) >= k, rolled, 0.0)


def _rope_kernel(pos_ref, freq_ref, cos_ref, sin_ref):
    q = pos_ref.shape[2]
    lane = _lane_iota((q, LANES))
    group = lane >> int(math.log2(MLA_ROPE))
    rows = jnp.concatenate([pos_ref[0], jnp.zeros((8 - ROPE_PACK, q), F32)], axis=0)
    cols = jnp.concatenate([rows] * (LANES // 8), axis=0).T
    pos = cols[:, ROPE_PACK - 1:ROPE_PACK]
    for k in range(ROPE_PACK - 2, -1, -1):
        pos = jnp.where(group == k, cols[:, k:k + 1], pos)
    ang = pos * freq_ref[...]
    c = jnp.cos(ang)
    s = jnp.sin(ang)
    rope = (lane >= MISC_ROPE) & (lane < MISC_ROPE + MLA_ROPE)
    for k in range(ROPE_PACK):
        shift = (MISC_ROPE - MLA_ROPE * k) % LANES
        ck = pltpu.roll(c, shift, 1) if shift else c
        sk = pltpu.roll(s, shift, 1) if shift else s
        cos_ref[k * q:(k + 1) * q, :] = jnp.where(rope, ck, jnp.where(lane < MISC_ROPE, 1.0, 0.0))
        sin_ref[k * q:(k + 1) * q, :] = jnp.where(rope, sk, 0.0)


ROPE_PACK = LANES // MLA_ROPE


def _rope_tables(positions, tm):
    t = positions.size
    pos = positions.astype(F32).reshape(t // tm, ROPE_PACK, tm // ROPE_PACK)
    inv = ROPE_BASE ** (-np.arange(0, MLA_ROPE, 2, dtype=np.float32) / MLA_ROPE)
    freq = np.tile(np.concatenate([inv, inv]), ROPE_PACK)[None, :].astype(np.float32)
    return pl.pallas_call(
        _rope_kernel,
        grid=(t // tm,),
        in_specs=[pl.BlockSpec((1, ROPE_PACK, tm // ROPE_PACK), lambda i: (i, 0, 0)),
                  pl.BlockSpec((1, LANES), lambda i: (0, 0))],
        out_specs=[pl.BlockSpec((tm, LANES), lambda i: (i, 0))] * 2,
        out_shape=[jax.ShapeDtypeStruct((t, LANES), F32)] * 2,
        compiler_params=_cparams("parallel"),
        name="rope_tables",
    )(pos, jnp.asarray(freq))


def _arrange_w_in_kernel(wt_ref, w_ref, wvt_ref):
    gw = GROUP_WIDTH
    rows = lambda lo, hi: wt_ref[0, lo:hi, :]

    def put(seg0, piece_t):
        w_ref[:, seg0:seg0 + piece_t.shape[0]] = piece_t.T.astype(BF16)

    for j in range(3):
        put(SEG_A[0] + j * gw, rows(j * gw, (j + 1) * gw))
    put(SEG_B[0], rows(3 * gw, 4 * gw) * (HEAD_DIM ** -0.5 * LOG2E))
    put(SEG_B[0] + gw, rows(4 * gw, 5 * gw))
    wvt_ref[...] = rows(5 * gw, 6 * gw).astype(BF16)
    c0 = 6 * gw
    win = pltpu.roll(rows(c0, c0 + 3 * gw + 8), 3 * gw + 8 - N_HEADS, 0)
    for j in range(3):
        put(SEG_C[0] + j * gw, win[j * gw:(j + 1) * gw])
    d0 = c0 + 3 * gw + 8
    put(SEG_D[0], rows(d0, d0 + MLA_Q_LORA))
    put(SEG_D[0] + MLA_Q_LORA, rows(d0 + MLA_Q_LORA, d0 + MLA_Q_LORA + MLA_KV_LORA))
    kr0 = d0 + MLA_Q_LORA + MLA_KV_LORA
    half = MLA_ROPE // 2
    first8 = jnp.where(_row_iota((8, D_MODEL)) < N_HEADS, rows(c0, c0 + 8), rows(d0 - 8, d0))
    zeros = lambda n: jnp.zeros((n, D_MODEL), F32)
    misc_t = jnp.concatenate([first8, zeros(MISC_ROPE - 8), rows(kr0, kr0 + MLA_ROPE),
                              zeros(LANES - MISC_ROPE - MLA_ROPE)], axis=0)
    misc2_t = jnp.concatenate([zeros(MISC_ROPE), -rows(kr0 + half, kr0 + MLA_ROPE), rows(kr0, kr0 + half),
                               zeros(LANES - MISC_ROPE - MLA_ROPE)], axis=0)
    put(SEG_M[0], misc_t)
    put(SEG_M2[0], misc2_t)


def _inproj_kernel(x_ref, g_ref, wt_ref, cw_ref, cos_ref, sin_ref, nq_ref, nkv_ref, wq2_ref, wk_ref,
                   wvt2_ref, oa, ob, oc, om, ovt, q_ref, k_ref, vt_ref, w_ref, wvt_ref, halo_ref,
                   *, tm, tiles_per_seq):
    @pl.when(pl.program_id(0) == 0)
    def _():
        _arrange_w_in_kernel(wt_ref, w_ref, wvt_ref)

    h = _rms(x_ref[...], g_ref[...]).astype(BF16)
    for o, (lo, hi) in ((ob, SEG_B), (oc, SEG_C)):
        o[...] = jnp.dot(h, w_ref[:, lo:hi], preferred_element_type=F32).astype(o.dtype)

    gw = GROUP_WIDTH
    pa = jnp.dot(h, w_ref[:, SEG_A[0]:SEG_A[1]], preferred_element_type=F32)
    cv = pa[:, gw:2 * gw] * pa[:, 2 * gw:3 * gw]

    @pl.when(pl.program_id(0) % tiles_per_seq == 0)
    def _():
        halo_ref[...] = jnp.zeros_like(halo_ref)

    halo = halo_ref[...]
    row8 = _row_iota(halo.shape)
    acc = cv * cw_ref[CONV_A_WIDTH - 1:CONV_A_WIDTH, :]
    for k in range(1, CONV_A_WIDTH):
        shifted = pltpu.roll(cv, k, 0)
        top = jnp.where(row8 < k, pltpu.roll(halo, k, 0), shifted[0:8])
        shifted = jnp.concatenate([top, shifted[8:]], axis=0)
        acc = acc + shifted * cw_ref[CONV_A_WIDTH - 1 - k:CONV_A_WIDTH - k, :]
    halo_ref[...] = cv[tm - 8:tm]
    oa[...] = (pa[:, 0:gw] * acc).astype(oa.dtype)
    ovt[...] = lax.dot_general(wvt_ref[...], h, (((1,), (1,)), ((), ())),
                               preferred_element_type=F32).astype(ovt.dtype)
    misc = jnp.dot(h, w_ref[:, SEG_M[0]:SEG_M[1]], preferred_element_type=F32)
    misc2 = jnp.dot(h, w_ref[:, SEG_M2[0]:SEG_M2[1]], preferred_element_type=F32)
    om[...] = misc

    pd = jnp.dot(h, w_ref[:, SEG_D[0]:SEG_D[1]], preferred_element_type=F32)
    cq = _rms(pd[:, 0:MLA_Q_LORA], nq_ref[...]).astype(BF16)
    ckv = _rms(pd[:, MLA_Q_LORA:MLA_Q_LORA + MLA_KV_LORA], nkv_ref[...]).astype(BF16)
    cos = cos_ref[...]
    sin = sin_ref[...]
    cos4 = jnp.concatenate([cos] * N_HEADS, axis=1)
    sin4 = jnp.concatenate([sin] * N_HEADS, axis=1)
    scale = (MLA_NOPE + MLA_ROPE) ** -0.5 * LOG2E
    q2 = jnp.dot(cq, wq2_ref[...], preferred_element_type=F32)
    q_ref[...] = ((q2[:, 0:HEAD_PAD] * cos4 + q2[:, HEAD_PAD:2 * HEAD_PAD] * sin4) * scale).astype(q_ref.dtype)
    lane = _lane_iota(cos.shape)
    rope = (lane >= MISC_ROPE) & (lane < MISC_ROPE + MLA_ROPE)
    kr = jnp.where(rope, misc * cos + misc2 * sin, 0.0)
    k = jnp.dot(ckv, wk_ref[...], preferred_element_type=F32)
    k_ref[...] = (k + jnp.concatenate([kr] * N_HEADS, axis=1)).astype(k_ref.dtype)
    vt_ref[...] = lax.dot_general(wvt2_ref[...], ckv, (((1,), (1,)), ((), ())),
                                  preferred_element_type=F32).astype(vt_ref.dtype)


def _inproj(x, g, w_in_t, layer, conv_w, cos, sin, nq, nkv, wq2, wk, wvt2, tm, seq):
    t = x.shape[0]
    once = lambda a: pl.BlockSpec((1,) + a.shape[1:], lambda i: (layer, 0, 0), pipeline_mode=pl.Buffered(1))
    full = lambda a: pl.BlockSpec(a.shape, lambda i: (0, 0))
    tile = lambda wd: pl.BlockSpec((tm, wd), lambda i: (i, 0))
    cols = lambda: pl.BlockSpec((GROUP_WIDTH, tm), lambda i: (0, i))
    seg = lambda s: s[1] - s[0]
    return pl.pallas_call(
        functools.partial(_inproj_kernel, tm=tm, tiles_per_seq=seq // tm),
        grid=(t // tm,),
        in_specs=[tile(D_MODEL), full(g), once(w_in_t), full(conv_w), tile(LANES), tile(LANES),
                  full(nq), full(nkv), full(wq2), full(wk), full(wvt2)],
        out_specs=[tile(GROUP_WIDTH), tile(seg(SEG_B)), tile(seg(SEG_C)), tile(LANES), cols(),
                   tile(HEAD_PAD), tile(HEAD_PAD), cols()],
        out_shape=[jax.ShapeDtypeStruct((t, GROUP_WIDTH), BF16), jax.ShapeDtypeStruct((t, seg(SEG_B)), BF16),
                   jax.ShapeDtypeStruct((t, seg(SEG_C)), BF16), jax.ShapeDtypeStruct((t, LANES), F32),
                   jax.ShapeDtypeStruct((GROUP_WIDTH, t), BF16),
                   jax.ShapeDtypeStruct((t, HEAD_PAD), BF16), jax.ShapeDtypeStruct((t, HEAD_PAD), BF16),
                   jax.ShapeDtypeStruct((GROUP_WIDTH, t), BF16)],
        scratch_shapes=[pltpu.VMEM((D_MODEL, IN_COLS_PADDED), BF16), pltpu.VMEM((GROUP_WIDTH, D_MODEL), BF16),
                        pltpu.VMEM((8, GROUP_WIDTH), F32)],
        compiler_params=_cparams("arbitrary"),
        name="inproj",
    )(x, g, w_in_t, conv_w, cos, sin, nq, nkv, wq2, wk, wvt2)


def _scalar_prep_kernel(m_ref, p_ref, qk_ref, place_ref, const_ref,
                        col_ref, row_ref, qa_ref, ka_ref, tref_ref, *, tq):
    s = m_ref.shape[0]
    tref_ref[...] = jnp.zeros_like(tref_ref)
    tile_ref = jnp.zeros((1, LANES), F32)
    m = m_ref[...]
    bias = p_ref[0:1, :]
    a_log = p_ref[1:2, :]
    lane = _lane_iota(m.shape)
    z = m + bias
    logf = jnp.minimum(z, 0.0) - jnp.log(1.0 + jnp.exp(-jnp.abs(z)))
    dt = _softplus(z)
    a = dt * (-jnp.exp(a_log))
    is_f = lane < MISC_DT
    is_dt = (lane >= MISC_DT) & (lane < MISC_DT + N_HEADS)
    v = jnp.where(is_f, logf, jnp.where(is_dt, a, 0.0))
    r = _row_iota((SSM_CHUNK, SSM_CHUNK))
    c = _lane_iota((SSM_CHUNK, SSM_CHUNK))
    tril = jnp.where(r >= c, 1.0, 0.0).astype(BF16)
    carry = jnp.zeros((1, LANES), F32)
    lane_1 = _lane_iota((1, LANES))
    lane_b = _lane_iota((SSM_CHUNK, LANES))
    low = lane_b < HEAD_DIM
    aug_lanes = (lane_b >= AUG_LANE) & (lane_b < AUG_LANE + AUG_TERMS)
    for ci in range(s // SSM_CHUNK):
        rest = v[ci * SSM_CHUNK:(ci + 1) * SSM_CHUNK]
        cs = jnp.zeros((SSM_CHUNK, LANES), F32)
        for _ in range(3):
            term = rest.astype(BF16)
            cs = cs + jnp.dot(tril, term, preferred_element_type=F32)
            rest = rest - term.astype(F32)
        cs = cs + jnp.where(lane_1 < MISC_DT, carry, 0.0)
        carry = cs[SSM_CHUNK - 1:SSM_CHUNK]
        acum = pltpu.roll(cs, COL_ACUM - MISC_DT, 1)
        out = jnp.where(lane_b < MISC_DT, cs * LOG2E,
                        jnp.where(lane_b < COL_ACUM, dt[ci * SSM_CHUNK:(ci + 1) * SSM_CHUNK],
                                  jnp.where(lane_b < COL_ACUM + N_HEADS, acum, 0.0)))
        rows = slice(ci * SSM_CHUNK, (ci + 1) * SSM_CHUNK)
        col_ref[rows, :] = out
        row_ref[0, :, rows] = out.T[:N_SCALAR_ROWS]
        if (ci * SSM_CHUNK) % tq == 0:
            tile_ref = out[0:1, :]
            ti = (ci * SSM_CHUNK) // tq
            tref_ref[0, ti:ti + 1, :] = tile_ref
        c = out - tile_ref
        c_hi = c.astype(BF16)
        r1 = c - c_hi.astype(F32)
        c_mid = r1.astype(BF16)
        c_lo = (r1 - c_mid.astype(F32)).astype(BF16)
        compact = jnp.dot(jnp.concatenate([c_hi, c_mid, c_lo], axis=1), place_ref[...],
                          preferred_element_type=F32) + const_ref[0:1, :]
        for side, o_ref in enumerate((qa_ref, ka_ref)):
            dec = compact[:, side * LANES:(side + 1) * LANES]
            for h in range(N_HEADS):
                pair = qk_ref[rows, side * GROUP_WIDTH + (h // 2) * LANES:
                              side * GROUP_WIDTH + (h // 2 + 1) * LANES].astype(F32)
                feat = pair if h % 2 == 0 else pltpu.roll(pair, HEAD_DIM, 1)
                dec_h = pltpu.roll(dec, AUG_LANE - AUG_TERMS * h, 1)
                group = jnp.where(low, feat, jnp.where(aug_lanes, dec_h, 0.0))
                o_ref[rows, h * LANES:(h + 1) * LANES] = group.astype(o_ref.dtype)


AUG_TERMS = 6


def _fox_placement():
    place = np.zeros((3 * LANES, 2 * LANES), np.float32)
    const = np.zeros((8, 2 * LANES), np.float32)
    for h in range(N_HEADS):
        a0 = AUG_TERMS * h
        for term in range(3):
            place[term * LANES + COL_CUMF + h, a0 + term] = 1.0
            place[term * LANES + COL_CUMF + h, LANES + a0 + 3 + term] = -1.0
            const[0, a0 + 3 + term] = 1.0
            const[0, LANES + a0 + term] = 1.0
    return jnp.asarray(place, BF16), jnp.asarray(const, F32)


def _scalar_prep(misc, params, qk, batch, seq, tq):
    place, const = _fox_placement()
    full = lambda a: pl.BlockSpec(a.shape, lambda b: (0,) * a.ndim)
    return pl.pallas_call(
        functools.partial(_scalar_prep_kernel, tq=tq),
        grid=(batch,),
        in_specs=[pl.BlockSpec((seq, LANES), lambda b: (b, 0)),
                  pl.BlockSpec((8, LANES), lambda b: (0, 0)),
                  pl.BlockSpec((seq, 2 * GROUP_WIDTH), lambda b: (b, 0)),
                  full(place), full(const)],
        out_specs=[pl.BlockSpec((seq, LANES), lambda b: (b, 0)),
                   pl.BlockSpec((1, N_SCALAR_ROWS, seq), lambda b: (b, 0, 0)),
                   pl.BlockSpec((seq, HEAD_PAD), lambda b: (b, 0)),
                   pl.BlockSpec((seq, HEAD_PAD), lambda b: (b, 0)),
                   pl.BlockSpec((1, 8, LANES), lambda b: (b, 0, 0))],
        out_shape=[jax.ShapeDtypeStruct((batch * seq, LANES), F32),
                   jax.ShapeDtypeStruct((batch, N_SCALAR_ROWS, seq), F32),
                   jax.ShapeDtypeStruct((batch * seq, HEAD_PAD), BF16),
                   jax.ShapeDtypeStruct((batch * seq, HEAD_PAD), BF16),
                   jax.ShapeDtypeStruct((batch, 8, LANES), F32)],
        compiler_params=_cparams("parallel"),
        name="scalar_prep",
    )(misc, params, qk, place, const)


def _pair_lanes(col, base, shape):
    lane = _lane_iota(shape)
    return jnp.where(lane < HEAD_DIM, col[:, base:base + 1], col[:, base + 1:base + 2])


def _ssd_kernel(p_ref, col_ref, row_ref, cw_ref, par_ref, o_ref, u_ref):
    s = p_ref.shape[0]
    q = SSM_CHUNK
    gw = GROUP_WIDTH
    xbc = p_ref[:, gw:3 * gw].astype(F32)
    acc = xbc * cw_ref[SSM_CONV - 1:SSM_CONV, :]
    for k in range(1, SSM_CONV):
        acc = acc + _shift_rows(xbc, k) * cw_ref[SSM_CONV - 1 - k:SSM_CONV - k, :]
    u_ref[...] = _silu(acc + cw_ref[SSM_CONV:SSM_CONV + 1, :])

    d_skip = par_ref[0:1, :]
    norm_g = par_ref[1:2, :]
    lane_q = _lane_iota((q, LANES))
    low = lane_q < HEAD_DIM
    tri = _row_iota((q, q)) >= _lane_iota((q, q))

    def chunk(ci, states):
        rows = pl.ds(ci * q, q)
        u = u_ref[rows, :]
        col = col_ref[rows, :]
        bm = u[:, gw:gw + LANES]
        cm = u[:, gw + LANES:gw + 2 * LANES]
        z = p_ref[rows, 0:gw].astype(F32)
        new_states = []
        ys = []
        for g in range(2):
            sel = low if g == 0 else jnp.logical_not(low)
            cg = jnp.where(sel, cm, 0.0).astype(BF16)
            bg = jnp.where(sel, bm, 0.0)
            gmat = lax.dot_general(cg, bm.astype(BF16), (((1,), (1,)), ((), ())),
                                   preferred_element_type=F32)
            xs = u[:, g * LANES:(g + 1) * LANES]
            dt2 = _pair_lanes(col, COL_DT + 2 * g, (q, LANES))
            ac2 = _pair_lanes(col, COL_ACUM + 2 * g, (q, LANES))
            xdt = xs * dt2
            xdt_b = xdt.astype(BF16)
            st = states[g]
            y_off = jnp.dot(cg, st.astype(BF16), preferred_element_type=F32) * jnp.exp(ac2)
            halves = []
            for hh in range(2):
                h = 2 * g + hh
                ac_col = col[:, COL_ACUM + h:COL_ACUM + h + 1]
                ac_row = row_ref[0, COL_ACUM + h:COL_ACUM + h + 1, rows]
                decay = jnp.exp(jnp.where(tri, ac_col - ac_row, -1e30))
                mm = (gmat * decay).astype(BF16)
                halves.append(jnp.dot(mm, xdt_b, preferred_element_type=F32))
            y = jnp.where(low, halves[0], halves[1]) + y_off + d_skip[:, g * LANES:(g + 1) * LANES] * xs
            ys.append(y)
            ac_last = ac2[q - 1:q, :]
            w_end = jnp.exp(ac_last - ac2)
            xw = (xdt * w_end).astype(BF16)
            upd = jnp.dot(bg.T.astype(BF16), xw, preferred_element_type=F32)
            new_states.append(st * jnp.exp(ac_last) + upd)
        yfull = jnp.concatenate(ys, axis=1) * _silu(z)
        o_ref[rows, :] = _rms(yfull, norm_g).astype(o_ref.dtype)
        return tuple(new_states)

    init = (jnp.zeros((LANES, LANES), F32), jnp.zeros((LANES, LANES), F32))
    states = init
    for ci in range(s // q):
        states = chunk(ci, states)


def _ssd_mixer(pc, col, rows, conv_wb, par, batch, seq):
    gw = GROUP_WIDTH
    return pl.pallas_call(
        _ssd_kernel,
        grid=(batch,),
        in_specs=[pl.BlockSpec((seq, 3 * gw), lambda b: (b, 0)),
                  pl.BlockSpec((seq, LANES), lambda b: (b, 0)),
                  pl.BlockSpec((1, N_SCALAR_ROWS, seq), lambda b: (b, 0, 0)),
                  pl.BlockSpec((8, 2 * gw), lambda b: (0, 0)),
                  pl.BlockSpec((8, gw), lambda b: (0, 0))],
        out_specs=pl.BlockSpec((seq, gw), lambda b: (b, 0)),
        out_shape=jax.ShapeDtypeStruct((batch * seq, gw), BF16),
        scratch_shapes=[pltpu.VMEM((seq, 2 * gw), F32)],
        compiler_params=_cparams("parallel"),
        name="ssd_mixer",
    )(pc, col, rows, conv_wb, par)


def _attn_kernel(*refs, fox, tq, nq):
    if fox:
        tref_ref, q_ref, k_ref, vt_ref, o_ref = refs
    else:
        q_ref, k_ref, vt_ref, o_ref = refs
        tref_ref = None
    b = pl.program_id(0)
    i = pl.program_id(1)
    key = _row_iota((tq, tq))
    qry = _lane_iota((tq, tq))
    if fox:
        allowed = key <= qry
    else:
        shift = int(math.log2(MLA_CHUNK))
        allowed = (key >> shift) <= (qry >> shift)
    qs = [q_ref[:, h * LANES:(h + 1) * LANES] for h in range(N_HEADS)]
    ones_rows = jnp.ones((16, tq), BF16)

    def step(j, masked, carry):
        rk = pl.ds(j * tq, tq)
        scores = [lax.dot_general(k_ref[rk, h * LANES:(h + 1) * LANES], qs[h], (((1,), (1,)), ((), ())),
                                  preferred_element_type=F32) for h in range(N_HEADS)]
        probs = []
        for h in range(N_HEADS):
            m, l, _ = carry[h]
            s = scores[h]
            if masked:
                s = jnp.where(allowed, s, -1e30)
            delta = (tref_ref[b, i, h] - tref_ref[b, j, h]) if fox else 0.0
            m_new = jnp.maximum(m, jnp.max(s, axis=0, keepdims=True) + delta)
            alpha = jnp.exp2(m - m_new)
            p = jnp.exp2(s - (m_new - delta))
            probs.append((m_new, alpha, p.astype(BF16)))
        new = []
        for h in range(N_HEADS):
            pair = h // 2
            m_new, alpha, p = probs[h]
            lhs = jnp.concatenate([vt_ref[pair * LANES:(pair + 1) * LANES, rk], ones_rows], axis=0)
            pv = jnp.dot(lhs, p, preferred_element_type=F32)
            new.append((m_new, alpha * carry[h][1] + pv[LANES:LANES + 1], alpha * carry[h][2] + pv[0:LANES]))
        return tuple(new)

    init = tuple((jnp.full((1, tq), -1e30, F32), jnp.zeros((1, tq), F32), jnp.zeros((LANES, tq), F32))
                 for _ in range(N_HEADS))
    top = _row_iota((LANES, tq)) < HEAD_DIM

    def sweep(n_before):
        carry = init
        for j in range(n_before):
            carry = step(j, False, carry)
        carry = step(n_before, True, carry)
        outs = [acc / l for (_, l, acc) in carry]
        o_t = jnp.concatenate([jnp.where(top, outs[0], outs[1]), jnp.where(top, outs[2], outs[3])], axis=0)
        o_ref[...] = o_t.T.astype(o_ref.dtype)

    for n_before in range(nq):
        pl.when(i == n_before)(functools.partial(sweep, n_before))


def _attention(q, k, vt, tref, batch, seq, tq, name):
    nq = seq // tq
    fox = tref is not None
    kern = functools.partial(_attn_kernel, fox=fox, tq=tq, nq=nq)
    grid_spec = pltpu.PrefetchScalarGridSpec(
        num_scalar_prefetch=1 if fox else 0,
        grid=(batch, nq),
        in_specs=[pl.BlockSpec((tq, HEAD_PAD), lambda b, i, *_: (b * nq + i, 0)),
                  pl.BlockSpec((seq, HEAD_PAD), lambda b, i, *_: (b, 0)),
                  pl.BlockSpec((GROUP_WIDTH, seq), lambda b, i, *_: (0, b))],
        out_specs=pl.BlockSpec((tq, GROUP_WIDTH), lambda b, i, *_: (b * nq + i, 0)),
    )
    args = ((tref,) if fox else ()) + (q, k, vt)
    return pl.pallas_call(
        kern,
        grid_spec=grid_spec,
        out_shape=jax.ShapeDtypeStruct((batch * seq, GROUP_WIDTH), BF16),
        compiler_params=_cparams("parallel", "arbitrary"),
        name=name,
    )(*args)


def _outproj_kernel(x_ref, ya, yb, yc, yd, w_ref, g_ref, wr_ref, br_ref,
                    x2_ref, h2_ref, rrow_ref, cnt_ref, *, tm, moe_tile):
    y = jnp.concatenate([ya[...], yb[...], yc[...], yd[...]], axis=1)
    x2 = x_ref[...] + jnp.dot(y, w_ref[...], preferred_element_type=F32)
    x2_ref[...] = x2
    h2 = _rms(x2, g_ref[...])
    h2_ref[...] = h2.astype(h2_ref.dtype)
    h_hi = h2.astype(BF16)
    h_lo = (h2 - h_hi.astype(F32)).astype(BF16)
    part = jnp.dot(h_hi, wr_ref[...], preferred_element_type=F32)
    logits = (part[:, 0:LANES] + part[:, LANES:2 * LANES]
              + jnp.dot(h_lo, wr_ref[:, 0:LANES], preferred_element_type=F32) + br_ref[...])
    lt = logits.T[0:ROUTER_ROWS]
    row = _row_iota(lt.shape)
    neg = -1e30
    big = 1 << 20
    gmask = (row >= N_EXPERTS) & (row < N_EXPERTS + N_EXPERT_GROUPS)
    gl = jnp.where(gmask, lt, neg)
    gmax = jnp.max(gl, axis=0, keepdims=True)
    gsum = jnp.sum(jnp.where(gmask, jnp.exp(gl - gmax), 0.0), axis=0, keepdims=True)
    g_w = 1.0 / gsum
    g_idx = jnp.min(jnp.where(gmask & (gl == gmax), row, big), axis=0, keepdims=True) - N_EXPERTS
    emask = (row < N_EXPERTS) & ((row >> int(math.log2(EXPERTS_PER_GROUP))) == g_idx)
    el = jnp.where(emask, lt, neg)
    e1v = jnp.max(el, axis=0, keepdims=True)
    esum = jnp.sum(jnp.where(emask, jnp.exp(el - e1v), 0.0), axis=0, keepdims=True)
    i1 = jnp.min(jnp.where(emask & (el == e1v), row, big), axis=0, keepdims=True)
    el2 = jnp.where(row == i1, neg, el)
    e2v = jnp.max(el2, axis=0, keepdims=True)
    i2 = jnp.min(jnp.where(emask & (row != i1) & (el2 == e2v), row, big), axis=0, keepdims=True)
    p1 = 1.0 / esum
    p2 = jnp.exp(e2v - e1v) / esum
    w1 = g_w * (p1 / (p1 + p2))
    w2 = g_w * (p2 / (p1 + p2))
    out_row = _row_iota(rrow_ref.shape)
    rrow_ref[...] = jnp.where(out_row == 0, i1.astype(F32),
                              jnp.where(out_row == 1, i2.astype(F32),
                                        jnp.where(out_row == 2, w1, jnp.where(out_row == 3, w2, 0.0))))
    step = pl.program_id(0)

    @pl.when(step == 0)
    def _():
        cnt_ref[...] = jnp.zeros_like(cnt_ref)

    chosen = jnp.where((row == i1) | (row == i2), 1.0, 0.0).astype(BF16)
    tiles = tm // moe_tile
    tile_of = (_row_iota((tm, LANES)) >> int(math.log2(moe_tile))) + step * tiles
    to_tile = jnp.where(_lane_iota((tm, LANES)) == tile_of, 1.0, 0.0).astype(BF16)
    counts = jnp.dot(chosen, to_tile, preferred_element_type=F32)
    cnt_ref[...] += counts[0:N_EXPERTS]


def _outproj(x, ya, yb, yc, yd, w, g, wr, br, tm, moe_tile):
    t = x.shape[0]
    full = lambda a: pl.BlockSpec(a.shape, lambda i: (0, 0))
    tile = lambda wd: pl.BlockSpec((tm, wd), lambda i: (i, 0))
    return pl.pallas_call(
        functools.partial(_outproj_kernel, tm=tm, moe_tile=moe_tile),
        grid=(t // tm,),
        in_specs=[tile(D_MODEL)] + [tile(GROUP_WIDTH)] * 4 + [full(w), full(g), full(wr), full(br)],
        out_specs=[tile(D_MODEL), tile(D_MODEL), pl.BlockSpec((8, tm), lambda i: (0, i)),
                   pl.BlockSpec((N_EXPERTS, LANES), lambda i: (0, 0))],
        out_shape=[jax.ShapeDtypeStruct((t, D_MODEL), F32), jax.ShapeDtypeStruct((t, D_MODEL), BF16),
                   jax.ShapeDtypeStruct((8, t), F32), jax.ShapeDtypeStruct((N_EXPERTS, LANES), F32)],
        compiler_params=_cparams("arbitrary"),
        name="outproj_router",
    )(x, ya, yb, yc, yd, w, g, wr, br)


MOE_TILE = 256
CHUNK = 8
LOCAL_ROWS = 2 * MOE_TILE + 256
PACKED = D_MODEL // 2
XS_WIDTH = PACKED
U32 = jnp.uint32


def _pack_bf16_pairs(x, exact=False):
    if not exact:
        x = x.astype(BF16).astype(F32)
    half = x.shape[1] // 2
    lo = lax.bitcast_convert_type(x[:, :half], U32)
    hi = lax.bitcast_convert_type(x[:, half:], U32)
    return hi | (lo >> 16)


def _unpack_bf16_pairs(words):
    lo = lax.bitcast_convert_type(words << 16, F32)
    hi = lax.bitcast_convert_type(words & U32(0xFFFF0000), F32)
    return jnp.concatenate([lo, hi], axis=1).astype(BF16)


COPY_ROWS = (2 * CHUNK, CHUNK)


def _route_kernel(r_ref, cnt_ref, lrow_ref, lcol_ref, tab_ref, ctab_ref, meta_ref,
                  loff_ref, goff_ref, n8_ref, *, tm, nbp, tiles):
    i = pl.program_id(0)
    hi = lax.Precision.HIGHEST

    @pl.when(i == 0)
    def _():
        cnt = cnt_ref[...]
        n8 = jnp.floor((cnt + (CHUNK - 1)) * (1.0 / CHUNK)) * CHUNK
        er = _row_iota((N_EXPERTS, N_EXPERTS))
        ec = _lane_iota((N_EXPERTS, N_EXPERTS))
        below = jnp.where(er > ec, 1.0, 0.0)
        loff = jnp.dot(below, n8, preferred_element_type=F32, precision=hi)
        rows_e = jnp.sum(n8, axis=-1, keepdims=True) + jnp.zeros_like(n8)
        padded = jnp.floor((rows_e + (MOE_ROWS - 1)) * (1.0 / MOE_ROWS)) * MOE_ROWS
        e_start = jnp.dot(below, padded, preferred_element_type=F32, precision=hi)
        tr = _row_iota((LANES, LANES))
        tc = _lane_iota((LANES, LANES))
        earlier = jnp.where(tr < tc, 1.0, 0.0)
        goff = e_start + jnp.dot(n8, earlier, preferred_element_type=F32, precision=hi)
        loff_ref[...] = loff
        goff_ref[...] = goff
        n8_ref[...] = n8
        big = jnp.floor(n8 * (0.5 / CHUNK))
        small = n8 * (1.0 / CHUNK) - 2.0 * big
        row_t = _row_iota(tab_ref.shape)
        tab_ref[...] = jnp.where(row_t == 0, jnp.sum(big, axis=0, keepdims=True),
                                 jnp.where(row_t == 1, jnp.sum(small, axis=0, keepdims=True), 0.0)).astype(I32)
        reps = nbp // LANES
        pend_b = jnp.concatenate([e_start + padded] * reps, axis=1)
        vend_b = jnp.concatenate([e_start + rows_e] * reps, axis=1)
        used_b = jnp.concatenate([padded] * reps, axis=1) > 0.0
        b0 = (_lane_iota((N_EXPERTS, nbp)) * MOE_ROWS).astype(F32)
        bexp = jnp.sum(jnp.where(pend_b <= b0, 1.0, 0.0), axis=0, keepdims=True)
        bexp = jnp.minimum(bexp, N_EXPERTS - 1.0)
        e_b = _row_iota((N_EXPERTS, nbp)).astype(F32)
        is_e = e_b == bexp
        vend = jnp.sum(jnp.where(is_e, vend_b, 0.0), axis=0, keepdims=True)
        nvalid = jnp.clip(vend - b0[0:1], 0.0, float(MOE_ROWS))
        total = jnp.max(pend_b, axis=0, keepdims=True) * (1.0 / MOE_ROWS)
        order = jnp.sum(jnp.where(used_b & (e_b < bexp), 1.0, 0.0), axis=0, keepdims=True)
        nxt = jnp.min(jnp.where(used_b & (e_b > bexp), e_b, float(N_EXPERTS)), axis=0, keepdims=True)
        row = _row_iota((8, nbp))
        meta = jnp.where(row == 0, bexp, jnp.where(row == 1, nvalid, jnp.where(row == 2, total,
                         jnp.where(row == 3, order, jnp.where(row == 4, nxt, 0.0)))))
        meta_ref[...] = meta.astype(I32)

    su = jnp.where(_row_iota((tm, tm)) < _lane_iota((tm, tm)), 1.0, 0.0).astype(BF16)
    incl = jnp.where(_row_iota((N_EXPERTS, N_EXPERTS)) >= _lane_iota((N_EXPERTS, N_EXPERTS)), 1.0, 0.0)
    cidx = _lane_iota((N_EXPERTS, LANES)).astype(F32)
    e_iota = _row_iota((N_EXPERTS, tm))
    out_row = _row_iota((LANES, tm))
    for k in range(tiles):
        tok = slice(k * tm, (k + 1) * tm)
        oh0 = e_iota == r_ref[0:1, tok].astype(I32)
        oh1 = e_iota == r_ref[1:2, tok].astype(I32)
        oh = jnp.where(oh0 | oh1, 1.0, 0.0)
        tile_lane = _lane_iota((N_EXPERTS, LANES)) == i * tiles + k
        before = jnp.dot(oh.astype(BF16), su, preferred_element_type=F32)
        base = jnp.sum(jnp.where(tile_lane, loff_ref[...], 0.0), axis=-1, keepdims=True) + before
        d0 = jnp.sum(jnp.where(oh0, base, 0.0), axis=0, keepdims=True)
        d1 = jnp.sum(jnp.where(oh1, base, 0.0), axis=0, keepdims=True)
        lrow_ref[k, 0:1, :] = d0.astype(I32)
        lrow_ref[k, 1:2, :] = d1.astype(I32)
        pick_tile = lambda ref: jnp.sum(jnp.where(tile_lane, ref[...], 0.0), axis=-1, keepdims=True)
        nch = pick_tile(n8_ref) * (1.0 / CHUNK)
        n_big = jnp.floor(nch * 0.5)
        n_small = nch - 2.0 * n_big
        loff_t = pick_tile(loff_ref)
        goff_t = pick_tile(goff_ref)
        for c, (n, rows, first) in enumerate(((n_big, COPY_ROWS[0], 0.0),
                                              (n_small, COPY_ROWS[1], n_big * COPY_ROWS[0]))):
            cend = jnp.dot(incl, n + jnp.zeros((N_EXPERTS, LANES), F32), preferred_element_type=F32, precision=hi)
            cstart = cend - n
            mine = (cidx >= cstart) & (cidx < cend)
            step_rows = first + (cidx - cstart) * rows
            ctab_ref[k, 2 * c:2 * c + 1, :] = jnp.sum(jnp.where(mine, loff_t + step_rows, 0.0), axis=0,
                                                      keepdims=True).astype(I32)
            ctab_ref[k, 2 * c + 1:2 * c + 2, :] = jnp.sum(jnp.where(mine, goff_t + step_rows, 0.0), axis=0,
                                                          keepdims=True).astype(I32)
        lcol_ref[tok, :] = jnp.where(out_row == 0, d0, jnp.where(out_row == 1, d1,
                                     jnp.where(out_row == 2, r_ref[2:3, tok],
                                               jnp.where(out_row == 3, r_ref[3:4, tok], 0.0)))).T


def _route(rrow, cnt, tm, nbp):
    t = rrow.shape[1]
    nt = t // tm
    tiles = next(r for r in (4, 2, 1) if nt % r == 0)
    kern = functools.partial(_route_kernel, tm=tm, nbp=nbp, tiles=tiles)
    return pl.pallas_call(
        kern,
        grid=(nt // tiles,),
        in_specs=[pl.BlockSpec((8, tiles * tm), lambda i: (0, i)),
                  pl.BlockSpec((N_EXPERTS, LANES), lambda i: (0, 0))],
        out_specs=[pl.BlockSpec((tiles, 2, tm), lambda i: (i, 0, 0)),
                   pl.BlockSpec((tiles * tm, LANES), lambda i: (i, 0)),
                   pl.BlockSpec((8, LANES), lambda i: (0, 0)),
                   pl.BlockSpec((tiles, 2 * len(COPY_ROWS), LANES), lambda i: (i, 0, 0)),
                   pl.BlockSpec((8, nbp), lambda i: (0, 0))],
        out_shape=[jax.ShapeDtypeStruct((nt, 2, tm), I32), jax.ShapeDtypeStruct((t, LANES), F32),
                   jax.ShapeDtypeStruct((8, LANES), I32), jax.ShapeDtypeStruct((nt, 2 * len(COPY_ROWS), LANES), I32),
                   jax.ShapeDtypeStruct((8, nbp), I32)],
        scratch_shapes=[pltpu.VMEM((N_EXPERTS, LANES), F32)] * 3,
        compiler_params=_cparams("arbitrary"),
        name="moe_route",
    )(rrow, cnt)


def _chunk_copies(tabs, i, local_ref, global_ref, sem, to_global, action):
    tab_ref, ctab_ref = tabs
    for k, rows in enumerate(COPY_ROWS):
        count = tab_ref[k, i]

        def copy(lo, go, rows=rows):
            lsl = local_ref.at[pl.ds(pl.multiple_of(lo, CHUNK), rows)]
            gsl = global_ref.at[pl.ds(pl.multiple_of(go, CHUNK), rows)]
            return pltpu.make_async_copy(lsl, gsl, sem) if to_global else pltpu.make_async_copy(gsl, lsl, sem)

        if action == "wait":
            def one(c, c1, copy=copy):
                copy(0, 0).wait()
                return c1
        else:
            def one(c, c1, copy=copy, k=k):
                copy(ctab_ref[i, 2 * k, c], ctab_ref[i, 2 * k + 1, c]).start()
                return c1

        lax.fori_loop(0, count, one, 0)


def _scatter_kernel(tab_ref, ctab_ref, meta_ref, lrow_ref, h_ref, xs_ref, buf_ref, zero_ref, sem, zsem, *, tm, nb):
    i = pl.program_id(0)
    tabs = (tab_ref, ctab_ref)

    @pl.when(i == 0)
    def _():
        zero_ref[...] = jnp.zeros_like(zero_ref)
        n_used = meta_ref[2, 0]

        def each_piece(action):
            def body(b, c):
                for piece in range(MOE_ROWS // ZERO_ROWS):
                    @pl.when((b < n_used) & (meta_ref[1, b] < (piece + 1) * ZERO_ROWS))
                    def _():
                        start = pl.multiple_of(b * MOE_ROWS + piece * ZERO_ROWS, ZERO_ROWS)
                        cp = pltpu.make_async_copy(zero_ref, xs_ref.at[pl.ds(start, ZERO_ROWS)], zsem)
                        getattr(cp, action)()
                return c
            lax.fori_loop(0, nb, body, 0)

        each_piece("start")
        each_piece("wait")

    rows = _row_iota((LOCAL_ROWS, tm))

    def drain(tile, slot):
        _chunk_copies(tabs, tile, buf_ref.at[slot], xs_ref, sem.at[slot], True, "wait")

    for slot in range(2):
        tile = 2 * i + slot
        p0 = rows == lrow_ref[slot, 0:1, :]
        p1 = rows == lrow_ref[slot, 1:2, :]
        perm = jnp.where(p0 | p1, 1.0, 0.0).astype(BF16)
        sorted_rows = jnp.dot(perm, h_ref[slot * tm:(slot + 1) * tm, :],
                              preferred_element_type=F32)

        @pl.when(i > 0)
        def _():
            drain(tile - 2, slot)

        buf = buf_ref.at[slot]
        buf[...] = _pack_bf16_pairs(sorted_rows, exact=True)
        _chunk_copies(tabs, tile, buf, xs_ref, sem.at[slot], True, "start")

    @pl.when(i == pl.num_programs(0) - 1)
    def _():
        drain(2 * i, 0)
        drain(2 * i + 1, 1)


def _scatter(tab, ctab, meta, lrow, h2, tm, nb):
    t = h2.shape[0]
    kern = functools.partial(_scatter_kernel, tm=tm, nb=nb)
    grid_spec = pltpu.PrefetchScalarGridSpec(
        num_scalar_prefetch=3,
        grid=(t // (2 * tm),),
        in_specs=[pl.BlockSpec((2, 2, tm), lambda i, *_: (i, 0, 0)),
                  pl.BlockSpec((2 * tm, D_MODEL), lambda i, *_: (i, 0))],
        out_specs=pl.BlockSpec(memory_space=pl.ANY),
        scratch_shapes=[pltpu.VMEM((2, LOCAL_ROWS, XS_WIDTH), U32), pltpu.VMEM((ZERO_ROWS, XS_WIDTH), U32),
                        pltpu.SemaphoreType.DMA((2,)), pltpu.SemaphoreType.DMA],
    )
    return pl.pallas_call(
        kern,
        grid_spec=grid_spec,
        out_shape=jax.ShapeDtypeStruct((nb * MOE_ROWS, XS_WIDTH), U32),
        compiler_params=_cparams("arbitrary"),
        name="moe_scatter",
    )(tab, ctab, meta, lrow, h2)


def _ffn_kernel(meta_ref, x_ref, wg_ref, wu_ref, wd_ref, o_ref, wgu_b, wd_b, wg_f, wu_f, wd_f, wsem, *, layer):
    b = pl.program_id(0)
    live = b < meta_ref[2, 0]
    expert = meta_ref[0, b]
    prev = meta_ref[0, jnp.maximum(b - 1, 0)]

    def fetch(e, slot):
        return [pltpu.make_async_copy(src.at[layer, e], dst.at[slot], wsem.at[slot])
                for src, dst in ((wg_ref, wg_f), (wu_ref, wu_f), (wd_ref, wd_f))]

    def first_block(slot):
        @pl.when(b == 0)
        def _():
            for cp in fetch(expert, slot):
                cp.start()

        for cp in fetch(expert, slot):
            cp.wait()
        wgu_b[:, 0:EXPERT_FF] = wg_f[slot].astype(BF16)
        wgu_b[:, EXPERT_FF:2 * EXPERT_FF] = wu_f[slot].astype(BF16)
        wd_b[...] = wd_f[slot].astype(BF16)
        nxt = meta_ref[4, b]

        @pl.when(nxt < N_EXPERTS)
        def _():
            for cp in fetch(nxt, 1 - slot):
                cp.start()

    changed = live & ((b == 0) | (expert != prev))
    odd = (meta_ref[3, b] & 1) == 1

    @pl.when(changed & jnp.logical_not(odd))
    def _():
        first_block(0)

    @pl.when(changed & odd)
    def _():
        first_block(1)

    @pl.when(live)
    def _():
        x = _unpack_bf16_pairs(x_ref[...])
        gu = jnp.dot(x, wgu_b[...], preferred_element_type=F32)
        act = (_silu(gu[:, 0:EXPERT_FF]) * gu[:, EXPERT_FF:2 * EXPERT_FF]).astype(BF16)
        y = jnp.dot(act, wd_b[...], preferred_element_type=F32)
        o_ref[...] = _pack_bf16_pairs(y)


def _ffn(meta, xs, wg, wu, wd, layer, nb):
    def blk(b, m):
        return (jnp.maximum(jnp.minimum(b, m[2, 0] - 1), 0), 0)

    grid_spec = pltpu.PrefetchScalarGridSpec(
        num_scalar_prefetch=1,
        grid=(nb,),
        in_specs=[pl.BlockSpec((MOE_ROWS, XS_WIDTH), blk)] + [pl.BlockSpec(memory_space=pl.ANY)] * 3,
        out_specs=pl.BlockSpec((MOE_ROWS, PACKED), blk),
        scratch_shapes=[pltpu.VMEM((D_MODEL, 2 * EXPERT_FF), BF16), pltpu.VMEM((EXPERT_FF, D_MODEL), BF16),
                        pltpu.VMEM((2, D_MODEL, EXPERT_FF), F32), pltpu.VMEM((2, D_MODEL, EXPERT_FF), F32),
                        pltpu.VMEM((2, EXPERT_FF, D_MODEL), F32), pltpu.SemaphoreType.DMA((2,))],
    )
    return pl.pallas_call(
        functools.partial(_ffn_kernel, layer=layer),
        grid_spec=grid_spec,
        out_shape=jax.ShapeDtypeStruct((nb * MOE_ROWS, PACKED), U32),
        compiler_params=_cparams("arbitrary"),
        name="moe_experts",
    )(meta, xs, wg, wu, wd)


def _gather_kernel(tab_ref, ctab_ref, lcol_ref, x_ref, g_ref, ys_ref, o_ref, buf_ref, sem, *, tm, final):
    i = pl.program_id(0)

    last = pl.num_programs(0) - 1

    def fetch(tile, slot, action):
        _chunk_copies((tab_ref, ctab_ref), tile, buf_ref.at[slot], ys_ref, sem.at[slot], False, action)

    @pl.when(i == 0)
    def _():
        buf_ref[...] = jnp.zeros_like(buf_ref)
        fetch(0, 0, "start")

    col = _lane_iota((tm, LOCAL_ROWS)).astype(F32)
    for slot in range(2):
        tile = 2 * i + slot
        tok = slice(slot * tm, (slot + 1) * tm)
        if slot == 0:
            fetch(tile + 1, 1, "start")
        else:
            @pl.when(i < last)
            def _():
                fetch(tile + 1, 0, "start")

        lc = lcol_ref[tok, :]
        pick0 = jnp.where(col == lc[:, 0:1], 1.0, 0.0).astype(BF16)
        pick1 = jnp.where(col == lc[:, 1:2], 1.0, 0.0).astype(BF16)
        fetch(tile, slot, "wait")
        y = _unpack_bf16_pairs(buf_ref[slot])
        both = jnp.dot(jnp.concatenate([pick0, pick1], axis=0), y, preferred_element_type=F32)
        x = x_ref[tok, :] + lc[:, 2:3] * both[0:tm] + lc[:, 3:4] * both[tm:2 * tm]
        o_ref[tok, :] = _rms(x, g_ref[...]) if final else x


def _gather(tab, ctab, lcol, x2, g, ys, tm, final):
    t = x2.shape[0]
    kern = functools.partial(_gather_kernel, tm=tm, final=final)
    grid_spec = pltpu.PrefetchScalarGridSpec(
        num_scalar_prefetch=2,
        grid=(t // (2 * tm),),
        in_specs=[pl.BlockSpec((2 * tm, LANES), lambda i, *_: (i, 0)),
                  pl.BlockSpec((2 * tm, D_MODEL), lambda i, *_: (i, 0)),
                  pl.BlockSpec((1, D_MODEL), lambda i, *_: (0, 0)),
                  pl.BlockSpec(memory_space=pl.ANY)],
        out_specs=pl.BlockSpec((2 * tm, D_MODEL), lambda i, *_: (i, 0)),
        scratch_shapes=[pltpu.VMEM((2, LOCAL_ROWS, PACKED), U32), pltpu.SemaphoreType.DMA((2,))],
    )
    return pl.pallas_call(
        kern,
        grid_spec=grid_spec,
        out_shape=jax.ShapeDtypeStruct((t, D_MODEL), F32),
        compiler_params=_cparams("arbitrary"),
        name="moe_combine",
    )(tab, ctab, lcol, x2, g, ys)


def _pad_rows(a, rows=8):
    return jnp.zeros((rows, a.shape[-1]), F32).at[:a.shape[0]].set(a.astype(F32))


def _arrange_mla(w_uq, w_ukv):
    half = MLA_ROPE // 2
    qd = MLA_NOPE + MLA_ROPE
    wq, wqs, wk, wv = [], [], [], []
    zq = jnp.zeros((MLA_Q_LORA, LANES - qd), w_uq.dtype)
    zk = jnp.zeros((MLA_KV_LORA, LANES - MLA_NOPE), w_ukv.dtype)
    for h in range(N_HEADS):
        q = w_uq[:, h * qd:(h + 1) * qd]
        nope, rope = q[:, :MLA_NOPE], q[:, MLA_NOPE:]
        wq.append(jnp.concatenate([nope, rope, zq], axis=1))
        wqs.append(jnp.concatenate([jnp.zeros_like(nope), -rope[:, half:], rope[:, :half], zq], axis=1))
        kv = w_ukv[:, h * 2 * MLA_NOPE:(h + 1) * 2 * MLA_NOPE]
        wk.append(jnp.concatenate([kv[:, :MLA_NOPE], zk], axis=1))
        wv.append(kv[:, MLA_NOPE:])
    cat = lambda xs: jnp.concatenate(xs, axis=1).astype(BF16)
    return cat(wq), cat(wqs), cat(wk), cat(wv).T


def kernel(x, positions, norm_mix, w_in, conv_a, fox_forget_bias, ssm_conv_w, ssm_conv_b, ssm_dt_bias,
           ssm_a_log, ssm_d, ssm_norm, mla_q_norm, mla_kv_norm, mla_w_uq, mla_w_ukv, w_out, norm_ffn,
           router_group_w, router_group_b, router_expert_w, router_expert_b, expert_w_gate, expert_w_up,
           expert_w_down, norm_final):
    batch, seq, d = x.shape
    t = batch * seq
    depth = w_in.shape[0]
    tm = min(ROW_TILE, seq)
    tq = min(ATTN_TQ, seq)
    tmd = MOE_TILE
    assert d == D_MODEL and seq % tm == 0 and seq % tq == 0 and tq % SSM_CHUNK == 0, (x.shape,)
    assert seq // tq <= 8 and tm % (LANES * ROPE_PACK) == 0, (seq, tm)
    assert t % (2 * tmd) == 0 and t // tmd <= LANES and tm % tmd == 0, (t, tmd)
    max_rows = 2 * t + (CHUNK - 1) * N_EXPERTS * (t // tmd) + N_EXPERTS * (MOE_ROWS - 1)
    nb = -(-max_rows // MOE_ROWS)
    nbp = -(-nb // LANES) * LANES

    xf = x.reshape(t, d)
    cos, sin = _rope_tables(positions, tm)
    w_in_t = jnp.swapaxes(w_in, 1, 2)

    for l in range(depth):
        wq, wqs, wk, wv = _arrange_mla(mla_w_uq[l], mla_w_ukv[l])
        ya, pb, pc, misc, fox_vt, q, k, v = _inproj(
            xf, norm_mix[l][None, :], w_in_t, l, _pad_rows(conv_a[l]), cos, sin, mla_q_norm[l][None, :],
            mla_kv_norm[l][None, :], jnp.concatenate([wq, wqs], axis=1), wk, wv, tm, seq)

        sp = jnp.zeros((8, LANES), F32)
        sp = sp.at[0, MISC_F:MISC_F + N_HEADS].set(fox_forget_bias[l])
        sp = sp.at[0, MISC_DT:MISC_DT + N_HEADS].set(ssm_dt_bias[l])
        sp = sp.at[1, MISC_DT:MISC_DT + N_HEADS].set(ssm_a_log[l])
        col, rows, fox_q, fox_k, tref = _scalar_prep(misc, sp, pb, batch, seq, tq)

        yb = _attention(fox_q, fox_k, fox_vt, tref, batch, seq, tq, "fox_attention")
        conv_wb = _pad_rows(jnp.concatenate([ssm_conv_w[l], ssm_conv_b[l][None, :]], axis=0))
        ssd_par = _pad_rows(jnp.stack([jnp.repeat(ssm_d[l], HEAD_DIM), ssm_norm[l]]))
        yc = _ssd_mixer(pc, col, rows, conv_wb, ssd_par, batch, seq)
        yd = _attention(q, k, v, None, batch, seq, tq, "mla_attention")

        pad = jnp.zeros((d, LANES - N_EXPERTS - N_EXPERT_GROUPS), F32)
        wr = jnp.concatenate([router_expert_w[l], router_group_w[l], pad], axis=1)
        wr_hi = wr.astype(BF16)
        wr = jnp.concatenate([wr_hi, (wr - wr_hi.astype(F32)).astype(BF16)], axis=1)
        br = jnp.concatenate([router_expert_b[l], router_group_b[l], pad[0]])[None, :]
        x2, h2, rrow, cnt = _outproj(xf, ya, yb, yc, yd, w_out[l].astype(BF16), norm_ffn[l][None, :], wr, br,
                                     tm, tmd)

        lrow, lcol, tab, ctab, meta = _route(rrow, cnt, tmd, nbp)
        xs = _scatter(tab, ctab, meta, lrow, h2, tmd, nb)
        ys = _ffn(meta, xs, expert_w_gate, expert_w_up, expert_w_down, l, nb)
        final = l == depth - 1
        xf = _gather(tab, ctab, lcol, x2, norm_final[None, :], ys, tmd, final)

    return xf.reshape(batch, seq, d)
```

```python
import functools
import math

import jax
import jax.numpy as jnp
import numpy as np
from jax import lax
from jax.experimental import pallas as pl
from jax.experimental.pallas import tpu as pltpu

F32 = jnp.float32
BF16 = jnp.bfloat16
I32 = jnp.int32

LANES = 128
VMEM_LIMIT_BYTES = 56 * 1024 * 1024

D_MODEL = 1024
RMS_EPS = 1e-6
LOG2E = math.log2(math.e)
GROUP_WIDTH = 256
HEAD_DIM = 64
N_HEADS = 4

CONV_A_WIDTH = 3
SSM_CONV = 4
SSM_STATE = 64
SSM_CHUNK = 256

MLA_NOPE = 64
MLA_ROPE = 32
MLA_Q_LORA = 256
MLA_KV_LORA = 128
ROPE_BASE = 10000.0
MLA_CHUNK = 64
ATTN_TQ = 512
ROW_TILE = 1024

N_EXPERT_GROUPS = 4
EXPERTS_PER_GROUP = 8
N_EXPERTS = 32
EXPERT_FF = 256
ROUTER_ROWS = 48
MOE_ROWS = 512
ZERO_ROWS = 256

SEG_A = (0, 768)
SEG_B = (768, 1280)
SEG_C = (1280, 2048)
SEG_D = (2048, 2432)
SEG_M = (2432, 2560)
SEG_M2 = (2560, 2688)
IN_COLS_PADDED = 2688
HEAD_PAD = N_HEADS * LANES
AUG_LANE = HEAD_DIM
MISC_F = 0
MISC_DT = 4
MISC_ROPE = 64
COL_CUMF = 0
COL_DT = 4
COL_ACUM = 8
N_SCALAR_ROWS = 16


def _cparams(*sem):
    return pltpu.CompilerParams(dimension_semantics=sem, vmem_limit_bytes=VMEM_LIMIT_BYTES)


def _lane_iota(shape):
    return lax.broadcasted_iota(I32, shape, len(shape) - 1)


def _row_iota(shape):
    return lax.broadcasted_iota(I32, shape, 0)


def _rms(x, g):
    ms = jnp.mean(x * x, axis=-1, keepdims=True)
    return x * lax.rsqrt(ms + RMS_EPS) * g


def _silu(x):
    return x / (1.0 + jnp.exp(-x))


def _softplus(x):
    return jnp.maximum(x, 0.0) + jnp.log(1.0 + jnp.exp(-jnp.abs(x)))


def _shift_rows(x, k):
    rolled = pltpu.roll(x, k, 0)
    return jnp.where(_row_iota(x.shape) >= k, rolled, 0.0)


def _rope_kernel(pos_ref, freq_ref, cos_ref, sin_ref):
    q = pos_ref.shape[2]
    lane = _lane_iota((q, LANES))
    group = lane >> int(math.log2(MLA_ROPE))
    rows = jnp.concatenate([pos_ref[0], jnp.zeros((8 - ROPE_PACK, q), F32)], axis=0)
    cols = jnp.concatenate([rows] * (LANES // 8), axis=0).T
    pos = cols[:, ROPE_PACK - 1:ROPE_PACK]
    for k in range(ROPE_PACK - 2, -1, -1):
        pos = jnp.where(group == k, cols[:, k:k + 1], pos)
    ang = pos * freq_ref[...]
    c = jnp.cos(ang)
    s = jnp.sin(ang)
    rope = (lane >= MISC_ROPE) & (lane < MISC_ROPE + MLA_ROPE)
    for k in range(ROPE_PACK):
        shift = (MISC_ROPE - MLA_ROPE * k) % LANES
        ck = pltpu.roll(c, shift, 1) if shift else c
        sk = pltpu.roll(s, shift, 1) if shift else s
        cos_ref[k * q:(k + 1) * q, :] = jnp.where(rope, ck, jnp.where(lane < MISC_ROPE, 1.0, 0.0))
        sin_ref[k * q:(k + 1) * q, :] = jnp.where(rope, sk, 0.0)


ROPE_PACK = LANES // MLA_ROPE


def _rope_tables(positions, tm):
    t = positions.size
    pos = positions.astype(F32).reshape(t // tm, ROPE_PACK, tm // ROPE_PACK)
    inv = ROPE_BASE ** (-np.arange(0, MLA_ROPE, 2, dtype=np.float32) / MLA_ROPE)
    freq = np.tile(np.concatenate([inv, inv]), ROPE_PACK)[None, :].astype(np.float32)
    return pl.pallas_call(
        _rope_kernel,
        grid=(t // tm,),
        in_specs=[pl.BlockSpec((1, ROPE_PACK, tm // ROPE_PACK), lambda i: (i, 0, 0)),
                  pl.BlockSpec((1, LANES), lambda i: (0, 0))],
        out_specs=[pl.BlockSpec((tm, LANES), lambda i: (i, 0))] * 2,
        out_shape=[jax.ShapeDtypeStruct((t, LANES), F32)] * 2,
        compiler_params=_cparams("parallel"),
        name="rope_tables",
    )(pos, jnp.asarray(freq))


def _arrange_w_in_kernel(wt_ref, w_ref, wvt_ref):
    gw = GROUP_WIDTH
    rows = lambda lo, hi: wt_ref[0, lo:hi, :]

    def put(seg0, piece_t):
        w_ref[:, seg0:seg0 + piece_t.shape[0]] = piece_t.T.astype(BF16)

    for j in range(3):
        put(SEG_A[0] + j * gw, rows(j * gw, (j + 1) * gw))
    put(SEG_B[0], rows(3 * gw, 4 * gw) * (HEAD_DIM ** -0.5 * LOG2E))
    put(SEG_B[0] + gw, rows(4 * gw, 5 * gw))
    wvt_ref[...] = rows(5 * gw, 6 * gw).astype(BF16)
    c0 = 6 * gw
    win = pltpu.roll(rows(c0, c0 + 3 * gw + 8), 3 * gw + 8 - N_HEADS, 0)
    for j in range(3):
        put(SEG_C[0] + j * gw, win[j * gw:(j + 1) * gw])
    d0 = c0 + 3 * gw + 8
    put(SEG_D[0], rows(d0, d0 + MLA_Q_LORA))
    put(SEG_D[0] + MLA_Q_LORA, rows(d0 + MLA_Q_LORA, d0 + MLA_Q_LORA + MLA_KV_LORA))
    kr0 = d0 + MLA_Q_LORA + MLA_KV_LORA
    half = MLA_ROPE // 2
    first8 = jnp.where(_row_iota((8, D_MODEL)) < N_HEADS, rows(c0, c0 + 8), rows(d0 - 8, d0))
    zeros = lambda n: jnp.zeros((n, D_MODEL), F32)
    misc_t = jnp.concatenate([first8, zeros(MISC_ROPE - 8), rows(kr0, kr0 + MLA_ROPE),
                              zeros(LANES - MISC_ROPE - MLA_ROPE)], axis=0)
    misc2_t = jnp.concatenate([zeros(MISC_ROPE), -rows(kr0 + half, kr0 + MLA_ROPE), rows(kr0, kr0 + half),
                               zeros(LANES - MISC_ROPE - MLA_ROPE)], axis=0)
    put(SEG_M[0], misc_t)
    put(SEG_M2[0], misc2_t)


def _inproj_kernel(x_ref, g_ref, wt_ref, cw_ref, cos_ref, sin_ref, nq_ref, nkv_ref, wq2_ref, wk_ref,
                   wvt2_ref, oa, ob, oc, om, ovt, q_ref, k_ref, vt_ref, w_ref, wvt_ref, halo_ref,
                   *, tm, tiles_per_seq):
    @pl.when(pl.program_id(0) == 0)
    def _():
        _arrange_w_in_kernel(wt_ref, w_ref, wvt_ref)

    h = _rms(x_ref[...], g_ref[...]).astype(BF16)
    for o, (lo, hi) in ((ob, SEG_B), (oc, SEG_C)):
        o[...] = jnp.dot(h, w_ref[:, lo:hi], preferred_element_type=F32).astype(o.dtype)

    gw = GROUP_WIDTH
    pa = jnp.dot(h, w_ref[:, SEG_A[0]:SEG_A[1]], preferred_element_type=F32)
    cv = pa[:, gw:2 * gw] * pa[:, 2 * gw:3 * gw]

    @pl.when(pl.program_id(0) % tiles_per_seq == 0)
    def _():
        halo_ref[...] = jnp.zeros_like(halo_ref)

    halo = halo_ref[...]
    row8 = _row_iota(halo.shape)
    acc = cv * cw_ref[CONV_A_WIDTH - 1:CONV_A_WIDTH, :]
    for k in range(1, CONV_A_WIDTH):
        shifted = pltpu.roll(cv, k, 0)
        top = jnp.where(row8 < k, pltpu.roll(halo, k, 0), shifted[0:8])
        shifted = jnp.concatenate([top, shifted[8:]], axis=0)
        acc = acc + shifted * cw_ref[CONV_A_WIDTH - 1 - k:CONV_A_WIDTH - k, :]
    halo_ref[...] = cv[tm - 8:tm]
    oa[...] = (pa[:, 0:gw] * acc).astype(oa.dtype)
    ovt[...] = lax.dot_general(wvt_ref[...], h, (((1,), (1,)), ((), ())),
                               preferred_element_type=F32).astype(ovt.dtype)
    misc = jnp.dot(h, w_ref[:, SEG_M[0]:SEG_M[1]], preferred_element_type=F32)
    misc2 = jnp.dot(h, w_ref[:, SEG_M2[0]:SEG_M2[1]], preferred_element_type=F32)
    om[...] = misc

    pd = jnp.dot(h, w_ref[:, SEG_D[0]:SEG_D[1]], preferred_element_type=F32)
    cq = _rms(pd[:, 0:MLA_Q_LORA], nq_ref[...]).astype(BF16)
    ckv = _rms(pd[:, MLA_Q_LORA:MLA_Q_LORA + MLA_KV_LORA], nkv_ref[...]).astype(BF16)
    cos = cos_ref[...]
    sin = sin_ref[...]
    cos4 = jnp.concatenate([cos] * N_HEADS, axis=1)
    sin4 = jnp.concatenate([sin] * N_HEADS, axis=1)
    scale = (MLA_NOPE + MLA_ROPE) ** -0.5 * LOG2E
    q2 = jnp.dot(cq, wq2_ref[...], preferred_element_type=F32)
    q_ref[...] = ((q2[:, 0:HEAD_PAD] * cos4 + q2[:, HEAD_PAD:2 * HEAD_PAD] * sin4) * scale).astype(q_ref.dtype)
    lane = _lane_iota(cos.shape)
    rope = (lane >= MISC_ROPE) & (lane < MISC_ROPE + MLA_ROPE)
    kr = jnp.where(rope, misc * cos + misc2 * sin, 0.0)
    k = jnp.dot(ckv, wk_ref[...], preferred_element_type=F32)
    k_ref[...] = (k + jnp.concatenate([kr] * N_HEADS, axis=1)).astype(k_ref.dtype)
    vt_ref[...] = lax.dot_general(wvt2_ref[...], ckv, (((1,), (1,)), ((), ())),
                                  preferred_element_type=F32).astype(vt_ref.dtype)


def _inproj(x, g, w_in_t, layer, conv_w, cos, sin, nq, nkv, wq2, wk, wvt2, tm, seq):
    t = x.shape[0]
    once = lambda a: pl.BlockSpec((1,) + a.shape[1:], lambda i: (layer, 0, 0), pipeline_mode=pl.Buffered(1))
    full = lambda a: pl.BlockSpec(a.shape, lambda i: (0, 0))
    tile = lambda wd: pl.BlockSpec((tm, wd), lambda i: (i, 0))
    cols = lambda: pl.BlockSpec((GROUP_WIDTH, tm), lambda i: (0, i))
    seg = lambda s: s[1] - s[0]
    return pl.pallas_call(
        functools.partial(_inproj_kernel, tm=tm, tiles_per_seq=seq // tm),
        grid=(t // tm,),
        in_specs=[tile(D_MODEL), full(g), once(w_in_t), full(conv_w), tile(LANES), tile(LANES),
                  full(nq), full(nkv), full(wq2), full(wk), full(wvt2)],
        out_specs=[tile(GROUP_WIDTH), tile(seg(SEG_B)), tile(seg(SEG_C)), tile(LANES), cols(),
                   tile(HEAD_PAD), tile(HEAD_PAD), cols()],
        out_shape=[jax.ShapeDtypeStruct((t, GROUP_WIDTH), BF16), jax.ShapeDtypeStruct((t, seg(SEG_B)), BF16),
                   jax.ShapeDtypeStruct((t, seg(SEG_C)), BF16), jax.ShapeDtypeStruct((t, LANES), F32),
                   jax.ShapeDtypeStruct((GROUP_WIDTH, t), BF16),
                   jax.ShapeDtypeStruct((t, HEAD_PAD), BF16), jax.ShapeDtypeStruct((t, HEAD_PAD), BF16),
                   jax.ShapeDtypeStruct((GROUP_WIDTH, t), BF16)],
        scratch_shapes=[pltpu.VMEM((D_MODEL, IN_COLS_PADDED), BF16), pltpu.VMEM((GROUP_WIDTH, D_MODEL), BF16),
                        pltpu.VMEM((8, GROUP_WIDTH), F32)],
        compiler_params=_cparams("arbitrary"),
        name="inproj",
    )(x, g, w_in_t, conv_w, cos, sin, nq, nkv, wq2, wk, wvt2)


def _scalar_prep_kernel(m_ref, p_ref, qk_ref, place_ref, const_ref,
                        col_ref, row_ref, qa_ref, ka_ref, tref_ref, *, tq):
    s = m_ref.shape[0]
    tref_ref[...] = jnp.zeros_like(tref_ref)
    tile_ref = jnp.zeros((1, LANES), F32)
    m = m_ref[...]
    bias = p_ref[0:1, :]
    a_log = p_ref[1:2, :]
    lane = _lane_iota(m.shape)
    z = m + bias
    logf = jnp.minimum(z, 0.0) - jnp.log(1.0 + jnp.exp(-jnp.abs(z)))
    dt = _softplus(z)
    a = dt * (-jnp.exp(a_log))
    is_f = lane < MISC_DT
    is_dt = (lane >= MISC_DT) & (lane < MISC_DT + N_HEADS)
    v = jnp.where(is_f, logf, jnp.where(is_dt, a, 0.0))
    r = _row_iota((SSM_CHUNK, SSM_CHUNK))
    c = _lane_iota((SSM_CHUNK, SSM_CHUNK))
    tril = jnp.where(r >= c, 1.0, 0.0).astype(BF16)
    carry = jnp.zeros((1, LANES), F32)
    lane_1 = _lane_iota((1, LANES))
    lane_b = _lane_iota((SSM_CHUNK, LANES))
    low = lane_b < HEAD_DIM
    aug_lanes = (lane_b >= AUG_LANE) & (lane_b < AUG_LANE + AUG_TERMS)
    for ci in range(s // SSM_CHUNK):
        rest = v[ci * SSM_CHUNK:(ci + 1) * SSM_CHUNK]
        cs = jnp.zeros((SSM_CHUNK, LANES), F32)
        for _ in range(3):
            term = rest.astype(BF16)
            cs = cs + jnp.dot(tril, term, preferred_element_type=F32)
            rest = rest - term.astype(F32)
        cs = cs + jnp.where(lane_1 < MISC_DT, carry, 0.0)
        carry = cs[SSM_CHUNK - 1:SSM_CHUNK]
        acum = pltpu.roll(cs, COL_ACUM - MISC_DT, 1)
        out = jnp.where(lane_b < MISC_DT, cs * LOG2E,
                        jnp.where(lane_b < COL_ACUM, dt[ci * SSM_CHUNK:(ci + 1) * SSM_CHUNK],
                                  jnp.where(lane_b < COL_ACUM + N_HEADS, acum, 0.0)))
        rows = slice(ci * SSM_CHUNK, (ci + 1) * SSM_CHUNK)
        col_ref[rows, :] = out
        row_ref[0, :, rows] = out.T[:N_SCALAR_ROWS]
        if (ci * SSM_CHUNK) % tq == 0:
            tile_ref = out[0:1, :]
            ti = (ci * SSM_CHUNK) // tq
            tref_ref[0, ti:ti + 1, :] = tile_ref
        c = out - tile_ref
        c_hi = c.astype(BF16)
        r1 = c - c_hi.astype(F32)
        c_mid = r1.astype(BF16)
        c_lo = (r1 - c_mid.astype(F32)).astype(BF16)
        compact = jnp.dot(jnp.concatenate([c_hi, c_mid, c_lo], axis=1), place_ref[...],
                          preferred_element_type=F32) + const_ref[0:1, :]
        for side, o_ref in enumerate((qa_ref, ka_ref)):
            dec = compact[:, side * LANES:(side + 1) * LANES]
            for h in range(N_HEADS):
                pair = qk_ref[rows, side * GROUP_WIDTH + (h // 2) * LANES:
                              side * GROUP_WIDTH + (h // 2 + 1) * LANES].astype(F32)
                feat = pair if h % 2 == 0 else pltpu.roll(pair, HEAD_DIM, 1)
                dec_h = pltpu.roll(dec, AUG_LANE - AUG_TERMS * h, 1)
                group = jnp.where(low, feat, jnp.where(aug_lanes, dec_h, 0.0))
                o_ref[rows, h * LANES:(h + 1) * LANES] = group.astype(o_ref.dtype)


AUG_TERMS = 6


def _fox_placement():
    place = np.zeros((3 * LANES, 2 * LANES), np.float32)
    const = np.zeros((8, 2 * LANES), np.float32)
    for h in range(N_HEADS):
        a0 = AUG_TERMS * h
        for term in range(3):
            place[term * LANES + COL_CUMF + h, a0 + term] = 1.0
            place[term * LANES + COL_CUMF + h, LANES + a0 + 3 + term] = -1.0
            const[0, a0 + 3 + term] = 1.0
            const[0, LANES + a0 + term] = 1.0
    return jnp.asarray(place, BF16), jnp.asarray(const, F32)


def _scalar_prep(misc, params, qk, batch, seq, tq):
    place, const = _fox_placement()
    full = lambda a: pl.BlockSpec(a.shape, lambda b: (0,) * a.ndim)
    return pl.pallas_call(
        functools.partial(_scalar_prep_kernel, tq=tq),
        grid=(batch,),
        in_specs=[pl.BlockSpec((seq, LANES), lambda b: (b, 0)),
                  pl.BlockSpec((8, LANES), lambda b: (0, 0)),
                  pl.BlockSpec((seq, 2 * GROUP_WIDTH), lambda b: (b, 0)),
                  full(place), full(const)],
        out_specs=[pl.BlockSpec((seq, LANES), lambda b: (b, 0)),
                   pl.BlockSpec((1, N_SCALAR_ROWS, seq), lambda b: (b, 0, 0)),
                   pl.BlockSpec((seq, HEAD_PAD), lambda b: (b, 0)),
                   pl.BlockSpec((seq, HEAD_PAD), lambda b: (b, 0)),
                   pl.BlockSpec((1, 8, LANES), lambda b: (b, 0, 0))],
        out_shape=[jax.ShapeDtypeStruct((batch * seq, LANES), F32),
                   jax.ShapeDtypeStruct((batch, N_SCALAR_ROWS, seq), F32),
                   jax.ShapeDtypeStruct((batch * seq, HEAD_PAD), BF16),
                   jax.ShapeDtypeStruct((batch * seq, HEAD_PAD), BF16),
                   jax.ShapeDtypeStruct((batch, 8, LANES), F32)],
        compiler_params=_cparams("parallel"),
        name="scalar_prep",
    )(misc, params, qk, place, const)


def _pair_lanes(col, base, shape):
    lane = _lane_iota(shape)
    return jnp.where(lane < HEAD_DIM, col[:, base:base + 1], col[:, base + 1:base + 2])


def _ssd_kernel(p_ref, col_ref, row_ref, cw_ref, par_ref, o_ref, u_ref):
    s = p_ref.shape[0]
    q = SSM_CHUNK
    gw = GROUP_WIDTH
    xbc = p_ref[:, gw:3 * gw].astype(F32)

    def conv(x, shift):
        acc = x * cw_ref[SSM_CONV - 1:SSM_CONV, :]
        for k in range(1, SSM_CONV):
            acc = acc + shift(x, k) * cw_ref[SSM_CONV - 1 - k:SSM_CONV - k, :]
        return _silu(acc + cw_ref[SSM_CONV:SSM_CONV + 1, :])

    u_ref[...] = conv(xbc, lambda x, k: pltpu.roll(x, k, 0))
    u_ref[0:8, :] = conv(xbc[0:8], _shift_rows)

    d_skip = par_ref[0:1, :]
    norm_g = par_ref[1:2, :]
    lane_q = _lane_iota((q, LANES))
    low = lane_q < HEAD_DIM
    tri = _row_iota((q, q)) >= _lane_iota((q, q))

    def chunk(ci, states):
        rows = pl.ds(ci * q, q)
        u = u_ref[rows, :]
        col = col_ref[rows, :]
        bm = u[:, gw:gw + LANES]
        cm = u[:, gw + LANES:gw + 2 * LANES]
        z = p_ref[rows, 0:gw].astype(F32)
        new_states = []
        ys = []
        for g in range(2):
            sel = low if g == 0 else jnp.logical_not(low)
            cg = jnp.where(sel, cm, 0.0).astype(BF16)
            bg = jnp.where(sel, bm, 0.0)
            gmat = lax.dot_general(cg, bm.astype(BF16), (((1,), (1,)), ((), ())),
                                   preferred_element_type=F32)
            xs = u[:, g * LANES:(g + 1) * LANES]
            dt2 = _pair_lanes(col, COL_DT + 2 * g, (q, LANES))
            ac2 = _pair_lanes(col, COL_ACUM + 2 * g, (q, LANES))
            xdt = xs * dt2
            xdt_b = xdt.astype(BF16)
            st = states[g]
            y_off = jnp.dot(cg, st.astype(BF16), preferred_element_type=F32) * jnp.exp(ac2)
            halves = []
            for hh in range(2):
                h = 2 * g + hh
                ac_col = col[:, COL_ACUM + h:COL_ACUM + h + 1]
                ac_row = row_ref[0, COL_ACUM + h:COL_ACUM + h + 1, rows]
                decay = jnp.exp(jnp.where(tri, ac_col - ac_row, -1e30))
                mm = (gmat * decay).astype(BF16)
                halves.append(jnp.dot(mm, xdt_b, preferred_element_type=F32))
            y = jnp.where(low, halves[0], halves[1]) + y_off + d_skip[:, g * LANES:(g + 1) * LANES] * xs
            ys.append(y)
            ac_last = ac2[q - 1:q, :]
            w_end = jnp.exp(ac_last - ac2)
            xw = (xdt * w_end).astype(BF16)
            upd = jnp.dot(bg.T.astype(BF16), xw, preferred_element_type=F32)
            new_states.append(st * jnp.exp(ac_last) + upd)
        yfull = jnp.concatenate(ys, axis=1) * _silu(z)
        o_ref[rows, :] = _rms(yfull, norm_g).astype(o_ref.dtype)
        return tuple(new_states)

    init = (jnp.zeros((LANES, LANES), F32), jnp.zeros((LANES, LANES), F32))
    states = init
    for ci in range(s // q):
        states = chunk(ci, states)


def _ssd_mixer(pc, col, rows, conv_wb, par, batch, seq):
    gw = GROUP_WIDTH
    return pl.pallas_call(
        _ssd_kernel,
        grid=(batch,),
        in_specs=[pl.BlockSpec((seq, 3 * gw), lambda b: (b, 0)),
                  pl.BlockSpec((seq, LANES), lambda b: (b, 0)),
                  pl.BlockSpec((1, N_SCALAR_ROWS, seq), lambda b: (b, 0, 0)),
                  pl.BlockSpec((8, 2 * gw), lambda b: (0, 0)),
                  pl.BlockSpec((8, gw), lambda b: (0, 0))],
        out_specs=pl.BlockSpec((seq, gw), lambda b: (b, 0)),
        out_shape=jax.ShapeDtypeStruct((batch * seq, gw), BF16),
        scratch_shapes=[pltpu.VMEM((seq, 2 * gw), F32)],
        compiler_params=_cparams("parallel"),
        name="ssd_mixer",
    )(pc, col, rows, conv_wb, par)


def _attn_kernel(*refs, fox, tq, nq):
    if fox:
        tref_ref, q_ref, k_ref, vt_ref, o_ref = refs
    else:
        q_ref, k_ref, vt_ref, o_ref = refs
        tref_ref = None
    b = pl.program_id(0)
    i = pl.program_id(1)
    key = _row_iota((tq, tq))
    qry = _lane_iota((tq, tq))
    if fox:
        allowed = key <= qry
    else:
        shift = int(math.log2(MLA_CHUNK))
        allowed = (key >> shift) <= (qry >> shift)
    qs = [q_ref[:, h * LANES:(h + 1) * LANES] for h in range(N_HEADS)]
    ones_rows = jnp.ones((16, tq), BF16)

    def step(j, masked, carry):
        rk = pl.ds(j * tq, tq)
        scores = [lax.dot_general(k_ref[rk, h * LANES:(h + 1) * LANES], qs[h], (((1,), (1,)), ((), ())),
                                  preferred_element_type=F32) for h in range(N_HEADS)]
        probs = []
        for h in range(N_HEADS):
            m, l, _ = carry[h]
            s = scores[h]
            if masked:
                s = jnp.where(allowed, s, -1e30)
            delta = (tref_ref[b, i, h] - tref_ref[b, j, h]) if fox else 0.0
            m_new = jnp.maximum(m, jnp.max(s, axis=0, keepdims=True) + delta)
            alpha = jnp.exp2(m - m_new)
            p = jnp.exp2(s - (m_new - delta))
            probs.append((m_new, alpha, p.astype(BF16)))
        new = []
        for h in range(N_HEADS):
            pair = h // 2
            m_new, alpha, p = probs[h]
            lhs = jnp.concatenate([vt_ref[pair * LANES:(pair + 1) * LANES, rk], ones_rows], axis=0)
            pv = jnp.dot(lhs, p, preferred_element_type=F32)
            new.append((m_new, alpha * carry[h][1] + pv[LANES:LANES + 1], alpha * carry[h][2] + pv[0:LANES]))
        return tuple(new)

    init = tuple((jnp.full((1, tq), -1e30, F32), jnp.zeros((1, tq), F32), jnp.zeros((LANES, tq), F32))
                 for _ in range(N_HEADS))
    top = _row_iota((LANES, tq)) < HEAD_DIM

    def sweep(n_before):
        carry = init
        for j in range(n_before):
            carry = step(j, False, carry)
        carry = step(n_before, True, carry)
        outs = [acc / l for (_, l, acc) in carry]
        o_t = jnp.concatenate([jnp.where(top, outs[0], outs[1]), jnp.where(top, outs[2], outs[3])], axis=0)
        o_ref[...] = o_t.T.astype(o_ref.dtype)

    for n_before in range(nq):
        pl.when(i == n_before)(functools.partial(sweep, n_before))


def _attention(q, k, vt, tref, batch, seq, tq, name):
    nq = seq // tq
    fox = tref is not None
    kern = functools.partial(_attn_kernel, fox=fox, tq=tq, nq=nq)
    grid_spec = pltpu.PrefetchScalarGridSpec(
        num_scalar_prefetch=1 if fox else 0,
        grid=(batch, nq),
        in_specs=[pl.BlockSpec((tq, HEAD_PAD), lambda b, i, *_: (b * nq + i, 0)),
                  pl.BlockSpec((seq, HEAD_PAD), lambda b, i, *_: (b, 0)),
                  pl.BlockSpec((GROUP_WIDTH, seq), lambda b, i, *_: (0, b))],
        out_specs=pl.BlockSpec((tq, GROUP_WIDTH), lambda b, i, *_: (b * nq + i, 0)),
    )
    args = ((tref,) if fox else ()) + (q, k, vt)
    return pl.pallas_call(
        kern,
        grid_spec=grid_spec,
        out_shape=jax.ShapeDtypeStruct((batch * seq, GROUP_WIDTH), BF16),
        compiler_params=_cparams("parallel", "arbitrary"),
        name=name,
    )(*args)


def _outproj_kernel(x_ref, ya, yb, yc, yd, w_ref, g_ref, wr_ref, br_ref,
                    x2_ref, h2_ref, rrow_ref, cnt_ref, *, tm, moe_tile):
    y = jnp.concatenate([ya[...], yb[...], yc[...], yd[...]], axis=1)
    x2 = x_ref[...] + jnp.dot(y, w_ref[...], preferred_element_type=F32)
    x2_ref[...] = x2
    h2 = _rms(x2, g_ref[...])
    h2_ref[...] = h2.astype(h2_ref.dtype)
    h_hi = h2.astype(BF16)
    h_lo = (h2 - h_hi.astype(F32)).astype(BF16)
    part = jnp.dot(h_hi, wr_ref[...], preferred_element_type=F32)
    logits = (part[:, 0:LANES] + part[:, LANES:2 * LANES]
              + jnp.dot(h_lo, wr_ref[:, 0:LANES], preferred_element_type=F32) + br_ref[...])
    lt = logits.T[0:ROUTER_ROWS]
    row = _row_iota(lt.shape)
    neg = -1e30
    big = 1 << 20
    gmask = (row >= N_EXPERTS) & (row < N_EXPERTS + N_EXPERT_GROUPS)
    gl = jnp.where(gmask, lt, neg)
    gmax = jnp.max(gl, axis=0, keepdims=True)
    gsum = jnp.sum(jnp.where(gmask, jnp.exp(gl - gmax), 0.0), axis=0, keepdims=True)
    g_w = 1.0 / gsum
    g_idx = jnp.min(jnp.where(gmask & (gl == gmax), row, big), axis=0, keepdims=True) - N_EXPERTS
    emask = (row < N_EXPERTS) & ((row >> int(math.log2(EXPERTS_PER_GROUP))) == g_idx)
    el = jnp.where(emask, lt, neg)
    e1v = jnp.max(el, axis=0, keepdims=True)
    esum = jnp.sum(jnp.where(emask, jnp.exp(el - e1v), 0.0), axis=0, keepdims=True)
    i1 = jnp.min(jnp.where(emask & (el == e1v), row, big), axis=0, keepdims=True)
    el2 = jnp.where(row == i1, neg, el)
    e2v = jnp.max(el2, axis=0, keepdims=True)
    i2 = jnp.min(jnp.where(emask & (row != i1) & (el2 == e2v), row, big), axis=0, keepdims=True)
    p1 = 1.0 / esum
    p2 = jnp.exp(e2v - e1v) / esum
    w1 = g_w * (p1 / (p1 + p2))
    w2 = g_w * (p2 / (p1 + p2))
    out_row = _row_iota(rrow_ref.shape)
    rrow_ref[...] = jnp.where(out_row == 0, i1.astype(F32),
                              jnp.where(out_row == 1, i2.astype(F32),
                                        jnp.where(out_row == 2, w1, jnp.where(out_row == 3, w2, 0.0))))
    step = pl.program_id(0)

    @pl.when(step == 0)
    def _():
        cnt_ref[...] = jnp.zeros_like(cnt_ref)

    chosen = jnp.where((row == i1) | (row == i2), 1.0, 0.0).astype(BF16)
    tiles = tm // moe_tile
    tile_of = (_row_iota((tm, LANES)) >> int(math.log2(moe_tile))) + step * tiles
    to_tile = jnp.where(_lane_iota((tm, LANES)) == tile_of, 1.0, 0.0).astype(BF16)
    counts = jnp.dot(chosen, to_tile, preferred_element_type=F32)
    cnt_ref[...] += counts[0:N_EXPERTS]


def _outproj(x, ya, yb, yc, yd, w, g, wr, br, tm, moe_tile):
    t = x.shape[0]
    full = lambda a: pl.BlockSpec(a.shape, lambda i: (0, 0))
    tile = lambda wd: pl.BlockSpec((tm, wd), lambda i: (i, 0))
    return pl.pallas_call(
        functools.partial(_outproj_kernel, tm=tm, moe_tile=moe_tile),
        grid=(t // tm,),
        in_specs=[tile(D_MODEL)] + [tile(GROUP_WIDTH)] * 4 + [full(w), full(g), full(wr), full(br)],
        out_specs=[tile(D_MODEL), tile(D_MODEL), pl.BlockSpec((8, tm), lambda i: (0, i)),
                   pl.BlockSpec((N_EXPERTS, LANES), lambda i: (0, 0))],
        out_shape=[jax.ShapeDtypeStruct((t, D_MODEL), F32), jax.ShapeDtypeStruct((t, D_MODEL), BF16),
                   jax.ShapeDtypeStruct((8, t), F32), jax.ShapeDtypeStruct((N_EXPERTS, LANES), F32)],
        compiler_params=_cparams("arbitrary"),
        name="outproj_router",
    )(x, ya, yb, yc, yd, w, g, wr, br)


MOE_TILE = 256
CHUNK = 8
LOCAL_ROWS = 2 * MOE_TILE + 256
PACKED = D_MODEL // 2
XS_WIDTH = PACKED
U32 = jnp.uint32


def _pack_bf16_pairs(x, exact=False):
    if not exact:
        x = x.astype(BF16).astype(F32)
    half = x.shape[1] // 2
    lo = lax.bitcast_convert_type(x[:, :half], U32)
    hi = lax.bitcast_convert_type(x[:, half:], U32)
    return hi | (lo >> 16)


def _unpack_bf16_pairs(words):
    lo = lax.bitcast_convert_type(words << 16, F32)
    hi = lax.bitcast_convert_type(words & U32(0xFFFF0000), F32)
    return jnp.concatenate([lo, hi], axis=1).astype(BF16)


COPY_ROWS = (2 * CHUNK, CHUNK)


def _route_kernel(r_ref, cnt_ref, lrow_ref, lcol_ref, tab_ref, ctab_ref, meta_ref,
                  loff_ref, goff_ref, n8_ref, *, tm, nbp, tiles):
    i = pl.program_id(0)
    hi = lax.Precision.HIGHEST

    @pl.when(i == 0)
    def _():
        cnt = cnt_ref[...]
        n8 = jnp.floor((cnt + (CHUNK - 1)) * (1.0 / CHUNK)) * CHUNK
        er = _row_iota((N_EXPERTS, N_EXPERTS))
        ec = _lane_iota((N_EXPERTS, N_EXPERTS))
        below = jnp.where(er > ec, 1.0, 0.0)
        loff = jnp.dot(below, n8, preferred_element_type=F32, precision=hi)
        rows_e = jnp.sum(n8, axis=-1, keepdims=True) + jnp.zeros_like(n8)
        padded = jnp.floor((rows_e + (MOE_ROWS - 1)) * (1.0 / MOE_ROWS)) * MOE_ROWS
        e_start = jnp.dot(below, padded, preferred_element_type=F32, precision=hi)
        tr = _row_iota((LANES, LANES))
        tc = _lane_iota((LANES, LANES))
        earlier = jnp.where(tr < tc, 1.0, 0.0)
        goff = e_start + jnp.dot(n8, earlier, preferred_element_type=F32, precision=hi)
        loff_ref[...] = loff
        goff_ref[...] = goff
        n8_ref[...] = n8
        big = jnp.floor(n8 * (0.5 / CHUNK))
        small = n8 * (1.0 / CHUNK) - 2.0 * big
        row_t = _row_iota(tab_ref.shape)
        tab_ref[...] = jnp.where(row_t == 0, jnp.sum(big, axis=0, keepdims=True),
                                 jnp.where(row_t == 1, jnp.sum(small, axis=0, keepdims=True), 0.0)).astype(I32)
        reps = nbp // LANES
        pend_b = jnp.concatenate([e_start + padded] * reps, axis=1)
        vend_b = jnp.concatenate([e_start + rows_e] * reps, axis=1)
        used_b = jnp.concatenate([padded] * reps, axis=1) > 0.0
        b0 = (_lane_iota((N_EXPERTS, nbp)) * MOE_ROWS).astype(F32)
        bexp = jnp.sum(jnp.where(pend_b <= b0, 1.0, 0.0), axis=0, keepdims=True)
        bexp = jnp.minimum(bexp, N_EXPERTS - 1.0)
        e_b = _row_iota((N_EXPERTS, nbp)).astype(F32)
        is_e = e_b == bexp
        vend = jnp.sum(jnp.where(is_e, vend_b, 0.0), axis=0, keepdims=True)
        nvalid = jnp.clip(vend - b0[0:1], 0.0, float(MOE_ROWS))
        total = jnp.max(pend_b, axis=0, keepdims=True) * (1.0 / MOE_ROWS)
        order = jnp.sum(jnp.where(used_b & (e_b < bexp), 1.0, 0.0), axis=0, keepdims=True)
        nxt = jnp.min(jnp.where(used_b & (e_b > bexp), e_b, float(N_EXPERTS)), axis=0, keepdims=True)
        row = _row_iota((8, nbp))
        meta = jnp.where(row == 0, bexp, jnp.where(row == 1, nvalid, jnp.where(row == 2, total,
                         jnp.where(row == 3, order, jnp.where(row == 4, nxt, 0.0)))))
        meta_ref[...] = meta.astype(I32)

    su = jnp.where(_row_iota((tm, tm)) < _lane_iota((tm, tm)), 1.0, 0.0).astype(BF16)
    incl = jnp.where(_row_iota((N_EXPERTS, N_EXPERTS)) >= _lane_iota((N_EXPERTS, N_EXPERTS)), 1.0, 0.0)
    cidx = _lane_iota((N_EXPERTS, LANES)).astype(F32)
    e_iota = _row_iota((N_EXPERTS, tm))
    out_row = _row_iota((LANES, tm))
    for k in range(tiles):
        tok = slice(k * tm, (k + 1) * tm)
        oh0 = e_iota == r_ref[0:1, tok].astype(I32)
        oh1 = e_iota == r_ref[1:2, tok].astype(I32)
        oh = jnp.where(oh0 | oh1, 1.0, 0.0)
        tile_lane = _lane_iota((N_EXPERTS, LANES)) == i * tiles + k
        before = jnp.dot(oh.astype(BF16), su, preferred_element_type=F32)
        base = jnp.sum(jnp.where(tile_lane, loff_ref[...], 0.0), axis=-1, keepdims=True) + before
        d0 = jnp.sum(jnp.where(oh0, base, 0.0), axis=0, keepdims=True)
        d1 = jnp.sum(jnp.where(oh1, base, 0.0), axis=0, keepdims=True)
        lrow_ref[k, 0:1, :] = d0.astype(I32)
        lrow_ref[k, 1:2, :] = d1.astype(I32)
        pick_tile = lambda ref: jnp.sum(jnp.where(tile_lane, ref[...], 0.0), axis=-1, keepdims=True)
        nch = pick_tile(n8_ref) * (1.0 / CHUNK)
        n_big = jnp.floor(nch * 0.5)
        n_small = nch - 2.0 * n_big
        loff_t = pick_tile(loff_ref)
        goff_t = pick_tile(goff_ref)
        for c, (n, rows, first) in enumerate(((n_big, COPY_ROWS[0], 0.0),
                                              (n_small, COPY_ROWS[1], n_big * COPY_ROWS[0]))):
            cend = jnp.dot(incl, n + jnp.zeros((N_EXPERTS, LANES), F32), preferred_element_type=F32, precision=hi)
            cstart = cend - n
            mine = (cidx >= cstart) & (cidx < cend)
            step_rows = first + (cidx - cstart) * rows
            ctab_ref[k, 2 * c:2 * c + 1, :] = jnp.sum(jnp.where(mine, loff_t + step_rows, 0.0), axis=0,
                                                      keepdims=True).astype(I32)
            ctab_ref[k, 2 * c + 1:2 * c + 2, :] = jnp.sum(jnp.where(mine, goff_t + step_rows, 0.0), axis=0,
                                                          keepdims=True).astype(I32)
        lcol_ref[tok, :] = jnp.where(out_row == 0, d0, jnp.where(out_row == 1, d1,
                                     jnp.where(out_row == 2, r_ref[2:3, tok],
                                               jnp.where(out_row == 3, r_ref[3:4, tok], 0.0)))).T


def _route(rrow, cnt, tm, nbp):
    t = rrow.shape[1]
    nt = t // tm
    tiles = next(r for r in (4, 2, 1) if nt % r == 0)
    kern = functools.partial(_route_kernel, tm=tm, nbp=nbp, tiles=tiles)
    return pl.pallas_call(
        kern,
        grid=(nt // tiles,),
        in_specs=[pl.BlockSpec((8, tiles * tm), lambda i: (0, i)),
                  pl.BlockSpec((N_EXPERTS, LANES), lambda i: (0, 0))],
        out_specs=[pl.BlockSpec((tiles, 2, tm), lambda i: (i, 0, 0)),
                   pl.BlockSpec((tiles * tm, LANES), lambda i: (i, 0)),
                   pl.BlockSpec((8, LANES), lambda i: (0, 0)),
                   pl.BlockSpec((tiles, 2 * len(COPY_ROWS), LANES), lambda i: (i, 0, 0)),
                   pl.BlockSpec((8, nbp), lambda i: (0, 0))],
        out_shape=[jax.ShapeDtypeStruct((nt, 2, tm), I32), jax.ShapeDtypeStruct((t, LANES), F32),
                   jax.ShapeDtypeStruct((8, LANES), I32), jax.ShapeDtypeStruct((nt, 2 * len(COPY_ROWS), LANES), I32),
                   jax.ShapeDtypeStruct((8, nbp), I32)],
        scratch_shapes=[pltpu.VMEM((N_EXPERTS, LANES), F32)] * 3,
        compiler_params=_cparams("arbitrary"),
        name="moe_route",
    )(rrow, cnt)


def _chunk_copies(tabs, i, local_ref, global_ref, sem, to_global, action):
    tab_ref, ctab_ref = tabs
    for k, rows in enumerate(COPY_ROWS):
        count = tab_ref[k, i]

        def copy(lo, go, rows=rows):
            lsl = local_ref.at[pl.ds(pl.multiple_of(lo, CHUNK), rows)]
            gsl = global_ref.at[pl.ds(pl.multiple_of(go, CHUNK), rows)]
            return pltpu.make_async_copy(lsl, gsl, sem) if to_global else pltpu.make_async_copy(gsl, lsl, sem)

        if action == "wait":
            def one(c, c1, copy=copy):
                copy(0, 0).wait()
                return c1
        else:
            def one(c, c1, copy=copy, k=k):
                copy(ctab_ref[i, 2 * k, c], ctab_ref[i, 2 * k + 1, c]).start()
                return c1

        lax.fori_loop(0, count, one, 0)


def _scatter_kernel(tab_ref, ctab_ref, meta_ref, lrow_ref, h_ref, xs_ref, buf_ref, zero_ref, sem, zsem, *, tm, nb):
    i = pl.program_id(0)
    tabs = (tab_ref, ctab_ref)

    @pl.when(i == 0)
    def _():
        zero_ref[...] = jnp.zeros_like(zero_ref)
        n_used = meta_ref[2, 0]

        def each_piece(action):
            def body(b, c):
                for piece in range(MOE_ROWS // ZERO_ROWS):
                    @pl.when((b < n_used) & (meta_ref[1, b] < (piece + 1) * ZERO_ROWS))
                    def _():
                        start = pl.multiple_of(b * MOE_ROWS + piece * ZERO_ROWS, ZERO_ROWS)
                        cp = pltpu.make_async_copy(zero_ref, xs_ref.at[pl.ds(start, ZERO_ROWS)], zsem)
                        getattr(cp, action)()
                return c
            lax.fori_loop(0, nb, body, 0)

        each_piece("start")
        each_piece("wait")

    rows = _row_iota((LOCAL_ROWS, tm))

    def drain(tile, slot):
        _chunk_copies(tabs, tile, buf_ref.at[slot], xs_ref, sem.at[slot], True, "wait")

    for slot in range(2):
        tile = 2 * i + slot
        p0 = rows == lrow_ref[slot, 0:1, :]
        p1 = rows == lrow_ref[slot, 1:2, :]
        perm = jnp.where(p0 | p1, 1.0, 0.0).astype(BF16)
        sorted_rows = jnp.dot(perm, h_ref[slot * tm:(slot + 1) * tm, :],
                              preferred_element_type=F32)

        @pl.when(i > 0)
        def _():
            drain(tile - 2, slot)

        buf = buf_ref.at[slot]
        buf[...] = _pack_bf16_pairs(sorted_rows, exact=True)
        _chunk_copies(tabs, tile, buf, xs_ref, sem.at[slot], True, "start")

    @pl.when(i == pl.num_programs(0) - 1)
    def _():
        drain(2 * i, 0)
        drain(2 * i + 1, 1)


def _scatter(tab, ctab, meta, lrow, h2, tm, nb):
    t = h2.shape[0]
    kern = functools.partial(_scatter_kernel, tm=tm, nb=nb)
    grid_spec = pltpu.PrefetchScalarGridSpec(
        num_scalar_prefetch=3,
        grid=(t // (2 * tm),),
        in_specs=[pl.BlockSpec((2, 2, tm), lambda i, *_: (i, 0, 0)),
                  pl.BlockSpec((2 * tm, D_MODEL), lambda i, *_: (i, 0))],
        out_specs=pl.BlockSpec(memory_space=pl.ANY),
        scratch_shapes=[pltpu.VMEM((2, LOCAL_ROWS, XS_WIDTH), U32), pltpu.VMEM((ZERO_ROWS, XS_WIDTH), U32),
                        pltpu.SemaphoreType.DMA((2,)), pltpu.SemaphoreType.DMA],
    )
    return pl.pallas_call(
        kern,
        grid_spec=grid_spec,
        out_shape=jax.ShapeDtypeStruct((nb * MOE_ROWS, XS_WIDTH), U32),
        compiler_params=_cparams("arbitrary"),
        name="moe_scatter",
    )(tab, ctab, meta, lrow, h2)


def _ffn_kernel(meta_ref, x_ref, wg_ref, wu_ref, wd_ref, o_ref, wgu_b, wd_b, wg_f, wu_f, wd_f, wsem, *, layer):
    b = pl.program_id(0)
    live = b < meta_ref[2, 0]
    expert = meta_ref[0, b]
    prev = meta_ref[0, jnp.maximum(b - 1, 0)]

    def fetch(e, slot):
        return [pltpu.make_async_copy(src.at[layer, e], dst.at[slot], wsem.at[slot])
                for src, dst in ((wg_ref, wg_f), (wu_ref, wu_f), (wd_ref, wd_f))]

    def first_block(slot):
        @pl.when(b == 0)
        def _():
            for cp in fetch(expert, slot):
                cp.start()

        for cp in fetch(expert, slot):
            cp.wait()
        wgu_b[:, 0:EXPERT_FF] = wg_f[slot].astype(BF16)
        wgu_b[:, EXPERT_FF:2 * EXPERT_FF] = wu_f[slot].astype(BF16)
        wd_b[...] = wd_f[slot].astype(BF16)
        nxt = meta_ref[4, b]

        @pl.when(nxt < N_EXPERTS)
        def _():
            for cp in fetch(nxt, 1 - slot):
                cp.start()

    changed = live & ((b == 0) | (expert != prev))
    odd = (meta_ref[3, b] & 1) == 1

    @pl.when(changed & jnp.logical_not(odd))
    def _():
        first_block(0)

    @pl.when(changed & odd)
    def _():
        first_block(1)

    @pl.when(live)
    def _():
        x = _unpack_bf16_pairs(x_ref[...])
        gu = jnp.dot(x, wgu_b[...], preferred_element_type=F32)
        act = (_silu(gu[:, 0:EXPERT_FF]) * gu[:, EXPERT_FF:2 * EXPERT_FF]).astype(BF16)
        y = jnp.dot(act, wd_b[...], preferred_element_type=F32)
        o_ref[...] = _pack_bf16_pairs(y)


def _ffn(meta, xs, wg, wu, wd, layer, nb):
    def blk(b, m):
        return (jnp.maximum(jnp.minimum(b, m[2, 0] - 1), 0), 0)

    grid_spec = pltpu.PrefetchScalarGridSpec(
        num_scalar_prefetch=1,
        grid=(nb,),
        in_specs=[pl.BlockSpec((MOE_ROWS, XS_WIDTH), blk)] + [pl.BlockSpec(memory_space=pl.ANY)] * 3,
        out_specs=pl.BlockSpec((MOE_ROWS, PACKED), blk),
        scratch_shapes=[pltpu.VMEM((D_MODEL, 2 * EXPERT_FF), BF16), pltpu.VMEM((EXPERT_FF, D_MODEL), BF16),
                        pltpu.VMEM((2, D_MODEL, EXPERT_FF), F32), pltpu.VMEM((2, D_MODEL, EXPERT_FF), F32),
                        pltpu.VMEM((2, EXPERT_FF, D_MODEL), F32), pltpu.SemaphoreType.DMA((2,))],
    )
    return pl.pallas_call(
        functools.partial(_ffn_kernel, layer=layer),
        grid_spec=grid_spec,
        out_shape=jax.ShapeDtypeStruct((nb * MOE_ROWS, PACKED), U32),
        compiler_params=_cparams("arbitrary"),
        name="moe_experts",
    )(meta, xs, wg, wu, wd)


def _gather_kernel(tab_ref, ctab_ref, lcol_ref, x_ref, g_ref, ys_ref, o_ref, buf_ref, sem, *, tm, final):
    i = pl.program_id(0)

    last = pl.num_programs(0) - 1

    def fetch(tile, slot, action):
        _chunk_copies((tab_ref, ctab_ref), tile, buf_ref.at[slot], ys_ref, sem.at[slot], False, action)

    @pl.when(i == 0)
    def _():
        buf_ref[...] = jnp.zeros_like(buf_ref)
        fetch(0, 0, "start")

    col = _lane_iota((tm, LOCAL_ROWS)).astype(F32)
    for slot in range(2):
        tile = 2 * i + slot
        tok = slice(slot * tm, (slot + 1) * tm)
        if slot == 0:
            fetch(tile + 1, 1, "start")
        else:
            @pl.when(i < last)
            def _():
                fetch(tile + 1, 0, "start")

        lc = lcol_ref[tok, :]
        pick0 = jnp.where(col == lc[:, 0:1], 1.0, 0.0).astype(BF16)
        pick1 = jnp.where(col == lc[:, 1:2], 1.0, 0.0).astype(BF16)
        fetch(tile, slot, "wait")
        y = _unpack_bf16_pairs(buf_ref[slot])
        both = jnp.dot(jnp.concatenate([pick0, pick1], axis=0), y, preferred_element_type=F32)
        x = x_ref[tok, :] + lc[:, 2:3] * both[0:tm] + lc[:, 3:4] * both[tm:2 * tm]
        o_ref[tok, :] = _rms(x, g_ref[...]) if final else x


def _gather(tab, ctab, lcol, x2, g, ys, tm, final):
    t = x2.shape[0]
    kern = functools.partial(_gather_kernel, tm=tm, final=final)
    grid_spec = pltpu.PrefetchScalarGridSpec(
        num_scalar_prefetch=2,
        grid=(t // (2 * tm),),
        in_specs=[pl.BlockSpec((2 * tm, LANES), lambda i, *_: (i, 0)),
                  pl.BlockSpec((2 * tm, D_MODEL), lambda i, *_: (i, 0)),
                  pl.BlockSpec((1, D_MODEL), lambda i, *_: (0, 0)),
                  pl.BlockSpec(memory_space=pl.ANY)],
        out_specs=pl.BlockSpec((2 * tm, D_MODEL), lambda i, *_: (i, 0)),
        scratch_shapes=[pltpu.VMEM((2, LOCAL_ROWS, PACKED), U32), pltpu.SemaphoreType.DMA((2,))],
    )
    return pl.pallas_call(
        kern,
        grid_spec=grid_spec,
        out_shape=jax.ShapeDtypeStruct((t, D_MODEL), F32),
        compiler_params=_cparams("arbitrary"),
        name="moe_combine",
    )(tab, ctab, lcol, x2, g, ys)


def _pad_rows(a, rows=8):
    return jnp.zeros((rows, a.shape[-1]), F32).at[:a.shape[0]].set(a.astype(F32))


def _arrange_mla(w_uq, w_ukv):
    half = MLA_ROPE // 2
    qd = MLA_NOPE + MLA_ROPE
    wq, wqs, wk, wv = [], [], [], []
    zq = jnp.zeros((MLA_Q_LORA, LANES - qd), w_uq.dtype)
    zk = jnp.zeros((MLA_KV_LORA, LANES - MLA_NOPE), w_ukv.dtype)
    for h in range(N_HEADS):
        q = w_uq[:, h * qd:(h + 1) * qd]
        nope, rope = q[:, :MLA_NOPE], q[:, MLA_NOPE:]
        wq.append(jnp.concatenate([nope, rope, zq], axis=1))
        wqs.append(jnp.concatenate([jnp.zeros_like(nope), -rope[:, half:], rope[:, :half], zq], axis=1))
        kv = w_ukv[:, h * 2 * MLA_NOPE:(h + 1) * 2 * MLA_NOPE]
        wk.append(jnp.concatenate([kv[:, :MLA_NOPE], zk], axis=1))
        wv.append(kv[:, MLA_NOPE:])
    cat = lambda xs: jnp.concatenate(xs, axis=1).astype(BF16)
    return cat(wq), cat(wqs), cat(wk), cat(wv).T


def kernel(x, positions, norm_mix, w_in, conv_a, fox_forget_bias, ssm_conv_w, ssm_conv_b, ssm_dt_bias,
           ssm_a_log, ssm_d, ssm_norm, mla_q_norm, mla_kv_norm, mla_w_uq, mla_w_ukv, w_out, norm_ffn,
           router_group_w, router_group_b, router_expert_w, router_expert_b, expert_w_gate, expert_w_up,
           expert_w_down, norm_final):
    batch, seq, d = x.shape
    t = batch * seq
    depth = w_in.shape[0]
    tm = min(ROW_TILE, seq)
    tq = min(ATTN_TQ, seq)
    tmd = MOE_TILE
    assert d == D_MODEL and seq % tm == 0 and seq % tq == 0 and tq % SSM_CHUNK == 0, (x.shape,)
    assert seq // tq <= 8 and tm % (LANES * ROPE_PACK) == 0, (seq, tm)
    assert t % (2 * tmd) == 0 and t // tmd <= LANES and tm % tmd == 0, (t, tmd)
    max_rows = 2 * t + (CHUNK - 1) * N_EXPERTS * (t // tmd) + N_EXPERTS * (MOE_ROWS - 1)
    nb = -(-max_rows // MOE_ROWS)
    nbp = -(-nb // LANES) * LANES

    xf = x.reshape(t, d)
    cos, sin = _rope_tables(positions, tm)
    w_in_t = jnp.swapaxes(w_in, 1, 2)

    for l in range(depth):
        wq, wqs, wk, wv = _arrange_mla(mla_w_uq[l], mla_w_ukv[l])
        ya, pb, pc, misc, fox_vt, q, k, v = _inproj(
            xf, norm_mix[l][None, :], w_in_t, l, _pad_rows(conv_a[l]), cos, sin, mla_q_norm[l][None, :],
            mla_kv_norm[l][None, :], jnp.concatenate([wq, wqs], axis=1), wk, wv, tm, seq)

        sp = jnp.zeros((8, LANES), F32)
        sp = sp.at[0, MISC_F:MISC_F + N_HEADS].set(fox_forget_bias[l])
        sp = sp.at[0, MISC_DT:MISC_DT + N_HEADS].set(ssm_dt_bias[l])
        sp = sp.at[1, MISC_DT:MISC_DT + N_HEADS].set(ssm_a_log[l])
        col, rows, fox_q, fox_k, tref = _scalar_prep(misc, sp, pb, batch, seq, tq)

        yb = _attention(fox_q, fox_k, fox_vt, tref, batch, seq, tq, "fox_attention")
        conv_wb = _pad_rows(jnp.concatenate([ssm_conv_w[l], ssm_conv_b[l][None, :]], axis=0))
        ssd_par = _pad_rows(jnp.stack([jnp.repeat(ssm_d[l], HEAD_DIM), ssm_norm[l]]))
        yc = _ssd_mixer(pc, col, rows, conv_wb, ssd_par, batch, seq)
        yd = _attention(q, k, v, None, batch, seq, tq, "mla_attention")

        pad = jnp.zeros((d, LANES - N_EXPERTS - N_EXPERT_GROUPS), F32)
        wr = jnp.concatenate([router_expert_w[l], router_group_w[l], pad], axis=1)
        wr_hi = wr.astype(BF16)
        wr = jnp.concatenate([wr_hi, (wr - wr_hi.astype(F32)).astype(BF16)], axis=1)
        br = jnp.concatenate([router_expert_b[l], router_group_b[l], pad[0]])[None, :]
        x2, h2, rrow, cnt = _outproj(xf, ya, yb, yc, yd, w_out[l].astype(BF16), norm_ffn[l][None, :], wr, br,
                                     tm, tmd)

        lrow, lcol, tab, ctab, meta = _route(rrow, cnt, tmd, nbp)
        xs = _scatter(tab, ctab, meta, lrow, h2, tmd, nb)
        ys = _ffn(meta, xs, expert_w_gate, expert_w_up, expert_w_down, l, nb)
        final = l == depth - 1
        xf = _gather(tab, ctab, lcol, x2, norm_final[None, :], ys, tmd, final)

    return xf.reshape(batch, seq, d)
```

```python
import functools
import math

import jax
import jax.numpy as jnp
import numpy as np
from jax import lax
from jax.experimental import pallas as pl
from jax.experimental.pallas import tpu as pltpu

F32 = jnp.float32
BF16 = jnp.bfloat16
I32 = jnp.int32

LANES = 128
VMEM_LIMIT_BYTES = 56 * 1024 * 1024

D_MODEL = 1024
RMS_EPS = 1e-6
LOG2E = math.log2(math.e)
GROUP_WIDTH = 256
HEAD_DIM = 64
N_HEADS = 4

CONV_A_WIDTH = 3
SSM_CONV = 4
SSM_STATE = 64
SSM_CHUNK = 256

MLA_NOPE = 64
MLA_ROPE = 32
MLA_Q_LORA = 256
MLA_KV_LORA = 128
ROPE_BASE = 10000.0
MLA_CHUNK = 64
ATTN_TQ = 512
ROW_TILE = 1024

N_EXPERT_GROUPS = 4
EXPERTS_PER_GROUP = 8
N_EXPERTS = 32
EXPERT_FF = 256
ROUTER_ROWS = 48
MOE_ROWS = 512
ZERO_ROWS = 256

SEG_A = (0, 768)
SEG_B = (768, 1280)
SEG_C = (1280, 2048)
SEG_D = (2048, 2432)
SEG_M = (2432, 2560)
SEG_M2 = (2560, 2688)
IN_COLS_PADDED = 2688
HEAD_PAD = N_HEADS * LANES
AUG_LANE = HEAD_DIM
MISC_F = 0
MISC_DT = 4
MISC_ROPE = 64
COL_CUMF = 0
COL_DT = 4
COL_ACUM = 8
N_SCALAR_ROWS = 16


def _cparams(*sem):
    return pltpu.CompilerParams(dimension_semantics=sem, vmem_limit_bytes=VMEM_LIMIT_BYTES)


def _lane_iota(shape):
    return lax.broadcasted_iota(I32, shape, len(shape) - 1)


def _row_iota(shape):
    return lax.broadcasted_iota(I32, shape, 0)


def _rms(x, g):
    ms = jnp.mean(x * x, axis=-1, keepdims=True)
    return x * lax.rsqrt(ms + RMS_EPS) * g


def _silu(x):
    return x / (1.0 + jnp.exp(-x))


def _softplus(x):
    return jnp.maximum(x, 0.0) + jnp.log(1.0 + jnp.exp(-jnp.abs(x)))


def _shift_rows(x, k):
    rolled = pltpu.roll(x, k, 0)
    return jnp.where(_row_iota(x.shape) >= k, rolled, 0.0)


def _rope_kernel(pos_ref, freq_ref, cos_ref, sin_ref):
    q = pos_ref.shape[2]
    lane = _lane_iota((q, LANES))
    group = lane >> int(math.log2(MLA_ROPE))
    rows = jnp.concatenate([pos_ref[0], jnp.zeros((8 - ROPE_PACK, q), F32)], axis=0)
    cols = jnp.concatenate([rows] * (LANES // 8), axis=0).T
    pos = cols[:, ROPE_PACK - 1:ROPE_PACK]
    for k in range(ROPE_PACK - 2, -1, -1):
        pos = jnp.where(group == k, cols[:, k:k + 1], pos)
    ang = pos * freq_ref[...]
    c = jnp.cos(ang)
    s = jnp.sin(ang)
    rope = (lane >= MISC_ROPE) & (lane < MISC_ROPE + MLA_ROPE)
    for k in range(ROPE_PACK):
        shift = (MISC_ROPE - MLA_ROPE * k) % LANES
        ck = pltpu.roll(c, shift, 1) if shift else c
        sk = pltpu.roll(s, shift, 1) if shift else s
        cos_ref[k * q:(k + 1) * q, :] = jnp.where(rope, ck, jnp.where(lane < MISC_ROPE, 1.0, 0.0))
        sin_ref[k * q:(k + 1) * q, :] = jnp.where(rope, sk, 0.0)


ROPE_PACK = LANES // MLA_ROPE


def _rope_tables(positions, tm):
    t = positions.size
    pos = positions.astype(F32).reshape(t // tm, ROPE_PACK, tm // ROPE_PACK)
    inv = ROPE_BASE ** (-np.arange(0, MLA_ROPE, 2, dtype=np.float32) / MLA_ROPE)
    freq = np.tile(np.concatenate([inv, inv]), ROPE_PACK)[None, :].astype(np.float32)
    return pl.pallas_call(
        _rope_kernel,
        grid=(t // tm,),
        in_specs=[pl.BlockSpec((1, ROPE_PACK, tm // ROPE_PACK), lambda i: (i, 0, 0)),
                  pl.BlockSpec((1, LANES), lambda i: (0, 0))],
        out_specs=[pl.BlockSpec((tm, LANES), lambda i: (i, 0))] * 2,
        out_shape=[jax.ShapeDtypeStruct((t, LANES), F32)] * 2,
        compiler_params=_cparams("parallel"),
        name="rope_tables",
    )(pos, jnp.asarray(freq))


def _arrange_w_in_kernel(wt_ref, w_ref, wvt_ref):
    gw = GROUP_WIDTH
    rows = lambda lo, hi: wt_ref[0, lo:hi, :]

    def put(seg0, piece_t):
        w_ref[:, seg0:seg0 + piece_t.shape[0]] = piece_t.T.astype(BF16)

    for j in range(3):
        put(SEG_A[0] + j * gw, rows(j * gw, (j + 1) * gw))
    put(SEG_B[0], rows(3 * gw, 4 * gw) * (HEAD_DIM ** -0.5 * LOG2E))
    put(SEG_B[0] + gw, rows(4 * gw, 5 * gw))
    wvt_ref[...] = rows(5 * gw, 6 * gw).astype(BF16)
    c0 = 6 * gw
    win = pltpu.roll(rows(c0, c0 + 3 * gw + 8), 3 * gw + 8 - N_HEADS, 0)
    for j in range(3):
        put(SEG_C[0] + j * gw, win[j * gw:(j + 1) * gw])
    d0 = c0 + 3 * gw + 8
    put(SEG_D[0], rows(d0, d0 + MLA_Q_LORA))
    put(SEG_D[0] + MLA_Q_LORA, rows(d0 + MLA_Q_LORA, d0 + MLA_Q_LORA + MLA_KV_LORA))
    kr0 = d0 + MLA_Q_LORA + MLA_KV_LORA
    half = MLA_ROPE // 2
    first8 = jnp.where(_row_iota((8, D_MODEL)) < N_HEADS, rows(c0, c0 + 8), rows(d0 - 8, d0))
    zeros = lambda n: jnp.zeros((n, D_MODEL), F32)
    misc_t = jnp.concatenate([first8, zeros(MISC_ROPE - 8), rows(kr0, kr0 + MLA_ROPE),
                              zeros(LANES - MISC_ROPE - MLA_ROPE)], axis=0)
    misc2_t = jnp.concatenate([zeros(MISC_ROPE), -rows(kr0 + half, kr0 + MLA_ROPE), rows(kr0, kr0 + half),
                               zeros(LANES - MISC_ROPE - MLA_ROPE)], axis=0)
    put(SEG_M[0], misc_t)
    put(SEG_M2[0], misc2_t)


def _inproj_kernel(x_ref, g_ref, wt_ref, cw_ref, cos_ref, sin_ref, nq_ref, nkv_ref, wq2_ref, wk_ref,
                   wvt2_ref, oa, ob, oc, om, ovt, q_ref, k_ref, vt_ref, w_ref, wvt_ref, halo_ref,
                   *, tm, tiles_per_seq):
    @pl.when(pl.program_id(0) == 0)
    def _():
        _arrange_w_in_kernel(wt_ref, w_ref, wvt_ref)

    h = _rms(x_ref[...], g_ref[...]).astype(BF16)
    for o, (lo, hi) in ((ob, SEG_B), (oc, SEG_C)):
        o[...] = jnp.dot(h, w_ref[:, lo:hi], preferred_element_type=F32).astype(o.dtype)

    gw = GROUP_WIDTH
    pa = jnp.dot(h, w_ref[:, SEG_A[0]:SEG_A[1]], preferred_element_type=F32)
    cv = pa[:, gw:2 * gw] * pa[:, 2 * gw:3 * gw]

    @pl.when(pl.program_id(0) % tiles_per_seq == 0)
    def _():
        halo_ref[...] = jnp.zeros_like(halo_ref)

    halo = halo_ref[...]
    row8 = _row_iota(halo.shape)
    acc = cv * cw_ref[CONV_A_WIDTH - 1:CONV_A_WIDTH, :]
    for k in range(1, CONV_A_WIDTH):
        shifted = pltpu.roll(cv, k, 0)
        top = jnp.where(row8 < k, pltpu.roll(halo, k, 0), shifted[0:8])
        shifted = jnp.concatenate([top, shifted[8:]], axis=0)
        acc = acc + shifted * cw_ref[CONV_A_WIDTH - 1 - k:CONV_A_WIDTH - k, :]
    halo_ref[...] = cv[tm - 8:tm]
    oa[...] = (pa[:, 0:gw] * acc).astype(oa.dtype)
    ovt[...] = lax.dot_general(wvt_ref[...], h, (((1,), (1,)), ((), ())),
                               preferred_element_type=F32).astype(ovt.dtype)
    misc = jnp.dot(h, w_ref[:, SEG_M[0]:SEG_M[1]], preferred_element_type=F32)
    misc2 = jnp.dot(h, w_ref[:, SEG_M2[0]:SEG_M2[1]], preferred_element_type=F32)
    om[...] = misc

    pd = jnp.dot(h, w_ref[:, SEG_D[0]:SEG_D[1]], preferred_element_type=F32)
    cq = _rms(pd[:, 0:MLA_Q_LORA], nq_ref[...]).astype(BF16)
    ckv = _rms(pd[:, MLA_Q_LORA:MLA_Q_LORA + MLA_KV_LORA], nkv_ref[...]).astype(BF16)
    cos = cos_ref[...]
    sin = sin_ref[...]
    cos4 = jnp.concatenate([cos] * N_HEADS, axis=1)
    sin4 = jnp.concatenate([sin] * N_HEADS, axis=1)
    scale = (MLA_NOPE + MLA_ROPE) ** -0.5 * LOG2E
    q2 = jnp.dot(cq, wq2_ref[...], preferred_element_type=F32)
    q_ref[...] = ((q2[:, 0:HEAD_PAD] * cos4 + q2[:, HEAD_PAD:2 * HEAD_PAD] * sin4) * scale).astype(q_ref.dtype)
    lane = _lane_iota(cos.shape)
    rope = (lane >= MISC_ROPE) & (lane < MISC_ROPE + MLA_ROPE)
    kr = jnp.where(rope, misc * cos + misc2 * sin, 0.0)
    k = jnp.dot(ckv, wk_ref[...], preferred_element_type=F32)
    k_ref[...] = (k + jnp.concatenate([kr] * N_HEADS, axis=1)).astype(k_ref.dtype)
    vt_ref[...] = lax.dot_general(wvt2_ref[...], ckv, (((1,), (1,)), ((), ())),
                                  preferred_element_type=F32).astype(vt_ref.dtype)


def _inproj(x, g, w_in_t, layer, conv_w, cos, sin, nq, nkv, wq2, wk, wvt2, tm, seq):
    t = x.shape[0]
    once = lambda a: pl.BlockSpec((1,) + a.shape[1:], lambda i: (layer, 0, 0), pipeline_mode=pl.Buffered(1))
    full = lambda a: pl.BlockSpec(a.shape, lambda i: (0, 0))
    tile = lambda wd: pl.BlockSpec((tm, wd), lambda i: (i, 0))
    cols = lambda: pl.BlockSpec((GROUP_WIDTH, tm), lambda i: (0, i))
    seg = lambda s: s[1] - s[0]
    return pl.pallas_call(
        functools.partial(_inproj_kernel, tm=tm, tiles_per_seq=seq // tm),
        grid=(t // tm,),
        in_specs=[tile(D_MODEL), full(g), once(w_in_t), full(conv_w), tile(LANES), tile(LANES),
                  full(nq), full(nkv), full(wq2), full(wk), full(wvt2)],
        out_specs=[tile(GROUP_WIDTH), tile(seg(SEG_B)), tile(seg(SEG_C)), tile(LANES), cols(),
                   tile(HEAD_PAD), tile(HEAD_PAD), cols()],
        out_shape=[jax.ShapeDtypeStruct((t, GROUP_WIDTH), BF16), jax.ShapeDtypeStruct((t, seg(SEG_B)), BF16),
                   jax.ShapeDtypeStruct((t, seg(SEG_C)), BF16), jax.ShapeDtypeStruct((t, LANES), F32),
                   jax.ShapeDtypeStruct((GROUP_WIDTH, t), BF16),
                   jax.ShapeDtypeStruct((t, HEAD_PAD), BF16), jax.ShapeDtypeStruct((t, HEAD_PAD), BF16),
                   jax.ShapeDtypeStruct((GROUP_WIDTH, t), BF16)],
        scratch_shapes=[pltpu.VMEM((D_MODEL, IN_COLS_PADDED), BF16), pltpu.VMEM((GROUP_WIDTH, D_MODEL), BF16),
                        pltpu.VMEM((8, GROUP_WIDTH), F32)],
        compiler_params=_cparams("arbitrary"),
        name="inproj",
    )(x, g, w_in_t, conv_w, cos, sin, nq, nkv, wq2, wk, wvt2)


def _scalar_prep_kernel(m_ref, p_ref, qk_ref, place_ref,
                        col_ref, row_ref, qa_ref, ka_ref, tref_ref, *, tq):
    s = m_ref.shape[0]
    tref_ref[...] = jnp.zeros_like(tref_ref)
    m = m_ref[...]
    bias = p_ref[0:1, :]
    a_log = p_ref[1:2, :]
    lane = _lane_iota(m.shape)
    z = m + bias
    logf = jnp.minimum(z, 0.0) - jnp.log(1.0 + jnp.exp(-jnp.abs(z)))
    dt = _softplus(z)
    a = dt * (-jnp.exp(a_log))
    is_f = lane < MISC_DT
    is_dt = (lane >= MISC_DT) & (lane < MISC_DT + N_HEADS)
    v = jnp.where(is_f, logf, jnp.where(is_dt, a, 0.0))
    r = _row_iota((SSM_CHUNK, SSM_CHUNK))
    c = _lane_iota((SSM_CHUNK, SSM_CHUNK))
    tril = jnp.where(r >= c, 1.0, 0.0).astype(BF16)
    carry = jnp.zeros((1, LANES), F32)
    lane_1 = _lane_iota((1, LANES))
    lane_b = _lane_iota((SSM_CHUNK, LANES))
    for ci in range(s // SSM_CHUNK):
        rest = v[ci * SSM_CHUNK:(ci + 1) * SSM_CHUNK]
        cs = jnp.zeros((SSM_CHUNK, LANES), F32)
        for _ in range(3):
            term = rest.astype(BF16)
            cs = cs + jnp.dot(tril, term, preferred_element_type=F32)
            rest = rest - term.astype(F32)
        cs = cs + jnp.where(lane_1 < MISC_DT, carry, 0.0)
        carry = cs[SSM_CHUNK - 1:SSM_CHUNK]
        acum = pltpu.roll(cs, COL_ACUM - MISC_DT, 1)
        out = jnp.where(lane_b < MISC_DT, cs * LOG2E,
                        jnp.where(lane_b < COL_ACUM, dt[ci * SSM_CHUNK:(ci + 1) * SSM_CHUNK],
                                  jnp.where(lane_b < COL_ACUM + N_HEADS, acum, 0.0)))
        rows = slice(ci * SSM_CHUNK, (ci + 1) * SSM_CHUNK)
        col_ref[rows, :] = out
        row_ref[0, :, rows] = out.T[:N_SCALAR_ROWS]
    for ci in range(s // SSM_CHUNK):
        rows = slice(ci * SSM_CHUNK, (ci + 1) * SSM_CHUNK)
        ti = (ci * SSM_CHUNK) // tq
        tile_ref = col_ref[ti * tq:ti * tq + 1, :]
        if (ci * SSM_CHUNK) % tq == 0:
            tref_ref[0, ti:ti + 1, :] = tile_ref
        c = col_ref[rows, :] - tile_ref
        c_hi = c.astype(BF16).astype(F32)
        r1 = c - c_hi
        c_mid = r1.astype(BF16).astype(F32)
        c_lo = r1 - c_mid
        terms = jnp.where(lane_b < N_HEADS, c_hi,
                          jnp.where(lane_b < 2 * N_HEADS, pltpu.roll(c_mid, N_HEADS, 1),
                                    jnp.where(lane_b < 3 * N_HEADS, pltpu.roll(c_lo, 2 * N_HEADS, 1),
                                              jnp.where(lane_b == ONE_LANE, 1.0, 0.0)))).astype(BF16)
        for side, o_ref in enumerate((qa_ref, ka_ref)):
            for j in range(N_HEADS // 2):
                pair = qk_ref[rows, side * GROUP_WIDTH + j * LANES:
                              side * GROUP_WIDTH + (j + 1) * LANES]
                lhs = jnp.concatenate([pair, terms], axis=1)
                o_ref[rows, 2 * j * LANES:(2 * j + 2) * LANES] = jnp.dot(
                    lhs, place_ref[side, j], preferred_element_type=F32).astype(o_ref.dtype)


AUG_TERMS = 6
ONE_LANE = LANES - 1


def _fox_placement():
    assert COL_CUMF == 0
    place = np.zeros((2, N_HEADS // 2, 2 * LANES, 2 * LANES), np.float32)
    for h in range(N_HEADS):
        j, half = divmod(h, 2)
        o = half * LANES
        one = LANES + ONE_LANE
        for side in range(2):
            for d in range(HEAD_DIM):
                place[side, j, half * HEAD_DIM + d, o + d] = 1.0
        for term in range(3):
            src = LANES + term * N_HEADS + h
            place[0, j, src, o + AUG_LANE + term] = 1.0
            place[0, j, one, o + AUG_LANE + 3 + term] = 1.0
            place[1, j, one, o + AUG_LANE + term] = 1.0
            place[1, j, src, o + AUG_LANE + 3 + term] = -1.0
    return jnp.asarray(place, BF16)


def _scalar_prep(misc, params, qk, batch, seq, tq):
    place = _fox_placement()
    full = lambda a: pl.BlockSpec(a.shape, lambda b: (0,) * a.ndim)
    return pl.pallas_call(
        functools.partial(_scalar_prep_kernel, tq=tq),
        grid=(batch,),
        in_specs=[pl.BlockSpec((seq, LANES), lambda b: (b, 0)),
                  pl.BlockSpec((8, LANES), lambda b: (0, 0)),
                  pl.BlockSpec((seq, 2 * GROUP_WIDTH), lambda b: (b, 0)),
                  full(place)],
        out_specs=[pl.BlockSpec((seq, LANES), lambda b: (b, 0)),
                   pl.BlockSpec((1, N_SCALAR_ROWS, seq), lambda b: (b, 0, 0)),
                   pl.BlockSpec((seq, HEAD_PAD), lambda b: (b, 0)),
                   pl.BlockSpec((seq, HEAD_PAD), lambda b: (b, 0)),
                   pl.BlockSpec((1, 8, LANES), lambda b: (b, 0, 0))],
        out_shape=[jax.ShapeDtypeStruct((batch * seq, LANES), F32),
                   jax.ShapeDtypeStruct((batch, N_SCALAR_ROWS, seq), F32),
                   jax.ShapeDtypeStruct((batch * seq, HEAD_PAD), BF16),
                   jax.ShapeDtypeStruct((batch * seq, HEAD_PAD), BF16),
                   jax.ShapeDtypeStruct((batch, 8, LANES), F32)],
        compiler_params=_cparams("parallel"),
        name="scalar_prep",
    )(misc, params, qk, place)


def _pair_lanes(col, base, shape):
    lane = _lane_iota(shape)
    return jnp.where(lane < HEAD_DIM, col[:, base:base + 1], col[:, base + 1:base + 2])


def _ssd_kernel(p_ref, col_ref, row_ref, cw_ref, par_ref, o_ref, u_ref):
    s = p_ref.shape[0]
    q = SSM_CHUNK
    gw = GROUP_WIDTH
    xbc = p_ref[:, gw:3 * gw].astype(F32)
    acc = xbc * cw_ref[SSM_CONV - 1:SSM_CONV, :]
    for k in range(1, SSM_CONV):
        acc = acc + _shift_rows(xbc, k) * cw_ref[SSM_CONV - 1 - k:SSM_CONV - k, :]
    u_ref[...] = _silu(acc + cw_ref[SSM_CONV:SSM_CONV + 1, :])

    d_skip = par_ref[0:1, :]
    norm_g = par_ref[1:2, :]
    lane_q = _lane_iota((q, LANES))
    low = lane_q < HEAD_DIM
    tri = _row_iota((q, q)) >= _lane_iota((q, q))

    def chunk(ci, states):
        rows = pl.ds(ci * q, q)
        u = u_ref[rows, :]
        col = col_ref[rows, :]
        bm = u[:, gw:gw + LANES]
        cm = u[:, gw + LANES:gw + 2 * LANES]
        z = p_ref[rows, 0:gw].astype(F32)
        new_states = []
        ys = []
        for g in range(2):
            sel = low if g == 0 else jnp.logical_not(low)
            cg = jnp.where(sel, cm, 0.0).astype(BF16)
            bg = jnp.where(sel, bm, 0.0)
            gmat = lax.dot_general(cg, bm.astype(BF16), (((1,), (1,)), ((), ())),
                                   preferred_element_type=F32)
            xs = u[:, g * LANES:(g + 1) * LANES]
            dt2 = _pair_lanes(col, COL_DT + 2 * g, (q, LANES))
            ac2 = _pair_lanes(col, COL_ACUM + 2 * g, (q, LANES))
            xdt = xs * dt2
            xdt_b = xdt.astype(BF16)
            st = states[g]
            y_off = jnp.dot(cg, st.astype(BF16), preferred_element_type=F32) * jnp.exp(ac2)
            halves = []
            for hh in range(2):
                h = 2 * g + hh
                ac_col = col[:, COL_ACUM + h:COL_ACUM + h + 1]
                ac_row = row_ref[0, COL_ACUM + h:COL_ACUM + h + 1, rows]
                decay = jnp.exp(jnp.where(tri, ac_col - ac_row, -1e30))
                mm = (gmat * decay).astype(BF16)
                halves.append(jnp.dot(mm, xdt_b, preferred_element_type=F32))
            y = jnp.where(low, halves[0], halves[1]) + y_off + d_skip[:, g * LANES:(g + 1) * LANES] * xs
            ys.append(y)
            ac_last = ac2[q - 1:q, :]
            w_end = jnp.exp(ac_last - ac2)
            xw = (xdt * w_end).astype(BF16)
            upd = jnp.dot(bg.T.astype(BF16), xw, preferred_element_type=F32)
            new_states.append(st * jnp.exp(ac_last) + upd)
        yfull = jnp.concatenate(ys, axis=1) * _silu(z)
        o_ref[rows, :] = _rms(yfull, norm_g).astype(o_ref.dtype)
        return tuple(new_states)

    init = (jnp.zeros((LANES, LANES), F32), jnp.zeros((LANES, LANES), F32))
    states = init
    for ci in range(s // q):
        states = chunk(ci, states)


def _ssd_mixer(pc, col, rows, conv_wb, par, batch, seq):
    gw = GROUP_WIDTH
    return pl.pallas_call(
        _ssd_kernel,
        grid=(batch,),
        in_specs=[pl.BlockSpec((seq, 3 * gw), lambda b: (b, 0)),
                  pl.BlockSpec((seq, LANES), lambda b: (b, 0)),
                  pl.BlockSpec((1, N_SCALAR_ROWS, seq), lambda b: (b, 0, 0)),
                  pl.BlockSpec((8, 2 * gw), lambda b: (0, 0)),
                  pl.BlockSpec((8, gw), lambda b: (0, 0))],
        out_specs=pl.BlockSpec((seq, gw), lambda b: (b, 0)),
        out_shape=jax.ShapeDtypeStruct((batch * seq, gw), BF16),
        scratch_shapes=[pltpu.VMEM((seq, 2 * gw), F32)],
        compiler_params=_cparams("parallel"),
        name="ssd_mixer",
    )(pc, col, rows, conv_wb, par)


def _attn_kernel(*refs, fox, tq, nq):
    if fox:
        tref_ref, q_ref, k_ref, vt_ref, o_ref = refs
    else:
        q_ref, k_ref, vt_ref, o_ref = refs
        tref_ref = None
    b = pl.program_id(0)
    i = pl.program_id(1)
    key = _row_iota((tq, tq))
    qry = _lane_iota((tq, tq))
    if fox:
        allowed = key <= qry
    else:
        shift = int(math.log2(MLA_CHUNK))
        allowed = (key >> shift) <= (qry >> shift)
    qs = [q_ref[:, h * LANES:(h + 1) * LANES] for h in range(N_HEADS)]
    ones_rows = jnp.ones((16, tq), BF16)

    def step(j, masked, carry):
        rk = pl.ds(j * tq, tq)
        scores = [lax.dot_general(k_ref[rk, h * LANES:(h + 1) * LANES], qs[h], (((1,), (1,)), ((), ())),
                                  preferred_element_type=F32) for h in range(N_HEADS)]
        probs = []
        for h in range(N_HEADS):
            m, l, _ = carry[h]
            s = scores[h]
            if masked:
                s = jnp.where(allowed, s, -1e30)
            delta = (tref_ref[b, i, h] - tref_ref[b, j, h]) if fox else 0.0
            m_new = jnp.maximum(m, jnp.max(s, axis=0, keepdims=True) + delta)
            alpha = jnp.exp2(m - m_new)
            p = jnp.exp2(s - (m_new - delta))
            probs.append((m_new, alpha, p.astype(BF16)))
        new = []
        for h in range(N_HEADS):
            pair = h // 2
            m_new, alpha, p = probs[h]
            lhs = jnp.concatenate([vt_ref[pair * LANES:(pair + 1) * LANES, rk], ones_rows], axis=0)
            pv = jnp.dot(lhs, p, preferred_element_type=F32)
            new.append((m_new, alpha * carry[h][1] + pv[LANES:LANES + 1], alpha * carry[h][2] + pv[0:LANES]))
        return tuple(new)

    init = tuple((jnp.full((1, tq), -1e30, F32), jnp.zeros((1, tq), F32), jnp.zeros((LANES, tq), F32))
                 for _ in range(N_HEADS))
    top = _row_iota((LANES, tq)) < HEAD_DIM

    def sweep(n_before):
        carry = init
        for j in range(n_before):
            carry = step(j, False, carry)
        carry = step(n_before, True, carry)
        outs = [acc / l for (_, l, acc) in carry]
        o_t = jnp.concatenate([jnp.where(top, outs[0], outs[1]), jnp.where(top, outs[2], outs[3])], axis=0)
        o_ref[...] = o_t.T.astype(o_ref.dtype)

    for n_before in range(nq):
        pl.when(i == n_before)(functools.partial(sweep, n_before))


def _attention(q, k, vt, tref, batch, seq, tq, name):
    nq = seq // tq
    fox = tref is not None
    kern = functools.partial(_attn_kernel, fox=fox, tq=tq, nq=nq)
    grid_spec = pltpu.PrefetchScalarGridSpec(
        num_scalar_prefetch=1 if fox else 0,
        grid=(batch, nq),
        in_specs=[pl.BlockSpec((tq, HEAD_PAD), lambda b, i, *_: (b * nq + i, 0)),
                  pl.BlockSpec((seq, HEAD_PAD), lambda b, i, *_: (b, 0)),
                  pl.BlockSpec((GROUP_WIDTH, seq), lambda b, i, *_: (0, b))],
        out_specs=pl.BlockSpec((tq, GROUP_WIDTH), lambda b, i, *_: (b * nq + i, 0)),
    )
    args = ((tref,) if fox else ()) + (q, k, vt)
    return pl.pallas_call(
        kern,
        grid_spec=grid_spec,
        out_shape=jax.ShapeDtypeStruct((batch * seq, GROUP_WIDTH), BF16),
        compiler_params=_cparams("parallel", "arbitrary"),
        name=name,
    )(*args)


def _outproj_kernel(x_ref, ya, yb, yc, yd, w_ref, g_ref, wr_ref, br_ref,
                    x2_ref, h2_ref, rrow_ref, cnt_ref, *, tm, moe_tile):
    y = jnp.concatenate([ya[...], yb[...], yc[...], yd[...]], axis=1)
    x2 = x_ref[...] + jnp.dot(y, w_ref[...], preferred_element_type=F32)
    x2_ref[...] = x2
    h2 = _rms(x2, g_ref[...])
    h2_ref[...] = h2.astype(h2_ref.dtype)
    h_hi = h2.astype(BF16)
    h_lo = (h2 - h_hi.astype(F32)).astype(BF16)
    part = jnp.dot(h_hi, wr_ref[...], preferred_element_type=F32)
    logits = (part[:, 0:LANES] + part[:, LANES:2 * LANES]
              + jnp.dot(h_lo, wr_ref[:, 0:LANES], preferred_element_type=F32) + br_ref[...])
    lt = logits.T[0:ROUTER_ROWS]
    row = _row_iota(lt.shape)
    neg = -1e30
    big = 1 << 20
    gmask = (row >= N_EXPERTS) & (row < N_EXPERTS + N_EXPERT_GROUPS)
    gl = jnp.where(gmask, lt, neg)
    gmax = jnp.max(gl, axis=0, keepdims=True)
    gsum = jnp.sum(jnp.where(gmask, jnp.exp(gl - gmax), 0.0), axis=0, keepdims=True)
    g_w = 1.0 / gsum
    g_idx = jnp.min(jnp.where(gmask & (gl == gmax), row, big), axis=0, keepdims=True) - N_EXPERTS
    emask = (row < N_EXPERTS) & ((row >> int(math.log2(EXPERTS_PER_GROUP))) == g_idx)
    el = jnp.where(emask, lt, neg)
    e1v = jnp.max(el, axis=0, keepdims=True)
    esum = jnp.sum(jnp.where(emask, jnp.exp(el - e1v), 0.0), axis=0, keepdims=True)
    i1 = jnp.min(jnp.where(emask & (el == e1v), row, big), axis=0, keepdims=True)
    el2 = jnp.where(row == i1, neg, el)
    e2v = jnp.max(el2, axis=0, keepdims=True)
    i2 = jnp.min(jnp.where(emask & (row != i1) & (el2 == e2v), row, big), axis=0, keepdims=True)
    p1 = 1.0 / esum
    p2 = jnp.exp(e2v - e1v) / esum
    w1 = g_w * (p1 / (p1 + p2))
    w2 = g_w * (p2 / (p1 + p2))
    out_row = _row_iota(rrow_ref.shape)
    rrow_ref[...] = jnp.where(out_row == 0, i1.astype(F32),
                              jnp.where(out_row == 1, i2.astype(F32),
                                        jnp.where(out_row == 2, w1, jnp.where(out_row == 3, w2, 0.0))))
    step = pl.program_id(0)

    @pl.when(step == 0)
    def _():
        cnt_ref[...] = jnp.zeros_like(cnt_ref)

    chosen = jnp.where((row == i1) | (row == i2), 1.0, 0.0).astype(BF16)
    tiles = tm // moe_tile
    tile_of = (_row_iota((tm, LANES)) >> int(math.log2(moe_tile))) + step * tiles
    to_tile = jnp.where(_lane_iota((tm, LANES)) == tile_of, 1.0, 0.0).astype(BF16)
    counts = jnp.dot(chosen, to_tile, preferred_element_type=F32)
    cnt_ref[...] += counts[0:N_EXPERTS]


def _outproj(x, ya, yb, yc, yd, w, g, wr, br, tm, moe_tile):
    t = x.shape[0]
    full = lambda a: pl.BlockSpec(a.shape, lambda i: (0, 0))
    tile = lambda wd: pl.BlockSpec((tm, wd), lambda i: (i, 0))
    return pl.pallas_call(
        functools.partial(_outproj_kernel, tm=tm, moe_tile=moe_tile),
        grid=(t // tm,),
        in_specs=[tile(D_MODEL)] + [tile(GROUP_WIDTH)] * 4 + [full(w), full(g), full(wr), full(br)],
        out_specs=[tile(D_MODEL), tile(D_MODEL), pl.BlockSpec((8, tm), lambda i: (0, i)),
                   pl.BlockSpec((N_EXPERTS, LANES), lambda i: (0, 0))],
        out_shape=[jax.ShapeDtypeStruct((t, D_MODEL), F32), jax.ShapeDtypeStruct((t, D_MODEL), BF16),
                   jax.ShapeDtypeStruct((8, t), F32), jax.ShapeDtypeStruct((N_EXPERTS, LANES), F32)],
        compiler_params=_cparams("arbitrary"),
        name="outproj_router",
    )(x, ya, yb, yc, yd, w, g, wr, br)


MOE_TILE = 256
CHUNK = 8
LOCAL_ROWS = 2 * MOE_TILE + 256
PACKED = D_MODEL // 2
XS_WIDTH = PACKED
U32 = jnp.uint32


def _pack_bf16_pairs(x, exact=False):
    if not exact:
        x = x.astype(BF16).astype(F32)
    half = x.shape[1] // 2
    lo = lax.bitcast_convert_type(x[:, :half], U32)
    hi = lax.bitcast_convert_type(x[:, half:], U32)
    return hi | (lo >> 16)


def _unpack_bf16_pairs(words):
    lo = lax.bitcast_convert_type(words << 16, F32)
    hi = lax.bitcast_convert_type(words & U32(0xFFFF0000), F32)
    return jnp.concatenate([lo, hi], axis=1).astype(BF16)


COPY_ROWS = (2 * CHUNK, CHUNK)


def _route_kernel(r_ref, cnt_ref, lrow_ref, lcol_ref, tab_ref, ctab_ref, meta_ref,
                  loff_ref, goff_ref, n8_ref, *, tm, nbp, tiles):
    i = pl.program_id(0)
    hi = lax.Precision.HIGHEST

    @pl.when(i == 0)
    def _():
        cnt = cnt_ref[...]
        n8 = jnp.floor((cnt + (CHUNK - 1)) * (1.0 / CHUNK)) * CHUNK
        er = _row_iota((N_EXPERTS, N_EXPERTS))
        ec = _lane_iota((N_EXPERTS, N_EXPERTS))
        below = jnp.where(er > ec, 1.0, 0.0)
        loff = jnp.dot(below, n8, preferred_element_type=F32, precision=hi)
        rows_e = jnp.sum(n8, axis=-1, keepdims=True) + jnp.zeros_like(n8)
        padded = jnp.floor((rows_e + (MOE_ROWS - 1)) * (1.0 / MOE_ROWS)) * MOE_ROWS
        e_start = jnp.dot(below, padded, preferred_element_type=F32, precision=hi)
        tr = _row_iota((LANES, LANES))
        tc = _lane_iota((LANES, LANES))
        earlier = jnp.where(tr < tc, 1.0, 0.0)
        goff = e_start + jnp.dot(n8, earlier, preferred_element_type=F32, precision=hi)
        loff_ref[...] = loff
        goff_ref[...] = goff
        n8_ref[...] = n8
        big = jnp.floor(n8 * (0.5 / CHUNK))
        small = n8 * (1.0 / CHUNK) - 2.0 * big
        row_t = _row_iota(tab_ref.shape)
        tab_ref[...] = jnp.where(row_t == 0, jnp.sum(big, axis=0, keepdims=True),
                                 jnp.where(row_t == 1, jnp.sum(small, axis=0, keepdims=True), 0.0)).astype(I32)
        reps = nbp // LANES
        pend_b = jnp.concatenate([e_start + padded] * reps, axis=1)
        vend_b = jnp.concatenate([e_start + rows_e] * reps, axis=1)
        used_b = jnp.concatenate([padded] * reps, axis=1) > 0.0
        b0 = (_lane_iota((N_EXPERTS, nbp)) * MOE_ROWS).astype(F32)
        bexp = jnp.sum(jnp.where(pend_b <= b0, 1.0, 0.0), axis=0, keepdims=True)
        bexp = jnp.minimum(bexp, N_EXPERTS - 1.0)
        e_b = _row_iota((N_EXPERTS, nbp)).astype(F32)
        is_e = e_b == bexp
        vend = jnp.sum(jnp.where(is_e, vend_b, 0.0), axis=0, keepdims=True)
        nvalid = jnp.clip(vend - b0[0:1], 0.0, float(MOE_ROWS))
        total = jnp.max(pend_b, axis=0, keepdims=True) * (1.0 / MOE_ROWS)
        order = jnp.sum(jnp.where(used_b & (e_b < bexp), 1.0, 0.0), axis=0, keepdims=True)
        nxt = jnp.min(jnp.where(used_b & (e_b > bexp), e_b, float(N_EXPERTS)), axis=0, keepdims=True)
        row = _row_iota((8, nbp))
        meta = jnp.where(row == 0, bexp, jnp.where(row == 1, nvalid, jnp.where(row == 2, total,
                         jnp.where(row == 3, order, jnp.where(row == 4, nxt, 0.0)))))
        meta_ref[...] = meta.astype(I32)

    su = jnp.where(_row_iota((tm, tm)) < _lane_iota((tm, tm)), 1.0, 0.0).astype(BF16)
    incl = jnp.where(_row_iota((N_EXPERTS, N_EXPERTS)) >= _lane_iota((N_EXPERTS, N_EXPERTS)), 1.0, 0.0)
    cidx = _lane_iota((N_EXPERTS, LANES)).astype(F32)
    e_iota = _row_iota((N_EXPERTS, tm))
    out_row = _row_iota((LANES, tm))
    for k in range(tiles):
        tok = slice(k * tm, (k + 1) * tm)
        oh0 = e_iota == r_ref[0:1, tok].astype(I32)
        oh1 = e_iota == r_ref[1:2, tok].astype(I32)
        oh = jnp.where(oh0 | oh1, 1.0, 0.0)
        tile_lane = _lane_iota((N_EXPERTS, LANES)) == i * tiles + k
        before = jnp.dot(oh.astype(BF16), su, preferred_element_type=F32)
        base = jnp.sum(jnp.where(tile_lane, loff_ref[...], 0.0), axis=-1, keepdims=True) + before
        d0 = jnp.sum(jnp.where(oh0, base, 0.0), axis=0, keepdims=True)
        d1 = jnp.sum(jnp.where(oh1, base, 0.0), axis=0, keepdims=True)
        lrow_ref[k, 0:1, :] = d0.astype(I32)
        lrow_ref[k, 1:2, :] = d1.astype(I32)
        pick_tile = lambda ref: jnp.sum(jnp.where(tile_lane, ref[...], 0.0), axis=-1, keepdims=True)
        nch = pick_tile(n8_ref) * (1.0 / CHUNK)
        n_big = jnp.floor(nch * 0.5)
        n_small = nch - 2.0 * n_big
        loff_t = pick_tile(loff_ref)
        goff_t = pick_tile(goff_ref)
        for c, (n, rows, first) in enumerate(((n_big, COPY_ROWS[0], 0.0),
                                              (n_small, COPY_ROWS[1], n_big * COPY_ROWS[0]))):
            cend = jnp.dot(incl, n + jnp.zeros((N_EXPERTS, LANES), F32), preferred_element_type=F32, precision=hi)
            cstart = cend - n
            mine = (cidx >= cstart) & (cidx < cend)
            step_rows = first + (cidx - cstart) * rows
            ctab_ref[k, 2 * c:2 * c + 1, :] = jnp.sum(jnp.where(mine, loff_t + step_rows, 0.0), axis=0,
                                                      keepdims=True).astype(I32)
            ctab_ref[k, 2 * c + 1:2 * c + 2, :] = jnp.sum(jnp.where(mine, goff_t + step_rows, 0.0), axis=0,
                                                          keepdims=True).astype(I32)
        lcol_ref[tok, :] = jnp.where(out_row == 0, d0, jnp.where(out_row == 1, d1,
                                     jnp.where(out_row == 2, r_ref[2:3, tok],
                                               jnp.where(out_row == 3, r_ref[3:4, tok], 0.0)))).T


def _route(rrow, cnt, tm, nbp):
    t = rrow.shape[1]
    nt = t // tm
    tiles = next(r for r in (4, 2, 1) if nt % r == 0)
    kern = functools.partial(_route_kernel, tm=tm, nbp=nbp, tiles=tiles)
    return pl.pallas_call(
        kern,
        grid=(nt // tiles,),
        in_specs=[pl.BlockSpec((8, tiles * tm), lambda i: (0, i)),
                  pl.BlockSpec((N_EXPERTS, LANES), lambda i: (0, 0))],
        out_specs=[pl.BlockSpec((tiles, 2, tm), lambda i: (i, 0, 0)),
                   pl.BlockSpec((tiles * tm, LANES), lambda i: (i, 0)),
                   pl.BlockSpec((8, LANES), lambda i: (0, 0)),
                   pl.BlockSpec((tiles, 2 * len(COPY_ROWS), LANES), lambda i: (i, 0, 0)),
                   pl.BlockSpec((8, nbp), lambda i: (0, 0))],
        out_shape=[jax.ShapeDtypeStruct((nt, 2, tm), I32), jax.ShapeDtypeStruct((t, LANES), F32),
                   jax.ShapeDtypeStruct((8, LANES), I32), jax.ShapeDtypeStruct((nt, 2 * len(COPY_ROWS), LANES), I32),
                   jax.ShapeDtypeStruct((8, nbp), I32)],
        scratch_shapes=[pltpu.VMEM((N_EXPERTS, LANES), F32)] * 3,
        compiler_params=_cparams("arbitrary"),
        name="moe_route",
    )(rrow, cnt)


def _chunk_copies(tabs, i, local_ref, global_ref, sem, to_global, action):
    tab_ref, ctab_ref = tabs
    for k, rows in enumerate(COPY_ROWS):
        count = tab_ref[k, i]

        def copy(lo, go, rows=rows):
            lsl = local_ref.at[pl.ds(pl.multiple_of(lo, CHUNK), rows)]
            gsl = global_ref.at[pl.ds(pl.multiple_of(go, CHUNK), rows)]
            return pltpu.make_async_copy(lsl, gsl, sem) if to_global else pltpu.make_async_copy(gsl, lsl, sem)

        if action == "wait":
            def one(c, c1, copy=copy):
                copy(0, 0).wait()
                return c1
        else:
            def one(c, c1, copy=copy, k=k):
                copy(ctab_ref[i, 2 * k, c], ctab_ref[i, 2 * k + 1, c]).start()
                return c1

        lax.fori_loop(0, count, one, 0)


def _scatter_kernel(tab_ref, ctab_ref, meta_ref, lrow_ref, h_ref, xs_ref, buf_ref, zero_ref, sem, zsem, *, tm, nb):
    i = pl.program_id(0)
    tabs = (tab_ref, ctab_ref)

    @pl.when(i == 0)
    def _():
        zero_ref[...] = jnp.zeros_like(zero_ref)
        n_used = meta_ref[2, 0]

        def each_piece(action):
            def body(b, c):
                for piece in range(MOE_ROWS // ZERO_ROWS):
                    @pl.when((b < n_used) & (meta_ref[1, b] < (piece + 1) * ZERO_ROWS))
                    def _():
                        start = pl.multiple_of(b * MOE_ROWS + piece * ZERO_ROWS, ZERO_ROWS)
                        cp = pltpu.make_async_copy(zero_ref, xs_ref.at[pl.ds(start, ZERO_ROWS)], zsem)
                        getattr(cp, action)()
                return c
            lax.fori_loop(0, nb, body, 0)

        each_piece("start")
        each_piece("wait")

    rows = _row_iota((LOCAL_ROWS, tm))

    def drain(tile, slot):
        _chunk_copies(tabs, tile, buf_ref.at[slot], xs_ref, sem.at[slot], True, "wait")

    for slot in range(2):
        tile = 2 * i + slot
        p0 = rows == lrow_ref[slot, 0:1, :]
        p1 = rows == lrow_ref[slot, 1:2, :]
        perm = jnp.where(p0 | p1, 1.0, 0.0).astype(BF16)
        sorted_rows = jnp.dot(perm, h_ref[slot * tm:(slot + 1) * tm, :],
                              preferred_element_type=F32)

        @pl.when(i > 0)
        def _():
            drain(tile - 2, slot)

        buf = buf_ref.at[slot]
        buf[...] = _pack_bf16_pairs(sorted_rows, exact=True)
        _chunk_copies(tabs, tile, buf, xs_ref, sem.at[slot], True, "start")

    @pl.when(i == pl.num_programs(0) - 1)
    def _():
        drain(2 * i, 0)
        drain(2 * i + 1, 1)


def _scatter(tab, ctab, meta, lrow, h2, tm, nb):
    t = h2.shape[0]
    kern = functools.partial(_scatter_kernel, tm=tm, nb=nb)
    grid_spec = pltpu.PrefetchScalarGridSpec(
        num_scalar_prefetch=3,
        grid=(t // (2 * tm),),
        in_specs=[pl.BlockSpec((2, 2, tm), lambda i, *_: (i, 0, 0)),
                  pl.BlockSpec((2 * tm, D_MODEL), lambda i, *_: (i, 0))],
        out_specs=pl.BlockSpec(memory_space=pl.ANY),
        scratch_shapes=[pltpu.VMEM((2, LOCAL_ROWS, XS_WIDTH), U32), pltpu.VMEM((ZERO_ROWS, XS_WIDTH), U32),
                        pltpu.SemaphoreType.DMA((2,)), pltpu.SemaphoreType.DMA],
    )
    return pl.pallas_call(
        kern,
        grid_spec=grid_spec,
        out_shape=jax.ShapeDtypeStruct((nb * MOE_ROWS, XS_WIDTH), U32),
        compiler_params=_cparams("arbitrary"),
        name="moe_scatter",
    )(tab, ctab, meta, lrow, h2)


def _ffn_kernel(meta_ref, x_ref, wg_ref, wu_ref, wd_ref, o_ref, wgu_b, wd_b, wg_f, wu_f, wd_f, wsem, *, layer):
    b = pl.program_id(0)
    live = b < meta_ref[2, 0]
    expert = meta_ref[0, b]
    prev = meta_ref[0, jnp.maximum(b - 1, 0)]

    def fetch(e, slot):
        return [pltpu.make_async_copy(src.at[layer, e], dst.at[slot], wsem.at[slot])
                for src, dst in ((wg_ref, wg_f), (wu_ref, wu_f), (wd_ref, wd_f))]

    def first_block(slot):
        @pl.when(b == 0)
        def _():
            for cp in fetch(expert, slot):
                cp.start()

        for cp in fetch(expert, slot):
            cp.wait()
        wgu_b[:, 0:EXPERT_FF] = wg_f[slot].astype(BF16)
        wgu_b[:, EXPERT_FF:2 * EXPERT_FF] = wu_f[slot].astype(BF16)
        wd_b[...] = wd_f[slot].astype(BF16)
        nxt = meta_ref[4, b]

        @pl.when(nxt < N_EXPERTS)
        def _():
            for cp in fetch(nxt, 1 - slot):
                cp.start()

    changed = live & ((b == 0) | (expert != prev))
    odd = (meta_ref[3, b] & 1) == 1

    @pl.when(changed & jnp.logical_not(odd))
    def _():
        first_block(0)

    @pl.when(changed & odd)
    def _():
        first_block(1)

    @pl.when(live)
    def _():
        x = _unpack_bf16_pairs(x_ref[...])
        gu = jnp.dot(x, wgu_b[...], preferred_element_type=F32)
        act = (_silu(gu[:, 0:EXPERT_FF]) * gu[:, EXPERT_FF:2 * EXPERT_FF]).astype(BF16)
        y = jnp.dot(act, wd_b[...], preferred_element_type=F32)
        o_ref[...] = _pack_bf16_pairs(y)


def _ffn(meta, xs, wg, wu, wd, layer, nb):
    def blk(b, m):
        return (jnp.maximum(jnp.minimum(b, m[2, 0] - 1), 0), 0)

    grid_spec = pltpu.PrefetchScalarGridSpec(
        num_scalar_prefetch=1,
        grid=(nb,),
        in_specs=[pl.BlockSpec((MOE_ROWS, XS_WIDTH), blk)] + [pl.BlockSpec(memory_space=pl.ANY)] * 3,
        out_specs=pl.BlockSpec((MOE_ROWS, PACKED), blk),
        scratch_shapes=[pltpu.VMEM((D_MODEL, 2 * EXPERT_FF), BF16), pltpu.VMEM((EXPERT_FF, D_MODEL), BF16),
                        pltpu.VMEM((2, D_MODEL, EXPERT_FF), F32), pltpu.VMEM((2, D_MODEL, EXPERT_FF), F32),
                        pltpu.VMEM((2, EXPERT_FF, D_MODEL), F32), pltpu.SemaphoreType.DMA((2,))],
    )
    return pl.pallas_call(
        functools.partial(_ffn_kernel, layer=layer),
        grid_spec=grid_spec,
        out_shape=jax.ShapeDtypeStruct((nb * MOE_ROWS, PACKED), U32),
        compiler_params=_cparams("arbitrary"),
        name="moe_experts",
    )(meta, xs, wg, wu, wd)


def _gather_kernel(tab_ref, ctab_ref, lcol_ref, x_ref, g_ref, ys_ref, o_ref, buf_ref, sem, *, tm, final):
    i = pl.program_id(0)

    last = pl.num_programs(0) - 1

    def fetch(tile, slot, action):
        _chunk_copies((tab_ref, ctab_ref), tile, buf_ref.at[slot], ys_ref, sem.at[slot], False, action)

    @pl.when(i == 0)
    def _():
        buf_ref[...] = jnp.zeros_like(buf_ref)
        fetch(0, 0, "start")

    col = _lane_iota((tm, LOCAL_ROWS)).astype(F32)
    for slot in range(2):
        tile = 2 * i + slot
        tok = slice(slot * tm, (slot + 1) * tm)
        if slot == 0:
            fetch(tile + 1, 1, "start")
        else:
            @pl.when(i < last)
            def _():
                fetch(tile + 1, 0, "start")

        lc = lcol_ref[tok, :]
        pick0 = jnp.where(col == lc[:, 0:1], 1.0, 0.0).astype(BF16)
        pick1 = jnp.where(col == lc[:, 1:2], 1.0, 0.0).astype(BF16)
        fetch(tile, slot, "wait")
        y = _unpack_bf16_pairs(buf_ref[slot])
        both = jnp.dot(jnp.concatenate([pick0, pick1], axis=0), y, preferred_element_type=F32)
        x = x_ref[tok, :] + lc[:, 2:3] * both[0:tm] + lc[:, 3:4] * both[tm:2 * tm]
        o_ref[tok, :] = _rms(x, g_ref[...]) if final else x


def _gather(tab, ctab, lcol, x2, g, ys, tm, final):
    t = x2.shape[0]
    kern = functools.partial(_gather_kernel, tm=tm, final=final)
    grid_spec = pltpu.PrefetchScalarGridSpec(
        num_scalar_prefetch=2,
        grid=(t // (2 * tm),),
        in_specs=[pl.BlockSpec((2 * tm, LANES), lambda i, *_: (i, 0)),
                  pl.BlockSpec((2 * tm, D_MODEL), lambda i, *_: (i, 0)),
                  pl.BlockSpec((1, D_MODEL), lambda i, *_: (0, 0)),
                  pl.BlockSpec(memory_space=pl.ANY)],
        out_specs=pl.BlockSpec((2 * tm, D_MODEL), lambda i, *_: (i, 0)),
        scratch_shapes=[pltpu.VMEM((2, LOCAL_ROWS, PACKED), U32), pltpu.SemaphoreType.DMA((2,))],
    )
    return pl.pallas_call(
        kern,
        grid_spec=grid_spec,
        out_shape=jax.ShapeDtypeStruct((t, D_MODEL), F32),
        compiler_params=_cparams("arbitrary"),
        name="moe_combine",
    )(tab, ctab, lcol, x2, g, ys)


def _pad_rows(a, rows=8):
    return jnp.zeros((rows, a.shape[-1]), F32).at[:a.shape[0]].set(a.astype(F32))


def _arrange_mla(w_uq, w_ukv):
    half = MLA_ROPE // 2
    qd = MLA_NOPE + MLA_ROPE
    wq, wqs, wk, wv = [], [], [], []
    zq = jnp.zeros((MLA_Q_LORA, LANES - qd), w_uq.dtype)
    zk = jnp.zeros((MLA_KV_LORA, LANES - MLA_NOPE), w_ukv.dtype)
    for h in range(N_HEADS):
        q = w_uq[:, h * qd:(h + 1) * qd]
        nope, rope = q[:, :MLA_NOPE], q[:, MLA_NOPE:]
        wq.append(jnp.concatenate([nope, rope, zq], axis=1))
        wqs.append(jnp.concatenate([jnp.zeros_like(nope), -rope[:, half:], rope[:, :half], zq], axis=1))
        kv = w_ukv[:, h * 2 * MLA_NOPE:(h + 1) * 2 * MLA_NOPE]
        wk.append(jnp.concatenate([kv[:, :MLA_NOPE], zk], axis=1))
        wv.append(kv[:, MLA_NOPE:])
    cat = lambda xs: jnp.concatenate(xs, axis=1).astype(BF16)
    return cat(wq), cat(wqs), cat(wk), cat(wv).T


def kernel(x, positions, norm_mix, w_in, conv_a, fox_forget_bias, ssm_conv_w, ssm_conv_b, ssm_dt_bias,
           ssm_a_log, ssm_d, ssm_norm, mla_q_norm, mla_kv_norm, mla_w_uq, mla_w_ukv, w_out, norm_ffn,
           router_group_w, router_group_b, router_expert_w, router_expert_b, expert_w_gate, expert_w_up,
           expert_w_down, norm_final):
    batch, seq, d = x.shape
    t = batch * seq
    depth = w_in.shape[0]
    tm = min(ROW_TILE, seq)
    tq = min(ATTN_TQ, seq)
    tmd = MOE_TILE
    assert d == D_MODEL and seq % tm == 0 and seq % tq == 0 and tq % SSM_CHUNK == 0, (x.shape,)
    assert seq // tq <= 8 and tm % (LANES * ROPE_PACK) == 0, (seq, tm)
    assert t % (2 * tmd) == 0 and t // tmd <= LANES and tm % tmd == 0, (t, tmd)
    max_rows = 2 * t + (CHUNK - 1) * N_EXPERTS * (t // tmd) + N_EXPERTS * (MOE_ROWS - 1)
    nb = -(-max_rows // MOE_ROWS)
    nbp = -(-nb // LANES) * LANES

    xf = x.reshape(t, d)
    cos, sin = _rope_tables(positions, tm)
    w_in_t = jnp.swapaxes(w_in, 1, 2)

    for l in range(depth):
        wq, wqs, wk, wv = _arrange_mla(mla_w_uq[l], mla_w_ukv[l])
        ya, pb, pc, misc, fox_vt, q, k, v = _inproj(
            xf, norm_mix[l][None, :], w_in_t, l, _pad_rows(conv_a[l]), cos, sin, mla_q_norm[l][None, :],
            mla_kv_norm[l][None, :], jnp.concatenate([wq, wqs], axis=1), wk, wv, tm, seq)

        sp = jnp.zeros((8, LANES), F32)
        sp = sp.at[0, MISC_F:MISC_F + N_HEADS].set(fox_forget_bias[l])
        sp = sp.at[0, MISC_DT:MISC_DT + N_HEADS].set(ssm_dt_bias[l])
        sp = sp.at[1, MISC_DT:MISC_DT + N_HEADS].set(ssm_a_log[l])
        col, rows, fox_q, fox_k, tref = _scalar_prep(misc, sp, pb, batch, seq, tq)

        yb = _attention(fox_q, fox_k, fox_vt, tref, batch, seq, tq, "fox_attention")
        conv_wb = _pad_rows(jnp.concatenate([ssm_conv_w[l], ssm_conv_b[l][None, :]], axis=0))
        ssd_par = _pad_rows(jnp.stack([jnp.repeat(ssm_d[l], HEAD_DIM), ssm_norm[l]]))
        yc = _ssd_mixer(pc, col, rows, conv_wb, ssd_par, batch, seq)
        yd = _attention(q, k, v, None, batch, seq, tq, "mla_attention")

        pad = jnp.zeros((d, LANES - N_EXPERTS - N_EXPERT_GROUPS), F32)
        wr = jnp.concatenate([router_expert_w[l], router_group_w[l], pad], axis=1)
        wr_hi = wr.astype(BF16)
        wr = jnp.concatenate([wr_hi, (wr - wr_hi.astype(F32)).astype(BF16)], axis=1)
        br = jnp.concatenate([router_expert_b[l], router_group_b[l], pad[0]])[None, :]
        x2, h2, rrow, cnt = _outproj(xf, ya, yb, yc, yd, w_out[l].astype(BF16), norm_ffn[l][None, :], wr, br,
                                     tm, tmd)

        lrow, lcol, tab, ctab, meta = _route(rrow, cnt, tmd, nbp)
        xs = _scatter(tab, ctab, meta, lrow, h2, tmd, nb)
        ys = _ffn(meta, xs, expert_w_gate, expert_w_up, expert_w_down, l, nb)
        final = l == depth - 1
        xf = _gather(tab, ctab, lcol, x2, norm_final[None, :], ys, tmd, final)

    return xf.reshape(batch, seq, d)
```

```python
import functools
import math

import jax
import jax.numpy as jnp
import numpy as np
from jax import lax
from jax.experimental import pallas as pl
from jax.experimental.pallas import tpu as pltpu

F32 = jnp.float32
BF16 = jnp.bfloat16
I32 = jnp.int32

LANES = 128
VMEM_LIMIT_BYTES = 56 * 1024 * 1024

D_MODEL = 1024
RMS_EPS = 1e-6
LOG2E = math.log2(math.e)
GROUP_WIDTH = 256
HEAD_DIM = 64
N_HEADS = 4

CONV_A_WIDTH = 3
SSM_CONV = 4
SSM_STATE = 64
SSM_CHUNK = 256

MLA_NOPE = 64
MLA_ROPE = 32
MLA_Q_LORA = 256
MLA_KV_LORA = 128
ROPE_BASE = 10000.0
MLA_CHUNK = 64
ATTN_TQ = 512
ROW_TILE = 1024

N_EXPERT_GROUPS = 4
EXPERTS_PER_GROUP = 8
N_EXPERTS = 32
EXPERT_FF = 256
ROUTER_ROWS = 48
MOE_ROWS = 512
ZERO_ROWS = 256

SEG_A = (0, 768)
SEG_B = (768, 1280)
SEG_C = (1280, 2048)
SEG_D = (2048, 2432)
SEG_M = (2432, 2560)
SEG_M2 = (2560, 2688)
IN_COLS_PADDED = 2688
HEAD_PAD = N_HEADS * LANES
AUG_LANE = HEAD_DIM
MISC_F = 0
MISC_DT = 4
MISC_ROPE = 64
COL_CUMF = 0
COL_DT = 4
COL_ACUM = 8
N_SCALAR_ROWS = 16


def _cparams(*sem):
    return pltpu.CompilerParams(dimension_semantics=sem, vmem_limit_bytes=VMEM_LIMIT_BYTES)


def _lane_iota(shape):
    return lax.broadcasted_iota(I32, shape, len(shape) - 1)


def _row_iota(shape):
    return lax.broadcasted_iota(I32, shape, 0)


def _rms(x, g):
    ms = jnp.mean(x * x, axis=-1, keepdims=True)
    return x * lax.rsqrt(ms + RMS_EPS) * g


def _silu(x):
    return x / (1.0 + jnp.exp(-x))


def _softplus(x):
    return jnp.maximum(x, 0.0) + jnp.log(1.0 + jnp.exp(-jnp.abs(x)))


def _shift_rows(x, k):
    rolled = pltpu.roll(x, k, 0)
    return jnp.where(_row_iota(x.shape) >= k, rolled, 0.0)


def _rope_kernel(pos_ref, freq_ref, cos_ref, sin_ref):
    q = pos_ref.shape[2]
    lane = _lane_iota((q, LANES))
    group = lane >> int(math.log2(MLA_ROPE))
    rows = jnp.concatenate([pos_ref[0], jnp.zeros((8 - ROPE_PACK, q), F32)], axis=0)
    cols = jnp.concatenate([rows] * (LANES // 8), axis=0).T
    pos = cols[:, ROPE_PACK - 1:ROPE_PACK]
    for k in range(ROPE_PACK - 2, -1, -1):
        pos = jnp.where(group == k, cols[:, k:k + 1], pos)
    ang = pos * freq_ref[...]
    c = jnp.cos(ang)
    s = jnp.sin(ang)
    rope = (lane >= MISC_ROPE) & (lane < MISC_ROPE + MLA_ROPE)
    for k in range(ROPE_PACK):
        shift = (MISC_ROPE - MLA_ROPE * k) % LANES
        ck = pltpu.roll(c, shift, 1) if shift else c
        sk = pltpu.roll(s, shift, 1) if shift else s
        cos_ref[k * q:(k + 1) * q, :] = jnp.where(rope, ck, jnp.where(lane < MISC_ROPE, 1.0, 0.0))
        sin_ref[k * q:(k + 1) * q, :] = jnp.where(rope, sk, 0.0)


ROPE_PACK = LANES // MLA_ROPE


def _rope_tables(positions, tm):
    t = positions.size
    pos = positions.astype(F32).reshape(t // tm, ROPE_PACK, tm // ROPE_PACK)
    inv = ROPE_BASE ** (-np.arange(0, MLA_ROPE, 2, dtype=np.float32) / MLA_ROPE)
    freq = np.tile(np.concatenate([inv, inv]), ROPE_PACK)[None, :].astype(np.float32)
    return pl.pallas_call(
        _rope_kernel,
        grid=(t // tm,),
        in_specs=[pl.BlockSpec((1, ROPE_PACK, tm // ROPE_PACK), lambda i: (i, 0, 0)),
                  pl.BlockSpec((1, LANES), lambda i: (0, 0))],
        out_specs=[pl.BlockSpec((tm, LANES), lambda i: (i, 0))] * 2,
        out_shape=[jax.ShapeDtypeStruct((t, LANES), F32)] * 2,
        compiler_params=_cparams("parallel"),
        name="rope_tables",
    )(pos, jnp.asarray(freq))


def _arrange_w_in_kernel(wt_ref, w_ref, wvt_ref):
    gw = GROUP_WIDTH
    rows = lambda lo, hi: wt_ref[0, lo:hi, :]

    def put(seg0, piece_t):
        w_ref[:, seg0:seg0 + piece_t.shape[0]] = piece_t.T.astype(BF16)

    for j in range(3):
        put(SEG_A[0] + j * gw, rows(j * gw, (j + 1) * gw))
    put(SEG_B[0], rows(3 * gw, 4 * gw) * (HEAD_DIM ** -0.5 * LOG2E))
    put(SEG_B[0] + gw, rows(4 * gw, 5 * gw))
    wvt_ref[...] = rows(5 * gw, 6 * gw).astype(BF16)
    c0 = 6 * gw
    win = pltpu.roll(rows(c0, c0 + 3 * gw + 8), 3 * gw + 8 - N_HEADS, 0)
    for j in range(3):
        put(SEG_C[0] + j * gw, win[j * gw:(j + 1) * gw])
    d0 = c0 + 3 * gw + 8
    put(SEG_D[0], rows(d0, d0 + MLA_Q_LORA))
    put(SEG_D[0] + MLA_Q_LORA, rows(d0 + MLA_Q_LORA, d0 + MLA_Q_LORA + MLA_KV_LORA))
    kr0 = d0 + MLA_Q_LORA + MLA_KV_LORA
    half = MLA_ROPE // 2
    first8 = jnp.where(_row_iota((8, D_MODEL)) < N_HEADS, rows(c0, c0 + 8), rows(d0 - 8, d0))
    zeros = lambda n: jnp.zeros((n, D_MODEL), F32)
    misc_t = jnp.concatenate([first8, zeros(MISC_ROPE - 8), rows(kr0, kr0 + MLA_ROPE),
                              zeros(LANES - MISC_ROPE - MLA_ROPE)], axis=0)
    misc2_t = jnp.concatenate([zeros(MISC_ROPE), -rows(kr0 + half, kr0 + MLA_ROPE), rows(kr0, kr0 + half),
                               zeros(LANES - MISC_ROPE - MLA_ROPE)], axis=0)
    put(SEG_M[0], misc_t)
    put(SEG_M2[0], misc2_t)


def _inproj_kernel(x_ref, g_ref, wt_ref, cw_ref, cos_ref, sin_ref, nq_ref, nkv_ref, wq2_ref, wk_ref,
                   wvt2_ref, oa, ob, oc, om, ovt, q_ref, k_ref, vt_ref, w_ref, wvt_ref, halo_ref,
                   *, tm, tiles_per_seq):
    @pl.when(pl.program_id(0) == 0)
    def _():
        _arrange_w_in_kernel(wt_ref, w_ref, wvt_ref)

    h = _rms(x_ref[...], g_ref[...]).astype(BF16)
    for o, (lo, hi) in ((ob, SEG_B), (oc, SEG_C)):
        o[...] = jnp.dot(h, w_ref[:, lo:hi], preferred_element_type=F32).astype(o.dtype)

    gw = GROUP_WIDTH
    pa = jnp.dot(h, w_ref[:, SEG_A[0]:SEG_A[1]], preferred_element_type=F32)
    cv = pa[:, gw:2 * gw] * pa[:, 2 * gw:3 * gw]

    @pl.when(pl.program_id(0) % tiles_per_seq == 0)
    def _():
        halo_ref[...] = jnp.zeros_like(halo_ref)

    halo = halo_ref[...]
    row8 = _row_iota(halo.shape)
    acc = cv * cw_ref[CONV_A_WIDTH - 1:CONV_A_WIDTH, :]
    for k in range(1, CONV_A_WIDTH):
        shifted = pltpu.roll(cv, k, 0)
        top = jnp.where(row8 < k, pltpu.roll(halo, k, 0), shifted[0:8])
        shifted = jnp.concatenate([top, shifted[8:]], axis=0)
        acc = acc + shifted * cw_ref[CONV_A_WIDTH - 1 - k:CONV_A_WIDTH - k, :]
    halo_ref[...] = cv[tm - 8:tm]
    oa[...] = (pa[:, 0:gw] * acc).astype(oa.dtype)
    ovt[...] = lax.dot_general(wvt_ref[...], h, (((1,), (1,)), ((), ())),
                               preferred_element_type=F32).astype(ovt.dtype)
    misc = jnp.dot(h, w_ref[:, SEG_M[0]:SEG_M[1]], preferred_element_type=F32)
    misc2 = jnp.dot(h, w_ref[:, SEG_M2[0]:SEG_M2[1]], preferred_element_type=F32)
    om[...] = misc

    pd = jnp.dot(h, w_ref[:, SEG_D[0]:SEG_D[1]], preferred_element_type=F32)
    cq = _rms(pd[:, 0:MLA_Q_LORA], nq_ref[...]).astype(BF16)
    ckv = _rms(pd[:, MLA_Q_LORA:MLA_Q_LORA + MLA_KV_LORA], nkv_ref[...]).astype(BF16)
    cos = cos_ref[...]
    sin = sin_ref[...]
    cos4 = jnp.concatenate([cos] * N_HEADS, axis=1)
    sin4 = jnp.concatenate([sin] * N_HEADS, axis=1)
    scale = (MLA_NOPE + MLA_ROPE) ** -0.5 * LOG2E
    q2 = jnp.dot(cq, wq2_ref[...], preferred_element_type=F32)
    q_ref[...] = ((q2[:, 0:HEAD_PAD] * cos4 + q2[:, HEAD_PAD:2 * HEAD_PAD] * sin4) * scale).astype(q_ref.dtype)
    lane = _lane_iota(cos.shape)
    rope = (lane >= MISC_ROPE) & (lane < MISC_ROPE + MLA_ROPE)
    kr = jnp.where(rope, misc * cos + misc2 * sin, 0.0)
    k = jnp.dot(ckv, wk_ref[...], preferred_element_type=F32)
    k_ref[...] = (k + jnp.concatenate([kr] * N_HEADS, axis=1)).astype(k_ref.dtype)
    vt_ref[...] = lax.dot_general(wvt2_ref[...], ckv, (((1,), (1,)), ((), ())),
                                  preferred_element_type=F32).astype(vt_ref.dtype)


def _inproj(x, g, w_in_t, layer, conv_w, cos, sin, nq, nkv, wq2, wk, wvt2, tm, seq):
    t = x.shape[0]
    once = lambda a: pl.BlockSpec((1,) + a.shape[1:], lambda i: (layer, 0, 0), pipeline_mode=pl.Buffered(1))
    full = lambda a: pl.BlockSpec(a.shape, lambda i: (0, 0))
    tile = lambda wd: pl.BlockSpec((tm, wd), lambda i: (i, 0))
    cols = lambda: pl.BlockSpec((GROUP_WIDTH, tm), lambda i: (0, i))
    seg = lambda s: s[1] - s[0]
    return pl.pallas_call(
        functools.partial(_inproj_kernel, tm=tm, tiles_per_seq=seq // tm),
        grid=(t // tm,),
        in_specs=[tile(D_MODEL), full(g), once(w_in_t), full(conv_w), tile(LANES), tile(LANES),
                  full(nq), full(nkv), full(wq2), full(wk), full(wvt2)],
        out_specs=[tile(GROUP_WIDTH), tile(seg(SEG_B)), tile(seg(SEG_C)), tile(LANES), cols(),
                   tile(HEAD_PAD), tile(HEAD_PAD), cols()],
        out_shape=[jax.ShapeDtypeStruct((t, GROUP_WIDTH), BF16), jax.ShapeDtypeStruct((t, seg(SEG_B)), BF16),
                   jax.ShapeDtypeStruct((t, seg(SEG_C)), BF16), jax.ShapeDtypeStruct((t, LANES), F32),
                   jax.ShapeDtypeStruct((GROUP_WIDTH, t), BF16),
                   jax.ShapeDtypeStruct((t, HEAD_PAD), BF16), jax.ShapeDtypeStruct((t, HEAD_PAD), BF16),
                   jax.ShapeDtypeStruct((GROUP_WIDTH, t), BF16)],
        scratch_shapes=[pltpu.VMEM((D_MODEL, IN_COLS_PADDED), BF16), pltpu.VMEM((GROUP_WIDTH, D_MODEL), BF16),
                        pltpu.VMEM((8, GROUP_WIDTH), F32)],
        compiler_params=_cparams("arbitrary"),
        name="inproj",
    )(x, g, w_in_t, conv_w, cos, sin, nq, nkv, wq2, wk, wvt2)


def _scalar_prep_kernel(m_ref, p_ref, qk_ref, place_ref,
                        col_ref, row_ref, qa_ref, ka_ref, tref_ref, *, tq):
    s = m_ref.shape[0]
    tref_ref[...] = jnp.zeros_like(tref_ref)
    m = m_ref[...]
    bias = p_ref[0:1, :]
    a_log = p_ref[1:2, :]
    lane = _lane_iota(m.shape)
    z = m + bias
    logf = jnp.minimum(z, 0.0) - jnp.log(1.0 + jnp.exp(-jnp.abs(z)))
    dt = _softplus(z)
    a = dt * (-jnp.exp(a_log))
    is_f = lane < MISC_DT
    is_dt = (lane >= MISC_DT) & (lane < MISC_DT + N_HEADS)
    v = jnp.where(is_f, logf, jnp.where(is_dt, a, 0.0))
    r = _row_iota((SSM_CHUNK, SSM_CHUNK))
    c = _lane_iota((SSM_CHUNK, SSM_CHUNK))
    tril = jnp.where(r >= c, 1.0, 0.0).astype(BF16)
    carry = jnp.zeros((1, LANES), F32)
    lane_1 = _lane_iota((1, LANES))
    lane_b = _lane_iota((SSM_CHUNK, LANES))
    for ci in range(s // SSM_CHUNK):
        rest = v[ci * SSM_CHUNK:(ci + 1) * SSM_CHUNK]
        cs = jnp.zeros((SSM_CHUNK, LANES), F32)
        for _ in range(3):
            term = rest.astype(BF16)
            cs = cs + jnp.dot(tril, term, preferred_element_type=F32)
            rest = rest - term.astype(F32)
        cs = cs + jnp.where(lane_1 < MISC_DT, carry, 0.0)
        carry = cs[SSM_CHUNK - 1:SSM_CHUNK]
        acum = pltpu.roll(cs, COL_ACUM - MISC_DT, 1)
        out = jnp.where(lane_b < MISC_DT, cs * LOG2E,
                        jnp.where(lane_b < COL_ACUM, dt[ci * SSM_CHUNK:(ci + 1) * SSM_CHUNK],
                                  jnp.where(lane_b < COL_ACUM + N_HEADS, acum, 0.0)))
        rows = slice(ci * SSM_CHUNK, (ci + 1) * SSM_CHUNK)
        col_ref[rows, :] = out
        row_ref[0, :, rows] = out.T[:N_SCALAR_ROWS]
    for ci in range(s // SSM_CHUNK):
        rows = slice(ci * SSM_CHUNK, (ci + 1) * SSM_CHUNK)
        ti = (ci * SSM_CHUNK) // tq
        tile_ref = col_ref[ti * tq:ti * tq + 1, :]
        if (ci * SSM_CHUNK) % tq == 0:
            tref_ref[0, ti:ti + 1, :] = tile_ref
        c = col_ref[rows, :] - tile_ref
        c_hi = c.astype(BF16).astype(F32)
        r1 = c - c_hi
        c_mid = r1.astype(BF16).astype(F32)
        c_lo = r1 - c_mid
        terms = jnp.where(lane_b < N_HEADS, c_hi,
                          jnp.where(lane_b < 2 * N_HEADS, pltpu.roll(c_mid, N_HEADS, 1),
                                    jnp.where(lane_b < 3 * N_HEADS, pltpu.roll(c_lo, 2 * N_HEADS, 1),
                                              jnp.where(lane_b == ONE_LANE, 1.0, 0.0)))).astype(BF16)
        for side, o_ref in enumerate((qa_ref, ka_ref)):
            for j in range(N_HEADS // 2):
                pair = qk_ref[rows, side * GROUP_WIDTH + j * LANES:
                              side * GROUP_WIDTH + (j + 1) * LANES]
                lhs = jnp.concatenate([pair, terms], axis=1)
                o_ref[rows, 2 * j * LANES:(2 * j + 2) * LANES] = jnp.dot(
                    lhs, place_ref[side, j], preferred_element_type=F32).astype(o_ref.dtype)


AUG_TERMS = 6
ONE_LANE = LANES - 1


def _fox_placement():
    assert COL_CUMF == 0
    place = np.zeros((2, N_HEADS // 2, 2 * LANES, 2 * LANES), np.float32)
    for h in range(N_HEADS):
        j, half = divmod(h, 2)
        o = half * LANES
        one = LANES + ONE_LANE
        for side in range(2):
            for d in range(HEAD_DIM):
                place[side, j, half * HEAD_DIM + d, o + d] = 1.0
        for term in range(3):
            src = LANES + term * N_HEADS + h
            place[0, j, src, o + AUG_LANE + term] = 1.0
            place[0, j, one, o + AUG_LANE + 3 + term] = 1.0
            place[1, j, one, o + AUG_LANE + term] = 1.0
            place[1, j, src, o + AUG_LANE + 3 + term] = -1.0
    return jnp.asarray(place, BF16)


def _scalar_prep(misc, params, qk, batch, seq, tq):
    place = _fox_placement()
    full = lambda a: pl.BlockSpec(a.shape, lambda b: (0,) * a.ndim)
    return pl.pallas_call(
        functools.partial(_scalar_prep_kernel, tq=tq),
        grid=(batch,),
        in_specs=[pl.BlockSpec((seq, LANES), lambda b: (b, 0)),
                  pl.BlockSpec((8, LANES), lambda b: (0, 0)),
                  pl.BlockSpec((seq, 2 * GROUP_WIDTH), lambda b: (b, 0)),
                  full(place)],
        out_specs=[pl.BlockSpec((seq, LANES), lambda b: (b, 0)),
                   pl.BlockSpec((1, N_SCALAR_ROWS, seq), lambda b: (b, 0, 0)),
                   pl.BlockSpec((seq, HEAD_PAD), lambda b: (b, 0)),
                   pl.BlockSpec((seq, HEAD_PAD), lambda b: (b, 0)),
                   pl.BlockSpec((1, 8, LANES), lambda b: (b, 0, 0))],
        out_shape=[jax.ShapeDtypeStruct((batch * seq, LANES), F32),
                   jax.ShapeDtypeStruct((batch, N_SCALAR_ROWS, seq), F32),
                   jax.ShapeDtypeStruct((batch * seq, HEAD_PAD), BF16),
                   jax.ShapeDtypeStruct((batch * seq, HEAD_PAD), BF16),
                   jax.ShapeDtypeStruct((batch, 8, LANES), F32)],
        compiler_params=_cparams("parallel"),
        name="scalar_prep",
    )(misc, params, qk, place)


def _pair_lanes(col, base, shape):
    lane = _lane_iota(shape)
    return jnp.where(lane < HEAD_DIM, col[:, base:base + 1], col[:, base + 1:base + 2])


def _ssd_kernel(p_ref, col_ref, row_ref, cw_ref, par_ref, o_ref, u_ref):
    s = p_ref.shape[0]
    q = SSM_CHUNK
    gw = GROUP_WIDTH
    xbc = p_ref[:, gw:3 * gw].astype(F32)
    acc = xbc * cw_ref[SSM_CONV - 1:SSM_CONV, :]
    for k in range(1, SSM_CONV):
        acc = acc + _shift_rows(xbc, k) * cw_ref[SSM_CONV - 1 - k:SSM_CONV - k, :]
    u_ref[...] = _silu(acc + cw_ref[SSM_CONV:SSM_CONV + 1, :])

    d_skip = par_ref[0:1, :]
    norm_g = par_ref[1:2, :]
    lane_q = _lane_iota((q, LANES))
    low = lane_q < HEAD_DIM
    tri = _row_iota((q, q)) >= _lane_iota((q, q))

    def chunk(ci, states):
        rows = pl.ds(ci * q, q)
        u = u_ref[rows, :]
        col = col_ref[rows, :]
        bm = u[:, gw:gw + LANES]
        cm = u[:, gw + LANES:gw + 2 * LANES]
        z = p_ref[rows, 0:gw].astype(F32)
        new_states = []
        ys = []
        for g in range(2):
            sel = low if g == 0 else jnp.logical_not(low)
            cg = jnp.where(sel, cm, 0.0).astype(BF16)
            bg = jnp.where(sel, bm, 0.0)
            gmat = lax.dot_general(cg, bm.astype(BF16), (((1,), (1,)), ((), ())),
                                   preferred_element_type=F32)
            xs = u[:, g * LANES:(g + 1) * LANES]
            dt2 = _pair_lanes(col, COL_DT + 2 * g, (q, LANES))
            ac2 = _pair_lanes(col, COL_ACUM + 2 * g, (q, LANES))
            xdt = xs * dt2
            xdt_b = xdt.astype(BF16)
            st = states[g]
            y_off = jnp.dot(cg, st.astype(BF16), preferred_element_type=F32) * jnp.exp(ac2)
            halves = []
            for hh in range(2):
                h = 2 * g + hh
                ac_col = col[:, COL_ACUM + h:COL_ACUM + h + 1]
                ac_row = row_ref[0, COL_ACUM + h:COL_ACUM + h + 1, rows]
                decay = jnp.exp(jnp.where(tri, ac_col - ac_row, -1e30))
                mm = (gmat * decay).astype(BF16)
                halves.append(jnp.dot(mm, xdt_b, preferred_element_type=F32))
            y = jnp.where(low, halves[0], halves[1]) + y_off + d_skip[:, g * LANES:(g + 1) * LANES] * xs
            ys.append(y)
            ac_last = ac2[q - 1:q, :]
            w_end = jnp.exp(ac_last - ac2)
            xw = (xdt * w_end).astype(BF16)
            upd = jnp.dot(bg.T.astype(BF16), xw, preferred_element_type=F32)
            new_states.append(st * jnp.exp(ac_last) + upd)
        yfull = jnp.concatenate(ys, axis=1) * _silu(z)
        o_ref[rows, :] = _rms(yfull, norm_g).astype(o_ref.dtype)
        return tuple(new_states)

    init = (jnp.zeros((LANES, LANES), F32), jnp.zeros((LANES, LANES), F32))
    states = init
    for ci in range(s // q):
        states = chunk(ci, states)


def _ssd_mixer(pc, col, rows, conv_wb, par, batch, seq):
    gw = GROUP_WIDTH
    return pl.pallas_call(
        _ssd_kernel,
        grid=(batch,),
        in_specs=[pl.BlockSpec((seq, 3 * gw), lambda b: (b, 0)),
                  pl.BlockSpec((seq, LANES), lambda b: (b, 0)),
                  pl.BlockSpec((1, N_SCALAR_ROWS, seq), lambda b: (b, 0, 0)),
                  pl.BlockSpec((8, 2 * gw), lambda b: (0, 0)),
                  pl.BlockSpec((8, gw), lambda b: (0, 0))],
        out_specs=pl.BlockSpec((seq, gw), lambda b: (b, 0)),
        out_shape=jax.ShapeDtypeStruct((batch * seq, gw), BF16),
        scratch_shapes=[pltpu.VMEM((seq, 2 * gw), F32)],
        compiler_params=_cparams("parallel"),
        name="ssd_mixer",
    )(pc, col, rows, conv_wb, par)


def _attn_kernel(*refs, fox, tq, nq):
    if fox:
        tref_ref, q_ref, k_ref, vt_ref, o_ref = refs
    else:
        q_ref, k_ref, vt_ref, o_ref = refs
        tref_ref = None
    b = pl.program_id(0)
    i = pl.program_id(1)
    key = _row_iota((tq, tq))
    qry = _lane_iota((tq, tq))
    if fox:
        allowed = key <= qry
    else:
        shift = int(math.log2(MLA_CHUNK))
        allowed = (key >> shift) <= (qry >> shift)
    qs = [q_ref[:, h * LANES:(h + 1) * LANES] for h in range(N_HEADS)]
    ones_rows = jnp.ones((16, tq), BF16)

    def step(j, masked, carry):
        rk = pl.ds(j * tq, tq)
        scores = [lax.dot_general(k_ref[rk, h * LANES:(h + 1) * LANES], qs[h], (((1,), (1,)), ((), ())),
                                  preferred_element_type=F32) for h in range(N_HEADS)]
        probs = []
        for h in range(N_HEADS):
            m, l, _ = carry[h]
            s = scores[h]
            if masked:
                s = jnp.where(allowed, s, -1e30)
            delta = (tref_ref[b, i, h] - tref_ref[b, j, h]) if fox else 0.0
            m_new = jnp.maximum(m, jnp.max(s, axis=0, keepdims=True) + delta)
            alpha = jnp.exp2(m - m_new)
            p = jnp.exp2(s - (m_new - delta))
            probs.append((m_new, alpha, p.astype(BF16)))
        new = []
        for h in range(N_HEADS):
            pair = h // 2
            m_new, alpha, p = probs[h]
            lhs = jnp.concatenate([vt_ref[pair * LANES:(pair + 1) * LANES, rk], ones_rows], axis=0)
            pv = jnp.dot(lhs, p, preferred_element_type=F32)
            new.append((m_new, alpha * carry[h][1] + pv[LANES:LANES + 1], alpha * carry[h][2] + pv[0:LANES]))
        return tuple(new)

    init = tuple((jnp.full((1, tq), -1e30, F32), jnp.zeros((1, tq), F32), jnp.zeros((LANES, tq), F32))
                 for _ in range(N_HEADS))
    top = _row_iota((LANES, tq)) < HEAD_DIM

    def sweep(n_before):
        carry = init
        for j in range(n_before):
            carry = step(j, False, carry)
        carry = step(n_before, True, carry)
        outs = [acc / l for (_, l, acc) in carry]
        o_t = jnp.concatenate([jnp.where(top, outs[0], outs[1]), jnp.where(top, outs[2], outs[3])], axis=0)
        o_ref[...] = o_t.T.astype(o_ref.dtype)

    for n_before in range(nq):
        pl.when(i == n_before)(functools.partial(sweep, n_before))


def _attention(q, k, vt, tref, batch, seq, tq, name):
    nq = seq // tq
    fox = tref is not None
    kern = functools.partial(_attn_kernel, fox=fox, tq=tq, nq=nq)
    grid_spec = pltpu.PrefetchScalarGridSpec(
        num_scalar_prefetch=1 if fox else 0,
        grid=(batch, nq),
        in_specs=[pl.BlockSpec((tq, HEAD_PAD), lambda b, i, *_: (b * nq + i, 0)),
                  pl.BlockSpec((seq, HEAD_PAD), lambda b, i, *_: (b, 0)),
                  pl.BlockSpec((GROUP_WIDTH, seq), lambda b, i, *_: (0, b))],
        out_specs=pl.BlockSpec((tq, GROUP_WIDTH), lambda b, i, *_: (b * nq + i, 0)),
    )
    args = ((tref,) if fox else ()) + (q, k, vt)
    return pl.pallas_call(
        kern,
        grid_spec=grid_spec,
        out_shape=jax.ShapeDtypeStruct((batch * seq, GROUP_WIDTH), BF16),
        compiler_params=_cparams("parallel", "arbitrary"),
        name=name,
    )(*args)


def _outproj_kernel(x_ref, ya, yb, yc, yd, w_ref, g_ref, wr_ref, br_ref,
                    x2_ref, h2_ref, rrow_ref, cnt_ref, *, tm, moe_tile):
    y = jnp.concatenate([ya[...], yb[...], yc[...], yd[...]], axis=1)
    x2 = x_ref[...] + jnp.dot(y, w_ref[...], preferred_element_type=F32)
    x2_ref[...] = x2
    h2 = _rms(x2, g_ref[...])
    h2_ref[...] = h2.astype(h2_ref.dtype)
    h_hi = h2.astype(BF16)
    h_lo = (h2 - h_hi.astype(F32)).astype(BF16)
    part = jnp.dot(h_hi, wr_ref[...], preferred_element_type=F32)
    logits = (part[:, 0:LANES] + part[:, LANES:2 * LANES]
              + jnp.dot(h_lo, wr_ref[:, 0:LANES], preferred_element_type=F32) + br_ref[...])
    lt = logits.T[0:ROUTER_ROWS]
    row = _row_iota(lt.shape)
    neg = -1e30
    big = 1 << 20
    gmask = (row >= N_EXPERTS) & (row < N_EXPERTS + N_EXPERT_GROUPS)
    gl = jnp.where(gmask, lt, neg)
    gmax = jnp.max(gl, axis=0, keepdims=True)
    gsum = jnp.sum(jnp.where(gmask, jnp.exp(gl - gmax), 0.0), axis=0, keepdims=True)
    g_w = 1.0 / gsum
    g_idx = jnp.min(jnp.where(gmask & (gl == gmax), row, big), axis=0, keepdims=True) - N_EXPERTS
    emask = (row < N_EXPERTS) & ((row >> int(math.log2(EXPERTS_PER_GROUP))) == g_idx)
    el = jnp.where(emask, lt, neg)
    e1v = jnp.max(el, axis=0, keepdims=True)
    esum = jnp.sum(jnp.where(emask, jnp.exp(el - e1v), 0.0), axis=0, keepdims=True)
    i1 = jnp.min(jnp.where(emask & (el == e1v), row, big), axis=0, keepdims=True)
    el2 = jnp.where(row == i1, neg, el)
    e2v = jnp.max(el2, axis=0, keepdims=True)
    i2 = jnp.min(jnp.where(emask & (row != i1) & (el2 == e2v), row, big), axis=0, keepdims=True)
    p1 = 1.0 / esum
    p2 = jnp.exp(e2v - e1v) / esum
    w1 = g_w * (p1 / (p1 + p2))
    w2 = g_w * (p2 / (p1 + p2))
    out_row = _row_iota(rrow_ref.shape)
    rrow_ref[...] = jnp.where(out_row == 0, i1.astype(F32),
                              jnp.where(out_row == 1, i2.astype(F32),
                                        jnp.where(out_row == 2, w1, jnp.where(out_row == 3, w2, 0.0))))
    step = pl.program_id(0)

    @pl.when(step == 0)
    def _():
        cnt_ref[...] = jnp.zeros_like(cnt_ref)

    chosen = jnp.where((row == i1) | (row == i2), 1.0, 0.0).astype(BF16)
    tiles = tm // moe_tile
    tile_of = (_row_iota((tm, LANES)) >> int(math.log2(moe_tile))) + step * tiles
    to_tile = jnp.where(_lane_iota((tm, LANES)) == tile_of, 1.0, 0.0).astype(BF16)
    counts = jnp.dot(chosen, to_tile, preferred_element_type=F32)
    cnt_ref[...] += counts[0:N_EXPERTS]


def _outproj(x, ya, yb, yc, yd, w, g, wr, br, tm, moe_tile):
    t = x.shape[0]
    full = lambda a: pl.BlockSpec(a.shape, lambda i: (0, 0))
    tile = lambda wd: pl.BlockSpec((tm, wd), lambda i: (i, 0))
    return pl.pallas_call(
        functools.partial(_outproj_kernel, tm=tm, moe_tile=moe_tile),
        grid=(t // tm,),
        in_specs=[tile(D_MODEL)] + [tile(GROUP_WIDTH)] * 4 + [full(w), full(g), full(wr), full(br)],
        out_specs=[tile(D_MODEL), tile(D_MODEL), pl.BlockSpec((8, tm), lambda i: (0, i)),
                   pl.BlockSpec((N_EXPERTS, LANES), lambda i: (0, 0))],
        out_shape=[jax.ShapeDtypeStruct((t, D_MODEL), F32), jax.ShapeDtypeStruct((t, D_MODEL), BF16),
                   jax.ShapeDtypeStruct((8, t), F32), jax.ShapeDtypeStruct((N_EXPERTS, LANES), F32)],
        compiler_params=_cparams("arbitrary"),
        name="outproj_router",
    )(x, ya, yb, yc, yd, w, g, wr, br)


MOE_TILE = 256
CHUNK = 8
LOCAL_ROWS = 2 * MOE_TILE + 256
PACKED = D_MODEL // 2
XS_WIDTH = PACKED
U32 = jnp.uint32


def _pack_bf16_pairs(x, exact=False):
    if not exact:
        x = x.astype(BF16).astype(F32)
    half = x.shape[1] // 2
    lo = lax.bitcast_convert_type(x[:, :half], U32)
    hi = lax.bitcast_convert_type(x[:, half:], U32)
    return hi | (lo >> 16)


def _unpack_bf16_pairs(words):
    lo = lax.bitcast_convert_type(words << 16, F32)
    hi = lax.bitcast_convert_type(words & U32(0xFFFF0000), F32)
    return jnp.concatenate([lo, hi], axis=1).astype(BF16)


COPY_ROWS = (2 * CHUNK, CHUNK)


def _route_kernel(r_ref, cnt_ref, lrow_ref, lcol_ref, tab_ref, ctab_ref, meta_ref,
                  loff_ref, goff_ref, n8_ref, *, tm, nbp, tiles):
    i = pl.program_id(0)
    hi = lax.Precision.HIGHEST

    @pl.when(i == 0)
    def _():
        cnt = cnt_ref[...]
        n8 = jnp.floor((cnt + (CHUNK - 1)) * (1.0 / CHUNK)) * CHUNK
        er = _row_iota((N_EXPERTS, N_EXPERTS))
        ec = _lane_iota((N_EXPERTS, N_EXPERTS))
        below = jnp.where(er > ec, 1.0, 0.0)
        loff = jnp.dot(below, n8, preferred_element_type=F32, precision=hi)
        rows_e = jnp.sum(n8, axis=-1, keepdims=True) + jnp.zeros_like(n8)
        padded = jnp.floor((rows_e + (MOE_ROWS - 1)) * (1.0 / MOE_ROWS)) * MOE_ROWS
        e_start = jnp.dot(below, padded, preferred_element_type=F32, precision=hi)
        tr = _row_iota((LANES, LANES))
        tc = _lane_iota((LANES, LANES))
        earlier = jnp.where(tr < tc, 1.0, 0.0)
        goff = e_start + jnp.dot(n8, earlier, preferred_element_type=F32, precision=hi)
        loff_ref[...] = loff
        goff_ref[...] = goff
        n8_ref[...] = n8
        big = jnp.floor(n8 * (0.5 / CHUNK))
        small = n8 * (1.0 / CHUNK) - 2.0 * big
        row_t = _row_iota(tab_ref.shape)
        tab_ref[...] = jnp.where(row_t == 0, jnp.sum(big, axis=0, keepdims=True),
                                 jnp.where(row_t == 1, jnp.sum(small, axis=0, keepdims=True), 0.0)).astype(I32)
        reps = nbp // LANES
        pend_b = jnp.concatenate([e_start + padded] * reps, axis=1)
        vend_b = jnp.concatenate([e_start + rows_e] * reps, axis=1)
        used_b = jnp.concatenate([padded] * reps, axis=1) > 0.0
        b0 = (_lane_iota((N_EXPERTS, nbp)) * MOE_ROWS).astype(F32)
        bexp = jnp.sum(jnp.where(pend_b <= b0, 1.0, 0.0), axis=0, keepdims=True)
        bexp = jnp.minimum(bexp, N_EXPERTS - 1.0)
        e_b = _row_iota((N_EXPERTS, nbp)).astype(F32)
        is_e = e_b == bexp
        vend = jnp.sum(jnp.where(is_e, vend_b, 0.0), axis=0, keepdims=True)
        nvalid = jnp.clip(vend - b0[0:1], 0.0, float(MOE_ROWS))
        total = jnp.max(pend_b, axis=0, keepdims=True) * (1.0 / MOE_ROWS)
        order = jnp.sum(jnp.where(used_b & (e_b < bexp), 1.0, 0.0), axis=0, keepdims=True)
        nxt = jnp.min(jnp.where(used_b & (e_b > bexp), e_b, float(N_EXPERTS)), axis=0, keepdims=True)
        row = _row_iota((8, nbp))
        meta = jnp.where(row == 0, bexp, jnp.where(row == 1, nvalid, jnp.where(row == 2, total,
                         jnp.where(row == 3, order, jnp.where(row == 4, nxt, 0.0)))))
        meta_ref[...] = meta.astype(I32)

    su = jnp.where(_row_iota((tm, tm)) < _lane_iota((tm, tm)), 1.0, 0.0).astype(BF16)
    incl = jnp.where(_row_iota((N_EXPERTS, N_EXPERTS)) >= _lane_iota((N_EXPERTS, N_EXPERTS)), 1.0, 0.0)
    cidx = _lane_iota((N_EXPERTS, LANES)).astype(F32)
    e_iota = _row_iota((N_EXPERTS, tm))
    out_row = _row_iota((LANES, tm))
    for k in range(tiles):
        tok = slice(k * tm, (k + 1) * tm)
        oh0 = e_iota == r_ref[0:1, tok].astype(I32)
        oh1 = e_iota == r_ref[1:2, tok].astype(I32)
        oh = jnp.where(oh0 | oh1, 1.0, 0.0)
        tile_lane = _lane_iota((N_EXPERTS, LANES)) == i * tiles + k
        before = jnp.dot(oh.astype(BF16), su, preferred_element_type=F32)
        base = jnp.sum(jnp.where(tile_lane, loff_ref[...], 0.0), axis=-1, keepdims=True) + before
        d0 = jnp.sum(jnp.where(oh0, base, 0.0), axis=0, keepdims=True)
        d1 = jnp.sum(jnp.where(oh1, base, 0.0), axis=0, keepdims=True)
        lrow_ref[k, 0:1, :] = d0.astype(I32)
        lrow_ref[k, 1:2, :] = d1.astype(I32)
        pick_tile = lambda ref: jnp.sum(jnp.where(tile_lane, ref[...], 0.0), axis=-1, keepdims=True)
        nch = pick_tile(n8_ref) * (1.0 / CHUNK)
        n_big = jnp.floor(nch * 0.5)
        n_small = nch - 2.0 * n_big
        loff_t = pick_tile(loff_ref)
        goff_t = pick_tile(goff_ref)
        for c, (n, rows, first) in enumerate(((n_big, COPY_ROWS[0], 0.0),
                                              (n_small, COPY_ROWS[1], n_big * COPY_ROWS[0]))):
            cend = jnp.dot(incl, n + jnp.zeros((N_EXPERTS, LANES), F32), preferred_element_type=F32, precision=hi)
            cstart = cend - n
            mine = (cidx >= cstart) & (cidx < cend)
            step_rows = first + (cidx - cstart) * rows
            ctab_ref[k, 2 * c:2 * c + 1, :] = jnp.sum(jnp.where(mine, loff_t + step_rows, 0.0), axis=0,
                                                      keepdims=True).astype(I32)
            ctab_ref[k, 2 * c + 1:2 * c + 2, :] = jnp.sum(jnp.where(mine, goff_t + step_rows, 0.0), axis=0,
                                                          keepdims=True).astype(I32)
        lcol_ref[tok, :] = jnp.where(out_row == 0, d0, jnp.where(out_row == 1, d1,
                                     jnp.where(out_row == 2, r_ref[2:3, tok],
                                               jnp.where(out_row == 3, r_ref[3:4, tok], 0.0)))).T


def _route(rrow, cnt, tm, nbp):
    t = rrow.shape[1]
    nt = t // tm
    tiles = next(r for r in (4, 2, 1) if nt % r == 0)
    kern = functools.partial(_route_kernel, tm=tm, nbp=nbp, tiles=tiles)
    return pl.pallas_call(
        kern,
        grid=(nt // tiles,),
        in_specs=[pl.BlockSpec((8, tiles * tm), lambda i: (0, i)),
                  pl.BlockSpec((N_EXPERTS, LANES), lambda i: (0, 0))],
        out_specs=[pl.BlockSpec((tiles, 2, tm), lambda i: (i, 0, 0)),
                   pl.BlockSpec((tiles * tm, LANES), lambda i: (i, 0)),
                   pl.BlockSpec((8, LANES), lambda i: (0, 0)),
                   pl.BlockSpec((tiles, 2 * len(COPY_ROWS), LANES), lambda i: (i, 0, 0)),
                   pl.BlockSpec((8, nbp), lambda i: (0, 0))],
        out_shape=[jax.ShapeDtypeStruct((nt, 2, tm), I32), jax.ShapeDtypeStruct((t, LANES), F32),
                   jax.ShapeDtypeStruct((8, LANES), I32), jax.ShapeDtypeStruct((nt, 2 * len(COPY_ROWS), LANES), I32),
                   jax.ShapeDtypeStruct((8, nbp), I32)],
        scratch_shapes=[pltpu.VMEM((N_EXPERTS, LANES), F32)] * 3,
        compiler_params=_cparams("arbitrary"),
        name="moe_route",
    )(rrow, cnt)


def _chunk_copies(tabs, i, local_ref, global_ref, sem, to_global, action):
    tab_ref, ctab_ref = tabs
    for k, rows in enumerate(COPY_ROWS):
        count = tab_ref[k, i]

        def copy(lo, go, rows=rows):
            lsl = local_ref.at[pl.ds(pl.multiple_of(lo, CHUNK), rows)]
            gsl = global_ref.at[pl.ds(pl.multiple_of(go, CHUNK), rows)]
            return pltpu.make_async_copy(lsl, gsl, sem) if to_global else pltpu.make_async_copy(gsl, lsl, sem)

        if action == "wait":
            def one(c, c1, copy=copy):
                copy(0, 0).wait()
                return c1
        else:
            def one(c, c1, copy=copy, k=k):
                copy(ctab_ref[i, 2 * k, c], ctab_ref[i, 2 * k + 1, c]).start()
                return c1

        lax.fori_loop(0, count, one, 0)


def _scatter_kernel(tab_ref, ctab_ref, meta_ref, lrow_ref, h_ref, xs_ref, buf_ref, zero_ref, sem, zsem, *, tm, nb):
    i = pl.program_id(0)
    tabs = (tab_ref, ctab_ref)

    @pl.when(i == 0)
    def _():
        zero_ref[...] = jnp.zeros_like(zero_ref)
        n_used = meta_ref[2, 0]

        def each_piece(action):
            def body(b, c):
                for piece in range(MOE_ROWS // ZERO_ROWS):
                    @pl.when((b < n_used) & (meta_ref[1, b] < (piece + 1) * ZERO_ROWS))
                    def _():
                        start = pl.multiple_of(b * MOE_ROWS + piece * ZERO_ROWS, ZERO_ROWS)
                        cp = pltpu.make_async_copy(zero_ref, xs_ref.at[pl.ds(start, ZERO_ROWS)], zsem)
                        getattr(cp, action)()
                return c
            lax.fori_loop(0, nb, body, 0)

        each_piece("start")
        each_piece("wait")

    rows = _row_iota((LOCAL_ROWS, tm))

    def drain(tile, slot):
        _chunk_copies(tabs, tile, buf_ref.at[slot], xs_ref, sem.at[slot], True, "wait")

    for slot in range(2):
        tile = 2 * i + slot
        p0 = rows == lrow_ref[slot, 0:1, :]
        p1 = rows == lrow_ref[slot, 1:2, :]
        perm = jnp.where(p0 | p1, 1.0, 0.0).astype(BF16)
        sorted_rows = jnp.dot(perm, h_ref[slot * tm:(slot + 1) * tm, :],
                              preferred_element_type=F32)

        @pl.when(i > 0)
        def _():
            drain(tile - 2, slot)

        buf = buf_ref.at[slot]
        buf[...] = _pack_bf16_pairs(sorted_rows, exact=True)
        _chunk_copies(tabs, tile, buf, xs_ref, sem.at[slot], True, "start")

    @pl.when(i == pl.num_programs(0) - 1)
    def _():
        drain(2 * i, 0)
        drain(2 * i + 1, 1)


def _scatter(tab, ctab, meta, lrow, h2, tm, nb):
    t = h2.shape[0]
    kern = functools.partial(_scatter_kernel, tm=tm, nb=nb)
    grid_spec = pltpu.PrefetchScalarGridSpec(
        num_scalar_prefetch=3,
        grid=(t // (2 * tm),),
        in_specs=[pl.BlockSpec((2, 2, tm), lambda i, *_: (i, 0, 0)),
                  pl.BlockSpec((2 * tm, D_MODEL), lambda i, *_: (i, 0))],
        out_specs=pl.BlockSpec(memory_space=pl.ANY),
        scratch_shapes=[pltpu.VMEM((2, LOCAL_ROWS, XS_WIDTH), U32), pltpu.VMEM((ZERO_ROWS, XS_WIDTH), U32),
                        pltpu.SemaphoreType.DMA((2,)), pltpu.SemaphoreType.DMA],
    )
    return pl.pallas_call(
        kern,
        grid_spec=grid_spec,
        out_shape=jax.ShapeDtypeStruct((nb * MOE_ROWS, XS_WIDTH), U32),
        compiler_params=_cparams("arbitrary"),
        name="moe_scatter",
    )(tab, ctab, meta, lrow, h2)


def _ffn_kernel(meta_ref, x_ref, wg_ref, wu_ref, wd_ref, o_ref, wgu_b, wd_b, wg_f, wu_f, wd_f, wsem, *, layer):
    b = pl.program_id(0)
    live = b < meta_ref[2, 0]
    expert = meta_ref[0, b]
    prev = meta_ref[0, jnp.maximum(b - 1, 0)]

    def fetch(e, slot):
        return [pltpu.make_async_copy(src.at[layer, e], dst.at[slot], wsem.at[slot])
                for src, dst in ((wg_ref, wg_f), (wu_ref, wu_f), (wd_ref, wd_f))]

    def first_block(slot):
        @pl.when(b == 0)
        def _():
            for cp in fetch(expert, slot):
                cp.start()

        for cp in fetch(expert, slot):
            cp.wait()
        wgu_b[:, 0:EXPERT_FF] = wg_f[slot].astype(BF16)
        wgu_b[:, EXPERT_FF:2 * EXPERT_FF] = wu_f[slot].astype(BF16)
        wd_b[...] = wd_f[slot].astype(BF16)
        nxt = meta_ref[4, b]

        @pl.when(nxt < N_EXPERTS)
        def _():
            for cp in fetch(nxt, 1 - slot):
                cp.start()

    changed = live & ((b == 0) | (expert != prev))
    odd = (meta_ref[3, b] & 1) == 1

    @pl.when(changed & jnp.logical_not(odd))
    def _():
        first_block(0)

    @pl.when(changed & odd)
    def _():
        first_block(1)

    @pl.when(live)
    def _():
        x = _unpack_bf16_pairs(x_ref[...])
        gu = jnp.dot(x, wgu_b[...], preferred_element_type=F32)
        act = (_silu(gu[:, 0:EXPERT_FF]) * gu[:, EXPERT_FF:2 * EXPERT_FF]).astype(BF16)
        y = jnp.dot(act, wd_b[...], preferred_element_type=F32)
        o_ref[...] = _pack_bf16_pairs(y)


def _ffn(meta, xs, wg, wu, wd, layer, nb):
    def blk(b, m):
        return (jnp.maximum(jnp.minimum(b, m[2, 0] - 1), 0), 0)

    grid_spec = pltpu.PrefetchScalarGridSpec(
        num_scalar_prefetch=1,
        grid=(nb,),
        in_specs=[pl.BlockSpec((MOE_ROWS, XS_WIDTH), blk)] + [pl.BlockSpec(memory_space=pl.ANY)] * 3,
        out_specs=pl.BlockSpec((MOE_ROWS, PACKED), blk),
        scratch_shapes=[pltpu.VMEM((D_MODEL, 2 * EXPERT_FF), BF16), pltpu.VMEM((EXPERT_FF, D_MODEL), BF16),
                        pltpu.VMEM((2, D_MODEL, EXPERT_FF), F32), pltpu.VMEM((2, D_MODEL, EXPERT_FF), F32),
                        pltpu.VMEM((2, EXPERT_FF, D_MODEL), F32), pltpu.SemaphoreType.DMA((2,))],
    )
    return pl.pallas_call(
        functools.partial(_ffn_kernel, layer=layer),
        grid_spec=grid_spec,
        out_shape=jax.ShapeDtypeStruct((nb * MOE_ROWS, PACKED), U32),
        compiler_params=_cparams("arbitrary"),
        name="moe_experts",
    )(meta, xs, wg, wu, wd)


def _gather_kernel(tab_ref, ctab_ref, lcol_ref, x_ref, g_ref, ys_ref, o_ref, buf_ref, sem, *, tm, final):
    i = pl.program_id(0)

    last = pl.num_programs(0) - 1

    def fetch(tile, slot, action):
        _chunk_copies((tab_ref, ctab_ref), tile, buf_ref.at[slot], ys_ref, sem.at[slot], False, action)

    @pl.when(i == 0)
    def _():
        buf_ref[...] = jnp.zeros_like(buf_ref)
        fetch(0, 0, "start")

    col = _lane_iota((tm, LOCAL_ROWS)).astype(F32)
    def picks(tok):
        lc = lcol_ref[tok, :]
        return jnp.concatenate([jnp.where(col == lc[:, 0:1], 1.0, 0.0).astype(BF16),
                                jnp.where(col == lc[:, 1:2], 1.0, 0.0).astype(BF16)], axis=0)

    toks = [slice(slot * tm, (slot + 1) * tm) for slot in range(2)]
    fetch(2 * i + 1, 1, "start")
    pick = picks(toks[0])
    for slot in range(2):
        tile = 2 * i + slot
        tok = toks[slot]
        fetch(tile, slot, "wait")
        y = _unpack_bf16_pairs(buf_ref[slot])
        both = jnp.dot(pick, y, preferred_element_type=F32)
        if slot == 0:
            pick = picks(toks[1])
        lc = lcol_ref[tok, :]
        x = x_ref[tok, :] + lc[:, 2:3] * both[0:tm] + lc[:, 3:4] * both[tm:2 * tm]
        o_ref[tok, :] = _rms(x, g_ref[...]) if final else x
        if slot == 0:
            @pl.when(i < last)
            def _():
                fetch(tile + 2, 0, "start")


def _gather(tab, ctab, lcol, x2, g, ys, tm, final):
    t = x2.shape[0]
    kern = functools.partial(_gather_kernel, tm=tm, final=final)
    grid_spec = pltpu.PrefetchScalarGridSpec(
        num_scalar_prefetch=2,
        grid=(t // (2 * tm),),
        in_specs=[pl.BlockSpec((2 * tm, LANES), lambda i, *_: (i, 0)),
                  pl.BlockSpec((2 * tm, D_MODEL), lambda i, *_: (i, 0)),
                  pl.BlockSpec((1, D_MODEL), lambda i, *_: (0, 0)),
                  pl.BlockSpec(memory_space=pl.ANY)],
        out_specs=pl.BlockSpec((2 * tm, D_MODEL), lambda i, *_: (i, 0)),
        scratch_shapes=[pltpu.VMEM((2, LOCAL_ROWS, PACKED), U32), pltpu.SemaphoreType.DMA((2,))],
    )
    return pl.pallas_call(
        kern,
        grid_spec=grid_spec,
        out_shape=jax.ShapeDtypeStruct((t, D_MODEL), F32),
        compiler_params=_cparams("arbitrary"),
        name="moe_combine",
    )(tab, ctab, lcol, x2, g, ys)


def _pad_rows(a, rows=8):
    return jnp.zeros((rows, a.shape[-1]), F32).at[:a.shape[0]].set(a.astype(F32))


def _arrange_mla(w_uq, w_ukv):
    half = MLA_ROPE // 2
    qd = MLA_NOPE + MLA_ROPE
    wq, wqs, wk, wv = [], [], [], []
    zq = jnp.zeros((MLA_Q_LORA, LANES - qd), w_uq.dtype)
    zk = jnp.zeros((MLA_KV_LORA, LANES - MLA_NOPE), w_ukv.dtype)
    for h in range(N_HEADS):
        q = w_uq[:, h * qd:(h + 1) * qd]
        nope, rope = q[:, :MLA_NOPE], q[:, MLA_NOPE:]
        wq.append(jnp.concatenate([nope, rope, zq], axis=1))
        wqs.append(jnp.concatenate([jnp.zeros_like(nope), -rope[:, half:], rope[:, :half], zq], axis=1))
        kv = w_ukv[:, h * 2 * MLA_NOPE:(h + 1) * 2 * MLA_NOPE]
        wk.append(jnp.concatenate([kv[:, :MLA_NOPE], zk], axis=1))
        wv.append(kv[:, MLA_NOPE:])
    cat = lambda xs: jnp.concatenate(xs, axis=1).astype(BF16)
    return cat(wq), cat(wqs), cat(wk), cat(wv).T


def kernel(x, positions, norm_mix, w_in, conv_a, fox_forget_bias, ssm_conv_w, ssm_conv_b, ssm_dt_bias,
           ssm_a_log, ssm_d, ssm_norm, mla_q_norm, mla_kv_norm, mla_w_uq, mla_w_ukv, w_out, norm_ffn,
           router_group_w, router_group_b, router_expert_w, router_expert_b, expert_w_gate, expert_w_up,
           expert_w_down, norm_final):
    batch, seq, d = x.shape
    t = batch * seq
    depth = w_in.shape[0]
    tm = min(ROW_TILE, seq)
    tq = min(ATTN_TQ, seq)
    tmd = MOE_TILE
    assert d == D_MODEL and seq % tm == 0 and seq % tq == 0 and tq % SSM_CHUNK == 0, (x.shape,)
    assert seq // tq <= 8 and tm % (LANES * ROPE_PACK) == 0, (seq, tm)
    assert t % (2 * tmd) == 0 and t // tmd <= LANES and tm % tmd == 0, (t, tmd)
    max_rows = 2 * t + (CHUNK - 1) * N_EXPERTS * (t // tmd) + N_EXPERTS * (MOE_ROWS - 1)
    nb = -(-max_rows // MOE_ROWS)
    nbp = -(-nb // LANES) * LANES

    xf = x.reshape(t, d)
    cos, sin = _rope_tables(positions, tm)
    w_in_t = jnp.swapaxes(w_in, 1, 2)

    for l in range(depth):
        wq, wqs, wk, wv = _arrange_mla(mla_w_uq[l], mla_w_ukv[l])
        ya, pb, pc, misc, fox_vt, q, k, v = _inproj(
            xf, norm_mix[l][None, :], w_in_t, l, _pad_rows(conv_a[l]), cos, sin, mla_q_norm[l][None, :],
            mla_kv_norm[l][None, :], jnp.concatenate([wq, wqs], axis=1), wk, wv, tm, seq)

        sp = jnp.zeros((8, LANES), F32)
        sp = sp.at[0, MISC_F:MISC_F + N_HEADS].set(fox_forget_bias[l])
        sp = sp.at[0, MISC_DT:MISC_DT + N_HEADS].set(ssm_dt_bias[l])
        sp = sp.at[1, MISC_DT:MISC_DT + N_HEADS].set(ssm_a_log[l])
        col, rows, fox_q, fox_k, tref = _scalar_prep(misc, sp, pb, batch, seq, tq)

        yb = _attention(fox_q, fox_k, fox_vt, tref, batch, seq, tq, "fox_attention")
        conv_wb = _pad_rows(jnp.concatenate([ssm_conv_w[l], ssm_conv_b[l][None, :]], axis=0))
        ssd_par = _pad_rows(jnp.stack([jnp.repeat(ssm_d[l], HEAD_DIM), ssm_norm[l]]))
        yc = _ssd_mixer(pc, col, rows, conv_wb, ssd_par, batch, seq)
        yd = _attention(q, k, v, None, batch, seq, tq, "mla_attention")

        pad = jnp.zeros((d, LANES - N_EXPERTS - N_EXPERT_GROUPS), F32)
        wr = jnp.concatenate([router_expert_w[l], router_group_w[l], pad], axis=1)
        wr_hi = wr.astype(BF16)
        wr = jnp.concatenate([wr_hi, (wr - wr_hi.astype(F32)).astype(BF16)], axis=1)
        br = jnp.concatenate([router_expert_b[l], router_group_b[l], pad[0]])[None, :]
        x2, h2, rrow, cnt = _outproj(xf, ya, yb, yc, yd, w_out[l].astype(BF16), norm_ffn[l][None, :], wr, br,
                                     tm, tmd)

        lrow, lcol, tab, ctab, meta = _route(rrow, cnt, tmd, nbp)
        xs = _scatter(tab, ctab, meta, lrow, h2, tmd, nb)
        ys = _ffn(meta, xs, expert_w_gate, expert_w_up, expert_w_down, l, nb)
        final = l == depth - 1
        xf = _gather(tab, ctab, lcol, x2, norm_final[None, :], ys, tmd, final)

    return xf.reshape(batch, seq, d)
```

```python
import functools
import math

import jax
import jax.numpy as jnp
import numpy as np
from jax import lax
from jax.experimental import pallas as pl
from jax.experimental.pallas import tpu as pltpu

F32 = jnp.float32
BF16 = jnp.bfloat16
I32 = jnp.int32

LANES = 128
VMEM_LIMIT_BYTES = 56 * 1024 * 1024

D_MODEL = 1024
RMS_EPS = 1e-6
LOG2E = math.log2(math.e)
GROUP_WIDTH = 256
HEAD_DIM = 64
N_HEADS = 4

CONV_A_WIDTH = 3
SSM_CONV = 4
SSM_STATE = 64
SSM_CHUNK = 256

MLA_NOPE = 64
MLA_ROPE = 32
MLA_Q_LORA = 256
MLA_KV_LORA = 128
ROPE_BASE = 10000.0
MLA_CHUNK = 64
ATTN_TQ = 512
ROW_TILE = 1024

N_EXPERT_GROUPS = 4
EXPERTS_PER_GROUP = 8
N_EXPERTS = 32
EXPERT_FF = 256
ROUTER_ROWS = 48
MOE_ROWS = 512
ZERO_ROWS = 256

SEG_A = (0, 768)
SEG_B = (768, 1280)
SEG_C = (1280, 2048)
SEG_D = (2048, 2432)
SEG_M = (2432, 2560)
SEG_M2 = (2560, 2688)
IN_COLS_PADDED = 2688
HEAD_PAD = N_HEADS * LANES
AUG_LANE = HEAD_DIM
MISC_F = 0
MISC_DT = 4
MISC_ROPE = 64
COL_CUMF = 0
COL_DT = 4
COL_ACUM = 8
N_SCALAR_ROWS = 16


def _cparams(*sem):
    return pltpu.CompilerParams(dimension_semantics=sem, vmem_limit_bytes=VMEM_LIMIT_BYTES)


def _lane_iota(shape):
    return lax.broadcasted_iota(I32, shape, len(shape) - 1)


def _row_iota(shape):
    return lax.broadcasted_iota(I32, shape, 0)


def _rms(x, g):
    ms = jnp.mean(x * x, axis=-1, keepdims=True)
    return x * lax.rsqrt(ms + RMS_EPS) * g


def _silu(x):
    return x / (1.0 + jnp.exp(-x))


def _softplus(x):
    return jnp.maximum(x, 0.0) + jnp.log(1.0 + jnp.exp(-jnp.abs(x)))


def _shift_rows(x, k):
    rolled = pltpu.roll(x, k, 0)
    return jnp.where(_row_iota(x.shape) >= k, rolled, 0.0)


def _rope_kernel(pos_ref, freq_ref, cos_ref, sin_ref):
    q = pos_ref.shape[2]
    lane = _lane_iota((q, LANES))
    group = lane >> int(math.log2(MLA_ROPE))
    rows = jnp.concatenate([pos_ref[0], jnp.zeros((8 - ROPE_PACK, q), F32)], axis=0)
    cols = jnp.concatenate([rows] * (LANES // 8), axis=0).T
    pos = cols[:, ROPE_PACK - 1:ROPE_PACK]
    for k in range(ROPE_PACK - 2, -1, -1):
        pos = jnp.where(group == k, cols[:, k:k + 1], pos)
    ang = pos * freq_ref[...]
    c = jnp.cos(ang)
    s = jnp.sin(ang)
    rope = (lane >= MISC_ROPE) & (lane < MISC_ROPE + MLA_ROPE)
    for k in range(ROPE_PACK):
        shift = (MISC_ROPE - MLA_ROPE * k) % LANES
        ck = pltpu.roll(c, shift, 1) if shift else c
        sk = pltpu.roll(s, shift, 1) if shift else s
        cos_ref[k * q:(k + 1) * q, :] = jnp.where(rope, ck, jnp.where(lane < MISC_ROPE, 1.0, 0.0))
        sin_ref[k * q:(k + 1) * q, :] = jnp.where(rope, sk, 0.0)


ROPE_PACK = LANES // MLA_ROPE


def _rope_tables(positions, tm):
    t = positions.size
    pos = positions.astype(F32).reshape(t // tm, ROPE_PACK, tm // ROPE_PACK)
    inv = ROPE_BASE ** (-np.arange(0, MLA_ROPE, 2, dtype=np.float32) / MLA_ROPE)
    freq = np.tile(np.concatenate([inv, inv]), ROPE_PACK)[None, :].astype(np.float32)
    return pl.pallas_call(
        _rope_kernel,
        grid=(t // tm,),
        in_specs=[pl.BlockSpec((1, ROPE_PACK, tm // ROPE_PACK), lambda i: (i, 0, 0)),
                  pl.BlockSpec((1, LANES), lambda i: (0, 0))],
        out_specs=[pl.BlockSpec((tm, LANES), lambda i: (i, 0))] * 2,
        out_shape=[jax.ShapeDtypeStruct((t, LANES), F32)] * 2,
        compiler_params=_cparams("parallel"),
        name="rope_tables",
    )(pos, jnp.asarray(freq))


def _arrange_w_in_kernel(wt_ref, w_ref, wvt_ref):
    gw = GROUP_WIDTH
    rows = lambda lo, hi: wt_ref[0, lo:hi, :]

    def put(seg0, piece_t):
        w_ref[:, seg0:seg0 + piece_t.shape[0]] = piece_t.T.astype(BF16)

    for j in range(3):
        put(SEG_A[0] + j * gw, rows(j * gw, (j + 1) * gw))
    put(SEG_B[0], rows(3 * gw, 4 * gw) * (HEAD_DIM ** -0.5 * LOG2E))
    put(SEG_B[0] + gw, rows(4 * gw, 5 * gw))
    wvt_ref[...] = rows(5 * gw, 6 * gw).astype(BF16)
    c0 = 6 * gw
    win = pltpu.roll(rows(c0, c0 + 3 * gw + 8), 3 * gw + 8 - N_HEADS, 0)
    for j in range(3):
        put(SEG_C[0] + j * gw, win[j * gw:(j + 1) * gw])
    d0 = c0 + 3 * gw + 8
    put(SEG_D[0], rows(d0, d0 + MLA_Q_LORA))
    put(SEG_D[0] + MLA_Q_LORA, rows(d0 + MLA_Q_LORA, d0 + MLA_Q_LORA + MLA_KV_LORA))
    kr0 = d0 + MLA_Q_LORA + MLA_KV_LORA
    half = MLA_ROPE // 2
    first8 = jnp.where(_row_iota((8, D_MODEL)) < N_HEADS, rows(c0, c0 + 8), rows(d0 - 8, d0))
    zeros = lambda n: jnp.zeros((n, D_MODEL), F32)
    misc_t = jnp.concatenate([first8, zeros(MISC_ROPE - 8), rows(kr0, kr0 + MLA_ROPE),
                              zeros(LANES - MISC_ROPE - MLA_ROPE)], axis=0)
    misc2_t = jnp.concatenate([zeros(MISC_ROPE), -rows(kr0 + half, kr0 + MLA_ROPE), rows(kr0, kr0 + half),
                               zeros(LANES - MISC_ROPE - MLA_ROPE)], axis=0)
    put(SEG_M[0], misc_t)
    put(SEG_M2[0], misc2_t)


def _inproj_kernel(x_ref, g_ref, wt_ref, cw_ref, cos_ref, sin_ref, nq_ref, nkv_ref, wq2_ref, wk_ref,
                   wvt2_ref, oa, ob, oc, om, ovt, q_ref, k_ref, vt_ref, w_ref, wvt_ref, halo_ref,
                   *, tm, tiles_per_seq):
    @pl.when(pl.program_id(0) == 0)
    def _():
        _arrange_w_in_kernel(wt_ref, w_ref, wvt_ref)

    h = _rms(x_ref[...], g_ref[...]).astype(BF16)
    for o, (lo, hi) in ((ob, SEG_B), (oc, SEG_C)):
        o[...] = jnp.dot(h, w_ref[:, lo:hi], preferred_element_type=F32).astype(o.dtype)

    gw = GROUP_WIDTH
    pa = jnp.dot(h, w_ref[:, SEG_A[0]:SEG_A[1]], preferred_element_type=F32)
    cv = pa[:, gw:2 * gw] * pa[:, 2 * gw:3 * gw]

    @pl.when(pl.program_id(0) % tiles_per_seq == 0)
    def _():
        halo_ref[...] = jnp.zeros_like(halo_ref)

    halo = halo_ref[...]
    row8 = _row_iota(halo.shape)
    acc = cv * cw_ref[CONV_A_WIDTH - 1:CONV_A_WIDTH, :]
    for k in range(1, CONV_A_WIDTH):
        shifted = pltpu.roll(cv, k, 0)
        top = jnp.where(row8 < k, pltpu.roll(halo, k, 0), shifted[0:8])
        shifted = jnp.concatenate([top, shifted[8:]], axis=0)
        acc = acc + shifted * cw_ref[CONV_A_WIDTH - 1 - k:CONV_A_WIDTH - k, :]
    halo_ref[...] = cv[tm - 8:tm]
    oa[...] = (pa[:, 0:gw] * acc).astype(oa.dtype)
    ovt[...] = lax.dot_general(wvt_ref[...], h, (((1,), (1,)), ((), ())),
                               preferred_element_type=F32).astype(ovt.dtype)
    misc = jnp.dot(h, w_ref[:, SEG_M[0]:SEG_M[1]], preferred_element_type=F32)
    misc2 = jnp.dot(h, w_ref[:, SEG_M2[0]:SEG_M2[1]], preferred_element_type=F32)
    om[...] = misc

    pd = jnp.dot(h, w_ref[:, SEG_D[0]:SEG_D[1]], preferred_element_type=F32)
    cq = _rms(pd[:, 0:MLA_Q_LORA], nq_ref[...]).astype(BF16)
    ckv = _rms(pd[:, MLA_Q_LORA:MLA_Q_LORA + MLA_KV_LORA], nkv_ref[...]).astype(BF16)
    cos = cos_ref[...]
    sin = sin_ref[...]
    cos4 = jnp.concatenate([cos] * N_HEADS, axis=1)
    sin4 = jnp.concatenate([sin] * N_HEADS, axis=1)
    scale = (MLA_NOPE + MLA_ROPE) ** -0.5 * LOG2E
    q2 = jnp.dot(cq, wq2_ref[...], preferred_element_type=F32)
    q_ref[...] = ((q2[:, 0:HEAD_PAD] * cos4 + q2[:, HEAD_PAD:2 * HEAD_PAD] * sin4) * scale).astype(q_ref.dtype)
    lane = _lane_iota(cos.shape)
    rope = (lane >= MISC_ROPE) & (lane < MISC_ROPE + MLA_ROPE)
    kr = jnp.where(rope, misc * cos + misc2 * sin, 0.0)
    k = jnp.dot(ckv, wk_ref[...], preferred_element_type=F32)
    k_ref[...] = (k + jnp.concatenate([kr] * N_HEADS, axis=1)).astype(k_ref.dtype)
    vt_ref[...] = lax.dot_general(wvt2_ref[...], ckv, (((1,), (1,)), ((), ())),
                                  preferred_element_type=F32).astype(vt_ref.dtype)


def _inproj(x, g, w_in_t, layer, conv_w, cos, sin, nq, nkv, wq2, wk, wvt2, tm, seq):
    t = x.shape[0]
    once = lambda a: pl.BlockSpec((1,) + a.shape[1:], lambda i: (layer, 0, 0), pipeline_mode=pl.Buffered(1))
    full = lambda a: pl.BlockSpec(a.shape, lambda i: (0, 0))
    tile = lambda wd: pl.BlockSpec((tm, wd), lambda i: (i, 0))
    cols = lambda: pl.BlockSpec((GROUP_WIDTH, tm), lambda i: (0, i))
    seg = lambda s: s[1] - s[0]
    return pl.pallas_call(
        functools.partial(_inproj_kernel, tm=tm, tiles_per_seq=seq // tm),
        grid=(t // tm,),
        in_specs=[tile(D_MODEL), full(g), once(w_in_t), full(conv_w), tile(LANES), tile(LANES),
                  full(nq), full(nkv), full(wq2), full(wk), full(wvt2)],
        out_specs=[tile(GROUP_WIDTH), tile(seg(SEG_B)), tile(seg(SEG_C)), tile(LANES), cols(),
                   tile(HEAD_PAD), tile(HEAD_PAD), cols()],
        out_shape=[jax.ShapeDtypeStruct((t, GROUP_WIDTH), BF16), jax.ShapeDtypeStruct((t, seg(SEG_B)), BF16),
                   jax.ShapeDtypeStruct((t, seg(SEG_C)), BF16), jax.ShapeDtypeStruct((t, LANES), F32),
                   jax.ShapeDtypeStruct((GROUP_WIDTH, t), BF16),
                   jax.ShapeDtypeStruct((t, HEAD_PAD), BF16), jax.ShapeDtypeStruct((t, HEAD_PAD), BF16),
                   jax.ShapeDtypeStruct((GROUP_WIDTH, t), BF16)],
        scratch_shapes=[pltpu.VMEM((D_MODEL, IN_COLS_PADDED), BF16), pltpu.VMEM((GROUP_WIDTH, D_MODEL), BF16),
                        pltpu.VMEM((8, GROUP_WIDTH), F32)],
        compiler_params=_cparams("arbitrary"),
        name="inproj",
    )(x, g, w_in_t, conv_w, cos, sin, nq, nkv, wq2, wk, wvt2)


def _scalar_prep_kernel(m_ref, p_ref, qk_ref, place_ref,
                        col_ref, row_ref, qa_ref, ka_ref, tref_ref, *, tq):
    s = m_ref.shape[0]
    tref_ref[...] = jnp.zeros_like(tref_ref)
    m = m_ref[...]
    bias = p_ref[0:1, :]
    a_log = p_ref[1:2, :]
    lane = _lane_iota(m.shape)
    z = m + bias
    logf = jnp.minimum(z, 0.0) - jnp.log(1.0 + jnp.exp(-jnp.abs(z)))
    dt = _softplus(z)
    a = dt * (-jnp.exp(a_log))
    is_f = lane < MISC_DT
    is_dt = (lane >= MISC_DT) & (lane < MISC_DT + N_HEADS)
    v = jnp.where(is_f, logf, jnp.where(is_dt, a, 0.0))
    r = _row_iota((SSM_CHUNK, SSM_CHUNK))
    c = _lane_iota((SSM_CHUNK, SSM_CHUNK))
    tril = jnp.where(r >= c, 1.0, 0.0).astype(BF16)
    carry = jnp.zeros((1, LANES), F32)
    lane_1 = _lane_iota((1, LANES))
    lane_b = _lane_iota((SSM_CHUNK, LANES))
    for ci in range(s // SSM_CHUNK):
        rest = v[ci * SSM_CHUNK:(ci + 1) * SSM_CHUNK]
        cs = jnp.zeros((SSM_CHUNK, LANES), F32)
        for _ in range(3):
            term = rest.astype(BF16)
            cs = cs + jnp.dot(tril, term, preferred_element_type=F32)
            rest = rest - term.astype(F32)
        cs = cs + jnp.where(lane_1 < MISC_DT, carry, 0.0)
        carry = cs[SSM_CHUNK - 1:SSM_CHUNK]
        acum = pltpu.roll(cs, COL_ACUM - MISC_DT, 1)
        out = jnp.where(lane_b < MISC_DT, cs * LOG2E,
                        jnp.where(lane_b < COL_ACUM, dt[ci * SSM_CHUNK:(ci + 1) * SSM_CHUNK],
                                  jnp.where(lane_b < COL_ACUM + N_HEADS, acum, 0.0)))
        rows = slice(ci * SSM_CHUNK, (ci + 1) * SSM_CHUNK)
        col_ref[rows, :] = out
        row_ref[0, :, rows] = out.T[:N_SCALAR_ROWS]
    for ci in range(s // SSM_CHUNK):
        rows = slice(ci * SSM_CHUNK, (ci + 1) * SSM_CHUNK)
        ti = (ci * SSM_CHUNK) // tq
        tile_ref = col_ref[ti * tq:ti * tq + 1, :]
        if (ci * SSM_CHUNK) % tq == 0:
            tref_ref[0, ti:ti + 1, :] = tile_ref
        c = col_ref[rows, :] - tile_ref
        c_hi = c.astype(BF16).astype(F32)
        r1 = c - c_hi
        c_mid = r1.astype(BF16).astype(F32)
        c_lo = r1 - c_mid
        terms = jnp.where(lane_b < N_HEADS, c_hi,
                          jnp.where(lane_b < 2 * N_HEADS, pltpu.roll(c_mid, N_HEADS, 1),
                                    jnp.where(lane_b < 3 * N_HEADS, pltpu.roll(c_lo, 2 * N_HEADS, 1),
                                              jnp.where(lane_b == ONE_LANE, 1.0, 0.0)))).astype(BF16)
        for side, o_ref in enumerate((qa_ref, ka_ref)):
            for j in range(N_HEADS // 2):
                pair = qk_ref[rows, side * GROUP_WIDTH + j * LANES:
                              side * GROUP_WIDTH + (j + 1) * LANES]
                lhs = jnp.concatenate([pair, terms], axis=1)
                o_ref[rows, 2 * j * LANES:(2 * j + 2) * LANES] = jnp.dot(
                    lhs, place_ref[side, j], preferred_element_type=F32).astype(o_ref.dtype)


AUG_TERMS = 6
ONE_LANE = LANES - 1


def _fox_placement():
    assert COL_CUMF == 0
    place = np.zeros((2, N_HEADS // 2, 2 * LANES, 2 * LANES), np.float32)
    for h in range(N_HEADS):
        j, half = divmod(h, 2)
        o = half * LANES
        one = LANES + ONE_LANE
        for side in range(2):
            for d in range(HEAD_DIM):
                place[side, j, half * HEAD_DIM + d, o + d] = 1.0
        for term in range(3):
            src = LANES + term * N_HEADS + h
            place[0, j, src, o + AUG_LANE + term] = 1.0
            place[0, j, one, o + AUG_LANE + 3 + term] = 1.0
            place[1, j, one, o + AUG_LANE + term] = 1.0
            place[1, j, src, o + AUG_LANE + 3 + term] = -1.0
    return jnp.asarray(place, BF16)


def _scalar_prep(misc, params, qk, batch, seq, tq):
    place = _fox_placement()
    full = lambda a: pl.BlockSpec(a.shape, lambda b: (0,) * a.ndim)
    return pl.pallas_call(
        functools.partial(_scalar_prep_kernel, tq=tq),
        grid=(batch,),
        in_specs=[pl.BlockSpec((seq, LANES), lambda b: (b, 0)),
                  pl.BlockSpec((8, LANES), lambda b: (0, 0)),
                  pl.BlockSpec((seq, 2 * GROUP_WIDTH), lambda b: (b, 0)),
                  full(place)],
        out_specs=[pl.BlockSpec((seq, LANES), lambda b: (b, 0)),
                   pl.BlockSpec((1, N_SCALAR_ROWS, seq), lambda b: (b, 0, 0)),
                   pl.BlockSpec((seq, HEAD_PAD), lambda b: (b, 0)),
                   pl.BlockSpec((seq, HEAD_PAD), lambda b: (b, 0)),
                   pl.BlockSpec((1, 8, LANES), lambda b: (b, 0, 0))],
        out_shape=[jax.ShapeDtypeStruct((batch * seq, LANES), F32),
                   jax.ShapeDtypeStruct((batch, N_SCALAR_ROWS, seq), F32),
                   jax.ShapeDtypeStruct((batch * seq, HEAD_PAD), BF16),
                   jax.ShapeDtypeStruct((batch * seq, HEAD_PAD), BF16),
                   jax.ShapeDtypeStruct((batch, 8, LANES), F32)],
        compiler_params=_cparams("parallel"),
        name="scalar_prep",
    )(misc, params, qk, place)


def _pair_lanes(col, base, shape):
    lane = _lane_iota(shape)
    return jnp.where(lane < HEAD_DIM, col[:, base:base + 1], col[:, base + 1:base + 2])


def _ssd_kernel(p_ref, col_ref, row_ref, cw_ref, par_ref, o_ref, u_ref):
    s = p_ref.shape[0]
    q = SSM_CHUNK
    gw = GROUP_WIDTH
    xbc = p_ref[:, gw:3 * gw].astype(F32)
    acc = xbc * cw_ref[SSM_CONV - 1:SSM_CONV, :]
    for k in range(1, SSM_CONV):
        acc = acc + _shift_rows(xbc, k) * cw_ref[SSM_CONV - 1 - k:SSM_CONV - k, :]
    u_ref[...] = _silu(acc + cw_ref[SSM_CONV:SSM_CONV + 1, :])

    d_skip = par_ref[0:1, :]
    norm_g = par_ref[1:2, :]
    lane_q = _lane_iota((q, LANES))
    low = lane_q < HEAD_DIM
    tri = _row_iota((q, q)) >= _lane_iota((q, q))

    def chunk(ci, states):
        rows = pl.ds(ci * q, q)
        u = u_ref[rows, :]
        col = col_ref[rows, :]
        bm = u[:, gw:gw + LANES]
        cm = u[:, gw + LANES:gw + 2 * LANES]
        z = p_ref[rows, 0:gw].astype(F32)
        new_states = []
        ys = []
        for g in range(2):
            sel = low if g == 0 else jnp.logical_not(low)
            cg = jnp.where(sel, cm, 0.0).astype(BF16)
            bg = jnp.where(sel, bm, 0.0)
            gmat = lax.dot_general(cg, bm.astype(BF16), (((1,), (1,)), ((), ())),
                                   preferred_element_type=F32)
            xs = u[:, g * LANES:(g + 1) * LANES]
            dt2 = _pair_lanes(col, COL_DT + 2 * g, (q, LANES))
            ac2 = _pair_lanes(col, COL_ACUM + 2 * g, (q, LANES))
            xdt = xs * dt2
            xdt_b = xdt.astype(BF16)
            st = states[g]
            y_off = jnp.dot(cg, st.astype(BF16), preferred_element_type=F32) * jnp.exp(ac2)
            halves = []
            for hh in range(2):
                h = 2 * g + hh
                ac_col = col[:, COL_ACUM + h:COL_ACUM + h + 1]
                ac_row = row_ref[0, COL_ACUM + h:COL_ACUM + h + 1, rows]
                decay = jnp.exp(jnp.where(tri, ac_col - ac_row, -1e30))
                mm = (gmat * decay).astype(BF16)
                halves.append(jnp.dot(mm, xdt_b, preferred_element_type=F32))
            y = jnp.where(low, halves[0], halves[1]) + y_off + d_skip[:, g * LANES:(g + 1) * LANES] * xs
            ys.append(y)
            ac_last = ac2[q - 1:q, :]
            w_end = jnp.exp(ac_last - ac2)
            xw = (xdt * w_end).astype(BF16)
            upd = jnp.dot(bg.T.astype(BF16), xw, preferred_element_type=F32)
            new_states.append(st * jnp.exp(ac_last) + upd)
        yfull = jnp.concatenate(ys, axis=1) * _silu(z)
        o_ref[rows, :] = _rms(yfull, norm_g).astype(o_ref.dtype)
        return tuple(new_states)

    init = (jnp.zeros((LANES, LANES), F32), jnp.zeros((LANES, LANES), F32))
    states = init
    for ci in range(s // q):
        states = chunk(ci, states)


def _ssd_mixer(pc, col, rows, conv_wb, par, batch, seq):
    gw = GROUP_WIDTH
    return pl.pallas_call(
        _ssd_kernel,
        grid=(batch,),
        in_specs=[pl.BlockSpec((seq, 3 * gw), lambda b: (b, 0)),
                  pl.BlockSpec((seq, LANES), lambda b: (b, 0)),
                  pl.BlockSpec((1, N_SCALAR_ROWS, seq), lambda b: (b, 0, 0)),
                  pl.BlockSpec((8, 2 * gw), lambda b: (0, 0)),
                  pl.BlockSpec((8, gw), lambda b: (0, 0))],
        out_specs=pl.BlockSpec((seq, gw), lambda b: (b, 0)),
        out_shape=jax.ShapeDtypeStruct((batch * seq, gw), BF16),
        scratch_shapes=[pltpu.VMEM((seq, 2 * gw), F32)],
        compiler_params=_cparams("parallel"),
        name="ssd_mixer",
    )(pc, col, rows, conv_wb, par)


def _attn_kernel(*refs, fox, tq, nq):
    if fox:
        tref_ref, q_ref, k_ref, vt_ref, o_ref = refs
    else:
        q_ref, k_ref, vt_ref, o_ref = refs
        tref_ref = None
    b = pl.program_id(0)
    i = pl.program_id(1)
    key = _row_iota((tq, tq))
    qry = _lane_iota((tq, tq))
    if fox:
        allowed = key <= qry
    else:
        shift = int(math.log2(MLA_CHUNK))
        allowed = (key >> shift) <= (qry >> shift)
    qs = [q_ref[:, h * LANES:(h + 1) * LANES] for h in range(N_HEADS)]
    ones_rows = jnp.ones((16, tq), BF16)

    def step(j, masked, carry):
        rk = pl.ds(j * tq, tq)
        scores = [lax.dot_general(k_ref[rk, h * LANES:(h + 1) * LANES], qs[h], (((1,), (1,)), ((), ())),
                                  preferred_element_type=F32) for h in range(N_HEADS)]
        probs = []
        for h in range(N_HEADS):
            m, l, _ = carry[h]
            s = scores[h]
            if masked:
                s = jnp.where(allowed, s, -1e30)
            delta = (tref_ref[b, i, h] - tref_ref[b, j, h]) if fox else 0.0
            m_new = jnp.maximum(m, jnp.max(s, axis=0, keepdims=True) + delta)
            alpha = jnp.exp2(m - m_new)
            p = jnp.exp2(s - (m_new - delta))
            probs.append((m_new, alpha, p.astype(BF16)))
        new = []
        for h in range(N_HEADS):
            pair = h // 2
            m_new, alpha, p = probs[h]
            lhs = jnp.concatenate([vt_ref[pair * LANES:(pair + 1) * LANES, rk], ones_rows], axis=0)
            pv = jnp.dot(lhs, p, preferred_element_type=F32)
            new.append((m_new, alpha * carry[h][1] + pv[LANES:LANES + 1], alpha * carry[h][2] + pv[0:LANES]))
        return tuple(new)

    init = tuple((jnp.full((1, tq), -1e30, F32), jnp.zeros((1, tq), F32), jnp.zeros((LANES, tq), F32))
                 for _ in range(N_HEADS))
    top = _row_iota((LANES, tq)) < HEAD_DIM

    def sweep(n_before):
        carry = init
        for j in range(n_before):
            carry = step(j, False, carry)
        carry = step(n_before, True, carry)
        outs = [acc / l for (_, l, acc) in carry]
        o_t = jnp.concatenate([jnp.where(top, outs[0], outs[1]), jnp.where(top, outs[2], outs[3])], axis=0)
        o_ref[...] = o_t.T.astype(o_ref.dtype)

    for n_before in range(nq):
        pl.when(i == n_before)(functools.partial(sweep, n_before))


def _attention(q, k, vt, tref, batch, seq, tq, name):
    nq = seq // tq
    fox = tref is not None
    kern = functools.partial(_attn_kernel, fox=fox, tq=tq, nq=nq)
    grid_spec = pltpu.PrefetchScalarGridSpec(
        num_scalar_prefetch=1 if fox else 0,
        grid=(batch, nq),
        in_specs=[pl.BlockSpec((tq, HEAD_PAD), lambda b, i, *_: (b * nq + i, 0)),
                  pl.BlockSpec((seq, HEAD_PAD), lambda b, i, *_: (b, 0)),
                  pl.BlockSpec((GROUP_WIDTH, seq), lambda b, i, *_: (0, b))],
        out_specs=pl.BlockSpec((tq, GROUP_WIDTH), lambda b, i, *_: (b * nq + i, 0)),
    )
    args = ((tref,) if fox else ()) + (q, k, vt)
    return pl.pallas_call(
        kern,
        grid_spec=grid_spec,
        out_shape=jax.ShapeDtypeStruct((batch * seq, GROUP_WIDTH), BF16),
        compiler_params=_cparams("parallel", "arbitrary"),
        name=name,
    )(*args)


def _outproj_kernel(x_ref, ya, yb, yc, yd, w_ref, g_ref, wr_ref, br_ref,
                    x2_ref, h2_ref, rrow_ref, cnt_ref, *, tm, moe_tile):
    y = jnp.concatenate([ya[...], yb[...], yc[...], yd[...]], axis=1)
    x2 = x_ref[...] + jnp.dot(y, w_ref[...], preferred_element_type=F32)
    x2_ref[...] = x2
    h2 = _rms(x2, g_ref[...])
    h2_ref[...] = h2.astype(h2_ref.dtype)
    h_hi = h2.astype(BF16)
    h_lo = (h2 - h_hi.astype(F32)).astype(BF16)
    part = jnp.dot(h_hi, wr_ref[...], preferred_element_type=F32)
    logits = (part[:, 0:LANES] + part[:, LANES:2 * LANES]
              + jnp.dot(h_lo, wr_ref[:, 0:LANES], preferred_element_type=F32) + br_ref[...])
    lt = logits.T[0:ROUTER_ROWS]
    row = _row_iota(lt.shape)
    neg = -1e30
    big = 1 << 20
    gmask = (row >= N_EXPERTS) & (row < N_EXPERTS + N_EXPERT_GROUPS)
    gl = jnp.where(gmask, lt, neg)
    gmax = jnp.max(gl, axis=0, keepdims=True)
    gsum = jnp.sum(jnp.where(gmask, jnp.exp(gl - gmax), 0.0), axis=0, keepdims=True)
    g_w = 1.0 / gsum
    g_idx = jnp.min(jnp.where(gmask & (gl == gmax), row, big), axis=0, keepdims=True) - N_EXPERTS
    emask = (row < N_EXPERTS) & ((row >> int(math.log2(EXPERTS_PER_GROUP))) == g_idx)
    el = jnp.where(emask, lt, neg)
    e1v = jnp.max(el, axis=0, keepdims=True)
    esum = jnp.sum(jnp.where(emask, jnp.exp(el - e1v), 0.0), axis=0, keepdims=True)
    i1 = jnp.min(jnp.where(emask & (el == e1v), row, big), axis=0, keepdims=True)
    el2 = jnp.where(row == i1, neg, el)
    e2v = jnp.max(el2, axis=0, keepdims=True)
    i2 = jnp.min(jnp.where(emask & (row != i1) & (el2 == e2v), row, big), axis=0, keepdims=True)
    p1 = 1.0 / esum
    p2 = jnp.exp(e2v - e1v) / esum
    w1 = g_w * (p1 / (p1 + p2))
    w2 = g_w * (p2 / (p1 + p2))
    out_row = _row_iota(rrow_ref.shape)
    rrow_ref[...] = jnp.where(out_row == 0, i1.astype(F32),
                              jnp.where(out_row == 1, i2.astype(F32),
                                        jnp.where(out_row == 2, w1, jnp.where(out_row == 3, w2, 0.0))))
    step = pl.program_id(0)

    @pl.when(step == 0)
    def _():
        cnt_ref[...] = jnp.zeros_like(cnt_ref)

    chosen = jnp.where((row == i1) | (row == i2), 1.0, 0.0).astype(BF16)
    tiles = tm // moe_tile
    tile_of = (_row_iota((tm, LANES)) >> int(math.log2(moe_tile))) + step * tiles
    to_tile = jnp.where(_lane_iota((tm, LANES)) == tile_of, 1.0, 0.0).astype(BF16)
    counts = jnp.dot(chosen, to_tile, preferred_element_type=F32)
    cnt_ref[...] += counts[0:N_EXPERTS]


def _outproj(x, ya, yb, yc, yd, w, g, wr, br, tm, moe_tile):
    t = x.shape[0]
    full = lambda a: pl.BlockSpec(a.shape, lambda i: (0, 0))
    tile = lambda wd: pl.BlockSpec((tm, wd), lambda i: (i, 0))
    return pl.pallas_call(
        functools.partial(_outproj_kernel, tm=tm, moe_tile=moe_tile),
        grid=(t // tm,),
        in_specs=[tile(D_MODEL)] + [tile(GROUP_WIDTH)] * 4 + [full(w), full(g), full(wr), full(br)],
        out_specs=[tile(D_MODEL), tile(D_MODEL), pl.BlockSpec((8, tm), lambda i: (0, i)),
                   pl.BlockSpec((N_EXPERTS, LANES), lambda i: (0, 0))],
        out_shape=[jax.ShapeDtypeStruct((t, D_MODEL), F32), jax.ShapeDtypeStruct((t, D_MODEL), BF16),
                   jax.ShapeDtypeStruct((8, t), F32), jax.ShapeDtypeStruct((N_EXPERTS, LANES), F32)],
        compiler_params=_cparams("arbitrary"),
        name="outproj_router",
    )(x, ya, yb, yc, yd, w, g, wr, br)


MOE_TILE = 256
CHUNK = 8
LOCAL_ROWS = 2 * MOE_TILE + 256
PACKED = D_MODEL // 2
XS_WIDTH = PACKED
U32 = jnp.uint32


def _pack_bf16_pairs(x, exact=False):
    if not exact:
        x = x.astype(BF16).astype(F32)
    half = x.shape[1] // 2
    lo = lax.bitcast_convert_type(x[:, :half], U32)
    hi = lax.bitcast_convert_type(x[:, half:], U32)
    return hi | (lo >> 16)


def _unpack_bf16_pairs(words):
    lo = lax.bitcast_convert_type(words << 16, F32)
    hi = lax.bitcast_convert_type(words & U32(0xFFFF0000), F32)
    return jnp.concatenate([lo, hi], axis=1).astype(BF16)


COPY_ROWS = (2 * CHUNK, CHUNK)


def _route_kernel(r_ref, cnt_ref, lrow_ref, lcol_ref, tab_ref, ctab_ref, meta_ref,
                  loff_ref, goff_ref, n8_ref, *, tm, nbp, tiles):
    i = pl.program_id(0)
    hi = lax.Precision.HIGHEST

    @pl.when(i == 0)
    def _():
        cnt = cnt_ref[...]
        n8 = jnp.floor((cnt + (CHUNK - 1)) * (1.0 / CHUNK)) * CHUNK
        er = _row_iota((N_EXPERTS, N_EXPERTS))
        ec = _lane_iota((N_EXPERTS, N_EXPERTS))
        below = jnp.where(er > ec, 1.0, 0.0)
        loff = jnp.dot(below, n8, preferred_element_type=F32, precision=hi)
        rows_e = jnp.sum(n8, axis=-1, keepdims=True) + jnp.zeros_like(n8)
        padded = jnp.floor((rows_e + (MOE_ROWS - 1)) * (1.0 / MOE_ROWS)) * MOE_ROWS
        e_start = jnp.dot(below, padded, preferred_element_type=F32, precision=hi)
        tr = _row_iota((LANES, LANES))
        tc = _lane_iota((LANES, LANES))
        earlier = jnp.where(tr < tc, 1.0, 0.0)
        goff = e_start + jnp.dot(n8, earlier, preferred_element_type=F32, precision=hi)
        loff_ref[...] = loff
        goff_ref[...] = goff
        n8_ref[...] = n8
        big = jnp.floor(n8 * (0.5 / CHUNK))
        small = n8 * (1.0 / CHUNK) - 2.0 * big
        row_t = _row_iota(tab_ref.shape)
        tab_ref[...] = jnp.where(row_t == 0, jnp.sum(big, axis=0, keepdims=True),
                                 jnp.where(row_t == 1, jnp.sum(small, axis=0, keepdims=True), 0.0)).astype(I32)
        reps = nbp // LANES
        pend_b = jnp.concatenate([e_start + padded] * reps, axis=1)
        vend_b = jnp.concatenate([e_start + rows_e] * reps, axis=1)
        used_b = jnp.concatenate([padded] * reps, axis=1) > 0.0
        b0 = (_lane_iota((N_EXPERTS, nbp)) * MOE_ROWS).astype(F32)
        bexp = jnp.sum(jnp.where(pend_b <= b0, 1.0, 0.0), axis=0, keepdims=True)
        bexp = jnp.minimum(bexp, N_EXPERTS - 1.0)
        e_b = _row_iota((N_EXPERTS, nbp)).astype(F32)
        is_e = e_b == bexp
        vend = jnp.sum(jnp.where(is_e, vend_b, 0.0), axis=0, keepdims=True)
        nvalid = jnp.clip(vend - b0[0:1], 0.0, float(MOE_ROWS))
        total = jnp.max(pend_b, axis=0, keepdims=True) * (1.0 / MOE_ROWS)
        order = jnp.sum(jnp.where(used_b & (e_b < bexp), 1.0, 0.0), axis=0, keepdims=True)
        nxt = jnp.min(jnp.where(used_b & (e_b > bexp), e_b, float(N_EXPERTS)), axis=0, keepdims=True)
        row = _row_iota((8, nbp))
        meta = jnp.where(row == 0, bexp, jnp.where(row == 1, nvalid, jnp.where(row == 2, total,
                         jnp.where(row == 3, order, jnp.where(row == 4, nxt, 0.0)))))
        meta_ref[...] = meta.astype(I32)

    su = jnp.where(_row_iota((tm, tm)) < _lane_iota((tm, tm)), 1.0, 0.0).astype(BF16)
    incl = jnp.where(_row_iota((N_EXPERTS, N_EXPERTS)) >= _lane_iota((N_EXPERTS, N_EXPERTS)), 1.0, 0.0)
    cidx = _lane_iota((N_EXPERTS, LANES)).astype(F32)
    e_iota = _row_iota((N_EXPERTS, tm))
    out_row = _row_iota((LANES, tm))
    for k in range(tiles):
        tok = slice(k * tm, (k + 1) * tm)
        oh0 = e_iota == r_ref[0:1, tok].astype(I32)
        oh1 = e_iota == r_ref[1:2, tok].astype(I32)
        oh = jnp.where(oh0 | oh1, 1.0, 0.0)
        tile_lane = _lane_iota((N_EXPERTS, LANES)) == i * tiles + k
        before = jnp.dot(oh.astype(BF16), su, preferred_element_type=F32)
        base = jnp.sum(jnp.where(tile_lane, loff_ref[...], 0.0), axis=-1, keepdims=True) + before
        d0 = jnp.sum(jnp.where(oh0, base, 0.0), axis=0, keepdims=True)
        d1 = jnp.sum(jnp.where(oh1, base, 0.0), axis=0, keepdims=True)
        lrow_ref[k, 0:1, :] = d0.astype(I32)
        lrow_ref[k, 1:2, :] = d1.astype(I32)
        pick_tile = lambda ref: jnp.sum(jnp.where(tile_lane, ref[...], 0.0), axis=-1, keepdims=True)
        nch = pick_tile(n8_ref) * (1.0 / CHUNK)
        n_big = jnp.floor(nch * 0.5)
        n_small = nch - 2.0 * n_big
        loff_t = pick_tile(loff_ref)
        goff_t = pick_tile(goff_ref)
        for c, (n, rows, first) in enumerate(((n_big, COPY_ROWS[0], 0.0),
                                              (n_small, COPY_ROWS[1], n_big * COPY_ROWS[0]))):
            cend = jnp.dot(incl, n + jnp.zeros((N_EXPERTS, LANES), F32), preferred_element_type=F32, precision=hi)
            cstart = cend - n
            mine = (cidx >= cstart) & (cidx < cend)
            step_rows = first + (cidx - cstart) * rows
            ctab_ref[k, 2 * c:2 * c + 1, :] = jnp.sum(jnp.where(mine, loff_t + step_rows, 0.0), axis=0,
                                                      keepdims=True).astype(I32)
            ctab_ref[k, 2 * c + 1:2 * c + 2, :] = jnp.sum(jnp.where(mine, goff_t + step_rows, 0.0), axis=0,
                                                          keepdims=True).astype(I32)
        lcol_ref[tok, :] = jnp.where(out_row == 0, d0, jnp.where(out_row == 1, d1,
                                     jnp.where(out_row == 2, r_ref[2:3, tok],
                                               jnp.where(out_row == 3, r_ref[3:4, tok], 0.0)))).T


def _route(rrow, cnt, tm, nbp):
    t = rrow.shape[1]
    nt = t // tm
    tiles = next(r for r in (4, 2, 1) if nt % r == 0)
    kern = functools.partial(_route_kernel, tm=tm, nbp=nbp, tiles=tiles)
    return pl.pallas_call(
        kern,
        grid=(nt // tiles,),
        in_specs=[pl.BlockSpec((8, tiles * tm), lambda i: (0, i)),
                  pl.BlockSpec((N_EXPERTS, LANES), lambda i: (0, 0))],
        out_specs=[pl.BlockSpec((tiles, 2, tm), lambda i: (i, 0, 0)),
                   pl.BlockSpec((tiles * tm, LANES), lambda i: (i, 0)),
                   pl.BlockSpec((8, LANES), lambda i: (0, 0)),
                   pl.BlockSpec((tiles, 2 * len(COPY_ROWS), LANES), lambda i: (i, 0, 0)),
                   pl.BlockSpec((8, nbp), lambda i: (0, 0))],
        out_shape=[jax.ShapeDtypeStruct((nt, 2, tm), I32), jax.ShapeDtypeStruct((t, LANES), F32),
                   jax.ShapeDtypeStruct((8, LANES), I32), jax.ShapeDtypeStruct((nt, 2 * len(COPY_ROWS), LANES), I32),
                   jax.ShapeDtypeStruct((8, nbp), I32)],
        scratch_shapes=[pltpu.VMEM((N_EXPERTS, LANES), F32)] * 3,
        compiler_params=_cparams("arbitrary"),
        name="moe_route",
    )(rrow, cnt)


def _chunk_copies(tabs, i, local_ref, global_ref, sem, to_global, action):
    tab_ref, ctab_ref = tabs
    for k, rows in enumerate(COPY_ROWS):
        count = tab_ref[k, i]

        def copy(lo, go, rows=rows):
            lsl = local_ref.at[pl.ds(pl.multiple_of(lo, CHUNK), rows)]
            gsl = global_ref.at[pl.ds(pl.multiple_of(go, CHUNK), rows)]
            return pltpu.make_async_copy(lsl, gsl, sem) if to_global else pltpu.make_async_copy(gsl, lsl, sem)

        if action == "wait":
            def one(c, c1, copy=copy):
                copy(0, 0).wait()
                return c1
        else:
            def one(c, c1, copy=copy, k=k):
                copy(ctab_ref[i, 2 * k, c], ctab_ref[i, 2 * k + 1, c]).start()
                return c1

        lax.fori_loop(0, count, one, 0)


def _scatter_kernel(tab_ref, ctab_ref, meta_ref, lrow_ref, h_ref, xs_ref, buf_ref, zero_ref, sem, zsem, *, tm, nb):
    i = pl.program_id(0)
    tabs = (tab_ref, ctab_ref)

    @pl.when(i == 0)
    def _():
        zero_ref[...] = jnp.zeros_like(zero_ref)
        n_used = meta_ref[2, 0]

        def each_piece(action):
            def body(b, c):
                for piece in range(MOE_ROWS // ZERO_ROWS):
                    @pl.when((b < n_used) & (meta_ref[1, b] < (piece + 1) * ZERO_ROWS))
                    def _():
                        start = pl.multiple_of(b * MOE_ROWS + piece * ZERO_ROWS, ZERO_ROWS)
                        cp = pltpu.make_async_copy(zero_ref, xs_ref.at[pl.ds(start, ZERO_ROWS)], zsem)
                        getattr(cp, action)()
                return c
            lax.fori_loop(0, nb, body, 0)

        each_piece("start")
        each_piece("wait")

    rows = _row_iota((LOCAL_ROWS, tm))

    def drain(tile, slot):
        _chunk_copies(tabs, tile, buf_ref.at[slot], xs_ref, sem.at[slot], True, "wait")

    sorted_rows = []
    for slot in range(2):
        p0 = rows == lrow_ref[slot, 0:1, :]
        p1 = rows == lrow_ref[slot, 1:2, :]
        perm = jnp.where(p0 | p1, 1.0, 0.0).astype(BF16)
        sorted_rows.append(jnp.dot(perm, h_ref[slot * tm:(slot + 1) * tm, :],
                                   preferred_element_type=F32))
    for slot in range(2):
        tile = 2 * i + slot

        @pl.when(i > 0)
        def _():
            drain(tile - 2, slot)

        buf = buf_ref.at[slot]
        buf[...] = _pack_bf16_pairs(sorted_rows[slot], exact=True)
        _chunk_copies(tabs, tile, buf, xs_ref, sem.at[slot], True, "start")

    @pl.when(i == pl.num_programs(0) - 1)
    def _():
        drain(2 * i, 0)
        drain(2 * i + 1, 1)


def _scatter(tab, ctab, meta, lrow, h2, tm, nb):
    t = h2.shape[0]
    kern = functools.partial(_scatter_kernel, tm=tm, nb=nb)
    grid_spec = pltpu.PrefetchScalarGridSpec(
        num_scalar_prefetch=3,
        grid=(t // (2 * tm),),
        in_specs=[pl.BlockSpec((2, 2, tm), lambda i, *_: (i, 0, 0)),
                  pl.BlockSpec((2 * tm, D_MODEL), lambda i, *_: (i, 0))],
        out_specs=pl.BlockSpec(memory_space=pl.ANY),
        scratch_shapes=[pltpu.VMEM((2, LOCAL_ROWS, XS_WIDTH), U32), pltpu.VMEM((ZERO_ROWS, XS_WIDTH), U32),
                        pltpu.SemaphoreType.DMA((2,)), pltpu.SemaphoreType.DMA],
    )
    return pl.pallas_call(
        kern,
        grid_spec=grid_spec,
        out_shape=jax.ShapeDtypeStruct((nb * MOE_ROWS, XS_WIDTH), U32),
        compiler_params=_cparams("arbitrary"),
        name="moe_scatter",
    )(tab, ctab, meta, lrow, h2)


def _ffn_kernel(meta_ref, x_ref, wg_ref, wu_ref, wd_ref, o_ref, wgu_b, wd_b, wg_f, wu_f, wd_f, wsem, *, layer):
    b = pl.program_id(0)
    live = b < meta_ref[2, 0]
    expert = meta_ref[0, b]
    prev = meta_ref[0, jnp.maximum(b - 1, 0)]

    def fetch(e, slot):
        return [pltpu.make_async_copy(src.at[layer, e], dst.at[slot], wsem.at[slot])
                for src, dst in ((wg_ref, wg_f), (wu_ref, wu_f), (wd_ref, wd_f))]

    def first_block(slot):
        @pl.when(b == 0)
        def _():
            for cp in fetch(expert, slot):
                cp.start()

        for cp in fetch(expert, slot):
            cp.wait()
        wgu_b[:, 0:EXPERT_FF] = wg_f[slot].astype(BF16)
        wgu_b[:, EXPERT_FF:2 * EXPERT_FF] = wu_f[slot].astype(BF16)
        wd_b[...] = wd_f[slot].astype(BF16)
        nxt = meta_ref[4, b]

        @pl.when(nxt < N_EXPERTS)
        def _():
            for cp in fetch(nxt, 1 - slot):
                cp.start()

    changed = live & ((b == 0) | (expert != prev))
    odd = (meta_ref[3, b] & 1) == 1

    @pl.when(changed & jnp.logical_not(odd))
    def _():
        first_block(0)

    @pl.when(changed & odd)
    def _():
        first_block(1)

    @pl.when(live)
    def _():
        x = _unpack_bf16_pairs(x_ref[...])
        gu = jnp.dot(x, wgu_b[...], preferred_element_type=F32)
        act = (_silu(gu[:, 0:EXPERT_FF]) * gu[:, EXPERT_FF:2 * EXPERT_FF]).astype(BF16)
        y = jnp.dot(act, wd_b[...], preferred_element_type=F32)
        o_ref[...] = _pack_bf16_pairs(y)


def _ffn(meta, xs, wg, wu, wd, layer, nb):
    def blk(b, m):
        return (jnp.maximum(jnp.minimum(b, m[2, 0] - 1), 0), 0)

    grid_spec = pltpu.PrefetchScalarGridSpec(
        num_scalar_prefetch=1,
        grid=(nb,),
        in_specs=[pl.BlockSpec((MOE_ROWS, XS_WIDTH), blk)] + [pl.BlockSpec(memory_space=pl.ANY)] * 3,
        out_specs=pl.BlockSpec((MOE_ROWS, PACKED), blk),
        scratch_shapes=[pltpu.VMEM((D_MODEL, 2 * EXPERT_FF), BF16), pltpu.VMEM((EXPERT_FF, D_MODEL), BF16),
                        pltpu.VMEM((2, D_MODEL, EXPERT_FF), F32), pltpu.VMEM((2, D_MODEL, EXPERT_FF), F32),
                        pltpu.VMEM((2, EXPERT_FF, D_MODEL), F32), pltpu.SemaphoreType.DMA((2,))],
    )
    return pl.pallas_call(
        functools.partial(_ffn_kernel, layer=layer),
        grid_spec=grid_spec,
        out_shape=jax.ShapeDtypeStruct((nb * MOE_ROWS, PACKED), U32),
        compiler_params=_cparams("arbitrary"),
        name="moe_experts",
    )(meta, xs, wg, wu, wd)


def _gather_kernel(tab_ref, ctab_ref, lcol_ref, x_ref, g_ref, ys_ref, o_ref, buf_ref, sem, *, tm, final):
    i = pl.program_id(0)

    last = pl.num_programs(0) - 1

    def fetch(tile, slot, action):
        _chunk_copies((tab_ref, ctab_ref), tile, buf_ref.at[slot], ys_ref, sem.at[slot], False, action)

    @pl.when(i == 0)
    def _():
        buf_ref[...] = jnp.zeros_like(buf_ref)
        fetch(0, 0, "start")

    col = _lane_iota((tm, LOCAL_ROWS)).astype(F32)
    def picks(tok):
        lc = lcol_ref[tok, :]
        return jnp.concatenate([jnp.where(col == lc[:, 0:1], 1.0, 0.0).astype(BF16),
                                jnp.where(col == lc[:, 1:2], 1.0, 0.0).astype(BF16)], axis=0)

    toks = [slice(slot * tm, (slot + 1) * tm) for slot in range(2)]
    fetch(2 * i + 1, 1, "start")
    pick = picks(toks[0])
    for slot in range(2):
        tile = 2 * i + slot
        tok = toks[slot]
        fetch(tile, slot, "wait")
        y = _unpack_bf16_pairs(buf_ref[slot])
        both = jnp.dot(pick, y, preferred_element_type=F32)
        if slot == 0:
            pick = picks(toks[1])
        lc = lcol_ref[tok, :]
        x = x_ref[tok, :] + lc[:, 2:3] * both[0:tm] + lc[:, 3:4] * both[tm:2 * tm]
        o_ref[tok, :] = _rms(x, g_ref[...]) if final else x
        if slot == 0:
            @pl.when(i < last)
            def _():
                fetch(tile + 2, 0, "start")


def _gather(tab, ctab, lcol, x2, g, ys, tm, final):
    t = x2.shape[0]
    kern = functools.partial(_gather_kernel, tm=tm, final=final)
    grid_spec = pltpu.PrefetchScalarGridSpec(
        num_scalar_prefetch=2,
        grid=(t // (2 * tm),),
        in_specs=[pl.BlockSpec((2 * tm, LANES), lambda i, *_: (i, 0)),
                  pl.BlockSpec((2 * tm, D_MODEL), lambda i, *_: (i, 0)),
                  pl.BlockSpec((1, D_MODEL), lambda i, *_: (0, 0)),
                  pl.BlockSpec(memory_space=pl.ANY)],
        out_specs=pl.BlockSpec((2 * tm, D_MODEL), lambda i, *_: (i, 0)),
        scratch_shapes=[pltpu.VMEM((2, LOCAL_ROWS, PACKED), U32), pltpu.SemaphoreType.DMA((2,))],
    )
    return pl.pallas_call(
        kern,
        grid_spec=grid_spec,
        out_shape=jax.ShapeDtypeStruct((t, D_MODEL), F32),
        compiler_params=_cparams("arbitrary"),
        name="moe_combine",
    )(tab, ctab, lcol, x2, g, ys)


def _pad_rows(a, rows=8):
    return jnp.zeros((rows, a.shape[-1]), F32).at[:a.shape[0]].set(a.astype(F32))


def _arrange_mla(w_uq, w_ukv):
    half = MLA_ROPE // 2
    qd = MLA_NOPE + MLA_ROPE
    wq, wqs, wk, wv = [], [], [], []
    zq = jnp.zeros((MLA_Q_LORA, LANES - qd), w_uq.dtype)
    zk = jnp.zeros((MLA_KV_LORA, LANES - MLA_NOPE), w_ukv.dtype)
    for h in range(N_HEADS):
        q = w_uq[:, h * qd:(h + 1) * qd]
        nope, rope = q[:, :MLA_NOPE], q[:, MLA_NOPE:]
        wq.append(jnp.concatenate([nope, rope, zq], axis=1))
        wqs.append(jnp.concatenate([jnp.zeros_like(nope), -rope[:, half:], rope[:, :half], zq], axis=1))
        kv = w_ukv[:, h * 2 * MLA_NOPE:(h + 1) * 2 * MLA_NOPE]
        wk.append(jnp.concatenate([kv[:, :MLA_NOPE], zk], axis=1))
        wv.append(kv[:, MLA_NOPE:])
    cat = lambda xs: jnp.concatenate(xs, axis=1).astype(BF16)
    return cat(wq), cat(wqs), cat(wk), cat(wv).T


def kernel(x, positions, norm_mix, w_in, conv_a, fox_forget_bias, ssm_conv_w, ssm_conv_b, ssm_dt_bias,
           ssm_a_log, ssm_d, ssm_norm, mla_q_norm, mla_kv_norm, mla_w_uq, mla_w_ukv, w_out, norm_ffn,
           router_group_w, router_group_b, router_expert_w, router_expert_b, expert_w_gate, expert_w_up,
           expert_w_down, norm_final):
    batch, seq, d = x.shape
    t = batch * seq
    depth = w_in.shape[0]
    tm = min(ROW_TILE, seq)
    tq = min(ATTN_TQ, seq)
    tmd = MOE_TILE
    assert d == D_MODEL and seq % tm == 0 and seq % tq == 0 and tq % SSM_CHUNK == 0, (x.shape,)
    assert seq // tq <= 8 and tm % (LANES * ROPE_PACK) == 0, (seq, tm)
    assert t % (2 * tmd) == 0 and t // tmd <= LANES and tm % tmd == 0, (t, tmd)
    max_rows = 2 * t + (CHUNK - 1) * N_EXPERTS * (t // tmd) + N_EXPERTS * (MOE_ROWS - 1)
    nb = -(-max_rows // MOE_ROWS)
    nbp = -(-nb // LANES) * LANES

    xf = x.reshape(t, d)
    cos, sin = _rope_tables(positions, tm)
    w_in_t = jnp.swapaxes(w_in, 1, 2)

    for l in range(depth):
        wq, wqs, wk, wv = _arrange_mla(mla_w_uq[l], mla_w_ukv[l])
        ya, pb, pc, misc, fox_vt, q, k, v = _inproj(
            xf, norm_mix[l][None, :], w_in_t, l, _pad_rows(conv_a[l]), cos, sin, mla_q_norm[l][None, :],
            mla_kv_norm[l][None, :], jnp.concatenate([wq, wqs], axis=1), wk, wv, tm, seq)

        sp = jnp.zeros((8, LANES), F32)
        sp = sp.at[0, MISC_F:MISC_F + N_HEADS].set(fox_forget_bias[l])
        sp = sp.at[0, MISC_DT:MISC_DT + N_HEADS].set(ssm_dt_bias[l])
        sp = sp.at[1, MISC_DT:MISC_DT + N_HEADS].set(ssm_a_log[l])
        col, rows, fox_q, fox_k, tref = _scalar_prep(misc, sp, pb, batch, seq, tq)

        yb = _attention(fox_q, fox_k, fox_vt, tref, batch, seq, tq, "fox_attention")
        conv_wb = _pad_rows(jnp.concatenate([ssm_conv_w[l], ssm_conv_b[l][None, :]], axis=0))
        ssd_par = _pad_rows(jnp.stack([jnp.repeat(ssm_d[l], HEAD_DIM), ssm_norm[l]]))
        yc = _ssd_mixer(pc, col, rows, conv_wb, ssd_par, batch, seq)
        yd = _attention(q, k, v, None, batch, seq, tq, "mla_attention")

        pad = jnp.zeros((d, LANES - N_EXPERTS - N_EXPERT_GROUPS), F32)
        wr = jnp.concatenate([router_expert_w[l], router_group_w[l], pad], axis=1)
        wr_hi = wr.astype(BF16)
        wr = jnp.concatenate([wr_hi, (wr - wr_hi.astype(F32)).astype(BF16)], axis=1)
        br = jnp.concatenate([router_expert_b[l], router_group_b[l], pad[0]])[None, :]
        x2, h2, rrow, cnt = _outproj(xf, ya, yb, yc, yd, w_out[l].astype(BF16), norm_ffn[l][None, :], wr, br,
                                     tm, tmd)

        lrow, lcol, tab, ctab, meta = _route(rrow, cnt, tmd, nbp)
        xs = _scatter(tab, ctab, meta, lrow, h2, tmd, nb)
        ys = _ffn(meta, xs, expert_w_gate, expert_w_up, expert_w_down, l, nb)
        final = l == depth - 1
        xf = _gather(tab, ctab, lcol, x2, norm_final[None, :], ys, tmd, final)

    return xf.reshape(batch, seq, d)
```
